```python
import math
import jax, jax.numpy as jnp
from jax import lax
import numpy as np

D_MODEL = 1024
BATCH = 2
SEQ = 8192
DEPTH = 2

RW_HEAD = 64
RW_WIDTH = D_MODEL // 2
RW_HEADS = RW_WIDTH // RW_HEAD
RW_DECAY_LORA = 64
RW_AAA_LORA = 64
RW_GATE_LORA = 128
RW_GN_EPS = 64e-5
RW_IN = 3 * RW_WIDTH + RW_DECAY_LORA + RW_AAA_LORA + RW_GATE_LORA
GLA_HEADS = 4
GLA_DK = 64
GLA_DV = 128
GLA_GATE_LORA = 16
GLA_GATE_TAU = 16.0
GLA_IN = 2 * GLA_HEADS * GLA_DK + 2 * GLA_HEADS * GLA_DV + GLA_GATE_LORA
S5_GROUP = 16
S5_WIDTH = D_MODEL // 2
S5_GROUPS = S5_WIDTH // S5_GROUP
S5_STATE = 64
HG_DK = 128
HG_WIDTH = D_MODEL // 2
HG_HEADS = HG_WIDTH // HG_DK
HG_DV = HG_WIDTH // HG_HEADS
CD_IN = S5_WIDTH + 4 * HG_WIDTH
CHUNK = 64
NORM_EPS = 1e-5
MOE_GROUPS = 4
MOE_PER_GROUP = 8
N_EXPERTS = MOE_GROUPS * MOE_PER_GROUP
EXPERT_HIDDEN = 512
TOP_K = 2
MOE_BLOCK = 128
DN_ALPHA = (2.0 * DEPTH) ** 0.25
DN_BETA = (8.0 * DEPTH) ** -0.25
N_EVEN = (DEPTH + 1) // 2
N_ODD = DEPTH // 2

kernel_name = "hybrid_rwkv7_gla_s5_hgrn2_hmoe_deepnorm"

F32 = jnp.float32


def _split(t, sizes):
    return jnp.split(t, np.cumsum(sizes)[:-1].tolist(), axis=-1)


def _heads(t, n):
    return t.reshape(t.shape[:-1] + (n, t.shape[-1] // n))


def _layer_norm(x, g, b):
    xf = x.astype(F32)
    mu = jnp.mean(xf, -1, keepdims=True)
    var = jnp.mean(jnp.square(xf - mu), -1, keepdims=True)
    return ((xf - mu) * lax.rsqrt(var + NORM_EPS) * g + b).astype(x.dtype)


def _gated_head_rmsnorm(o, gate, g):
    of = o.astype(F32)
    of = of * lax.rsqrt(jnp.mean(of * of, -1, keepdims=True) + NORM_EPS) * g
    out = of * jax.nn.silu(gate.astype(F32))
    return out.reshape(out.shape[:-2] + (-1,))


def _chunked_gated_linear_attention(q, k, v, log_g):
    B_, L, H, K = q.shape
    V = v.shape[-1]
    n = L // CHUNK

    def to_chunks(t):
        t = t.astype(F32).reshape(B_, n, CHUNK, H, t.shape[-1])
        return jnp.transpose(t, (1, 0, 3, 2, 4))

    qc, kc, vc, gc = (to_chunks(t) for t in (q, k, v, log_g))
    causal = jnp.tril(jnp.ones((CHUNK, CHUNK), bool))[:, :, None]

    def step(S, inp):
        q_c, k_c, v_c, g_c = inp
        b = jnp.cumsum(g_c, axis=2)
        b_last = b[:, :, -1]
        o_inter = jnp.einsum('bhck,bhkv->bhcv', q_c * jnp.exp(b), S)
        diff = b[:, :, :, None, :] - b[:, :, None, :, :]
        decay = jnp.exp(jnp.where(causal, diff, -jnp.inf))
        scores = jnp.einsum('bhik,bhjk,bhijk->bhij', q_c, k_c, decay)
        o = o_inter + jnp.einsum('bhij,bhjv->bhiv', scores, v_c)
        k_dec = k_c * jnp.exp(b_last[:, :, None, :] - b)
        S = jnp.exp(b_last)[..., None] * S + jnp.einsum('bhjk,bhjv->bhkv', k_dec, v_c)
        return S, o

    S0 = jnp.zeros((B_, H, K, V), F32)
    _, o = lax.scan(step, S0, (qc, kc, vc, gc))
    return jnp.transpose(o, (1, 0, 3, 2, 4)).reshape(B_, L, H, V).astype(v.dtype)


def _rwkv7_time_mix(p, mu, w0, w2, a0, a2, g2, k_k, k_a, r_k, gn_g, gn_b):
    B_, L, _ = p.shape
    prev = jnp.pad(p, ((0, 0), (1, 0), (0, 0)))[:, :-1]
    p = p + mu * (prev - p)
    r, wl, k, v, al, gl = _split(p, [RW_WIDTH, RW_DECAY_LORA, RW_WIDTH, RW_WIDTH, RW_AAA_LORA, RW_GATE_LORA])
    w = -jax.nn.softplus(-(w0 + jnp.tanh(wl) @ w2)) - 0.5
    a = jax.nn.sigmoid(a0 + al @ a2)
    g = jax.nn.sigmoid(gl) @ g2
    kk = _heads((k * k_k).astype(F32), RW_HEADS)
    kk = kk / jnp.maximum(jnp.sqrt(jnp.sum(kk * kk, -1, keepdims=True)), 1e-12)
    k = k * (1.0 + (a - 1.0) * k_a)
    rh, kh, vh, ah = (_heads(t.astype(F32), RW_HEADS) for t in (r, k, v, a))
    decay = jnp.exp(-jnp.exp(_heads(w.astype(F32), RW_HEADS)))
    xs = tuple(jnp.moveaxis(t, 1, 0) for t in (rh, decay, kh, vh, -kk, kk * ah))

    def step(S, inp):
        r_t, w_t, k_t, v_t, a_t, b_t = inp
        sa = jnp.einsum('bhvk,bhk->bhv', S, a_t)
        S = S * w_t[:, :, None, :] + sa[..., None] * b_t[:, :, None, :] + v_t[..., None] * k_t[:, :, None, :]
        return S, jnp.einsum('bhvk,bhk->bhv', S, r_t)

    S0 = jnp.zeros((B_, RW_HEADS, RW_HEAD, RW_HEAD), F32)
    _, y = lax.scan(step, S0, xs)
    y = jnp.moveaxis(y, 0, 1)
    mean = jnp.mean(y, -1, keepdims=True)
    var = jnp.mean(jnp.square(y - mean), -1, keepdims=True)
    y = ((y - mean) * lax.rsqrt(var + RW_GN_EPS)).reshape(B_, L, RW_WIDTH) * gn_g + gn_b
    bonus = (jnp.sum(rh * kh * r_k, -1, keepdims=True) * vh).reshape(B_, L, RW_WIDTH)
    return ((y + bonus) * g).astype(p.dtype)


def _gla(q, k, v, gk_low, gate, gk_w2, gk_b, norm_g):
    log_a = jax.nn.log_sigmoid((gk_low @ gk_w2 + gk_b).astype(F32)) / GLA_GATE_TAU
    o = _chunked_gated_linear_attention(
        _heads(q, GLA_HEADS) * (GLA_DK ** -0.5), _heads(k, GLA_HEADS),
        _heads(v, GLA_HEADS), _heads(log_a, GLA_HEADS))
    return _gated_head_rmsnorm(o, _heads(gate, GLA_HEADS), norm_g).astype(q.dtype)


def _hgrn2(q, f, i, gate, lb, norm_g):
    ff = f.astype(F32)
    forget = lb + (1.0 - lb) * jax.nn.sigmoid(ff)
    inp_gate = (1.0 - lb) * jax.nn.sigmoid(-ff)
    o = _chunked_gated_linear_attention(
        _heads(jax.nn.silu(q), HG_HEADS), _heads(inp_gate, HG_HEADS),
        _heads(i, HG_HEADS), _heads(jnp.log(forget), HG_HEADS))
    return _gated_head_rmsnorm(o, _heads(gate, HG_HEADS), norm_g).astype(q.dtype)


def _s5(u, a_re, a_im, log_dt, b_re, b_im, c_re, c_im, d, glu_w, glu_b):
    B_, L, _ = u.shape
    uf = u.astype(F32)
    ug = uf.reshape(B_, L, S5_GROUPS, S5_GROUP)
    lam_re = jnp.minimum(a_re.astype(F32), -1e-4)
    lam_im = a_im.astype(F32)
    dt = jnp.exp(log_dt.astype(F32))[:, None]
    mag = jnp.exp(lam_re * dt)
    abar_re = mag * jnp.cos(lam_im * dt)
    abar_im = mag * jnp.sin(lam_im * dt)
    den = lam_re * lam_re + lam_im * lam_im
    num_re = abar_re - 1.0
    z_re = (num_re * lam_re + abar_im * lam_im) / den
    z_im = (abar_im * lam_re - num_re * lam_im) / den
    bu_re = jnp.einsum('blgc,gnc->blgn', ug, b_re.astype(F32))
    bu_im = jnp.einsum('blgc,gnc->blgn', ug, b_im.astype(F32))
    x_re = z_re * bu_re - z_im * bu_im
    x_im = z_re * bu_im + z_im * bu_re
    at_re = jnp.broadcast_to(abar_re, x_re.shape)
    at_im = jnp.broadcast_to(abar_im, x_im.shape)

    def combine(e1, e2):
        a1r, a1i, b1r, b1i = e1
        a2r, a2i, b2r, b2i = e2
        return (a2r * a1r - a2i * a1i, a2r * a1i + a2i * a1r,
                a2r * b1r - a2i * b1i + b2r, a2r * b1i + a2i * b1r + b2i)

    _, _, h_re, h_im = lax.associative_scan(combine, (at_re, at_im, x_re, x_im), axis=1)
    y = (jnp.einsum('blgn,gcn->blgc', h_re, c_re.astype(F32))
         - jnp.einsum('blgn,gcn->blgc', h_im, c_im.astype(F32)))
    y = y.reshape(B_, L, S5_WIDTH) + d * uf
    y = jax.nn.gelu(y)
    y = y * jax.nn.sigmoid(y @ glu_w.astype(F32) + glu_b)
    return y.astype(u.dtype)


def _mixer_rwkv7_gla(h, w_in, rw_mu, rw_w0, rw_w2, rw_a0, rw_a2, rw_g2, rw_k_k, rw_k_a,
                     rw_r_k, rw_gn_g, rw_gn_b, gla_gk_w2, gla_gk_b, gla_norm_g, w_out):
    p = h @ w_in
    p_rw, q, k, v, gk_low, gate = _split(
        p, [RW_IN, GLA_HEADS * GLA_DK, GLA_HEADS * GLA_DK, GLA_HEADS * GLA_DV, GLA_GATE_LORA, GLA_HEADS * GLA_DV])
    y_rw = _rwkv7_time_mix(p_rw, rw_mu, rw_w0, rw_w2, rw_a0, rw_a2, rw_g2, rw_k_k, rw_k_a, rw_r_k, rw_gn_g, rw_gn_b)
    y_gla = _gla(q, k, v, gk_low, gate, gla_gk_w2, gla_gk_b, gla_norm_g)
    return jnp.concatenate([y_rw, y_gla], -1).astype(h.dtype) @ w_out


def _mixer_s5_hgrn2(h, w_in, s5_a_re, s5_a_im, s5_log_dt, s5_b_re, s5_b_im, s5_c_re, s5_c_im,
                    s5_d, s5_glu_w, s5_glu_b, lb, hg_norm_g, w_out):
    p = h @ w_in
    u, q, f, i, gate = _split(p, [S5_WIDTH, HG_WIDTH, HG_WIDTH, HG_WIDTH, HG_WIDTH])
    y_s5 = _s5(u, s5_a_re, s5_a_im, s5_log_dt, s5_b_re, s5_b_im, s5_c_re, s5_c_im, s5_d, s5_glu_w, s5_glu_b)
    y_hg = _hgrn2(q, f, i, gate, lb, hg_norm_g)
    return jnp.concatenate([y_s5, y_hg], -1).astype(h.dtype) @ w_out


def _expert_dispatch(xt, expert, gate, w1, w3, w2):
    T, D = xt.shape
    A = T * TOP_K
    e_flat = expert.reshape(A)
    tok_flat = jnp.repeat(jnp.arange(T, dtype=jnp.int32), TOP_K)
    g_flat = gate.reshape(A)
    order = jnp.argsort(e_flat)
    e_s, tok_s, g_s = e_flat[order], tok_flat[order], g_flat[order]
    counts = jnp.bincount(e_flat, length=N_EXPERTS)
    padded = (counts + MOE_BLOCK - 1) // MOE_BLOCK * MOE_BLOCK
    start = jnp.cumsum(counts) - counts
    pend = jnp.cumsum(padded)
    pstart = pend - padded
    dest = pstart[e_s] + (jnp.arange(A, dtype=jnp.int32) - start[e_s])
    n_blocks = -(-A // MOE_BLOCK) + N_EXPERTS
    R = n_blocks * MOE_BLOCK
    row_tok = jnp.full((R,), T, jnp.int32).at[dest].set(tok_s)
    x_pad = jnp.concatenate([xt, jnp.zeros((1, D), xt.dtype)], 0)
    x_rows = x_pad[row_tok].reshape(n_blocks, MOE_BLOCK, D)
    block_e = jnp.minimum(jnp.searchsorted(pend, jnp.arange(n_blocks) * MOE_BLOCK, side='right'), N_EXPERTS - 1)

    def run_block(args):
        xb, e = args
        hid = jax.nn.silu(xb @ w1[e]) * (xb @ w3[e])
        return hid @ w2[e]

    y_rows = lax.map(run_block, (x_rows, block_e)).reshape(R, D)
    y = jnp.zeros((T, D), F32).at[tok_s].add(g_s[:, None] * y_rows[dest].astype(F32))
    return y.astype(xt.dtype)


def _hierarchical_moe(x, wg, bg, we, be, w1, w3, w2):
    B_, L, D = x.shape
    T = B_ * L
    xt = x.reshape(T, D)
    xf = xt.astype(F32)
    coarse = jax.nn.softmax(xf @ wg.astype(F32) + bg, axis=-1)
    p_grp, grp = lax.top_k(coarse, 1)
    grp = grp[:, 0]
    fine_all = jnp.einsum('td,gde->tge', xf, we.astype(F32)) + be
    fine = jax.nn.softmax(fine_all[jnp.arange(T), grp], axis=-1)
    top_p, top_j = lax.top_k(fine, TOP_K)
    gate = p_grp * (top_p / jnp.sum(top_p, -1, keepdims=True))
    expert = grp[:, None] * MOE_PER_GROUP + top_j
    return _expert_dispatch(xt, expert, gate, w1, w3, w2).reshape(B_, L, D)


def setup_inputs(seed: int = 0) -> dict:
    key = jax.random.key(seed)
    keys = jax.random.split(key, 48)
    ks = iter([keys[n] for n in range(48)])

    def nrm(shape, scale):
        return jax.random.normal(next(ks), shape, F32) * scale

    E, O = N_EVEN, N_ODD
    ratio = jnp.arange(RW_WIDTH, dtype=F32) / (RW_WIDTH - 1)
    w0_base = -6.5 + 5.0 * ratio ** 0.85
    n_idx = jnp.arange(S5_STATE, dtype=F32)
    return {
        "x": nrm((BATCH, SEQ, D_MODEL), 1.0),
        "ab_w_in": nrm((E, D_MODEL, RW_IN + GLA_IN), D_MODEL ** -0.5),
        "rw_mu": jax.random.uniform(next(ks), (E, RW_IN), F32),
        "rw_w0": w0_base + nrm((E, RW_WIDTH), 0.1),
        "rw_w2": nrm((E, RW_DECAY_LORA, RW_WIDTH), 0.5 * RW_DECAY_LORA ** -0.5),
        "rw_a0": nrm((E, RW_WIDTH), 0.1),
        "rw_a2": nrm((E, RW_AAA_LORA, RW_WIDTH), RW_AAA_LORA ** -0.5),
        "rw_g2": nrm((E, RW_GATE_LORA, RW_WIDTH), RW_GATE_LORA ** -0.5),
        "rw_k_k": 0.85 + nrm((E, RW_WIDTH), 0.02),
        "rw_k_a": 1.0 + nrm((E, RW_WIDTH), 0.02),
        "rw_r_k": -0.04 + nrm((E, RW_HEADS, RW_HEAD), 0.1),
        "rw_gn_g": 1.0 + nrm((E, RW_WIDTH), 0.02),
        "rw_gn_b": nrm((E, RW_WIDTH), 0.02),
        "gla_gk_w2": nrm((E, GLA_GATE_LORA, GLA_HEADS * GLA_DK), GLA_GATE_LORA ** -0.5),
        "gla_gk_b": nrm((E, GLA_HEADS * GLA_DK), 0.02),
        "gla_norm_g": 1.0 + nrm((E, GLA_DV), 0.02),
        "ab_w_out": nrm((E, D_MODEL, D_MODEL), DN_BETA * D_MODEL ** -0.5),
        "cd_w_in": nrm((O, D_MODEL, CD_IN), D_MODEL ** -0.5),
        "s5_a_re": -0.5 + nrm((O, S5_GROUPS, S5_STATE), 0.01),
        "s5_a_im": math.pi * n_idx + nrm((O, S5_GROUPS, S5_STATE), 0.01),
        "s5_log_dt": jax.random.uniform(next(ks), (O, S5_GROUPS), F32, math.log(1e-3), math.log(1e-1)),
        "s5_b_re": nrm((O, S5_GROUPS, S5_STATE, S5_GROUP), (2 * S5_GROUP) ** -0.5),
        "s5_b_im": nrm((O, S5_GROUPS, S5_STATE, S5_GROUP), (2 * S5_GROUP) ** -0.5),
        "s5_c_re": nrm((O, S5_GROUPS, S5_GROUP, S5_STATE), S5_STATE ** -0.5),
        "s5_c_im": nrm((O, S5_GROUPS, S5_GROUP, S5_STATE), S5_STATE ** -0.5),
        "s5_d": nrm((O, S5_WIDTH), 1.0),
        "s5_glu_w": nrm((O, S5_WIDTH, S5_WIDTH), S5_WIDTH ** -0.5),
        "s5_glu_b": nrm((O, S5_WIDTH), 0.02),
        "hg_lb": 1.0 + nrm((DEPTH, HG_WIDTH), 0.1),
        "hg_norm_g": 1.0 + nrm((O, HG_DV), 0.02),
        "cd_w_out": nrm((O, D_MODEL, D_MODEL), DN_BETA * D_MODEL ** -0.5),
        "ln1_g": 1.0 + nrm((DEPTH, D_MODEL), 0.02),
        "ln1_b": nrm((DEPTH, D_MODEL), 0.02),
        "moe_wg": nrm((DEPTH, D_MODEL, MOE_GROUPS), D_MODEL ** -0.5),
        "moe_bg": nrm((DEPTH, MOE_GROUPS), 0.01),
        "moe_we": nrm((DEPTH, MOE_GROUPS, D_MODEL, MOE_PER_GROUP), D_MODEL ** -0.5),
        "moe_be": nrm((DEPTH, MOE_GROUPS, MOE_PER_GROUP), 0.01),
        "moe_w1": nrm((DEPTH, N_EXPERTS, D_MODEL, EXPERT_HIDDEN), D_MODEL ** -0.5),
        "moe_w3": nrm((DEPTH, N_EXPERTS, D_MODEL, EXPERT_HIDDEN), D_MODEL ** -0.5),
        "moe_w2": nrm((DEPTH, N_EXPERTS, EXPERT_HIDDEN, D_MODEL), DN_BETA * EXPERT_HIDDEN ** -0.5),
        "ln2_g": 1.0 + nrm((DEPTH, D_MODEL), 0.02),
        "ln2_b": nrm((DEPTH, D_MODEL), 0.02),
    }


def reference(x, ab_w_in, rw_mu, rw_w0, rw_w2, rw_a0, rw_a2, rw_g2, rw_k_k, rw_k_a, rw_r_k,
              rw_gn_g, rw_gn_b, gla_gk_w2, gla_gk_b, gla_norm_g, ab_w_out,
              cd_w_in, s5_a_re, s5_a_im, s5_log_dt, s5_b_re, s5_b_im, s5_c_re, s5_c_im,
              s5_d, s5_glu_w, s5_glu_b, hg_lb, hg_norm_g, cd_w_out,
              ln1_g, ln1_b, moe_wg, moe_bg, moe_we, moe_be, moe_w1, moe_w3, moe_w2, ln2_g, ln2_b):
    h = x
    lb_sm = jax.nn.softmax(hg_lb.astype(F32), axis=0)
    lower_bounds = jnp.cumsum(lb_sm, axis=0) - lb_sm[0]
    for layer in range(DEPTH):
        j = layer // 2
        if layer % 2 == 0:
            mix = _mixer_rwkv7_gla(h, ab_w_in[j], rw_mu[j], rw_w0[j], rw_w2[j], rw_a0[j], rw_a2[j], rw_g2[j],
                                   rw_k_k[j], rw_k_a[j], rw_r_k[j], rw_gn_g[j], rw_gn_b[j],
                                   gla_gk_w2[j], gla_gk_b[j], gla_norm_g[j], ab_w_out[j])
        else:
            mix = _mixer_s5_hgrn2(h, cd_w_in[j], s5_a_re[j], s5_a_im[j], s5_log_dt[j], s5_b_re[j], s5_b_im[j],
                                  s5_c_re[j], s5_c_im[j], s5_d[j], s5_glu_w[j], s5_glu_b[j],
                                  lower_bounds[layer], hg_norm_g[j], cd_w_out[j])
        h = _layer_norm(DN_ALPHA * h + mix, ln1_g[layer], ln1_b[layer])
        ffn = _hierarchical_moe(h, moe_wg[layer], moe_bg[layer], moe_we[layer], moe_be[layer],
                                moe_w1[layer], moe_w3[layer], moe_w2[layer])
        h = _layer_norm(DN_ALPHA * h + ffn, ln2_g[layer], ln2_b[layer])
    return h
```

```python
import functools
import math

import numpy as np
import jax
import jax.numpy as jnp
from jax import lax
from jax.experimental import pallas as pl
from jax.experimental.pallas import tpu as pltpu

F32 = jnp.float32
BF16 = jnp.bfloat16

D_MODEL = 1024
DEPTH = 2
RW_HEAD = 64
RW_WIDTH = 512
RW_GN_EPS = 64e-5
GLA_HEADS = 4
GLA_DK = 64
GLA_DV = 128
GLA_GATE_TAU = 16.0
S5_GROUP = 16
S5_GROUPS = 32
S5_STATE = 64
HG_HEADS = 4
HG_DK = 128
CHUNK = 64
NORM_EPS = 1e-5
MOE_GROUPS = 4
MOE_PER_GROUP = 8
N_EXPERTS = 32
EXPERT_HIDDEN = 512
MOE_BLOCK = 128
DN_ALPHA = (2.0 * DEPTH) ** 0.25

LANES = 128
VMEM_LIMIT = 56 * 1024 * 1024


def _cparams(n_axes=1):
    return pltpu.CompilerParams(dimension_semantics=("arbitrary",) * n_axes,
                                vmem_limit_bytes=VMEM_LIMIT)


def _d(a, b):
    return jnp.dot(a, b, preferred_element_type=F32)


def _d_nt(a, b):
    return lax.dot_general(a, b, (((1,), (1,)), ((), ())), preferred_element_type=F32)


def _d_tn(a, b):
    return lax.dot_general(a, b, (((0,), (0,)), ((), ())), preferred_element_type=F32)


def _split(a):
    hi = a.astype(BF16)
    lo = (a - hi.astype(F32)).astype(BF16)
    return hi, lo


def _split3(a):
    hi = a.astype(BF16)
    r1 = a - hi.astype(F32)
    mid = r1.astype(BF16)
    lo = (r1 - mid.astype(F32)).astype(BF16)
    return hi, mid, lo


def _bdot(a, b):
    return _d(a.astype(BF16), b.astype(BF16))


def _bdot_nt(a, b):
    return _d_nt(a.astype(BF16), b.astype(BF16))


def _bdot_tn(a, b):
    return _d_tn(a.astype(BF16), b.astype(BF16))


def _hdot_with(d, a, b):
    ah, al = _split(a)
    bh, bl = _split(b)
    return d(ah, bh) + (d(ah, bl) + d(al, bh))


def _hdot(a, b):
    return _hdot_with(_d, a, b)


def _hdot_nt(a, b):
    return _hdot_with(_d_nt, a, b)


def _hdot_tn(a, b):
    return _hdot_with(_d_tn, a, b)


def _xdot_l(a, e):
    ah, am, al = _split3(a)
    return _d(ah, e) + (_d(am, e) + _d(al, e))


def _xdot_r(e, a):
    ah, am, al = _split3(a)
    return _d(e, ah) + (_d(e, am) + _d(e, al))


def _iota(shape, dim):
    return lax.broadcasted_iota(jnp.int32, shape, dim)


def _softplus(x):
    return jnp.maximum(x, 0.0) + jnp.log1p(jnp.exp(-jnp.abs(x)))


def _sigmoid(x):
    return 1.0 / (1.0 + jnp.exp(-x))


def _silu(x):
    return x * _sigmoid(x)


def _tril_incl(n):
    return jnp.where(_iota((n, n), 0) >= _iota((n, n), 1), 1.0, 0.0).astype(BF16)


def _cumsum_rows(g):
    return _xdot_r(_tril_incl(g.shape[0]), g)


def _shift_mix(x_ref, prev_ref, mu):
    x = x_ref[...]
    c = x.shape[0]
    rolled = pltpu.roll(x, 1, 0)
    prev = jnp.where(_iota(x.shape, 0) == 0, jnp.broadcast_to(prev_ref[0:1, :], x.shape), rolled)
    prev_ref[0:1, :] = x[c - 1:c, :]
    return x + mu * (prev - x)


def _layer_norm(x, g, b):
    mu = jnp.mean(x, axis=-1, keepdims=True)
    xc = x - mu
    var = jnp.mean(xc * xc, axis=-1, keepdims=True)
    return xc * lax.rsqrt(var + NORM_EPS) * g + b


def _mm_body(x_ref, w_ref, o_ref):
    o_ref[...] = _d(x_ref[...].astype(BF16), w_ref[...])


def _matmul(x, w_bf16, tm):
    m, k = x.shape
    n = w_bf16.shape[1]
    return pl.pallas_call(
        _mm_body,
        grid=(m // tm,),
        in_specs=[pl.BlockSpec((tm, k), lambda i: (i, 0)),
                  pl.BlockSpec((k, n), lambda i: (0, 0))],
        out_specs=pl.BlockSpec((tm, n), lambda i: (i, 0)),
        out_shape=jax.ShapeDtypeStruct((m, n), F32),
        compiler_params=_cparams(1),
        name="in_proj",
    )(x, w_bf16)


_RV_MU_R, _RV_MU_K, _RV_MU_V, _RV_W0, _RV_A0, _RV_KK, _RV_KA, _RV_RK, _RV_GNG, _RV_GNB = range(10)


def _rwkv_body(r_ref, k_ref, v_ref, lo_ref, vec_ref, mulo_ref, w2_ref, a2_ref, g2_ref, ones_ref,
               o_ref, pr_ref, pk_ref, pv_ref, plo_ref, st_ref):
    c = r_ref.shape[0]
    npair = RW_WIDTH // LANES

    @pl.when(pl.program_id(1) == 0)
    def _():
        pr_ref[...] = jnp.zeros_like(pr_ref)
        pk_ref[...] = jnp.zeros_like(pk_ref)
        pv_ref[...] = jnp.zeros_like(pv_ref)
        plo_ref[...] = jnp.zeros_like(plo_ref)
        st_ref[...] = jnp.zeros_like(st_ref)

    def vec(i):
        return vec_ref[i:i + 1, :]

    xr = _shift_mix(r_ref, pr_ref, vec(_RV_MU_R))
    xk = _shift_mix(k_ref, pk_ref, vec(_RV_MU_K))
    xv = _shift_mix(v_ref, pv_ref, vec(_RV_MU_V))
    xlo = _shift_mix(lo_ref, plo_ref, mulo_ref[0:1, :])
    lo_a = xlo[:, :LANES]
    lo_g = xlo[:, LANES:]

    w = -_softplus(-(vec(_RV_W0) + _hdot(jnp.tanh(lo_a), w2_ref[...]))) - 0.5
    g = -jnp.exp(w)
    a = _sigmoid(vec(_RV_A0) + _hdot(lo_a, a2_ref[...]))
    gate = _hdot(_sigmoid(lo_g), g2_ref[...])
    ones_bd = ones_ref[...]
    kk = xk * vec(_RV_KK)
    kk = kk / jnp.maximum(jnp.sqrt(_xdot_l(kk * kk, ones_bd)), 1e-12)
    k2 = xk * (1.0 + (a - 1.0) * vec(_RV_KA))

    gc = _cumsum_rows(g)
    g_last = gc[c - 1:c, :]
    e_neg = jnp.exp(-gc)
    at = -kk * jnp.exp(gc - g)
    bt = (kk * a) * e_neg
    kt = k2 * e_neg
    rt = xr * jnp.exp(gc)
    e_end = jnp.exp(g_last - gc)
    bh = (kk * a) * e_end
    kh = k2 * e_end
    gam = jnp.exp(g_last)

    lane = _iota((c, LANES), 1)
    m1 = lane < RW_HEAD
    row2 = _iota((2 * c, 4 * c), 0)
    col2 = _iota((2 * c, 4 * c), 1) & (c - 1)
    tri = ((row2 < c) & (row2 > col2)) | ((row2 >= c) & ((row2 - c) >= col2))
    eye2 = _iota((2 * c, 2 * c), 0) == _iota((2 * c, 2 * c), 1)
    eye_p = _iota((LANES, LANES), 0) == _iota((LANES, LANES), 1)
    bd_p = (_iota((LANES, LANES), 0) >> 6) == (_iota((LANES, LANES), 1) >> 6)

    def halves(x):
        return jnp.concatenate([jnp.where(m1, x, 0.0), jnp.where(m1, 0.0, x)], axis=0)

    ys = []
    for p in range(npair):
        sl = slice(p * LANES, (p + 1) * LANES)
        at_p, bt_p, kt_p, rt_p, v_p = at[:, sl], bt[:, sl], kt[:, sl], rt[:, sl], xv[:, sl]
        lhs = jnp.concatenate([at_p, rt_p], axis=0)
        rhs = jnp.concatenate([halves(bt_p), halves(kt_p)], axis=0)
        aa = jnp.where(tri, _hdot_nt(lhs, rhs), 0.0)
        a_ab = aa[:c, :2 * c]
        a_ak = aa[:c, 2 * c:]
        a_row = aa[c:, :]
        abd = halves(a_ab)
        ak = _hdot(abd, abd)
        pinv = jnp.where(eye2, 1.0, 0.0) + abd
        nlev = int(math.log2(c))
        for lev in range(1, nlev):
            if lev < nlev - 1:
                out = _hdot(ak, jnp.concatenate([ak, pinv], axis=1))
                ak = out[:, :2 * c]
                pinv = pinv + out[:, 2 * c:]
            else:
                pinv = pinv + _hdot(ak, pinv)
        hp = st_ref[p]
        vv = halves(v_p)
        x = _hdot(jnp.concatenate([at_p, a_ak], axis=1), jnp.concatenate([hp, vv], axis=0))
        u2 = _hdot(pinv, halves(x))
        u = u2[:c] + u2[c:]
        y = _hdot(jnp.concatenate([rt_p, a_row], axis=1),
                  jnp.concatenate([hp, halves(u), vv], axis=0))
        ys.append(y)
        dg = jnp.where(eye_p, jnp.broadcast_to(gam[:, sl], (LANES, LANES)), 0.0)
        upd = _hdot_tn(jnp.concatenate([bh[:, sl], kh[:, sl]], axis=0),
                       jnp.concatenate([u, v_p], axis=0))
        st_ref[p] = _hdot(dg, hp) + jnp.where(bd_p, upd, 0.0)

    y = jnp.concatenate(ys, axis=1)
    inv_n = 1.0 / RW_HEAD
    mean = _xdot_l(y, ones_bd) * inv_n
    yc = y - mean
    var = _xdot_l(yc * yc, ones_bd) * inv_n
    yn = yc * lax.rsqrt(var + RW_GN_EPS) * vec(_RV_GNG) + vec(_RV_GNB)
    bonus = _xdot_l(xr * k2 * vec(_RV_RK), ones_bd) * xv
    o_ref[...] = (yn + bonus) * gate


def _rwkv(p0, vec, mulo, w2p, a2p, g2, batch, seq):
    nc = seq // CHUNK
    c = CHUNK
    ones_bd = jnp.asarray(np.kron(np.eye(RW_WIDTH // RW_HEAD), np.ones((RW_HEAD, RW_HEAD))), BF16)

    def col(j, width):
        return pl.BlockSpec((c, width), lambda b, i: (b * nc + i, j))

    def full(shape):
        return pl.BlockSpec(shape, lambda b, i: (0,) * len(shape))

    return pl.pallas_call(
        _rwkv_body,
        grid=(batch, nc),
        in_specs=[col(0, 512), col(1, 512), col(2, 512), col(10, 256),
                  full((16, 512)), full((8, 256)), full((128, 512)), full((128, 512)),
                  full((128, 512)), full((512, 512))],
        out_specs=pl.BlockSpec((c, 512), lambda b, i: (b * nc + i, 0)),
        out_shape=jax.ShapeDtypeStruct((batch * seq, RW_WIDTH), F32),
        scratch_shapes=[pltpu.VMEM((8, 512), F32), pltpu.VMEM((8, 512), F32), pltpu.VMEM((8, 512), F32),
                        pltpu.VMEM((8, 256), F32), pltpu.VMEM((RW_WIDTH // LANES, LANES, LANES), F32)],
        compiler_params=_cparams(2),
        name="rwkv7",
    )(p0, p0, p0, p0, vec, mulo, w2p, a2p, g2, ones_bd)


def _gla_core(q, k, v, g, st_ref, heads_per_block):
    c = q.shape[0]
    hpb = heads_per_block
    nblk = q.shape[1] // LANES
    dk = LANES // hpb
    b = _cumsum_rows(g)
    row = _iota(b.shape, 0)

    def brow(i):
        return jnp.broadcast_to(b[i:i + 1, :], b.shape)

    b15, b31, b47, blast = brow(15), brow(31), brow(47), brow(c - 1)
    ref_b = jnp.where(row < 32, b15, b47)
    ref_d = jnp.where(row < 16, 0.0, jnp.where(row < 32, b15, jnp.where(row < 48, b31, b47)))
    q_a = q * jnp.exp(jnp.minimum(b - b31, 0.0))
    k_a = k * jnp.exp(jnp.minimum(b31 - b, 0.0))
    q_b = q * jnp.exp(jnp.minimum(b - ref_b, 0.0))
    k_b = k * jnp.exp(jnp.minimum(ref_b - b, 0.0))
    q_d = q * jnp.exp(b - ref_d)
    k_d = k * jnp.exp(ref_d - b)
    q_i = q * jnp.exp(b)
    k_s = k * jnp.exp(blast - b)
    gam = jnp.exp(b[c - 1:c, :])

    ri = _iota((hpb * c, c), 0) & (c - 1)
    ci = _iota((hpb * c, c), 1)
    mask_a = (ri >= 32) & (ci < 32)
    mask_b = ((ri >> 5) == (ci >> 5)) & (((ri >> 4) & 1) == 1) & (((ci >> 4) & 1) == 0)
    mask_d = ((ri >> 4) == (ci >> 4)) & (ri >= ci)
    lane = _iota((c, LANES), 1)
    eye_p = _iota((LANES, LANES), 0) == _iota((LANES, LANES), 1)
    dk_shift = int(math.log2(dk))
    bd = (_iota((LANES, hpb * LANES), 0) >> dk_shift) == (_iota((LANES, hpb * LANES), 1) >> 7)

    def heads_rows(x):
        if hpb == 1:
            return x
        return jnp.concatenate([jnp.where((lane >> dk_shift) == h, x, 0.0) for h in range(hpb)], axis=0)

    outs = []
    for blk in range(nblk):
        sl = slice(blk * LANES, (blk + 1) * LANES)
        vsl = slice(blk * hpb * LANES, (blk + 1) * hpb * LANES)
        v_p = v[:, vsl]
        s_a = _bdot_nt(heads_rows(q_a[:, sl]), k_a[:, sl])
        s_b = _bdot_nt(heads_rows(q_b[:, sl]), k_b[:, sl])
        s_d = _bdot_nt(heads_rows(q_d[:, sl]), k_d[:, sl])
        p = jnp.where(mask_a, s_a, 0.0) + jnp.where(mask_b, s_b, 0.0) + jnp.where(mask_d, s_d, 0.0)
        pv = _bdot(p, v_p)
        o = pv[:c]
        for h in range(1, hpb):
            o = jnp.where((_iota(o.shape, 1) >> 7) == h, pv[h * c:(h + 1) * c], o)
        sp = st_ref[blk]
        o = o + _bdot(q_i[:, sl], sp)
        outs.append(o)
        dg = jnp.where(eye_p, jnp.broadcast_to(gam[:, sl], (LANES, LANES)), 0.0)
        upd = _bdot_tn(k_s[:, sl], v_p)
        st_ref[blk] = _hdot(dg, sp) + jnp.where(bd, upd, 0.0)
    return jnp.concatenate(outs, axis=1)


def _gated_rmsnorm(o, gate, norm_g):
    nh = o.shape[1] // LANES
    outs = []
    for h in range(nh):
        sl = slice(h * LANES, (h + 1) * LANES)
        oh = o[:, sl]
        ms = jnp.mean(oh * oh, axis=-1, keepdims=True)
        outs.append(oh * lax.rsqrt(ms + NORM_EPS) * norm_g * _silu(gate[:, sl]))
    return jnp.concatenate(outs, axis=1)


def _gla_body(q_ref, k_ref, v_ref, gate_ref, gk_ref, w2_ref, vec_ref, ng_ref, o_ref, st_ref):
    @pl.when(pl.program_id(1) == 0)
    def _():
        st_ref[...] = jnp.zeros_like(st_ref)

    z = _hdot(gk_ref[...], w2_ref[...]) + vec_ref[0:1, :]
    g = -_softplus(-z) * (1.0 / GLA_GATE_TAU)
    q = q_ref[...] * (GLA_DK ** -0.5)
    o = _gla_core(q, k_ref[...], v_ref[...], g, st_ref, 2)
    o_ref[...] = _gated_rmsnorm(o, gate_ref[...], ng_ref[0:1, :])


def _gla(p0, gk_w2p, gk_b, norm_g, batch, seq):
    nc = seq // CHUNK
    c = CHUNK

    def col(j, width):
        return pl.BlockSpec((c, width), lambda b, i: (b * nc + i, j))

    def full(shape):
        return pl.BlockSpec(shape, lambda b, i: (0,) * len(shape))

    return pl.pallas_call(
        _gla_body,
        grid=(batch, nc),
        in_specs=[col(11, 256), col(12, 256), col(3, 512), col(4, 512), col(26, 128),
                  full((128, 256)), full((8, 256)), full((8, 128))],
        out_specs=pl.BlockSpec((c, 512), lambda b, i: (b * nc + i, 0)),
        out_shape=jax.ShapeDtypeStruct((batch * seq, 512), F32),
        scratch_shapes=[pltpu.VMEM((2, LANES, 2 * LANES), F32)],
        compiler_params=_cparams(2),
        name="gla",
    )(p0, p0, p0, p0, p0, gk_w2p, gk_b, norm_g)


def _hgrn_body(q_ref, f_ref, i_ref, gate_ref, lb_ref, ng_ref, o_ref, st_ref):
    @pl.when(pl.program_id(1) == 0)
    def _():
        st_ref[...] = jnp.zeros_like(st_ref)

    lb = lb_ref[0:1, :]
    f = f_ref[...]
    forget = lb + (1.0 - lb) * _sigmoid(f)
    k = (1.0 - lb) * _sigmoid(-f)
    o = _gla_core(_silu(q_ref[...]), k, i_ref[...], jnp.log(forget), st_ref, 1)
    o_ref[...] = _gated_rmsnorm(o, gate_ref[...], ng_ref[0:1, :])


def _hgrn(p1, lb, norm_g, batch, seq):
    nc = seq // CHUNK
    c = CHUNK

    def col(j):
        return pl.BlockSpec((c, 512), lambda b, i: (b * nc + i, j))

    def full(shape):
        return pl.BlockSpec(shape, lambda b, i: (0,) * len(shape))

    return pl.pallas_call(
        _hgrn_body,
        grid=(batch, nc),
        in_specs=[col(1), col(2), col(3), col(4), full((8, 512)), full((8, 128))],
        out_specs=pl.BlockSpec((c, 512), lambda b, i: (b * nc + i, 0)),
        out_shape=jax.ShapeDtypeStruct((batch * seq, 512), F32),
        scratch_shapes=[pltpu.VMEM((HG_HEADS, LANES, LANES), F32)],
        compiler_params=_cparams(2),
        name="hgrn2",
    )(p1, p1, p1, p1, lb, norm_g)


def _s5_body(u_ref, toep_ref, bm_ref, e_ref, pw_ref, o_ref, *, rows_per_seq):
    u = u_ref[0]
    rows = u.shape[0]
    s = _d(u, bm_ref[0])
    rin = _iota((rows, LANES), 0) & (rows_per_seq - 1)
    pw = pw_ref[0]
    h = s
    nlev = int(math.log2(rows_per_seq))
    for lev in range(nlev):
        sh = 1 << lev
        hs = jnp.where(rin >= sh, pltpu.roll(h, sh, 0), 0.0)
        hs_sw = pltpu.roll(hs, S5_STATE, 1)
        h = h + pw[2 * lev:2 * lev + 1, :] * hs + pw[2 * lev + 1:2 * lev + 2, :] * hs_sw
    hprev = jnp.where(rin >= 1, pltpu.roll(h, 1, 0), 0.0)
    o_ref[0] = _d(u, toep_ref[0]) + _hdot(hprev, e_ref[0])


def _s5_scan(uc, toep, bmat, emat, pw, rows_per_seq):
    ng, rows, width = uc.shape
    return pl.pallas_call(
        functools.partial(_s5_body, rows_per_seq=rows_per_seq),
        grid=(ng,),
        in_specs=[pl.BlockSpec((1, rows, width), lambda g: (g, 0, 0)),
                  pl.BlockSpec((1, width, width), lambda g: (g, 0, 0)),
                  pl.BlockSpec((1, width, LANES), lambda g: (g, 0, 0)),
                  pl.BlockSpec((1, LANES, width), lambda g: (g, 0, 0)),
                  pl.BlockSpec((1, 16, LANES), lambda g: (g, 0, 0))],
        out_specs=pl.BlockSpec((1, rows, width), lambda g: (g, 0, 0)),
        out_shape=jax.ShapeDtypeStruct((ng, rows, width), F32),
        compiler_params=_cparams(1),
        name="s5_scan",
    )(uc, toep, bmat, emat, pw)


def _s5_post_body(y_ref, u_ref, vec_ref, w_ref, o_ref):
    y = y_ref[...] + vec_ref[0:1, :] * u_ref[...]
    y = 0.5 * y * (1.0 + jnp.tanh(math.sqrt(2.0 / math.pi) * (y + 0.044715 * (y * y * y))))
    o_ref[...] = y * _sigmoid(_bdot(y, w_ref[...]) + vec_ref[1:2, :])


def _s5_post(y_ssm, p1, vec, glu_w, tm):
    t = y_ssm.shape[0]
    return pl.pallas_call(
        _s5_post_body,
        grid=(t // tm,),
        in_specs=[pl.BlockSpec((tm, 512), lambda i: (i, 0)),
                  pl.BlockSpec((tm, 512), lambda i: (i, 0)),
                  pl.BlockSpec((8, 512), lambda i: (0, 0)),
                  pl.BlockSpec((512, 512), lambda i: (0, 0))],
        out_specs=pl.BlockSpec((tm, 512), lambda i: (i, 0)),
        out_shape=jax.ShapeDtypeStruct((t, 512), F32),
        compiler_params=_cparams(1),
        name="s5_post",
    )(y_ssm, p1, vec, glu_w)


def _s5_tables(a_re, a_im, log_dt, b_re, b_im, c_re, c_im, rows_per_seq):
    c = CHUNK
    lam_re = jnp.minimum(a_re, -1e-4)
    lam_im = a_im
    dt = jnp.exp(log_dt)[:, None]
    mag = jnp.exp(lam_re * dt)
    abar_re = mag * jnp.cos(lam_im * dt)
    abar_im = mag * jnp.sin(lam_im * dt)
    den = lam_re * lam_re + lam_im * lam_im
    num_re = abar_re - 1.0
    z_re = (num_re * lam_re + abar_im * lam_im) / den
    z_im = (abar_im * lam_re - num_re * lam_im) / den

    def power(n):
        n = jnp.asarray(n, F32)[..., None, None]
        m = jnp.exp(n * (lam_re * dt))
        return m * jnp.cos(n * (lam_im * dt)), m * jnp.sin(n * (lam_im * dt))

    def cmul(ar, ai, br, bi):
        return ar * br - ai * bi, ar * bi + ai * br

    tau = jnp.arange(c)
    p_re, p_im = power(tau)
    zb_re, zb_im = cmul(z_re[..., None], z_im[..., None], b_re, b_im)
    cp_re, cp_im = cmul(c_re[None], c_im[None], p_re[:, :, None, :], p_im[:, :, None, :])
    hi = lax.Precision.HIGHEST
    kern = (jnp.einsum('tgon,gni->gtoi', cp_re, zb_re, precision=hi)
            - jnp.einsum('tgon,gni->gtoi', cp_im, zb_im, precision=hi))
    tt = jnp.arange(c)[None, :] - jnp.arange(c)[:, None]
    kt = jnp.where((tt >= 0)[None, :, :, None, None], kern[:, jnp.clip(tt, 0, c - 1)], 0.0)
    toep = jnp.transpose(kt, (0, 1, 4, 2, 3)).reshape(S5_GROUPS, c * S5_GROUP, c * S5_GROUP)
    q_re, q_im = power(c - 1 - tau)
    bm_re, bm_im = cmul(q_re[..., None], q_im[..., None], zb_re[None], zb_im[None])
    bmat = jnp.concatenate([jnp.transpose(bm_re, (1, 0, 3, 2)), jnp.transpose(bm_im, (1, 0, 3, 2))],
                           axis=-1).reshape(S5_GROUPS, c * S5_GROUP, 2 * S5_STATE)
    r_re, r_im = power(tau + 1)
    e_re, e_im = cmul(c_re[None], c_im[None], r_re[:, :, None, :], r_im[:, :, None, :])
    emat = jnp.concatenate([jnp.transpose(e_re, (1, 3, 0, 2)), -jnp.transpose(e_im, (1, 3, 0, 2))],
                           axis=1).reshape(S5_GROUPS, 2 * S5_STATE, c * S5_GROUP)
    nlev = int(math.log2(rows_per_seq))
    assert 2 * nlev <= 16
    s_re, s_im = power(c * (2 ** jnp.arange(nlev)))
    pw = jnp.zeros((S5_GROUPS, 16, 2 * S5_STATE), F32)
    pw = pw.at[:, 0:2 * nlev:2, :].set(jnp.transpose(jnp.concatenate([s_re, s_re], -1), (1, 0, 2)))
    pw = pw.at[:, 1:2 * nlev:2, :].set(jnp.transpose(jnp.concatenate([-s_im, s_im], -1), (1, 0, 2)))
    return toep.astype(BF16), bmat.astype(BF16), emat, pw


def _proj_route_body(ya_ref, yb_ref, x_ref, w_ref, ln_ref, wr_ref, br_ref,
                     h_ref, route_ref, cnt_ref, carry_ref):
    tm = ya_ref.shape[0]
    half = w_ref.shape[0] // 2

    @pl.when(pl.program_id(0) == 0)
    def _():
        carry_ref[...] = jnp.zeros_like(carry_ref)

    mix = _bdot(ya_ref[...], w_ref[:half, :]) + _bdot(yb_ref[...], w_ref[half:, :])
    h = _layer_norm(DN_ALPHA * x_ref[...] + mix, ln_ref[0:1, :], ln_ref[1:2, :])
    h_ref[...] = h

    logits = _hdot(h, wr_ref[...]) + br_ref[0:1, :]
    lane = _iota((tm, LANES), 1).astype(F32)
    neg = -jnp.inf

    def softmax_masked(mask):
        xm = jnp.where(mask, logits, neg)
        m = jnp.max(xm, axis=-1, keepdims=True)
        e = jnp.exp(xm - m)
        return e / jnp.sum(e, axis=-1, keepdims=True)

    def top1(pm):
        m = jnp.max(pm, axis=-1, keepdims=True)
        idx = jnp.min(jnp.where(pm == m, lane, float(LANES)), axis=-1, keepdims=True)
        return m, idx

    coarse = jnp.where(lane < MOE_GROUPS, softmax_masked(lane < MOE_GROUPS), -1.0)
    p_grp, grp = top1(coarse)
    lo = MOE_GROUPS + MOE_PER_GROUP * grp
    fmask = (lane >= lo) & (lane < lo + MOE_PER_GROUP)
    fine = jnp.where(fmask, softmax_masked(fmask), -1.0)
    p1, j1 = top1(fine)
    p2, j2 = top1(jnp.where(lane == j1, -1.0, fine))
    denom = p1 + p2
    g1 = p_grp * (p1 / denom)
    g2 = p_grp * (p2 / denom)
    e1 = j1 - MOE_GROUPS
    e2 = j2 - MOE_GROUPS

    oh1 = jnp.where(lane == e1, 1.0, 0.0)
    oh2 = jnp.where(lane == e2, 1.0, 0.0)
    cnt = oh1 + oh2
    strict = jnp.where(_iota((tm, tm), 0) > _iota((tm, tm), 1), 1.0, 0.0).astype(BF16)
    before = _d(strict, cnt.astype(BF16)) + carry_ref[0:1, :]
    r1 = jnp.sum(oh1 * before, axis=-1, keepdims=True)
    r2 = jnp.sum(oh2 * before, axis=-1, keepdims=True)
    carry_ref[0:1, :] = carry_ref[0:1, :] + jnp.sum(cnt, axis=0, keepdims=True)
    cnt_ref[...] = carry_ref[...]

    out = jnp.where(lane == 0, e1, 0.0)
    out = jnp.where(lane == 1, e2, out)
    out = jnp.where(lane == 2, r1, out)
    out = jnp.where(lane == 3, r2, out)
    out = jnp.where(lane == 4, g1, out)
    out = jnp.where(lane == 5, g2, out)
    route_ref[...] = out[:, :8]


def _proj_route(ya, yb, resid, w_out, ln, wr, br, tm):
    t, d = resid.shape
    return pl.pallas_call(
        _proj_route_body,
        grid=(t // tm,),
        in_specs=[pl.BlockSpec((tm, 512), lambda i: (i, 0)),
                  pl.BlockSpec((tm, 512), lambda i: (i, 0)),
                  pl.BlockSpec((tm, d), lambda i: (i, 0)),
                  pl.BlockSpec((d, d), lambda i: (0, 0)),
                  pl.BlockSpec((8, d), lambda i: (0, 0)),
                  pl.BlockSpec((d, LANES), lambda i: (0, 0)),
                  pl.BlockSpec((8, LANES), lambda i: (0, 0))],
        out_specs=[pl.BlockSpec((tm, d), lambda i: (i, 0)),
                   pl.BlockSpec((tm, 8), lambda i: (i, 0)),
                   pl.BlockSpec((8, LANES), lambda i: (0, 0))],
        out_shape=[jax.ShapeDtypeStruct((t, d), F32),
                   jax.ShapeDtypeStruct((t, 8), F32),
                   jax.ShapeDtypeStruct((8, LANES), F32)],
        scratch_shapes=[pltpu.VMEM((8, LANES), F32)],
        compiler_params=_cparams(1),
        name="proj_ln_route",
    )(ya, yb, resid, w_out, ln, wr, br)


def _row_copy(src_ref, src_row, dst_ref, dst_row, sem):
    return pltpu.make_async_copy(src_ref.at[pl.ds(src_row, 1)], dst_ref.at[pl.ds(dst_row, 1)], sem)


def _dispatch_body(dest_ref, h_ref, init_ref, rows_ref, sem):
    del init_ref
    tm = h_ref.shape[0]

    def copies(i):
        return [_row_copy(h_ref, i, rows_ref, dest_ref[0, 0, 2 * i + s], sem) for s in range(2)]

    def issue(i, carry):
        for cp in copies(i):
            cp.start()
        return carry

    lax.fori_loop(0, tm, issue, 0)

    def drain(i, carry):
        for cp in copies(i):
            cp.wait()
        return carry

    lax.fori_loop(0, tm, drain, 0)


def _dispatch(dest3, h, rows_init, tm):
    t, d = h.shape
    return pl.pallas_call(
        _dispatch_body,
        grid=(t // tm,),
        in_specs=[pl.BlockSpec((1, 1, 2 * tm), lambda i: (i, 0, 0), memory_space=pltpu.SMEM),
                  pl.BlockSpec((tm, d), lambda i: (i, 0)),
                  pl.BlockSpec(memory_space=pl.ANY)],
        out_specs=pl.BlockSpec(memory_space=pl.ANY),
        out_shape=jax.ShapeDtypeStruct(rows_init.shape, F32),
        scratch_shapes=[pltpu.SemaphoreType.DMA],
        input_output_aliases={2: 0},
        compiler_params=_cparams(1),
        name="moe_dispatch",
    )(dest3, h, rows_init)


def _expert_body(be_ref, nused_ref, x_ref, w1_ref, w3_ref, w2_ref, o_ref):
    del be_ref

    @pl.when(pl.program_id(0) < nused_ref[0])
    def _():
        xb = x_ref[...].astype(BF16)
        h1 = _d(xb, w1_ref[0].astype(BF16))
        h3 = _d(xb, w3_ref[0].astype(BF16))
        hid = _silu(h1) * h3
        o_ref[...] = _d(hid.astype(BF16), w2_ref[0].astype(BF16))

    @pl.when(pl.program_id(0) >= nused_ref[0])
    def _():
        o_ref[...] = jnp.zeros_like(o_ref)


def _experts(block_e, nused, x_rows, w1, w3, w2):
    r, d = x_rows.shape
    nb = r // MOE_BLOCK
    hid = w1.shape[-1]
    grid_spec = pltpu.PrefetchScalarGridSpec(
        num_scalar_prefetch=2,
        grid=(nb,),
        in_specs=[pl.BlockSpec((MOE_BLOCK, d), lambda i, be, nu: (i, 0)),
                  pl.BlockSpec((1, d, hid), lambda i, be, nu: (be[i], 0, 0)),
                  pl.BlockSpec((1, d, hid), lambda i, be, nu: (be[i], 0, 0)),
                  pl.BlockSpec((1, hid, d), lambda i, be, nu: (be[i], 0, 0))],
        out_specs=pl.BlockSpec((MOE_BLOCK, d), lambda i, be, nu: (i, 0)),
    )
    return pl.pallas_call(
        _expert_body,
        grid_spec=grid_spec,
        out_shape=jax.ShapeDtypeStruct((r, d), F32),
        compiler_params=_cparams(1),
        name="moe_experts",
    )(block_e, nused, x_rows, w1, w3, w2)


def _combine_body(dest_ref, gate_ref, h_ref, ln_ref, rows_ref, o_ref, buf_ref, sem):
    tm = h_ref.shape[0]

    def copies(i):
        return [_row_copy(rows_ref, dest_ref[0, 0, 2 * i + s], buf_ref.at[s], i, sem) for s in range(2)]

    def issue(i, carry):
        for cp in copies(i):
            cp.start()
        return carry

    lax.fori_loop(0, tm, issue, 0)

    def drain(i, carry):
        for cp in copies(i):
            cp.wait()
        return carry

    lax.fori_loop(0, tm, drain, 0)
    gate = gate_ref[...]
    y = gate[:, 4:5] * buf_ref[0] + gate[:, 5:6] * buf_ref[1]
    o_ref[...] = _layer_norm(DN_ALPHA * h_ref[...] + y, ln_ref[0:1, :], ln_ref[1:2, :])


def _combine(dest3, route, h, ln, y_rows, tm):
    t, d = h.shape
    return pl.pallas_call(
        _combine_body,
        grid=(t // tm,),
        in_specs=[pl.BlockSpec((1, 1, 2 * tm), lambda i: (i, 0, 0), memory_space=pltpu.SMEM),
                  pl.BlockSpec((tm, 8), lambda i: (i, 0)),
                  pl.BlockSpec((tm, d), lambda i: (i, 0)),
                  pl.BlockSpec((8, d), lambda i: (0, 0)),
                  pl.BlockSpec(memory_space=pl.ANY)],
        out_specs=pl.BlockSpec((tm, d), lambda i: (i, 0)),
        out_shape=jax.ShapeDtypeStruct((t, d), F32),
        scratch_shapes=[pltpu.VMEM((2, tm, d), F32), pltpu.SemaphoreType.DMA],
        compiler_params=_cparams(1),
        name="moe_combine_ln",
    )(dest3, route, h, ln, y_rows)


def _moe(h, route, counts, w1, w3, w2, ln, tm):
    t, d = h.shape
    a = 2 * t
    expert = route[:, 0:2].astype(jnp.int32)
    rank = route[:, 2:4].astype(jnp.int32)
    cnt = counts[0, :N_EXPERTS].astype(jnp.int32)
    padded = (cnt + MOE_BLOCK - 1) // MOE_BLOCK * MOE_BLOCK
    pend = jnp.cumsum(padded)
    pstart = pend - padded
    dest = pstart[expert] + rank
    n_blocks = -(-a // MOE_BLOCK) + N_EXPERTS
    block_e = jnp.minimum(jnp.searchsorted(pend, jnp.arange(n_blocks) * MOE_BLOCK, side='right'),
                          N_EXPERTS - 1).astype(jnp.int32)
    nused = (pend[-1:] // MOE_BLOCK).astype(jnp.int32)
    dest3 = dest.reshape(t // tm, 1, 2 * tm)
    x_rows = _dispatch(dest3, h, jnp.zeros((n_blocks * MOE_BLOCK, d), F32), tm)
    y_rows = _experts(block_e, nused, x_rows, w1, w3, w2)
    return _combine(dest3, route, h, ln, y_rows, tm)


def _pad_rows(x, rows):
    return jnp.zeros((rows,) + x.shape[1:], x.dtype).at[:x.shape[0]].set(x)


def _route_weights(wg, bg, we, be):
    d = wg.shape[0]
    wr = jnp.zeros((d, LANES), F32)
    wr = wr.at[:, :MOE_GROUPS].set(wg)
    wr = wr.at[:, MOE_GROUPS:MOE_GROUPS + N_EXPERTS].set(jnp.transpose(we, (1, 0, 2)).reshape(d, N_EXPERTS))
    br = jnp.zeros((8, LANES), F32)
    br = br.at[0, :MOE_GROUPS].set(bg)
    br = br.at[0, MOE_GROUPS:MOE_GROUPS + N_EXPERTS].set(be.reshape(N_EXPERTS))
    return wr, br


def kernel(x, ab_w_in, rw_mu, rw_w0, rw_w2, rw_a0, rw_a2, rw_g2, rw_k_k, rw_k_a, rw_r_k, rw_gn_g, rw_gn_b, gla_gk_w2, gla_gk_b, gla_norm_g, ab_w_out, cd_w_in, s5_a_re, s5_a_im, s5_log_dt, s5_b_re, s5_b_im, s5_c_re, s5_c_im, s5_d, s5_glu_w, s5_glu_b, hg_lb, hg_norm_g, cd_w_out, ln1_g, ln1_b, moe_wg, moe_bg, moe_we, moe_be, moe_w1, moe_w3, moe_w2, ln2_g, ln2_b):
    batch, seq, d = x.shape
    t = batch * seq
    assert d == D_MODEL and seq % CHUNK == 0
    rows_per_seq = seq // CHUNK
    assert rows_per_seq & (rows_per_seq - 1) == 0, "S5 chunk scan assumes a power-of-two chunk count"
    tm = min(256, t)
    assert t % tm == 0
    xt = x.reshape(t, d)
    ln1 = [_pad_rows(jnp.stack([ln1_g[l], ln1_b[l]]), 8) for l in range(DEPTH)]
    ln2 = [_pad_rows(jnp.stack([ln2_g[l], ln2_b[l]]), 8) for l in range(DEPTH)]

    j = 0
    w = ab_w_in[j]
    r_, wl_, k_, v_, al_, gl_ = 0, 512, 576, 1088, 1600, 1664
    gq, gk, gv, glow, ggate = 1792, 2048, 2304, 2816, 2832
    w0cols = jnp.concatenate([
        w[:, r_:r_ + 512], w[:, k_:k_ + 512], w[:, v_:v_ + 512], w[:, gv:gv + 512], w[:, ggate:ggate + 512],
        w[:, wl_:wl_ + 64], w[:, al_:al_ + 64], w[:, gl_:gl_ + 128], w[:, gq:gq + 256], w[:, gk:gk + 256],
        w[:, glow:glow + 16], jnp.zeros((d, 112), F32)], axis=1).astype(BF16)
    p0 = _matmul(xt, w0cols, tm)

    mu = rw_mu[j]
    vec = _pad_rows(jnp.stack([mu[r_:r_ + 512], mu[k_:k_ + 512], mu[v_:v_ + 512], rw_w0[j], rw_a0[j],
                               rw_k_k[j], rw_k_a[j], rw_r_k[j].reshape(-1), rw_gn_g[j], rw_gn_b[j]]), 16)
    mulo = _pad_rows(jnp.concatenate([mu[wl_:wl_ + 64], mu[al_:al_ + 64], mu[gl_:gl_ + 128]])[None], 8)
    w2p = _pad_rows(rw_w2[j], 128)
    a2p = jnp.zeros((128, 512), F32).at[64:].set(rw_a2[j])
    y_rw = _rwkv(p0, vec, mulo, w2p, a2p, rw_g2[j], batch, seq)
    y_gla = _gla(p0, _pad_rows(gla_gk_w2[j], 128), _pad_rows(gla_gk_b[j][None], 8),
                 _pad_rows(gla_norm_g[j][None], 8), batch, seq)

    wr, br = _route_weights(moe_wg[0], moe_bg[0], moe_we[0], moe_be[0])
    h, route, counts = _proj_route(y_rw, y_gla, xt, ab_w_out[j].astype(BF16), ln1[0], wr, br, tm)
    h = _moe(h, route, counts, moe_w1[0], moe_w3[0], moe_w2[0], ln2[0], tm)

    p1 = _matmul(h, cd_w_in[j].astype(BF16), tm)
    lb_sm = jax.nn.softmax(hg_lb.astype(F32), axis=0)
    lower = (jnp.cumsum(lb_sm, axis=0) - lb_sm[0])[1]
    y_hg = _hgrn(p1, _pad_rows(lower[None], 8), _pad_rows(hg_norm_g[j][None], 8), batch, seq)

    toep, bmat, emat, pw = _s5_tables(s5_a_re[j], s5_a_im[j], s5_log_dt[j], s5_b_re[j], s5_b_im[j],
                                      s5_c_re[j], s5_c_im[j], rows_per_seq)
    nrows = t // CHUNK
    uc = p1[:, :512].reshape(nrows, CHUNK, S5_GROUPS, S5_GROUP)
    uc = jnp.transpose(uc, (2, 0, 1, 3)).reshape(S5_GROUPS, nrows, CHUNK * S5_GROUP).astype(BF16)
    yc = _s5_scan(uc, toep, bmat, emat, pw, rows_per_seq)
    y_ssm = jnp.transpose(yc.reshape(S5_GROUPS, nrows, CHUNK, S5_GROUP), (1, 2, 0, 3)).reshape(t, 512)
    y_s5 = _s5_post(y_ssm, p1, _pad_rows(jnp.stack([s5_d[j], s5_glu_b[j]]), 8),
                    s5_glu_w[j].astype(BF16), tm)

    wr, br = _route_weights(moe_wg[1], moe_bg[1], moe_we[1], moe_be[1])
    h2, route, counts = _proj_route(y_s5, y_hg, h, cd_w_out[j].astype(BF16), ln1[1], wr, br, tm)
    out = _moe(h2, route, counts, moe_w1[1], moe_w3[1], moe_w2[1], ln2[1], tm)
    return out.reshape(batch, seq, d)
```

```python
import functools
import math

import numpy as np
import jax
import jax.numpy as jnp
from jax import lax
from jax.experimental import pallas as pl
from jax.experimental.pallas import tpu as pltpu

F32 = jnp.float32
BF16 = jnp.bfloat16

D_MODEL = 1024
DEPTH = 2
RW_HEAD = 64
RW_WIDTH = 512
RW_GN_EPS = 64e-5
GLA_HEADS = 4
GLA_DK = 64
GLA_DV = 128
GLA_GATE_TAU = 16.0
S5_GROUP = 16
S5_GROUPS = 32
S5_STATE = 64
HG_HEADS = 4
HG_DK = 128
CHUNK = 64
S5_CHUNK = 16
S5_SCAN_ROWS = 32
NORM_EPS = 1e-5
MOE_GROUPS = 4
MOE_PER_GROUP = 8
N_EXPERTS = 32
EXPERT_HIDDEN = 512
MOE_BLOCK = 128
DN_ALPHA = (2.0 * DEPTH) ** 0.25

LANES = 128
VMEM_LIMIT = 56 * 1024 * 1024


def _cparams(n_axes=1):
    return pltpu.CompilerParams(dimension_semantics=("arbitrary",) * n_axes,
                                vmem_limit_bytes=VMEM_LIMIT)


def _d(a, b):
    return jnp.dot(a, b, preferred_element_type=F32)


def _d_nt(a, b):
    return lax.dot_general(a, b, (((1,), (1,)), ((), ())), preferred_element_type=F32)


def _d_tn(a, b):
    return lax.dot_general(a, b, (((0,), (0,)), ((), ())), preferred_element_type=F32)


def _split(a):
    hi = a.astype(BF16)
    lo = (a - hi.astype(F32)).astype(BF16)
    return hi, lo


def _split3(a):
    hi = a.astype(BF16)
    r1 = a - hi.astype(F32)
    mid = r1.astype(BF16)
    lo = (r1 - mid.astype(F32)).astype(BF16)
    return hi, mid, lo


def _bdot(a, b):
    return _d(a.astype(BF16), b.astype(BF16))


def _bdot_nt(a, b):
    return _d_nt(a.astype(BF16), b.astype(BF16))


def _bdot_tn(a, b):
    return _d_tn(a.astype(BF16), b.astype(BF16))


def _hdot_with(d, a, b):
    ah, al = _split(a)
    bh, bl = _split(b)
    return d(ah, bh) + (d(ah, bl) + d(al, bh))


def _hdot(a, b):
    return _hdot_with(_d, a, b)


def _hdot_nt(a, b):
    return _hdot_with(_d_nt, a, b)


def _hdot_tn(a, b):
    return _hdot_with(_d_tn, a, b)


def _xdot_l(a, e):
    ah, am, al = _split3(a)
    return _d(ah, e) + (_d(am, e) + _d(al, e))


def _xdot_r(e, a):
    ah, am, al = _split3(a)
    return _d(e, ah) + (_d(e, am) + _d(e, al))


def _iota(shape, dim):
    return lax.broadcasted_iota(jnp.int32, shape, dim)


def _softplus(x):
    return jnp.maximum(x, 0.0) + jnp.log1p(jnp.exp(-jnp.abs(x)))


def _sigmoid(x):
    return 1.0 / (1.0 + jnp.exp(-x))


def _silu(x):
    return x * _sigmoid(x)


def _tril_incl(n):
    return jnp.where(_iota((n, n), 0) >= _iota((n, n), 1), 1.0, 0.0).astype(BF16)


def _cumsum_rows(g):
    return _xdot_r(_tril_incl(g.shape[0]), g)


def _shift_mix(x, prev_ref, b, mu):
    c = x.shape[0]
    rolled = pltpu.roll(x, 1, 0)
    prev = jnp.where(_iota(x.shape, 0) == 0, jnp.broadcast_to(prev_ref[b, 0:1, :], x.shape), rolled)
    prev_ref[b, 0:1, :] = x[c - 1:c, :]
    return x + mu * (prev - x)


def _layer_norm(x, g, b):
    mu = jnp.mean(x, axis=-1, keepdims=True)
    xc = x - mu
    var = jnp.mean(xc * xc, axis=-1, keepdims=True)
    return xc * lax.rsqrt(var + NORM_EPS) * g + b


def _mm_body(x_ref, w_ref, o_ref):
    o_ref[...] = _d(x_ref[...].astype(BF16), w_ref[...])


def _matmul(x, w_bf16, tm):
    m, k = x.shape
    n = w_bf16.shape[1]
    return pl.pallas_call(
        _mm_body,
        grid=(m // tm,),
        in_specs=[pl.BlockSpec((tm, k), lambda i: (i, 0)),
                  pl.BlockSpec((k, n), lambda i: (0, 0))],
        out_specs=pl.BlockSpec((tm, n), lambda i: (i, 0)),
        out_shape=jax.ShapeDtypeStruct((m, n), F32),
        compiler_params=_cparams(1),
        name="in_proj",
    )(x, w_bf16)


_RV_MU_R, _RV_MU_K, _RV_MU_V, _RV_W0, _RV_A0, _RV_KK, _RV_KA, _RV_RK, _RV_GNG, _RV_GNB = range(10)


def _rwkv_body(r_ref, k_ref, v_ref, lo_ref, vec_ref, mulo_ref, w2_ref, a2_ref, g2_ref, ones_ref,
               o_ref, pr_ref, pk_ref, pv_ref, plo_ref, st_ref):
    nb, c = r_ref.shape[0], r_ref.shape[1]
    npair = RW_WIDTH // LANES

    @pl.when(pl.program_id(0) == 0)
    def _():
        pr_ref[...] = jnp.zeros_like(pr_ref)
        pk_ref[...] = jnp.zeros_like(pk_ref)
        pv_ref[...] = jnp.zeros_like(pv_ref)
        plo_ref[...] = jnp.zeros_like(plo_ref)
        st_ref[...] = jnp.zeros_like(st_ref)

    def vec(i):
        return vec_ref[i:i + 1, :]

    ones_bd = ones_ref[...]
    lane = _iota((c, LANES), 1)
    m1 = lane < RW_HEAD
    row2 = _iota((2 * c, 4 * c), 0)
    col2 = _iota((2 * c, 4 * c), 1) & (c - 1)
    tri = ((row2 < c) & (row2 > col2)) | ((row2 >= c) & ((row2 - c) >= col2))
    eye2 = jnp.where(_iota((2 * c, 2 * c), 0) == _iota((2 * c, 2 * c), 1), 1.0, 0.0)
    bd_p = (_iota((LANES, LANES), 0) >> 6) == (_iota((LANES, LANES), 1) >> 6)

    def halves(x):
        return jnp.concatenate([jnp.where(m1, x, 0.0), jnp.where(m1, 0.0, x)], axis=0)

    prep = []
    for b in range(nb):
        xr = _shift_mix(r_ref[b], pr_ref, b, vec(_RV_MU_R))
        xk = _shift_mix(k_ref[b], pk_ref, b, vec(_RV_MU_K))
        xv = _shift_mix(v_ref[b], pv_ref, b, vec(_RV_MU_V))
        xlo = _shift_mix(lo_ref[b], plo_ref, b, mulo_ref[0:1, :])
        lo_a = xlo[:, :LANES]
        lo_g = xlo[:, LANES:]
        w = -_softplus(-(vec(_RV_W0) + _bdot(jnp.tanh(lo_a), w2_ref[...]))) - 0.5
        g = -jnp.exp(w)
        a = _sigmoid(vec(_RV_A0) + _bdot(lo_a, a2_ref[...]))
        gate = _bdot(_sigmoid(lo_g), g2_ref[...])
        kk = xk * vec(_RV_KK)
        kk = kk / jnp.maximum(jnp.sqrt(_xdot_l(kk * kk, ones_bd)), 1e-12)
        k2 = xk * (1.0 + (a - 1.0) * vec(_RV_KA))
        gc = _cumsum_rows(g)
        g_last = gc[c - 1:c, :]
        e_neg = jnp.exp(-gc)
        e_end = jnp.exp(g_last - gc)
        prep.append(dict(
            xr=xr, xv=xv, k2=k2, gate=gate,
            at=-kk * jnp.exp(gc - g),
            bt=(kk * a) * e_neg, kt=k2 * e_neg, rt=xr * jnp.exp(gc),
            bh=(kk * a) * e_end, kh=k2 * e_end,
            gam=jnp.exp(g_last)))

    chains = [(b, p) for b in range(nb) for p in range(npair)]

    def part(ch, name):
        b, p = ch
        return prep[b][name][:, p * LANES:(p + 1) * LANES]

    a_ak, a_row, vv, ak, pinv = {}, {}, {}, {}, {}
    for ch in chains:
        lhs = jnp.concatenate([part(ch, 'at'), part(ch, 'rt')], axis=0)
        rhs = jnp.concatenate([halves(part(ch, 'bt')), halves(part(ch, 'kt'))], axis=0)
        aa = jnp.where(tri, _bdot_nt(lhs, rhs), 0.0)
        a_ak[ch] = aa[:c, 2 * c:].astype(BF16)
        a_row[ch] = aa[c:, :].astype(BF16)
        abd = halves(aa[:c, :2 * c])
        pinv[ch] = eye2 + abd
        ak[ch] = abd
        vv[ch] = halves(part(ch, 'xv')).astype(BF16)
    nlev = int(math.log2(c))
    for lev in range(nlev):
        for ch in chains:
            akb = ak[ch].astype(BF16)
            if lev == 0:
                ak[ch] = _d(akb, akb)
            elif lev < nlev - 1:
                out = _d(akb, jnp.concatenate([akb, pinv[ch].astype(BF16)], axis=1))
                ak[ch] = out[:, :2 * c]
                pinv[ch] = pinv[ch] + out[:, 2 * c:]
            else:
                pinv[ch] = pinv[ch] + _d(akb, pinv[ch].astype(BF16))
    x2 = {ch: _d(a_ak[ch], vv[ch]) for ch in chains}

    sts = {ch: st_ref[ch[0] * npair + ch[1]] for ch in chains}
    xs = {ch: _d_nt(jnp.concatenate([part(ch, 'at'), part(ch, 'rt')], axis=0).astype(BF16),
                    sts[ch].astype(BF16)) for ch in chains}
    us = {}
    for ch in chains:
        u2 = _d(pinv[ch].astype(BF16), halves(xs[ch][:c] + x2[ch]).astype(BF16))
        us[ch] = u2[:c] + u2[c:]
    ys = {}
    for ch in chains:
        b, p = ch
        u = us[ch]
        ys[ch] = xs[ch][c:] + _d(a_row[ch], jnp.concatenate([halves(u).astype(BF16), vv[ch]], axis=0))
        upd = _d_tn(jnp.concatenate([u, part(ch, 'xv')], axis=0).astype(BF16),
                    jnp.concatenate([part(ch, 'bh'), part(ch, 'kh')], axis=0).astype(BF16))
        st_ref[b * npair + p] = sts[ch] * part(ch, 'gam') + jnp.where(bd_p, upd, 0.0)

    inv_n = 1.0 / RW_HEAD
    for b in range(nb):
        y = jnp.concatenate([ys[(b, p)] for p in range(npair)], axis=1)
        mean = _xdot_l(y, ones_bd) * inv_n
        yc = y - mean
        var = _xdot_l(yc * yc, ones_bd) * inv_n
        yn = yc * lax.rsqrt(var + RW_GN_EPS) * vec(_RV_GNG) + vec(_RV_GNB)
        bonus = _xdot_l(prep[b]['xr'] * prep[b]['k2'] * vec(_RV_RK), ones_bd) * prep[b]['xv']
        o_ref[b] = (yn + bonus) * prep[b]['gate']


def _rwkv(p0, vec, mulo, w2p, a2p, g2):
    batch, seq, _ = p0.shape
    nc = seq // CHUNK
    c = CHUNK
    ones_bd = jnp.asarray(np.kron(np.eye(RW_WIDTH // RW_HEAD), np.ones((RW_HEAD, RW_HEAD))), BF16)

    def col(j, width):
        return pl.BlockSpec((batch, c, width), lambda i: (0, i, j))

    def full(shape):
        return pl.BlockSpec(shape, lambda i: (0,) * len(shape))

    return pl.pallas_call(
        _rwkv_body,
        grid=(nc,),
        in_specs=[col(0, 512), col(1, 512), col(2, 512), col(10, 256),
                  full((16, 512)), full((8, 256)), full((128, 512)), full((128, 512)),
                  full((128, 512)), full((512, 512))],
        out_specs=pl.BlockSpec((batch, c, 512), lambda i: (0, i, 0)),
        out_shape=jax.ShapeDtypeStruct((batch, seq, RW_WIDTH), F32),
        scratch_shapes=[pltpu.VMEM((batch, 8, 512), F32), pltpu.VMEM((batch, 8, 512), F32),
                        pltpu.VMEM((batch, 8, 512), F32), pltpu.VMEM((batch, 8, 256), F32),
                        pltpu.VMEM((batch * RW_WIDTH // LANES, LANES, LANES), F32)],
        compiler_params=_cparams(1),
        name="rwkv7",
    )(p0, p0, p0, p0, vec, mulo, w2p, a2p, g2, ones_bd)


def _gla_core(qs, ks, vs, gs, st_ref, heads_per_block):
    nb = len(qs)
    c = qs[0].shape[0]
    hpb = heads_per_block
    nblk = qs[0].shape[1] // LANES
    dk_shift = int(math.log2(LANES // hpb))
    row = _iota(qs[0].shape, 0)

    prep = []
    for q, k, g in zip(qs, ks, gs):
        b = _cumsum_rows(g)

        def brow(i, b=b):
            return jnp.broadcast_to(b[i:i + 1, :], b.shape)

        b15, b31, b47, blast = brow(15), brow(31), brow(47), brow(c - 1)
        ref_b = jnp.where(row < 32, b15, b47)
        ref_d = jnp.where(row < 16, 0.0, jnp.where(row < 32, b15, jnp.where(row < 48, b31, b47)))
        prep.append(dict(
            q_a=q * jnp.exp(jnp.minimum(b - b31, 0.0)), k_a=k * jnp.exp(jnp.minimum(b31 - b, 0.0)),
            q_b=q * jnp.exp(jnp.minimum(b - ref_b, 0.0)), k_b=k * jnp.exp(jnp.minimum(ref_b - b, 0.0)),
            q_d=q * jnp.exp(b - ref_d), k_d=k * jnp.exp(ref_d - b),
            q_i=q * jnp.exp(b), k_s=k * jnp.exp(blast - b), gam=jnp.exp(b[c - 1:c, :])))

    ri = _iota((hpb * c, c), 0) & (c - 1)
    ci = _iota((hpb * c, c), 1)
    mask_a = (ri >= 32) & (ci < 32)
    mask_b = ((ri >> 5) == (ci >> 5)) & (((ri >> 4) & 1) == 1) & (((ci >> 4) & 1) == 0)
    mask_d = ((ri >> 4) == (ci >> 4)) & (ri >= ci)
    lane = _iota((c, LANES), 1)
    bd = (_iota((hpb * LANES, LANES), 0) >> 7) == (_iota((hpb * LANES, LANES), 1) >> dk_shift)

    def heads_rows(x):
        if hpb == 1:
            return x
        return jnp.concatenate([jnp.where((lane >> dk_shift) == h, x, 0.0) for h in range(hpb)], axis=0)

    chains = [(bi, blk) for bi in range(nb) for blk in range(nblk)]

    def part(ch, name):
        bi, blk = ch
        return prep[bi][name][:, blk * LANES:(blk + 1) * LANES]

    v_p = {ch: vs[ch[0]][:, ch[1] * hpb * LANES:(ch[1] + 1) * hpb * LANES].astype(BF16) for ch in chains}
    probs = {}
    for ch in chains:
        s_a = _bdot_nt(heads_rows(part(ch, 'q_a')), part(ch, 'k_a'))
        s_b = _bdot_nt(heads_rows(part(ch, 'q_b')), part(ch, 'k_b'))
        s_d = _bdot_nt(heads_rows(part(ch, 'q_d')), part(ch, 'k_d'))
        probs[ch] = (jnp.where(mask_a, s_a, 0.0) + jnp.where(mask_b, s_b, 0.0)
                     + jnp.where(mask_d, s_d, 0.0)).astype(BF16)
    outs = {}
    for ch in chains:
        pv = _d(probs[ch], v_p[ch])
        o = pv[:c]
        for h in range(1, hpb):
            o = jnp.where((_iota(o.shape, 1) >> 7) == h, pv[h * c:(h + 1) * c], o)
        si = ch[0] * nblk + ch[1]
        sp = st_ref[si]
        outs[ch] = o + _bdot_nt(part(ch, 'q_i'), sp)
        upd = _d_tn(v_p[ch], part(ch, 'k_s').astype(BF16))
        st_ref[si] = sp * part(ch, 'gam') + jnp.where(bd, upd, 0.0)
    return [jnp.concatenate([outs[(bi, blk)] for blk in range(nblk)], axis=1) for bi in range(nb)]


def _gated_rmsnorm(o, gate, norm_g):
    nh = o.shape[1] // LANES
    outs = []
    for h in range(nh):
        sl = slice(h * LANES, (h + 1) * LANES)
        oh = o[:, sl]
        ms = jnp.mean(oh * oh, axis=-1, keepdims=True)
        outs.append(oh * lax.rsqrt(ms + NORM_EPS) * norm_g * _silu(gate[:, sl]))
    return jnp.concatenate(outs, axis=1)


def _gla_body(q_ref, k_ref, v_ref, gate_ref, gk_ref, w2_ref, vec_ref, ng_ref, o_ref, st_ref):
    nb = q_ref.shape[0]

    @pl.when(pl.program_id(0) == 0)
    def _():
        st_ref[...] = jnp.zeros_like(st_ref)

    gs = [-_softplus(-(_hdot(gk_ref[b], w2_ref[...]) + vec_ref[0:1, :])) * (1.0 / GLA_GATE_TAU)
          for b in range(nb)]
    qs = [q_ref[b] * (GLA_DK ** -0.5) for b in range(nb)]
    os_ = _gla_core(qs, [k_ref[b] for b in range(nb)], [v_ref[b] for b in range(nb)], gs, st_ref, 2)
    for b in range(nb):
        o_ref[b] = _gated_rmsnorm(os_[b], gate_ref[b], ng_ref[0:1, :])


def _seq_specs(batch, c):
    def col(j, width):
        return pl.BlockSpec((batch, c, width), lambda i: (0, i, j))

    def full(shape):
        return pl.BlockSpec(shape, lambda i: (0,) * len(shape))

    return col, full


def _gla(p0, gk_w2p, gk_b, norm_g):
    batch, seq, _ = p0.shape
    c = CHUNK
    col, full = _seq_specs(batch, c)
    return pl.pallas_call(
        _gla_body,
        grid=(seq // c,),
        in_specs=[col(11, 256), col(12, 256), col(3, 512), col(4, 512), col(26, 128),
                  full((128, 256)), full((8, 256)), full((8, 128))],
        out_specs=pl.BlockSpec((batch, c, 512), lambda i: (0, i, 0)),
        out_shape=jax.ShapeDtypeStruct((batch, seq, 512), F32),
        scratch_shapes=[pltpu.VMEM((batch * 2, 2 * LANES, LANES), F32)],
        compiler_params=_cparams(1),
        name="gla",
    )(p0, p0, p0, p0, p0, gk_w2p, gk_b, norm_g)


def _hgrn_body(q_ref, f_ref, i_ref, gate_ref, lb_ref, ng_ref, o_ref, st_ref):
    nb = q_ref.shape[0]

    @pl.when(pl.program_id(0) == 0)
    def _():
        st_ref[...] = jnp.zeros_like(st_ref)

    lb = lb_ref[0:1, :]
    qs, ks, gs = [], [], []
    for b in range(nb):
        f = f_ref[b]
        gs.append(jnp.log(lb + (1.0 - lb) * _sigmoid(f)))
        ks.append((1.0 - lb) * _sigmoid(-f))
        qs.append(_silu(q_ref[b]))
    os_ = _gla_core(qs, ks, [i_ref[b] for b in range(nb)], gs, st_ref, 1)
    for b in range(nb):
        o_ref[b] = _gated_rmsnorm(os_[b], gate_ref[b], ng_ref[0:1, :])


def _hgrn(p1, lb, norm_g):
    batch, seq, _ = p1.shape
    c = CHUNK
    col, full = _seq_specs(batch, c)
    return pl.pallas_call(
        _hgrn_body,
        grid=(seq // c,),
        in_specs=[col(1, 512), col(2, 512), col(3, 512), col(4, 512), full((8, 512)), full((8, 128))],
        out_specs=pl.BlockSpec((batch, c, 512), lambda i: (0, i, 0)),
        out_shape=jax.ShapeDtypeStruct((batch, seq, 512), F32),
        scratch_shapes=[pltpu.VMEM((batch * HG_HEADS, LANES, LANES), F32)],
        compiler_params=_cparams(1),
        name="hgrn2",
    )(p1, p1, p1, p1, lb, norm_g)


def _s5_body(u_ref, toep_ref, bm_ref, e_ref, pw_ref, o_ref, *, rows_per_seq):
    u = u_ref[0]
    rows = u.shape[0]
    s = _d(u, bm_ref[0])
    rin = _iota((rows, LANES), 0) & (rows_per_seq - 1)
    pw = pw_ref[0]
    h = s
    nlev = int(math.log2(rows_per_seq))
    for lev in range(nlev):
        sh = 1 << lev
        hs = jnp.where(rin >= sh, pltpu.roll(h, sh, 0), 0.0)
        hs_sw = pltpu.roll(hs, S5_STATE, 1)
        h = h + pw[2 * lev:2 * lev + 1, :] * hs + pw[2 * lev + 1:2 * lev + 2, :] * hs_sw
    hprev = jnp.where(rin >= 1, pltpu.roll(h, 1, 0), 0.0)
    o_ref[0] = _d(u, toep_ref[0]) + _hdot(hprev, e_ref[0])


def _s5_scan(uc, toep, bmat, emat, pw, rows_per_seq):
    ng, rows, width = uc.shape
    return pl.pallas_call(
        functools.partial(_s5_body, rows_per_seq=rows_per_seq),
        grid=(ng,),
        in_specs=[pl.BlockSpec((1, rows, width), lambda g: (g, 0, 0)),
                  pl.BlockSpec((1, width, width), lambda g: (g, 0, 0)),
                  pl.BlockSpec((1, width, LANES), lambda g: (g, 0, 0)),
                  pl.BlockSpec((1, LANES, width), lambda g: (g, 0, 0)),
                  pl.BlockSpec((1, S5_SCAN_ROWS, LANES), lambda g: (g, 0, 0))],
        out_specs=pl.BlockSpec((1, rows, width), lambda g: (g, 0, 0)),
        out_shape=jax.ShapeDtypeStruct((ng, rows, width), F32),
        compiler_params=_cparams(1),
        name="s5_scan",
    )(uc, toep, bmat, emat, pw)


def _s5_post_body(y_ref, u_ref, vec_ref, w_ref, o_ref):
    y = y_ref[...] + vec_ref[0:1, :] * u_ref[...]
    y = 0.5 * y * (1.0 + jnp.tanh(math.sqrt(2.0 / math.pi) * (y + 0.044715 * (y * y * y))))
    o_ref[...] = y * _sigmoid(_bdot(y, w_ref[...]) + vec_ref[1:2, :])


def _s5_post(y_ssm, p1, vec, glu_w, tm):
    t = y_ssm.shape[0]
    return pl.pallas_call(
        _s5_post_body,
        grid=(t // tm,),
        in_specs=[pl.BlockSpec((tm, 512), lambda i: (i, 0)),
                  pl.BlockSpec((tm, 512), lambda i: (i, 0)),
                  pl.BlockSpec((8, 512), lambda i: (0, 0)),
                  pl.BlockSpec((512, 512), lambda i: (0, 0))],
        out_specs=pl.BlockSpec((tm, 512), lambda i: (i, 0)),
        out_shape=jax.ShapeDtypeStruct((t, 512), F32),
        compiler_params=_cparams(1),
        name="s5_post",
    )(y_ssm, p1, vec, glu_w)


def _s5_tables(a_re, a_im, log_dt, b_re, b_im, c_re, c_im, rows_per_seq):
    c = S5_CHUNK
    lam_re = jnp.minimum(a_re, -1e-4)
    lam_im = a_im
    dt = jnp.exp(log_dt)[:, None]
    mag = jnp.exp(lam_re * dt)
    abar_re = mag * jnp.cos(lam_im * dt)
    abar_im = mag * jnp.sin(lam_im * dt)
    den = lam_re * lam_re + lam_im * lam_im
    num_re = abar_re - 1.0
    z_re = (num_re * lam_re + abar_im * lam_im) / den
    z_im = (abar_im * lam_re - num_re * lam_im) / den

    def power(n):
        n = jnp.asarray(n, F32)[..., None, None]
        m = jnp.exp(n * (lam_re * dt))
        return m * jnp.cos(n * (lam_im * dt)), m * jnp.sin(n * (lam_im * dt))

    def cmul(ar, ai, br, bi):
        return ar * br - ai * bi, ar * bi + ai * br

    tau = jnp.arange(c)
    p_re, p_im = power(tau)
    zb_re, zb_im = cmul(z_re[..., None], z_im[..., None], b_re, b_im)
    cp_re, cp_im = cmul(c_re[None], c_im[None], p_re[:, :, None, :], p_im[:, :, None, :])
    hi = lax.Precision.HIGHEST
    kern = (jnp.einsum('tgon,gni->gtoi', cp_re, zb_re, precision=hi)
            - jnp.einsum('tgon,gni->gtoi', cp_im, zb_im, precision=hi))
    tt = jnp.arange(c)[None, :] - jnp.arange(c)[:, None]
    kt = jnp.where((tt >= 0)[None, :, :, None, None], kern[:, jnp.clip(tt, 0, c - 1)], 0.0)
    toep = jnp.transpose(kt, (0, 1, 4, 2, 3)).reshape(S5_GROUPS, c * S5_GROUP, c * S5_GROUP)
    q_re, q_im = power(c - 1 - tau)
    bm_re, bm_im = cmul(q_re[..., None], q_im[..., None], zb_re[None], zb_im[None])
    bmat = jnp.concatenate([jnp.transpose(bm_re, (1, 0, 3, 2)), jnp.transpose(bm_im, (1, 0, 3, 2))],
                           axis=-1).reshape(S5_GROUPS, c * S5_GROUP, 2 * S5_STATE)
    r_re, r_im = power(tau + 1)
    e_re, e_im = cmul(c_re[None], c_im[None], r_re[:, :, None, :], r_im[:, :, None, :])
    emat = jnp.concatenate([jnp.transpose(e_re, (1, 3, 0, 2)), -jnp.transpose(e_im, (1, 3, 0, 2))],
                           axis=1).reshape(S5_GROUPS, 2 * S5_STATE, c * S5_GROUP)
    nlev = int(math.log2(rows_per_seq))
    assert 2 * nlev <= S5_SCAN_ROWS
    s_re, s_im = power(c * (2 ** jnp.arange(nlev)))
    pw = jnp.zeros((S5_GROUPS, S5_SCAN_ROWS, 2 * S5_STATE), F32)
    pw = pw.at[:, 0:2 * nlev:2, :].set(jnp.transpose(jnp.concatenate([s_re, s_re], -1), (1, 0, 2)))
    pw = pw.at[:, 1:2 * nlev:2, :].set(jnp.transpose(jnp.concatenate([-s_im, s_im], -1), (1, 0, 2)))
    return toep.astype(BF16), bmat.astype(BF16), emat, pw


def _proj_route_body(ya_ref, yb_ref, x_ref, w_ref, ln_ref, wr_ref, br_ref,
                     h_ref, route_ref, cnt_ref, carry_ref):
    tm = ya_ref.shape[0]
    half = w_ref.shape[0] // 2

    @pl.when(pl.program_id(0) == 0)
    def _():
        carry_ref[...] = jnp.zeros_like(carry_ref)

    mix = _bdot(ya_ref[...], w_ref[:half, :]) + _bdot(yb_ref[...], w_ref[half:, :])
    h = _layer_norm(DN_ALPHA * x_ref[...] + mix, ln_ref[0:1, :], ln_ref[1:2, :])
    h_ref[...] = h

    logits = _hdot(h, wr_ref[...]) + br_ref[0:1, :]
    lane = _iota((tm, LANES), 1).astype(F32)
    neg = -jnp.inf

    def softmax_masked(mask):
        xm = jnp.where(mask, logits, neg)
        m = jnp.max(xm, axis=-1, keepdims=True)
        e = jnp.exp(xm - m)
        return e / jnp.sum(e, axis=-1, keepdims=True)

    def top1(pm):
        m = jnp.max(pm, axis=-1, keepdims=True)
        idx = jnp.min(jnp.where(pm == m, lane, float(LANES)), axis=-1, keepdims=True)
        return m, idx

    coarse = jnp.where(lane < MOE_GROUPS, softmax_masked(lane < MOE_GROUPS), -1.0)
    p_grp, grp = top1(coarse)
    lo = MOE_GROUPS + MOE_PER_GROUP * grp
    fmask = (lane >= lo) & (lane < lo + MOE_PER_GROUP)
    fine = jnp.where(fmask, softmax_masked(fmask), -1.0)
    p1, j1 = top1(fine)
    p2, j2 = top1(jnp.where(lane == j1, -1.0, fine))
    denom = p1 + p2
    g1 = p_grp * (p1 / denom)
    g2 = p_grp * (p2 / denom)
    e1 = j1 - MOE_GROUPS
    e2 = j2 - MOE_GROUPS

    oh1 = jnp.where(lane == e1, 1.0, 0.0)
    oh2 = jnp.where(lane == e2, 1.0, 0.0)
    cnt = oh1 + oh2
    strict = jnp.where(_iota((tm, tm), 0) > _iota((tm, tm), 1), 1.0, 0.0).astype(BF16)
    before = _d(strict, cnt.astype(BF16)) + carry_ref[0:1, :]
    r1 = jnp.sum(oh1 * before, axis=-1, keepdims=True)
    r2 = jnp.sum(oh2 * before, axis=-1, keepdims=True)
    carry_ref[0:1, :] = carry_ref[0:1, :] + jnp.sum(cnt, axis=0, keepdims=True)
    cnt_ref[...] = carry_ref[...]

    out = jnp.where(lane == 0, e1, 0.0)
    out = jnp.where(lane == 1, e2, out)
    out = jnp.where(lane == 2, r1, out)
    out = jnp.where(lane == 3, r2, out)
    out = jnp.where(lane == 4, g1, out)
    out = jnp.where(lane == 5, g2, out)
    route_ref[...] = out[:, :8]


def _proj_route(ya, yb, resid, w_out, ln, wr, br, tm):
    t, d = resid.shape
    return pl.pallas_call(
        _proj_route_body,
        grid=(t // tm,),
        in_specs=[pl.BlockSpec((tm, 512), lambda i: (i, 0)),
                  pl.BlockSpec((tm, 512), lambda i: (i, 0)),
                  pl.BlockSpec((tm, d), lambda i: (i, 0)),
                  pl.BlockSpec((d, d), lambda i: (0, 0)),
                  pl.BlockSpec((8, d), lambda i: (0, 0)),
                  pl.BlockSpec((d, LANES), lambda i: (0, 0)),
                  pl.BlockSpec((8, LANES), lambda i: (0, 0))],
        out_specs=[pl.BlockSpec((tm, d), lambda i: (i, 0)),
                   pl.BlockSpec((tm, 8), lambda i: (i, 0)),
                   pl.BlockSpec((8, LANES), lambda i: (0, 0))],
        out_shape=[jax.ShapeDtypeStruct((t, d), F32),
                   jax.ShapeDtypeStruct((t, 8), F32),
                   jax.ShapeDtypeStruct((8, LANES), F32)],
        scratch_shapes=[pltpu.VMEM((8, LANES), F32)],
        compiler_params=_cparams(1),
        name="proj_ln_route",
    )(ya, yb, resid, w_out, ln, wr, br)


def _row_copy(src_ref, src_row, dst_ref, dst_row, sem):
    return pltpu.make_async_copy(src_ref.at[pl.ds(src_row, 1)], dst_ref.at[pl.ds(dst_row, 1)], sem)


def _dispatch_body(dest_ref, h_ref, init_ref, rows_ref, sem):
    del init_ref
    tm = h_ref.shape[0]

    def copies(i):
        return [_row_copy(h_ref, i, rows_ref, dest_ref[0, 0, 2 * i + s], sem) for s in range(2)]

    def issue(i, carry):
        for cp in copies(i):
            cp.start()
        return carry

    lax.fori_loop(0, tm, issue, 0)

    def drain(i, carry):
        for cp in copies(i):
            cp.wait()
        return carry

    lax.fori_loop(0, tm, drain, 0)


def _dispatch(dest3, h, rows_init, tm):
    t, d = h.shape
    return pl.pallas_call(
        _dispatch_body,
        grid=(t // tm,),
        in_specs=[pl.BlockSpec((1, 1, 2 * tm), lambda i: (i, 0, 0), memory_space=pltpu.SMEM),
                  pl.BlockSpec((tm, d), lambda i: (i, 0)),
                  pl.BlockSpec(memory_space=pl.ANY)],
        out_specs=pl.BlockSpec(memory_space=pl.ANY),
        out_shape=jax.ShapeDtypeStruct(rows_init.shape, F32),
        scratch_shapes=[pltpu.SemaphoreType.DMA],
        input_output_aliases={2: 0},
        compiler_params=_cparams(1),
        name="moe_dispatch",
    )(dest3, h, rows_init)


def _expert_body(be_ref, nused_ref, x_ref, w1_ref, w3_ref, w2_ref, o_ref):
    del be_ref

    @pl.when(pl.program_id(0) < nused_ref[0])
    def _():
        xb = x_ref[...].astype(BF16)
        h1 = _d(xb, w1_ref[0].astype(BF16))
        h3 = _d(xb, w3_ref[0].astype(BF16))
        hid = _silu(h1) * h3
        o_ref[...] = _d(hid.astype(BF16), w2_ref[0].astype(BF16))

    @pl.when(pl.program_id(0) >= nused_ref[0])
    def _():
        o_ref[...] = jnp.zeros_like(o_ref)


def _experts(block_e, nused, x_rows, w1, w3, w2):
    r, d = x_rows.shape
    nb = r // MOE_BLOCK
    hid = w1.shape[-1]
    grid_spec = pltpu.PrefetchScalarGridSpec(
        num_scalar_prefetch=2,
        grid=(nb,),
        in_specs=[pl.BlockSpec((MOE_BLOCK, d), lambda i, be, nu: (i, 0)),
                  pl.BlockSpec((1, d, hid), lambda i, be, nu: (be[i], 0, 0)),
                  pl.BlockSpec((1, d, hid), lambda i, be, nu: (be[i], 0, 0)),
                  pl.BlockSpec((1, hid, d), lambda i, be, nu: (be[i], 0, 0))],
        out_specs=pl.BlockSpec((MOE_BLOCK, d), lambda i, be, nu: (i, 0)),
    )
    return pl.pallas_call(
        _expert_body,
        grid_spec=grid_spec,
        out_shape=jax.ShapeDtypeStruct((r, d), F32),
        compiler_params=_cparams(1),
        name="moe_experts",
    )(block_e, nused, x_rows, w1, w3, w2)


def _combine_body(dest_ref, gate_ref, h_ref, ln_ref, rows_ref, o_ref, buf_ref, sem):
    tm = h_ref.shape[0]

    def copies(i):
        return [_row_copy(rows_ref, dest_ref[0, 0, 2 * i + s], buf_ref.at[s], i, sem) for s in range(2)]

    def issue(i, carry):
        for cp in copies(i):
            cp.start()
        return carry

    lax.fori_loop(0, tm, issue, 0)

    def drain(i, carry):
        for cp in copies(i):
            cp.wait()
        return carry

    lax.fori_loop(0, tm, drain, 0)
    gate = gate_ref[...]
    y = gate[:, 4:5] * buf_ref[0] + gate[:, 5:6] * buf_ref[1]
    o_ref[...] = _layer_norm(DN_ALPHA * h_ref[...] + y, ln_ref[0:1, :], ln_ref[1:2, :])


def _combine(dest3, route, h, ln, y_rows, tm):
    t, d = h.shape
    return pl.pallas_call(
        _combine_body,
        grid=(t // tm,),
        in_specs=[pl.BlockSpec((1, 1, 2 * tm), lambda i: (i, 0, 0), memory_space=pltpu.SMEM),
                  pl.BlockSpec((tm, 8), lambda i: (i, 0)),
                  pl.BlockSpec((tm, d), lambda i: (i, 0)),
                  pl.BlockSpec((8, d), lambda i: (0, 0)),
                  pl.BlockSpec(memory_space=pl.ANY)],
        out_specs=pl.BlockSpec((tm, d), lambda i: (i, 0)),
        out_shape=jax.ShapeDtypeStruct((t, d), F32),
        scratch_shapes=[pltpu.VMEM((2, tm, d), F32), pltpu.SemaphoreType.DMA],
        compiler_params=_cparams(1),
        name="moe_combine_ln",
    )(dest3, route, h, ln, y_rows)


def _moe(h, route, counts, w1, w3, w2, ln, tm):
    t, d = h.shape
    a = 2 * t
    expert = route[:, 0:2].astype(jnp.int32)
    rank = route[:, 2:4].astype(jnp.int32)
    cnt = counts[0, :N_EXPERTS].astype(jnp.int32)
    padded = (cnt + MOE_BLOCK - 1) // MOE_BLOCK * MOE_BLOCK
    pend = jnp.cumsum(padded)
    pstart = pend - padded
    dest = pstart[expert] + rank
    n_blocks = -(-a // MOE_BLOCK) + N_EXPERTS
    block_row = jnp.arange(n_blocks, dtype=jnp.int32) * MOE_BLOCK
    block_e = jnp.minimum(jnp.sum((pend[None, :] <= block_row[:, None]).astype(jnp.int32), axis=1),
                          N_EXPERTS - 1)
    nused = (pend[-1:] // MOE_BLOCK).astype(jnp.int32)
    dest3 = dest.reshape(t // tm, 1, 2 * tm)
    x_rows = _dispatch(dest3, h, jnp.zeros((n_blocks * MOE_BLOCK, d), F32), tm)
    y_rows = _experts(block_e, nused, x_rows, w1, w3, w2)
    return _combine(dest3, route, h, ln, y_rows, tm)


def _pad_rows(x, rows):
    return jnp.zeros((rows,) + x.shape[1:], x.dtype).at[:x.shape[0]].set(x)


def _route_weights(wg, bg, we, be):
    d = wg.shape[0]
    wr = jnp.zeros((d, LANES), F32)
    wr = wr.at[:, :MOE_GROUPS].set(wg)
    wr = wr.at[:, MOE_GROUPS:MOE_GROUPS + N_EXPERTS].set(jnp.transpose(we, (1, 0, 2)).reshape(d, N_EXPERTS))
    br = jnp.zeros((8, LANES), F32)
    br = br.at[0, :MOE_GROUPS].set(bg)
    br = br.at[0, MOE_GROUPS:MOE_GROUPS + N_EXPERTS].set(be.reshape(N_EXPERTS))
    return wr, br


def kernel(x, ab_w_in, rw_mu, rw_w0, rw_w2, rw_a0, rw_a2, rw_g2, rw_k_k, rw_k_a, rw_r_k, rw_gn_g, rw_gn_b, gla_gk_w2, gla_gk_b, gla_norm_g, ab_w_out, cd_w_in, s5_a_re, s5_a_im, s5_log_dt, s5_b_re, s5_b_im, s5_c_re, s5_c_im, s5_d, s5_glu_w, s5_glu_b, hg_lb, hg_norm_g, cd_w_out, ln1_g, ln1_b, moe_wg, moe_bg, moe_we, moe_be, moe_w1, moe_w3, moe_w2, ln2_g, ln2_b):
    batch, seq, d = x.shape
    t = batch * seq
    assert d == D_MODEL and seq % CHUNK == 0
    rows_per_seq = seq // S5_CHUNK
    assert rows_per_seq & (rows_per_seq - 1) == 0, "S5 chunk scan assumes a power-of-two chunk count"
    tm = min(256, t)
    assert t % tm == 0
    xt = x.reshape(t, d)
    ln1 = [_pad_rows(jnp.stack([ln1_g[l], ln1_b[l]]), 8) for l in range(DEPTH)]
    ln2 = [_pad_rows(jnp.stack([ln2_g[l], ln2_b[l]]), 8) for l in range(DEPTH)]

    j = 0
    w = ab_w_in[j]
    r_, wl_, k_, v_, al_, gl_ = 0, 512, 576, 1088, 1600, 1664
    gq, gk, gv, glow, ggate = 1792, 2048, 2304, 2816, 2832
    w0cols = jnp.concatenate([
        w[:, r_:r_ + 512], w[:, k_:k_ + 512], w[:, v_:v_ + 512], w[:, gv:gv + 512], w[:, ggate:ggate + 512],
        w[:, wl_:wl_ + 64], w[:, al_:al_ + 64], w[:, gl_:gl_ + 128], w[:, gq:gq + 256], w[:, gk:gk + 256],
        w[:, glow:glow + 16], jnp.zeros((d, 112), F32)], axis=1).astype(BF16)
    p0 = _matmul(xt, w0cols, tm)

    mu = rw_mu[j]
    vec = _pad_rows(jnp.stack([mu[r_:r_ + 512], mu[k_:k_ + 512], mu[v_:v_ + 512], rw_w0[j], rw_a0[j],
                               rw_k_k[j], rw_k_a[j], rw_r_k[j].reshape(-1), rw_gn_g[j], rw_gn_b[j]]), 16)
    mulo = _pad_rows(jnp.concatenate([mu[wl_:wl_ + 64], mu[al_:al_ + 64], mu[gl_:gl_ + 128]])[None], 8)
    w2p = _pad_rows(rw_w2[j], 128)
    a2p = jnp.zeros((128, 512), F32).at[64:].set(rw_a2[j])
    p0 = p0.reshape(batch, seq, -1)
    y_rw = _rwkv(p0, vec, mulo, w2p, a2p, rw_g2[j]).reshape(t, RW_WIDTH)
    y_gla = _gla(p0, _pad_rows(gla_gk_w2[j], 128), _pad_rows(gla_gk_b[j][None], 8),
                 _pad_rows(gla_norm_g[j][None], 8)).reshape(t, 512)

    wr, br = _route_weights(moe_wg[0], moe_bg[0], moe_we[0], moe_be[0])
    h, route, counts = _proj_route(y_rw, y_gla, xt, ab_w_out[j].astype(BF16), ln1[0], wr, br, tm)
    h = _moe(h, route, counts, moe_w1[0], moe_w3[0], moe_w2[0], ln2[0], tm)

    p1 = _matmul(h, cd_w_in[j].astype(BF16), tm)
    lb_sm = jax.nn.softmax(hg_lb.astype(F32), axis=0)
    lower = (jnp.cumsum(lb_sm, axis=0) - lb_sm[0])[1]
    y_hg = _hgrn(p1.reshape(batch, seq, -1), _pad_rows(lower[None], 8),
                 _pad_rows(hg_norm_g[j][None], 8)).reshape(t, 512)

    toep, bmat, emat, pw = _s5_tables(s5_a_re[j], s5_a_im[j], s5_log_dt[j], s5_b_re[j], s5_b_im[j],
                                      s5_c_re[j], s5_c_im[j], rows_per_seq)
    nrows = t // S5_CHUNK
    uc = p1[:, :512].reshape(nrows, S5_CHUNK, S5_GROUPS, S5_GROUP)
    uc = jnp.transpose(uc, (2, 0, 1, 3)).reshape(S5_GROUPS, nrows, S5_CHUNK * S5_GROUP).astype(BF16)
    yc = _s5_scan(uc, toep, bmat, emat, pw, rows_per_seq)
    y_ssm = jnp.transpose(yc.reshape(S5_GROUPS, nrows, S5_CHUNK, S5_GROUP), (1, 2, 0, 3)).reshape(t, 512)
    y_s5 = _s5_post(y_ssm, p1, _pad_rows(jnp.stack([s5_d[j], s5_glu_b[j]]), 8),
                    s5_glu_w[j].astype(BF16), tm)

    wr, br = _route_weights(moe_wg[1], moe_bg[1], moe_we[1], moe_be[1])
    h2, route, counts = _proj_route(y_s5, y_hg, h, cd_w_out[j].astype(BF16), ln1[1], wr, br, tm)
    out = _moe(h2, route, counts, moe_w1[1], moe_w3[1], moe_w2[1], ln2[1], tm)
    return out.reshape(batch, seq, d)
```

```python
import functools
import math

import numpy as np
import jax
import jax.numpy as jnp
from jax import lax
from jax.experimental import pallas as pl
from jax.experimental.pallas import tpu as pltpu

F32 = jnp.float32
BF16 = jnp.bfloat16

D_MODEL = 1024
DEPTH = 2
RW_HEAD = 64
RW_WIDTH = 512
RW_GN_EPS = 64e-5
GLA_HEADS = 4
GLA_DK = 64
GLA_DV = 128
GLA_GATE_TAU = 16.0
S5_GROUP = 16
S5_GROUPS = 32
S5_STATE = 64
HG_HEADS = 4
HG_DK = 128
CHUNK = 64
S5_CHUNK = 16
S5_SCAN_ROWS = 32
NORM_EPS = 1e-5
MOE_GROUPS = 4
MOE_PER_GROUP = 8
N_EXPERTS = 32
EXPERT_HIDDEN = 512
MOE_BLOCK = 128
COMBINE_TILE = 128
DN_ALPHA = (2.0 * DEPTH) ** 0.25

LANES = 128
VMEM_LIMIT = 56 * 1024 * 1024


def _cparams(n_axes=1):
    return pltpu.CompilerParams(dimension_semantics=("arbitrary",) * n_axes,
                                vmem_limit_bytes=VMEM_LIMIT)


def _d(a, b):
    return jnp.dot(a, b, preferred_element_type=F32)


def _d_nt(a, b):
    return lax.dot_general(a, b, (((1,), (1,)), ((), ())), preferred_element_type=F32)


def _d_tn(a, b):
    return lax.dot_general(a, b, (((0,), (0,)), ((), ())), preferred_element_type=F32)


def _split(a):
    hi = a.astype(BF16)
    lo = (a - hi.astype(F32)).astype(BF16)
    return hi, lo


def _split3(a):
    hi = a.astype(BF16)
    r1 = a - hi.astype(F32)
    mid = r1.astype(BF16)
    lo = (r1 - mid.astype(F32)).astype(BF16)
    return hi, mid, lo


def _bdot(a, b):
    return _d(a.astype(BF16), b.astype(BF16))


def _bdot_nt(a, b):
    return _d_nt(a.astype(BF16), b.astype(BF16))


def _bdot_tn(a, b):
    return _d_tn(a.astype(BF16), b.astype(BF16))


def _hdot_with(d, a, b):
    ah, al = _split(a)
    bh, bl = _split(b)
    return d(ah, bh) + (d(ah, bl) + d(al, bh))


def _hdot(a, b):
    return _hdot_with(_d, a, b)


def _hdot_nt(a, b):
    return _hdot_with(_d_nt, a, b)


def _hdot_tn(a, b):
    return _hdot_with(_d_tn, a, b)


def _xdot_l(a, e):
    ah, am, al = _split3(a)
    return _d(ah, e) + (_d(am, e) + _d(al, e))


def _xdot_r(e, a):
    ah, am, al = _split3(a)
    return _d(e, ah) + (_d(e, am) + _d(e, al))


def _iota(shape, dim):
    return lax.broadcasted_iota(jnp.int32, shape, dim)


def _softplus(x):
    return jnp.maximum(x, 0.0) + jnp.log1p(jnp.exp(-jnp.abs(x)))


def _sigmoid(x):
    return 1.0 / (1.0 + jnp.exp(-x))


def _silu(x):
    return x * _sigmoid(x)


def _tril_incl(n):
    return jnp.where(_iota((n, n), 0) >= _iota((n, n), 1), 1.0, 0.0).astype(BF16)


def _cumsum_rows(g):
    return _xdot_r(_tril_incl(g.shape[0]), g)


def _shift_mix(x, prev_ref, b, mu):
    c = x.shape[0]
    rolled = pltpu.roll(x, 1, 0)
    prev = jnp.where(_iota(x.shape, 0) == 0, jnp.broadcast_to(prev_ref[b, 0:1, :], x.shape), rolled)
    prev_ref[b, 0:1, :] = x[c - 1:c, :]
    return x + mu * (prev - x)


def _layer_norm(x, g, b):
    mu = jnp.mean(x, axis=-1, keepdims=True)
    xc = x - mu
    var = jnp.mean(xc * xc, axis=-1, keepdims=True)
    return xc * lax.rsqrt(var + NORM_EPS) * g + b


def _mm_body(x_ref, w_ref, o_ref):
    o_ref[...] = _d(x_ref[...].astype(BF16), w_ref[...])


def _matmul(x, w_bf16, tm):
    m, k = x.shape
    n = w_bf16.shape[1]
    return pl.pallas_call(
        _mm_body,
        grid=(m // tm,),
        in_specs=[pl.BlockSpec((tm, k), lambda i: (i, 0)),
                  pl.BlockSpec((k, n), lambda i: (0, 0))],
        out_specs=pl.BlockSpec((tm, n), lambda i: (i, 0)),
        out_shape=jax.ShapeDtypeStruct((m, n), F32),
        compiler_params=_cparams(1),
        name="in_proj",
    )(x, w_bf16)


_RV_MU_R, _RV_MU_K, _RV_MU_V, _RV_W0, _RV_A0, _RV_KK, _RV_KA, _RV_RK, _RV_GNG, _RV_GNB = range(10)


def _rwkv_body(r_ref, k_ref, v_ref, lo_ref, vec_ref, mulo_ref, w2_ref, a2_ref, g2_ref, ones_ref,
               o_ref, pr_ref, pk_ref, pv_ref, plo_ref, st_ref):
    nb, c = r_ref.shape[0], r_ref.shape[1]
    npair = RW_WIDTH // LANES

    @pl.when(pl.program_id(0) == 0)
    def _():
        pr_ref[...] = jnp.zeros_like(pr_ref)
        pk_ref[...] = jnp.zeros_like(pk_ref)
        pv_ref[...] = jnp.zeros_like(pv_ref)
        plo_ref[...] = jnp.zeros_like(plo_ref)
        st_ref[...] = jnp.zeros_like(st_ref)

    def vec(i):
        return vec_ref[i:i + 1, :]

    ones_bd = ones_ref[...]
    lane = _iota((c, LANES), 1)
    m1 = lane < RW_HEAD
    row2 = _iota((2 * c, 4 * c), 0)
    col2 = _iota((2 * c, 4 * c), 1) & (c - 1)
    tri = ((row2 < c) & (row2 > col2)) | ((row2 >= c) & ((row2 - c) >= col2))
    eye2 = jnp.where(_iota((2 * c, 2 * c), 0) == _iota((2 * c, 2 * c), 1), 1.0, 0.0)
    bd_p = (_iota((LANES, LANES), 0) >> 6) == (_iota((LANES, LANES), 1) >> 6)

    def halves(x):
        return jnp.concatenate([jnp.where(m1, x, 0.0), jnp.where(m1, 0.0, x)], axis=0)

    prep = []
    for b in range(nb):
        xr = _shift_mix(r_ref[b], pr_ref, b, vec(_RV_MU_R))
        xk = _shift_mix(k_ref[b], pk_ref, b, vec(_RV_MU_K))
        xv = _shift_mix(v_ref[b], pv_ref, b, vec(_RV_MU_V))
        xlo = _shift_mix(lo_ref[b], plo_ref, b, mulo_ref[0:1, :])
        lo_a = xlo[:, :LANES]
        lo_g = xlo[:, LANES:]
        w = -_softplus(-(vec(_RV_W0) + _bdot(jnp.tanh(lo_a), w2_ref[...]))) - 0.5
        g = -jnp.exp(w)
        a = _sigmoid(vec(_RV_A0) + _bdot(lo_a, a2_ref[...]))
        gate = _bdot(_sigmoid(lo_g), g2_ref[...])
        kk = xk * vec(_RV_KK)
        kk = kk / jnp.maximum(jnp.sqrt(_xdot_l(kk * kk, ones_bd)), 1e-12)
        k2 = xk * (1.0 + (a - 1.0) * vec(_RV_KA))
        gc = _cumsum_rows(g)
        g_last = gc[c - 1:c, :]
        e_neg = jnp.exp(-gc)
        e_end = jnp.exp(g_last - gc)
        prep.append(dict(
            xr=xr, xv=xv, k2=k2, gate=gate,
            at=-kk * jnp.exp(gc - g),
            bt=(kk * a) * e_neg, kt=k2 * e_neg, rt=xr * jnp.exp(gc),
            bh=(kk * a) * e_end, kh=k2 * e_end,
            gam=jnp.exp(g_last)))

    chains = [(b, p) for b in range(nb) for p in range(npair)]

    def part(ch, name):
        b, p = ch
        return prep[b][name][:, p * LANES:(p + 1) * LANES]

    a_ak, a_row, vv, ak, pinv = {}, {}, {}, {}, {}
    for ch in chains:
        lhs = jnp.concatenate([part(ch, 'at'), part(ch, 'rt')], axis=0)
        rhs = jnp.concatenate([halves(part(ch, 'bt')), halves(part(ch, 'kt'))], axis=0)
        aa = jnp.where(tri, _bdot_nt(lhs, rhs), 0.0)
        a_ak[ch] = aa[:c, 2 * c:].astype(BF16)
        a_row[ch] = aa[c:, :].astype(BF16)
        abd = halves(aa[:c, :2 * c])
        pinv[ch] = eye2 + abd
        ak[ch] = abd
        vv[ch] = halves(part(ch, 'xv')).astype(BF16)
    nlev = int(math.log2(c))
    for lev in range(nlev):
        for ch in chains:
            akb = ak[ch].astype(BF16)
            if lev == 0:
                ak[ch] = _d(akb, akb)
            elif lev < nlev - 1:
                out = _d(akb, jnp.concatenate([akb, pinv[ch].astype(BF16)], axis=1))
                ak[ch] = out[:, :2 * c]
                pinv[ch] = pinv[ch] + out[:, 2 * c:]
            else:
                pinv[ch] = pinv[ch] + _d(akb, pinv[ch].astype(BF16))
    x2 = {ch: _d(a_ak[ch], vv[ch]) for ch in chains}

    sts = {ch: st_ref[ch[0] * npair + ch[1]] for ch in chains}
    xs = {ch: _d_nt(jnp.concatenate([part(ch, 'at'), part(ch, 'rt')], axis=0).astype(BF16),
                    sts[ch].astype(BF16)) for ch in chains}
    us = {}
    for ch in chains:
        u2 = _d(pinv[ch].astype(BF16), halves(xs[ch][:c] + x2[ch]).astype(BF16))
        us[ch] = u2[:c] + u2[c:]
    ys = {}
    for ch in chains:
        b, p = ch
        u = us[ch]
        ys[ch] = xs[ch][c:] + _d(a_row[ch], jnp.concatenate([halves(u).astype(BF16), vv[ch]], axis=0))
        upd = _d_tn(jnp.concatenate([u, part(ch, 'xv')], axis=0).astype(BF16),
                    jnp.concatenate([part(ch, 'bh'), part(ch, 'kh')], axis=0).astype(BF16))
        st_ref[b * npair + p] = sts[ch] * part(ch, 'gam') + jnp.where(bd_p, upd, 0.0)

    inv_n = 1.0 / RW_HEAD
    for b in range(nb):
        y = jnp.concatenate([ys[(b, p)] for p in range(npair)], axis=1)
        mean = _xdot_l(y, ones_bd) * inv_n
        yc = y - mean
        var = _xdot_l(yc * yc, ones_bd) * inv_n
        yn = yc * lax.rsqrt(var + RW_GN_EPS) * vec(_RV_GNG) + vec(_RV_GNB)
        bonus = _xdot_l(prep[b]['xr'] * prep[b]['k2'] * vec(_RV_RK), ones_bd) * prep[b]['xv']
        o_ref[b] = (yn + bonus) * prep[b]['gate']


def _rwkv(p0, vec, mulo, w2p, a2p, g2):
    batch, seq, _ = p0.shape
    nc = seq // CHUNK
    c = CHUNK
    ones_bd = jnp.asarray(np.kron(np.eye(RW_WIDTH // RW_HEAD), np.ones((RW_HEAD, RW_HEAD))), BF16)

    def col(j, width):
        return pl.BlockSpec((batch, c, width), lambda i: (0, i, j))

    def full(shape):
        return pl.BlockSpec(shape, lambda i: (0,) * len(shape))

    return pl.pallas_call(
        _rwkv_body,
        grid=(nc,),
        in_specs=[col(0, 512), col(1, 512), col(2, 512), col(10, 256),
                  full((16, 512)), full((8, 256)), full((128, 512)), full((128, 512)),
                  full((128, 512)), full((512, 512))],
        out_specs=pl.BlockSpec((batch, c, 512), lambda i: (0, i, 0)),
        out_shape=jax.ShapeDtypeStruct((batch, seq, RW_WIDTH), F32),
        scratch_shapes=[pltpu.VMEM((batch, 8, 512), F32), pltpu.VMEM((batch, 8, 512), F32),
                        pltpu.VMEM((batch, 8, 512), F32), pltpu.VMEM((batch, 8, 256), F32),
                        pltpu.VMEM((batch * RW_WIDTH // LANES, LANES, LANES), F32)],
        compiler_params=_cparams(1),
        name="rwkv7",
    )(p0, p0, p0, p0, vec, mulo, w2p, a2p, g2, ones_bd)


def _gla_core(qs, ks, vs, gs, st_ref, heads_per_block):
    nb = len(qs)
    c = qs[0].shape[0]
    hpb = heads_per_block
    nblk = qs[0].shape[1] // LANES
    dk_shift = int(math.log2(LANES // hpb))
    row = _iota(qs[0].shape, 0)

    prep = []
    for q, k, g in zip(qs, ks, gs):
        b = _cumsum_rows(g)

        def brow(i, b=b):
            return jnp.broadcast_to(b[i:i + 1, :], b.shape)

        b15, b31, b47, blast = brow(15), brow(31), brow(47), brow(c - 1)
        ref_b = jnp.where(row < 32, b15, b47)
        ref_d = jnp.where(row < 16, 0.0, jnp.where(row < 32, b15, jnp.where(row < 48, b31, b47)))
        prep.append(dict(
            q_a=q * jnp.exp(jnp.minimum(b - b31, 0.0)), k_a=k * jnp.exp(jnp.minimum(b31 - b, 0.0)),
            q_b=q * jnp.exp(jnp.minimum(b - ref_b, 0.0)), k_b=k * jnp.exp(jnp.minimum(ref_b - b, 0.0)),
            q_d=q * jnp.exp(b - ref_d), k_d=k * jnp.exp(ref_d - b),
            q_i=q * jnp.exp(b), k_s=k * jnp.exp(blast - b), gam=jnp.exp(b[c - 1:c, :])))

    ri = _iota((hpb * c, c), 0) & (c - 1)
    ci = _iota((hpb * c, c), 1)
    mask_a = (ri >= 32) & (ci < 32)
    mask_b = ((ri >> 5) == (ci >> 5)) & (((ri >> 4) & 1) == 1) & (((ci >> 4) & 1) == 0)
    mask_d = ((ri >> 4) == (ci >> 4)) & (ri >= ci)
    lane = _iota((c, LANES), 1)
    bd = (_iota((hpb * LANES, LANES), 0) >> 7) == (_iota((hpb * LANES, LANES), 1) >> dk_shift)

    def heads_rows(x):
        if hpb == 1:
            return x
        return jnp.concatenate([jnp.where((lane >> dk_shift) == h, x, 0.0) for h in range(hpb)], axis=0)

    chains = [(bi, blk) for bi in range(nb) for blk in range(nblk)]

    def part(ch, name):
        bi, blk = ch
        return prep[bi][name][:, blk * LANES:(blk + 1) * LANES]

    v_p = {ch: vs[ch[0]][:, ch[1] * hpb * LANES:(ch[1] + 1) * hpb * LANES].astype(BF16) for ch in chains}
    probs = {}
    for ch in chains:
        s_a = _bdot_nt(heads_rows(part(ch, 'q_a')), part(ch, 'k_a'))
        s_b = _bdot_nt(heads_rows(part(ch, 'q_b')), part(ch, 'k_b'))
        s_d = _bdot_nt(heads_rows(part(ch, 'q_d')), part(ch, 'k_d'))
        probs[ch] = (jnp.where(mask_a, s_a, 0.0) + jnp.where(mask_b, s_b, 0.0)
                     + jnp.where(mask_d, s_d, 0.0)).astype(BF16)
    outs = {}
    for ch in chains:
        pv = _d(probs[ch], v_p[ch])
        o = pv[:c]
        for h in range(1, hpb):
            o = jnp.where((_iota(o.shape, 1) >> 7) == h, pv[h * c:(h + 1) * c], o)
        si = ch[0] * nblk + ch[1]
        sp = st_ref[si]
        outs[ch] = o + _bdot_nt(part(ch, 'q_i'), sp)
        upd = _d_tn(v_p[ch], part(ch, 'k_s').astype(BF16))
        st_ref[si] = sp * part(ch, 'gam') + jnp.where(bd, upd, 0.0)
    return [jnp.concatenate([outs[(bi, blk)] for blk in range(nblk)], axis=1) for bi in range(nb)]


def _gated_rmsnorm(o, gate, norm_g):
    nh = o.shape[1] // LANES
    outs = []
    for h in range(nh):
        sl = slice(h * LANES, (h + 1) * LANES)
        oh = o[:, sl]
        ms = jnp.mean(oh * oh, axis=-1, keepdims=True)
        outs.append(oh * lax.rsqrt(ms + NORM_EPS) * norm_g * _silu(gate[:, sl]))
    return jnp.concatenate(outs, axis=1)


def _gla_body(q_ref, k_ref, v_ref, gate_ref, gk_ref, w2_ref, vec_ref, ng_ref, o_ref, st_ref):
    nb = q_ref.shape[0]

    @pl.when(pl.program_id(0) == 0)
    def _():
        st_ref[...] = jnp.zeros_like(st_ref)

    gs = [-_softplus(-(_hdot(gk_ref[b], w2_ref[...]) + vec_ref[0:1, :])) * (1.0 / GLA_GATE_TAU)
          for b in range(nb)]
    qs = [q_ref[b] * (GLA_DK ** -0.5) for b in range(nb)]
    os_ = _gla_core(qs, [k_ref[b] for b in range(nb)], [v_ref[b] for b in range(nb)], gs, st_ref, 2)
    for b in range(nb):
        o_ref[b] = _gated_rmsnorm(os_[b], gate_ref[b], ng_ref[0:1, :])


def _seq_specs(batch, c):
    def col(j, width):
        return pl.BlockSpec((batch, c, width), lambda i: (0, i, j))

    def full(shape):
        return pl.BlockSpec(shape, lambda i: (0,) * len(shape))

    return col, full


def _gla(p0, gk_w2p, gk_b, norm_g):
    batch, seq, _ = p0.shape
    c = CHUNK
    col, full = _seq_specs(batch, c)
    return pl.pallas_call(
        _gla_body,
        grid=(seq // c,),
        in_specs=[col(11, 256), col(12, 256), col(3, 512), col(4, 512), col(26, 128),
                  full((128, 256)), full((8, 256)), full((8, 128))],
        out_specs=pl.BlockSpec((batch, c, 512), lambda i: (0, i, 0)),
        out_shape=jax.ShapeDtypeStruct((batch, seq, 512), F32),
        scratch_shapes=[pltpu.VMEM((batch * 2, 2 * LANES, LANES), F32)],
        compiler_params=_cparams(1),
        name="gla",
    )(p0, p0, p0, p0, p0, gk_w2p, gk_b, norm_g)


def _hgrn_body(q_ref, f_ref, i_ref, gate_ref, lb_ref, ng_ref, o_ref, st_ref):
    nb = q_ref.shape[0]

    @pl.when(pl.program_id(0) == 0)
    def _():
        st_ref[...] = jnp.zeros_like(st_ref)

    lb = lb_ref[0:1, :]
    qs, ks, gs = [], [], []
    for b in range(nb):
        f = f_ref[b]
        gs.append(jnp.log(lb + (1.0 - lb) * _sigmoid(f)))
        ks.append((1.0 - lb) * _sigmoid(-f))
        qs.append(_silu(q_ref[b]))
    os_ = _gla_core(qs, ks, [i_ref[b] for b in range(nb)], gs, st_ref, 1)
    for b in range(nb):
        o_ref[b] = _gated_rmsnorm(os_[b], gate_ref[b], ng_ref[0:1, :])


def _hgrn(p1, lb, norm_g):
    batch, seq, _ = p1.shape
    c = CHUNK
    col, full = _seq_specs(batch, c)
    return pl.pallas_call(
        _hgrn_body,
        grid=(seq // c,),
        in_specs=[col(1, 512), col(2, 512), col(3, 512), col(4, 512), full((8, 512)), full((8, 128))],
        out_specs=pl.BlockSpec((batch, c, 512), lambda i: (0, i, 0)),
        out_shape=jax.ShapeDtypeStruct((batch, seq, 512), F32),
        scratch_shapes=[pltpu.VMEM((batch * HG_HEADS, LANES, LANES), F32)],
        compiler_params=_cparams(1),
        name="hgrn2",
    )(p1, p1, p1, p1, lb, norm_g)


def _s5_body(u_ref, toep_ref, bm_ref, e_ref, pw_ref, o_ref, *, rows_per_seq):
    u = u_ref[0]
    rows = u.shape[0]
    s = _d(u, bm_ref[0])
    rin = _iota((rows, LANES), 0) & (rows_per_seq - 1)
    pw = pw_ref[0]
    h = s
    nlev = int(math.log2(rows_per_seq))
    for lev in range(nlev):
        sh = 1 << lev
        hs = jnp.where(rin >= sh, pltpu.roll(h, sh, 0), 0.0)
        hs_sw = pltpu.roll(hs, S5_STATE, 1)
        h = h + pw[2 * lev:2 * lev + 1, :] * hs + pw[2 * lev + 1:2 * lev + 2, :] * hs_sw
    hprev = jnp.where(rin >= 1, pltpu.roll(h, 1, 0), 0.0)
    o_ref[0] = _d(u, toep_ref[0]) + _hdot(hprev, e_ref[0])


def _s5_scan(uc, toep, bmat, emat, pw, rows_per_seq):
    ng, rows, width = uc.shape
    return pl.pallas_call(
        functools.partial(_s5_body, rows_per_seq=rows_per_seq),
        grid=(ng,),
        in_specs=[pl.BlockSpec((1, rows, width), lambda g: (g, 0, 0)),
                  pl.BlockSpec((1, width, width), lambda g: (g, 0, 0)),
                  pl.BlockSpec((1, width, LANES), lambda g: (g, 0, 0)),
                  pl.BlockSpec((1, LANES, width), lambda g: (g, 0, 0)),
                  pl.BlockSpec((1, S5_SCAN_ROWS, LANES), lambda g: (g, 0, 0))],
        out_specs=pl.BlockSpec((1, rows, width), lambda g: (g, 0, 0)),
        out_shape=jax.ShapeDtypeStruct((ng, rows, width), F32),
        compiler_params=_cparams(1),
        name="s5_scan",
    )(uc, toep, bmat, emat, pw)


def _s5_post_body(y_ref, u_ref, vec_ref, w_ref, o_ref):
    y = y_ref[...] + vec_ref[0:1, :] * u_ref[...]
    y = 0.5 * y * (1.0 + jnp.tanh(math.sqrt(2.0 / math.pi) * (y + 0.044715 * (y * y * y))))
    o_ref[...] = y * _sigmoid(_bdot(y, w_ref[...]) + vec_ref[1:2, :])


def _s5_post(y_ssm, p1, vec, glu_w, tm):
    t = y_ssm.shape[0]
    return pl.pallas_call(
        _s5_post_body,
        grid=(t // tm,),
        in_specs=[pl.BlockSpec((tm, 512), lambda i: (i, 0)),
                  pl.BlockSpec((tm, 512), lambda i: (i, 0)),
                  pl.BlockSpec((8, 512), lambda i: (0, 0)),
                  pl.BlockSpec((512, 512), lambda i: (0, 0))],
        out_specs=pl.BlockSpec((tm, 512), lambda i: (i, 0)),
        out_shape=jax.ShapeDtypeStruct((t, 512), F32),
        compiler_params=_cparams(1),
        name="s5_post",
    )(y_ssm, p1, vec, glu_w)


def _s5_tables(a_re, a_im, log_dt, b_re, b_im, c_re, c_im, rows_per_seq):
    c = S5_CHUNK
    lam_re = jnp.minimum(a_re, -1e-4)
    lam_im = a_im
    dt = jnp.exp(log_dt)[:, None]
    mag = jnp.exp(lam_re * dt)
    abar_re = mag * jnp.cos(lam_im * dt)
    abar_im = mag * jnp.sin(lam_im * dt)
    den = lam_re * lam_re + lam_im * lam_im
    num_re = abar_re - 1.0
    z_re = (num_re * lam_re + abar_im * lam_im) / den
    z_im = (abar_im * lam_re - num_re * lam_im) / den

    def power(n):
        n = jnp.asarray(n, F32)[..., None, None]
        m = jnp.exp(n * (lam_re * dt))
        return m * jnp.cos(n * (lam_im * dt)), m * jnp.sin(n * (lam_im * dt))

    def cmul(ar, ai, br, bi):
        return ar * br - ai * bi, ar * bi + ai * br

    tau = jnp.arange(c)
    p_re, p_im = power(tau)
    zb_re, zb_im = cmul(z_re[..., None], z_im[..., None], b_re, b_im)
    cp_re, cp_im = cmul(c_re[None], c_im[None], p_re[:, :, None, :], p_im[:, :, None, :])
    hi = lax.Precision.HIGHEST
    kern = (jnp.einsum('tgon,gni->gtoi', cp_re, zb_re, precision=hi)
            - jnp.einsum('tgon,gni->gtoi', cp_im, zb_im, precision=hi))
    tt = jnp.arange(c)[None, :] - jnp.arange(c)[:, None]
    kt = jnp.where((tt >= 0)[None, :, :, None, None], kern[:, jnp.clip(tt, 0, c - 1)], 0.0)
    toep = jnp.transpose(kt, (0, 1, 4, 2, 3)).reshape(S5_GROUPS, c * S5_GROUP, c * S5_GROUP)
    q_re, q_im = power(c - 1 - tau)
    bm_re, bm_im = cmul(q_re[..., None], q_im[..., None], zb_re[None], zb_im[None])
    bmat = jnp.concatenate([jnp.transpose(bm_re, (1, 0, 3, 2)), jnp.transpose(bm_im, (1, 0, 3, 2))],
                           axis=-1).reshape(S5_GROUPS, c * S5_GROUP, 2 * S5_STATE)
    r_re, r_im = power(tau + 1)
    e_re, e_im = cmul(c_re[None], c_im[None], r_re[:, :, None, :], r_im[:, :, None, :])
    emat = jnp.concatenate([jnp.transpose(e_re, (1, 3, 0, 2)), -jnp.transpose(e_im, (1, 3, 0, 2))],
                           axis=1).reshape(S5_GROUPS, 2 * S5_STATE, c * S5_GROUP)
    nlev = int(math.log2(rows_per_seq))
    assert 2 * nlev <= S5_SCAN_ROWS
    s_re, s_im = power(c * (2 ** jnp.arange(nlev)))
    pw = jnp.zeros((S5_GROUPS, S5_SCAN_ROWS, 2 * S5_STATE), F32)
    pw = pw.at[:, 0:2 * nlev:2, :].set(jnp.transpose(jnp.concatenate([s_re, s_re], -1), (1, 0, 2)))
    pw = pw.at[:, 1:2 * nlev:2, :].set(jnp.transpose(jnp.concatenate([-s_im, s_im], -1), (1, 0, 2)))
    return toep.astype(BF16), bmat.astype(BF16), emat, pw


def _proj_route_body(ya_ref, yb_ref, x_ref, w_ref, ln_ref, wr_ref, br_ref,
                     h_ref, route_ref, cnt_ref, carry_ref):
    tm = ya_ref.shape[0]
    half = w_ref.shape[0] // 2

    @pl.when(pl.program_id(0) == 0)
    def _():
        carry_ref[...] = jnp.zeros_like(carry_ref)

    mix = _bdot(ya_ref[...], w_ref[:half, :]) + _bdot(yb_ref[...], w_ref[half:, :])
    h = _layer_norm(DN_ALPHA * x_ref[...] + mix, ln_ref[0:1, :], ln_ref[1:2, :])
    h_ref[...] = h

    logits = _hdot(h, wr_ref[...]) + br_ref[0:1, :]
    lane = _iota((tm, LANES), 1).astype(F32)
    neg = -jnp.inf

    def softmax_masked(mask):
        xm = jnp.where(mask, logits, neg)
        m = jnp.max(xm, axis=-1, keepdims=True)
        e = jnp.exp(xm - m)
        return e / jnp.sum(e, axis=-1, keepdims=True)

    def top1(pm):
        m = jnp.max(pm, axis=-1, keepdims=True)
        idx = jnp.min(jnp.where(pm == m, lane, float(LANES)), axis=-1, keepdims=True)
        return m, idx

    coarse = jnp.where(lane < MOE_GROUPS, softmax_masked(lane < MOE_GROUPS), -1.0)
    p_grp, grp = top1(coarse)
    lo = MOE_GROUPS + MOE_PER_GROUP * grp
    fmask = (lane >= lo) & (lane < lo + MOE_PER_GROUP)
    fine = jnp.where(fmask, softmax_masked(fmask), -1.0)
    p1, j1 = top1(fine)
    p2, j2 = top1(jnp.where(lane == j1, -1.0, fine))
    denom = p1 + p2
    g1 = p_grp * (p1 / denom)
    g2 = p_grp * (p2 / denom)
    e1 = j1 - MOE_GROUPS
    e2 = j2 - MOE_GROUPS

    oh1 = jnp.where(lane == e1, 1.0, 0.0)
    oh2 = jnp.where(lane == e2, 1.0, 0.0)
    cnt = oh1 + oh2
    strict = jnp.where(_iota((tm, tm), 0) > _iota((tm, tm), 1), 1.0, 0.0).astype(BF16)
    before = _d(strict, cnt.astype(BF16)) + carry_ref[0:1, :]
    r1 = jnp.sum(oh1 * before, axis=-1, keepdims=True)
    r2 = jnp.sum(oh2 * before, axis=-1, keepdims=True)
    carry_ref[0:1, :] = carry_ref[0:1, :] + jnp.sum(cnt, axis=0, keepdims=True)
    cnt_ref[...] = carry_ref[...]

    out = jnp.where(lane == 0, e1, 0.0)
    out = jnp.where(lane == 1, e2, out)
    out = jnp.where(lane == 2, r1, out)
    out = jnp.where(lane == 3, r2, out)
    out = jnp.where(lane == 4, g1, out)
    out = jnp.where(lane == 5, g2, out)
    route_ref[...] = out[:, :8]


def _proj_route(ya, yb, resid, w_out, ln, wr, br, tm):
    t, d = resid.shape
    return pl.pallas_call(
        _proj_route_body,
        grid=(t // tm,),
        in_specs=[pl.BlockSpec((tm, 512), lambda i: (i, 0)),
                  pl.BlockSpec((tm, 512), lambda i: (i, 0)),
                  pl.BlockSpec((tm, d), lambda i: (i, 0)),
                  pl.BlockSpec((d, d), lambda i: (0, 0)),
                  pl.BlockSpec((8, d), lambda i: (0, 0)),
                  pl.BlockSpec((d, LANES), lambda i: (0, 0)),
                  pl.BlockSpec((8, LANES), lambda i: (0, 0))],
        out_specs=[pl.BlockSpec((tm, d), lambda i: (i, 0)),
                   pl.BlockSpec((tm, 8), lambda i: (i, 0)),
                   pl.BlockSpec((8, LANES), lambda i: (0, 0))],
        out_shape=[jax.ShapeDtypeStruct((t, d), F32),
                   jax.ShapeDtypeStruct((t, 8), F32),
                   jax.ShapeDtypeStruct((8, LANES), F32)],
        scratch_shapes=[pltpu.VMEM((8, LANES), F32)],
        compiler_params=_cparams(1),
        name="proj_ln_route",
    )(ya, yb, resid, w_out, ln, wr, br)


def _row_copy(src_ref, src_row, dst_ref, dst_row, sem):
    return pltpu.make_async_copy(src_ref.at[pl.ds(src_row, 1)], dst_ref.at[pl.ds(dst_row, 1)], sem)


def _expert_body(be_ref, nused_ref, tok_ref, tok_next_ref, h_ref, w1_ref, w3_ref, w2_ref, o_ref,
                 xbuf_ref, sem):
    del be_ref
    i = pl.program_id(0)
    nused = nused_ref[0]
    rows = xbuf_ref.shape[1]

    def gather(tok_smem, slot):
        return [_row_copy(h_ref, tok_smem[0, 0, r], xbuf_ref.at[slot], r, sem.at[slot]) for r in range(rows)]

    @pl.when((i == 0) & (nused > 0))
    def _():
        for cp in gather(tok_ref, 0):
            cp.start()

    @pl.when(i < nused)
    def _():
        for cp in gather(tok_next_ref, (i + 1) % 2):
            cp.start()
        slot = i % 2
        for cp in gather(tok_ref, slot):
            cp.wait()
        xb = xbuf_ref[slot].astype(BF16)
        h1 = _d(xb, w1_ref[0].astype(BF16))
        h3 = _d(xb, w3_ref[0].astype(BF16))
        hid = _silu(h1) * h3
        o_ref[...] = _d(hid.astype(BF16), w2_ref[0].astype(BF16))

    @pl.when(i + 1 == nused)
    def _():
        for cp in gather(tok_next_ref, (i + 1) % 2):
            cp.wait()

    @pl.when(i >= nused)
    def _():
        o_ref[...] = jnp.zeros_like(o_ref)


def _experts(block_e, nused, row_tok, h, w1, w3, w2):
    d = h.shape[1]
    nb = row_tok.shape[0]
    hid = w1.shape[-1]
    grid_spec = pltpu.PrefetchScalarGridSpec(
        num_scalar_prefetch=2,
        grid=(nb,),
        in_specs=[pl.BlockSpec((1, 1, MOE_BLOCK), lambda i, be, nu: (i, 0, 0), memory_space=pltpu.SMEM),
                  pl.BlockSpec((1, 1, MOE_BLOCK), lambda i, be, nu: (jnp.minimum(i + 1, nb - 1), 0, 0),
                               memory_space=pltpu.SMEM),
                  pl.BlockSpec(memory_space=pl.ANY),
                  pl.BlockSpec((1, d, hid), lambda i, be, nu: (be[i], 0, 0)),
                  pl.BlockSpec((1, d, hid), lambda i, be, nu: (be[i], 0, 0)),
                  pl.BlockSpec((1, hid, d), lambda i, be, nu: (be[i], 0, 0))],
        out_specs=pl.BlockSpec((MOE_BLOCK, d), lambda i, be, nu: (i, 0)),
        scratch_shapes=[pltpu.VMEM((2, MOE_BLOCK, d), F32), pltpu.SemaphoreType.DMA((2,))],
    )
    return pl.pallas_call(
        _expert_body,
        grid_spec=grid_spec,
        out_shape=jax.ShapeDtypeStruct((nb * MOE_BLOCK, d), F32),
        compiler_params=_cparams(1),
        name="moe_experts",
    )(block_e, nused, row_tok, row_tok, h, w1, w3, w2)


def _combine_body(dest_ref, dest_next_ref, gate_ref, h_ref, ln_ref, rows_ref, o_ref, buf_ref, sem):
    i = pl.program_id(0)
    last = pl.num_programs(0) - 1
    tm = h_ref.shape[0]

    def gather(dref, slot):
        return [_row_copy(rows_ref, dref[0, 0, 2 * r + s], buf_ref.at[slot, s], r, sem.at[slot])
                for r in range(tm) for s in range(2)]

    @pl.when(i == 0)
    def _():
        for cp in gather(dest_ref, 0):
            cp.start()

    for cp in gather(dest_next_ref, (i + 1) % 2):
        cp.start()
    slot = i % 2
    for cp in gather(dest_ref, slot):
        cp.wait()
    gate = gate_ref[...]
    y = gate[:, 4:5] * buf_ref[slot, 0] + gate[:, 5:6] * buf_ref[slot, 1]
    o_ref[...] = _layer_norm(DN_ALPHA * h_ref[...] + y, ln_ref[0:1, :], ln_ref[1:2, :])

    @pl.when(i == last)
    def _():
        for cp in gather(dest_next_ref, (i + 1) % 2):
            cp.wait()


def _combine(dest3, route, h, ln, y_rows, tm):
    t, d = h.shape
    nt = t // tm
    return pl.pallas_call(
        _combine_body,
        grid=(nt,),
        in_specs=[pl.BlockSpec((1, 1, 2 * tm), lambda i: (i, 0, 0), memory_space=pltpu.SMEM),
                  pl.BlockSpec((1, 1, 2 * tm), lambda i: (jnp.minimum(i + 1, nt - 1), 0, 0),
                               memory_space=pltpu.SMEM),
                  pl.BlockSpec((tm, 8), lambda i: (i, 0)),
                  pl.BlockSpec((tm, d), lambda i: (i, 0)),
                  pl.BlockSpec((8, d), lambda i: (0, 0)),
                  pl.BlockSpec(memory_space=pl.ANY)],
        out_specs=pl.BlockSpec((tm, d), lambda i: (i, 0)),
        out_shape=jax.ShapeDtypeStruct((t, d), F32),
        scratch_shapes=[pltpu.VMEM((2, 2, tm, d), F32), pltpu.SemaphoreType.DMA((2,))],
        compiler_params=_cparams(1),
        name="moe_combine_ln",
    )(dest3, dest3, route, h, ln, y_rows)


def _moe(h, route, counts, w1, w3, w2, ln):
    t, d = h.shape
    tm = min(COMBINE_TILE, t)
    a = 2 * t
    expert = route[:, 0:2].astype(jnp.int32)
    rank = route[:, 2:4].astype(jnp.int32)
    cnt = counts[0, :N_EXPERTS].astype(jnp.int32)
    padded = (cnt + MOE_BLOCK - 1) // MOE_BLOCK * MOE_BLOCK
    pend = jnp.cumsum(padded)
    pstart = pend - padded
    dest = pstart[expert] + rank
    n_blocks = -(-a // MOE_BLOCK) + N_EXPERTS
    block_row = jnp.arange(n_blocks, dtype=jnp.int32) * MOE_BLOCK
    block_e = jnp.minimum(jnp.sum((pend[None, :] <= block_row[:, None]).astype(jnp.int32), axis=1),
                          N_EXPERTS - 1)
    nused = (pend[-1:] // MOE_BLOCK).astype(jnp.int32)
    tok = jnp.broadcast_to(jnp.arange(t, dtype=jnp.int32)[:, None], (t, 2))
    row_tok = jnp.zeros((n_blocks * MOE_BLOCK,), jnp.int32).at[dest.reshape(-1)].set(tok.reshape(-1))
    y_rows = _experts(block_e, nused, row_tok.reshape(n_blocks, 1, MOE_BLOCK), h, w1, w3, w2)
    return _combine(dest.reshape(t // tm, 1, 2 * tm), route, h, ln, y_rows, tm)


def _pad_rows(x, rows):
    return jnp.zeros((rows,) + x.shape[1:], x.dtype).at[:x.shape[0]].set(x)


def _route_weights(wg, bg, we, be):
    d = wg.shape[0]
    wr = jnp.zeros((d, LANES), F32)
    wr = wr.at[:, :MOE_GROUPS].set(wg)
    wr = wr.at[:, MOE_GROUPS:MOE_GROUPS + N_EXPERTS].set(jnp.transpose(we, (1, 0, 2)).reshape(d, N_EXPERTS))
    br = jnp.zeros((8, LANES), F32)
    br = br.at[0, :MOE_GROUPS].set(bg)
    br = br.at[0, MOE_GROUPS:MOE_GROUPS + N_EXPERTS].set(be.reshape(N_EXPERTS))
    return wr, br


def kernel(x, ab_w_in, rw_mu, rw_w0, rw_w2, rw_a0, rw_a2, rw_g2, rw_k_k, rw_k_a, rw_r_k, rw_gn_g, rw_gn_b, gla_gk_w2, gla_gk_b, gla_norm_g, ab_w_out, cd_w_in, s5_a_re, s5_a_im, s5_log_dt, s5_b_re, s5_b_im, s5_c_re, s5_c_im, s5_d, s5_glu_w, s5_glu_b, hg_lb, hg_norm_g, cd_w_out, ln1_g, ln1_b, moe_wg, moe_bg, moe_we, moe_be, moe_w1, moe_w3, moe_w2, ln2_g, ln2_b):
    batch, seq, d = x.shape
    t = batch * seq
    assert d == D_MODEL and seq % CHUNK == 0
    rows_per_seq = seq // S5_CHUNK
    assert rows_per_seq & (rows_per_seq - 1) == 0, "S5 chunk scan assumes a power-of-two chunk count"
    tm = min(256, t)
    assert t % tm == 0
    xt = x.reshape(t, d)
    ln1 = [_pad_rows(jnp.stack([ln1_g[l], ln1_b[l]]), 8) for l in range(DEPTH)]
    ln2 = [_pad_rows(jnp.stack([ln2_g[l], ln2_b[l]]), 8) for l in range(DEPTH)]

    j = 0
    w = ab_w_in[j]
    r_, wl_, k_, v_, al_, gl_ = 0, 512, 576, 1088, 1600, 1664
    gq, gk, gv, glow, ggate = 1792, 2048, 2304, 2816, 2832
    w0cols = jnp.concatenate([
        w[:, r_:r_ + 512], w[:, k_:k_ + 512], w[:, v_:v_ + 512], w[:, gv:gv + 512], w[:, ggate:ggate + 512],
        w[:, wl_:wl_ + 64], w[:, al_:al_ + 64], w[:, gl_:gl_ + 128], w[:, gq:gq + 256], w[:, gk:gk + 256],
        w[:, glow:glow + 16], jnp.zeros((d, 112), F32)], axis=1).astype(BF16)
    p0 = _matmul(xt, w0cols, tm)

    mu = rw_mu[j]
    vec = _pad_rows(jnp.stack([mu[r_:r_ + 512], mu[k_:k_ + 512], mu[v_:v_ + 512], rw_w0[j], rw_a0[j],
                               rw_k_k[j], rw_k_a[j], rw_r_k[j].reshape(-1), rw_gn_g[j], rw_gn_b[j]]), 16)
    mulo = _pad_rows(jnp.concatenate([mu[wl_:wl_ + 64], mu[al_:al_ + 64], mu[gl_:gl_ + 128]])[None], 8)
    w2p = _pad_rows(rw_w2[j], 128)
    a2p = jnp.zeros((128, 512), F32).at[64:].set(rw_a2[j])
    p0 = p0.reshape(batch, seq, -1)
    y_rw = _rwkv(p0, vec, mulo, w2p, a2p, rw_g2[j]).reshape(t, RW_WIDTH)
    y_gla = _gla(p0, _pad_rows(gla_gk_w2[j], 128), _pad_rows(gla_gk_b[j][None], 8),
                 _pad_rows(gla_norm_g[j][None], 8)).reshape(t, 512)

    wr, br = _route_weights(moe_wg[0], moe_bg[0], moe_we[0], moe_be[0])
    h, route, counts = _proj_route(y_rw, y_gla, xt, ab_w_out[j].astype(BF16), ln1[0], wr, br, tm)
    h = _moe(h, route, counts, moe_w1[0], moe_w3[0], moe_w2[0], ln2[0])

    p1 = _matmul(h, cd_w_in[j].astype(BF16), tm)
    lb_sm = jax.nn.softmax(hg_lb.astype(F32), axis=0)
    lower = (jnp.cumsum(lb_sm, axis=0) - lb_sm[0])[1]
    y_hg = _hgrn(p1.reshape(batch, seq, -1), _pad_rows(lower[None], 8),
                 _pad_rows(hg_norm_g[j][None], 8)).reshape(t, 512)

    toep, bmat, emat, pw = _s5_tables(s5_a_re[j], s5_a_im[j], s5_log_dt[j], s5_b_re[j], s5_b_im[j],
                                      s5_c_re[j], s5_c_im[j], rows_per_seq)
    nrows = t // S5_CHUNK
    uc = p1[:, :512].reshape(nrows, S5_CHUNK, S5_GROUPS, S5_GROUP)
    uc = jnp.transpose(uc, (2, 0, 1, 3)).reshape(S5_GROUPS, nrows, S5_CHUNK * S5_GROUP).astype(BF16)
    yc = _s5_scan(uc, toep, bmat, emat, pw, rows_per_seq)
    y_ssm = jnp.transpose(yc.reshape(S5_GROUPS, nrows, S5_CHUNK, S5_GROUP), (1, 2, 0, 3)).reshape(t, 512)
    y_s5 = _s5_post(y_ssm, p1, _pad_rows(jnp.stack([s5_d[j], s5_glu_b[j]]), 8),
                    s5_glu_w[j].astype(BF16), tm)

    wr, br = _route_weights(moe_wg[1], moe_bg[1], moe_we[1], moe_be[1])
    h2, route, counts = _proj_route(y_s5, y_hg, h, cd_w_out[j].astype(BF16), ln1[1], wr, br, tm)
    out = _moe(h2, route, counts, moe_w1[1], moe_w3[1], moe_w2[1], ln2[1])
    return out.reshape(batch, seq, d)
```

```python
import functools
import math

import numpy as np
import jax
import jax.numpy as jnp
from jax import lax
from jax.experimental import pallas as pl
from jax.experimental.pallas import tpu as pltpu

F32 = jnp.float32
BF16 = jnp.bfloat16

D_MODEL = 1024
DEPTH = 2
RW_HEAD = 64
RW_WIDTH = 512
RW_GN_EPS = 64e-5
GLA_HEADS = 4
GLA_DK = 64
GLA_DV = 128
GLA_GATE_TAU = 16.0
S5_GROUP = 16
S5_GROUPS = 32
S5_STATE = 64
HG_HEADS = 4
HG_DK = 128
CHUNK = 64
S5_CHUNK = 16
S5_SCAN_ROWS = 32
NORM_EPS = 1e-5
MOE_GROUPS = 4
MOE_PER_GROUP = 8
N_EXPERTS = 32
EXPERT_HIDDEN = 512
MOE_BLOCK = 128
COMBINE_TILE = 128
DN_ALPHA = (2.0 * DEPTH) ** 0.25

LANES = 128
VMEM_LIMIT = 56 * 1024 * 1024


def _cparams(n_axes=1):
    return pltpu.CompilerParams(dimension_semantics=("arbitrary",) * n_axes,
                                vmem_limit_bytes=VMEM_LIMIT)


def _d(a, b):
    return jnp.dot(a, b, preferred_element_type=F32)


def _d_nt(a, b):
    return lax.dot_general(a, b, (((1,), (1,)), ((), ())), preferred_element_type=F32)


def _d_tn(a, b):
    return lax.dot_general(a, b, (((0,), (0,)), ((), ())), preferred_element_type=F32)


def _split(a):
    hi = a.astype(BF16)
    lo = (a - hi.astype(F32)).astype(BF16)
    return hi, lo


def _split3(a):
    hi = a.astype(BF16)
    r1 = a - hi.astype(F32)
    mid = r1.astype(BF16)
    lo = (r1 - mid.astype(F32)).astype(BF16)
    return hi, mid, lo


def _bdot(a, b):
    return _d(a.astype(BF16), b.astype(BF16))


def _bdot_nt(a, b):
    return _d_nt(a.astype(BF16), b.astype(BF16))


def _bdot_tn(a, b):
    return _d_tn(a.astype(BF16), b.astype(BF16))


def _hdot_with(d, a, b):
    ah, al = _split(a)
    bh, bl = _split(b)
    return d(ah, bh) + (d(ah, bl) + d(al, bh))


def _hdot(a, b):
    return _hdot_with(_d, a, b)


def _hdot_nt(a, b):
    return _hdot_with(_d_nt, a, b)


def _hdot_tn(a, b):
    return _hdot_with(_d_tn, a, b)


def _xdot_l(a, e):
    ah, am, al = _split3(a)
    return _d(ah, e) + (_d(am, e) + _d(al, e))


def _xdot_r(e, a):
    ah, am, al = _split3(a)
    return _d(e, ah) + (_d(e, am) + _d(e, al))


def _iota(shape, dim):
    return lax.broadcasted_iota(jnp.int32, shape, dim)


def _softplus(x):
    return jnp.maximum(x, 0.0) + jnp.log1p(jnp.exp(-jnp.abs(x)))


def _sigmoid(x):
    return 1.0 / (1.0 + jnp.exp(-x))


def _silu(x):
    return x * _sigmoid(x)


def _tril_incl(n):
    return jnp.where(_iota((n, n), 0) >= _iota((n, n), 1), 1.0, 0.0).astype(BF16)


def _cumsum_rows(g):
    return _xdot_r(_tril_incl(g.shape[0]), g)


def _shift_mix(x, prev_ref, b, mu):
    c = x.shape[0]
    rolled = pltpu.roll(x, 1, 0)
    prev = jnp.where(_iota(x.shape, 0) == 0, jnp.broadcast_to(prev_ref[b, 0:1, :], x.shape), rolled)
    prev_ref[b, 0:1, :] = x[c - 1:c, :]
    return x + mu * (prev - x)


def _layer_norm(x, g, b):
    mu = jnp.mean(x, axis=-1, keepdims=True)
    xc = x - mu
    var = jnp.mean(xc * xc, axis=-1, keepdims=True)
    return xc * lax.rsqrt(var + NORM_EPS) * g + b


def _mm_body(x_ref, w_ref, o_ref):
    o_ref[...] = _d(x_ref[...].astype(BF16), w_ref[...])


def _matmul(x, w_bf16, tm):
    m, k = x.shape
    n = w_bf16.shape[1]
    return pl.pallas_call(
        _mm_body,
        grid=(m // tm,),
        in_specs=[pl.BlockSpec((tm, k), lambda i: (i, 0)),
                  pl.BlockSpec((k, n), lambda i: (0, 0))],
        out_specs=pl.BlockSpec((tm, n), lambda i: (i, 0)),
        out_shape=jax.ShapeDtypeStruct((m, n), F32),
        compiler_params=_cparams(1),
        name="in_proj",
    )(x, w_bf16)


_RV_MU_R, _RV_MU_K, _RV_MU_V, _RV_W0, _RV_A0, _RV_KK, _RV_KA, _RV_RK, _RV_GNG, _RV_GNB = range(10)


def _rwkv_body(r_ref, k_ref, v_ref, lo_ref, vec_ref, mulo_ref, w2_ref, a2_ref, g2_ref, ones_ref,
               o_ref, pr_ref, pk_ref, pv_ref, plo_ref, st_ref):
    nb, c = r_ref.shape[0], r_ref.shape[1]
    npair = RW_WIDTH // LANES

    @pl.when(pl.program_id(0) == 0)
    def _():
        pr_ref[...] = jnp.zeros_like(pr_ref)
        pk_ref[...] = jnp.zeros_like(pk_ref)
        pv_ref[...] = jnp.zeros_like(pv_ref)
        plo_ref[...] = jnp.zeros_like(plo_ref)
        st_ref[...] = jnp.zeros_like(st_ref)

    def vec(i):
        return vec_ref[i:i + 1, :]

    ones_bd = ones_ref[...]
    lane = _iota((c, LANES), 1)
    m1 = lane < RW_HEAD
    row2 = _iota((2 * c, 4 * c), 0)
    col2 = _iota((2 * c, 4 * c), 1) & (c - 1)
    tri = ((row2 < c) & (row2 > col2)) | ((row2 >= c) & ((row2 - c) >= col2))
    eye2 = jnp.where(_iota((2 * c, 2 * c), 0) == _iota((2 * c, 2 * c), 1), 1.0, 0.0)
    bd_p = (_iota((LANES, LANES), 0) >> 6) == (_iota((LANES, LANES), 1) >> 6)

    def halves(x):
        return jnp.concatenate([jnp.where(m1, x, 0.0), jnp.where(m1, 0.0, x)], axis=0)

    prep = []
    for b in range(nb):
        xr = _shift_mix(r_ref[b], pr_ref, b, vec(_RV_MU_R))
        xk = _shift_mix(k_ref[b], pk_ref, b, vec(_RV_MU_K))
        xv = _shift_mix(v_ref[b], pv_ref, b, vec(_RV_MU_V))
        xlo = _shift_mix(lo_ref[b], plo_ref, b, mulo_ref[0:1, :])
        lo_a = xlo[:, :LANES]
        lo_g = xlo[:, LANES:]
        w = -_softplus(-(vec(_RV_W0) + _bdot(jnp.tanh(lo_a), w2_ref[...]))) - 0.5
        g = -jnp.exp(w)
        a = _sigmoid(vec(_RV_A0) + _bdot(lo_a, a2_ref[...]))
        gate = _bdot(_sigmoid(lo_g), g2_ref[...])
        kk = xk * vec(_RV_KK)
        kk = kk / jnp.maximum(jnp.sqrt(_xdot_l(kk * kk, ones_bd)), 1e-12)
        k2 = xk * (1.0 + (a - 1.0) * vec(_RV_KA))
        gc = _cumsum_rows(g)
        g_last = gc[c - 1:c, :]
        e_neg = jnp.exp(-gc)
        e_end = jnp.exp(g_last - gc)
        prep.append(dict(
            xr=xr, xv=xv, k2=k2, gate=gate,
            at=-kk * jnp.exp(gc - g),
            bt=(kk * a) * e_neg, kt=k2 * e_neg, rt=xr * jnp.exp(gc),
            bh=(kk * a) * e_end, kh=k2 * e_end,
            gam=jnp.exp(g_last)))

    chains = [(b, p) for b in range(nb) for p in range(npair)]

    def part(ch, name):
        b, p = ch
        return prep[b][name][:, p * LANES:(p + 1) * LANES]

    a_ak, a_row, vv, ak, pinv = {}, {}, {}, {}, {}
    for ch in chains:
        lhs = jnp.concatenate([part(ch, 'at'), part(ch, 'rt')], axis=0)
        rhs = jnp.concatenate([halves(part(ch, 'bt')), halves(part(ch, 'kt'))], axis=0)
        aa = jnp.where(tri, _bdot_nt(lhs, rhs), 0.0)
        a_ak[ch] = aa[:c, 2 * c:].astype(BF16)
        a_row[ch] = aa[c:, :].astype(BF16)
        abd = halves(aa[:c, :2 * c])
        pinv[ch] = eye2 + abd
        ak[ch] = abd
        vv[ch] = halves(part(ch, 'xv')).astype(BF16)
    nlev = int(math.log2(c))
    for lev in range(nlev):
        for ch in chains:
            akb = ak[ch].astype(BF16)
            if lev == 0:
                ak[ch] = _d(akb, akb)
            elif lev < nlev - 1:
                out = _d(akb, jnp.concatenate([akb, pinv[ch].astype(BF16)], axis=1))
                ak[ch] = out[:, :2 * c]
                pinv[ch] = pinv[ch] + out[:, 2 * c:]
            else:
                pinv[ch] = pinv[ch] + _d(akb, pinv[ch].astype(BF16))
    x2 = {ch: _d(a_ak[ch], vv[ch]) for ch in chains}

    sts = {ch: st_ref[ch[0] * npair + ch[1]] for ch in chains}
    xs = {ch: _d_nt(jnp.concatenate([part(ch, 'at'), part(ch, 'rt')], axis=0).astype(BF16),
                    sts[ch].astype(BF16)) for ch in chains}
    us = {}
    for ch in chains:
        u2 = _d(pinv[ch].astype(BF16), halves(xs[ch][:c] + x2[ch]).astype(BF16))
        us[ch] = u2[:c] + u2[c:]
    ys = {}
    for ch in chains:
        b, p = ch
        u = us[ch]
        ys[ch] = xs[ch][c:] + _d(a_row[ch], jnp.concatenate([halves(u).astype(BF16), vv[ch]], axis=0))
        upd = _d_tn(jnp.concatenate([u, part(ch, 'xv')], axis=0).astype(BF16),
                    jnp.concatenate([part(ch, 'bh'), part(ch, 'kh')], axis=0).astype(BF16))
        st_ref[b * npair + p] = sts[ch] * part(ch, 'gam') + jnp.where(bd_p, upd, 0.0)

    inv_n = 1.0 / RW_HEAD
    for b in range(nb):
        y = jnp.concatenate([ys[(b, p)] for p in range(npair)], axis=1)
        mean = _xdot_l(y, ones_bd) * inv_n
        yc = y - mean
        var = _xdot_l(yc * yc, ones_bd) * inv_n
        yn = yc * lax.rsqrt(var + RW_GN_EPS) * vec(_RV_GNG) + vec(_RV_GNB)
        bonus = _xdot_l(prep[b]['xr'] * prep[b]['k2'] * vec(_RV_RK), ones_bd) * prep[b]['xv']
        o_ref[b] = (yn + bonus) * prep[b]['gate']


def _rwkv(p0, vec, mulo, w2p, a2p, g2):
    batch, seq, _ = p0.shape
    nc = seq // CHUNK
    c = CHUNK
    ones_bd = jnp.asarray(np.kron(np.eye(RW_WIDTH // RW_HEAD), np.ones((RW_HEAD, RW_HEAD))), BF16)

    def col(j, width):
        return pl.BlockSpec((batch, c, width), lambda i: (0, i, j))

    def full(shape):
        return pl.BlockSpec(shape, lambda i: (0,) * len(shape))

    return pl.pallas_call(
        _rwkv_body,
        grid=(nc,),
        in_specs=[col(0, 512), col(1, 512), col(2, 512), col(10, 256),
                  full((16, 512)), full((8, 256)), full((128, 512)), full((128, 512)),
                  full((128, 512)), full((512, 512))],
        out_specs=pl.BlockSpec((batch, c, 512), lambda i: (0, i, 0)),
        out_shape=jax.ShapeDtypeStruct((batch, seq, RW_WIDTH), F32),
        scratch_shapes=[pltpu.VMEM((batch, 8, 512), F32), pltpu.VMEM((batch, 8, 512), F32),
                        pltpu.VMEM((batch, 8, 512), F32), pltpu.VMEM((batch, 8, 256), F32),
                        pltpu.VMEM((batch * RW_WIDTH // LANES, LANES, LANES), F32)],
        compiler_params=_cparams(1),
        name="rwkv7",
    )(p0, p0, p0, p0, vec, mulo, w2p, a2p, g2, ones_bd)


def _gla_core(qs, ks, vs, gs, st_ref, heads_per_block):
    nb = len(qs)
    c = qs[0].shape[0]
    hpb = heads_per_block
    nblk = qs[0].shape[1] // LANES
    dk_shift = int(math.log2(LANES // hpb))
    row = _iota(qs[0].shape, 0)

    prep = []
    for q, k, g in zip(qs, ks, gs):
        b = _cumsum_rows(g)

        def brow(i, b=b):
            return jnp.broadcast_to(b[i:i + 1, :], b.shape)

        b15, b31, b47, blast = brow(15), brow(31), brow(47), brow(c - 1)
        ref_b = jnp.where(row < 32, b15, b47)
        ref_d = jnp.where(row < 16, 0.0, jnp.where(row < 32, b15, jnp.where(row < 48, b31, b47)))
        prep.append(dict(
            q_a=q * jnp.exp(jnp.minimum(b - b31, 0.0)), k_a=k * jnp.exp(jnp.minimum(b31 - b, 0.0)),
            q_b=q * jnp.exp(jnp.minimum(b - ref_b, 0.0)), k_b=k * jnp.exp(jnp.minimum(ref_b - b, 0.0)),
            q_d=q * jnp.exp(b - ref_d), k_d=k * jnp.exp(ref_d - b),
            q_i=q * jnp.exp(b), k_s=k * jnp.exp(blast - b), gam=jnp.exp(b[c - 1:c, :])))

    ri = _iota((hpb * c, c), 0) & (c - 1)
    ci = _iota((hpb * c, c), 1)
    mask_a = (ri >= 32) & (ci < 32)
    mask_b = ((ri >> 5) == (ci >> 5)) & (((ri >> 4) & 1) == 1) & (((ci >> 4) & 1) == 0)
    mask_d = ((ri >> 4) == (ci >> 4)) & (ri >= ci)
    lane = _iota((c, LANES), 1)
    bd = (_iota((hpb * LANES, LANES), 0) >> 7) == (_iota((hpb * LANES, LANES), 1) >> dk_shift)

    def heads_rows(x):
        if hpb == 1:
            return x
        return jnp.concatenate([jnp.where((lane >> dk_shift) == h, x, 0.0) for h in range(hpb)], axis=0)

    chains = [(bi, blk) for bi in range(nb) for blk in range(nblk)]

    def part(ch, name):
        bi, blk = ch
        return prep[bi][name][:, blk * LANES:(blk + 1) * LANES]

    v_p = {ch: vs[ch[0]][:, ch[1] * hpb * LANES:(ch[1] + 1) * hpb * LANES].astype(BF16) for ch in chains}
    probs = {}
    for ch in chains:
        s_a = _bdot_nt(heads_rows(part(ch, 'q_a')), part(ch, 'k_a'))
        s_b = _bdot_nt(heads_rows(part(ch, 'q_b')), part(ch, 'k_b'))
        s_d = _bdot_nt(heads_rows(part(ch, 'q_d')), part(ch, 'k_d'))
        probs[ch] = (jnp.where(mask_a, s_a, 0.0) + jnp.where(mask_b, s_b, 0.0)
                     + jnp.where(mask_d, s_d, 0.0)).astype(BF16)
    outs = {}
    for ch in chains:
        pv = _d(probs[ch], v_p[ch])
        o = pv[:c]
        for h in range(1, hpb):
            o = jnp.where((_iota(o.shape, 1) >> 7) == h, pv[h * c:(h + 1) * c], o)
        si = ch[0] * nblk + ch[1]
        sp = st_ref[si]
        outs[ch] = o + _bdot_nt(part(ch, 'q_i'), sp)
        upd = _d_tn(v_p[ch], part(ch, 'k_s').astype(BF16))
        st_ref[si] = sp * part(ch, 'gam') + jnp.where(bd, upd, 0.0)
    return [jnp.concatenate([outs[(bi, blk)] for blk in range(nblk)], axis=1) for bi in range(nb)]


def _gated_rmsnorm(o, gate, norm_g):
    nh = o.shape[1] // LANES
    outs = []
    for h in range(nh):
        sl = slice(h * LANES, (h + 1) * LANES)
        oh = o[:, sl]
        ms = jnp.mean(oh * oh, axis=-1, keepdims=True)
        outs.append(oh * lax.rsqrt(ms + NORM_EPS) * norm_g * _silu(gate[:, sl]))
    return jnp.concatenate(outs, axis=1)


def _gla_body(q_ref, k_ref, v_ref, gate_ref, gk_ref, w2_ref, vec_ref, ng_ref, o_ref, st_ref):
    nb = q_ref.shape[0]

    @pl.when(pl.program_id(0) == 0)
    def _():
        st_ref[...] = jnp.zeros_like(st_ref)

    gs = [-_softplus(-(_hdot(gk_ref[b], w2_ref[...]) + vec_ref[0:1, :])) * (1.0 / GLA_GATE_TAU)
          for b in range(nb)]
    qs = [q_ref[b] * (GLA_DK ** -0.5) for b in range(nb)]
    os_ = _gla_core(qs, [k_ref[b] for b in range(nb)], [v_ref[b] for b in range(nb)], gs, st_ref, 2)
    for b in range(nb):
        o_ref[b] = _gated_rmsnorm(os_[b], gate_ref[b], ng_ref[0:1, :])


def _seq_specs(batch, c):
    def col(j, width):
        return pl.BlockSpec((batch, c, width), lambda i: (0, i, j))

    def full(shape):
        return pl.BlockSpec(shape, lambda i: (0,) * len(shape))

    return col, full


def _gla(p0, gk_w2p, gk_b, norm_g):
    batch, seq, _ = p0.shape
    c = CHUNK
    col, full = _seq_specs(batch, c)
    return pl.pallas_call(
        _gla_body,
        grid=(seq // c,),
        in_specs=[col(11, 256), col(12, 256), col(3, 512), col(4, 512), col(26, 128),
                  full((128, 256)), full((8, 256)), full((8, 128))],
        out_specs=pl.BlockSpec((batch, c, 512), lambda i: (0, i, 0)),
        out_shape=jax.ShapeDtypeStruct((batch, seq, 512), F32),
        scratch_shapes=[pltpu.VMEM((batch * 2, 2 * LANES, LANES), F32)],
        compiler_params=_cparams(1),
        name="gla",
    )(p0, p0, p0, p0, p0, gk_w2p, gk_b, norm_g)


def _hgrn_body(q_ref, f_ref, i_ref, gate_ref, lb_ref, ng_ref, o_ref, st_ref):
    nb = q_ref.shape[0]

    @pl.when(pl.program_id(0) == 0)
    def _():
        st_ref[...] = jnp.zeros_like(st_ref)

    lb = lb_ref[0:1, :]
    qs, ks, gs = [], [], []
    for b in range(nb):
        f = f_ref[b]
        gs.append(jnp.log(lb + (1.0 - lb) * _sigmoid(f)))
        ks.append((1.0 - lb) * _sigmoid(-f))
        qs.append(_silu(q_ref[b]))
    os_ = _gla_core(qs, ks, [i_ref[b] for b in range(nb)], gs, st_ref, 1)
    for b in range(nb):
        o_ref[b] = _gated_rmsnorm(os_[b], gate_ref[b], ng_ref[0:1, :])


def _hgrn(p1, lb, norm_g):
    batch, seq, _ = p1.shape
    c = CHUNK
    col, full = _seq_specs(batch, c)
    return pl.pallas_call(
        _hgrn_body,
        grid=(seq // c,),
        in_specs=[col(1, 512), col(2, 512), col(3, 512), col(4, 512), full((8, 512)), full((8, 128))],
        out_specs=pl.BlockSpec((batch, c, 512), lambda i: (0, i, 0)),
        out_shape=jax.ShapeDtypeStruct((batch, seq, 512), F32),
        scratch_shapes=[pltpu.VMEM((batch * HG_HEADS, LANES, LANES), F32)],
        compiler_params=_cparams(1),
        name="hgrn2",
    )(p1, p1, p1, p1, lb, norm_g)


def _s5_body(u_ref, toep_ref, bm_ref, e_ref, pw_ref, o_ref, *, rows_per_seq):
    u = u_ref[0]
    rows = u.shape[0]
    s = _d(u, bm_ref[0])
    rin = _iota((rows, LANES), 0) & (rows_per_seq - 1)
    pw = pw_ref[0]
    h = s
    nlev = int(math.log2(rows_per_seq))
    for lev in range(nlev):
        sh = 1 << lev
        hs = jnp.where(rin >= sh, pltpu.roll(h, sh, 0), 0.0)
        hs_sw = pltpu.roll(hs, S5_STATE, 1)
        h = h + pw[2 * lev:2 * lev + 1, :] * hs + pw[2 * lev + 1:2 * lev + 2, :] * hs_sw
    hprev = jnp.where(rin >= 1, pltpu.roll(h, 1, 0), 0.0)
    o_ref[0] = _d(u, toep_ref[0]) + _hdot(hprev, e_ref[0])


def _s5_scan(uc, toep, bmat, emat, pw, rows_per_seq):
    ng, rows, width = uc.shape
    return pl.pallas_call(
        functools.partial(_s5_body, rows_per_seq=rows_per_seq),
        grid=(ng,),
        in_specs=[pl.BlockSpec((1, rows, width), lambda g: (g, 0, 0)),
                  pl.BlockSpec((1, width, width), lambda g: (g, 0, 0)),
                  pl.BlockSpec((1, width, LANES), lambda g: (g, 0, 0)),
                  pl.BlockSpec((1, LANES, width), lambda g: (g, 0, 0)),
                  pl.BlockSpec((1, S5_SCAN_ROWS, LANES), lambda g: (g, 0, 0))],
        out_specs=pl.BlockSpec((1, rows, width), lambda g: (g, 0, 0)),
        out_shape=jax.ShapeDtypeStruct((ng, rows, width), F32),
        compiler_params=_cparams(1),
        name="s5_scan",
    )(uc, toep, bmat, emat, pw)


def _s5_post_body(y_ref, u_ref, vec_ref, w_ref, o_ref):
    y = y_ref[...] + vec_ref[0:1, :] * u_ref[...]
    y = 0.5 * y * (1.0 + jnp.tanh(math.sqrt(2.0 / math.pi) * (y + 0.044715 * (y * y * y))))
    o_ref[...] = y * _sigmoid(_bdot(y, w_ref[...]) + vec_ref[1:2, :])


def _s5_post(y_ssm, p1, vec, glu_w, tm):
    t = y_ssm.shape[0]
    return pl.pallas_call(
        _s5_post_body,
        grid=(t // tm,),
        in_specs=[pl.BlockSpec((tm, 512), lambda i: (i, 0)),
                  pl.BlockSpec((tm, 512), lambda i: (i, 0)),
                  pl.BlockSpec((8, 512), lambda i: (0, 0)),
                  pl.BlockSpec((512, 512), lambda i: (0, 0))],
        out_specs=pl.BlockSpec((tm, 512), lambda i: (i, 0)),
        out_shape=jax.ShapeDtypeStruct((t, 512), F32),
        compiler_params=_cparams(1),
        name="s5_post",
    )(y_ssm, p1, vec, glu_w)


def _s5_tables(a_re, a_im, log_dt, b_re, b_im, c_re, c_im, rows_per_seq):
    c = S5_CHUNK
    lam_re = jnp.minimum(a_re, -1e-4)
    lam_im = a_im
    dt = jnp.exp(log_dt)[:, None]
    mag = jnp.exp(lam_re * dt)
    abar_re = mag * jnp.cos(lam_im * dt)
    abar_im = mag * jnp.sin(lam_im * dt)
    den = lam_re * lam_re + lam_im * lam_im
    num_re = abar_re - 1.0
    z_re = (num_re * lam_re + abar_im * lam_im) / den
    z_im = (abar_im * lam_re - num_re * lam_im) / den

    def power(n):
        n = jnp.asarray(n, F32)[..., None, None]
        m = jnp.exp(n * (lam_re * dt))
        return m * jnp.cos(n * (lam_im * dt)), m * jnp.sin(n * (lam_im * dt))

    def cmul(ar, ai, br, bi):
        return ar * br - ai * bi, ar * bi + ai * br

    tau = jnp.arange(c)
    p_re, p_im = power(tau)
    zb_re, zb_im = cmul(z_re[..., None], z_im[..., None], b_re, b_im)
    cp_re, cp_im = cmul(c_re[None], c_im[None], p_re[:, :, None, :], p_im[:, :, None, :])
    hi = lax.Precision.HIGHEST
    kern = (jnp.einsum('tgon,gni->gtoi', cp_re, zb_re, precision=hi)
            - jnp.einsum('tgon,gni->gtoi', cp_im, zb_im, precision=hi))
    tt = jnp.arange(c)[None, :] - jnp.arange(c)[:, None]
    kt = jnp.where((tt >= 0)[None, :, :, None, None], kern[:, jnp.clip(tt, 0, c - 1)], 0.0)
    toep = jnp.transpose(kt, (0, 1, 4, 2, 3)).reshape(S5_GROUPS, c * S5_GROUP, c * S5_GROUP)
    q_re, q_im = power(c - 1 - tau)
    bm_re, bm_im = cmul(q_re[..., None], q_im[..., None], zb_re[None], zb_im[None])
    bmat = jnp.concatenate([jnp.transpose(bm_re, (1, 0, 3, 2)), jnp.transpose(bm_im, (1, 0, 3, 2))],
                           axis=-1).reshape(S5_GROUPS, c * S5_GROUP, 2 * S5_STATE)
    r_re, r_im = power(tau + 1)
    e_re, e_im = cmul(c_re[None], c_im[None], r_re[:, :, None, :], r_im[:, :, None, :])
    emat = jnp.concatenate([jnp.transpose(e_re, (1, 3, 0, 2)), -jnp.transpose(e_im, (1, 3, 0, 2))],
                           axis=1).reshape(S5_GROUPS, 2 * S5_STATE, c * S5_GROUP)
    nlev = int(math.log2(rows_per_seq))
    assert 2 * nlev <= S5_SCAN_ROWS
    s_re, s_im = power(c * (2 ** jnp.arange(nlev)))
    pw = jnp.zeros((S5_GROUPS, S5_SCAN_ROWS, 2 * S5_STATE), F32)
    pw = pw.at[:, 0:2 * nlev:2, :].set(jnp.transpose(jnp.concatenate([s_re, s_re], -1), (1, 0, 2)))
    pw = pw.at[:, 1:2 * nlev:2, :].set(jnp.transpose(jnp.concatenate([-s_im, s_im], -1), (1, 0, 2)))
    return toep.astype(BF16), bmat.astype(BF16), emat, pw


def _proj_route_body(ya_ref, yb_ref, x_ref, w_ref, ln_ref, wr_ref, br_ref,
                     h_ref, route_ref, cnt_ref, carry_ref):
    tm = ya_ref.shape[0]
    half = w_ref.shape[0] // 2

    @pl.when(pl.program_id(0) == 0)
    def _():
        carry_ref[...] = jnp.zeros_like(carry_ref)

    mix = _bdot(ya_ref[...], w_ref[:half, :]) + _bdot(yb_ref[...], w_ref[half:, :])
    h = _layer_norm(DN_ALPHA * x_ref[...] + mix, ln_ref[0:1, :], ln_ref[1:2, :])
    h_ref[...] = h

    logits = _hdot(h, wr_ref[...]) + br_ref[0:1, :]
    lane = _iota((tm, LANES), 1).astype(F32)
    neg = -jnp.inf

    def softmax_masked(mask):
        xm = jnp.where(mask, logits, neg)
        m = jnp.max(xm, axis=-1, keepdims=True)
        e = jnp.exp(xm - m)
        return e / jnp.sum(e, axis=-1, keepdims=True)

    def top1(pm):
        m = jnp.max(pm, axis=-1, keepdims=True)
        idx = jnp.min(jnp.where(pm == m, lane, float(LANES)), axis=-1, keepdims=True)
        return m, idx

    coarse = jnp.where(lane < MOE_GROUPS, softmax_masked(lane < MOE_GROUPS), -1.0)
    p_grp, grp = top1(coarse)
    lo = MOE_GROUPS + MOE_PER_GROUP * grp
    fmask = (lane >= lo) & (lane < lo + MOE_PER_GROUP)
    fine = jnp.where(fmask, softmax_masked(fmask), -1.0)
    p1, j1 = top1(fine)
    p2, j2 = top1(jnp.where(lane == j1, -1.0, fine))
    denom = p1 + p2
    g1 = p_grp * (p1 / denom)
    g2 = p_grp * (p2 / denom)
    e1 = j1 - MOE_GROUPS
    e2 = j2 - MOE_GROUPS

    oh1 = jnp.where(lane == e1, 1.0, 0.0)
    oh2 = jnp.where(lane == e2, 1.0, 0.0)
    cnt = oh1 + oh2
    strict = jnp.where(_iota((tm, tm), 0) > _iota((tm, tm), 1), 1.0, 0.0).astype(BF16)
    before = _d(strict, cnt.astype(BF16)) + carry_ref[0:1, :]
    r1 = jnp.sum(oh1 * before, axis=-1, keepdims=True)
    r2 = jnp.sum(oh2 * before, axis=-1, keepdims=True)
    carry_ref[0:1, :] = carry_ref[0:1, :] + jnp.sum(cnt, axis=0, keepdims=True)
    cnt_ref[...] = carry_ref[...]

    out = jnp.where(lane == 0, e1, 0.0)
    out = jnp.where(lane == 1, e2, out)
    out = jnp.where(lane == 2, r1, out)
    out = jnp.where(lane == 3, r2, out)
    out = jnp.where(lane == 4, g1, out)
    out = jnp.where(lane == 5, g2, out)
    route_ref[...] = out[:, :8]


def _proj_route(ya, yb, resid, w_out, ln, wr, br, tm):
    t, d = resid.shape
    return pl.pallas_call(
        _proj_route_body,
        grid=(t // tm,),
        in_specs=[pl.BlockSpec((tm, 512), lambda i: (i, 0)),
                  pl.BlockSpec((tm, 512), lambda i: (i, 0)),
                  pl.BlockSpec((tm, d), lambda i: (i, 0)),
                  pl.BlockSpec((d, d), lambda i: (0, 0)),
                  pl.BlockSpec((8, d), lambda i: (0, 0)),
                  pl.BlockSpec((d, LANES), lambda i: (0, 0)),
                  pl.BlockSpec((8, LANES), lambda i: (0, 0))],
        out_specs=[pl.BlockSpec((tm, d), lambda i: (i, 0)),
                   pl.BlockSpec((tm, 8), lambda i: (i, 0)),
                   pl.BlockSpec((8, LANES), lambda i: (0, 0))],
        out_shape=[jax.ShapeDtypeStruct((t, d), F32),
                   jax.ShapeDtypeStruct((t, 8), F32),
                   jax.ShapeDtypeStruct((8, LANES), F32)],
        scratch_shapes=[pltpu.VMEM((8, LANES), F32)],
        compiler_params=_cparams(1),
        name="proj_ln_route",
    )(ya, yb, resid, w_out, ln, wr, br)


def _row_copy(src_ref, src_row, dst_ref, dst_row, sem):
    return pltpu.make_async_copy(src_ref.at[pl.ds(src_row, 1)], dst_ref.at[pl.ds(dst_row, 1)], sem)


def _start_all(copies):
    for n, cp in enumerate(copies):
        cp.start(priority=n % 2)


def _expert_body(be_ref, nused_ref, tok_ref, tok_next_ref, h_ref, w1_ref, w3_ref, w2_ref, o_ref,
                 xbuf_ref, sem):
    del be_ref
    i = pl.program_id(0)
    nused = nused_ref[0]
    rows = xbuf_ref.shape[1]

    def gather(tok_smem, slot):
        return [_row_copy(h_ref, tok_smem[0, 0, r], xbuf_ref.at[slot], r, sem.at[slot]) for r in range(rows)]

    @pl.when((i == 0) & (nused > 0))
    def _():
        _start_all(gather(tok_ref, 0))

    @pl.when(i < nused)
    def _():
        _start_all(gather(tok_next_ref, (i + 1) % 2))
        slot = i % 2
        for cp in gather(tok_ref, slot):
            cp.wait()
        xb = xbuf_ref[slot].astype(BF16)
        h1 = _d(xb, w1_ref[0].astype(BF16))
        h3 = _d(xb, w3_ref[0].astype(BF16))
        hid = _silu(h1) * h3
        o_ref[...] = _d(hid.astype(BF16), w2_ref[0].astype(BF16))

    @pl.when(i + 1 == nused)
    def _():
        for cp in gather(tok_next_ref, (i + 1) % 2):
            cp.wait()

    @pl.when(i >= nused)
    def _():
        o_ref[...] = jnp.zeros_like(o_ref)


def _experts(block_e, nused, row_tok, h, w1, w3, w2, layer):
    d = h.shape[1]
    nb = row_tok.shape[0]
    hid = w1.shape[-1]
    grid_spec = pltpu.PrefetchScalarGridSpec(
        num_scalar_prefetch=2,
        grid=(nb,),
        in_specs=[pl.BlockSpec((1, 1, MOE_BLOCK), lambda i, be, nu: (i, 0, 0), memory_space=pltpu.SMEM),
                  pl.BlockSpec((1, 1, MOE_BLOCK), lambda i, be, nu: (jnp.minimum(i + 1, nb - 1), 0, 0),
                               memory_space=pltpu.SMEM),
                  pl.BlockSpec(memory_space=pl.ANY),
                  pl.BlockSpec((None, 1, d, hid), lambda i, be, nu: (layer, be[i], 0, 0)),
                  pl.BlockSpec((None, 1, d, hid), lambda i, be, nu: (layer, be[i], 0, 0)),
                  pl.BlockSpec((None, 1, hid, d), lambda i, be, nu: (layer, be[i], 0, 0))],
        out_specs=pl.BlockSpec((MOE_BLOCK, d), lambda i, be, nu: (i, 0)),
        scratch_shapes=[pltpu.VMEM((2, MOE_BLOCK, d), F32), pltpu.SemaphoreType.DMA((2,))],
    )
    return pl.pallas_call(
        _expert_body,
        grid_spec=grid_spec,
        out_shape=jax.ShapeDtypeStruct((nb * MOE_BLOCK, d), F32),
        compiler_params=_cparams(1),
        name="moe_experts",
    )(block_e, nused, row_tok, row_tok, h, w1, w3, w2)


def _combine_body(dest_ref, dest_next_ref, gate_ref, h_ref, ln_ref, rows_ref, o_ref, buf_ref, sem):
    i = pl.program_id(0)
    last = pl.num_programs(0) - 1
    tm = h_ref.shape[0]

    def gather(dref, slot):
        return [_row_copy(rows_ref, dref[0, 0, 2 * r + s], buf_ref.at[slot, s], r, sem.at[slot])
                for r in range(tm) for s in range(2)]

    @pl.when(i == 0)
    def _():
        _start_all(gather(dest_ref, 0))

    _start_all(gather(dest_next_ref, (i + 1) % 2))
    slot = i % 2
    for cp in gather(dest_ref, slot):
        cp.wait()
    gate = gate_ref[...]
    y = gate[:, 4:5] * buf_ref[slot, 0] + gate[:, 5:6] * buf_ref[slot, 1]
    o_ref[...] = _layer_norm(DN_ALPHA * h_ref[...] + y, ln_ref[0:1, :], ln_ref[1:2, :])

    @pl.when(i == last)
    def _():
        for cp in gather(dest_next_ref, (i + 1) % 2):
            cp.wait()


def _combine(dest3, route, h, ln, y_rows, tm):
    t, d = h.shape
    nt = t // tm
    return pl.pallas_call(
        _combine_body,
        grid=(nt,),
        in_specs=[pl.BlockSpec((1, 1, 2 * tm), lambda i: (i, 0, 0), memory_space=pltpu.SMEM),
                  pl.BlockSpec((1, 1, 2 * tm), lambda i: (jnp.minimum(i + 1, nt - 1), 0, 0),
                               memory_space=pltpu.SMEM),
                  pl.BlockSpec((tm, 8), lambda i: (i, 0)),
                  pl.BlockSpec((tm, d), lambda i: (i, 0)),
                  pl.BlockSpec((8, d), lambda i: (0, 0)),
                  pl.BlockSpec(memory_space=pl.ANY)],
        out_specs=pl.BlockSpec((tm, d), lambda i: (i, 0)),
        out_shape=jax.ShapeDtypeStruct((t, d), F32),
        scratch_shapes=[pltpu.VMEM((2, 2, tm, d), F32), pltpu.SemaphoreType.DMA((2,))],
        compiler_params=_cparams(1),
        name="moe_combine_ln",
    )(dest3, dest3, route, h, ln, y_rows)


def _moe(h, route, counts, w1, w3, w2, layer, ln):
    t, d = h.shape
    tm = min(COMBINE_TILE, t)
    a = 2 * t
    expert = route[:, 0:2].astype(jnp.int32)
    rank = route[:, 2:4].astype(jnp.int32)
    cnt = counts[0, :N_EXPERTS].astype(jnp.int32)
    padded = (cnt + MOE_BLOCK - 1) // MOE_BLOCK * MOE_BLOCK
    pend = jnp.cumsum(padded)
    pstart = pend - padded
    dest = pstart[expert] + rank
    n_blocks = -(-a // MOE_BLOCK) + N_EXPERTS
    block_row = jnp.arange(n_blocks, dtype=jnp.int32) * MOE_BLOCK
    block_e = jnp.minimum(jnp.sum((pend[None, :] <= block_row[:, None]).astype(jnp.int32), axis=1),
                          N_EXPERTS - 1)
    nused = (pend[-1:] // MOE_BLOCK).astype(jnp.int32)
    tok = jnp.broadcast_to(jnp.arange(t, dtype=jnp.int32)[:, None], (t, 2))
    row_tok = jnp.zeros((n_blocks * MOE_BLOCK,), jnp.int32).at[dest.reshape(-1)].set(tok.reshape(-1))
    y_rows = _experts(block_e, nused, row_tok.reshape(n_blocks, 1, MOE_BLOCK), h, w1, w3, w2, layer)
    return _combine(dest.reshape(t // tm, 1, 2 * tm), route, h, ln, y_rows, tm)


def _pad_rows(x, rows):
    return jnp.zeros((rows,) + x.shape[1:], x.dtype).at[:x.shape[0]].set(x)


def _route_weights(wg, bg, we, be):
    d = wg.shape[0]
    wr = jnp.zeros((d, LANES), F32)
    wr = wr.at[:, :MOE_GROUPS].set(wg)
    wr = wr.at[:, MOE_GROUPS:MOE_GROUPS + N_EXPERTS].set(jnp.transpose(we, (1, 0, 2)).reshape(d, N_EXPERTS))
    br = jnp.zeros((8, LANES), F32)
    br = br.at[0, :MOE_GROUPS].set(bg)
    br = br.at[0, MOE_GROUPS:MOE_GROUPS + N_EXPERTS].set(be.reshape(N_EXPERTS))
    return wr, br


def kernel(x, ab_w_in, rw_mu, rw_w0, rw_w2, rw_a0, rw_a2, rw_g2, rw_k_k, rw_k_a, rw_r_k, rw_gn_g, rw_gn_b, gla_gk_w2, gla_gk_b, gla_norm_g, ab_w_out, cd_w_in, s5_a_re, s5_a_im, s5_log_dt, s5_b_re, s5_b_im, s5_c_re, s5_c_im, s5_d, s5_glu_w, s5_glu_b, hg_lb, hg_norm_g, cd_w_out, ln1_g, ln1_b, moe_wg, moe_bg, moe_we, moe_be, moe_w1, moe_w3, moe_w2, ln2_g, ln2_b):
    batch, seq, d = x.shape
    t = batch * seq
    assert d == D_MODEL and seq % CHUNK == 0
    rows_per_seq = seq // S5_CHUNK
    assert rows_per_seq & (rows_per_seq - 1) == 0, "S5 chunk scan assumes a power-of-two chunk count"
    tm = min(256, t)
    assert t % tm == 0
    xt = x.reshape(t, d)
    ln1 = [_pad_rows(jnp.stack([ln1_g[l], ln1_b[l]]), 8) for l in range(DEPTH)]
    ln2 = [_pad_rows(jnp.stack([ln2_g[l], ln2_b[l]]), 8) for l in range(DEPTH)]

    j = 0
    w = ab_w_in[j]
    r_, wl_, k_, v_, al_, gl_ = 0, 512, 576, 1088, 1600, 1664
    gq, gk, gv, glow, ggate = 1792, 2048, 2304, 2816, 2832
    w0cols = jnp.concatenate([
        w[:, r_:r_ + 512], w[:, k_:k_ + 512], w[:, v_:v_ + 512], w[:, gv:gv + 512], w[:, ggate:ggate + 512],
        w[:, wl_:wl_ + 64], w[:, al_:al_ + 64], w[:, gl_:gl_ + 128], w[:, gq:gq + 256], w[:, gk:gk + 256],
        w[:, glow:glow + 16], jnp.zeros((d, 112), F32)], axis=1).astype(BF16)
    p0 = _matmul(xt, w0cols, tm)

    mu = rw_mu[j]
    vec = _pad_rows(jnp.stack([mu[r_:r_ + 512], mu[k_:k_ + 512], mu[v_:v_ + 512], rw_w0[j], rw_a0[j],
                               rw_k_k[j], rw_k_a[j], rw_r_k[j].reshape(-1), rw_gn_g[j], rw_gn_b[j]]), 16)
    mulo = _pad_rows(jnp.concatenate([mu[wl_:wl_ + 64], mu[al_:al_ + 64], mu[gl_:gl_ + 128]])[None], 8)
    w2p = _pad_rows(rw_w2[j], 128)
    a2p = jnp.zeros((128, 512), F32).at[64:].set(rw_a2[j])
    p0 = p0.reshape(batch, seq, -1)
    y_rw = _rwkv(p0, vec, mulo, w2p, a2p, rw_g2[j]).reshape(t, RW_WIDTH)
    y_gla = _gla(p0, _pad_rows(gla_gk_w2[j], 128), _pad_rows(gla_gk_b[j][None], 8),
                 _pad_rows(gla_norm_g[j][None], 8)).reshape(t, 512)

    wr, br = _route_weights(moe_wg[0], moe_bg[0], moe_we[0], moe_be[0])
    h, route, counts = _proj_route(y_rw, y_gla, xt, ab_w_out[j].astype(BF16), ln1[0], wr, br, tm)
    h = _moe(h, route, counts, moe_w1, moe_w3, moe_w2, 0, ln2[0])

    p1 = _matmul(h, cd_w_in[j].astype(BF16), tm)
    lb_sm = jax.nn.softmax(hg_lb.astype(F32), axis=0)
    lower = (jnp.cumsum(lb_sm, axis=0) - lb_sm[0])[1]
    y_hg = _hgrn(p1.reshape(batch, seq, -1), _pad_rows(lower[None], 8),
                 _pad_rows(hg_norm_g[j][None], 8)).reshape(t, 512)

    toep, bmat, emat, pw = _s5_tables(s5_a_re[j], s5_a_im[j], s5_log_dt[j], s5_b_re[j], s5_b_im[j],
                                      s5_c_re[j], s5_c_im[j], rows_per_seq)
    nrows = t // S5_CHUNK
    uc = p1[:, :512].reshape(nrows, S5_CHUNK, S5_GROUPS, S5_GROUP)
    uc = jnp.transpose(uc, (2, 0, 1, 3)).reshape(S5_GROUPS, nrows, S5_CHUNK * S5_GROUP).astype(BF16)
    yc = _s5_scan(uc, toep, bmat, emat, pw, rows_per_seq)
    y_ssm = jnp.transpose(yc.reshape(S5_GROUPS, nrows, S5_CHUNK, S5_GROUP), (1, 2, 0, 3)).reshape(t, 512)
    y_s5 = _s5_post(y_ssm, p1, _pad_rows(jnp.stack([s5_d[j], s5_glu_b[j]]), 8),
                    s5_glu_w[j].astype(BF16), tm)

    wr, br = _route_weights(moe_wg[1], moe_bg[1], moe_we[1], moe_be[1])
    h2, route, counts = _proj_route(y_s5, y_hg, h, cd_w_out[j].astype(BF16), ln1[1], wr, br, tm)
    out = _moe(h2, route, counts, moe_w1, moe_w3, moe_w2, 1, ln2[1])
    return out.reshape(batch, seq, d)
```

```python
import functools
import math

import numpy as np
import jax
import jax.numpy as jnp
from jax import lax
from jax.experimental import pallas as pl
from jax.experimental.pallas import tpu as pltpu

F32 = jnp.float32
BF16 = jnp.bfloat16

D_MODEL = 1024
DEPTH = 2
RW_HEAD = 64
RW_WIDTH = 512
RW_GN_EPS = 64e-5
GLA_HEADS = 4
GLA_DK = 64
GLA_DV = 128
GLA_GATE_TAU = 16.0
S5_GROUP = 16
S5_GROUPS = 32
S5_STATE = 64
HG_HEADS = 4
HG_DK = 128
CHUNK = 64
S5_CHUNK = 16
S5_SCAN_ROWS = 32
NORM_EPS = 1e-5
MOE_GROUPS = 4
MOE_PER_GROUP = 8
N_EXPERTS = 32
EXPERT_HIDDEN = 512
MOE_BLOCK = 128
COMBINE_TILE = 128
DN_ALPHA = (2.0 * DEPTH) ** 0.25

LANES = 128
VMEM_LIMIT = 56 * 1024 * 1024


def _cparams(n_axes=1):
    return pltpu.CompilerParams(dimension_semantics=("arbitrary",) * n_axes,
                                vmem_limit_bytes=VMEM_LIMIT)


def _d(a, b):
    return jnp.dot(a, b, preferred_element_type=F32)


def _d_nt(a, b):
    return lax.dot_general(a, b, (((1,), (1,)), ((), ())), preferred_element_type=F32)


def _d_tn(a, b):
    return lax.dot_general(a, b, (((0,), (0,)), ((), ())), preferred_element_type=F32)


def _split(a):
    hi = a.astype(BF16)
    lo = (a - hi.astype(F32)).astype(BF16)
    return hi, lo


def _split3(a):
    hi = a.astype(BF16)
    r1 = a - hi.astype(F32)
    mid = r1.astype(BF16)
    lo = (r1 - mid.astype(F32)).astype(BF16)
    return hi, mid, lo


def _bdot(a, b):
    return _d(a.astype(BF16), b.astype(BF16))


def _bdot_nt(a, b):
    return _d_nt(a.astype(BF16), b.astype(BF16))


def _bdot_tn(a, b):
    return _d_tn(a.astype(BF16), b.astype(BF16))


def _hdot_with(d, a, b):
    ah, al = _split(a)
    bh, bl = _split(b)
    return d(ah, bh) + (d(ah, bl) + d(al, bh))


def _hdot(a, b):
    return _hdot_with(_d, a, b)


def _hdot_nt(a, b):
    return _hdot_with(_d_nt, a, b)


def _hdot_tn(a, b):
    return _hdot_with(_d_tn, a, b)


def _xdot_l(a, e):
    ah, am, al = _split3(a)
    return _d(ah, e) + (_d(am, e) + _d(al, e))


def _xdot_r(e, a):
    ah, am, al = _split3(a)
    return _d(e, ah) + (_d(e, am) + _d(e, al))


def _iota(shape, dim):
    return lax.broadcasted_iota(jnp.int32, shape, dim)


def _softplus(x):
    return jnp.maximum(x, 0.0) + jnp.log1p(jnp.exp(-jnp.abs(x)))


def _sigmoid(x):
    return 1.0 / (1.0 + jnp.exp(-x))


def _silu(x):
    return x * _sigmoid(x)


def _tril_incl(n):
    return jnp.where(_iota((n, n), 0) >= _iota((n, n), 1), 1.0, 0.0).astype(BF16)


def _cumsum_rows(g):
    return _xdot_r(_tril_incl(g.shape[0]), g)


def _shift_mix(x, prev_ref, b, mu):
    c = x.shape[0]
    rolled = pltpu.roll(x, 1, 0)
    prev = jnp.where(_iota(x.shape, 0) == 0, jnp.broadcast_to(prev_ref[b, 0:1, :], x.shape), rolled)
    prev_ref[b, 0:1, :] = x[c - 1:c, :]
    return x + mu * (prev - x)


def _layer_norm(x, g, b):
    mu = jnp.mean(x, axis=-1, keepdims=True)
    xc = x - mu
    var = jnp.mean(xc * xc, axis=-1, keepdims=True)
    return xc * lax.rsqrt(var + NORM_EPS) * g + b


def _mm_body(x_ref, w_ref, o_ref):
    o_ref[...] = _d(x_ref[...].astype(BF16), w_ref[...])


def _matmul(x, w_bf16, tm):
    m, k = x.shape
    n = w_bf16.shape[1]
    return pl.pallas_call(
        _mm_body,
        grid=(m // tm,),
        in_specs=[pl.BlockSpec((tm, k), lambda i: (i, 0)),
                  pl.BlockSpec((k, n), lambda i: (0, 0))],
        out_specs=pl.BlockSpec((tm, n), lambda i: (i, 0)),
        out_shape=jax.ShapeDtypeStruct((m, n), F32),
        compiler_params=_cparams(1),
        name="in_proj",
    )(x, w_bf16)


_RV_MU_R, _RV_MU_K, _RV_MU_V, _RV_W0, _RV_A0, _RV_KK, _RV_KA, _RV_RK, _RV_GNG, _RV_GNB = range(10)


def _rwkv_body(r_ref, k_ref, v_ref, lo_ref, vec_ref, mulo_ref, w2_ref, a2_ref, g2_ref, ones_ref,
               o_ref, pr_ref, pk_ref, pv_ref, plo_ref, st_ref):
    nb, c = r_ref.shape[0], r_ref.shape[1]
    npair = RW_WIDTH // LANES

    @pl.when(pl.program_id(0) == 0)
    def _():
        pr_ref[...] = jnp.zeros_like(pr_ref)
        pk_ref[...] = jnp.zeros_like(pk_ref)
        pv_ref[...] = jnp.zeros_like(pv_ref)
        plo_ref[...] = jnp.zeros_like(plo_ref)
        st_ref[...] = jnp.zeros_like(st_ref)

    def vec(i):
        return vec_ref[i:i + 1, :]

    ones_bd = ones_ref[...]
    lane = _iota((c, LANES), 1)
    m1 = lane < RW_HEAD
    row2 = _iota((2 * c, 4 * c), 0)
    col2 = _iota((2 * c, 4 * c), 1) & (c - 1)
    tri = ((row2 < c) & (row2 > col2)) | ((row2 >= c) & ((row2 - c) >= col2))
    eye2 = jnp.where(_iota((2 * c, 2 * c), 0) == _iota((2 * c, 2 * c), 1), 1.0, 0.0)
    bd_p = (_iota((LANES, LANES), 0) >> 6) == (_iota((LANES, LANES), 1) >> 6)

    def halves(x):
        return jnp.concatenate([jnp.where(m1, x, 0.0), jnp.where(m1, 0.0, x)], axis=0)

    prep = []
    for b in range(nb):
        xr = _shift_mix(r_ref[b], pr_ref, b, vec(_RV_MU_R))
        xk = _shift_mix(k_ref[b], pk_ref, b, vec(_RV_MU_K))
        xv = _shift_mix(v_ref[b], pv_ref, b, vec(_RV_MU_V))
        xlo = _shift_mix(lo_ref[b], plo_ref, b, mulo_ref[0:1, :])
        lo_a = xlo[:, :LANES]
        lo_g = xlo[:, LANES:]
        w = -_softplus(-(vec(_RV_W0) + _bdot(jnp.tanh(lo_a), w2_ref[...]))) - 0.5
        g = -jnp.exp(w)
        a = _sigmoid(vec(_RV_A0) + _bdot(lo_a, a2_ref[...]))
        gate = _bdot(_sigmoid(lo_g), g2_ref[...])
        kk = xk * vec(_RV_KK)
        kk = kk / jnp.maximum(jnp.sqrt(_xdot_l(kk * kk, ones_bd)), 1e-12)
        k2 = xk * (1.0 + (a - 1.0) * vec(_RV_KA))
        gc = _cumsum_rows(g)
        g_last = gc[c - 1:c, :]
        e_neg = jnp.exp(-gc)
        e_end = jnp.exp(g_last - gc)
        prep.append(dict(
            xr=xr, xv=xv, k2=k2, gate=gate,
            at=-kk * jnp.exp(gc - g),
            bt=(kk * a) * e_neg, kt=k2 * e_neg, rt=xr * jnp.exp(gc),
            bh=(kk * a) * e_end, kh=k2 * e_end,
            gam=jnp.exp(g_last)))

    chains = [(b, p) for b in range(nb) for p in range(npair)]

    def part(ch, name):
        b, p = ch
        return prep[b][name][:, p * LANES:(p + 1) * LANES]

    a_ak, a_row, vv, ak, pinv = {}, {}, {}, {}, {}
    for ch in chains:
        lhs = jnp.concatenate([part(ch, 'at'), part(ch, 'rt')], axis=0)
        rhs = jnp.concatenate([halves(part(ch, 'bt')), halves(part(ch, 'kt'))], axis=0)
        aa = jnp.where(tri, _bdot_nt(lhs, rhs), 0.0)
        a_ak[ch] = aa[:c, 2 * c:].astype(BF16)
        a_row[ch] = aa[c:, :].astype(BF16)
        abd = halves(aa[:c, :2 * c])
        pinv[ch] = eye2 + abd
        ak[ch] = abd
        vv[ch] = halves(part(ch, 'xv')).astype(BF16)
    nlev = int(math.log2(c))
    for lev in range(nlev):
        for ch in chains:
            akb = ak[ch].astype(BF16)
            if lev == 0:
                ak[ch] = _d(akb, akb)
            elif lev < nlev - 1:
                out = _d(akb, jnp.concatenate([akb, pinv[ch].astype(BF16)], axis=1))
                ak[ch] = out[:, :2 * c]
                pinv[ch] = pinv[ch] + out[:, 2 * c:]
            else:
                pinv[ch] = pinv[ch] + _d(akb, pinv[ch].astype(BF16))
    x2 = {ch: _d(a_ak[ch], vv[ch]) for ch in chains}

    sts = {ch: st_ref[ch[0] * npair + ch[1]] for ch in chains}
    xs = {ch: _d_nt(jnp.concatenate([part(ch, 'at'), part(ch, 'rt')], axis=0).astype(BF16),
                    sts[ch].astype(BF16)) for ch in chains}
    us = {}
    for ch in chains:
        u2 = _d(pinv[ch].astype(BF16), halves(xs[ch][:c] + x2[ch]).astype(BF16))
        us[ch] = u2[:c] + u2[c:]
    ys = {}
    for ch in chains:
        b, p = ch
        u = us[ch]
        ys[ch] = xs[ch][c:] + _d(a_row[ch], jnp.concatenate([halves(u).astype(BF16), vv[ch]], axis=0))
        upd = _d_tn(jnp.concatenate([u, part(ch, 'xv')], axis=0).astype(BF16),
                    jnp.concatenate([part(ch, 'bh'), part(ch, 'kh')], axis=0).astype(BF16))
        st_ref[b * npair + p] = sts[ch] * part(ch, 'gam') + jnp.where(bd_p, upd, 0.0)

    inv_n = 1.0 / RW_HEAD
    for b in range(nb):
        y = jnp.concatenate([ys[(b, p)] for p in range(npair)], axis=1)
        mean = _xdot_l(y, ones_bd) * inv_n
        yc = y - mean
        var = _xdot_l(yc * yc, ones_bd) * inv_n
        yn = yc * lax.rsqrt(var + RW_GN_EPS) * vec(_RV_GNG) + vec(_RV_GNB)
        bonus = _xdot_l(prep[b]['xr'] * prep[b]['k2'] * vec(_RV_RK), ones_bd) * prep[b]['xv']
        o_ref[b] = (yn + bonus) * prep[b]['gate']


def _rwkv(p0, vec, mulo, w2p, a2p, g2):
    batch, seq, _ = p0.shape
    nc = seq // CHUNK
    c = CHUNK
    ones_bd = jnp.asarray(np.kron(np.eye(RW_WIDTH // RW_HEAD), np.ones((RW_HEAD, RW_HEAD))), BF16)

    def col(j, width):
        return pl.BlockSpec((batch, c, width), lambda i: (0, i, j))

    def full(shape):
        return pl.BlockSpec(shape, lambda i: (0,) * len(shape))

    return pl.pallas_call(
        _rwkv_body,
        grid=(nc,),
        in_specs=[col(0, 512), col(1, 512), col(2, 512), col(10, 256),
                  full((16, 512)), full((8, 256)), full((128, 512)), full((128, 512)),
                  full((128, 512)), full((512, 512))],
        out_specs=pl.BlockSpec((batch, c, 512), lambda i: (0, i, 0)),
        out_shape=jax.ShapeDtypeStruct((batch, seq, RW_WIDTH), F32),
        scratch_shapes=[pltpu.VMEM((batch, 8, 512), F32), pltpu.VMEM((batch, 8, 512), F32),
                        pltpu.VMEM((batch, 8, 512), F32), pltpu.VMEM((batch, 8, 256), F32),
                        pltpu.VMEM((batch * RW_WIDTH // LANES, LANES, LANES), F32)],
        compiler_params=_cparams(1),
        name="rwkv7",
    )(p0, p0, p0, p0, vec, mulo, w2p, a2p, g2, ones_bd)


def _gla_core(qs, ks, vs, gs, st_ref, heads_per_block):
    nb = len(qs)
    c = qs[0].shape[0]
    hpb = heads_per_block
    nblk = qs[0].shape[1] // LANES
    dk_shift = int(math.log2(LANES // hpb))
    row = _iota(qs[0].shape, 0)

    prep = []
    for q, k, g in zip(qs, ks, gs):
        b = _cumsum_rows(g)

        def brow(i, b=b):
            return jnp.broadcast_to(b[i:i + 1, :], b.shape)

        b15, b31, b47, blast = brow(15), brow(31), brow(47), brow(c - 1)
        ref_b = jnp.where(row < 32, b15, b47)
        ref_d = jnp.where(row < 16, 0.0, jnp.where(row < 32, b15, jnp.where(row < 48, b31, b47)))
        prep.append(dict(
            q_a=q * jnp.exp(jnp.minimum(b - b31, 0.0)), k_a=k * jnp.exp(jnp.minimum(b31 - b, 0.0)),
            q_b=q * jnp.exp(jnp.minimum(b - ref_b, 0.0)), k_b=k * jnp.exp(jnp.minimum(ref_b - b, 0.0)),
            q_d=q * jnp.exp(b - ref_d), k_d=k * jnp.exp(ref_d - b),
            q_i=q * jnp.exp(b), k_s=k * jnp.exp(blast - b), gam=jnp.exp(b[c - 1:c, :])))

    ri = _iota((hpb * c, c), 0) & (c - 1)
    ci = _iota((hpb * c, c), 1)
    mask_a = (ri >= 32) & (ci < 32)
    mask_b = ((ri >> 5) == (ci >> 5)) & (((ri >> 4) & 1) == 1) & (((ci >> 4) & 1) == 0)
    mask_d = ((ri >> 4) == (ci >> 4)) & (ri >= ci)
    lane = _iota((c, LANES), 1)
    bd = (_iota((hpb * LANES, LANES), 0) >> 7) == (_iota((hpb * LANES, LANES), 1) >> dk_shift)

    def heads_rows(x):
        if hpb == 1:
            return x
        return jnp.concatenate([jnp.where((lane >> dk_shift) == h, x, 0.0) for h in range(hpb)], axis=0)

    chains = [(bi, blk) for bi in range(nb) for blk in range(nblk)]

    def part(ch, name):
        bi, blk = ch
        return prep[bi][name][:, blk * LANES:(blk + 1) * LANES]

    v_p = {ch: vs[ch[0]][:, ch[1] * hpb * LANES:(ch[1] + 1) * hpb * LANES].astype(BF16) for ch in chains}
    probs = {}
    for ch in chains:
        s_a = _bdot_nt(heads_rows(part(ch, 'q_a')), part(ch, 'k_a'))
        s_b = _bdot_nt(heads_rows(part(ch, 'q_b')), part(ch, 'k_b'))
        s_d = _bdot_nt(heads_rows(part(ch, 'q_d')), part(ch, 'k_d'))
        probs[ch] = (jnp.where(mask_a, s_a, 0.0) + jnp.where(mask_b, s_b, 0.0)
                     + jnp.where(mask_d, s_d, 0.0)).astype(BF16)
    outs = {}
    for ch in chains:
        pv = _d(probs[ch], v_p[ch])
        o = pv[:c]
        for h in range(1, hpb):
            o = jnp.where((_iota(o.shape, 1) >> 7) == h, pv[h * c:(h + 1) * c], o)
        si = ch[0] * nblk + ch[1]
        sp = st_ref[si]
        outs[ch] = o + _bdot_nt(part(ch, 'q_i'), sp)
        upd = _d_tn(v_p[ch], part(ch, 'k_s').astype(BF16))
        st_ref[si] = sp * part(ch, 'gam') + jnp.where(bd, upd, 0.0)
    return [jnp.concatenate([outs[(bi, blk)] for blk in range(nblk)], axis=1) for bi in range(nb)]


def _gated_rmsnorm(o, gate, norm_g):
    nh = o.shape[1] // LANES
    outs = []
    for h in range(nh):
        sl = slice(h * LANES, (h + 1) * LANES)
        oh = o[:, sl]
        ms = jnp.mean(oh * oh, axis=-1, keepdims=True)
        outs.append(oh * lax.rsqrt(ms + NORM_EPS) * norm_g * _silu(gate[:, sl]))
    return jnp.concatenate(outs, axis=1)


def _gla_body(q_ref, k_ref, v_ref, gate_ref, gk_ref, w2_ref, vec_ref, ng_ref, o_ref, st_ref):
    nb = q_ref.shape[0]

    @pl.when(pl.program_id(0) == 0)
    def _():
        st_ref[...] = jnp.zeros_like(st_ref)

    gs = [-_softplus(-(_hdot(gk_ref[b], w2_ref[...]) + vec_ref[0:1, :])) * (1.0 / GLA_GATE_TAU)
          for b in range(nb)]
    qs = [q_ref[b] * (GLA_DK ** -0.5) for b in range(nb)]
    os_ = _gla_core(qs, [k_ref[b] for b in range(nb)], [v_ref[b] for b in range(nb)], gs, st_ref, 2)
    for b in range(nb):
        o_ref[b] = _gated_rmsnorm(os_[b], gate_ref[b], ng_ref[0:1, :])


def _seq_specs(batch, c):
    def col(j, width):
        return pl.BlockSpec((batch, c, width), lambda i: (0, i, j))

    def full(shape):
        return pl.BlockSpec(shape, lambda i: (0,) * len(shape))

    return col, full


def _gla(p0, gk_w2p, gk_b, norm_g):
    batch, seq, _ = p0.shape
    c = CHUNK
    col, full = _seq_specs(batch, c)
    return pl.pallas_call(
        _gla_body,
        grid=(seq // c,),
        in_specs=[col(11, 256), col(12, 256), col(3, 512), col(4, 512), col(26, 128),
                  full((128, 256)), full((8, 256)), full((8, 128))],
        out_specs=pl.BlockSpec((batch, c, 512), lambda i: (0, i, 0)),
        out_shape=jax.ShapeDtypeStruct((batch, seq, 512), F32),
        scratch_shapes=[pltpu.VMEM((batch * 2, 2 * LANES, LANES), F32)],
        compiler_params=_cparams(1),
        name="gla",
    )(p0, p0, p0, p0, p0, gk_w2p, gk_b, norm_g)


def _hgrn_body(q_ref, f_ref, i_ref, gate_ref, lb_ref, ng_ref, o_ref, st_ref):
    nb = q_ref.shape[0]

    @pl.when(pl.program_id(0) == 0)
    def _():
        st_ref[...] = jnp.zeros_like(st_ref)

    lb = lb_ref[0:1, :]
    qs, ks, gs = [], [], []
    for b in range(nb):
        f = f_ref[b]
        gs.append(jnp.log(lb + (1.0 - lb) * _sigmoid(f)))
        ks.append((1.0 - lb) * _sigmoid(-f))
        qs.append(_silu(q_ref[b]))
    os_ = _gla_core(qs, ks, [i_ref[b] for b in range(nb)], gs, st_ref, 1)
    for b in range(nb):
        o_ref[b] = _gated_rmsnorm(os_[b], gate_ref[b], ng_ref[0:1, :])


def _hgrn(p1, lb, norm_g):
    batch, seq, _ = p1.shape
    c = CHUNK
    col, full = _seq_specs(batch, c)
    return pl.pallas_call(
        _hgrn_body,
        grid=(seq // c,),
        in_specs=[col(1, 512), col(2, 512), col(3, 512), col(4, 512), full((8, 512)), full((8, 128))],
        out_specs=pl.BlockSpec((batch, c, 512), lambda i: (0, i, 0)),
        out_shape=jax.ShapeDtypeStruct((batch, seq, 512), F32),
        scratch_shapes=[pltpu.VMEM((batch * HG_HEADS, LANES, LANES), F32)],
        compiler_params=_cparams(1),
        name="hgrn2",
    )(p1, p1, p1, p1, lb, norm_g)


def _s5_body(u_ref, toep_ref, bm_ref, e_ref, pw_ref, o_ref, *, rows_per_seq):
    u = u_ref[0]
    rows = u.shape[0]
    s = _d(u, bm_ref[0])
    rin = _iota((rows, LANES), 0) & (rows_per_seq - 1)
    pw = pw_ref[0]
    h = s
    nlev = int(math.log2(rows_per_seq))
    for lev in range(nlev):
        sh = 1 << lev
        hs = jnp.where(rin >= sh, pltpu.roll(h, sh, 0), 0.0)
        hs_sw = pltpu.roll(hs, S5_STATE, 1)
        h = h + pw[2 * lev:2 * lev + 1, :] * hs + pw[2 * lev + 1:2 * lev + 2, :] * hs_sw
    hprev = jnp.where(rin >= 1, pltpu.roll(h, 1, 0), 0.0)
    o_ref[0] = _d(u, toep_ref[0]) + _hdot(hprev, e_ref[0])


def _s5_scan(uc, toep, bmat, emat, pw, rows_per_seq):
    ng, rows, width = uc.shape
    return pl.pallas_call(
        functools.partial(_s5_body, rows_per_seq=rows_per_seq),
        grid=(ng,),
        in_specs=[pl.BlockSpec((1, rows, width), lambda g: (g, 0, 0)),
                  pl.BlockSpec((1, width, width), lambda g: (g, 0, 0)),
                  pl.BlockSpec((1, width, LANES), lambda g: (g, 0, 0)),
                  pl.BlockSpec((1, LANES, width), lambda g: (g, 0, 0)),
                  pl.BlockSpec((1, S5_SCAN_ROWS, LANES), lambda g: (g, 0, 0))],
        out_specs=pl.BlockSpec((1, rows, width), lambda g: (g, 0, 0)),
        out_shape=jax.ShapeDtypeStruct((ng, rows, width), F32),
        compiler_params=_cparams(1),
        name="s5_scan",
    )(uc, toep, bmat, emat, pw)


def _s5_post_body(y_ref, u_ref, vec_ref, w_ref, o_ref):
    y = y_ref[...] + vec_ref[0:1, :] * u_ref[...]
    y = 0.5 * y * (1.0 + jnp.tanh(math.sqrt(2.0 / math.pi) * (y + 0.044715 * (y * y * y))))
    o_ref[...] = y * _sigmoid(_bdot(y, w_ref[...]) + vec_ref[1:2, :])


def _s5_post(y_ssm, p1, vec, glu_w, tm):
    t = y_ssm.shape[0]
    return pl.pallas_call(
        _s5_post_body,
        grid=(t // tm,),
        in_specs=[pl.BlockSpec((tm, 512), lambda i: (i, 0)),
                  pl.BlockSpec((tm, 512), lambda i: (i, 0)),
                  pl.BlockSpec((8, 512), lambda i: (0, 0)),
                  pl.BlockSpec((512, 512), lambda i: (0, 0))],
        out_specs=pl.BlockSpec((tm, 512), lambda i: (i, 0)),
        out_shape=jax.ShapeDtypeStruct((t, 512), F32),
        compiler_params=_cparams(1),
        name="s5_post",
    )(y_ssm, p1, vec, glu_w)


def _s5_tables(a_re, a_im, log_dt, b_re, b_im, c_re, c_im, rows_per_seq):
    c = S5_CHUNK
    lam_re = jnp.minimum(a_re, -1e-4)
    lam_im = a_im
    dt = jnp.exp(log_dt)[:, None]
    mag = jnp.exp(lam_re * dt)
    abar_re = mag * jnp.cos(lam_im * dt)
    abar_im = mag * jnp.sin(lam_im * dt)
    den = lam_re * lam_re + lam_im * lam_im
    num_re = abar_re - 1.0
    z_re = (num_re * lam_re + abar_im * lam_im) / den
    z_im = (abar_im * lam_re - num_re * lam_im) / den

    def power(n):
        n = jnp.asarray(n, F32)[..., None, None]
        m = jnp.exp(n * (lam_re * dt))
        return m * jnp.cos(n * (lam_im * dt)), m * jnp.sin(n * (lam_im * dt))

    def cmul(ar, ai, br, bi):
        return ar * br - ai * bi, ar * bi + ai * br

    tau = jnp.arange(c)
    p_re, p_im = power(tau)
    zb_re, zb_im = cmul(z_re[..., None], z_im[..., None], b_re, b_im)
    cp_re, cp_im = cmul(c_re[None], c_im[None], p_re[:, :, None, :], p_im[:, :, None, :])
    hi = lax.Precision.HIGHEST
    kern = (jnp.einsum('tgon,gni->gtoi', cp_re, zb_re, precision=hi)
            - jnp.einsum('tgon,gni->gtoi', cp_im, zb_im, precision=hi))
    tt = jnp.arange(c)[None, :] - jnp.arange(c)[:, None]
    kt = jnp.where((tt >= 0)[None, :, :, None, None], kern[:, jnp.clip(tt, 0, c - 1)], 0.0)
    toep = jnp.transpose(kt, (0, 1, 4, 2, 3)).reshape(S5_GROUPS, c * S5_GROUP, c * S5_GROUP)
    q_re, q_im = power(c - 1 - tau)
    bm_re, bm_im = cmul(q_re[..., None], q_im[..., None], zb_re[None], zb_im[None])
    bmat = jnp.concatenate([jnp.transpose(bm_re, (1, 0, 3, 2)), jnp.transpose(bm_im, (1, 0, 3, 2))],
                           axis=-1).reshape(S5_GROUPS, c * S5_GROUP, 2 * S5_STATE)
    r_re, r_im = power(tau + 1)
    e_re, e_im = cmul(c_re[None], c_im[None], r_re[:, :, None, :], r_im[:, :, None, :])
    emat = jnp.concatenate([jnp.transpose(e_re, (1, 3, 0, 2)), -jnp.transpose(e_im, (1, 3, 0, 2))],
                           axis=1).reshape(S5_GROUPS, 2 * S5_STATE, c * S5_GROUP)
    nlev = int(math.log2(rows_per_seq))
    assert 2 * nlev <= S5_SCAN_ROWS
    s_re, s_im = power(c * (2 ** jnp.arange(nlev)))
    pw = jnp.zeros((S5_GROUPS, S5_SCAN_ROWS, 2 * S5_STATE), F32)
    pw = pw.at[:, 0:2 * nlev:2, :].set(jnp.transpose(jnp.concatenate([s_re, s_re], -1), (1, 0, 2)))
    pw = pw.at[:, 1:2 * nlev:2, :].set(jnp.transpose(jnp.concatenate([-s_im, s_im], -1), (1, 0, 2)))
    return toep.astype(BF16), bmat.astype(BF16), emat, pw


def _proj_route_body(ya_ref, yb_ref, x_ref, w_ref, ln_ref, wr_ref, br_ref,
                     h_ref, route_ref, cnt_ref, carry_ref):
    tm = ya_ref.shape[0]
    half = w_ref.shape[0] // 2

    @pl.when(pl.program_id(0) == 0)
    def _():
        carry_ref[...] = jnp.zeros_like(carry_ref)

    mix = _bdot(ya_ref[...], w_ref[:half, :]) + _bdot(yb_ref[...], w_ref[half:, :])
    h = _layer_norm(DN_ALPHA * x_ref[...] + mix, ln_ref[0:1, :], ln_ref[1:2, :])
    h_ref[...] = h

    logits = _hdot(h, wr_ref[...]) + br_ref[0:1, :]
    lane = _iota((tm, LANES), 1).astype(F32)
    neg = -jnp.inf

    def softmax_masked(mask):
        xm = jnp.where(mask, logits, neg)
        m = jnp.max(xm, axis=-1, keepdims=True)
        e = jnp.exp(xm - m)
        return e / jnp.sum(e, axis=-1, keepdims=True)

    def top1(pm):
        m = jnp.max(pm, axis=-1, keepdims=True)
        idx = jnp.min(jnp.where(pm == m, lane, float(LANES)), axis=-1, keepdims=True)
        return m, idx

    coarse = jnp.where(lane < MOE_GROUPS, softmax_masked(lane < MOE_GROUPS), -1.0)
    p_grp, grp = top1(coarse)
    lo = MOE_GROUPS + MOE_PER_GROUP * grp
    fmask = (lane >= lo) & (lane < lo + MOE_PER_GROUP)
    fine = jnp.where(fmask, softmax_masked(fmask), -1.0)
    p1, j1 = top1(fine)
    p2, j2 = top1(jnp.where(lane == j1, -1.0, fine))
    denom = p1 + p2
    g1 = p_grp * (p1 / denom)
    g2 = p_grp * (p2 / denom)
    e1 = j1 - MOE_GROUPS
    e2 = j2 - MOE_GROUPS

    oh1 = jnp.where(lane == e1, 1.0, 0.0)
    oh2 = jnp.where(lane == e2, 1.0, 0.0)
    cnt = oh1 + oh2
    strict = jnp.where(_iota((tm, tm), 0) > _iota((tm, tm), 1), 1.0, 0.0).astype(BF16)
    before = _d(strict, cnt.astype(BF16)) + carry_ref[0:1, :]
    r1 = jnp.sum(oh1 * before, axis=-1, keepdims=True)
    r2 = jnp.sum(oh2 * before, axis=-1, keepdims=True)
    carry_ref[0:1, :] = carry_ref[0:1, :] + jnp.sum(cnt, axis=0, keepdims=True)
    cnt_ref[...] = carry_ref[...]

    out = jnp.where(lane == 0, e1, 0.0)
    out = jnp.where(lane == 1, e2, out)
    out = jnp.where(lane == 2, r1, out)
    out = jnp.where(lane == 3, r2, out)
    out = jnp.where(lane == 4, g1, out)
    out = jnp.where(lane == 5, g2, out)
    route_ref[...] = out[:, :8]


def _proj_route(ya, yb, resid, w_out, ln, wr, br, tm):
    t, d = resid.shape
    return pl.pallas_call(
        _proj_route_body,
        grid=(t // tm,),
        in_specs=[pl.BlockSpec((tm, 512), lambda i: (i, 0)),
                  pl.BlockSpec((tm, 512), lambda i: (i, 0)),
                  pl.BlockSpec((tm, d), lambda i: (i, 0)),
                  pl.BlockSpec((d, d), lambda i: (0, 0)),
                  pl.BlockSpec((8, d), lambda i: (0, 0)),
                  pl.BlockSpec((d, LANES), lambda i: (0, 0)),
                  pl.BlockSpec((8, LANES), lambda i: (0, 0))],
        out_specs=[pl.BlockSpec((tm, d), lambda i: (i, 0)),
                   pl.BlockSpec((tm, 8), lambda i: (i, 0)),
                   pl.BlockSpec((8, LANES), lambda i: (0, 0))],
        out_shape=[jax.ShapeDtypeStruct((t, d), F32),
                   jax.ShapeDtypeStruct((t, 8), F32),
                   jax.ShapeDtypeStruct((8, LANES), F32)],
        scratch_shapes=[pltpu.VMEM((8, LANES), F32)],
        compiler_params=_cparams(1),
        name="proj_ln_route",
    )(ya, yb, resid, w_out, ln, wr, br)


def _row_copy(src_ref, src_row, dst_ref, dst_row, sem):
    return pltpu.make_async_copy(src_ref.at[pl.ds(src_row, 1)], dst_ref.at[pl.ds(dst_row, 1)], sem)


def _start_all(copies):
    for n, cp in enumerate(copies):
        cp.start(priority=n % 2)


def _dispatch_body(dest_ref, h_ref, init_ref, rows_ref, hbuf_ref, lsem, ssem):
    del init_ref
    i = pl.program_id(0)
    last = pl.num_programs(0) - 1
    tm = hbuf_ref.shape[1]
    slot = i % 2

    def load(tile, s):
        return pltpu.make_async_copy(h_ref.at[pl.ds(tile * tm, tm)], hbuf_ref.at[s], lsem.at[s])

    def scatters(s):
        return [_row_copy(hbuf_ref.at[s], r, rows_ref, dest_ref[0, 0, 2 * r + q], ssem.at[s])
                for r in range(tm) for q in range(2)]

    @pl.when(i == 0)
    def _():
        load(0, 0).start()

    @pl.when(i >= 1)
    def _():
        for cp in scatters(1 - slot):
            cp.wait()

    @pl.when(i < last)
    def _():
        load(i + 1, 1 - slot).start()

    load(i, slot).wait()
    _start_all(scatters(slot))

    @pl.when(i == last)
    def _():
        for cp in scatters(slot):
            cp.wait()


def _dispatch(dest3, h, rows_init):
    t, d = h.shape
    nt, _, tm2 = dest3.shape
    tm = tm2 // 2
    return pl.pallas_call(
        _dispatch_body,
        grid=(nt,),
        in_specs=[pl.BlockSpec((1, 1, tm2), lambda i: (i, 0, 0), memory_space=pltpu.SMEM),
                  pl.BlockSpec(memory_space=pl.ANY),
                  pl.BlockSpec(memory_space=pl.ANY)],
        out_specs=pl.BlockSpec(memory_space=pl.ANY),
        out_shape=jax.ShapeDtypeStruct(rows_init.shape, F32),
        scratch_shapes=[pltpu.VMEM((2, tm, d), F32), pltpu.SemaphoreType.DMA((2,)),
                        pltpu.SemaphoreType.DMA((2,))],
        input_output_aliases={2: 0},
        compiler_params=_cparams(1),
        name="moe_dispatch",
    )(dest3, h, rows_init)


def _expert_body(first_ref, nblk_ref, nused_ref, x_ref, w1_ref, w3_ref, w2_ref, y_ref,
                 xbuf_ref, ybuf_ref, w1b_ref, w3b_ref, w2b_ref, xsem, ysem):
    e = pl.program_id(0)
    first = first_ref[e]
    nblk = nblk_ref[e]
    rows = xbuf_ref.shape[1]

    def x_copy(j, s):
        return pltpu.make_async_copy(x_ref.at[pl.ds((first + j) * rows, rows)], xbuf_ref.at[s], xsem.at[s])

    def y_copy(j, s):
        return pltpu.make_async_copy(ybuf_ref.at[s], y_ref.at[pl.ds((first + j) * rows, rows)], ysem.at[s])

    @pl.when(nblk > 0)
    def _():
        x_copy(0, 0).start()
        w1b_ref[...] = w1_ref[0].astype(BF16)
        w3b_ref[...] = w3_ref[0].astype(BF16)
        w2b_ref[...] = w2_ref[0].astype(BF16)

        def block(j, carry):
            s = j % 2

            @pl.when(j + 1 < nblk)
            def _():
                x_copy(j + 1, 1 - s).start()

            x_copy(j, s).wait()
            xb = xbuf_ref[s].astype(BF16)
            hid = _silu(_d(xb, w1b_ref[...])) * _d(xb, w3b_ref[...])

            @pl.when(j >= 2)
            def _():
                y_copy(j - 2, s).wait()

            ybuf_ref[s] = _d(hid.astype(BF16), w2b_ref[...])
            y_copy(j, s).start()
            return carry

        lax.fori_loop(0, nblk, block, 0)

        @pl.when(nblk >= 2)
        def _():
            y_copy(nblk - 2, nblk % 2).wait()

        y_copy(nblk - 1, (nblk - 1) % 2).wait()

    @pl.when(e == pl.num_programs(0) - 1)
    def _():
        ybuf_ref[0] = jnp.zeros(ybuf_ref.shape[1:], F32)
        nused = nused_ref[0]
        ntot = y_ref.shape[0] // rows

        def fill(b, carry):
            cp = pltpu.make_async_copy(ybuf_ref.at[0], y_ref.at[pl.ds(b * rows, rows)], ysem.at[0])
            cp.start()
            cp.wait()
            return carry

        lax.fori_loop(nused, ntot, fill, 0)


def _experts(first_blk, nblk, nused, x_rows, w1, w3, w2, layer):
    r, d = x_rows.shape
    hid = w1.shape[-1]
    grid_spec = pltpu.PrefetchScalarGridSpec(
        num_scalar_prefetch=3,
        grid=(N_EXPERTS,),
        in_specs=[pl.BlockSpec(memory_space=pl.ANY),
                  pl.BlockSpec((None, 1, d, hid), lambda e, *_: (layer, e, 0, 0)),
                  pl.BlockSpec((None, 1, d, hid), lambda e, *_: (layer, e, 0, 0)),
                  pl.BlockSpec((None, 1, hid, d), lambda e, *_: (layer, e, 0, 0))],
        out_specs=pl.BlockSpec(memory_space=pl.ANY),
        scratch_shapes=[pltpu.VMEM((2, MOE_BLOCK, d), F32), pltpu.VMEM((2, MOE_BLOCK, d), F32),
                        pltpu.VMEM((d, hid), BF16), pltpu.VMEM((d, hid), BF16), pltpu.VMEM((hid, d), BF16),
                        pltpu.SemaphoreType.DMA((2,)), pltpu.SemaphoreType.DMA((2,))],
    )
    return pl.pallas_call(
        _expert_body,
        grid_spec=grid_spec,
        out_shape=jax.ShapeDtypeStruct((r, d), F32),
        compiler_params=_cparams(1),
        name="moe_experts",
    )(first_blk, nblk, nused, x_rows, w1, w3, w2)


def _combine_body(dest_ref, dest_next_ref, gate_ref, h_ref, ln_ref, rows_ref, o_ref, buf_ref, sem):
    i = pl.program_id(0)
    last = pl.num_programs(0) - 1
    tm = h_ref.shape[0]

    def gather(dref, slot):
        return [_row_copy(rows_ref, dref[0, 0, 2 * r + s], buf_ref.at[slot, s], r, sem.at[slot])
                for r in range(tm) for s in range(2)]

    @pl.when(i == 0)
    def _():
        _start_all(gather(dest_ref, 0))

    _start_all(gather(dest_next_ref, (i + 1) % 2))
    slot = i % 2
    for cp in gather(dest_ref, slot):
        cp.wait()
    gate = gate_ref[...]
    y = gate[:, 4:5] * buf_ref[slot, 0] + gate[:, 5:6] * buf_ref[slot, 1]
    o_ref[...] = _layer_norm(DN_ALPHA * h_ref[...] + y, ln_ref[0:1, :], ln_ref[1:2, :])

    @pl.when(i == last)
    def _():
        for cp in gather(dest_next_ref, (i + 1) % 2):
            cp.wait()


def _combine(dest3, route, h, ln, y_rows, tm):
    t, d = h.shape
    nt = t // tm
    return pl.pallas_call(
        _combine_body,
        grid=(nt,),
        in_specs=[pl.BlockSpec((1, 1, 2 * tm), lambda i: (i, 0, 0), memory_space=pltpu.SMEM),
                  pl.BlockSpec((1, 1, 2 * tm), lambda i: (jnp.minimum(i + 1, nt - 1), 0, 0),
                               memory_space=pltpu.SMEM),
                  pl.BlockSpec((tm, 8), lambda i: (i, 0)),
                  pl.BlockSpec((tm, d), lambda i: (i, 0)),
                  pl.BlockSpec((8, d), lambda i: (0, 0)),
                  pl.BlockSpec(memory_space=pl.ANY)],
        out_specs=pl.BlockSpec((tm, d), lambda i: (i, 0)),
        out_shape=jax.ShapeDtypeStruct((t, d), F32),
        scratch_shapes=[pltpu.VMEM((2, 2, tm, d), F32), pltpu.SemaphoreType.DMA((2,))],
        compiler_params=_cparams(1),
        name="moe_combine_ln",
    )(dest3, dest3, route, h, ln, y_rows)


def _moe(h, route, counts, w1, w3, w2, layer, ln):
    t, d = h.shape
    tm = min(COMBINE_TILE, t)
    a = 2 * t
    expert = route[:, 0:2].astype(jnp.int32)
    rank = route[:, 2:4].astype(jnp.int32)
    cnt = counts[0, :N_EXPERTS].astype(jnp.int32)
    padded = (cnt + MOE_BLOCK - 1) // MOE_BLOCK * MOE_BLOCK
    pend = jnp.cumsum(padded)
    pstart = pend - padded
    dest = pstart[expert] + rank
    n_blocks = -(-a // MOE_BLOCK) + N_EXPERTS
    nused = (pend[-1:] // MOE_BLOCK).astype(jnp.int32)
    dest3 = dest.reshape(t // tm, 1, 2 * tm)
    x_rows = _dispatch(dest3, h, jnp.zeros((n_blocks * MOE_BLOCK, d), F32))
    y_rows = _experts((pstart // MOE_BLOCK).astype(jnp.int32), (padded // MOE_BLOCK).astype(jnp.int32),
                      nused, x_rows, w1, w3, w2, layer)
    return _combine(dest3, route, h, ln, y_rows, tm)


def _pad_rows(x, rows):
    return jnp.zeros((rows,) + x.shape[1:], x.dtype).at[:x.shape[0]].set(x)


def _route_weights(wg, bg, we, be):
    d = wg.shape[0]
    wr = jnp.zeros((d, LANES), F32)
    wr = wr.at[:, :MOE_GROUPS].set(wg)
    wr = wr.at[:, MOE_GROUPS:MOE_GROUPS + N_EXPERTS].set(jnp.transpose(we, (1, 0, 2)).reshape(d, N_EXPERTS))
    br = jnp.zeros((8, LANES), F32)
    br = br.at[0, :MOE_GROUPS].set(bg)
    br = br.at[0, MOE_GROUPS:MOE_GROUPS + N_EXPERTS].set(be.reshape(N_EXPERTS))
    return wr, br


def kernel(x, ab_w_in, rw_mu, rw_w0, rw_w2, rw_a0, rw_a2, rw_g2, rw_k_k, rw_k_a, rw_r_k, rw_gn_g, rw_gn_b, gla_gk_w2, gla_gk_b, gla_norm_g, ab_w_out, cd_w_in, s5_a_re, s5_a_im, s5_log_dt, s5_b_re, s5_b_im, s5_c_re, s5_c_im, s5_d, s5_glu_w, s5_glu_b, hg_lb, hg_norm_g, cd_w_out, ln1_g, ln1_b, moe_wg, moe_bg, moe_we, moe_be, moe_w1, moe_w3, moe_w2, ln2_g, ln2_b):
    batch, seq, d = x.shape
    t = batch * seq
    assert d == D_MODEL and seq % CHUNK == 0
    rows_per_seq = seq // S5_CHUNK
    assert rows_per_seq & (rows_per_seq - 1) == 0, "S5 chunk scan assumes a power-of-two chunk count"
    tm = min(256, t)
    assert t % tm == 0
    xt = x.reshape(t, d)
    ln1 = [_pad_rows(jnp.stack([ln1_g[l], ln1_b[l]]), 8) for l in range(DEPTH)]
    ln2 = [_pad_rows(jnp.stack([ln2_g[l], ln2_b[l]]), 8) for l in range(DEPTH)]

    j = 0
    w = ab_w_in[j]
    r_, wl_, k_, v_, al_, gl_ = 0, 512, 576, 1088, 1600, 1664
    gq, gk, gv, glow, ggate = 1792, 2048, 2304, 2816, 2832
    w0cols = jnp.concatenate([
        w[:, r_:r_ + 512], w[:, k_:k_ + 512], w[:, v_:v_ + 512], w[:, gv:gv + 512], w[:, ggate:ggate + 512],
        w[:, wl_:wl_ + 64], w[:, al_:al_ + 64], w[:, gl_:gl_ + 128], w[:, gq:gq + 256], w[:, gk:gk + 256],
        w[:, glow:glow + 16], jnp.zeros((d, 112), F32)], axis=1).astype(BF16)
    p0 = _matmul(xt, w0cols, tm)

    mu = rw_mu[j]
    vec = _pad_rows(jnp.stack([mu[r_:r_ + 512], mu[k_:k_ + 512], mu[v_:v_ + 512], rw_w0[j], rw_a0[j],
                               rw_k_k[j], rw_k_a[j], rw_r_k[j].reshape(-1), rw_gn_g[j], rw_gn_b[j]]), 16)
    mulo = _pad_rows(jnp.concatenate([mu[wl_:wl_ + 64], mu[al_:al_ + 64], mu[gl_:gl_ + 128]])[None], 8)
    w2p = _pad_rows(rw_w2[j], 128)
    a2p = jnp.zeros((128, 512), F32).at[64:].set(rw_a2[j])
    p0 = p0.reshape(batch, seq, -1)
    y_rw = _rwkv(p0, vec, mulo, w2p, a2p, rw_g2[j]).reshape(t, RW_WIDTH)
    y_gla = _gla(p0, _pad_rows(gla_gk_w2[j], 128), _pad_rows(gla_gk_b[j][None], 8),
                 _pad_rows(gla_norm_g[j][None], 8)).reshape(t, 512)

    wr, br = _route_weights(moe_wg[0], moe_bg[0], moe_we[0], moe_be[0])
    h, route, counts = _proj_route(y_rw, y_gla, xt, ab_w_out[j].astype(BF16), ln1[0], wr, br, tm)
    h = _moe(h, route, counts, moe_w1, moe_w3, moe_w2, 0, ln2[0])

    p1 = _matmul(h, cd_w_in[j].astype(BF16), tm)
    lb_sm = jax.nn.softmax(hg_lb.astype(F32), axis=0)
    lower = (jnp.cumsum(lb_sm, axis=0) - lb_sm[0])[1]
    y_hg = _hgrn(p1.reshape(batch, seq, -1), _pad_rows(lower[None], 8),
                 _pad_rows(hg_norm_g[j][None], 8)).reshape(t, 512)

    toep, bmat, emat, pw = _s5_tables(s5_a_re[j], s5_a_im[j], s5_log_dt[j], s5_b_re[j], s5_b_im[j],
                                      s5_c_re[j], s5_c_im[j], rows_per_seq)
    nrows = t // S5_CHUNK
    uc = p1[:, :512].reshape(nrows, S5_CHUNK, S5_GROUPS, S5_GROUP)
    uc = jnp.transpose(uc, (2, 0, 1, 3)).reshape(S5_GROUPS, nrows, S5_CHUNK * S5_GROUP).astype(BF16)
    yc = _s5_scan(uc, toep, bmat, emat, pw, rows_per_seq)
    y_ssm = jnp.transpose(yc.reshape(S5_GROUPS, nrows, S5_CHUNK, S5_GROUP), (1, 2, 0, 3)).reshape(t, 512)
    y_s5 = _s5_post(y_ssm, p1, _pad_rows(jnp.stack([s5_d[j], s5_glu_b[j]]), 8),
                    s5_glu_w[j].astype(BF16), tm)

    wr, br = _route_weights(moe_wg[1], moe_bg[1], moe_we[1], moe_be[1])
    h2, route, counts = _proj_route(y_s5, y_hg, h, cd_w_out[j].astype(BF16), ln1[1], wr, br, tm)
    out = _moe(h2, route, counts, moe_w1, moe_w3, moe_w2, 1, ln2[1])
    return out.reshape(batch, seq, d)
```

```python
import functools
import math

import numpy as np
import jax
import jax.numpy as jnp
from jax import lax
from jax.experimental import pallas as pl
from jax.experimental.pallas import tpu as pltpu

F32 = jnp.float32
BF16 = jnp.bfloat16

D_MODEL = 1024
DEPTH = 2
RW_HEAD = 64
RW_WIDTH = 512
RW_GN_EPS = 64e-5
GLA_HEADS = 4
GLA_DK = 64
GLA_DV = 128
GLA_GATE_TAU = 16.0
S5_GROUP = 16
S5_GROUPS = 32
S5_STATE = 64
HG_HEADS = 4
HG_DK = 128
CHUNK = 64
S5_CHUNK = 16
S5_SCAN_ROWS = 32
NORM_EPS = 1e-5
MOE_GROUPS = 4
MOE_PER_GROUP = 8
N_EXPERTS = 32
EXPERT_HIDDEN = 512
MOE_BLOCK = 128
COMBINE_TILE = 128
EXPERT_CHUNK = 512
ROW_TILE = 512
DN_ALPHA = (2.0 * DEPTH) ** 0.25

LANES = 128
VMEM_LIMIT = 56 * 1024 * 1024


def _cparams(n_axes=1):
    return pltpu.CompilerParams(dimension_semantics=("arbitrary",) * n_axes,
                                vmem_limit_bytes=VMEM_LIMIT)


def _d(a, b):
    return jnp.dot(a, b, preferred_element_type=F32)


def _d_nt(a, b):
    return lax.dot_general(a, b, (((1,), (1,)), ((), ())), preferred_element_type=F32)


def _d_tn(a, b):
    return lax.dot_general(a, b, (((0,), (0,)), ((), ())), preferred_element_type=F32)


def _split(a):
    hi = a.astype(BF16)
    lo = (a - hi.astype(F32)).astype(BF16)
    return hi, lo


def _split3(a):
    hi = a.astype(BF16)
    r1 = a - hi.astype(F32)
    mid = r1.astype(BF16)
    lo = (r1 - mid.astype(F32)).astype(BF16)
    return hi, mid, lo


def _bdot(a, b):
    return _d(a.astype(BF16), b.astype(BF16))


def _bdot_nt(a, b):
    return _d_nt(a.astype(BF16), b.astype(BF16))


def _bdot_tn(a, b):
    return _d_tn(a.astype(BF16), b.astype(BF16))


def _hdot_with(d, a, b):
    ah, al = _split(a)
    bh, bl = _split(b)
    return d(ah, bh) + (d(ah, bl) + d(al, bh))


def _hdot(a, b):
    return _hdot_with(_d, a, b)


def _hdot_nt(a, b):
    return _hdot_with(_d_nt, a, b)


def _hdot_tn(a, b):
    return _hdot_with(_d_tn, a, b)


def _xdot_l(a, e):
    ah, am, al = _split3(a)
    return _d(ah, e) + (_d(am, e) + _d(al, e))


def _xdot_r(e, a):
    ah, am, al = _split3(a)
    return _d(e, ah) + (_d(e, am) + _d(e, al))


def _iota(shape, dim):
    return lax.broadcasted_iota(jnp.int32, shape, dim)


def _softplus(x):
    return jnp.maximum(x, 0.0) + jnp.log1p(jnp.exp(-jnp.abs(x)))


def _sigmoid(x):
    return 1.0 / (1.0 + jnp.exp(-x))


def _silu(x):
    return x * _sigmoid(x)


def _tril_incl(n):
    return jnp.where(_iota((n, n), 0) >= _iota((n, n), 1), 1.0, 0.0).astype(BF16)


def _cumsum_rows(g):
    return _xdot_r(_tril_incl(g.shape[0]), g)


def _shift_mix(x, prev_ref, b, mu):
    c = x.shape[0]
    rolled = pltpu.roll(x, 1, 0)
    prev = jnp.where(_iota(x.shape, 0) == 0, jnp.broadcast_to(prev_ref[b, 0:1, :], x.shape), rolled)
    prev_ref[b, 0:1, :] = x[c - 1:c, :]
    return x + mu * (prev - x)


def _layer_norm(x, g, b):
    mu = jnp.mean(x, axis=-1, keepdims=True)
    xc = x - mu
    var = jnp.mean(xc * xc, axis=-1, keepdims=True)
    return xc * lax.rsqrt(var + NORM_EPS) * g + b


def _mm_body(x_ref, w_ref, o_ref):
    o_ref[...] = _d(x_ref[...].astype(BF16), w_ref[...])


def _matmul(x, w_bf16, tm):
    m, k = x.shape
    n = w_bf16.shape[1]
    return pl.pallas_call(
        _mm_body,
        grid=(m // tm,),
        in_specs=[pl.BlockSpec((tm, k), lambda i: (i, 0)),
                  pl.BlockSpec((k, n), lambda i: (0, 0))],
        out_specs=pl.BlockSpec((tm, n), lambda i: (i, 0)),
        out_shape=jax.ShapeDtypeStruct((m, n), F32),
        compiler_params=_cparams(1),
        name="in_proj",
    )(x, w_bf16)


_RV_MU_R, _RV_MU_K, _RV_MU_V, _RV_W0, _RV_A0, _RV_KK, _RV_KA, _RV_RK, _RV_GNG, _RV_GNB = range(10)


def _rwkv_body(r_ref, k_ref, v_ref, lo_ref, vec_ref, mulo_ref, w2_ref, a2_ref, g2_ref, ones_ref,
               o_ref, pr_ref, pk_ref, pv_ref, plo_ref, st_ref):
    nb, c = r_ref.shape[0], r_ref.shape[1]
    npair = RW_WIDTH // LANES

    @pl.when(pl.program_id(0) == 0)
    def _():
        pr_ref[...] = jnp.zeros_like(pr_ref)
        pk_ref[...] = jnp.zeros_like(pk_ref)
        pv_ref[...] = jnp.zeros_like(pv_ref)
        plo_ref[...] = jnp.zeros_like(plo_ref)
        st_ref[...] = jnp.zeros_like(st_ref)

    def vec(i):
        return vec_ref[i:i + 1, :]

    ones_bd = ones_ref[...]
    lane = _iota((c, LANES), 1)
    m1 = lane < RW_HEAD
    row2 = _iota((2 * c, 4 * c), 0)
    col2 = _iota((2 * c, 4 * c), 1) & (c - 1)
    tri = ((row2 < c) & (row2 > col2)) | ((row2 >= c) & ((row2 - c) >= col2))
    eye2 = jnp.where(_iota((2 * c, 2 * c), 0) == _iota((2 * c, 2 * c), 1), 1.0, 0.0)
    bd_p = (_iota((LANES, LANES), 0) >> 6) == (_iota((LANES, LANES), 1) >> 6)

    def halves(x):
        return jnp.concatenate([jnp.where(m1, x, 0.0), jnp.where(m1, 0.0, x)], axis=0)

    prep = []
    for b in range(nb):
        xr = _shift_mix(r_ref[b], pr_ref, b, vec(_RV_MU_R))
        xk = _shift_mix(k_ref[b], pk_ref, b, vec(_RV_MU_K))
        xv = _shift_mix(v_ref[b], pv_ref, b, vec(_RV_MU_V))
        xlo = _shift_mix(lo_ref[b], plo_ref, b, mulo_ref[0:1, :])
        lo_a = xlo[:, :LANES]
        lo_g = xlo[:, LANES:]
        w = -_softplus(-(vec(_RV_W0) + _bdot(jnp.tanh(lo_a), w2_ref[...]))) - 0.5
        g = -jnp.exp(w)
        a = _sigmoid(vec(_RV_A0) + _bdot(lo_a, a2_ref[...]))
        gate = _bdot(_sigmoid(lo_g), g2_ref[...])
        kk = xk * vec(_RV_KK)
        kk = kk / jnp.maximum(jnp.sqrt(_xdot_l(kk * kk, ones_bd)), 1e-12)
        k2 = xk * (1.0 + (a - 1.0) * vec(_RV_KA))
        gc = _cumsum_rows(g)
        g_last = gc[c - 1:c, :]
        e_neg = jnp.exp(-gc)
        e_end = jnp.exp(g_last - gc)
        prep.append(dict(
            xr=xr, xv=xv, k2=k2, gate=gate,
            at=-kk * jnp.exp(gc - g),
            bt=(kk * a) * e_neg, kt=k2 * e_neg, rt=xr * jnp.exp(gc),
            bh=(kk * a) * e_end, kh=k2 * e_end,
            gam=jnp.exp(g_last)))

    chains = [(b, p) for b in range(nb) for p in range(npair)]

    def part(ch, name):
        b, p = ch
        return prep[b][name][:, p * LANES:(p + 1) * LANES]

    a_ak, a_row, vv, ak, pinv = {}, {}, {}, {}, {}
    for ch in chains:
        lhs = jnp.concatenate([part(ch, 'at'), part(ch, 'rt')], axis=0)
        rhs = jnp.concatenate([halves(part(ch, 'bt')), halves(part(ch, 'kt'))], axis=0)
        aa = jnp.where(tri, _bdot_nt(lhs, rhs), 0.0)
        a_ak[ch] = aa[:c, 2 * c:].astype(BF16)
        a_row[ch] = aa[c:, :].astype(BF16)
        abd = halves(aa[:c, :2 * c])
        pinv[ch] = eye2 + abd
        ak[ch] = abd
        vv[ch] = halves(part(ch, 'xv')).astype(BF16)
    nlev = int(math.log2(c))
    for lev in range(nlev):
        for ch in chains:
            akb = ak[ch].astype(BF16)
            if lev == 0:
                ak[ch] = _d(akb, akb)
            elif lev < nlev - 1:
                out = _d(akb, jnp.concatenate([akb, pinv[ch].astype(BF16)], axis=1))
                ak[ch] = out[:, :2 * c]
                pinv[ch] = pinv[ch] + out[:, 2 * c:]
            else:
                pinv[ch] = pinv[ch] + _d(akb, pinv[ch].astype(BF16))
    x2 = {ch: _d(a_ak[ch], vv[ch]) for ch in chains}

    sts = {ch: st_ref[ch[0] * npair + ch[1]] for ch in chains}
    xs = {ch: _d_nt(jnp.concatenate([part(ch, 'at'), part(ch, 'rt')], axis=0).astype(BF16),
                    sts[ch].astype(BF16)) for ch in chains}
    us = {}
    for ch in chains:
        u2 = _d(pinv[ch].astype(BF16), halves(xs[ch][:c] + x2[ch]).astype(BF16))
        us[ch] = u2[:c] + u2[c:]
    ys = {}
    for ch in chains:
        b, p = ch
        u = us[ch]
        ys[ch] = xs[ch][c:] + _d(a_row[ch], jnp.concatenate([halves(u).astype(BF16), vv[ch]], axis=0))
        upd = _d_tn(jnp.concatenate([u, part(ch, 'xv')], axis=0).astype(BF16),
                    jnp.concatenate([part(ch, 'bh'), part(ch, 'kh')], axis=0).astype(BF16))
        st_ref[b * npair + p] = sts[ch] * part(ch, 'gam') + jnp.where(bd_p, upd, 0.0)

    inv_n = 1.0 / RW_HEAD
    for b in range(nb):
        y = jnp.concatenate([ys[(b, p)] for p in range(npair)], axis=1)
        mean = _xdot_l(y, ones_bd) * inv_n
        yc = y - mean
        var = _xdot_l(yc * yc, ones_bd) * inv_n
        yn = yc * lax.rsqrt(var + RW_GN_EPS) * vec(_RV_GNG) + vec(_RV_GNB)
        bonus = _xdot_l(prep[b]['xr'] * prep[b]['k2'] * vec(_RV_RK), ones_bd) * prep[b]['xv']
        o_ref[b] = (yn + bonus) * prep[b]['gate']


def _rwkv(p0, vec, mulo, w2p, a2p, g2):
    batch, seq, _ = p0.shape
    nc = seq // CHUNK
    c = CHUNK
    ones_bd = jnp.asarray(np.kron(np.eye(RW_WIDTH // RW_HEAD), np.ones((RW_HEAD, RW_HEAD))), BF16)

    def col(j, width):
        return pl.BlockSpec((batch, c, width), lambda i: (0, i, j))

    def full(shape):
        return pl.BlockSpec(shape, lambda i: (0,) * len(shape))

    return pl.pallas_call(
        _rwkv_body,
        grid=(nc,),
        in_specs=[col(0, 512), col(1, 512), col(2, 512), col(10, 256),
                  full((16, 512)), full((8, 256)), full((128, 512)), full((128, 512)),
                  full((128, 512)), full((512, 512))],
        out_specs=pl.BlockSpec((batch, c, 512), lambda i: (0, i, 0)),
        out_shape=jax.ShapeDtypeStruct((batch, seq, RW_WIDTH), F32),
        scratch_shapes=[pltpu.VMEM((batch, 8, 512), F32), pltpu.VMEM((batch, 8, 512), F32),
                        pltpu.VMEM((batch, 8, 512), F32), pltpu.VMEM((batch, 8, 256), F32),
                        pltpu.VMEM((batch * RW_WIDTH // LANES, LANES, LANES), F32)],
        compiler_params=_cparams(1),
        name="rwkv7",
    )(p0, p0, p0, p0, vec, mulo, w2p, a2p, g2, ones_bd)


def _gla_core(qs, ks, vs, gs, st_ref, heads_per_block):
    nb = len(qs)
    c = qs[0].shape[0]
    hpb = heads_per_block
    nblk = qs[0].shape[1] // LANES
    dk_shift = int(math.log2(LANES // hpb))
    row = _iota(qs[0].shape, 0)

    prep = []
    for q, k, g in zip(qs, ks, gs):
        b = _cumsum_rows(g)

        def brow(i, b=b):
            return jnp.broadcast_to(b[i:i + 1, :], b.shape)

        b15, b31, b47, blast = brow(15), brow(31), brow(47), brow(c - 1)
        ref_b = jnp.where(row < 32, b15, b47)
        ref_d = jnp.where(row < 16, 0.0, jnp.where(row < 32, b15, jnp.where(row < 48, b31, b47)))
        prep.append(dict(
            q_a=q * jnp.exp(jnp.minimum(b - b31, 0.0)), k_a=k * jnp.exp(jnp.minimum(b31 - b, 0.0)),
            q_b=q * jnp.exp(jnp.minimum(b - ref_b, 0.0)), k_b=k * jnp.exp(jnp.minimum(ref_b - b, 0.0)),
            q_d=q * jnp.exp(b - ref_d), k_d=k * jnp.exp(ref_d - b),
            q_i=q * jnp.exp(b), k_s=k * jnp.exp(blast - b), gam=jnp.exp(b[c - 1:c, :])))

    ri = _iota((hpb * c, c), 0) & (c - 1)
    ci = _iota((hpb * c, c), 1)
    mask_a = (ri >= 32) & (ci < 32)
    mask_b = ((ri >> 5) == (ci >> 5)) & (((ri >> 4) & 1) == 1) & (((ci >> 4) & 1) == 0)
    mask_d = ((ri >> 4) == (ci >> 4)) & (ri >= ci)
    lane = _iota((c, LANES), 1)
    bd = (_iota((hpb * LANES, LANES), 0) >> 7) == (_iota((hpb * LANES, LANES), 1) >> dk_shift)

    def heads_rows(x):
        if hpb == 1:
            return x
        return jnp.concatenate([jnp.where((lane >> dk_shift) == h, x, 0.0) for h in range(hpb)], axis=0)

    chains = [(bi, blk) for bi in range(nb) for blk in range(nblk)]

    def part(ch, name):
        bi, blk = ch
        return prep[bi][name][:, blk * LANES:(blk + 1) * LANES]

    v_p = {ch: vs[ch[0]][:, ch[1] * hpb * LANES:(ch[1] + 1) * hpb * LANES].astype(BF16) for ch in chains}
    probs = {}
    for ch in chains:
        s_a = _bdot_nt(heads_rows(part(ch, 'q_a')), part(ch, 'k_a'))
        s_b = _bdot_nt(heads_rows(part(ch, 'q_b')), part(ch, 'k_b'))
        s_d = _bdot_nt(heads_rows(part(ch, 'q_d')), part(ch, 'k_d'))
        probs[ch] = (jnp.where(mask_a, s_a, 0.0) + jnp.where(mask_b, s_b, 0.0)
                     + jnp.where(mask_d, s_d, 0.0)).astype(BF16)
    outs = {}
    for ch in chains:
        pv = _d(probs[ch], v_p[ch])
        o = pv[:c]
        for h in range(1, hpb):
            o = jnp.where((_iota(o.shape, 1) >> 7) == h, pv[h * c:(h + 1) * c], o)
        si = ch[0] * nblk + ch[1]
        sp = st_ref[si]
        outs[ch] = o + _bdot_nt(part(ch, 'q_i'), sp)
        upd = _d_tn(v_p[ch], part(ch, 'k_s').astype(BF16))
        st_ref[si] = sp * part(ch, 'gam') + jnp.where(bd, upd, 0.0)
    return [jnp.concatenate([outs[(bi, blk)] for blk in range(nblk)], axis=1) for bi in range(nb)]


def _gated_rmsnorm(o, gate, norm_g):
    nh = o.shape[1] // LANES
    outs = []
    for h in range(nh):
        sl = slice(h * LANES, (h + 1) * LANES)
        oh = o[:, sl]
        ms = jnp.mean(oh * oh, axis=-1, keepdims=True)
        outs.append(oh * lax.rsqrt(ms + NORM_EPS) * norm_g * _silu(gate[:, sl]))
    return jnp.concatenate(outs, axis=1)


def _gla_body(q_ref, k_ref, v_ref, gate_ref, gk_ref, w2_ref, vec_ref, ng_ref, o_ref, st_ref):
    nb = q_ref.shape[0]

    @pl.when(pl.program_id(0) == 0)
    def _():
        st_ref[...] = jnp.zeros_like(st_ref)

    gs = [-_softplus(-(_hdot(gk_ref[b], w2_ref[...]) + vec_ref[0:1, :])) * (1.0 / GLA_GATE_TAU)
          for b in range(nb)]
    qs = [q_ref[b] * (GLA_DK ** -0.5) for b in range(nb)]
    os_ = _gla_core(qs, [k_ref[b] for b in range(nb)], [v_ref[b] for b in range(nb)], gs, st_ref, 2)
    for b in range(nb):
        o_ref[b] = _gated_rmsnorm(os_[b], gate_ref[b], ng_ref[0:1, :])


def _seq_specs(batch, c):
    def col(j, width):
        return pl.BlockSpec((batch, c, width), lambda i: (0, i, j))

    def full(shape):
        return pl.BlockSpec(shape, lambda i: (0,) * len(shape))

    return col, full


def _gla(p0, gk_w2p, gk_b, norm_g):
    batch, seq, _ = p0.shape
    c = CHUNK
    col, full = _seq_specs(batch, c)
    return pl.pallas_call(
        _gla_body,
        grid=(seq // c,),
        in_specs=[col(11, 256), col(12, 256), col(3, 512), col(4, 512), col(26, 128),
                  full((128, 256)), full((8, 256)), full((8, 128))],
        out_specs=pl.BlockSpec((batch, c, 512), lambda i: (0, i, 0)),
        out_shape=jax.ShapeDtypeStruct((batch, seq, 512), F32),
        scratch_shapes=[pltpu.VMEM((batch * 2, 2 * LANES, LANES), F32)],
        compiler_params=_cparams(1),
        name="gla",
    )(p0, p0, p0, p0, p0, gk_w2p, gk_b, norm_g)


def _hgrn_body(q_ref, f_ref, i_ref, gate_ref, lb_ref, ng_ref, o_ref, st_ref):
    nb = q_ref.shape[0]

    @pl.when(pl.program_id(0) == 0)
    def _():
        st_ref[...] = jnp.zeros_like(st_ref)

    lb = lb_ref[0:1, :]
    qs, ks, gs = [], [], []
    for b in range(nb):
        f = f_ref[b]
        gs.append(jnp.log(lb + (1.0 - lb) * _sigmoid(f)))
        ks.append((1.0 - lb) * _sigmoid(-f))
        qs.append(_silu(q_ref[b]))
    os_ = _gla_core(qs, ks, [i_ref[b] for b in range(nb)], gs, st_ref, 1)
    for b in range(nb):
        o_ref[b] = _gated_rmsnorm(os_[b], gate_ref[b], ng_ref[0:1, :])


def _hgrn(p1, lb, norm_g):
    batch, seq, _ = p1.shape
    c = CHUNK
    col, full = _seq_specs(batch, c)
    return pl.pallas_call(
        _hgrn_body,
        grid=(seq // c,),
        in_specs=[col(1, 512), col(2, 512), col(3, 512), col(4, 512), full((8, 512)), full((8, 128))],
        out_specs=pl.BlockSpec((batch, c, 512), lambda i: (0, i, 0)),
        out_shape=jax.ShapeDtypeStruct((batch, seq, 512), F32),
        scratch_shapes=[pltpu.VMEM((batch * HG_HEADS, LANES, LANES), F32)],
        compiler_params=_cparams(1),
        name="hgrn2",
    )(p1, p1, p1, p1, lb, norm_g)


def _s5_body(u_ref, toep_ref, bm_ref, e_ref, pw_ref, o_ref, *, rows_per_seq):
    u = u_ref[0]
    rows = u.shape[0]
    s = _d(u, bm_ref[0])
    rin = _iota((rows, LANES), 0) & (rows_per_seq - 1)
    pw = pw_ref[0]
    h = s
    nlev = int(math.log2(rows_per_seq))
    for lev in range(nlev):
        sh = 1 << lev
        hs = jnp.where(rin >= sh, pltpu.roll(h, sh, 0), 0.0)
        hs_sw = pltpu.roll(hs, S5_STATE, 1)
        h = h + pw[2 * lev:2 * lev + 1, :] * hs + pw[2 * lev + 1:2 * lev + 2, :] * hs_sw
    hprev = jnp.where(rin >= 1, pltpu.roll(h, 1, 0), 0.0)
    o_ref[0] = _d(u, toep_ref[0]) + _hdot(hprev, e_ref[0])


def _s5_scan(uc, toep, bmat, emat, pw, rows_per_seq):
    ng, rows, width = uc.shape
    return pl.pallas_call(
        functools.partial(_s5_body, rows_per_seq=rows_per_seq),
        grid=(ng,),
        in_specs=[pl.BlockSpec((1, rows, width), lambda g: (g, 0, 0)),
                  pl.BlockSpec((1, width, width), lambda g: (g, 0, 0)),
                  pl.BlockSpec((1, width, LANES), lambda g: (g, 0, 0)),
                  pl.BlockSpec((1, LANES, width), lambda g: (g, 0, 0)),
                  pl.BlockSpec((1, S5_SCAN_ROWS, LANES), lambda g: (g, 0, 0))],
        out_specs=pl.BlockSpec((1, rows, width), lambda g: (g, 0, 0)),
        out_shape=jax.ShapeDtypeStruct((ng, rows, width), F32),
        compiler_params=_cparams(1),
        name="s5_scan",
    )(uc, toep, bmat, emat, pw)


def _s5_post_body(y_ref, u_ref, vec_ref, w_ref, o_ref):
    y = y_ref[...] + vec_ref[0:1, :] * u_ref[...]
    y = 0.5 * y * (1.0 + jnp.tanh(math.sqrt(2.0 / math.pi) * (y + 0.044715 * (y * y * y))))
    o_ref[...] = y * _sigmoid(_bdot(y, w_ref[...]) + vec_ref[1:2, :])


def _s5_post(y_ssm, p1, vec, glu_w, tm):
    t = y_ssm.shape[0]
    return pl.pallas_call(
        _s5_post_body,
        grid=(t // tm,),
        in_specs=[pl.BlockSpec((tm, 512), lambda i: (i, 0)),
                  pl.BlockSpec((tm, 512), lambda i: (i, 0)),
                  pl.BlockSpec((8, 512), lambda i: (0, 0)),
                  pl.BlockSpec((512, 512), lambda i: (0, 0))],
        out_specs=pl.BlockSpec((tm, 512), lambda i: (i, 0)),
        out_shape=jax.ShapeDtypeStruct((t, 512), F32),
        compiler_params=_cparams(1),
        name="s5_post",
    )(y_ssm, p1, vec, glu_w)


def _s5_tables(a_re, a_im, log_dt, b_re, b_im, c_re, c_im, rows_per_seq):
    c = S5_CHUNK
    lam_re = jnp.minimum(a_re, -1e-4)
    lam_im = a_im
    dt = jnp.exp(log_dt)[:, None]
    mag = jnp.exp(lam_re * dt)
    abar_re = mag * jnp.cos(lam_im * dt)
    abar_im = mag * jnp.sin(lam_im * dt)
    den = lam_re * lam_re + lam_im * lam_im
    num_re = abar_re - 1.0
    z_re = (num_re * lam_re + abar_im * lam_im) / den
    z_im = (abar_im * lam_re - num_re * lam_im) / den

    def power(n):
        n = jnp.asarray(n, F32)[..., None, None]
        m = jnp.exp(n * (lam_re * dt))
        return m * jnp.cos(n * (lam_im * dt)), m * jnp.sin(n * (lam_im * dt))

    def cmul(ar, ai, br, bi):
        return ar * br - ai * bi, ar * bi + ai * br

    tau = jnp.arange(c)
    p_re, p_im = power(tau)
    zb_re, zb_im = cmul(z_re[..., None], z_im[..., None], b_re, b_im)
    cp_re, cp_im = cmul(c_re[None], c_im[None], p_re[:, :, None, :], p_im[:, :, None, :])
    hi = lax.Precision.HIGHEST
    kern = (jnp.einsum('tgon,gni->gtoi', cp_re, zb_re, precision=hi)
            - jnp.einsum('tgon,gni->gtoi', cp_im, zb_im, precision=hi))
    tt = jnp.arange(c)[None, :] - jnp.arange(c)[:, None]
    kt = jnp.where((tt >= 0)[None, :, :, None, None], kern[:, jnp.clip(tt, 0, c - 1)], 0.0)
    toep = jnp.transpose(kt, (0, 1, 4, 2, 3)).reshape(S5_GROUPS, c * S5_GROUP, c * S5_GROUP)
    q_re, q_im = power(c - 1 - tau)
    bm_re, bm_im = cmul(q_re[..., None], q_im[..., None], zb_re[None], zb_im[None])
    bmat = jnp.concatenate([jnp.transpose(bm_re, (1, 0, 3, 2)), jnp.transpose(bm_im, (1, 0, 3, 2))],
                           axis=-1).reshape(S5_GROUPS, c * S5_GROUP, 2 * S5_STATE)
    r_re, r_im = power(tau + 1)
    e_re, e_im = cmul(c_re[None], c_im[None], r_re[:, :, None, :], r_im[:, :, None, :])
    emat = jnp.concatenate([jnp.transpose(e_re, (1, 3, 0, 2)), -jnp.transpose(e_im, (1, 3, 0, 2))],
                           axis=1).reshape(S5_GROUPS, 2 * S5_STATE, c * S5_GROUP)
    nlev = int(math.log2(rows_per_seq))
    assert 2 * nlev <= S5_SCAN_ROWS
    s_re, s_im = power(c * (2 ** jnp.arange(nlev)))
    pw = jnp.zeros((S5_GROUPS, S5_SCAN_ROWS, 2 * S5_STATE), F32)
    pw = pw.at[:, 0:2 * nlev:2, :].set(jnp.transpose(jnp.concatenate([s_re, s_re], -1), (1, 0, 2)))
    pw = pw.at[:, 1:2 * nlev:2, :].set(jnp.transpose(jnp.concatenate([-s_im, s_im], -1), (1, 0, 2)))
    return toep.astype(BF16), bmat.astype(BF16), emat, pw


def _proj_route_body(ya_ref, yb_ref, x_ref, w_ref, ln_ref, wr_ref, br_ref,
                     h_ref, route_ref, cnt_ref, carry_ref):
    tm = ya_ref.shape[0]
    half = w_ref.shape[0] // 2

    @pl.when(pl.program_id(0) == 0)
    def _():
        carry_ref[...] = jnp.zeros_like(carry_ref)

    mix = _bdot(ya_ref[...], w_ref[:half, :]) + _bdot(yb_ref[...], w_ref[half:, :])
    h = _layer_norm(DN_ALPHA * x_ref[...] + mix, ln_ref[0:1, :], ln_ref[1:2, :])
    h_ref[...] = h

    logits = _hdot(h, wr_ref[...]) + br_ref[0:1, :]
    lane = _iota((tm, LANES), 1).astype(F32)
    neg = -jnp.inf

    def softmax_masked(mask):
        xm = jnp.where(mask, logits, neg)
        m = jnp.max(xm, axis=-1, keepdims=True)
        e = jnp.exp(xm - m)
        return e / jnp.sum(e, axis=-1, keepdims=True)

    def top1(pm):
        m = jnp.max(pm, axis=-1, keepdims=True)
        idx = jnp.min(jnp.where(pm == m, lane, float(LANES)), axis=-1, keepdims=True)
        return m, idx

    coarse = jnp.where(lane < MOE_GROUPS, softmax_masked(lane < MOE_GROUPS), -1.0)
    p_grp, grp = top1(coarse)
    lo = MOE_GROUPS + MOE_PER_GROUP * grp
    fmask = (lane >= lo) & (lane < lo + MOE_PER_GROUP)
    fine = jnp.where(fmask, softmax_masked(fmask), -1.0)
    p1, j1 = top1(fine)
    p2, j2 = top1(jnp.where(lane == j1, -1.0, fine))
    denom = p1 + p2
    g1 = p_grp * (p1 / denom)
    g2 = p_grp * (p2 / denom)
    e1 = j1 - MOE_GROUPS
    e2 = j2 - MOE_GROUPS

    oh1 = jnp.where(lane == e1, 1.0, 0.0)
    oh2 = jnp.where(lane == e2, 1.0, 0.0)
    cnt = oh1 + oh2
    strict = jnp.where(_iota((tm, tm), 0) > _iota((tm, tm), 1), 1.0, 0.0).astype(BF16)
    before = _d(strict, cnt.astype(BF16)) + carry_ref[0:1, :]
    r1 = jnp.sum(oh1 * before, axis=-1, keepdims=True)
    r2 = jnp.sum(oh2 * before, axis=-1, keepdims=True)
    carry_ref[0:1, :] = carry_ref[0:1, :] + jnp.sum(cnt, axis=0, keepdims=True)
    cnt_ref[...] = carry_ref[...]

    out = jnp.where(lane == 0, e1, 0.0)
    out = jnp.where(lane == 1, e2, out)
    out = jnp.where(lane == 2, r1, out)
    out = jnp.where(lane == 3, r2, out)
    out = jnp.where(lane == 4, g1, out)
    out = jnp.where(lane == 5, g2, out)
    route_ref[...] = out[:, :8]


def _proj_route(ya, yb, resid, w_out, ln, wr, br, tm):
    t, d = resid.shape
    return pl.pallas_call(
        _proj_route_body,
        grid=(t // tm,),
        in_specs=[pl.BlockSpec((tm, 512), lambda i: (i, 0)),
                  pl.BlockSpec((tm, 512), lambda i: (i, 0)),
                  pl.BlockSpec((tm, d), lambda i: (i, 0)),
                  pl.BlockSpec((d, d), lambda i: (0, 0)),
                  pl.BlockSpec((8, d), lambda i: (0, 0)),
                  pl.BlockSpec((d, LANES), lambda i: (0, 0)),
                  pl.BlockSpec((8, LANES), lambda i: (0, 0))],
        out_specs=[pl.BlockSpec((tm, d), lambda i: (i, 0)),
                   pl.BlockSpec((tm, 8), lambda i: (i, 0)),
                   pl.BlockSpec((8, LANES), lambda i: (0, 0))],
        out_shape=[jax.ShapeDtypeStruct((t, d), F32),
                   jax.ShapeDtypeStruct((t, 8), F32),
                   jax.ShapeDtypeStruct((8, LANES), F32)],
        scratch_shapes=[pltpu.VMEM((8, LANES), F32)],
        compiler_params=_cparams(1),
        name="proj_ln_route",
    )(ya, yb, resid, w_out, ln, wr, br)


def _row_copy(src_ref, src_row, dst_ref, dst_row, sem):
    return pltpu.make_async_copy(src_ref.at[pl.ds(src_row, 1)], dst_ref.at[pl.ds(dst_row, 1)], sem)


def _start_all(copies):
    for n, cp in enumerate(copies):
        cp.start(priority=n % 2)


def _dispatch_body(dest_ref, h_ref, init_ref, rows_ref, hbuf_ref, lsem, ssem):
    del init_ref
    i = pl.program_id(0)
    last = pl.num_programs(0) - 1
    tm = hbuf_ref.shape[1]
    slot = i % 2

    def load(tile, s):
        return pltpu.make_async_copy(h_ref.at[pl.ds(tile * tm, tm)], hbuf_ref.at[s], lsem.at[s])

    def scatters(s):
        return [_row_copy(hbuf_ref.at[s], r, rows_ref, dest_ref[0, 0, 2 * r + q], ssem.at[s])
                for r in range(tm) for q in range(2)]

    @pl.when(i == 0)
    def _():
        load(0, 0).start()

    @pl.when(i >= 1)
    def _():
        for cp in scatters(1 - slot):
            cp.wait()

    @pl.when(i < last)
    def _():
        load(i + 1, 1 - slot).start()

    load(i, slot).wait()
    _start_all(scatters(slot))

    @pl.when(i == last)
    def _():
        for cp in scatters(slot):
            cp.wait()


def _dispatch(dest3, h, rows_init):
    t, d = h.shape
    nt, _, tm2 = dest3.shape
    tm = tm2 // 2
    return pl.pallas_call(
        _dispatch_body,
        grid=(nt,),
        in_specs=[pl.BlockSpec((1, 1, tm2), lambda i: (i, 0, 0), memory_space=pltpu.SMEM),
                  pl.BlockSpec(memory_space=pl.ANY),
                  pl.BlockSpec(memory_space=pl.ANY)],
        out_specs=pl.BlockSpec(memory_space=pl.ANY),
        out_shape=jax.ShapeDtypeStruct(rows_init.shape, F32),
        scratch_shapes=[pltpu.VMEM((2, tm, d), F32), pltpu.SemaphoreType.DMA((2,)),
                        pltpu.SemaphoreType.DMA((2,))],
        input_output_aliases={2: 0},
        compiler_params=_cparams(1),
        name="moe_dispatch",
    )(dest3, h, rows_init)


def _expert_body(first_ref, nblk_ref, nused_ref, x_ref, w1_ref, w3_ref, w2_ref, y_ref,
                 xbuf_ref, ybuf_ref, w1b_ref, w3b_ref, w2b_ref, xsem, ysem):
    e = pl.program_id(0)
    row0 = first_ref[e] * MOE_BLOCK
    nblk = nblk_ref[e]
    big = xbuf_ref.shape[1]
    per_big = big // MOE_BLOCK

    def stream(start_row, count, size):
        def x_copy(j, s):
            src = x_ref.at[pl.ds(pl.multiple_of(start_row + j * size, MOE_BLOCK), size)]
            return pltpu.make_async_copy(src, xbuf_ref.at[s, pl.ds(0, size)], xsem.at[s])

        def y_copy(j, s):
            dst = y_ref.at[pl.ds(pl.multiple_of(start_row + j * size, MOE_BLOCK), size)]
            return pltpu.make_async_copy(ybuf_ref.at[s, pl.ds(0, size)], dst, ysem.at[s])

        @pl.when(count > 0)
        def _():
            x_copy(0, 0).start()

            def chunk(j, carry):
                s = j % 2

                @pl.when(j + 1 < count)
                def _():
                    x_copy(j + 1, 1 - s).start()

                x_copy(j, s).wait()
                xb = xbuf_ref[s, pl.ds(0, size), :].astype(BF16)
                hid = _silu(_d(xb, w1b_ref[...])) * _d(xb, w3b_ref[...])

                @pl.when(j >= 2)
                def _():
                    y_copy(j - 2, s).wait()

                ybuf_ref[s, pl.ds(0, size), :] = _d(hid.astype(BF16), w2b_ref[...])
                y_copy(j, s).start()
                return carry

            lax.fori_loop(0, count, chunk, 0)

            @pl.when(count >= 2)
            def _():
                y_copy(count - 2, count % 2).wait()

            y_copy(count - 1, (count - 1) % 2).wait()

    @pl.when(nblk > 0)
    def _():
        w1b_ref[...] = w1_ref[0].astype(BF16)
        w3b_ref[...] = w3_ref[0].astype(BF16)
        w2b_ref[...] = w2_ref[0].astype(BF16)
        nbig = nblk // per_big
        stream(row0, nbig, big)
        stream(row0 + nbig * big, nblk - nbig * per_big, MOE_BLOCK)

    @pl.when(e == pl.num_programs(0) - 1)
    def _():
        ybuf_ref[0, pl.ds(0, MOE_BLOCK), :] = jnp.zeros((MOE_BLOCK, ybuf_ref.shape[2]), F32)
        nused = nused_ref[0]
        ntot = y_ref.shape[0] // MOE_BLOCK

        def fill(b, carry):
            cp = pltpu.make_async_copy(ybuf_ref.at[0, pl.ds(0, MOE_BLOCK)],
                                       y_ref.at[pl.ds(b * MOE_BLOCK, MOE_BLOCK)], ysem.at[0])
            cp.start()
            cp.wait()
            return carry

        lax.fori_loop(nused, ntot, fill, 0)


def _experts(first_blk, nblk, nused, x_rows, w1, w3, w2, layer):
    r, d = x_rows.shape
    hid = w1.shape[-1]
    grid_spec = pltpu.PrefetchScalarGridSpec(
        num_scalar_prefetch=3,
        grid=(N_EXPERTS,),
        in_specs=[pl.BlockSpec(memory_space=pl.ANY),
                  pl.BlockSpec((None, 1, d, hid), lambda e, *_: (layer, e, 0, 0)),
                  pl.BlockSpec((None, 1, d, hid), lambda e, *_: (layer, e, 0, 0)),
                  pl.BlockSpec((None, 1, hid, d), lambda e, *_: (layer, e, 0, 0))],
        out_specs=pl.BlockSpec(memory_space=pl.ANY),
        scratch_shapes=[pltpu.VMEM((2, EXPERT_CHUNK, d), F32), pltpu.VMEM((2, EXPERT_CHUNK, d), F32),
                        pltpu.VMEM((d, hid), BF16), pltpu.VMEM((d, hid), BF16), pltpu.VMEM((hid, d), BF16),
                        pltpu.SemaphoreType.DMA((2,)), pltpu.SemaphoreType.DMA((2,))],
    )
    return pl.pallas_call(
        _expert_body,
        grid_spec=grid_spec,
        out_shape=jax.ShapeDtypeStruct((r, d), F32),
        compiler_params=_cparams(1),
        name="moe_experts",
    )(first_blk, nblk, nused, x_rows, w1, w3, w2)


def _combine_body(dest_ref, dest_next_ref, gate_ref, h_ref, ln_ref, rows_ref, o_ref, buf_ref, sem):
    i = pl.program_id(0)
    last = pl.num_programs(0) - 1
    tm = h_ref.shape[0]

    def gather(dref, slot):
        return [_row_copy(rows_ref, dref[0, 0, 2 * r + s], buf_ref.at[slot, s], r, sem.at[slot])
                for r in range(tm) for s in range(2)]

    @pl.when(i == 0)
    def _():
        _start_all(gather(dest_ref, 0))

    _start_all(gather(dest_next_ref, (i + 1) % 2))
    slot = i % 2
    for cp in gather(dest_ref, slot):
        cp.wait()
    gate = gate_ref[...]
    y = gate[:, 4:5] * buf_ref[slot, 0] + gate[:, 5:6] * buf_ref[slot, 1]
    o_ref[...] = _layer_norm(DN_ALPHA * h_ref[...] + y, ln_ref[0:1, :], ln_ref[1:2, :])

    @pl.when(i == last)
    def _():
        for cp in gather(dest_next_ref, (i + 1) % 2):
            cp.wait()


def _combine(dest3, route, h, ln, y_rows, tm):
    t, d = h.shape
    nt = t // tm
    return pl.pallas_call(
        _combine_body,
        grid=(nt,),
        in_specs=[pl.BlockSpec((1, 1, 2 * tm), lambda i: (i, 0, 0), memory_space=pltpu.SMEM),
                  pl.BlockSpec((1, 1, 2 * tm), lambda i: (jnp.minimum(i + 1, nt - 1), 0, 0),
                               memory_space=pltpu.SMEM),
                  pl.BlockSpec((tm, 8), lambda i: (i, 0)),
                  pl.BlockSpec((tm, d), lambda i: (i, 0)),
                  pl.BlockSpec((8, d), lambda i: (0, 0)),
                  pl.BlockSpec(memory_space=pl.ANY)],
        out_specs=pl.BlockSpec((tm, d), lambda i: (i, 0)),
        out_shape=jax.ShapeDtypeStruct((t, d), F32),
        scratch_shapes=[pltpu.VMEM((2, 2, tm, d), F32), pltpu.SemaphoreType.DMA((2,))],
        compiler_params=_cparams(1),
        name="moe_combine_ln",
    )(dest3, dest3, route, h, ln, y_rows)


def _moe(h, route, counts, w1, w3, w2, layer, ln):
    t, d = h.shape
    tm = min(COMBINE_TILE, t)
    a = 2 * t
    expert = route[:, 0:2].astype(jnp.int32)
    rank = route[:, 2:4].astype(jnp.int32)
    cnt = counts[0, :N_EXPERTS].astype(jnp.int32)
    padded = (cnt + MOE_BLOCK - 1) // MOE_BLOCK * MOE_BLOCK
    pend = jnp.cumsum(padded)
    pstart = pend - padded
    dest = pstart[expert] + rank
    n_blocks = -(-a // MOE_BLOCK) + N_EXPERTS
    nused = (pend[-1:] // MOE_BLOCK).astype(jnp.int32)
    dest3 = dest.reshape(t // tm, 1, 2 * tm)
    x_rows = _dispatch(dest3, h, jnp.zeros((n_blocks * MOE_BLOCK, d), F32))
    y_rows = _experts((pstart // MOE_BLOCK).astype(jnp.int32), (padded // MOE_BLOCK).astype(jnp.int32),
                      nused, x_rows, w1, w3, w2, layer)
    return _combine(dest3, route, h, ln, y_rows, tm)


def _pad_rows(x, rows):
    return jnp.zeros((rows,) + x.shape[1:], x.dtype).at[:x.shape[0]].set(x)


def _route_weights(wg, bg, we, be):
    d = wg.shape[0]
    wr = jnp.zeros((d, LANES), F32)
    wr = wr.at[:, :MOE_GROUPS].set(wg)
    wr = wr.at[:, MOE_GROUPS:MOE_GROUPS + N_EXPERTS].set(jnp.transpose(we, (1, 0, 2)).reshape(d, N_EXPERTS))
    br = jnp.zeros((8, LANES), F32)
    br = br.at[0, :MOE_GROUPS].set(bg)
    br = br.at[0, MOE_GROUPS:MOE_GROUPS + N_EXPERTS].set(be.reshape(N_EXPERTS))
    return wr, br


def kernel(x, ab_w_in, rw_mu, rw_w0, rw_w2, rw_a0, rw_a2, rw_g2, rw_k_k, rw_k_a, rw_r_k, rw_gn_g, rw_gn_b, gla_gk_w2, gla_gk_b, gla_norm_g, ab_w_out, cd_w_in, s5_a_re, s5_a_im, s5_log_dt, s5_b_re, s5_b_im, s5_c_re, s5_c_im, s5_d, s5_glu_w, s5_glu_b, hg_lb, hg_norm_g, cd_w_out, ln1_g, ln1_b, moe_wg, moe_bg, moe_we, moe_be, moe_w1, moe_w3, moe_w2, ln2_g, ln2_b):
    batch, seq, d = x.shape
    t = batch * seq
    assert d == D_MODEL and seq % CHUNK == 0
    rows_per_seq = seq // S5_CHUNK
    assert rows_per_seq & (rows_per_seq - 1) == 0, "S5 chunk scan assumes a power-of-two chunk count"
    tm = min(ROW_TILE, t)
    assert t % tm == 0
    xt = x.reshape(t, d)
    ln1 = [_pad_rows(jnp.stack([ln1_g[l], ln1_b[l]]), 8) for l in range(DEPTH)]
    ln2 = [_pad_rows(jnp.stack([ln2_g[l], ln2_b[l]]), 8) for l in range(DEPTH)]

    j = 0
    w = ab_w_in[j]
    r_, wl_, k_, v_, al_, gl_ = 0, 512, 576, 1088, 1600, 1664
    gq, gk, gv, glow, ggate = 1792, 2048, 2304, 2816, 2832
    w0cols = jnp.concatenate([
        w[:, r_:r_ + 512], w[:, k_:k_ + 512], w[:, v_:v_ + 512], w[:, gv:gv + 512], w[:, ggate:ggate + 512],
        w[:, wl_:wl_ + 64], w[:, al_:al_ + 64], w[:, gl_:gl_ + 128], w[:, gq:gq + 256], w[:, gk:gk + 256],
        w[:, glow:glow + 16], jnp.zeros((d, 112), F32)], axis=1).astype(BF16)
    p0 = _matmul(xt, w0cols, tm)

    mu = rw_mu[j]
    vec = _pad_rows(jnp.stack([mu[r_:r_ + 512], mu[k_:k_ + 512], mu[v_:v_ + 512], rw_w0[j], rw_a0[j],
                               rw_k_k[j], rw_k_a[j], rw_r_k[j].reshape(-1), rw_gn_g[j], rw_gn_b[j]]), 16)
    mulo = _pad_rows(jnp.concatenate([mu[wl_:wl_ + 64], mu[al_:al_ + 64], mu[gl_:gl_ + 128]])[None], 8)
    w2p = _pad_rows(rw_w2[j], 128)
    a2p = jnp.zeros((128, 512), F32).at[64:].set(rw_a2[j])
    p0 = p0.reshape(batch, seq, -1)
    y_rw = _rwkv(p0, vec, mulo, w2p, a2p, rw_g2[j]).reshape(t, RW_WIDTH)
    y_gla = _gla(p0, _pad_rows(gla_gk_w2[j], 128), _pad_rows(gla_gk_b[j][None], 8),
                 _pad_rows(gla_norm_g[j][None], 8)).reshape(t, 512)

    wr, br = _route_weights(moe_wg[0], moe_bg[0], moe_we[0], moe_be[0])
    h, route, counts = _proj_route(y_rw, y_gla, xt, ab_w_out[j].astype(BF16), ln1[0], wr, br, tm)
    h = _moe(h, route, counts, moe_w1, moe_w3, moe_w2, 0, ln2[0])

    p1 = _matmul(h, cd_w_in[j].astype(BF16), tm)
    lb_sm = jax.nn.softmax(hg_lb.astype(F32), axis=0)
    lower = (jnp.cumsum(lb_sm, axis=0) - lb_sm[0])[1]
    y_hg = _hgrn(p1.reshape(batch, seq, -1), _pad_rows(lower[None], 8),
                 _pad_rows(hg_norm_g[j][None], 8)).reshape(t, 512)

    toep, bmat, emat, pw = _s5_tables(s5_a_re[j], s5_a_im[j], s5_log_dt[j], s5_b_re[j], s5_b_im[j],
                                      s5_c_re[j], s5_c_im[j], rows_per_seq)
    nrows = t // S5_CHUNK
    uc = p1[:, :512].reshape(nrows, S5_CHUNK, S5_GROUPS, S5_GROUP)
    uc = jnp.transpose(uc, (2, 0, 1, 3)).reshape(S5_GROUPS, nrows, S5_CHUNK * S5_GROUP).astype(BF16)
    yc = _s5_scan(uc, toep, bmat, emat, pw, rows_per_seq)
    y_ssm = jnp.transpose(yc.reshape(S5_GROUPS, nrows, S5_CHUNK, S5_GROUP), (1, 2, 0, 3)).reshape(t, 512)
    y_s5 = _s5_post(y_ssm, p1, _pad_rows(jnp.stack([s5_d[j], s5_glu_b[j]]), 8),
                    s5_glu_w[j].astype(BF16), tm)

    wr, br = _route_weights(moe_wg[1], moe_bg[1], moe_we[1], moe_be[1])
    h2, route, counts = _proj_route(y_s5, y_hg, h, cd_w_out[j].astype(BF16), ln1[1], wr, br, tm)
    out = _moe(h2, route, counts, moe_w1, moe_w3, moe_w2, 1, ln2[1])
    return out.reshape(batch, seq, d)
```

```python
import functools
import math

import numpy as np
import jax
import jax.numpy as jnp
from jax import lax
from jax.experimental import pallas as pl
from jax.experimental.pallas import tpu as pltpu

F32 = jnp.float32
BF16 = jnp.bfloat16

D_MODEL = 1024
DEPTH = 2
RW_HEAD = 64
RW_WIDTH = 512
RW_GN_EPS = 64e-5
GLA_HEADS = 4
GLA_DK = 64
GLA_DV = 128
GLA_GATE_TAU = 16.0
S5_GROUP = 16
S5_GROUPS = 32
S5_STATE = 64
HG_HEADS = 4
HG_DK = 128
CHUNK = 64
S5_CHUNK = 16
S5_SCAN_ROWS = 32
NORM_EPS = 1e-5
MOE_GROUPS = 4
MOE_PER_GROUP = 8
N_EXPERTS = 32
EXPERT_HIDDEN = 512
MOE_BLOCK = 128
COMBINE_TILE = 128
EXPERT_CHUNK = 512
ROW_TILE = 512
DN_ALPHA = (2.0 * DEPTH) ** 0.25

LANES = 128
VMEM_LIMIT = 56 * 1024 * 1024


def _cparams(n_axes=1):
    return pltpu.CompilerParams(dimension_semantics=("arbitrary",) * n_axes,
                                vmem_limit_bytes=VMEM_LIMIT)


def _d(a, b):
    return jnp.dot(a, b, preferred_element_type=F32)


def _d_nt(a, b):
    return lax.dot_general(a, b, (((1,), (1,)), ((), ())), preferred_element_type=F32)


def _d_tn(a, b):
    return lax.dot_general(a, b, (((0,), (0,)), ((), ())), preferred_element_type=F32)


def _split(a):
    hi = a.astype(BF16)
    lo = (a - hi.astype(F32)).astype(BF16)
    return hi, lo


def _split3(a):
    hi = a.astype(BF16)
    r1 = a - hi.astype(F32)
    mid = r1.astype(BF16)
    lo = (r1 - mid.astype(F32)).astype(BF16)
    return hi, mid, lo


def _bdot(a, b):
    return _d(a.astype(BF16), b.astype(BF16))


def _bdot_nt(a, b):
    return _d_nt(a.astype(BF16), b.astype(BF16))


def _bdot_tn(a, b):
    return _d_tn(a.astype(BF16), b.astype(BF16))


def _hdot_with(d, a, b):
    ah, al = _split(a)
    bh, bl = _split(b)
    return d(ah, bh) + (d(ah, bl) + d(al, bh))


def _hdot(a, b):
    return _hdot_with(_d, a, b)


def _hdot_nt(a, b):
    return _hdot_with(_d_nt, a, b)


def _hdot_tn(a, b):
    return _hdot_with(_d_tn, a, b)


def _xdot_l(a, e):
    ah, am, al = _split3(a)
    return _d(ah, e) + (_d(am, e) + _d(al, e))


def _xdot_r(e, a):
    ah, am, al = _split3(a)
    return _d(e, ah) + (_d(e, am) + _d(e, al))


def _iota(shape, dim):
    return lax.broadcasted_iota(jnp.int32, shape, dim)


def _softplus(x):
    return jnp.maximum(x, 0.0) + jnp.log1p(jnp.exp(-jnp.abs(x)))


def _sigmoid(x):
    return 1.0 / (1.0 + jnp.exp(-x))


def _silu(x):
    return x * _sigmoid(x)


def _tril_incl(n):
    return jnp.where(_iota((n, n), 0) >= _iota((n, n), 1), 1.0, 0.0).astype(BF16)


def _cumsum_rows(g):
    return _xdot_r(_tril_incl(g.shape[0]), g)


def _shift_mix(x, prev_ref, b, mu):
    c = x.shape[0]
    rolled = pltpu.roll(x, 1, 0)
    prev = jnp.where(_iota(x.shape, 0) == 0, jnp.broadcast_to(prev_ref[b, 0:1, :], x.shape), rolled)
    prev_ref[b, 0:1, :] = x[c - 1:c, :]
    return x + mu * (prev - x)


def _layer_norm(x, g, b):
    mu = jnp.mean(x, axis=-1, keepdims=True)
    xc = x - mu
    var = jnp.mean(xc * xc, axis=-1, keepdims=True)
    return xc * lax.rsqrt(var + NORM_EPS) * g + b


def _mm_body(x_ref, w_ref, o_ref):
    o_ref[...] = _d(x_ref[...].astype(BF16), w_ref[...])


def _matmul(x, w_bf16, tm):
    m, k = x.shape
    n = w_bf16.shape[1]
    return pl.pallas_call(
        _mm_body,
        grid=(m // tm,),
        in_specs=[pl.BlockSpec((tm, k), lambda i: (i, 0)),
                  pl.BlockSpec((k, n), lambda i: (0, 0))],
        out_specs=pl.BlockSpec((tm, n), lambda i: (i, 0)),
        out_shape=jax.ShapeDtypeStruct((m, n), F32),
        compiler_params=_cparams(1),
        name="in_proj",
    )(x, w_bf16)


_RV_MU_R, _RV_MU_K, _RV_MU_V, _RV_W0, _RV_A0, _RV_KK, _RV_KA, _RV_RK, _RV_GNG, _RV_GNB = range(10)


def _rwkv_body(r_ref, k_ref, v_ref, lo_ref, vec_ref, mulo_ref, w2_ref, a2_ref, g2_ref, ones_ref,
               o_ref, pr_ref, pk_ref, pv_ref, plo_ref, st_ref):
    nb, c = r_ref.shape[0], r_ref.shape[1]
    npair = RW_WIDTH // LANES

    @pl.when(pl.program_id(0) == 0)
    def _():
        pr_ref[...] = jnp.zeros_like(pr_ref)
        pk_ref[...] = jnp.zeros_like(pk_ref)
        pv_ref[...] = jnp.zeros_like(pv_ref)
        plo_ref[...] = jnp.zeros_like(plo_ref)
        st_ref[...] = jnp.zeros_like(st_ref)

    def vec(i):
        return vec_ref[i:i + 1, :]

    ones_bd = ones_ref[...]
    lane = _iota((c, LANES), 1)
    m1 = lane < RW_HEAD
    row2 = _iota((2 * c, 4 * c), 0)
    col2 = _iota((2 * c, 4 * c), 1) & (c - 1)
    tri = ((row2 < c) & (row2 > col2)) | ((row2 >= c) & ((row2 - c) >= col2))
    eye2 = jnp.where(_iota((2 * c, 2 * c), 0) == _iota((2 * c, 2 * c), 1), 1.0, 0.0)
    bd_p = (_iota((LANES, LANES), 0) >> 6) == (_iota((LANES, LANES), 1) >> 6)

    def halves(x):
        return jnp.concatenate([jnp.where(m1, x, 0.0), jnp.where(m1, 0.0, x)], axis=0)

    prep = []
    for b in range(nb):
        xr = _shift_mix(r_ref[b], pr_ref, b, vec(_RV_MU_R))
        xk = _shift_mix(k_ref[b], pk_ref, b, vec(_RV_MU_K))
        xv = _shift_mix(v_ref[b], pv_ref, b, vec(_RV_MU_V))
        xlo = _shift_mix(lo_ref[b], plo_ref, b, mulo_ref[0:1, :])
        lo_a = xlo[:, :LANES]
        lo_g = xlo[:, LANES:]
        w = -_softplus(-(vec(_RV_W0) + _bdot(jnp.tanh(lo_a), w2_ref[...]))) - 0.5
        g = -jnp.exp(w)
        a = _sigmoid(vec(_RV_A0) + _bdot(lo_a, a2_ref[...]))
        gate = _bdot(_sigmoid(lo_g), g2_ref[...])
        kk = xk * vec(_RV_KK)
        kk = kk / jnp.maximum(jnp.sqrt(_xdot_l(kk * kk, ones_bd)), 1e-12)
        k2 = xk * (1.0 + (a - 1.0) * vec(_RV_KA))
        gc = _cumsum_rows(g)
        g_last = gc[c - 1:c, :]
        e_neg = jnp.exp(-gc)
        e_end = jnp.exp(g_last - gc)
        prep.append(dict(
            xr=xr, xv=xv, k2=k2, gate=gate,
            at=-kk * jnp.exp(gc - g),
            bt=(kk * a) * e_neg, kt=k2 * e_neg, rt=xr * jnp.exp(gc),
            bh=(kk * a) * e_end, kh=k2 * e_end,
            gam=jnp.exp(g_last)))

    chains = [(b, p) for b in range(nb) for p in range(npair)]

    def part(ch, name):
        b, p = ch
        return prep[b][name][:, p * LANES:(p + 1) * LANES]

    a_ak, a_row, vv, ak, pinv = {}, {}, {}, {}, {}
    for ch in chains:
        lhs = jnp.concatenate([part(ch, 'at'), part(ch, 'rt')], axis=0)
        rhs = jnp.concatenate([halves(part(ch, 'bt')), halves(part(ch, 'kt'))], axis=0)
        aa = jnp.where(tri, _bdot_nt(lhs, rhs), 0.0)
        a_ak[ch] = aa[:c, 2 * c:].astype(BF16)
        a_row[ch] = aa[c:, :].astype(BF16)
        abd = halves(aa[:c, :2 * c])
        pinv[ch] = eye2 + abd
        ak[ch] = abd
        vv[ch] = halves(part(ch, 'xv')).astype(BF16)
    nlev = int(math.log2(c))
    for lev in range(nlev):
        for ch in chains:
            akb = ak[ch].astype(BF16)
            if lev == 0:
                ak[ch] = _d(akb, akb)
            elif lev < nlev - 1:
                out = _d(akb, jnp.concatenate([akb, pinv[ch].astype(BF16)], axis=1))
                ak[ch] = out[:, :2 * c]
                pinv[ch] = pinv[ch] + out[:, 2 * c:]
            else:
                pinv[ch] = pinv[ch] + _d(akb, pinv[ch].astype(BF16))
    x2 = {ch: _d(a_ak[ch], vv[ch]) for ch in chains}

    sts = {ch: st_ref[ch[0] * npair + ch[1]] for ch in chains}
    xs = {ch: _d_nt(jnp.concatenate([part(ch, 'at'), part(ch, 'rt')], axis=0).astype(BF16),
                    sts[ch].astype(BF16)) for ch in chains}
    us = {}
    for ch in chains:
        u2 = _d(pinv[ch].astype(BF16), halves(xs[ch][:c] + x2[ch]).astype(BF16))
        us[ch] = u2[:c] + u2[c:]
    ys = {}
    for ch in chains:
        b, p = ch
        u = us[ch]
        ys[ch] = xs[ch][c:] + _d(a_row[ch], jnp.concatenate([halves(u).astype(BF16), vv[ch]], axis=0))
        upd = _d_tn(jnp.concatenate([u, part(ch, 'xv')], axis=0).astype(BF16),
                    jnp.concatenate([part(ch, 'bh'), part(ch, 'kh')], axis=0).astype(BF16))
        st_ref[b * npair + p] = sts[ch] * part(ch, 'gam') + jnp.where(bd_p, upd, 0.0)

    inv_n = 1.0 / RW_HEAD
    for b in range(nb):
        y = jnp.concatenate([ys[(b, p)] for p in range(npair)], axis=1)
        mean = _xdot_l(y, ones_bd) * inv_n
        yc = y - mean
        var = _xdot_l(yc * yc, ones_bd) * inv_n
        yn = yc * lax.rsqrt(var + RW_GN_EPS) * vec(_RV_GNG) + vec(_RV_GNB)
        bonus = _xdot_l(prep[b]['xr'] * prep[b]['k2'] * vec(_RV_RK), ones_bd) * prep[b]['xv']
        o_ref[b] = (yn + bonus) * prep[b]['gate']


def _rwkv(p0, vec, mulo, w2p, a2p, g2):
    batch, seq, _ = p0.shape
    nc = seq // CHUNK
    c = CHUNK
    ones_bd = jnp.asarray(np.kron(np.eye(RW_WIDTH // RW_HEAD), np.ones((RW_HEAD, RW_HEAD))), BF16)

    def col(j, width):
        return pl.BlockSpec((batch, c, width), lambda i: (0, i, j))

    def full(shape):
        return pl.BlockSpec(shape, lambda i: (0,) * len(shape))

    return pl.pallas_call(
        _rwkv_body,
        grid=(nc,),
        in_specs=[col(0, 512), col(1, 512), col(2, 512), col(10, 256),
                  full((16, 512)), full((8, 256)), full((128, 512)), full((128, 512)),
                  full((128, 512)), full((512, 512))],
        out_specs=pl.BlockSpec((batch, c, 512), lambda i: (0, i, 0)),
        out_shape=jax.ShapeDtypeStruct((batch, seq, RW_WIDTH), F32),
        scratch_shapes=[pltpu.VMEM((batch, 8, 512), F32), pltpu.VMEM((batch, 8, 512), F32),
                        pltpu.VMEM((batch, 8, 512), F32), pltpu.VMEM((batch, 8, 256), F32),
                        pltpu.VMEM((batch * RW_WIDTH // LANES, LANES, LANES), F32)],
        compiler_params=_cparams(1),
        name="rwkv7",
    )(p0, p0, p0, p0, vec, mulo, w2p, a2p, g2, ones_bd)


def _gla_core(qs, ks, vs, gs, st_ref, heads_per_block):
    nb = len(qs)
    c = qs[0].shape[0]
    hpb = heads_per_block
    nblk = qs[0].shape[1] // LANES
    dk_shift = int(math.log2(LANES // hpb))
    row = _iota(qs[0].shape, 0)

    prep = []
    for q, k, g in zip(qs, ks, gs):
        b = _cumsum_rows(g)

        def brow(i, b=b):
            return jnp.broadcast_to(b[i:i + 1, :], b.shape)

        b15, b31, b47, blast = brow(15), brow(31), brow(47), brow(c - 1)
        ref_b = jnp.where(row < 32, b15, b47)
        ref_d = jnp.where(row < 16, 0.0, jnp.where(row < 32, b15, jnp.where(row < 48, b31, b47)))
        prep.append(dict(
            q_a=q * jnp.exp(jnp.minimum(b - b31, 0.0)), k_a=k * jnp.exp(jnp.minimum(b31 - b, 0.0)),
            q_b=q * jnp.exp(jnp.minimum(b - ref_b, 0.0)), k_b=k * jnp.exp(jnp.minimum(ref_b - b, 0.0)),
            q_d=q * jnp.exp(b - ref_d), k_d=k * jnp.exp(ref_d - b),
            q_i=q * jnp.exp(b), k_s=k * jnp.exp(blast - b), gam=jnp.exp(b[c - 1:c, :])))

    ri = _iota((hpb * c, c), 0) & (c - 1)
    ci = _iota((hpb * c, c), 1)
    mask_a = (ri >= 32) & (ci < 32)
    mask_b = ((ri >> 5) == (ci >> 5)) & (((ri >> 4) & 1) == 1) & (((ci >> 4) & 1) == 0)
    mask_d = ((ri >> 4) == (ci >> 4)) & (ri >= ci)
    lane = _iota((c, LANES), 1)
    bd = (_iota((hpb * LANES, LANES), 0) >> 7) == (_iota((hpb * LANES, LANES), 1) >> dk_shift)

    def heads_rows(x):
        if hpb == 1:
            return x
        return jnp.concatenate([jnp.where((lane >> dk_shift) == h, x, 0.0) for h in range(hpb)], axis=0)

    chains = [(bi, blk) for bi in range(nb) for blk in range(nblk)]

    def part(ch, name):
        bi, blk = ch
        return prep[bi][name][:, blk * LANES:(blk + 1) * LANES]

    v_p = {ch: vs[ch[0]][:, ch[1] * hpb * LANES:(ch[1] + 1) * hpb * LANES].astype(BF16) for ch in chains}
    probs = {}
    for ch in chains:
        s_a = _bdot_nt(heads_rows(part(ch, 'q_a')), part(ch, 'k_a'))
        s_b = _bdot_nt(heads_rows(part(ch, 'q_b')), part(ch, 'k_b'))
        s_d = _bdot_nt(heads_rows(part(ch, 'q_d')), part(ch, 'k_d'))
        probs[ch] = (jnp.where(mask_a, s_a, 0.0) + jnp.where(mask_b, s_b, 0.0)
                     + jnp.where(mask_d, s_d, 0.0)).astype(BF16)
    outs = {}
    for ch in chains:
        pv = _d(probs[ch], v_p[ch])
        o = pv[:c]
        for h in range(1, hpb):
            o = jnp.where((_iota(o.shape, 1) >> 7) == h, pv[h * c:(h + 1) * c], o)
        si = ch[0] * nblk + ch[1]
        sp = st_ref[si]
        outs[ch] = o + _bdot_nt(part(ch, 'q_i'), sp)
        upd = _d_tn(v_p[ch], part(ch, 'k_s').astype(BF16))
        st_ref[si] = sp * part(ch, 'gam') + jnp.where(bd, upd, 0.0)
    return [jnp.concatenate([outs[(bi, blk)] for blk in range(nblk)], axis=1) for bi in range(nb)]


def _gated_rmsnorm(o, gate, norm_g):
    nh = o.shape[1] // LANES
    outs = []
    for h in range(nh):
        sl = slice(h * LANES, (h + 1) * LANES)
        oh = o[:, sl]
        ms = jnp.mean(oh * oh, axis=-1, keepdims=True)
        outs.append(oh * lax.rsqrt(ms + NORM_EPS) * norm_g * _silu(gate[:, sl]))
    return jnp.concatenate(outs, axis=1)


def _gla_body(q_ref, k_ref, v_ref, gate_ref, gk_ref, w2_ref, vec_ref, ng_ref, o_ref, st_ref):
    nb = q_ref.shape[0]

    @pl.when(pl.program_id(0) == 0)
    def _():
        st_ref[...] = jnp.zeros_like(st_ref)

    gs = [-_softplus(-(_hdot(gk_ref[b], w2_ref[...]) + vec_ref[0:1, :])) * (1.0 / GLA_GATE_TAU)
          for b in range(nb)]
    qs = [q_ref[b] * (GLA_DK ** -0.5) for b in range(nb)]
    os_ = _gla_core(qs, [k_ref[b] for b in range(nb)], [v_ref[b] for b in range(nb)], gs, st_ref, 2)
    for b in range(nb):
        o_ref[b] = _gated_rmsnorm(os_[b], gate_ref[b], ng_ref[0:1, :])


def _seq_specs(batch, c):
    def col(j, width):
        return pl.BlockSpec((batch, c, width), lambda i: (0, i, j))

    def full(shape):
        return pl.BlockSpec(shape, lambda i: (0,) * len(shape))

    return col, full


def _gla(p0, gk_w2p, gk_b, norm_g):
    batch, seq, _ = p0.shape
    c = CHUNK
    col, full = _seq_specs(batch, c)
    return pl.pallas_call(
        _gla_body,
        grid=(seq // c,),
        in_specs=[col(11, 256), col(12, 256), col(3, 512), col(4, 512), col(26, 128),
                  full((128, 256)), full((8, 256)), full((8, 128))],
        out_specs=pl.BlockSpec((batch, c, 512), lambda i: (0, i, 0)),
        out_shape=jax.ShapeDtypeStruct((batch, seq, 512), F32),
        scratch_shapes=[pltpu.VMEM((batch * 2, 2 * LANES, LANES), F32)],
        compiler_params=_cparams(1),
        name="gla",
    )(p0, p0, p0, p0, p0, gk_w2p, gk_b, norm_g)


def _hgrn_body(q_ref, f_ref, i_ref, gate_ref, lb_ref, ng_ref, o_ref, st_ref):
    nb = q_ref.shape[0]

    @pl.when(pl.program_id(0) == 0)
    def _():
        st_ref[...] = jnp.zeros_like(st_ref)

    lb = lb_ref[0:1, :]
    qs, ks, gs = [], [], []
    for b in range(nb):
        f = f_ref[b]
        gs.append(jnp.log(lb + (1.0 - lb) * _sigmoid(f)))
        ks.append((1.0 - lb) * _sigmoid(-f))
        qs.append(_silu(q_ref[b]))
    os_ = _gla_core(qs, ks, [i_ref[b] for b in range(nb)], gs, st_ref, 1)
    for b in range(nb):
        o_ref[b] = _gated_rmsnorm(os_[b], gate_ref[b], ng_ref[0:1, :])


def _hgrn(p1, lb, norm_g):
    batch, seq, _ = p1.shape
    c = CHUNK
    col, full = _seq_specs(batch, c)
    return pl.pallas_call(
        _hgrn_body,
        grid=(seq // c,),
        in_specs=[col(1, 512), col(2, 512), col(3, 512), col(4, 512), full((8, 512)), full((8, 128))],
        out_specs=pl.BlockSpec((batch, c, 512), lambda i: (0, i, 0)),
        out_shape=jax.ShapeDtypeStruct((batch, seq, 512), F32),
        scratch_shapes=[pltpu.VMEM((batch * HG_HEADS, LANES, LANES), F32)],
        compiler_params=_cparams(1),
        name="hgrn2",
    )(p1, p1, p1, p1, lb, norm_g)


def _s5_body(u_ref, toep_ref, bm_ref, e_ref, pw_ref, o_ref, *, rows_per_seq):
    u = u_ref[0].T.astype(BF16)
    rows = u.shape[0]
    s = _d(u, bm_ref[0])
    rin = _iota((rows, LANES), 0) & (rows_per_seq - 1)
    pw = pw_ref[0]
    h = s
    nlev = int(math.log2(rows_per_seq))
    for lev in range(nlev):
        sh = 1 << lev
        hs = jnp.where(rin >= sh, pltpu.roll(h, sh, 0), 0.0)
        hs_sw = pltpu.roll(hs, S5_STATE, 1)
        h = h + pw[2 * lev:2 * lev + 1, :] * hs + pw[2 * lev + 1:2 * lev + 2, :] * hs_sw
    hprev = jnp.where(rin >= 1, pltpu.roll(h, 1, 0), 0.0)
    o_ref[0] = (_d(u, toep_ref[0]) + _hdot(hprev, e_ref[0])).T


def _s5_scan(uct, toep, bmat, emat, pw, rows_per_seq):
    ng, width, rows = uct.shape
    return pl.pallas_call(
        functools.partial(_s5_body, rows_per_seq=rows_per_seq),
        grid=(ng,),
        in_specs=[pl.BlockSpec((1, width, rows), lambda g: (g, 0, 0)),
                  pl.BlockSpec((1, width, width), lambda g: (g, 0, 0)),
                  pl.BlockSpec((1, width, LANES), lambda g: (g, 0, 0)),
                  pl.BlockSpec((1, LANES, width), lambda g: (g, 0, 0)),
                  pl.BlockSpec((1, S5_SCAN_ROWS, LANES), lambda g: (g, 0, 0))],
        out_specs=pl.BlockSpec((1, width, rows), lambda g: (g, 0, 0)),
        out_shape=jax.ShapeDtypeStruct((ng, width, rows), F32),
        compiler_params=_cparams(1),
        name="s5_scan",
    )(uct, toep, bmat, emat, pw)


def _s5_post_body(y_ref, u_ref, vec_ref, w_ref, o_ref):
    y = y_ref[...] + vec_ref[0:1, :] * u_ref[...]
    y = 0.5 * y * (1.0 + jnp.tanh(math.sqrt(2.0 / math.pi) * (y + 0.044715 * (y * y * y))))
    o_ref[...] = y * _sigmoid(_bdot(y, w_ref[...]) + vec_ref[1:2, :])


def _s5_post(y_ssm, p1, vec, glu_w, tm):
    t = y_ssm.shape[0]
    return pl.pallas_call(
        _s5_post_body,
        grid=(t // tm,),
        in_specs=[pl.BlockSpec((tm, 512), lambda i: (i, 0)),
                  pl.BlockSpec((tm, 512), lambda i: (i, 0)),
                  pl.BlockSpec((8, 512), lambda i: (0, 0)),
                  pl.BlockSpec((512, 512), lambda i: (0, 0))],
        out_specs=pl.BlockSpec((tm, 512), lambda i: (i, 0)),
        out_shape=jax.ShapeDtypeStruct((t, 512), F32),
        compiler_params=_cparams(1),
        name="s5_post",
    )(y_ssm, p1, vec, glu_w)


def _s5_tables(a_re, a_im, log_dt, b_re, b_im, c_re, c_im, rows_per_seq):
    c = S5_CHUNK
    lam_re = jnp.minimum(a_re, -1e-4)
    lam_im = a_im
    dt = jnp.exp(log_dt)[:, None]
    mag = jnp.exp(lam_re * dt)
    abar_re = mag * jnp.cos(lam_im * dt)
    abar_im = mag * jnp.sin(lam_im * dt)
    den = lam_re * lam_re + lam_im * lam_im
    num_re = abar_re - 1.0
    z_re = (num_re * lam_re + abar_im * lam_im) / den
    z_im = (abar_im * lam_re - num_re * lam_im) / den

    def power(n):
        n = jnp.asarray(n, F32)[..., None, None]
        m = jnp.exp(n * (lam_re * dt))
        return m * jnp.cos(n * (lam_im * dt)), m * jnp.sin(n * (lam_im * dt))

    def cmul(ar, ai, br, bi):
        return ar * br - ai * bi, ar * bi + ai * br

    tau = jnp.arange(c)
    p_re, p_im = power(tau)
    zb_re, zb_im = cmul(z_re[..., None], z_im[..., None], b_re, b_im)
    cp_re, cp_im = cmul(c_re[None], c_im[None], p_re[:, :, None, :], p_im[:, :, None, :])
    hi = lax.Precision.HIGHEST
    kern = (jnp.einsum('tgon,gni->gtoi', cp_re, zb_re, precision=hi)
            - jnp.einsum('tgon,gni->gtoi', cp_im, zb_im, precision=hi))
    tt = jnp.arange(c)[None, :] - jnp.arange(c)[:, None]
    kt = jnp.where((tt >= 0)[None, :, :, None, None], kern[:, jnp.clip(tt, 0, c - 1)], 0.0)
    toep = jnp.transpose(kt, (0, 1, 4, 2, 3)).reshape(S5_GROUPS, c * S5_GROUP, c * S5_GROUP)
    q_re, q_im = power(c - 1 - tau)
    bm_re, bm_im = cmul(q_re[..., None], q_im[..., None], zb_re[None], zb_im[None])
    bmat = jnp.concatenate([jnp.transpose(bm_re, (1, 0, 3, 2)), jnp.transpose(bm_im, (1, 0, 3, 2))],
                           axis=-1).reshape(S5_GROUPS, c * S5_GROUP, 2 * S5_STATE)
    r_re, r_im = power(tau + 1)
    e_re, e_im = cmul(c_re[None], c_im[None], r_re[:, :, None, :], r_im[:, :, None, :])
    emat = jnp.concatenate([jnp.transpose(e_re, (1, 3, 0, 2)), -jnp.transpose(e_im, (1, 3, 0, 2))],
                           axis=1).reshape(S5_GROUPS, 2 * S5_STATE, c * S5_GROUP)
    nlev = int(math.log2(rows_per_seq))
    assert 2 * nlev <= S5_SCAN_ROWS
    s_re, s_im = power(c * (2 ** jnp.arange(nlev)))
    pw = jnp.zeros((S5_GROUPS, S5_SCAN_ROWS, 2 * S5_STATE), F32)
    pw = pw.at[:, 0:2 * nlev:2, :].set(jnp.transpose(jnp.concatenate([s_re, s_re], -1), (1, 0, 2)))
    pw = pw.at[:, 1:2 * nlev:2, :].set(jnp.transpose(jnp.concatenate([-s_im, s_im], -1), (1, 0, 2)))
    return toep.astype(BF16), bmat.astype(BF16), emat, pw


def _proj_route_body(ya_ref, yb_ref, x_ref, w_ref, ln_ref, wr_ref, br_ref,
                     h_ref, route_ref, cnt_ref, carry_ref):
    tm = ya_ref.shape[0]
    half = w_ref.shape[0] // 2

    @pl.when(pl.program_id(0) == 0)
    def _():
        carry_ref[...] = jnp.zeros_like(carry_ref)

    mix = _bdot(ya_ref[...], w_ref[:half, :]) + _bdot(yb_ref[...], w_ref[half:, :])
    h = _layer_norm(DN_ALPHA * x_ref[...] + mix, ln_ref[0:1, :], ln_ref[1:2, :])
    h_ref[...] = h

    logits = _hdot(h, wr_ref[...]) + br_ref[0:1, :]
    lane = _iota((tm, LANES), 1).astype(F32)
    neg = -jnp.inf

    def softmax_masked(mask):
        xm = jnp.where(mask, logits, neg)
        m = jnp.max(xm, axis=-1, keepdims=True)
        e = jnp.exp(xm - m)
        return e / jnp.sum(e, axis=-1, keepdims=True)

    def top1(pm):
        m = jnp.max(pm, axis=-1, keepdims=True)
        idx = jnp.min(jnp.where(pm == m, lane, float(LANES)), axis=-1, keepdims=True)
        return m, idx

    coarse = jnp.where(lane < MOE_GROUPS, softmax_masked(lane < MOE_GROUPS), -1.0)
    p_grp, grp = top1(coarse)
    lo = MOE_GROUPS + MOE_PER_GROUP * grp
    fmask = (lane >= lo) & (lane < lo + MOE_PER_GROUP)
    fine = jnp.where(fmask, softmax_masked(fmask), -1.0)
    p1, j1 = top1(fine)
    p2, j2 = top1(jnp.where(lane == j1, -1.0, fine))
    denom = p1 + p2
    g1 = p_grp * (p1 / denom)
    g2 = p_grp * (p2 / denom)
    e1 = j1 - MOE_GROUPS
    e2 = j2 - MOE_GROUPS

    oh1 = jnp.where(lane == e1, 1.0, 0.0)
    oh2 = jnp.where(lane == e2, 1.0, 0.0)
    cnt = oh1 + oh2
    strict = jnp.where(_iota((tm, tm), 0) > _iota((tm, tm), 1), 1.0, 0.0).astype(BF16)
    before = _d(strict, cnt.astype(BF16)) + carry_ref[0:1, :]
    r1 = jnp.sum(oh1 * before, axis=-1, keepdims=True)
    r2 = jnp.sum(oh2 * before, axis=-1, keepdims=True)
    carry_ref[0:1, :] = carry_ref[0:1, :] + jnp.sum(cnt, axis=0, keepdims=True)
    cnt_ref[...] = carry_ref[...]

    out = jnp.where(lane == 0, e1, 0.0)
    out = jnp.where(lane == 1, e2, out)
    out = jnp.where(lane == 2, r1, out)
    out = jnp.where(lane == 3, r2, out)
    out = jnp.where(lane == 4, g1, out)
    out = jnp.where(lane == 5, g2, out)
    route_ref[...] = out[:, :8]


def _proj_route(ya, yb, resid, w_out, ln, wr, br, tm):
    t, d = resid.shape
    return pl.pallas_call(
        _proj_route_body,
        grid=(t // tm,),
        in_specs=[pl.BlockSpec((tm, 512), lambda i: (i, 0)),
                  pl.BlockSpec((tm, 512), lambda i: (i, 0)),
                  pl.BlockSpec((tm, d), lambda i: (i, 0)),
                  pl.BlockSpec((d, d), lambda i: (0, 0)),
                  pl.BlockSpec((8, d), lambda i: (0, 0)),
                  pl.BlockSpec((d, LANES), lambda i: (0, 0)),
                  pl.BlockSpec((8, LANES), lambda i: (0, 0))],
        out_specs=[pl.BlockSpec((tm, d), lambda i: (i, 0)),
                   pl.BlockSpec((tm, 8), lambda i: (i, 0)),
                   pl.BlockSpec((8, LANES), lambda i: (0, 0))],
        out_shape=[jax.ShapeDtypeStruct((t, d), F32),
                   jax.ShapeDtypeStruct((t, 8), F32),
                   jax.ShapeDtypeStruct((8, LANES), F32)],
        scratch_shapes=[pltpu.VMEM((8, LANES), F32)],
        compiler_params=_cparams(1),
        name="proj_ln_route",
    )(ya, yb, resid, w_out, ln, wr, br)


def _row_copy(src_ref, src_row, dst_ref, dst_row, sem):
    return pltpu.make_async_copy(src_ref.at[pl.ds(src_row, 1)], dst_ref.at[pl.ds(dst_row, 1)], sem)


def _start_all(copies):
    for n, cp in enumerate(copies):
        cp.start(priority=n % 2)


def _dispatch_body(pad_start_ref, pad_len_ref, used_rows_ref, dest_ref, h_ref, rows_ref, hbuf_ref, zbuf_ref,
                   lsem, ssem, zsem):
    i = pl.program_id(0)
    last = pl.num_programs(0) - 1
    tm = hbuf_ref.shape[1]
    slot = i % 2

    def load(tile, s):
        return pltpu.make_async_copy(h_ref.at[pl.ds(tile * tm, tm)], hbuf_ref.at[s], lsem.at[s])

    def scatters(s):
        return [_row_copy(hbuf_ref.at[s], r, rows_ref, dest_ref[0, 0, 2 * r + q], ssem.at[s])
                for r in range(tm) for q in range(2)]

    @pl.when(i == 0)
    def _():
        load(0, 0).start()

    @pl.when(i >= 1)
    def _():
        for cp in scatters(1 - slot):
            cp.wait()

    @pl.when(i < last)
    def _():
        load(i + 1, 1 - slot).start()

    load(i, slot).wait()
    _start_all(scatters(slot))

    @pl.when(i == last)
    def _():
        for cp in scatters(slot):
            cp.wait()
        zbuf_ref[...] = jnp.zeros_like(zbuf_ref)
        sub = 8
        for e in range(N_EXPERTS):
            start = pad_start_ref[e]
            npad = pad_len_ref[e]
            end = start + npad
            run = zbuf_ref.shape[0]
            while run >= sub:
                end = end - (npad & run)

                @pl.when((npad & run) != 0)
                def _(end=end, run=run):
                    dst = rows_ref.at[pl.ds(pl.multiple_of(end, run), run)]
                    cp = pltpu.make_async_copy(zbuf_ref.at[pl.ds(0, run)], dst, zsem)
                    cp.start()
                    cp.wait()

                run //= 2
            for k in range(sub - 1):
                @pl.when(k < (npad & (sub - 1)))
                def _(k=k):
                    cp = _row_copy(zbuf_ref, 0, rows_ref, start + k, zsem)
                    cp.start()
                    cp.wait()

        half = zbuf_ref.shape[0]

        def fill(b, carry):
            cp = pltpu.make_async_copy(zbuf_ref, rows_ref.at[pl.ds(pl.multiple_of(b * half, half), half)], zsem)
            cp.start()
            cp.wait()
            return carry

        lax.fori_loop(used_rows_ref[0] // half, rows_ref.shape[0] // half, fill, 0)


def _dispatch(pad_start, pad_len, used_rows, dest3, h, n_rows):
    t, d = h.shape
    nt, _, tm2 = dest3.shape
    tm = tm2 // 2
    grid_spec = pltpu.PrefetchScalarGridSpec(
        num_scalar_prefetch=3,
        grid=(nt,),
        in_specs=[pl.BlockSpec((1, 1, tm2), lambda i, *_: (i, 0, 0), memory_space=pltpu.SMEM),
                  pl.BlockSpec(memory_space=pl.ANY)],
        out_specs=pl.BlockSpec(memory_space=pl.ANY),
        scratch_shapes=[pltpu.VMEM((2, tm, d), F32), pltpu.VMEM((MOE_BLOCK // 2, d), F32),
                        pltpu.SemaphoreType.DMA((2,)), pltpu.SemaphoreType.DMA((2,)),
                        pltpu.SemaphoreType.DMA],
    )
    return pl.pallas_call(
        _dispatch_body,
        grid_spec=grid_spec,
        out_shape=jax.ShapeDtypeStruct((n_rows, d), F32),
        compiler_params=_cparams(1),
        name="moe_dispatch",
    )(pad_start, pad_len, used_rows, dest3, h)


def _expert_body(first_ref, nblk_ref, nused_ref, x_ref, w1_ref, w3_ref, w2_ref, y_ref,
                 xbuf_ref, ybuf_ref, w1b_ref, w3b_ref, w2b_ref, xsem, ysem):
    e = pl.program_id(0)
    row0 = first_ref[e] * MOE_BLOCK
    nblk = nblk_ref[e]
    big = xbuf_ref.shape[1]
    per_big = big // MOE_BLOCK

    def cast_weights():
        w1b_ref[...] = w1_ref[0].astype(BF16)
        w3b_ref[...] = w3_ref[0].astype(BF16)
        w2b_ref[...] = w2_ref[0].astype(BF16)

    def stream(start_row, count, size, casts_first):
        def x_copy(j, s):
            src = x_ref.at[pl.ds(pl.multiple_of(start_row + j * size, MOE_BLOCK), size)]
            return pltpu.make_async_copy(src, xbuf_ref.at[s, pl.ds(0, size)], xsem.at[s])

        def y_copy(j, s):
            dst = y_ref.at[pl.ds(pl.multiple_of(start_row + j * size, MOE_BLOCK), size)]
            return pltpu.make_async_copy(ybuf_ref.at[s, pl.ds(0, size)], dst, ysem.at[s])

        @pl.when(count > 0)
        def _():
            x_copy(0, 0).start()
            pl.when(casts_first)(cast_weights)

            def chunk(j, carry):
                s = j % 2

                @pl.when(j + 1 < count)
                def _():
                    x_copy(j + 1, 1 - s).start()

                x_copy(j, s).wait()
                xb = xbuf_ref[s, pl.ds(0, size), :].astype(BF16)
                hid = _silu(_d(xb, w1b_ref[...])) * _d(xb, w3b_ref[...])

                @pl.when(j >= 2)
                def _():
                    y_copy(j - 2, s).wait()

                ybuf_ref[s, pl.ds(0, size), :] = _d(hid.astype(BF16), w2b_ref[...])
                y_copy(j, s).start()
                return carry

            lax.fori_loop(0, count, chunk, 0)

            @pl.when(count >= 2)
            def _():
                y_copy(count - 2, count % 2).wait()

            y_copy(count - 1, (count - 1) % 2).wait()

    @pl.when(nblk > 0)
    def _():
        nbig = nblk // per_big
        stream(row0, nbig, big, nbig > 0)
        stream(row0 + nbig * big, nblk - nbig * per_big, MOE_BLOCK, nbig == 0)

    @pl.when(e == pl.num_programs(0) - 1)
    def _():
        ybuf_ref[0, pl.ds(0, MOE_BLOCK), :] = jnp.zeros((MOE_BLOCK, ybuf_ref.shape[2]), F32)
        nused = nused_ref[0]
        ntot = y_ref.shape[0] // MOE_BLOCK

        def fill(b, carry):
            cp = pltpu.make_async_copy(ybuf_ref.at[0, pl.ds(0, MOE_BLOCK)],
                                       y_ref.at[pl.ds(b * MOE_BLOCK, MOE_BLOCK)], ysem.at[0])
            cp.start()
            cp.wait()
            return carry

        lax.fori_loop(nused, ntot, fill, 0)


def _experts(first_blk, nblk, nused, x_rows, w1, w3, w2, layer):
    r, d = x_rows.shape
    hid = w1.shape[-1]
    grid_spec = pltpu.PrefetchScalarGridSpec(
        num_scalar_prefetch=3,
        grid=(N_EXPERTS,),
        in_specs=[pl.BlockSpec(memory_space=pl.ANY),
                  pl.BlockSpec((None, 1, d, hid), lambda e, *_: (layer, e, 0, 0)),
                  pl.BlockSpec((None, 1, d, hid), lambda e, *_: (layer, e, 0, 0)),
                  pl.BlockSpec((None, 1, hid, d), lambda e, *_: (layer, e, 0, 0))],
        out_specs=pl.BlockSpec(memory_space=pl.ANY),
        scratch_shapes=[pltpu.VMEM((2, EXPERT_CHUNK, d), F32), pltpu.VMEM((2, EXPERT_CHUNK, d), F32),
                        pltpu.VMEM((d, hid), BF16), pltpu.VMEM((d, hid), BF16), pltpu.VMEM((hid, d), BF16),
                        pltpu.SemaphoreType.DMA((2,)), pltpu.SemaphoreType.DMA((2,))],
    )
    return pl.pallas_call(
        _expert_body,
        grid_spec=grid_spec,
        out_shape=jax.ShapeDtypeStruct((r, d), F32),
        compiler_params=_cparams(1),
        name="moe_experts",
    )(first_blk, nblk, nused, x_rows, w1, w3, w2)


def _combine_body(dest_ref, dest_next_ref, gate_ref, h_ref, ln_ref, rows_ref, o_ref, buf_ref, sem):
    i = pl.program_id(0)
    last = pl.num_programs(0) - 1
    tm = h_ref.shape[0]

    def gather(dref, slot):
        return [_row_copy(rows_ref, dref[0, 0, 2 * r + s], buf_ref.at[slot, s], r, sem.at[slot])
                for r in range(tm) for s in range(2)]

    @pl.when(i == 0)
    def _():
        _start_all(gather(dest_ref, 0))

    _start_all(gather(dest_next_ref, (i + 1) % 2))
    slot = i % 2
    for cp in gather(dest_ref, slot):
        cp.wait()
    gate = gate_ref[...]
    y = gate[:, 4:5] * buf_ref[slot, 0] + gate[:, 5:6] * buf_ref[slot, 1]
    o_ref[...] = _layer_norm(DN_ALPHA * h_ref[...] + y, ln_ref[0:1, :], ln_ref[1:2, :])

    @pl.when(i == last)
    def _():
        for cp in gather(dest_next_ref, (i + 1) % 2):
            cp.wait()


def _combine(dest3, route, h, ln, y_rows, tm):
    t, d = h.shape
    nt = t // tm
    return pl.pallas_call(
        _combine_body,
        grid=(nt,),
        in_specs=[pl.BlockSpec((1, 1, 2 * tm), lambda i: (i, 0, 0), memory_space=pltpu.SMEM),
                  pl.BlockSpec((1, 1, 2 * tm), lambda i: (jnp.minimum(i + 1, nt - 1), 0, 0),
                               memory_space=pltpu.SMEM),
                  pl.BlockSpec((tm, 8), lambda i: (i, 0)),
                  pl.BlockSpec((tm, d), lambda i: (i, 0)),
                  pl.BlockSpec((8, d), lambda i: (0, 0)),
                  pl.BlockSpec(memory_space=pl.ANY)],
        out_specs=pl.BlockSpec((tm, d), lambda i: (i, 0)),
        out_shape=jax.ShapeDtypeStruct((t, d), F32),
        scratch_shapes=[pltpu.VMEM((2, 2, tm, d), F32), pltpu.SemaphoreType.DMA((2,))],
        compiler_params=_cparams(1),
        name="moe_combine_ln",
    )(dest3, dest3, route, h, ln, y_rows)


def _moe(h, route, counts, w1, w3, w2, layer, ln):
    t, d = h.shape
    tm = min(COMBINE_TILE, t)
    a = 2 * t
    expert = route[:, 0:2].astype(jnp.int32)
    rank = route[:, 2:4].astype(jnp.int32)
    cnt = counts[0, :N_EXPERTS].astype(jnp.int32)
    padded = (cnt + MOE_BLOCK - 1) // MOE_BLOCK * MOE_BLOCK
    pend = jnp.cumsum(padded)
    pstart = pend - padded
    dest = pstart[expert] + rank
    n_blocks = -(-a // MOE_BLOCK) + N_EXPERTS
    nused = (pend[-1:] // MOE_BLOCK).astype(jnp.int32)
    dest3 = dest.reshape(t // tm, 1, 2 * tm)
    x_rows = _dispatch((pstart + cnt).astype(jnp.int32), (padded - cnt).astype(jnp.int32),
                       pend[-1:].astype(jnp.int32), dest3, h, n_blocks * MOE_BLOCK)
    y_rows = _experts((pstart // MOE_BLOCK).astype(jnp.int32), (padded // MOE_BLOCK).astype(jnp.int32),
                      nused, x_rows, w1, w3, w2, layer)
    return _combine(dest3, route, h, ln, y_rows, tm)


def _pad_rows(x, rows):
    return jnp.zeros((rows,) + x.shape[1:], x.dtype).at[:x.shape[0]].set(x)


def _route_weights(wg, bg, we, be):
    d = wg.shape[0]
    wr = jnp.zeros((d, LANES), F32)
    wr = wr.at[:, :MOE_GROUPS].set(wg)
    wr = wr.at[:, MOE_GROUPS:MOE_GROUPS + N_EXPERTS].set(jnp.transpose(we, (1, 0, 2)).reshape(d, N_EXPERTS))
    br = jnp.zeros((8, LANES), F32)
    br = br.at[0, :MOE_GROUPS].set(bg)
    br = br.at[0, MOE_GROUPS:MOE_GROUPS + N_EXPERTS].set(be.reshape(N_EXPERTS))
    return wr, br


def kernel(x, ab_w_in, rw_mu, rw_w0, rw_w2, rw_a0, rw_a2, rw_g2, rw_k_k, rw_k_a, rw_r_k, rw_gn_g, rw_gn_b, gla_gk_w2, gla_gk_b, gla_norm_g, ab_w_out, cd_w_in, s5_a_re, s5_a_im, s5_log_dt, s5_b_re, s5_b_im, s5_c_re, s5_c_im, s5_d, s5_glu_w, s5_glu_b, hg_lb, hg_norm_g, cd_w_out, ln1_g, ln1_b, moe_wg, moe_bg, moe_we, moe_be, moe_w1, moe_w3, moe_w2, ln2_g, ln2_b):
    batch, seq, d = x.shape
    t = batch * seq
    assert d == D_MODEL and seq % CHUNK == 0
    rows_per_seq = seq // S5_CHUNK
    assert rows_per_seq & (rows_per_seq - 1) == 0, "S5 chunk scan assumes a power-of-two chunk count"
    tm = min(ROW_TILE, t)
    assert t % tm == 0
    xt = x.reshape(t, d)
    ln1 = [_pad_rows(jnp.stack([ln1_g[l], ln1_b[l]]), 8) for l in range(DEPTH)]
    ln2 = [_pad_rows(jnp.stack([ln2_g[l], ln2_b[l]]), 8) for l in range(DEPTH)]

    j = 0
    w = ab_w_in[j]
    r_, wl_, k_, v_, al_, gl_ = 0, 512, 576, 1088, 1600, 1664
    gq, gk, gv, glow, ggate = 1792, 2048, 2304, 2816, 2832
    w0cols = jnp.concatenate([
        w[:, r_:r_ + 512], w[:, k_:k_ + 512], w[:, v_:v_ + 512], w[:, gv:gv + 512], w[:, ggate:ggate + 512],
        w[:, wl_:wl_ + 64], w[:, al_:al_ + 64], w[:, gl_:gl_ + 128], w[:, gq:gq + 256], w[:, gk:gk + 256],
        w[:, glow:glow + 16], jnp.zeros((d, 112), F32)], axis=1).astype(BF16)
    p0 = _matmul(xt, w0cols, tm)

    mu = rw_mu[j]
    vec = _pad_rows(jnp.stack([mu[r_:r_ + 512], mu[k_:k_ + 512], mu[v_:v_ + 512], rw_w0[j], rw_a0[j],
                               rw_k_k[j], rw_k_a[j], rw_r_k[j].reshape(-1), rw_gn_g[j], rw_gn_b[j]]), 16)
    mulo = _pad_rows(jnp.concatenate([mu[wl_:wl_ + 64], mu[al_:al_ + 64], mu[gl_:gl_ + 128]])[None], 8)
    w2p = _pad_rows(rw_w2[j], 128)
    a2p = jnp.zeros((128, 512), F32).at[64:].set(rw_a2[j])
    p0 = p0.reshape(batch, seq, -1)
    y_rw = _rwkv(p0, vec, mulo, w2p, a2p, rw_g2[j]).reshape(t, RW_WIDTH)
    y_gla = _gla(p0, _pad_rows(gla_gk_w2[j], 128), _pad_rows(gla_gk_b[j][None], 8),
                 _pad_rows(gla_norm_g[j][None], 8)).reshape(t, 512)

    wr, br = _route_weights(moe_wg[0], moe_bg[0], moe_we[0], moe_be[0])
    h, route, counts = _proj_route(y_rw, y_gla, xt, ab_w_out[j].astype(BF16), ln1[0], wr, br, tm)
    h = _moe(h, route, counts, moe_w1, moe_w3, moe_w2, 0, ln2[0])

    p1 = _matmul(h, cd_w_in[j].astype(BF16), tm)
    lb_sm = jax.nn.softmax(hg_lb.astype(F32), axis=0)
    lower = (jnp.cumsum(lb_sm, axis=0) - lb_sm[0])[1]
    y_hg = _hgrn(p1.reshape(batch, seq, -1), _pad_rows(lower[None], 8),
                 _pad_rows(hg_norm_g[j][None], 8)).reshape(t, 512)

    toep, bmat, emat, pw = _s5_tables(s5_a_re[j], s5_a_im[j], s5_log_dt[j], s5_b_re[j], s5_b_im[j],
                                      s5_c_re[j], s5_c_im[j], rows_per_seq)
    nrows = t // S5_CHUNK
    uct = p1[:, :512].reshape(nrows, S5_CHUNK, S5_GROUPS, S5_GROUP)
    uct = jnp.transpose(uct, (2, 1, 3, 0)).reshape(S5_GROUPS, S5_CHUNK * S5_GROUP, nrows)
    yct = _s5_scan(uct, toep, bmat, emat, pw, rows_per_seq)
    y_ssm = jnp.transpose(yct.reshape(S5_GROUPS, S5_CHUNK, S5_GROUP, nrows), (3, 1, 0, 2)).reshape(t, 512)
    y_s5 = _s5_post(y_ssm, p1, _pad_rows(jnp.stack([s5_d[j], s5_glu_b[j]]), 8),
                    s5_glu_w[j].astype(BF16), tm)

    wr, br = _route_weights(moe_wg[1], moe_bg[1], moe_we[1], moe_be[1])
    h2, route, counts = _proj_route(y_s5, y_hg, h, cd_w_out[j].astype(BF16), ln1[1], wr, br, tm)
    out = _moe(h2, route, counts, moe_w1, moe_w3, moe_w2, 1, ln2[1])
    return out.reshape(batch, seq, d)
```

```python
import functools
import math

import numpy as np
import jax
import jax.numpy as jnp
from jax import lax
from jax.experimental import pallas as pl
from jax.experimental.pallas import tpu as pltpu

F32 = jnp.float32
BF16 = jnp.bfloat16

D_MODEL = 1024
DEPTH = 2
RW_HEAD = 64
RW_WIDTH = 512
RW_GN_EPS = 64e-5
GLA_HEADS = 4
GLA_DK = 64
GLA_DV = 128
GLA_GATE_TAU = 16.0
S5_GROUP = 16
S5_GROUPS = 32
S5_STATE = 64
HG_HEADS = 4
HG_DK = 128
CHUNK = 64
S5_CHUNK = 16
S5_SCAN_ROWS = 32
NORM_EPS = 1e-5
MOE_GROUPS = 4
MOE_PER_GROUP = 8
N_EXPERTS = 32
EXPERT_HIDDEN = 512
MOE_BLOCK = 128
COMBINE_TILE = 128
EXPERT_CHUNK = 512
ROW_TILE = 512
DN_ALPHA = (2.0 * DEPTH) ** 0.25

LANES = 128
VMEM_LIMIT = 56 * 1024 * 1024


def _cparams(n_axes=1):
    return pltpu.CompilerParams(dimension_semantics=("arbitrary",) * n_axes,
                                vmem_limit_bytes=VMEM_LIMIT)


def _d(a, b):
    return jnp.dot(a, b, preferred_element_type=F32)


def _d_nt(a, b):
    return lax.dot_general(a, b, (((1,), (1,)), ((), ())), preferred_element_type=F32)


def _d_tn(a, b):
    return lax.dot_general(a, b, (((0,), (0,)), ((), ())), preferred_element_type=F32)


def _split(a):
    hi = a.astype(BF16)
    lo = (a - hi.astype(F32)).astype(BF16)
    return hi, lo


def _split3(a):
    hi = a.astype(BF16)
    r1 = a - hi.astype(F32)
    mid = r1.astype(BF16)
    lo = (r1 - mid.astype(F32)).astype(BF16)
    return hi, mid, lo


def _bdot(a, b):
    return _d(a.astype(BF16), b.astype(BF16))


def _bdot_nt(a, b):
    return _d_nt(a.astype(BF16), b.astype(BF16))


def _bdot_tn(a, b):
    return _d_tn(a.astype(BF16), b.astype(BF16))


def _hdot_with(d, a, b):
    ah, al = _split(a)
    bh, bl = _split(b)
    return d(ah, bh) + (d(ah, bl) + d(al, bh))


def _hdot(a, b):
    return _hdot_with(_d, a, b)


def _hdot_nt(a, b):
    return _hdot_with(_d_nt, a, b)


def _hdot_tn(a, b):
    return _hdot_with(_d_tn, a, b)


def _xdot_l(a, e):
    ah, am, al = _split3(a)
    return _d(ah, e) + (_d(am, e) + _d(al, e))


def _xdot_r(e, a):
    ah, am, al = _split3(a)
    return _d(e, ah) + (_d(e, am) + _d(e, al))


def _iota(shape, dim):
    return lax.broadcasted_iota(jnp.int32, shape, dim)


def _softplus(x):
    return jnp.maximum(x, 0.0) + jnp.log1p(jnp.exp(-jnp.abs(x)))


def _sigmoid(x):
    return 1.0 / (1.0 + jnp.exp(-x))


def _silu(x):
    return x * _sigmoid(x)


def _tril_incl(n):
    return jnp.where(_iota((n, n), 0) >= _iota((n, n), 1), 1.0, 0.0).astype(BF16)


def _cumsum_rows(g):
    return _xdot_r(_tril_incl(g.shape[0]), g)


def _shift_mix(x, prev_ref, b, mu):
    c = x.shape[0]
    rolled = pltpu.roll(x, 1, 0)
    prev = jnp.where(_iota(x.shape, 0) == 0, jnp.broadcast_to(prev_ref[b, 0:1, :], x.shape), rolled)
    prev_ref[b, 0:1, :] = x[c - 1:c, :]
    return x + mu * (prev - x)


def _layer_norm(x, g, b):
    mu = jnp.mean(x, axis=-1, keepdims=True)
    xc = x - mu
    var = jnp.mean(xc * xc, axis=-1, keepdims=True)
    return xc * lax.rsqrt(var + NORM_EPS) * g + b


def _mm_body(x_ref, w_ref, o_ref):
    o_ref[...] = _d(x_ref[...].astype(BF16), w_ref[...])


def _matmul(x, w_bf16, tm):
    m, k = x.shape
    n = w_bf16.shape[1]
    return pl.pallas_call(
        _mm_body,
        grid=(m // tm,),
        in_specs=[pl.BlockSpec((tm, k), lambda i: (i, 0)),
                  pl.BlockSpec((k, n), lambda i: (0, 0))],
        out_specs=pl.BlockSpec((tm, n), lambda i: (i, 0)),
        out_shape=jax.ShapeDtypeStruct((m, n), F32),
        compiler_params=_cparams(1),
        name="in_proj",
    )(x, w_bf16)


_RV_MU_R, _RV_MU_K, _RV_MU_V, _RV_W0, _RV_A0, _RV_KK, _RV_KA, _RV_RK, _RV_GNG, _RV_GNB = range(10)


def _rwkv_body(r_ref, k_ref, v_ref, lo_ref, vec_ref, mulo_ref, w2_ref, a2_ref, g2_ref, ones_ref,
               o_ref, pr_ref, pk_ref, pv_ref, plo_ref, st_ref):
    nb, c = r_ref.shape[0], r_ref.shape[1]
    npair = RW_WIDTH // LANES

    @pl.when(pl.program_id(0) == 0)
    def _():
        pr_ref[...] = jnp.zeros_like(pr_ref)
        pk_ref[...] = jnp.zeros_like(pk_ref)
        pv_ref[...] = jnp.zeros_like(pv_ref)
        plo_ref[...] = jnp.zeros_like(plo_ref)
        st_ref[...] = jnp.zeros_like(st_ref)

    def vec(i):
        return vec_ref[i:i + 1, :]

    ones_bd = ones_ref[...]
    lane = _iota((c, LANES), 1)
    m1 = lane < RW_HEAD
    row2 = _iota((2 * c, 4 * c), 0)
    col2 = _iota((2 * c, 4 * c), 1) & (c - 1)
    tri = ((row2 < c) & (row2 > col2)) | ((row2 >= c) & ((row2 - c) >= col2))
    eye2 = jnp.where(_iota((2 * c, 2 * c), 0) == _iota((2 * c, 2 * c), 1), 1.0, 0.0)
    bd_p = (_iota((LANES, LANES), 0) >> 6) == (_iota((LANES, LANES), 1) >> 6)

    def halves(x):
        return jnp.concatenate([jnp.where(m1, x, 0.0), jnp.where(m1, 0.0, x)], axis=0)

    prep = []
    for b in range(nb):
        xr = _shift_mix(r_ref[b], pr_ref, b, vec(_RV_MU_R))
        xk = _shift_mix(k_ref[b], pk_ref, b, vec(_RV_MU_K))
        xv = _shift_mix(v_ref[b], pv_ref, b, vec(_RV_MU_V))
        xlo = _shift_mix(lo_ref[b], plo_ref, b, mulo_ref[0:1, :])
        lo_a = xlo[:, :LANES]
        lo_g = xlo[:, LANES:]
        w = -_softplus(-(vec(_RV_W0) + _bdot(jnp.tanh(lo_a), w2_ref[...]))) - 0.5
        g = -jnp.exp(w)
        a = _sigmoid(vec(_RV_A0) + _bdot(lo_a, a2_ref[...]))
        gate = _bdot(_sigmoid(lo_g), g2_ref[...])
        kk = xk * vec(_RV_KK)
        kk = kk / jnp.maximum(jnp.sqrt(_xdot_l(kk * kk, ones_bd)), 1e-12)
        k2 = xk * (1.0 + (a - 1.0) * vec(_RV_KA))
        gc = _cumsum_rows(g)
        g_last = gc[c - 1:c, :]
        e_neg = jnp.exp(-gc)
        e_end = jnp.exp(g_last - gc)
        prep.append(dict(
            xr=xr, xv=xv, k2=k2, gate=gate,
            at=-kk * jnp.exp(gc - g),
            bt=(kk * a) * e_neg, kt=k2 * e_neg, rt=xr * jnp.exp(gc),
            bh=(kk * a) * e_end, kh=k2 * e_end,
            gam=jnp.exp(g_last)))

    chains = [(b, p) for b in range(nb) for p in range(npair)]

    def part(ch, name):
        b, p = ch
        return prep[b][name][:, p * LANES:(p + 1) * LANES]

    a_ak, a_row, vv, ak, pinv = {}, {}, {}, {}, {}
    for ch in chains:
        lhs = jnp.concatenate([part(ch, 'at'), part(ch, 'rt')], axis=0)
        rhs = jnp.concatenate([halves(part(ch, 'bt')), halves(part(ch, 'kt'))], axis=0)
        aa = jnp.where(tri, _bdot_nt(lhs, rhs), 0.0)
        a_ak[ch] = aa[:c, 2 * c:].astype(BF16)
        a_row[ch] = aa[c:, :].astype(BF16)
        abd = halves(aa[:c, :2 * c])
        pinv[ch] = eye2 + abd
        ak[ch] = abd
        vv[ch] = halves(part(ch, 'xv')).astype(BF16)
    nlev = int(math.log2(c))
    for lev in range(nlev):
        for ch in chains:
            akb = ak[ch].astype(BF16)
            if lev == 0:
                ak[ch] = _d(akb, akb)
            elif lev < nlev - 1:
                out = _d(akb, jnp.concatenate([akb, pinv[ch].astype(BF16)], axis=1))
                ak[ch] = out[:, :2 * c]
                pinv[ch] = pinv[ch] + out[:, 2 * c:]
            else:
                pinv[ch] = pinv[ch] + _d(akb, pinv[ch].astype(BF16))
    x2 = {ch: _d(a_ak[ch], vv[ch]) for ch in chains}

    sts = {ch: st_ref[ch[0] * npair + ch[1]] for ch in chains}
    xs = {ch: _d_nt(jnp.concatenate([part(ch, 'at'), part(ch, 'rt')], axis=0).astype(BF16),
                    sts[ch].astype(BF16)) for ch in chains}
    us = {}
    for ch in chains:
        u2 = _d(pinv[ch].astype(BF16), halves(xs[ch][:c] + x2[ch]).astype(BF16))
        us[ch] = u2[:c] + u2[c:]
    ys = {}
    for ch in chains:
        b, p = ch
        u = us[ch]
        ys[ch] = xs[ch][c:] + _d(a_row[ch], jnp.concatenate([halves(u).astype(BF16), vv[ch]], axis=0))
        upd = _d_tn(jnp.concatenate([u, part(ch, 'xv')], axis=0).astype(BF16),
                    jnp.concatenate([part(ch, 'bh'), part(ch, 'kh')], axis=0).astype(BF16))
        st_ref[b * npair + p] = sts[ch] * part(ch, 'gam') + jnp.where(bd_p, upd, 0.0)

    inv_n = 1.0 / RW_HEAD
    for b in range(nb):
        y = jnp.concatenate([ys[(b, p)] for p in range(npair)], axis=1)
        mean = _xdot_l(y, ones_bd) * inv_n
        yc = y - mean
        var = _xdot_l(yc * yc, ones_bd) * inv_n
        yn = yc * lax.rsqrt(var + RW_GN_EPS) * vec(_RV_GNG) + vec(_RV_GNB)
        bonus = _xdot_l(prep[b]['xr'] * prep[b]['k2'] * vec(_RV_RK), ones_bd) * prep[b]['xv']
        o_ref[b] = (yn + bonus) * prep[b]['gate']


def _rwkv(p0, vec, mulo, w2p, a2p, g2):
    batch, seq, _ = p0.shape
    nc = seq // CHUNK
    c = CHUNK
    ones_bd = jnp.asarray(np.kron(np.eye(RW_WIDTH // RW_HEAD), np.ones((RW_HEAD, RW_HEAD))), BF16)

    def col(j, width):
        return pl.BlockSpec((batch, c, width), lambda i: (0, i, j))

    def full(shape):
        return pl.BlockSpec(shape, lambda i: (0,) * len(shape))

    return pl.pallas_call(
        _rwkv_body,
        grid=(nc,),
        in_specs=[col(0, 512), col(1, 512), col(2, 512), col(10, 256),
                  full((16, 512)), full((8, 256)), full((128, 512)), full((128, 512)),
                  full((128, 512)), full((512, 512))],
        out_specs=pl.BlockSpec((batch, c, 512), lambda i: (0, i, 0)),
        out_shape=jax.ShapeDtypeStruct((batch, seq, RW_WIDTH), F32),
        scratch_shapes=[pltpu.VMEM((batch, 8, 512), F32), pltpu.VMEM((batch, 8, 512), F32),
                        pltpu.VMEM((batch, 8, 512), F32), pltpu.VMEM((batch, 8, 256), F32),
                        pltpu.VMEM((batch * RW_WIDTH // LANES, LANES, LANES), F32)],
        compiler_params=_cparams(1),
        name="rwkv7",
    )(p0, p0, p0, p0, vec, mulo, w2p, a2p, g2, ones_bd)


def _gla_core(qs, ks, vs, gs, st_ref, heads_per_block):
    nb = len(qs)
    c = qs[0].shape[0]
    hpb = heads_per_block
    nblk = qs[0].shape[1] // LANES
    dk_shift = int(math.log2(LANES // hpb))
    row = _iota(qs[0].shape, 0)

    prep = []
    for q, k, g in zip(qs, ks, gs):
        b = _cumsum_rows(g)

        def brow(i, b=b):
            return jnp.broadcast_to(b[i:i + 1, :], b.shape)

        b15, b31, b47, blast = brow(15), brow(31), brow(47), brow(c - 1)
        ref_b = jnp.where(row < 32, b15, b47)
        ref_d = jnp.where(row < 16, 0.0, jnp.where(row < 32, b15, jnp.where(row < 48, b31, b47)))
        prep.append(dict(
            q_a=q * jnp.exp(jnp.minimum(b - b31, 0.0)), k_a=k * jnp.exp(jnp.minimum(b31 - b, 0.0)),
            q_b=q * jnp.exp(jnp.minimum(b - ref_b, 0.0)), k_b=k * jnp.exp(jnp.minimum(ref_b - b, 0.0)),
            q_d=q * jnp.exp(b - ref_d), k_d=k * jnp.exp(ref_d - b),
            q_i=q * jnp.exp(b), k_s=k * jnp.exp(blast - b), gam=jnp.exp(b[c - 1:c, :])))

    ri = _iota((hpb * c, c), 0) & (c - 1)
    ci = _iota((hpb * c, c), 1)
    mask_a = (ri >= 32) & (ci < 32)
    mask_b = ((ri >> 5) == (ci >> 5)) & (((ri >> 4) & 1) == 1) & (((ci >> 4) & 1) == 0)
    mask_d = ((ri >> 4) == (ci >> 4)) & (ri >= ci)
    lane = _iota((c, LANES), 1)
    bd = (_iota((hpb * LANES, LANES), 0) >> 7) == (_iota((hpb * LANES, LANES), 1) >> dk_shift)

    def heads_rows(x):
        if hpb == 1:
            return x
        return jnp.concatenate([jnp.where((lane >> dk_shift) == h, x, 0.0) for h in range(hpb)], axis=0)

    chains = [(bi, blk) for bi in range(nb) for blk in range(nblk)]

    def part(ch, name):
        bi, blk = ch
        return prep[bi][name][:, blk * LANES:(blk + 1) * LANES]

    v_p = {ch: vs[ch[0]][:, ch[1] * hpb * LANES:(ch[1] + 1) * hpb * LANES].astype(BF16) for ch in chains}
    probs = {}
    for ch in chains:
        s_a = _bdot_nt(heads_rows(part(ch, 'q_a')), part(ch, 'k_a'))
        s_b = _bdot_nt(heads_rows(part(ch, 'q_b')), part(ch, 'k_b'))
        s_d = _bdot_nt(heads_rows(part(ch, 'q_d')), part(ch, 'k_d'))
        probs[ch] = (jnp.where(mask_a, s_a, 0.0) + jnp.where(mask_b, s_b, 0.0)
                     + jnp.where(mask_d, s_d, 0.0)).astype(BF16)
    outs = {}
    for ch in chains:
        pv = _d(probs[ch], v_p[ch])
        o = pv[:c]
        for h in range(1, hpb):
            o = jnp.where((_iota(o.shape, 1) >> 7) == h, pv[h * c:(h + 1) * c], o)
        si = ch[0] * nblk + ch[1]
        sp = st_ref[si]
        outs[ch] = o + _bdot_nt(part(ch, 'q_i'), sp)
        upd = _d_tn(v_p[ch], part(ch, 'k_s').astype(BF16))
        st_ref[si] = sp * part(ch, 'gam') + jnp.where(bd, upd, 0.0)
    return [jnp.concatenate([outs[(bi, blk)] for blk in range(nblk)], axis=1) for bi in range(nb)]


def _gated_rmsnorm(o, gate, norm_g):
    nh = o.shape[1] // LANES
    outs = []
    for h in range(nh):
        sl = slice(h * LANES, (h + 1) * LANES)
        oh = o[:, sl]
        ms = jnp.mean(oh * oh, axis=-1, keepdims=True)
        outs.append(oh * lax.rsqrt(ms + NORM_EPS) * norm_g * _silu(gate[:, sl]))
    return jnp.concatenate(outs, axis=1)


def _gla_body(q_ref, k_ref, v_ref, gate_ref, gk_ref, w2_ref, vec_ref, ng_ref, o_ref, st_ref):
    nb = q_ref.shape[0]

    @pl.when(pl.program_id(0) == 0)
    def _():
        st_ref[...] = jnp.zeros_like(st_ref)

    gs = [-_softplus(-(_hdot(gk_ref[b], w2_ref[...]) + vec_ref[0:1, :])) * (1.0 / GLA_GATE_TAU)
          for b in range(nb)]
    qs = [q_ref[b] * (GLA_DK ** -0.5) for b in range(nb)]
    os_ = _gla_core(qs, [k_ref[b] for b in range(nb)], [v_ref[b] for b in range(nb)], gs, st_ref, 2)
    for b in range(nb):
        o_ref[b] = _gated_rmsnorm(os_[b], gate_ref[b], ng_ref[0:1, :])


def _seq_specs(batch, c):
    def col(j, width):
        return pl.BlockSpec((batch, c, width), lambda i: (0, i, j))

    def full(shape):
        return pl.BlockSpec(shape, lambda i: (0,) * len(shape))

    return col, full


def _gla(p0, gk_w2p, gk_b, norm_g):
    batch, seq, _ = p0.shape
    c = CHUNK
    col, full = _seq_specs(batch, c)
    return pl.pallas_call(
        _gla_body,
        grid=(seq // c,),
        in_specs=[col(11, 256), col(12, 256), col(3, 512), col(4, 512), col(26, 128),
                  full((128, 256)), full((8, 256)), full((8, 128))],
        out_specs=pl.BlockSpec((batch, c, 512), lambda i: (0, i, 0)),
        out_shape=jax.ShapeDtypeStruct((batch, seq, 512), F32),
        scratch_shapes=[pltpu.VMEM((batch * 2, 2 * LANES, LANES), F32)],
        compiler_params=_cparams(1),
        name="gla",
    )(p0, p0, p0, p0, p0, gk_w2p, gk_b, norm_g)


def _hgrn_body(q_ref, f_ref, i_ref, gate_ref, lb_ref, ng_ref, o_ref, st_ref):
    nb = q_ref.shape[0]

    @pl.when(pl.program_id(0) == 0)
    def _():
        st_ref[...] = jnp.zeros_like(st_ref)

    lb = lb_ref[0:1, :]
    qs, ks, gs = [], [], []
    for b in range(nb):
        f = f_ref[b]
        gs.append(jnp.log(lb + (1.0 - lb) * _sigmoid(f)))
        ks.append((1.0 - lb) * _sigmoid(-f))
        qs.append(_silu(q_ref[b]))
    os_ = _gla_core(qs, ks, [i_ref[b] for b in range(nb)], gs, st_ref, 1)
    for b in range(nb):
        o_ref[b] = _gated_rmsnorm(os_[b], gate_ref[b], ng_ref[0:1, :])


def _hgrn(p1, lb, norm_g):
    batch, seq, _ = p1.shape
    c = CHUNK
    col, full = _seq_specs(batch, c)
    return pl.pallas_call(
        _hgrn_body,
        grid=(seq // c,),
        in_specs=[col(1, 512), col(2, 512), col(3, 512), col(4, 512), full((8, 512)), full((8, 128))],
        out_specs=pl.BlockSpec((batch, c, 512), lambda i: (0, i, 0)),
        out_shape=jax.ShapeDtypeStruct((batch, seq, 512), F32),
        scratch_shapes=[pltpu.VMEM((batch * HG_HEADS, LANES, LANES), F32)],
        compiler_params=_cparams(1),
        name="hgrn2",
    )(p1, p1, p1, p1, lb, norm_g)


def _s5_body(u_ref, toep_ref, bm_ref, e_ref, pw_ref, o_ref, *, rows_per_seq):
    u = u_ref[0]
    rows = u.shape[0]
    s = _d(u, bm_ref[0])
    rin = _iota((rows, LANES), 0) & (rows_per_seq - 1)
    pw = pw_ref[0]
    h = s
    nlev = int(math.log2(rows_per_seq))
    for lev in range(nlev):
        sh = 1 << lev
        hs = jnp.where(rin >= sh, pltpu.roll(h, sh, 0), 0.0)
        hs_sw = pltpu.roll(hs, S5_STATE, 1)
        h = h + pw[2 * lev:2 * lev + 1, :] * hs + pw[2 * lev + 1:2 * lev + 2, :] * hs_sw
    hprev = jnp.where(rin >= 1, pltpu.roll(h, 1, 0), 0.0)
    o_ref[0] = _d(u, toep_ref[0]) + _hdot(hprev, e_ref[0])


def _s5_scan(uc, toep, bmat, emat, pw, rows_per_seq):
    ng, rows, width = uc.shape
    return pl.pallas_call(
        functools.partial(_s5_body, rows_per_seq=rows_per_seq),
        grid=(ng,),
        in_specs=[pl.BlockSpec((1, rows, width), lambda g: (g, 0, 0)),
                  pl.BlockSpec((1, width, width), lambda g: (g, 0, 0)),
                  pl.BlockSpec((1, width, LANES), lambda g: (g, 0, 0)),
                  pl.BlockSpec((1, LANES, width), lambda g: (g, 0, 0)),
                  pl.BlockSpec((1, S5_SCAN_ROWS, LANES), lambda g: (g, 0, 0))],
        out_specs=pl.BlockSpec((1, rows, width), lambda g: (g, 0, 0)),
        out_shape=jax.ShapeDtypeStruct((ng, rows, width), F32),
        compiler_params=_cparams(1),
        name="s5_scan",
    )(uc, toep, bmat, emat, pw)


def _s5_post_body(y_ref, u_ref, vec_ref, w_ref, o_ref):
    y = y_ref[...] + vec_ref[0:1, :] * u_ref[...]
    y = 0.5 * y * (1.0 + jnp.tanh(math.sqrt(2.0 / math.pi) * (y + 0.044715 * (y * y * y))))
    o_ref[...] = y * _sigmoid(_bdot(y, w_ref[...]) + vec_ref[1:2, :])


def _s5_post(y_ssm, p1, vec, glu_w, tm):
    t = y_ssm.shape[0]
    return pl.pallas_call(
        _s5_post_body,
        grid=(t // tm,),
        in_specs=[pl.BlockSpec((tm, 512), lambda i: (i, 0)),
                  pl.BlockSpec((tm, 512), lambda i: (i, 0)),
                  pl.BlockSpec((8, 512), lambda i: (0, 0)),
                  pl.BlockSpec((512, 512), lambda i: (0, 0))],
        out_specs=pl.BlockSpec((tm, 512), lambda i: (i, 0)),
        out_shape=jax.ShapeDtypeStruct((t, 512), F32),
        compiler_params=_cparams(1),
        name="s5_post",
    )(y_ssm, p1, vec, glu_w)


def _s5_tables(a_re, a_im, log_dt, b_re, b_im, c_re, c_im, rows_per_seq):
    c = S5_CHUNK
    lam_re = jnp.minimum(a_re, -1e-4)
    lam_im = a_im
    dt = jnp.exp(log_dt)[:, None]
    mag = jnp.exp(lam_re * dt)
    abar_re = mag * jnp.cos(lam_im * dt)
    abar_im = mag * jnp.sin(lam_im * dt)
    den = lam_re * lam_re + lam_im * lam_im
    num_re = abar_re - 1.0
    z_re = (num_re * lam_re + abar_im * lam_im) / den
    z_im = (abar_im * lam_re - num_re * lam_im) / den

    def power(n):
        n = jnp.asarray(n, F32)[..., None, None]
        m = jnp.exp(n * (lam_re * dt))
        return m * jnp.cos(n * (lam_im * dt)), m * jnp.sin(n * (lam_im * dt))

    def cmul(ar, ai, br, bi):
        return ar * br - ai * bi, ar * bi + ai * br

    tau = jnp.arange(c)
    p_re, p_im = power(tau)
    zb_re, zb_im = cmul(z_re[..., None], z_im[..., None], b_re, b_im)
    cp_re, cp_im = cmul(c_re[None], c_im[None], p_re[:, :, None, :], p_im[:, :, None, :])
    hi = lax.Precision.HIGHEST
    kern = (jnp.einsum('tgon,gni->gtoi', cp_re, zb_re, precision=hi)
            - jnp.einsum('tgon,gni->gtoi', cp_im, zb_im, precision=hi))
    tt = jnp.arange(c)[None, :] - jnp.arange(c)[:, None]
    kt = jnp.where((tt >= 0)[None, :, :, None, None], kern[:, jnp.clip(tt, 0, c - 1)], 0.0)
    toep = jnp.transpose(kt, (0, 1, 4, 2, 3)).reshape(S5_GROUPS, c * S5_GROUP, c * S5_GROUP)
    q_re, q_im = power(c - 1 - tau)
    bm_re, bm_im = cmul(q_re[..., None], q_im[..., None], zb_re[None], zb_im[None])
    bmat = jnp.concatenate([jnp.transpose(bm_re, (1, 0, 3, 2)), jnp.transpose(bm_im, (1, 0, 3, 2))],
                           axis=-1).reshape(S5_GROUPS, c * S5_GROUP, 2 * S5_STATE)
    r_re, r_im = power(tau + 1)
    e_re, e_im = cmul(c_re[None], c_im[None], r_re[:, :, None, :], r_im[:, :, None, :])
    emat = jnp.concatenate([jnp.transpose(e_re, (1, 3, 0, 2)), -jnp.transpose(e_im, (1, 3, 0, 2))],
                           axis=1).reshape(S5_GROUPS, 2 * S5_STATE, c * S5_GROUP)
    nlev = int(math.log2(rows_per_seq))
    assert 2 * nlev <= S5_SCAN_ROWS
    s_re, s_im = power(c * (2 ** jnp.arange(nlev)))
    pw = jnp.zeros((S5_GROUPS, S5_SCAN_ROWS, 2 * S5_STATE), F32)
    pw = pw.at[:, 0:2 * nlev:2, :].set(jnp.transpose(jnp.concatenate([s_re, s_re], -1), (1, 0, 2)))
    pw = pw.at[:, 1:2 * nlev:2, :].set(jnp.transpose(jnp.concatenate([-s_im, s_im], -1), (1, 0, 2)))
    return toep.astype(BF16), bmat.astype(BF16), emat, pw


def _proj_route_body(ya_ref, yb_ref, x_ref, w_ref, ln_ref, wr_ref, br_ref,
                     h_ref, route_ref, cnt_ref, carry_ref):
    tm = ya_ref.shape[0]
    half = w_ref.shape[0] // 2

    @pl.when(pl.program_id(0) == 0)
    def _():
        carry_ref[...] = jnp.zeros_like(carry_ref)

    mix = _bdot(ya_ref[...], w_ref[:half, :]) + _bdot(yb_ref[...], w_ref[half:, :])
    h = _layer_norm(DN_ALPHA * x_ref[...] + mix, ln_ref[0:1, :], ln_ref[1:2, :])
    h_ref[...] = h

    logits = _hdot(h, wr_ref[...]) + br_ref[0:1, :]
    lane = _iota((tm, LANES), 1).astype(F32)
    neg = -jnp.inf

    def softmax_masked(mask):
        xm = jnp.where(mask, logits, neg)
        m = jnp.max(xm, axis=-1, keepdims=True)
        e = jnp.exp(xm - m)
        return e / jnp.sum(e, axis=-1, keepdims=True)

    def top1(pm):
        m = jnp.max(pm, axis=-1, keepdims=True)
        idx = jnp.min(jnp.where(pm == m, lane, float(LANES)), axis=-1, keepdims=True)
        return m, idx

    coarse = jnp.where(lane < MOE_GROUPS, softmax_masked(lane < MOE_GROUPS), -1.0)
    p_grp, grp = top1(coarse)
    lo = MOE_GROUPS + MOE_PER_GROUP * grp
    fmask = (lane >= lo) & (lane < lo + MOE_PER_GROUP)
    fine = jnp.where(fmask, softmax_masked(fmask), -1.0)
    p1, j1 = top1(fine)
    p2, j2 = top1(jnp.where(lane == j1, -1.0, fine))
    denom = p1 + p2
    g1 = p_grp * (p1 / denom)
    g2 = p_grp * (p2 / denom)
    e1 = j1 - MOE_GROUPS
    e2 = j2 - MOE_GROUPS

    oh1 = jnp.where(lane == e1, 1.0, 0.0)
    oh2 = jnp.where(lane == e2, 1.0, 0.0)
    cnt = oh1 + oh2
    strict = jnp.where(_iota((tm, tm), 0) > _iota((tm, tm), 1), 1.0, 0.0).astype(BF16)
    before = _d(strict, cnt.astype(BF16)) + carry_ref[0:1, :]
    r1 = jnp.sum(oh1 * before, axis=-1, keepdims=True)
    r2 = jnp.sum(oh2 * before, axis=-1, keepdims=True)
    carry_ref[0:1, :] = carry_ref[0:1, :] + jnp.sum(cnt, axis=0, keepdims=True)
    cnt_ref[...] = carry_ref[...]

    out = jnp.where(lane == 0, e1, 0.0)
    out = jnp.where(lane == 1, e2, out)
    out = jnp.where(lane == 2, r1, out)
    out = jnp.where(lane == 3, r2, out)
    out = jnp.where(lane == 4, g1, out)
    out = jnp.where(lane == 5, g2, out)
    route_ref[...] = out[:, :8]


def _proj_route(ya, yb, resid, w_out, ln, wr, br, tm):
    t, d = resid.shape
    return pl.pallas_call(
        _proj_route_body,
        grid=(t // tm,),
        in_specs=[pl.BlockSpec((tm, 512), lambda i: (i, 0)),
                  pl.BlockSpec((tm, 512), lambda i: (i, 0)),
                  pl.BlockSpec((tm, d), lambda i: (i, 0)),
                  pl.BlockSpec((d, d), lambda i: (0, 0)),
                  pl.BlockSpec((8, d), lambda i: (0, 0)),
                  pl.BlockSpec((d, LANES), lambda i: (0, 0)),
                  pl.BlockSpec((8, LANES), lambda i: (0, 0))],
        out_specs=[pl.BlockSpec((tm, d), lambda i: (i, 0)),
                   pl.BlockSpec((tm, 8), lambda i: (i, 0)),
                   pl.BlockSpec((8, LANES), lambda i: (0, 0))],
        out_shape=[jax.ShapeDtypeStruct((t, d), F32),
                   jax.ShapeDtypeStruct((t, 8), F32),
                   jax.ShapeDtypeStruct((8, LANES), F32)],
        scratch_shapes=[pltpu.VMEM((8, LANES), F32)],
        compiler_params=_cparams(1),
        name="proj_ln_route",
    )(ya, yb, resid, w_out, ln, wr, br)


def _row_copy(src_ref, src_row, dst_ref, dst_row, sem):
    return pltpu.make_async_copy(src_ref.at[pl.ds(src_row, 1)], dst_ref.at[pl.ds(dst_row, 1)], sem)


def _start_all(copies):
    for n, cp in enumerate(copies):
        cp.start(priority=n % 2)


def _dispatch_body(pad_start_ref, pad_len_ref, used_rows_ref, dest_ref, h_ref, rows_ref, hbuf_ref, zbuf_ref,
                   lsem, ssem, zsem):
    i = pl.program_id(0)
    last = pl.num_programs(0) - 1
    tm = hbuf_ref.shape[1]
    slot = i % 2

    def load(tile, s):
        return pltpu.make_async_copy(h_ref.at[pl.ds(tile * tm, tm)], hbuf_ref.at[s], lsem.at[s])

    def scatters(s):
        return [_row_copy(hbuf_ref.at[s], r, rows_ref, dest_ref[0, 0, 2 * r + q], ssem.at[s])
                for r in range(tm) for q in range(2)]

    @pl.when(i == 0)
    def _():
        load(0, 0).start()

    @pl.when(i >= 1)
    def _():
        for cp in scatters(1 - slot):
            cp.wait()

    @pl.when(i < last)
    def _():
        load(i + 1, 1 - slot).start()

    load(i, slot).wait()
    _start_all(scatters(slot))

    @pl.when(i == last)
    def _():
        for cp in scatters(slot):
            cp.wait()
        zbuf_ref[...] = jnp.zeros_like(zbuf_ref)
        sub = 8
        half = zbuf_ref.shape[0]

        def pad_fills():
            out = []
            for e in range(N_EXPERTS):
                start = pad_start_ref[e]
                npad = pad_len_ref[e]
                end = start + npad
                run = half
                while run >= sub:
                    end = end - (npad & run)
                    dst = rows_ref.at[pl.ds(pl.multiple_of(end, run), run)]
                    out.append(((npad & run) != 0,
                                pltpu.make_async_copy(zbuf_ref.at[pl.ds(0, run)], dst, zsem)))
                    run //= 2
                for k in range(sub - 1):
                    out.append((k < (npad & (sub - 1)), _row_copy(zbuf_ref, 0, rows_ref, start + k, zsem)))
            return out

        for pred, cp in pad_fills():
            pl.when(pred)(cp.start)
        for pred, cp in pad_fills():
            pl.when(pred)(cp.wait)

        def tail(b):
            return pltpu.make_async_copy(
                zbuf_ref, rows_ref.at[pl.ds(pl.multiple_of(b * half, half), half)], zsem)

        first_free = used_rows_ref[0] // half
        n_half = rows_ref.shape[0] // half
        lax.fori_loop(first_free, n_half, lambda b, c: (tail(b).start(), c)[1], 0)
        lax.fori_loop(first_free, n_half, lambda b, c: (tail(b).wait(), c)[1], 0)


def _dispatch(pad_start, pad_len, used_rows, dest3, h, n_rows):
    t, d = h.shape
    nt, _, tm2 = dest3.shape
    tm = tm2 // 2
    grid_spec = pltpu.PrefetchScalarGridSpec(
        num_scalar_prefetch=3,
        grid=(nt,),
        in_specs=[pl.BlockSpec((1, 1, tm2), lambda i, *_: (i, 0, 0), memory_space=pltpu.SMEM),
                  pl.BlockSpec(memory_space=pl.ANY)],
        out_specs=pl.BlockSpec(memory_space=pl.ANY),
        scratch_shapes=[pltpu.VMEM((2, tm, d), F32), pltpu.VMEM((MOE_BLOCK // 2, d), F32),
                        pltpu.SemaphoreType.DMA((2,)), pltpu.SemaphoreType.DMA((2,)),
                        pltpu.SemaphoreType.DMA],
    )
    return pl.pallas_call(
        _dispatch_body,
        grid_spec=grid_spec,
        out_shape=jax.ShapeDtypeStruct((n_rows, d), F32),
        compiler_params=_cparams(1),
        name="moe_dispatch",
    )(pad_start, pad_len, used_rows, dest3, h)


def _expert_body(first_ref, nblk_ref, nused_ref, x_ref, w1_ref, w3_ref, w2_ref, y_ref,
                 xbuf_ref, ybuf_ref, w1b_ref, w3b_ref, w2b_ref, xsem, ysem):
    e = pl.program_id(0)
    row0 = first_ref[e] * MOE_BLOCK
    nblk = nblk_ref[e]
    big = xbuf_ref.shape[1]
    per_big = big // MOE_BLOCK

    def cast_weights():
        w1b_ref[...] = w1_ref[0].astype(BF16)
        w3b_ref[...] = w3_ref[0].astype(BF16)
        w2b_ref[...] = w2_ref[0].astype(BF16)

    def stream(start_row, count, size, casts_first):
        def x_copy(j, s):
            src = x_ref.at[pl.ds(pl.multiple_of(start_row + j * size, MOE_BLOCK), size)]
            return pltpu.make_async_copy(src, xbuf_ref.at[s, pl.ds(0, size)], xsem.at[s])

        def y_copy(j, s):
            dst = y_ref.at[pl.ds(pl.multiple_of(start_row + j * size, MOE_BLOCK), size)]
            return pltpu.make_async_copy(ybuf_ref.at[s, pl.ds(0, size)], dst, ysem.at[s])

        @pl.when(count > 0)
        def _():
            x_copy(0, 0).start(priority=1)
            pl.when(casts_first)(cast_weights)

            def chunk(j, carry):
                s = j % 2

                @pl.when(j + 1 < count)
                def _():
                    x_copy(j + 1, 1 - s).start(priority=1)

                x_copy(j, s).wait()
                xb = xbuf_ref[s, pl.ds(0, size), :].astype(BF16)
                hid = _silu(_d(xb, w1b_ref[...])) * _d(xb, w3b_ref[...])

                @pl.when(j >= 2)
                def _():
                    y_copy(j - 2, s).wait()

                ybuf_ref[s, pl.ds(0, size), :] = _d(hid.astype(BF16), w2b_ref[...])
                y_copy(j, s).start()
                return carry

            lax.fori_loop(0, count, chunk, 0)

            @pl.when(count >= 2)
            def _():
                y_copy(count - 2, count % 2).wait()

            y_copy(count - 1, (count - 1) % 2).wait()

    @pl.when(nblk > 0)
    def _():
        nbig = nblk // per_big
        stream(row0, nbig, big, nbig > 0)
        stream(row0 + nbig * big, nblk - nbig * per_big, MOE_BLOCK, nbig == 0)

    @pl.when(e == pl.num_programs(0) - 1)
    def _():
        ybuf_ref[0, pl.ds(0, MOE_BLOCK), :] = jnp.zeros((MOE_BLOCK, ybuf_ref.shape[2]), F32)
        nused = nused_ref[0]
        ntot = y_ref.shape[0] // MOE_BLOCK

        def fill(b, carry):
            cp = pltpu.make_async_copy(ybuf_ref.at[0, pl.ds(0, MOE_BLOCK)],
                                       y_ref.at[pl.ds(b * MOE_BLOCK, MOE_BLOCK)], ysem.at[0])
            cp.start()
            cp.wait()
            return carry

        lax.fori_loop(nused, ntot, fill, 0)


def _experts(first_blk, nblk, nused, x_rows, w1, w3, w2, layer):
    r, d = x_rows.shape
    hid = w1.shape[-1]
    grid_spec = pltpu.PrefetchScalarGridSpec(
        num_scalar_prefetch=3,
        grid=(N_EXPERTS,),
        in_specs=[pl.BlockSpec(memory_space=pl.ANY),
                  pl.BlockSpec((None, 1, d, hid), lambda e, *_: (layer, e, 0, 0)),
                  pl.BlockSpec((None, 1, d, hid), lambda e, *_: (layer, e, 0, 0)),
                  pl.BlockSpec((None, 1, hid, d), lambda e, *_: (layer, e, 0, 0))],
        out_specs=pl.BlockSpec(memory_space=pl.ANY),
        scratch_shapes=[pltpu.VMEM((2, EXPERT_CHUNK, d), F32), pltpu.VMEM((2, EXPERT_CHUNK, d), F32),
                        pltpu.VMEM((d, hid), BF16), pltpu.VMEM((d, hid), BF16), pltpu.VMEM((hid, d), BF16),
                        pltpu.SemaphoreType.DMA((2,)), pltpu.SemaphoreType.DMA((2,))],
    )
    return pl.pallas_call(
        _expert_body,
        grid_spec=grid_spec,
        out_shape=jax.ShapeDtypeStruct((r, d), F32),
        compiler_params=_cparams(1),
        name="moe_experts",
    )(first_blk, nblk, nused, x_rows, w1, w3, w2)


def _combine_body(dest_ref, dest_next_ref, gate_ref, h_ref, ln_ref, rows_ref, o_ref, buf_ref, sem):
    i = pl.program_id(0)
    last = pl.num_programs(0) - 1
    tm = h_ref.shape[0]

    def gather(dref, slot):
        return [_row_copy(rows_ref, dref[0, 0, 2 * r + s], buf_ref.at[slot, s], r, sem.at[slot])
                for r in range(tm) for s in range(2)]

    @pl.when(i == 0)
    def _():
        _start_all(gather(dest_ref, 0))

    _start_all(gather(dest_next_ref, (i + 1) % 2))
    slot = i % 2
    for cp in gather(dest_ref, slot):
        cp.wait()
    gate = gate_ref[...]
    y = gate[:, 4:5] * buf_ref[slot, 0] + gate[:, 5:6] * buf_ref[slot, 1]
    o_ref[...] = _layer_norm(DN_ALPHA * h_ref[...] + y, ln_ref[0:1, :], ln_ref[1:2, :])

    @pl.when(i == last)
    def _():
        for cp in gather(dest_next_ref, (i + 1) % 2):
            cp.wait()


def _combine(dest3, route, h, ln, y_rows, tm):
    t, d = h.shape
    nt = t // tm
    return pl.pallas_call(
        _combine_body,
        grid=(nt,),
        in_specs=[pl.BlockSpec((1, 1, 2 * tm), lambda i: (i, 0, 0), memory_space=pltpu.SMEM),
                  pl.BlockSpec((1, 1, 2 * tm), lambda i: (jnp.minimum(i + 1, nt - 1), 0, 0),
                               memory_space=pltpu.SMEM),
                  pl.BlockSpec((tm, 8), lambda i: (i, 0)),
                  pl.BlockSpec((tm, d), lambda i: (i, 0)),
                  pl.BlockSpec((8, d), lambda i: (0, 0)),
                  pl.BlockSpec(memory_space=pl.ANY)],
        out_specs=pl.BlockSpec((tm, d), lambda i: (i, 0)),
        out_shape=jax.ShapeDtypeStruct((t, d), F32),
        scratch_shapes=[pltpu.VMEM((2, 2, tm, d), F32), pltpu.SemaphoreType.DMA((2,))],
        compiler_params=_cparams(1),
        name="moe_combine_ln",
    )(dest3, dest3, route, h, ln, y_rows)


def _moe(h, route, counts, w1, w3, w2, layer, ln):
    t, d = h.shape
    tm = min(COMBINE_TILE, t)
    a = 2 * t
    expert = route[:, 0:2].astype(jnp.int32)
    rank = route[:, 2:4].astype(jnp.int32)
    cnt = counts[0, :N_EXPERTS].astype(jnp.int32)
    padded = (cnt + MOE_BLOCK - 1) // MOE_BLOCK * MOE_BLOCK
    pend = jnp.cumsum(padded)
    pstart = pend - padded
    sel = expert[..., None] == jnp.arange(N_EXPERTS, dtype=jnp.int32)
    dest = jnp.sum(jnp.where(sel, pstart, 0), axis=-1) + rank
    n_blocks = -(-a // MOE_BLOCK) + N_EXPERTS
    nused = (pend[-1:] // MOE_BLOCK).astype(jnp.int32)
    dest3 = dest.reshape(t // tm, 1, 2 * tm)
    x_rows = _dispatch((pstart + cnt).astype(jnp.int32), (padded - cnt).astype(jnp.int32),
                       pend[-1:].astype(jnp.int32), dest3, h, n_blocks * MOE_BLOCK)
    y_rows = _experts((pstart // MOE_BLOCK).astype(jnp.int32), (padded // MOE_BLOCK).astype(jnp.int32),
                      nused, x_rows, w1, w3, w2, layer)
    return _combine(dest3, route, h, ln, y_rows, tm)


def _pad_rows(x, rows):
    return jnp.zeros((rows,) + x.shape[1:], x.dtype).at[:x.shape[0]].set(x)


def _route_weights(wg, bg, we, be):
    d = wg.shape[0]
    wr = jnp.zeros((d, LANES), F32)
    wr = wr.at[:, :MOE_GROUPS].set(wg)
    wr = wr.at[:, MOE_GROUPS:MOE_GROUPS + N_EXPERTS].set(jnp.transpose(we, (1, 0, 2)).reshape(d, N_EXPERTS))
    br = jnp.zeros((8, LANES), F32)
    br = br.at[0, :MOE_GROUPS].set(bg)
    br = br.at[0, MOE_GROUPS:MOE_GROUPS + N_EXPERTS].set(be.reshape(N_EXPERTS))
    return wr, br


def kernel(x, ab_w_in, rw_mu, rw_w0, rw_w2, rw_a0, rw_a2, rw_g2, rw_k_k, rw_k_a, rw_r_k, rw_gn_g, rw_gn_b, gla_gk_w2, gla_gk_b, gla_norm_g, ab_w_out, cd_w_in, s5_a_re, s5_a_im, s5_log_dt, s5_b_re, s5_b_im, s5_c_re, s5_c_im, s5_d, s5_glu_w, s5_glu_b, hg_lb, hg_norm_g, cd_w_out, ln1_g, ln1_b, moe_wg, moe_bg, moe_we, moe_be, moe_w1, moe_w3, moe_w2, ln2_g, ln2_b):
    batch, seq, d = x.shape
    t = batch * seq
    assert d == D_MODEL and seq % CHUNK == 0
    rows_per_seq = seq // S5_CHUNK
    assert rows_per_seq & (rows_per_seq - 1) == 0, "S5 chunk scan assumes a power-of-two chunk count"
    tm = min(ROW_TILE, t)
    assert t % tm == 0
    xt = x.reshape(t, d)
    ln1 = [_pad_rows(jnp.stack([ln1_g[l], ln1_b[l]]), 8) for l in range(DEPTH)]
    ln2 = [_pad_rows(jnp.stack([ln2_g[l], ln2_b[l]]), 8) for l in range(DEPTH)]

    j = 0
    w = ab_w_in[j]
    r_, wl_, k_, v_, al_, gl_ = 0, 512, 576, 1088, 1600, 1664
    gq, gk, gv, glow, ggate = 1792, 2048, 2304, 2816, 2832
    w0cols = jnp.concatenate([
        w[:, r_:r_ + 512], w[:, k_:k_ + 512], w[:, v_:v_ + 512], w[:, gv:gv + 512], w[:, ggate:ggate + 512],
        w[:, wl_:wl_ + 64], w[:, al_:al_ + 64], w[:, gl_:gl_ + 128], w[:, gq:gq + 256], w[:, gk:gk + 256],
        w[:, glow:glow + 16], jnp.zeros((d, 112), F32)], axis=1).astype(BF16)
    p0 = _matmul(xt, w0cols, tm)

    mu = rw_mu[j]
    vec = _pad_rows(jnp.stack([mu[r_:r_ + 512], mu[k_:k_ + 512], mu[v_:v_ + 512], rw_w0[j], rw_a0[j],
                               rw_k_k[j], rw_k_a[j], rw_r_k[j].reshape(-1), rw_gn_g[j], rw_gn_b[j]]), 16)
    mulo = _pad_rows(jnp.concatenate([mu[wl_:wl_ + 64], mu[al_:al_ + 64], mu[gl_:gl_ + 128]])[None], 8)
    w2p = _pad_rows(rw_w2[j], 128)
    a2p = jnp.zeros((128, 512), F32).at[64:].set(rw_a2[j])
    p0 = p0.reshape(batch, seq, -1)
    y_rw = _rwkv(p0, vec, mulo, w2p, a2p, rw_g2[j]).reshape(t, RW_WIDTH)
    y_gla = _gla(p0, _pad_rows(gla_gk_w2[j], 128), _pad_rows(gla_gk_b[j][None], 8),
                 _pad_rows(gla_norm_g[j][None], 8)).reshape(t, 512)

    wr, br = _route_weights(moe_wg[0], moe_bg[0], moe_we[0], moe_be[0])
    h, route, counts = _proj_route(y_rw, y_gla, xt, ab_w_out[j].astype(BF16), ln1[0], wr, br, tm)
    h = _moe(h, route, counts, moe_w1, moe_w3, moe_w2, 0, ln2[0])

    p1 = _matmul(h, cd_w_in[j].astype(BF16), tm)
    lb_sm = jax.nn.softmax(hg_lb.astype(F32), axis=0)
    lower = (jnp.cumsum(lb_sm, axis=0) - lb_sm[0])[1]
    y_hg = _hgrn(p1.reshape(batch, seq, -1), _pad_rows(lower[None], 8),
                 _pad_rows(hg_norm_g[j][None], 8)).reshape(t, 512)

    toep, bmat, emat, pw = _s5_tables(s5_a_re[j], s5_a_im[j], s5_log_dt[j], s5_b_re[j], s5_b_im[j],
                                      s5_c_re[j], s5_c_im[j], rows_per_seq)
    nrows = t // S5_CHUNK
    uc = p1[:, :512].reshape(nrows, S5_CHUNK, S5_GROUPS, S5_GROUP)
    uc = jnp.transpose(uc, (2, 0, 1, 3)).reshape(S5_GROUPS, nrows, S5_CHUNK * S5_GROUP).astype(BF16)
    yc = _s5_scan(uc, toep, bmat, emat, pw, rows_per_seq)
    y_ssm = jnp.transpose(yc.reshape(S5_GROUPS, nrows, S5_CHUNK, S5_GROUP), (1, 2, 0, 3)).reshape(t, 512)
    y_s5 = _s5_post(y_ssm, p1, _pad_rows(jnp.stack([s5_d[j], s5_glu_b[j]]), 8),
                    s5_glu_w[j].astype(BF16), tm)

    wr, br = _route_weights(moe_wg[1], moe_bg[1], moe_we[1], moe_be[1])
    h2, route, counts = _proj_route(y_s5, y_hg, h, cd_w_out[j].astype(BF16), ln1[1], wr, br, tm)
    out = _moe(h2, route, counts, moe_w1, moe_w3, moe_w2, 1, ln2[1])
    return out.reshape(batch, seq, d)
```

```python
import functools
import math

import numpy as np
import jax
import jax.numpy as jnp
from jax import lax
from jax.experimental import pallas as pl
from jax.experimental.pallas import tpu as pltpu

F32 = jnp.float32
BF16 = jnp.bfloat16

D_MODEL = 1024
DEPTH = 2
RW_HEAD = 64
RW_WIDTH = 512
RW_GN_EPS = 64e-5
GLA_HEADS = 4
GLA_DK = 64
GLA_DV = 128
GLA_GATE_TAU = 16.0
S5_GROUP = 16
S5_GROUPS = 32
S5_STATE = 64
HG_HEADS = 4
HG_DK = 128
CHUNK = 64
S5_CHUNK = 16
S5_SCAN_ROWS = 32
S5_TILE_GROUPS = 8
NORM_EPS = 1e-5
MOE_GROUPS = 4
MOE_PER_GROUP = 8
N_EXPERTS = 32
EXPERT_HIDDEN = 512
MOE_BLOCK = 128
COMBINE_TILE = 128
EXPERT_CHUNK = 512
ROW_TILE = 512
DN_ALPHA = (2.0 * DEPTH) ** 0.25

LANES = 128
VMEM_LIMIT = 56 * 1024 * 1024


def _cparams(n_axes=1):
    return pltpu.CompilerParams(dimension_semantics=("arbitrary",) * n_axes,
                                vmem_limit_bytes=VMEM_LIMIT)


def _d(a, b):
    return jnp.dot(a, b, preferred_element_type=F32)


def _d_nt(a, b):
    return lax.dot_general(a, b, (((1,), (1,)), ((), ())), preferred_element_type=F32)


def _d_tn(a, b):
    return lax.dot_general(a, b, (((0,), (0,)), ((), ())), preferred_element_type=F32)


def _split(a):
    hi = a.astype(BF16)
    lo = (a - hi.astype(F32)).astype(BF16)
    return hi, lo


def _split3(a):
    hi = a.astype(BF16)
    r1 = a - hi.astype(F32)
    mid = r1.astype(BF16)
    lo = (r1 - mid.astype(F32)).astype(BF16)
    return hi, mid, lo


def _bdot(a, b):
    return _d(a.astype(BF16), b.astype(BF16))


def _bdot_nt(a, b):
    return _d_nt(a.astype(BF16), b.astype(BF16))


def _bdot_tn(a, b):
    return _d_tn(a.astype(BF16), b.astype(BF16))


def _hdot_with(d, a, b):
    ah, al = _split(a)
    bh, bl = _split(b)
    return d(ah, bh) + (d(ah, bl) + d(al, bh))


def _hdot(a, b):
    return _hdot_with(_d, a, b)


def _hdot_nt(a, b):
    return _hdot_with(_d_nt, a, b)


def _hdot_tn(a, b):
    return _hdot_with(_d_tn, a, b)


def _xdot_l(a, e):
    ah, am, al = _split3(a)
    return _d(ah, e) + (_d(am, e) + _d(al, e))


def _xdot_r(e, a):
    ah, am, al = _split3(a)
    return _d(e, ah) + (_d(e, am) + _d(e, al))


def _iota(shape, dim):
    return lax.broadcasted_iota(jnp.int32, shape, dim)


def _softplus(x):
    return jnp.maximum(x, 0.0) + jnp.log1p(jnp.exp(-jnp.abs(x)))


def _sigmoid(x):
    return 1.0 / (1.0 + jnp.exp(-x))


def _silu(x):
    return x * _sigmoid(x)


def _tril_incl(n):
    return jnp.where(_iota((n, n), 0) >= _iota((n, n), 1), 1.0, 0.0).astype(BF16)


def _cumsum_rows(g):
    return _xdot_r(_tril_incl(g.shape[0]), g)


def _shift_mix(x, prev_ref, b, mu):
    c = x.shape[0]
    rolled = pltpu.roll(x, 1, 0)
    prev = jnp.where(_iota(x.shape, 0) == 0, jnp.broadcast_to(prev_ref[b, 0:1, :], x.shape), rolled)
    prev_ref[b, 0:1, :] = x[c - 1:c, :]
    return x + mu * (prev - x)


def _layer_norm(x, g, b):
    mu = jnp.mean(x, axis=-1, keepdims=True)
    xc = x - mu
    var = jnp.mean(xc * xc, axis=-1, keepdims=True)
    return xc * lax.rsqrt(var + NORM_EPS) * g + b


def _mm_body(x_ref, w_ref, o_ref):
    o_ref[...] = _d(x_ref[...].astype(BF16), w_ref[...])


def _matmul(x, w_bf16, tm):
    m, k = x.shape
    n = w_bf16.shape[1]
    return pl.pallas_call(
        _mm_body,
        grid=(m // tm,),
        in_specs=[pl.BlockSpec((tm, k), lambda i: (i, 0)),
                  pl.BlockSpec((k, n), lambda i: (0, 0))],
        out_specs=pl.BlockSpec((tm, n), lambda i: (i, 0)),
        out_shape=jax.ShapeDtypeStruct((m, n), F32),
        compiler_params=_cparams(1),
        name="in_proj",
    )(x, w_bf16)


_RV_MU_R, _RV_MU_K, _RV_MU_V, _RV_W0, _RV_A0, _RV_KK, _RV_KA, _RV_RK, _RV_GNG, _RV_GNB = range(10)


def _rwkv_body(r_ref, k_ref, v_ref, lo_ref, vec_ref, mulo_ref, w2_ref, a2_ref, g2_ref, ones_ref,
               o_ref, pr_ref, pk_ref, pv_ref, plo_ref, st_ref):
    nb, c = r_ref.shape[0], r_ref.shape[1]
    npair = RW_WIDTH // LANES

    @pl.when(pl.program_id(0) == 0)
    def _():
        pr_ref[...] = jnp.zeros_like(pr_ref)
        pk_ref[...] = jnp.zeros_like(pk_ref)
        pv_ref[...] = jnp.zeros_like(pv_ref)
        plo_ref[...] = jnp.zeros_like(plo_ref)
        st_ref[...] = jnp.zeros_like(st_ref)

    def vec(i):
        return vec_ref[i:i + 1, :]

    ones_bd = ones_ref[...]
    lane = _iota((c, LANES), 1)
    m1 = lane < RW_HEAD
    row2 = _iota((2 * c, 4 * c), 0)
    col2 = _iota((2 * c, 4 * c), 1) & (c - 1)
    tri = ((row2 < c) & (row2 > col2)) | ((row2 >= c) & ((row2 - c) >= col2))
    eye2 = jnp.where(_iota((2 * c, 2 * c), 0) == _iota((2 * c, 2 * c), 1), 1.0, 0.0)
    bd_p = (_iota((LANES, LANES), 0) >> 6) == (_iota((LANES, LANES), 1) >> 6)

    def halves(x):
        return jnp.concatenate([jnp.where(m1, x, 0.0), jnp.where(m1, 0.0, x)], axis=0)

    prep = []
    for b in range(nb):
        xr = _shift_mix(r_ref[b], pr_ref, b, vec(_RV_MU_R))
        xk = _shift_mix(k_ref[b], pk_ref, b, vec(_RV_MU_K))
        xv = _shift_mix(v_ref[b], pv_ref, b, vec(_RV_MU_V))
        xlo = _shift_mix(lo_ref[b], plo_ref, b, mulo_ref[0:1, :])
        lo_a = xlo[:, :LANES]
        lo_g = xlo[:, LANES:]
        w = -_softplus(-(vec(_RV_W0) + _bdot(jnp.tanh(lo_a), w2_ref[...]))) - 0.5
        g = -jnp.exp(w)
        a = _sigmoid(vec(_RV_A0) + _bdot(lo_a, a2_ref[...]))
        gate = _bdot(_sigmoid(lo_g), g2_ref[...])
        kk = xk * vec(_RV_KK)
        kk = kk / jnp.maximum(jnp.sqrt(_xdot_l(kk * kk, ones_bd)), 1e-12)
        k2 = xk * (1.0 + (a - 1.0) * vec(_RV_KA))
        gc = _cumsum_rows(g)
        g_last = gc[c - 1:c, :]
        e_neg = jnp.exp(-gc)
        e_end = jnp.exp(g_last - gc)
        prep.append(dict(
            xr=xr, xv=xv, k2=k2, gate=gate,
            at=-kk * jnp.exp(gc - g),
            bt=(kk * a) * e_neg, kt=k2 * e_neg, rt=xr * jnp.exp(gc),
            bh=(kk * a) * e_end, kh=k2 * e_end,
            gam=jnp.exp(g_last)))

    chains = [(b, p) for b in range(nb) for p in range(npair)]

    def part(ch, name):
        b, p = ch
        return prep[b][name][:, p * LANES:(p + 1) * LANES]

    a_ak, a_row, vv, ak, pinv = {}, {}, {}, {}, {}
    for ch in chains:
        lhs = jnp.concatenate([part(ch, 'at'), part(ch, 'rt')], axis=0)
        rhs = jnp.concatenate([halves(part(ch, 'bt')), halves(part(ch, 'kt'))], axis=0)
        aa = jnp.where(tri, _bdot_nt(lhs, rhs), 0.0)
        a_ak[ch] = aa[:c, 2 * c:].astype(BF16)
        a_row[ch] = aa[c:, :].astype(BF16)
        abd = halves(aa[:c, :2 * c])
        pinv[ch] = eye2 + abd
        ak[ch] = abd
        vv[ch] = halves(part(ch, 'xv')).astype(BF16)
    nlev = int(math.log2(c))
    for lev in range(nlev):
        for ch in chains:
            akb = ak[ch].astype(BF16)
            if lev == 0:
                ak[ch] = _d(akb, akb)
            elif lev < nlev - 1:
                out = _d(akb, jnp.concatenate([akb, pinv[ch].astype(BF16)], axis=1))
                ak[ch] = out[:, :2 * c]
                pinv[ch] = pinv[ch] + out[:, 2 * c:]
            else:
                pinv[ch] = pinv[ch] + _d(akb, pinv[ch].astype(BF16))
    x2 = {ch: _d(a_ak[ch], vv[ch]) for ch in chains}

    sts = {ch: st_ref[ch[0] * npair + ch[1]] for ch in chains}
    xs = {ch: _d_nt(jnp.concatenate([part(ch, 'at'), part(ch, 'rt')], axis=0).astype(BF16),
                    sts[ch].astype(BF16)) for ch in chains}
    us = {}
    for ch in chains:
        u2 = _d(pinv[ch].astype(BF16), halves(xs[ch][:c] + x2[ch]).astype(BF16))
        us[ch] = u2[:c] + u2[c:]
    ys = {}
    for ch in chains:
        b, p = ch
        u = us[ch]
        ys[ch] = xs[ch][c:] + _d(a_row[ch], jnp.concatenate([halves(u).astype(BF16), vv[ch]], axis=0))
        upd = _d_tn(jnp.concatenate([u, part(ch, 'xv')], axis=0).astype(BF16),
                    jnp.concatenate([part(ch, 'bh'), part(ch, 'kh')], axis=0).astype(BF16))
        st_ref[b * npair + p] = sts[ch] * part(ch, 'gam') + jnp.where(bd_p, upd, 0.0)

    inv_n = 1.0 / RW_HEAD
    for b in range(nb):
        y = jnp.concatenate([ys[(b, p)] for p in range(npair)], axis=1)
        mean = _xdot_l(y, ones_bd) * inv_n
        yc = y - mean
        var = _xdot_l(yc * yc, ones_bd) * inv_n
        yn = yc * lax.rsqrt(var + RW_GN_EPS) * vec(_RV_GNG) + vec(_RV_GNB)
        bonus = _xdot_l(prep[b]['xr'] * prep[b]['k2'] * vec(_RV_RK), ones_bd) * prep[b]['xv']
        o_ref[b] = (yn + bonus) * prep[b]['gate']


def _rwkv(p0, vec, mulo, w2p, a2p, g2):
    batch, seq, _ = p0.shape
    nc = seq // CHUNK
    c = CHUNK
    ones_bd = jnp.asarray(np.kron(np.eye(RW_WIDTH // RW_HEAD), np.ones((RW_HEAD, RW_HEAD))), BF16)

    def col(j, width):
        return pl.BlockSpec((batch, c, width), lambda i: (0, i, j))

    def full(shape):
        return pl.BlockSpec(shape, lambda i: (0,) * len(shape))

    return pl.pallas_call(
        _rwkv_body,
        grid=(nc,),
        in_specs=[col(0, 512), col(1, 512), col(2, 512), col(10, 256),
                  full((16, 512)), full((8, 256)), full((128, 512)), full((128, 512)),
                  full((128, 512)), full((512, 512))],
        out_specs=pl.BlockSpec((batch, c, 512), lambda i: (0, i, 0)),
        out_shape=jax.ShapeDtypeStruct((batch, seq, RW_WIDTH), F32),
        scratch_shapes=[pltpu.VMEM((batch, 8, 512), F32), pltpu.VMEM((batch, 8, 512), F32),
                        pltpu.VMEM((batch, 8, 512), F32), pltpu.VMEM((batch, 8, 256), F32),
                        pltpu.VMEM((batch * RW_WIDTH // LANES, LANES, LANES), F32)],
        compiler_params=_cparams(1),
        name="rwkv7",
    )(p0, p0, p0, p0, vec, mulo, w2p, a2p, g2, ones_bd)


def _gla_core(qs, ks, vs, gs, st_ref, heads_per_block):
    nb = len(qs)
    c = qs[0].shape[0]
    hpb = heads_per_block
    nblk = qs[0].shape[1] // LANES
    dk_shift = int(math.log2(LANES // hpb))
    row = _iota(qs[0].shape, 0)

    prep = []
    for q, k, g in zip(qs, ks, gs):
        b = _cumsum_rows(g)

        def brow(i, b=b):
            return jnp.broadcast_to(b[i:i + 1, :], b.shape)

        b15, b31, b47, blast = brow(15), brow(31), brow(47), brow(c - 1)
        ref_b = jnp.where(row < 32, b15, b47)
        ref_d = jnp.where(row < 16, 0.0, jnp.where(row < 32, b15, jnp.where(row < 48, b31, b47)))
        prep.append(dict(
            q_a=q * jnp.exp(jnp.minimum(b - b31, 0.0)), k_a=k * jnp.exp(jnp.minimum(b31 - b, 0.0)),
            q_b=q * jnp.exp(jnp.minimum(b - ref_b, 0.0)), k_b=k * jnp.exp(jnp.minimum(ref_b - b, 0.0)),
            q_d=q * jnp.exp(b - ref_d), k_d=k * jnp.exp(ref_d - b),
            q_i=q * jnp.exp(b), k_s=k * jnp.exp(blast - b), gam=jnp.exp(b[c - 1:c, :])))

    ri = _iota((hpb * c, c), 0) & (c - 1)
    ci = _iota((hpb * c, c), 1)
    mask_a = (ri >= 32) & (ci < 32)
    mask_b = ((ri >> 5) == (ci >> 5)) & (((ri >> 4) & 1) == 1) & (((ci >> 4) & 1) == 0)
    mask_d = ((ri >> 4) == (ci >> 4)) & (ri >= ci)
    lane = _iota((c, LANES), 1)
    bd = (_iota((hpb * LANES, LANES), 0) >> 7) == (_iota((hpb * LANES, LANES), 1) >> dk_shift)

    def heads_rows(x):
        if hpb == 1:
            return x
        return jnp.concatenate([jnp.where((lane >> dk_shift) == h, x, 0.0) for h in range(hpb)], axis=0)

    chains = [(bi, blk) for bi in range(nb) for blk in range(nblk)]

    def part(ch, name):
        bi, blk = ch
        return prep[bi][name][:, blk * LANES:(blk + 1) * LANES]

    v_p = {ch: vs[ch[0]][:, ch[1] * hpb * LANES:(ch[1] + 1) * hpb * LANES].astype(BF16) for ch in chains}
    probs = {}
    for ch in chains:
        s_a = _bdot_nt(heads_rows(part(ch, 'q_a')), part(ch, 'k_a'))
        s_b = _bdot_nt(heads_rows(part(ch, 'q_b')), part(ch, 'k_b'))
        s_d = _bdot_nt(heads_rows(part(ch, 'q_d')), part(ch, 'k_d'))
        probs[ch] = (jnp.where(mask_a, s_a, 0.0) + jnp.where(mask_b, s_b, 0.0)
                     + jnp.where(mask_d, s_d, 0.0)).astype(BF16)
    outs = {}
    for ch in chains:
        pv = _d(probs[ch], v_p[ch])
        o = pv[:c]
        for h in range(1, hpb):
            o = jnp.where((_iota(o.shape, 1) >> 7) == h, pv[h * c:(h + 1) * c], o)
        si = ch[0] * nblk + ch[1]
        sp = st_ref[si]
        outs[ch] = o + _bdot_nt(part(ch, 'q_i'), sp)
        upd = _d_tn(v_p[ch], part(ch, 'k_s').astype(BF16))
        st_ref[si] = sp * part(ch, 'gam') + jnp.where(bd, upd, 0.0)
    return [jnp.concatenate([outs[(bi, blk)] for blk in range(nblk)], axis=1) for bi in range(nb)]


def _gated_rmsnorm(o, gate, norm_g):
    nh = o.shape[1] // LANES
    outs = []
    for h in range(nh):
        sl = slice(h * LANES, (h + 1) * LANES)
        oh = o[:, sl]
        ms = jnp.mean(oh * oh, axis=-1, keepdims=True)
        outs.append(oh * lax.rsqrt(ms + NORM_EPS) * norm_g * _silu(gate[:, sl]))
    return jnp.concatenate(outs, axis=1)


def _gla_body(q_ref, k_ref, v_ref, gate_ref, gk_ref, w2_ref, vec_ref, ng_ref, o_ref, st_ref):
    nb = q_ref.shape[0]

    @pl.when(pl.program_id(0) == 0)
    def _():
        st_ref[...] = jnp.zeros_like(st_ref)

    gs = [-_softplus(-(_hdot(gk_ref[b], w2_ref[...]) + vec_ref[0:1, :])) * (1.0 / GLA_GATE_TAU)
          for b in range(nb)]
    qs = [q_ref[b] * (GLA_DK ** -0.5) for b in range(nb)]
    os_ = _gla_core(qs, [k_ref[b] for b in range(nb)], [v_ref[b] for b in range(nb)], gs, st_ref, 2)
    for b in range(nb):
        o_ref[b] = _gated_rmsnorm(os_[b], gate_ref[b], ng_ref[0:1, :])


def _seq_specs(batch, c):
    def col(j, width):
        return pl.BlockSpec((batch, c, width), lambda i: (0, i, j))

    def full(shape):
        return pl.BlockSpec(shape, lambda i: (0,) * len(shape))

    return col, full


def _gla(p0, gk_w2p, gk_b, norm_g):
    batch, seq, _ = p0.shape
    c = CHUNK
    col, full = _seq_specs(batch, c)
    return pl.pallas_call(
        _gla_body,
        grid=(seq // c,),
        in_specs=[col(11, 256), col(12, 256), col(3, 512), col(4, 512), col(26, 128),
                  full((128, 256)), full((8, 256)), full((8, 128))],
        out_specs=pl.BlockSpec((batch, c, 512), lambda i: (0, i, 0)),
        out_shape=jax.ShapeDtypeStruct((batch, seq, 512), F32),
        scratch_shapes=[pltpu.VMEM((batch * 2, 2 * LANES, LANES), F32)],
        compiler_params=_cparams(1),
        name="gla",
    )(p0, p0, p0, p0, p0, gk_w2p, gk_b, norm_g)


def _hgrn_body(q_ref, f_ref, i_ref, gate_ref, lb_ref, ng_ref, o_ref, st_ref):
    nb = q_ref.shape[0]

    @pl.when(pl.program_id(0) == 0)
    def _():
        st_ref[...] = jnp.zeros_like(st_ref)

    lb = lb_ref[0:1, :]
    qs, ks, gs = [], [], []
    for b in range(nb):
        f = f_ref[b]
        gs.append(jnp.log(lb + (1.0 - lb) * _sigmoid(f)))
        ks.append((1.0 - lb) * _sigmoid(-f))
        qs.append(_silu(q_ref[b]))
    os_ = _gla_core(qs, ks, [i_ref[b] for b in range(nb)], gs, st_ref, 1)
    for b in range(nb):
        o_ref[b] = _gated_rmsnorm(os_[b], gate_ref[b], ng_ref[0:1, :])


def _hgrn(p1, lb, norm_g):
    batch, seq, _ = p1.shape
    c = CHUNK
    col, full = _seq_specs(batch, c)
    return pl.pallas_call(
        _hgrn_body,
        grid=(seq // c,),
        in_specs=[col(1, 512), col(2, 512), col(3, 512), col(4, 512), full((8, 512)), full((8, 128))],
        out_specs=pl.BlockSpec((batch, c, 512), lambda i: (0, i, 0)),
        out_shape=jax.ShapeDtypeStruct((batch, seq, 512), F32),
        scratch_shapes=[pltpu.VMEM((batch * HG_HEADS, LANES, LANES), F32)],
        compiler_params=_cparams(1),
        name="hgrn2",
    )(p1, p1, p1, p1, lb, norm_g)


def _s5_body(u_ref, toep_ref, bm_ref, e_ref, pw_ref, o_ref):
    seq = u_ref.shape[0]
    rows = seq // S5_CHUNK
    u = jnp.concatenate([u_ref[pl.ds(tk, rows, stride=S5_CHUNK), :] for tk in range(S5_CHUNK)],
                        axis=1).astype(BF16)
    s = _d(u, bm_ref[0])
    rin = _iota(s.shape, 0)
    pw = pw_ref[0]

    def swap_halves(x):
        return jnp.concatenate([pltpu.roll(x[:, k * LANES:(k + 1) * LANES], S5_STATE, 1)
                                for k in range(x.shape[1] // LANES)], axis=1)

    h = s
    for lev in range(int(math.log2(rows))):
        sh = 1 << lev
        hs = jnp.where(rin >= sh, pltpu.roll(h, sh, 0), 0.0)
        h = h + pw[2 * lev:2 * lev + 1, :] * hs + pw[2 * lev + 1:2 * lev + 2, :] * swap_halves(hs)
    hprev = jnp.where(rin >= 1, pltpu.roll(h, 1, 0), 0.0)
    y = _d(u, toep_ref[0]) + _bdot(hprev, e_ref[0])
    for tk in range(S5_CHUNK):
        o_ref[pl.ds(tk, rows, stride=S5_CHUNK), :] = y[:, tk * LANES:(tk + 1) * LANES]


def _s5_scan(p1, toep, bmat, emat, pw):
    batch, seq, _ = p1.shape
    nt, width, nstate = bmat.shape

    def table(shape):
        return pl.BlockSpec((1,) + shape, lambda j, b: (j, 0, 0), pipeline_mode=pl.Buffered(1))

    return pl.pallas_call(
        _s5_body,
        grid=(nt, batch),
        in_specs=[pl.BlockSpec((None, seq, LANES), lambda j, b: (b, 0, j)),
                  table((width, width)), table((width, nstate)), table((nstate, width)),
                  table((S5_SCAN_ROWS, nstate))],
        out_specs=pl.BlockSpec((None, seq, LANES), lambda j, b: (b, 0, j)),
        out_shape=jax.ShapeDtypeStruct((batch, seq, nt * LANES), F32),
        compiler_params=_cparams(2),
        name="s5_scan",
    )(p1, toep, bmat, emat, pw)


def _s5_post_body(y_ref, u_ref, vec_ref, w_ref, o_ref):
    y = y_ref[...] + vec_ref[0:1, :] * u_ref[...]
    y = 0.5 * y * (1.0 + jnp.tanh(math.sqrt(2.0 / math.pi) * (y + 0.044715 * (y * y * y))))
    o_ref[...] = y * _sigmoid(_bdot(y, w_ref[...]) + vec_ref[1:2, :])


def _s5_post(y_ssm, p1, vec, glu_w, tm):
    t = y_ssm.shape[0]
    return pl.pallas_call(
        _s5_post_body,
        grid=(t // tm,),
        in_specs=[pl.BlockSpec((tm, 512), lambda i: (i, 0)),
                  pl.BlockSpec((tm, 512), lambda i: (i, 0)),
                  pl.BlockSpec((8, 512), lambda i: (0, 0)),
                  pl.BlockSpec((512, 512), lambda i: (0, 0))],
        out_specs=pl.BlockSpec((tm, 512), lambda i: (i, 0)),
        out_shape=jax.ShapeDtypeStruct((t, 512), F32),
        compiler_params=_cparams(1),
        name="s5_post",
    )(y_ssm, p1, vec, glu_w)


def _s5_tables(a_re, a_im, log_dt, b_re, b_im, c_re, c_im, rows_per_seq):
    c = S5_CHUNK
    lam_re = jnp.minimum(a_re, -1e-4)
    lam_im = a_im
    dt = jnp.exp(log_dt)[:, None]
    mag = jnp.exp(lam_re * dt)
    abar_re = mag * jnp.cos(lam_im * dt)
    abar_im = mag * jnp.sin(lam_im * dt)
    den = lam_re * lam_re + lam_im * lam_im
    num_re = abar_re - 1.0
    z_re = (num_re * lam_re + abar_im * lam_im) / den
    z_im = (abar_im * lam_re - num_re * lam_im) / den

    def power(n):
        n = jnp.asarray(n, F32)[..., None, None]
        m = jnp.exp(n * (lam_re * dt))
        return m * jnp.cos(n * (lam_im * dt)), m * jnp.sin(n * (lam_im * dt))

    def cmul(ar, ai, br, bi):
        return ar * br - ai * bi, ar * bi + ai * br

    tau = jnp.arange(c)
    p_re, p_im = power(tau)
    zb_re, zb_im = cmul(z_re[..., None], z_im[..., None], b_re, b_im)
    cp_re, cp_im = cmul(c_re[None], c_im[None], p_re[:, :, None, :], p_im[:, :, None, :])
    hi = lax.Precision.HIGHEST
    kern = (jnp.einsum('tgon,gni->gtoi', cp_re, zb_re, precision=hi)
            - jnp.einsum('tgon,gni->gtoi', cp_im, zb_im, precision=hi))
    tt = jnp.arange(c)[None, :] - jnp.arange(c)[:, None]
    kt = jnp.where((tt >= 0)[None, :, :, None, None], kern[:, jnp.clip(tt, 0, c - 1)], 0.0)
    toep = jnp.transpose(kt, (0, 1, 4, 2, 3)).reshape(S5_GROUPS, c * S5_GROUP, c * S5_GROUP)
    q_re, q_im = power(c - 1 - tau)
    bm_re, bm_im = cmul(q_re[..., None], q_im[..., None], zb_re[None], zb_im[None])
    bmat = jnp.concatenate([jnp.transpose(bm_re, (1, 0, 3, 2)), jnp.transpose(bm_im, (1, 0, 3, 2))],
                           axis=-1).reshape(S5_GROUPS, c * S5_GROUP, 2 * S5_STATE)
    r_re, r_im = power(tau + 1)
    e_re, e_im = cmul(c_re[None], c_im[None], r_re[:, :, None, :], r_im[:, :, None, :])
    emat = jnp.concatenate([jnp.transpose(e_re, (1, 3, 0, 2)), -jnp.transpose(e_im, (1, 3, 0, 2))],
                           axis=1).reshape(S5_GROUPS, 2 * S5_STATE, c * S5_GROUP)
    nlev = int(math.log2(rows_per_seq))
    assert 2 * nlev <= S5_SCAN_ROWS
    s_re, s_im = power(c * (2 ** jnp.arange(nlev)))
    pw = jnp.zeros((S5_GROUPS, S5_SCAN_ROWS, 2 * S5_STATE), F32)
    pw = pw.at[:, 0:2 * nlev:2, :].set(jnp.transpose(jnp.concatenate([s_re, s_re], -1), (1, 0, 2)))
    pw = pw.at[:, 1:2 * nlev:2, :].set(jnp.transpose(jnp.concatenate([-s_im, s_im], -1), (1, 0, 2)))
    nt, gl = S5_GROUPS // S5_TILE_GROUPS, S5_TILE_GROUPS
    eye = jnp.eye(gl, dtype=F32)
    toep6 = toep.reshape(nt, gl, c, S5_GROUP, c, S5_GROUP)
    toep_t = jnp.einsum('jgtcuo,gh->jtgcuho', toep6, eye).reshape(nt, c * LANES, c * LANES)
    bmat5 = bmat.reshape(nt, gl, c, S5_GROUP, 2 * S5_STATE)
    bmat_t = jnp.einsum('jgtcn,gh->jtgchn', bmat5, eye).reshape(nt, c * LANES, gl * 2 * S5_STATE)
    emat5 = emat.reshape(nt, gl, 2 * S5_STATE, c, S5_GROUP)
    emat_t = jnp.einsum('jgnuo,gh->jgnuho', emat5, eye).reshape(nt, gl * 2 * S5_STATE, c * LANES)
    pw_t = jnp.transpose(pw.reshape(nt, gl, S5_SCAN_ROWS, 2 * S5_STATE), (0, 2, 1, 3))
    pw_t = pw_t.reshape(nt, S5_SCAN_ROWS, gl * 2 * S5_STATE)
    return toep_t.astype(BF16), bmat_t.astype(BF16), emat_t.astype(BF16), pw_t


def _proj_route_body(ya_ref, yb_ref, x_ref, w_ref, ln_ref, wr_ref, br_ref,
                     h_ref, route_ref, cnt_ref, carry_ref):
    tm = ya_ref.shape[0]
    half = w_ref.shape[0] // 2

    @pl.when(pl.program_id(0) == 0)
    def _():
        carry_ref[...] = jnp.zeros_like(carry_ref)

    mix = _bdot(ya_ref[...], w_ref[:half, :]) + _bdot(yb_ref[...], w_ref[half:, :])
    h = _layer_norm(DN_ALPHA * x_ref[...] + mix, ln_ref[0:1, :], ln_ref[1:2, :])
    h_ref[...] = h

    logits = _hdot(h, wr_ref[...]) + br_ref[0:1, :]
    lane = _iota((tm, LANES), 1).astype(F32)
    neg = -jnp.inf

    def softmax_masked(mask):
        xm = jnp.where(mask, logits, neg)
        m = jnp.max(xm, axis=-1, keepdims=True)
        e = jnp.exp(xm - m)
        return e / jnp.sum(e, axis=-1, keepdims=True)

    def top1(pm):
        m = jnp.max(pm, axis=-1, keepdims=True)
        idx = jnp.min(jnp.where(pm == m, lane, float(LANES)), axis=-1, keepdims=True)
        return m, idx

    coarse = jnp.where(lane < MOE_GROUPS, softmax_masked(lane < MOE_GROUPS), -1.0)
    p_grp, grp = top1(coarse)
    lo = MOE_GROUPS + MOE_PER_GROUP * grp
    fmask = (lane >= lo) & (lane < lo + MOE_PER_GROUP)
    fine = jnp.where(fmask, softmax_masked(fmask), -1.0)
    p1, j1 = top1(fine)
    p2, j2 = top1(jnp.where(lane == j1, -1.0, fine))
    denom = p1 + p2
    g1 = p_grp * (p1 / denom)
    g2 = p_grp * (p2 / denom)
    e1 = j1 - MOE_GROUPS
    e2 = j2 - MOE_GROUPS

    oh1 = jnp.where(lane == e1, 1.0, 0.0)
    oh2 = jnp.where(lane == e2, 1.0, 0.0)
    cnt = oh1 + oh2
    strict = jnp.where(_iota((tm, tm), 0) > _iota((tm, tm), 1), 1.0, 0.0).astype(BF16)
    before = _d(strict, cnt.astype(BF16)) + carry_ref[0:1, :]
    r1 = jnp.sum(oh1 * before, axis=-1, keepdims=True)
    r2 = jnp.sum(oh2 * before, axis=-1, keepdims=True)
    carry_ref[0:1, :] = carry_ref[0:1, :] + jnp.sum(cnt, axis=0, keepdims=True)
    cnt_ref[...] = carry_ref[...]

    out = jnp.where(lane == 0, e1, 0.0)
    out = jnp.where(lane == 1, e2, out)
    out = jnp.where(lane == 2, r1, out)
    out = jnp.where(lane == 3, r2, out)
    out = jnp.where(lane == 4, g1, out)
    out = jnp.where(lane == 5, g2, out)
    route_ref[...] = out[:, :8]


def _proj_route(ya, yb, resid, w_out, ln, wr, br, tm):
    t, d = resid.shape
    return pl.pallas_call(
        _proj_route_body,
        grid=(t // tm,),
        in_specs=[pl.BlockSpec((tm, 512), lambda i: (i, 0)),
                  pl.BlockSpec((tm, 512), lambda i: (i, 0)),
                  pl.BlockSpec((tm, d), lambda i: (i, 0)),
                  pl.BlockSpec((d, d), lambda i: (0, 0)),
                  pl.BlockSpec((8, d), lambda i: (0, 0)),
                  pl.BlockSpec((d, LANES), lambda i: (0, 0)),
                  pl.BlockSpec((8, LANES), lambda i: (0, 0))],
        out_specs=[pl.BlockSpec((tm, d), lambda i: (i, 0)),
                   pl.BlockSpec((tm, 8), lambda i: (i, 0)),
                   pl.BlockSpec((8, LANES), lambda i: (0, 0))],
        out_shape=[jax.ShapeDtypeStruct((t, d), F32),
                   jax.ShapeDtypeStruct((t, 8), F32),
                   jax.ShapeDtypeStruct((8, LANES), F32)],
        scratch_shapes=[pltpu.VMEM((8, LANES), F32)],
        compiler_params=_cparams(1),
        name="proj_ln_route",
    )(ya, yb, resid, w_out, ln, wr, br)


def _row_copy(src_ref, src_row, dst_ref, dst_row, sem):
    return pltpu.make_async_copy(src_ref.at[pl.ds(src_row, 1)], dst_ref.at[pl.ds(dst_row, 1)], sem)


def _start_all(copies):
    for n, cp in enumerate(copies):
        cp.start(priority=n % 2)


def _dispatch_body(pad_start_ref, pad_len_ref, used_rows_ref, dest_ref, h_ref, rows_ref, hbuf_ref, zbuf_ref,
                   lsem, ssem, zsem):
    i = pl.program_id(0)
    last = pl.num_programs(0) - 1
    tm = hbuf_ref.shape[1]
    slot = i % 2

    def load(tile, s):
        return pltpu.make_async_copy(h_ref.at[pl.ds(tile * tm, tm)], hbuf_ref.at[s], lsem.at[s])

    def scatters(s):
        return [_row_copy(hbuf_ref.at[s], r, rows_ref, dest_ref[0, 0, 2 * r + q], ssem.at[s])
                for r in range(tm) for q in range(2)]

    @pl.when(i == 0)
    def _():
        load(0, 0).start()

    @pl.when(i >= 1)
    def _():
        for cp in scatters(1 - slot):
            cp.wait()

    @pl.when(i < last)
    def _():
        load(i + 1, 1 - slot).start()

    load(i, slot).wait()
    _start_all(scatters(slot))

    @pl.when(i == last)
    def _():
        for cp in scatters(slot):
            cp.wait()
        zbuf_ref[...] = jnp.zeros_like(zbuf_ref)
        sub = 8
        half = zbuf_ref.shape[0]

        def pad_fills():
            out = []
            for e in range(N_EXPERTS):
                start = pad_start_ref[e]
                npad = pad_len_ref[e]
                end = start + npad
                run = half
                while run >= sub:
                    end = end - (npad & run)
                    dst = rows_ref.at[pl.ds(pl.multiple_of(end, run), run)]
                    out.append(((npad & run) != 0,
                                pltpu.make_async_copy(zbuf_ref.at[pl.ds(0, run)], dst, zsem)))
                    run //= 2
                for k in range(sub - 1):
                    out.append((k < (npad & (sub - 1)), _row_copy(zbuf_ref, 0, rows_ref, start + k, zsem)))
            return out

        for pred, cp in pad_fills():
            pl.when(pred)(cp.start)
        for pred, cp in pad_fills():
            pl.when(pred)(cp.wait)

        def tail(b):
            return pltpu.make_async_copy(
                zbuf_ref, rows_ref.at[pl.ds(pl.multiple_of(b * half, half), half)], zsem)

        first_free = used_rows_ref[0] // half
        n_half = rows_ref.shape[0] // half
        lax.fori_loop(first_free, n_half, lambda b, c: (tail(b).start(), c)[1], 0)
        lax.fori_loop(first_free, n_half, lambda b, c: (tail(b).wait(), c)[1], 0)


def _dispatch(pad_start, pad_len, used_rows, dest3, h, n_rows):
    t, d = h.shape
    nt, _, tm2 = dest3.shape
    tm = tm2 // 2
    grid_spec = pltpu.PrefetchScalarGridSpec(
        num_scalar_prefetch=3,
        grid=(nt,),
        in_specs=[pl.BlockSpec((1, 1, tm2), lambda i, *_: (i, 0, 0), memory_space=pltpu.SMEM),
                  pl.BlockSpec(memory_space=pl.ANY)],
        out_specs=pl.BlockSpec(memory_space=pl.ANY),
        scratch_shapes=[pltpu.VMEM((2, tm, d), F32), pltpu.VMEM((MOE_BLOCK // 2, d), F32),
                        pltpu.SemaphoreType.DMA((2,)), pltpu.SemaphoreType.DMA((2,)),
                        pltpu.SemaphoreType.DMA],
    )
    return pl.pallas_call(
        _dispatch_body,
        grid_spec=grid_spec,
        out_shape=jax.ShapeDtypeStruct((n_rows, d), F32),
        compiler_params=_cparams(1),
        name="moe_dispatch",
    )(pad_start, pad_len, used_rows, dest3, h)


def _expert_body(first_ref, nblk_ref, nused_ref, x_ref, w1_ref, w3_ref, w2_ref, y_ref,
                 xbuf_ref, ybuf_ref, w1b_ref, w3b_ref, w2b_ref, xsem, ysem):
    e = pl.program_id(0)
    row0 = first_ref[e] * MOE_BLOCK
    nblk = nblk_ref[e]
    big = xbuf_ref.shape[1]
    per_big = big // MOE_BLOCK

    def cast_weights():
        w1b_ref[...] = w1_ref[0].astype(BF16)
        w3b_ref[...] = w3_ref[0].astype(BF16)
        w2b_ref[...] = w2_ref[0].astype(BF16)

    def stream(start_row, count, size, casts_first):
        def x_copy(j, s):
            src = x_ref.at[pl.ds(pl.multiple_of(start_row + j * size, MOE_BLOCK), size)]
            return pltpu.make_async_copy(src, xbuf_ref.at[s, pl.ds(0, size)], xsem.at[s])

        def y_copy(j, s):
            dst = y_ref.at[pl.ds(pl.multiple_of(start_row + j * size, MOE_BLOCK), size)]
            return pltpu.make_async_copy(ybuf_ref.at[s, pl.ds(0, size)], dst, ysem.at[s])

        @pl.when(count > 0)
        def _():
            x_copy(0, 0).start(priority=1)
            pl.when(casts_first)(cast_weights)

            def chunk(j, carry):
                s = j % 2

                @pl.when(j + 1 < count)
                def _():
                    x_copy(j + 1, 1 - s).start(priority=1)

                x_copy(j, s).wait()
                xb = xbuf_ref[s, pl.ds(0, size), :].astype(BF16)
                hid = _silu(_d(xb, w1b_ref[...])) * _d(xb, w3b_ref[...])

                @pl.when(j >= 2)
                def _():
                    y_copy(j - 2, s).wait()

                ybuf_ref[s, pl.ds(0, size), :] = _d(hid.astype(BF16), w2b_ref[...])
                y_copy(j, s).start()
                return carry

            lax.fori_loop(0, count, chunk, 0)

            @pl.when(count >= 2)
            def _():
                y_copy(count - 2, count % 2).wait()

            y_copy(count - 1, (count - 1) % 2).wait()

    @pl.when(nblk > 0)
    def _():
        nbig = nblk // per_big
        stream(row0, nbig, big, nbig > 0)
        stream(row0 + nbig * big, nblk - nbig * per_big, MOE_BLOCK, nbig == 0)

    @pl.when(e == pl.num_programs(0) - 1)
    def _():
        ybuf_ref[0, pl.ds(0, MOE_BLOCK), :] = jnp.zeros((MOE_BLOCK, ybuf_ref.shape[2]), F32)
        nused = nused_ref[0]
        ntot = y_ref.shape[0] // MOE_BLOCK

        def fill(b, carry):
            cp = pltpu.make_async_copy(ybuf_ref.at[0, pl.ds(0, MOE_BLOCK)],
                                       y_ref.at[pl.ds(b * MOE_BLOCK, MOE_BLOCK)], ysem.at[0])
            cp.start()
            cp.wait()
            return carry

        lax.fori_loop(nused, ntot, fill, 0)


def _experts(first_blk, nblk, nused, x_rows, w1, w3, w2, layer):
    r, d = x_rows.shape
    hid = w1.shape[-1]
    grid_spec = pltpu.PrefetchScalarGridSpec(
        num_scalar_prefetch=3,
        grid=(N_EXPERTS,),
        in_specs=[pl.BlockSpec(memory_space=pl.ANY),
                  pl.BlockSpec((None, 1, d, hid), lambda e, *_: (layer, e, 0, 0)),
                  pl.BlockSpec((None, 1, d, hid), lambda e, *_: (layer, e, 0, 0)),
                  pl.BlockSpec((None, 1, hid, d), lambda e, *_: (layer, e, 0, 0))],
        out_specs=pl.BlockSpec(memory_space=pl.ANY),
        scratch_shapes=[pltpu.VMEM((2, EXPERT_CHUNK, d), F32), pltpu.VMEM((2, EXPERT_CHUNK, d), F32),
                        pltpu.VMEM((d, hid), BF16), pltpu.VMEM((d, hid), BF16), pltpu.VMEM((hid, d), BF16),
                        pltpu.SemaphoreType.DMA((2,)), pltpu.SemaphoreType.DMA((2,))],
    )
    return pl.pallas_call(
        _expert_body,
        grid_spec=grid_spec,
        out_shape=jax.ShapeDtypeStruct((r, d), F32),
        compiler_params=_cparams(1),
        name="moe_experts",
    )(first_blk, nblk, nused, x_rows, w1, w3, w2)


def _combine_body(dest_ref, dest_next_ref, gate_ref, h_ref, ln_ref, rows_ref, o_ref, buf_ref, sem):
    i = pl.program_id(0)
    last = pl.num_programs(0) - 1
    tm = h_ref.shape[0]

    def gather(dref, slot):
        return [_row_copy(rows_ref, dref[0, 0, 2 * r + s], buf_ref.at[slot, s], r, sem.at[slot])
                for r in range(tm) for s in range(2)]

    @pl.when(i == 0)
    def _():
        _start_all(gather(dest_ref, 0))

    _start_all(gather(dest_next_ref, (i + 1) % 2))
    slot = i % 2
    for cp in gather(dest_ref, slot):
        cp.wait()
    gate = gate_ref[...]
    y = gate[:, 4:5] * buf_ref[slot, 0] + gate[:, 5:6] * buf_ref[slot, 1]
    o_ref[...] = _layer_norm(DN_ALPHA * h_ref[...] + y, ln_ref[0:1, :], ln_ref[1:2, :])

    @pl.when(i == last)
    def _():
        for cp in gather(dest_next_ref, (i + 1) % 2):
            cp.wait()


def _combine(dest3, route, h, ln, y_rows, tm):
    t, d = h.shape
    nt = t // tm
    return pl.pallas_call(
        _combine_body,
        grid=(nt,),
        in_specs=[pl.BlockSpec((1, 1, 2 * tm), lambda i: (i, 0, 0), memory_space=pltpu.SMEM),
                  pl.BlockSpec((1, 1, 2 * tm), lambda i: (jnp.minimum(i + 1, nt - 1), 0, 0),
                               memory_space=pltpu.SMEM),
                  pl.BlockSpec((tm, 8), lambda i: (i, 0)),
                  pl.BlockSpec((tm, d), lambda i: (i, 0)),
                  pl.BlockSpec((8, d), lambda i: (0, 0)),
                  pl.BlockSpec(memory_space=pl.ANY)],
        out_specs=pl.BlockSpec((tm, d), lambda i: (i, 0)),
        out_shape=jax.ShapeDtypeStruct((t, d), F32),
        scratch_shapes=[pltpu.VMEM((2, 2, tm, d), F32), pltpu.SemaphoreType.DMA((2,))],
        compiler_params=_cparams(1),
        name="moe_combine_ln",
    )(dest3, dest3, route, h, ln, y_rows)


def _moe(h, route, counts, w1, w3, w2, layer, ln):
    t, d = h.shape
    tm = min(COMBINE_TILE, t)
    a = 2 * t
    expert = route[:, 0:2].astype(jnp.int32)
    rank = route[:, 2:4].astype(jnp.int32)
    cnt = counts[0, :N_EXPERTS].astype(jnp.int32)
    padded = (cnt + MOE_BLOCK - 1) // MOE_BLOCK * MOE_BLOCK
    pend = jnp.cumsum(padded)
    pstart = pend - padded
    sel = expert[..., None] == jnp.arange(N_EXPERTS, dtype=jnp.int32)
    dest = jnp.sum(jnp.where(sel, pstart, 0), axis=-1) + rank
    n_blocks = -(-a // MOE_BLOCK) + N_EXPERTS
    nused = (pend[-1:] // MOE_BLOCK).astype(jnp.int32)
    dest3 = dest.reshape(t // tm, 1, 2 * tm)
    x_rows = _dispatch((pstart + cnt).astype(jnp.int32), (padded - cnt).astype(jnp.int32),
                       pend[-1:].astype(jnp.int32), dest3, h, n_blocks * MOE_BLOCK)
    y_rows = _experts((pstart // MOE_BLOCK).astype(jnp.int32), (padded // MOE_BLOCK).astype(jnp.int32),
                      nused, x_rows, w1, w3, w2, layer)
    return _combine(dest3, route, h, ln, y_rows, tm)


def _pad_rows(x, rows):
    return jnp.zeros((rows,) + x.shape[1:], x.dtype).at[:x.shape[0]].set(x)


def _route_weights(wg, bg, we, be):
    d = wg.shape[0]
    wr = jnp.zeros((d, LANES), F32)
    wr = wr.at[:, :MOE_GROUPS].set(wg)
    wr = wr.at[:, MOE_GROUPS:MOE_GROUPS + N_EXPERTS].set(jnp.transpose(we, (1, 0, 2)).reshape(d, N_EXPERTS))
    br = jnp.zeros((8, LANES), F32)
    br = br.at[0, :MOE_GROUPS].set(bg)
    br = br.at[0, MOE_GROUPS:MOE_GROUPS + N_EXPERTS].set(be.reshape(N_EXPERTS))
    return wr, br


def kernel(x, ab_w_in, rw_mu, rw_w0, rw_w2, rw_a0, rw_a2, rw_g2, rw_k_k, rw_k_a, rw_r_k, rw_gn_g, rw_gn_b, gla_gk_w2, gla_gk_b, gla_norm_g, ab_w_out, cd_w_in, s5_a_re, s5_a_im, s5_log_dt, s5_b_re, s5_b_im, s5_c_re, s5_c_im, s5_d, s5_glu_w, s5_glu_b, hg_lb, hg_norm_g, cd_w_out, ln1_g, ln1_b, moe_wg, moe_bg, moe_we, moe_be, moe_w1, moe_w3, moe_w2, ln2_g, ln2_b):
    batch, seq, d = x.shape
    t = batch * seq
    assert d == D_MODEL and seq % CHUNK == 0
    rows_per_seq = seq // S5_CHUNK
    assert rows_per_seq & (rows_per_seq - 1) == 0, "S5 chunk scan assumes a power-of-two chunk count"
    tm = min(ROW_TILE, t)
    assert t % tm == 0
    xt = x.reshape(t, d)
    ln1 = [_pad_rows(jnp.stack([ln1_g[l], ln1_b[l]]), 8) for l in range(DEPTH)]
    ln2 = [_pad_rows(jnp.stack([ln2_g[l], ln2_b[l]]), 8) for l in range(DEPTH)]

    j = 0
    w = ab_w_in[j]
    r_, wl_, k_, v_, al_, gl_ = 0, 512, 576, 1088, 1600, 1664
    gq, gk, gv, glow, ggate = 1792, 2048, 2304, 2816, 2832
    w0cols = jnp.concatenate([
        w[:, r_:r_ + 512], w[:, k_:k_ + 512], w[:, v_:v_ + 512], w[:, gv:gv + 512], w[:, ggate:ggate + 512],
        w[:, wl_:wl_ + 64], w[:, al_:al_ + 64], w[:, gl_:gl_ + 128], w[:, gq:gq + 256], w[:, gk:gk + 256],
        w[:, glow:glow + 16], jnp.zeros((d, 112), F32)], axis=1).astype(BF16)
    p0 = _matmul(xt, w0cols, tm)

    mu = rw_mu[j]
    vec = _pad_rows(jnp.stack([mu[r_:r_ + 512], mu[k_:k_ + 512], mu[v_:v_ + 512], rw_w0[j], rw_a0[j],
                               rw_k_k[j], rw_k_a[j], rw_r_k[j].reshape(-1), rw_gn_g[j], rw_gn_b[j]]), 16)
    mulo = _pad_rows(jnp.concatenate([mu[wl_:wl_ + 64], mu[al_:al_ + 64], mu[gl_:gl_ + 128]])[None], 8)
    w2p = _pad_rows(rw_w2[j], 128)
    a2p = jnp.zeros((128, 512), F32).at[64:].set(rw_a2[j])
    p0 = p0.reshape(batch, seq, -1)
    y_rw = _rwkv(p0, vec, mulo, w2p, a2p, rw_g2[j]).reshape(t, RW_WIDTH)
    y_gla = _gla(p0, _pad_rows(gla_gk_w2[j], 128), _pad_rows(gla_gk_b[j][None], 8),
                 _pad_rows(gla_norm_g[j][None], 8)).reshape(t, 512)

    wr, br = _route_weights(moe_wg[0], moe_bg[0], moe_we[0], moe_be[0])
    h, route, counts = _proj_route(y_rw, y_gla, xt, ab_w_out[j].astype(BF16), ln1[0], wr, br, tm)
    h = _moe(h, route, counts, moe_w1, moe_w3, moe_w2, 0, ln2[0])

    p1 = _matmul(h, cd_w_in[j].astype(BF16), tm)
    lb_sm = jax.nn.softmax(hg_lb.astype(F32), axis=0)
    lower = (jnp.cumsum(lb_sm, axis=0) - lb_sm[0])[1]
    y_hg = _hgrn(p1.reshape(batch, seq, -1), _pad_rows(lower[None], 8),
                 _pad_rows(hg_norm_g[j][None], 8)).reshape(t, 512)

    toep, bmat, emat, pw = _s5_tables(s5_a_re[j], s5_a_im[j], s5_log_dt[j], s5_b_re[j], s5_b_im[j],
                                      s5_c_re[j], s5_c_im[j], rows_per_seq)
    y_ssm = _s5_scan(p1.reshape(batch, seq, -1), toep, bmat, emat, pw).reshape(t, 512)
    y_s5 = _s5_post(y_ssm, p1, _pad_rows(jnp.stack([s5_d[j], s5_glu_b[j]]), 8),
                    s5_glu_w[j].astype(BF16), tm)

    wr, br = _route_weights(moe_wg[1], moe_bg[1], moe_we[1], moe_be[1])
    h2, route, counts = _proj_route(y_s5, y_hg, h, cd_w_out[j].astype(BF16), ln1[1], wr, br, tm)
    out = _moe(h2, route, counts, moe_w1, moe_w3, moe_w2, 1, ln2[1])
    return out.reshape(batch, seq, d)
```

```python
import functools
import math

import numpy as np
import jax
import jax.numpy as jnp
from jax import lax
from jax.experimental import pallas as pl
from jax.experimental.pallas import tpu as pltpu

F32 = jnp.float32
BF16 = jnp.bfloat16

D_MODEL = 1024
DEPTH = 2
RW_HEAD = 64
RW_WIDTH = 512
RW_GN_EPS = 64e-5
GLA_HEADS = 4
GLA_DK = 64
GLA_DV = 128
GLA_GATE_TAU = 16.0
S5_GROUP = 16
S5_GROUPS = 32
S5_STATE = 64
HG_HEADS = 4
HG_DK = 128
CHUNK = 64
S5_CHUNK = 16
S5_SCAN_ROWS = 32
S5_TILE_GROUPS = 8
NORM_EPS = 1e-5
MOE_GROUPS = 4
MOE_PER_GROUP = 8
N_EXPERTS = 32
EXPERT_HIDDEN = 512
MOE_BLOCK = 128
COMBINE_TILE = 128
EXPERT_CHUNK = 512
ROW_TILE = 512
DN_ALPHA = (2.0 * DEPTH) ** 0.25

LANES = 128
VMEM_LIMIT = 56 * 1024 * 1024


def _cparams(n_axes=1):
    return pltpu.CompilerParams(dimension_semantics=("arbitrary",) * n_axes,
                                vmem_limit_bytes=VMEM_LIMIT)


def _d(a, b):
    return jnp.dot(a, b, preferred_element_type=F32)


def _d_nt(a, b):
    return lax.dot_general(a, b, (((1,), (1,)), ((), ())), preferred_element_type=F32)


def _d_tn(a, b):
    return lax.dot_general(a, b, (((0,), (0,)), ((), ())), preferred_element_type=F32)


def _split(a):
    hi = a.astype(BF16)
    lo = (a - hi.astype(F32)).astype(BF16)
    return hi, lo


def _split3(a):
    hi = a.astype(BF16)
    r1 = a - hi.astype(F32)
    mid = r1.astype(BF16)
    lo = (r1 - mid.astype(F32)).astype(BF16)
    return hi, mid, lo


def _bdot(a, b):
    return _d(a.astype(BF16), b.astype(BF16))


def _bdot_nt(a, b):
    return _d_nt(a.astype(BF16), b.astype(BF16))


def _bdot_tn(a, b):
    return _d_tn(a.astype(BF16), b.astype(BF16))


def _hdot_with(d, a, b):
    ah, al = _split(a)
    bh, bl = _split(b)
    return d(ah, bh) + (d(ah, bl) + d(al, bh))


def _hdot(a, b):
    return _hdot_with(_d, a, b)


def _hdot_nt(a, b):
    return _hdot_with(_d_nt, a, b)


def _hdot_tn(a, b):
    return _hdot_with(_d_tn, a, b)


def _xdot_l(a, e):
    ah, am, al = _split3(a)
    return _d(ah, e) + (_d(am, e) + _d(al, e))


def _xdot_r(e, a):
    ah, am, al = _split3(a)
    return _d(e, ah) + (_d(e, am) + _d(e, al))


def _iota(shape, dim):
    return lax.broadcasted_iota(jnp.int32, shape, dim)


def _softplus(x):
    return jnp.maximum(x, 0.0) + jnp.log1p(jnp.exp(-jnp.abs(x)))


def _sigmoid(x):
    return 1.0 / (1.0 + jnp.exp(-x))


def _silu(x):
    return x * _sigmoid(x)


def _tril_incl(n):
    return jnp.where(_iota((n, n), 0) >= _iota((n, n), 1), 1.0, 0.0).astype(BF16)


def _cumsum_rows(g):
    return _xdot_r(_tril_incl(g.shape[0]), g)


def _shift_mix(x, prev_ref, b, mu):
    c = x.shape[0]
    rolled = pltpu.roll(x, 1, 0)
    prev = jnp.where(_iota(x.shape, 0) == 0, jnp.broadcast_to(prev_ref[b, 0:1, :], x.shape), rolled)
    prev_ref[b, 0:1, :] = x[c - 1:c, :]
    return x + mu * (prev - x)


def _layer_norm(x, g, b):
    mu = jnp.mean(x, axis=-1, keepdims=True)
    xc = x - mu
    var = jnp.mean(xc * xc, axis=-1, keepdims=True)
    return xc * lax.rsqrt(var + NORM_EPS) * g + b


def _mm_body(x_ref, w_ref, o_ref):
    o_ref[...] = _d(x_ref[...].astype(BF16), w_ref[...])


def _matmul(x, w_bf16, tm):
    m, k = x.shape
    n = w_bf16.shape[1]
    return pl.pallas_call(
        _mm_body,
        grid=(m // tm,),
        in_specs=[pl.BlockSpec((tm, k), lambda i: (i, 0)),
                  pl.BlockSpec((k, n), lambda i: (0, 0))],
        out_specs=pl.BlockSpec((tm, n), lambda i: (i, 0)),
        out_shape=jax.ShapeDtypeStruct((m, n), F32),
        compiler_params=_cparams(1),
        name="in_proj",
    )(x, w_bf16)


_RV_MU_R, _RV_MU_K, _RV_MU_V, _RV_W0, _RV_A0, _RV_KK, _RV_KA, _RV_RK, _RV_GNG, _RV_GNB = range(10)


def _rwkv_body(r_ref, k_ref, v_ref, lo_ref, vec_ref, mulo_ref, w2_ref, a2_ref, g2_ref, ones_ref,
               o_ref, pr_ref, pk_ref, pv_ref, plo_ref, st_ref):
    nb, c = r_ref.shape[0], r_ref.shape[1]
    npair = RW_WIDTH // LANES

    @pl.when(pl.program_id(0) == 0)
    def _():
        pr_ref[...] = jnp.zeros_like(pr_ref)
        pk_ref[...] = jnp.zeros_like(pk_ref)
        pv_ref[...] = jnp.zeros_like(pv_ref)
        plo_ref[...] = jnp.zeros_like(plo_ref)
        st_ref[...] = jnp.zeros_like(st_ref)

    def vec(i):
        return vec_ref[i:i + 1, :]

    ones_bd = ones_ref[...]
    lane = _iota((c, LANES), 1)
    m1 = lane < RW_HEAD
    row2 = _iota((2 * c, 4 * c), 0)
    col2 = _iota((2 * c, 4 * c), 1) & (c - 1)
    tri = ((row2 < c) & (row2 > col2)) | ((row2 >= c) & ((row2 - c) >= col2))
    eye2 = jnp.where(_iota((2 * c, 2 * c), 0) == _iota((2 * c, 2 * c), 1), 1.0, 0.0)
    bd_p = (_iota((LANES, LANES), 0) >> 6) == (_iota((LANES, LANES), 1) >> 6)

    def halves(x):
        return jnp.concatenate([jnp.where(m1, x, 0.0), jnp.where(m1, 0.0, x)], axis=0)

    prep = []
    for b in range(nb):
        xr = _shift_mix(r_ref[b], pr_ref, b, vec(_RV_MU_R))
        xk = _shift_mix(k_ref[b], pk_ref, b, vec(_RV_MU_K))
        xv = _shift_mix(v_ref[b], pv_ref, b, vec(_RV_MU_V))
        xlo = _shift_mix(lo_ref[b], plo_ref, b, mulo_ref[0:1, :])
        lo_a = xlo[:, :LANES]
        lo_g = xlo[:, LANES:]
        w = -_softplus(-(vec(_RV_W0) + _bdot(jnp.tanh(lo_a), w2_ref[...]))) - 0.5
        g = -jnp.exp(w)
        a = _sigmoid(vec(_RV_A0) + _bdot(lo_a, a2_ref[...]))
        gate = _bdot(_sigmoid(lo_g), g2_ref[...])
        kk = xk * vec(_RV_KK)
        kk = kk / jnp.maximum(jnp.sqrt(_xdot_l(kk * kk, ones_bd)), 1e-12)
        k2 = xk * (1.0 + (a - 1.0) * vec(_RV_KA))
        gc = _cumsum_rows(g)
        g_last = gc[c - 1:c, :]
        e_neg = jnp.exp(-gc)
        e_end = jnp.exp(g_last - gc)
        prep.append(dict(
            xr=xr, xv=xv, k2=k2, gate=gate,
            at=-kk * jnp.exp(gc - g),
            bt=(kk * a) * e_neg, kt=k2 * e_neg, rt=xr * jnp.exp(gc),
            bh=(kk * a) * e_end, kh=k2 * e_end,
            gam=jnp.exp(g_last)))

    chains = [(b, p) for b in range(nb) for p in range(npair)]

    def part(ch, name):
        b, p = ch
        return prep[b][name][:, p * LANES:(p + 1) * LANES]

    a_ak, a_row, vv, ak, pinv = {}, {}, {}, {}, {}
    for ch in chains:
        lhs = jnp.concatenate([part(ch, 'at'), part(ch, 'rt')], axis=0)
        rhs = jnp.concatenate([halves(part(ch, 'bt')), halves(part(ch, 'kt'))], axis=0)
        aa = jnp.where(tri, _bdot_nt(lhs, rhs), 0.0)
        a_ak[ch] = aa[:c, 2 * c:].astype(BF16)
        a_row[ch] = aa[c:, :].astype(BF16)
        abd = halves(aa[:c, :2 * c])
        pinv[ch] = eye2 + abd
        ak[ch] = abd
        vv[ch] = halves(part(ch, 'xv')).astype(BF16)
    nlev = int(math.log2(c))
    for lev in range(nlev):
        for ch in chains:
            akb = ak[ch].astype(BF16)
            if lev == 0:
                ak[ch] = _d(akb, akb)
            elif lev < nlev - 1:
                out = _d(akb, jnp.concatenate([akb, pinv[ch].astype(BF16)], axis=1))
                ak[ch] = out[:, :2 * c]
                pinv[ch] = pinv[ch] + out[:, 2 * c:]
            else:
                pinv[ch] = pinv[ch] + _d(akb, pinv[ch].astype(BF16))
    x2 = {ch: _d(a_ak[ch], vv[ch]) for ch in chains}

    sts = {ch: st_ref[ch[0] * npair + ch[1]] for ch in chains}
    xs = {ch: _d_nt(jnp.concatenate([part(ch, 'at'), part(ch, 'rt')], axis=0).astype(BF16),
                    sts[ch].astype(BF16)) for ch in chains}
    us = {}
    for ch in chains:
        u2 = _d(pinv[ch].astype(BF16), halves(xs[ch][:c] + x2[ch]).astype(BF16))
        us[ch] = u2[:c] + u2[c:]
    ys = {}
    for ch in chains:
        b, p = ch
        u = us[ch]
        ys[ch] = xs[ch][c:] + _d(a_row[ch], jnp.concatenate([halves(u).astype(BF16), vv[ch]], axis=0))
        upd = _d_tn(jnp.concatenate([u, part(ch, 'xv')], axis=0).astype(BF16),
                    jnp.concatenate([part(ch, 'bh'), part(ch, 'kh')], axis=0).astype(BF16))
        st_ref[b * npair + p] = sts[ch] * part(ch, 'gam') + jnp.where(bd_p, upd, 0.0)

    inv_n = 1.0 / RW_HEAD
    for b in range(nb):
        y = jnp.concatenate([ys[(b, p)] for p in range(npair)], axis=1)
        mean = _xdot_l(y, ones_bd) * inv_n
        yc = y - mean
        var = _xdot_l(yc * yc, ones_bd) * inv_n
        yn = yc * lax.rsqrt(var + RW_GN_EPS) * vec(_RV_GNG) + vec(_RV_GNB)
        bonus = _xdot_l(prep[b]['xr'] * prep[b]['k2'] * vec(_RV_RK), ones_bd) * prep[b]['xv']
        o_ref[b] = (yn + bonus) * prep[b]['gate']


def _rwkv(p0, vec, mulo, w2p, a2p, g2):
    batch, seq, _ = p0.shape
    nc = seq // CHUNK
    c = CHUNK
    ones_bd = jnp.asarray(np.kron(np.eye(RW_WIDTH // RW_HEAD), np.ones((RW_HEAD, RW_HEAD))), BF16)

    def col(j, width):
        return pl.BlockSpec((batch, c, width), lambda i: (0, i, j))

    def full(shape):
        return pl.BlockSpec(shape, lambda i: (0,) * len(shape))

    return pl.pallas_call(
        _rwkv_body,
        grid=(nc,),
        in_specs=[col(0, 512), col(1, 512), col(2, 512), col(10, 256),
                  full((16, 512)), full((8, 256)), full((128, 512)), full((128, 512)),
                  full((128, 512)), full((512, 512))],
        out_specs=pl.BlockSpec((batch, c, 512), lambda i: (0, i, 0)),
        out_shape=jax.ShapeDtypeStruct((batch, seq, RW_WIDTH), F32),
        scratch_shapes=[pltpu.VMEM((batch, 8, 512), F32), pltpu.VMEM((batch, 8, 512), F32),
                        pltpu.VMEM((batch, 8, 512), F32), pltpu.VMEM((batch, 8, 256), F32),
                        pltpu.VMEM((batch * RW_WIDTH // LANES, LANES, LANES), F32)],
        compiler_params=_cparams(1),
        name="rwkv7",
    )(p0, p0, p0, p0, vec, mulo, w2p, a2p, g2, ones_bd)


def _gla_core(qs, ks, vs, gs, st_ref, heads_per_block):
    nb = len(qs)
    c = qs[0].shape[0]
    hpb = heads_per_block
    nblk = qs[0].shape[1] // LANES
    dk_shift = int(math.log2(LANES // hpb))
    row = _iota(qs[0].shape, 0)

    prep = []
    for q, k, g in zip(qs, ks, gs):
        b = _cumsum_rows(g)

        def brow(i, b=b):
            return jnp.broadcast_to(b[i:i + 1, :], b.shape)

        b15, b31, b47, blast = brow(15), brow(31), brow(47), brow(c - 1)
        ref_b = jnp.where(row < 32, b15, b47)
        ref_d = jnp.where(row < 16, 0.0, jnp.where(row < 32, b15, jnp.where(row < 48, b31, b47)))
        prep.append(dict(
            q_a=q * jnp.exp(jnp.minimum(b - b31, 0.0)), k_a=k * jnp.exp(jnp.minimum(b31 - b, 0.0)),
            q_b=q * jnp.exp(jnp.minimum(b - ref_b, 0.0)), k_b=k * jnp.exp(jnp.minimum(ref_b - b, 0.0)),
            q_d=q * jnp.exp(b - ref_d), k_d=k * jnp.exp(ref_d - b),
            q_i=q * jnp.exp(b), k_s=k * jnp.exp(blast - b), gam=jnp.exp(b[c - 1:c, :])))

    ri = _iota((hpb * c, c), 0) & (c - 1)
    ci = _iota((hpb * c, c), 1)
    mask_a = (ri >= 32) & (ci < 32)
    mask_b = ((ri >> 5) == (ci >> 5)) & (((ri >> 4) & 1) == 1) & (((ci >> 4) & 1) == 0)
    mask_d = ((ri >> 4) == (ci >> 4)) & (ri >= ci)
    lane = _iota((c, LANES), 1)
    bd = (_iota((hpb * LANES, LANES), 0) >> 7) == (_iota((hpb * LANES, LANES), 1) >> dk_shift)

    def heads_rows(x):
        if hpb == 1:
            return x
        return jnp.concatenate([jnp.where((lane >> dk_shift) == h, x, 0.0) for h in range(hpb)], axis=0)

    chains = [(bi, blk) for bi in range(nb) for blk in range(nblk)]

    def part(ch, name):
        bi, blk = ch
        return prep[bi][name][:, blk * LANES:(blk + 1) * LANES]

    v_p = {ch: vs[ch[0]][:, ch[1] * hpb * LANES:(ch[1] + 1) * hpb * LANES].astype(BF16) for ch in chains}
    probs = {}
    for ch in chains:
        s_a = _bdot_nt(heads_rows(part(ch, 'q_a')), part(ch, 'k_a'))
        s_b = _bdot_nt(heads_rows(part(ch, 'q_b')), part(ch, 'k_b'))
        s_d = _bdot_nt(heads_rows(part(ch, 'q_d')), part(ch, 'k_d'))
        probs[ch] = (jnp.where(mask_a, s_a, 0.0) + jnp.where(mask_b, s_b, 0.0)
                     + jnp.where(mask_d, s_d, 0.0)).astype(BF16)
    outs = {}
    for ch in chains:
        pv = _d(probs[ch], v_p[ch])
        o = pv[:c]
        for h in range(1, hpb):
            o = jnp.where((_iota(o.shape, 1) >> 7) == h, pv[h * c:(h + 1) * c], o)
        si = ch[0] * nblk + ch[1]
        sp = st_ref[si]
        outs[ch] = o + _bdot_nt(part(ch, 'q_i'), sp)
        upd = _d_tn(v_p[ch], part(ch, 'k_s').astype(BF16))
        st_ref[si] = sp * part(ch, 'gam') + jnp.where(bd, upd, 0.0)
    return [jnp.concatenate([outs[(bi, blk)] for blk in range(nblk)], axis=1) for bi in range(nb)]


def _gated_rmsnorm(o, gate, norm_g):
    nh = o.shape[1] // LANES
    outs = []
    for h in range(nh):
        sl = slice(h * LANES, (h + 1) * LANES)
        oh = o[:, sl]
        ms = jnp.mean(oh * oh, axis=-1, keepdims=True)
        outs.append(oh * lax.rsqrt(ms + NORM_EPS) * norm_g * _silu(gate[:, sl]))
    return jnp.concatenate(outs, axis=1)


def _gla_body(q_ref, k_ref, v_ref, gate_ref, gk_ref, w2_ref, vec_ref, ng_ref, o_ref, st_ref):
    nb = q_ref.shape[0]

    @pl.when(pl.program_id(0) == 0)
    def _():
        st_ref[...] = jnp.zeros_like(st_ref)

    gs = [-_softplus(-(_hdot(gk_ref[b], w2_ref[...]) + vec_ref[0:1, :])) * (1.0 / GLA_GATE_TAU)
          for b in range(nb)]
    qs = [q_ref[b] * (GLA_DK ** -0.5) for b in range(nb)]
    os_ = _gla_core(qs, [k_ref[b] for b in range(nb)], [v_ref[b] for b in range(nb)], gs, st_ref, 2)
    for b in range(nb):
        o_ref[b] = _gated_rmsnorm(os_[b], gate_ref[b], ng_ref[0:1, :])


def _seq_specs(batch, c):
    def col(j, width):
        return pl.BlockSpec((batch, c, width), lambda i: (0, i, j))

    def full(shape):
        return pl.BlockSpec(shape, lambda i: (0,) * len(shape))

    return col, full


def _gla(p0, gk_w2p, gk_b, norm_g):
    batch, seq, _ = p0.shape
    c = CHUNK
    col, full = _seq_specs(batch, c)
    return pl.pallas_call(
        _gla_body,
        grid=(seq // c,),
        in_specs=[col(11, 256), col(12, 256), col(3, 512), col(4, 512), col(26, 128),
                  full((128, 256)), full((8, 256)), full((8, 128))],
        out_specs=pl.BlockSpec((batch, c, 512), lambda i: (0, i, 0)),
        out_shape=jax.ShapeDtypeStruct((batch, seq, 512), F32),
        scratch_shapes=[pltpu.VMEM((batch * 2, 2 * LANES, LANES), F32)],
        compiler_params=_cparams(1),
        name="gla",
    )(p0, p0, p0, p0, p0, gk_w2p, gk_b, norm_g)


def _hgrn_body(q_ref, f_ref, i_ref, gate_ref, lb_ref, ng_ref, o_ref, st_ref):
    nb = q_ref.shape[0]

    @pl.when(pl.program_id(0) == 0)
    def _():
        st_ref[...] = jnp.zeros_like(st_ref)

    lb = lb_ref[0:1, :]
    qs, ks, gs = [], [], []
    for b in range(nb):
        f = f_ref[b]
        gs.append(jnp.log(lb + (1.0 - lb) * _sigmoid(f)))
        ks.append((1.0 - lb) * _sigmoid(-f))
        qs.append(_silu(q_ref[b]))
    os_ = _gla_core(qs, ks, [i_ref[b] for b in range(nb)], gs, st_ref, 1)
    for b in range(nb):
        o_ref[b] = _gated_rmsnorm(os_[b], gate_ref[b], ng_ref[0:1, :])


def _hgrn(p1, lb, norm_g):
    batch, seq, _ = p1.shape
    c = CHUNK
    col, full = _seq_specs(batch, c)
    return pl.pallas_call(
        _hgrn_body,
        grid=(seq // c,),
        in_specs=[col(1, 512), col(2, 512), col(3, 512), col(4, 512), full((8, 512)), full((8, 128))],
        out_specs=pl.BlockSpec((batch, c, 512), lambda i: (0, i, 0)),
        out_shape=jax.ShapeDtypeStruct((batch, seq, 512), F32),
        scratch_shapes=[pltpu.VMEM((batch * HG_HEADS, LANES, LANES), F32)],
        compiler_params=_cparams(1),
        name="hgrn2",
    )(p1, p1, p1, p1, lb, norm_g)


def _s5_body(u_ref, dcat_ref, bm_ref, e_ref, pw_ref, o_ref, toep_ref):
    seq = u_ref.shape[0]
    rows = seq // S5_CHUNK

    @pl.when(pl.program_id(1) == 0)
    def _():
        for tk in range(S5_CHUNK):
            lo = (S5_CHUNK - 1 - tk) * LANES
            toep_ref[tk * LANES:(tk + 1) * LANES, :] = dcat_ref[0, :, lo:lo + S5_CHUNK * LANES]

    u = jnp.concatenate([u_ref[pl.ds(tk, rows, stride=S5_CHUNK), :] for tk in range(S5_CHUNK)],
                        axis=1).astype(BF16)
    s = _d(u, bm_ref[0])
    rin = _iota(s.shape, 0)
    pw = pw_ref[0]

    def swap_halves(x):
        return jnp.concatenate([pltpu.roll(x[:, k * LANES:(k + 1) * LANES], S5_STATE, 1)
                                for k in range(x.shape[1] // LANES)], axis=1)

    h = s
    for lev in range(int(math.log2(rows))):
        sh = 1 << lev
        hs = jnp.where(rin >= sh, pltpu.roll(h, sh, 0), 0.0)
        h = h + pw[2 * lev:2 * lev + 1, :] * hs + pw[2 * lev + 1:2 * lev + 2, :] * swap_halves(hs)
    hprev = jnp.where(rin >= 1, pltpu.roll(h, 1, 0), 0.0)
    y = _d(u, toep_ref[...]) + _bdot(hprev, e_ref[0])
    for tk in range(S5_CHUNK):
        o_ref[pl.ds(tk, rows, stride=S5_CHUNK), :] = y[:, tk * LANES:(tk + 1) * LANES]


def _s5_scan(p1, dcat, bmat, emat, pw):
    batch, seq, _ = p1.shape
    nt, width, nstate = bmat.shape

    def table(shape):
        return pl.BlockSpec((1,) + shape, lambda j, b: (j, 0, 0), pipeline_mode=pl.Buffered(1))

    return pl.pallas_call(
        _s5_body,
        grid=(nt, batch),
        in_specs=[pl.BlockSpec((None, seq, LANES), lambda j, b: (b, 0, j)),
                  table(dcat.shape[1:]), table((width, nstate)), table((nstate, width)),
                  table((S5_SCAN_ROWS, nstate))],
        out_specs=pl.BlockSpec((None, seq, LANES), lambda j, b: (b, 0, j)),
        out_shape=jax.ShapeDtypeStruct((batch, seq, nt * LANES), F32),
        scratch_shapes=[pltpu.VMEM((width, width), BF16)],
        compiler_params=_cparams(2),
        name="s5_scan",
    )(p1, dcat, bmat, emat, pw)


def _s5_post_body(y_ref, u_ref, vec_ref, w_ref, o_ref):
    y = y_ref[...] + vec_ref[0:1, :] * u_ref[...]
    y = 0.5 * y * (1.0 + jnp.tanh(math.sqrt(2.0 / math.pi) * (y + 0.044715 * (y * y * y))))
    o_ref[...] = y * _sigmoid(_bdot(y, w_ref[...]) + vec_ref[1:2, :])


def _s5_post(y_ssm, p1, vec, glu_w, tm):
    t = y_ssm.shape[0]
    return pl.pallas_call(
        _s5_post_body,
        grid=(t // tm,),
        in_specs=[pl.BlockSpec((tm, 512), lambda i: (i, 0)),
                  pl.BlockSpec((tm, 512), lambda i: (i, 0)),
                  pl.BlockSpec((8, 512), lambda i: (0, 0)),
                  pl.BlockSpec((512, 512), lambda i: (0, 0))],
        out_specs=pl.BlockSpec((tm, 512), lambda i: (i, 0)),
        out_shape=jax.ShapeDtypeStruct((t, 512), F32),
        compiler_params=_cparams(1),
        name="s5_post",
    )(y_ssm, p1, vec, glu_w)


def _s5_tables(a_re, a_im, log_dt, b_re, b_im, c_re, c_im, rows_per_seq):
    c = S5_CHUNK
    lam_re = jnp.minimum(a_re, -1e-4)
    lam_im = a_im
    dt = jnp.exp(log_dt)[:, None]
    mag = jnp.exp(lam_re * dt)
    abar_re = mag * jnp.cos(lam_im * dt)
    abar_im = mag * jnp.sin(lam_im * dt)
    den = lam_re * lam_re + lam_im * lam_im
    num_re = abar_re - 1.0
    z_re = (num_re * lam_re + abar_im * lam_im) / den
    z_im = (abar_im * lam_re - num_re * lam_im) / den

    def power(n):
        n = jnp.asarray(n, F32)[..., None, None]
        m = jnp.exp(n * (lam_re * dt))
        return m * jnp.cos(n * (lam_im * dt)), m * jnp.sin(n * (lam_im * dt))

    def cmul(ar, ai, br, bi):
        return ar * br - ai * bi, ar * bi + ai * br

    tau = jnp.arange(c)
    p_re, p_im = power(tau)
    zb_re, zb_im = cmul(z_re[..., None], z_im[..., None], b_re, b_im)
    cp_re, cp_im = cmul(c_re[None], c_im[None], p_re[:, :, None, :], p_im[:, :, None, :])
    hi = lax.Precision.HIGHEST
    kern = (jnp.einsum('tgon,gni->gtoi', cp_re, zb_re, precision=hi)
            - jnp.einsum('tgon,gni->gtoi', cp_im, zb_im, precision=hi))
    q_re, q_im = power(c - 1 - tau)
    bm_re, bm_im = cmul(q_re[..., None], q_im[..., None], zb_re[None], zb_im[None])
    bmat = jnp.concatenate([jnp.transpose(bm_re, (1, 0, 3, 2)), jnp.transpose(bm_im, (1, 0, 3, 2))],
                           axis=-1).reshape(S5_GROUPS, c * S5_GROUP, 2 * S5_STATE)
    r_re, r_im = power(tau + 1)
    e_re, e_im = cmul(c_re[None], c_im[None], r_re[:, :, None, :], r_im[:, :, None, :])
    emat = jnp.concatenate([jnp.transpose(e_re, (1, 3, 0, 2)), -jnp.transpose(e_im, (1, 3, 0, 2))],
                           axis=1).reshape(S5_GROUPS, 2 * S5_STATE, c * S5_GROUP)
    nlev = int(math.log2(rows_per_seq))
    assert 2 * nlev <= S5_SCAN_ROWS
    s_re, s_im = power(c * (2 ** jnp.arange(nlev)))
    pw = jnp.zeros((S5_GROUPS, S5_SCAN_ROWS, 2 * S5_STATE), F32)
    pw = pw.at[:, 0:2 * nlev:2, :].set(jnp.transpose(jnp.concatenate([s_re, s_re], -1), (1, 0, 2)))
    pw = pw.at[:, 1:2 * nlev:2, :].set(jnp.transpose(jnp.concatenate([-s_im, s_im], -1), (1, 0, 2)))
    nt, gl = S5_GROUPS // S5_TILE_GROUPS, S5_TILE_GROUPS
    eye = jnp.eye(gl, dtype=F32)
    spread = jnp.tile(jnp.eye(S5_GROUP, dtype=F32), (1, gl))
    lane_group = jnp.arange(LANES) // S5_GROUP
    kern5 = jnp.transpose(kern.reshape(nt, gl, c, S5_GROUP, S5_GROUP), (0, 2, 1, 4, 3))
    dblk = jnp.einsum('jtgio,op->jtgip', kern5, spread, precision=hi)
    dblk = jnp.where((jnp.arange(gl)[:, None, None] == lane_group[None, None, :]), dblk, 0.0)
    dblk = dblk.reshape(nt, c, LANES, LANES)
    dcat = jnp.transpose(dblk, (0, 2, 1, 3)).reshape(nt, LANES, c * LANES)
    dcat = jnp.pad(dcat, ((0, 0), (0, 0), ((c - 1) * LANES, 0)))
    bmat5 = jnp.transpose(bmat.reshape(nt, gl, c, S5_GROUP, 2 * S5_STATE), (0, 2, 1, 3, 4))
    bmat_t = (bmat5[:, :, :, :, None, :] * eye[None, None, :, None, :, None])
    bmat_t = bmat_t.reshape(nt, c * LANES, gl * 2 * S5_STATE)
    emat5 = jnp.transpose(emat.reshape(nt, gl, 2 * S5_STATE, c, S5_GROUP), (0, 3, 1, 2, 4))
    eblk = jnp.einsum('jtgno,op->jtgnp', emat5, spread, precision=hi)
    eblk = jnp.where((jnp.arange(gl)[:, None, None] == lane_group[None, None, :]), eblk, 0.0)
    emat_t = jnp.transpose(eblk.reshape(nt, c, gl * 2 * S5_STATE, LANES), (0, 2, 1, 3))
    emat_t = emat_t.reshape(nt, gl * 2 * S5_STATE, c * LANES)
    pw_t = jnp.transpose(pw.reshape(nt, gl, S5_SCAN_ROWS, 2 * S5_STATE), (0, 2, 1, 3))
    pw_t = pw_t.reshape(nt, S5_SCAN_ROWS, gl * 2 * S5_STATE)
    return dcat.astype(BF16), bmat_t.astype(BF16), emat_t.astype(BF16), pw_t


def _proj_route_body(ya_ref, yb_ref, x_ref, w_ref, ln_ref, wr_ref, br_ref,
                     h_ref, route_ref, cnt_ref, carry_ref):
    tm = ya_ref.shape[0]
    half = w_ref.shape[0] // 2

    @pl.when(pl.program_id(0) == 0)
    def _():
        carry_ref[...] = jnp.zeros_like(carry_ref)

    mix = _bdot(ya_ref[...], w_ref[:half, :]) + _bdot(yb_ref[...], w_ref[half:, :])
    h = _layer_norm(DN_ALPHA * x_ref[...] + mix, ln_ref[0:1, :], ln_ref[1:2, :])
    h_ref[...] = h

    logits = _hdot(h, wr_ref[...]) + br_ref[0:1, :]
    lane = _iota((tm, LANES), 1).astype(F32)
    neg = -jnp.inf

    def softmax_masked(mask):
        xm = jnp.where(mask, logits, neg)
        m = jnp.max(xm, axis=-1, keepdims=True)
        e = jnp.exp(xm - m)
        return e / jnp.sum(e, axis=-1, keepdims=True)

    def top1(pm):
        m = jnp.max(pm, axis=-1, keepdims=True)
        idx = jnp.min(jnp.where(pm == m, lane, float(LANES)), axis=-1, keepdims=True)
        return m, idx

    coarse = jnp.where(lane < MOE_GROUPS, softmax_masked(lane < MOE_GROUPS), -1.0)
    p_grp, grp = top1(coarse)
    lo = MOE_GROUPS + MOE_PER_GROUP * grp
    fmask = (lane >= lo) & (lane < lo + MOE_PER_GROUP)
    fine = jnp.where(fmask, softmax_masked(fmask), -1.0)
    p1, j1 = top1(fine)
    p2, j2 = top1(jnp.where(lane == j1, -1.0, fine))
    denom = p1 + p2
    g1 = p_grp * (p1 / denom)
    g2 = p_grp * (p2 / denom)
    e1 = j1 - MOE_GROUPS
    e2 = j2 - MOE_GROUPS

    oh1 = jnp.where(lane == e1, 1.0, 0.0)
    oh2 = jnp.where(lane == e2, 1.0, 0.0)
    cnt = oh1 + oh2
    strict = jnp.where(_iota((tm, tm), 0) > _iota((tm, tm), 1), 1.0, 0.0).astype(BF16)
    before = _d(strict, cnt.astype(BF16)) + carry_ref[0:1, :]
    r1 = jnp.sum(oh1 * before, axis=-1, keepdims=True)
    r2 = jnp.sum(oh2 * before, axis=-1, keepdims=True)
    carry_ref[0:1, :] = carry_ref[0:1, :] + jnp.sum(cnt, axis=0, keepdims=True)
    cnt_ref[...] = carry_ref[...]

    out = jnp.where(lane == 0, e1, 0.0)
    out = jnp.where(lane == 1, e2, out)
    out = jnp.where(lane == 2, r1, out)
    out = jnp.where(lane == 3, r2, out)
    out = jnp.where(lane == 4, g1, out)
    out = jnp.where(lane == 5, g2, out)
    route_ref[...] = out[:, :8]


def _proj_route(ya, yb, resid, w_out, ln, wr, br, tm):
    t, d = resid.shape
    return pl.pallas_call(
        _proj_route_body,
        grid=(t // tm,),
        in_specs=[pl.BlockSpec((tm, 512), lambda i: (i, 0)),
                  pl.BlockSpec((tm, 512), lambda i: (i, 0)),
                  pl.BlockSpec((tm, d), lambda i: (i, 0)),
                  pl.BlockSpec((d, d), lambda i: (0, 0)),
                  pl.BlockSpec((8, d), lambda i: (0, 0)),
                  pl.BlockSpec((d, LANES), lambda i: (0, 0)),
                  pl.BlockSpec((8, LANES), lambda i: (0, 0))],
        out_specs=[pl.BlockSpec((tm, d), lambda i: (i, 0)),
                   pl.BlockSpec((tm, 8), lambda i: (i, 0)),
                   pl.BlockSpec((8, LANES), lambda i: (0, 0))],
        out_shape=[jax.ShapeDtypeStruct((t, d), F32),
                   jax.ShapeDtypeStruct((t, 8), F32),
                   jax.ShapeDtypeStruct((8, LANES), F32)],
        scratch_shapes=[pltpu.VMEM((8, LANES), F32)],
        compiler_params=_cparams(1),
        name="proj_ln_route",
    )(ya, yb, resid, w_out, ln, wr, br)


def _row_copy(src_ref, src_row, dst_ref, dst_row, sem):
    return pltpu.make_async_copy(src_ref.at[pl.ds(src_row, 1)], dst_ref.at[pl.ds(dst_row, 1)], sem)


def _start_all(copies):
    for n, cp in enumerate(copies):
        cp.start(priority=n % 2)


def _dispatch_body(pad_start_ref, pad_len_ref, used_rows_ref, dest_ref, h_ref, rows_ref, hbuf_ref, zbuf_ref,
                   lsem, ssem, zsem):
    i = pl.program_id(0)
    last = pl.num_programs(0) - 1
    tm = hbuf_ref.shape[1]
    slot = i % 2

    def load(tile, s):
        return pltpu.make_async_copy(h_ref.at[pl.ds(tile * tm, tm)], hbuf_ref.at[s], lsem.at[s])

    def scatters(s):
        return [_row_copy(hbuf_ref.at[s], r, rows_ref, dest_ref[0, 0, 2 * r + q], ssem.at[s])
                for r in range(tm) for q in range(2)]

    @pl.when(i == 0)
    def _():
        load(0, 0).start()

    @pl.when(i >= 1)
    def _():
        for cp in scatters(1 - slot):
            cp.wait()

    @pl.when(i < last)
    def _():
        load(i + 1, 1 - slot).start()

    load(i, slot).wait()
    _start_all(scatters(slot))

    @pl.when(i == last)
    def _():
        for cp in scatters(slot):
            cp.wait()
        zbuf_ref[...] = jnp.zeros_like(zbuf_ref)
        sub = 8
        half = zbuf_ref.shape[0]

        def pad_fills():
            out = []
            for e in range(N_EXPERTS):
                start = pad_start_ref[e]
                npad = pad_len_ref[e]
                end = start + npad
                run = half
                while run >= sub:
                    end = end - (npad & run)
                    dst = rows_ref.at[pl.ds(pl.multiple_of(end, run), run)]
                    out.append(((npad & run) != 0,
                                pltpu.make_async_copy(zbuf_ref.at[pl.ds(0, run)], dst, zsem)))
                    run //= 2
                for k in range(sub - 1):
                    out.append((k < (npad & (sub - 1)), _row_copy(zbuf_ref, 0, rows_ref, start + k, zsem)))
            return out

        for pred, cp in pad_fills():
            pl.when(pred)(cp.start)
        for pred, cp in pad_fills():
            pl.when(pred)(cp.wait)

        def tail(b):
            return pltpu.make_async_copy(
                zbuf_ref, rows_ref.at[pl.ds(pl.multiple_of(b * half, half), half)], zsem)

        first_free = used_rows_ref[0] // half
        n_half = rows_ref.shape[0] // half
        lax.fori_loop(first_free, n_half, lambda b, c: (tail(b).start(), c)[1], 0)
        lax.fori_loop(first_free, n_half, lambda b, c: (tail(b).wait(), c)[1], 0)


def _dispatch(pad_start, pad_len, used_rows, dest3, h, n_rows):
    t, d = h.shape
    nt, _, tm2 = dest3.shape
    tm = tm2 // 2
    grid_spec = pltpu.PrefetchScalarGridSpec(
        num_scalar_prefetch=3,
        grid=(nt,),
        in_specs=[pl.BlockSpec((1, 1, tm2), lambda i, *_: (i, 0, 0), memory_space=pltpu.SMEM),
                  pl.BlockSpec(memory_space=pl.ANY)],
        out_specs=pl.BlockSpec(memory_space=pl.ANY),
        scratch_shapes=[pltpu.VMEM((2, tm, d), F32), pltpu.VMEM((MOE_BLOCK // 2, d), F32),
                        pltpu.SemaphoreType.DMA((2,)), pltpu.SemaphoreType.DMA((2,)),
                        pltpu.SemaphoreType.DMA],
    )
    return pl.pallas_call(
        _dispatch_body,
        grid_spec=grid_spec,
        out_shape=jax.ShapeDtypeStruct((n_rows, d), F32),
        compiler_params=_cparams(1),
        name="moe_dispatch",
    )(pad_start, pad_len, used_rows, dest3, h)


def _expert_body(first_ref, nblk_ref, nused_ref, x_ref, w1_ref, w3_ref, w2_ref, y_ref,
                 xbuf_ref, ybuf_ref, w1b_ref, w3b_ref, w2b_ref, xsem, ysem):
    e = pl.program_id(0)
    row0 = first_ref[e] * MOE_BLOCK
    nblk = nblk_ref[e]
    big = xbuf_ref.shape[1]
    per_big = big // MOE_BLOCK

    def cast_weights():
        w1b_ref[...] = w1_ref[0].astype(BF16)
        w3b_ref[...] = w3_ref[0].astype(BF16)
        w2b_ref[...] = w2_ref[0].astype(BF16)

    def stream(start_row, count, size, casts_first):
        def x_copy(j, s):
            src = x_ref.at[pl.ds(pl.multiple_of(start_row + j * size, MOE_BLOCK), size)]
            return pltpu.make_async_copy(src, xbuf_ref.at[s, pl.ds(0, size)], xsem.at[s])

        def y_copy(j, s):
            dst = y_ref.at[pl.ds(pl.multiple_of(start_row + j * size, MOE_BLOCK), size)]
            return pltpu.make_async_copy(ybuf_ref.at[s, pl.ds(0, size)], dst, ysem.at[s])

        @pl.when(count > 0)
        def _():
            x_copy(0, 0).start(priority=1)
            pl.when(casts_first)(cast_weights)

            def chunk(j, carry):
                s = j % 2

                @pl.when(j + 1 < count)
                def _():
                    x_copy(j + 1, 1 - s).start(priority=1)

                x_copy(j, s).wait()
                xb = xbuf_ref[s, pl.ds(0, size), :].astype(BF16)
                hid = _silu(_d(xb, w1b_ref[...])) * _d(xb, w3b_ref[...])

                @pl.when(j >= 2)
                def _():
                    y_copy(j - 2, s).wait()

                ybuf_ref[s, pl.ds(0, size), :] = _d(hid.astype(BF16), w2b_ref[...])
                y_copy(j, s).start()
                return carry

            lax.fori_loop(0, count, chunk, 0)

            @pl.when(count >= 2)
            def _():
                y_copy(count - 2, count % 2).wait()

            y_copy(count - 1, (count - 1) % 2).wait()

    @pl.when(nblk > 0)
    def _():
        nbig = nblk // per_big
        stream(row0, nbig, big, nbig > 0)
        stream(row0 + nbig * big, nblk - nbig * per_big, MOE_BLOCK, nbig == 0)

    @pl.when(e == pl.num_programs(0) - 1)
    def _():
        ybuf_ref[0, pl.ds(0, MOE_BLOCK), :] = jnp.zeros((MOE_BLOCK, ybuf_ref.shape[2]), F32)
        nused = nused_ref[0]
        ntot = y_ref.shape[0] // MOE_BLOCK

        def fill(b, carry):
            cp = pltpu.make_async_copy(ybuf_ref.at[0, pl.ds(0, MOE_BLOCK)],
                                       y_ref.at[pl.ds(b * MOE_BLOCK, MOE_BLOCK)], ysem.at[0])
            cp.start()
            cp.wait()
            return carry

        lax.fori_loop(nused, ntot, fill, 0)


def _experts(first_blk, nblk, nused, x_rows, w1, w3, w2, layer):
    r, d = x_rows.shape
    hid = w1.shape[-1]
    grid_spec = pltpu.PrefetchScalarGridSpec(
        num_scalar_prefetch=3,
        grid=(N_EXPERTS,),
        in_specs=[pl.BlockSpec(memory_space=pl.ANY),
                  pl.BlockSpec((None, 1, d, hid), lambda e, *_: (layer, e, 0, 0)),
                  pl.BlockSpec((None, 1, d, hid), lambda e, *_: (layer, e, 0, 0)),
                  pl.BlockSpec((None, 1, hid, d), lambda e, *_: (layer, e, 0, 0))],
        out_specs=pl.BlockSpec(memory_space=pl.ANY),
        scratch_shapes=[pltpu.VMEM((2, EXPERT_CHUNK, d), F32), pltpu.VMEM((2, EXPERT_CHUNK, d), F32),
                        pltpu.VMEM((d, hid), BF16), pltpu.VMEM((d, hid), BF16), pltpu.VMEM((hid, d), BF16),
                        pltpu.SemaphoreType.DMA((2,)), pltpu.SemaphoreType.DMA((2,))],
    )
    return pl.pallas_call(
        _expert_body,
        grid_spec=grid_spec,
        out_shape=jax.ShapeDtypeStruct((r, d), F32),
        compiler_params=_cparams(1),
        name="moe_experts",
    )(first_blk, nblk, nused, x_rows, w1, w3, w2)


def _combine_body(dest_ref, dest_next_ref, gate_ref, h_ref, ln_ref, rows_ref, o_ref, buf_ref, sem):
    i = pl.program_id(0)
    last = pl.num_programs(0) - 1
    tm = h_ref.shape[0]

    def gather(dref, slot):
        return [_row_copy(rows_ref, dref[0, 0, 2 * r + s], buf_ref.at[slot, s], r, sem.at[slot])
                for r in range(tm) for s in range(2)]

    @pl.when(i == 0)
    def _():
        _start_all(gather(dest_ref, 0))

    _start_all(gather(dest_next_ref, (i + 1) % 2))
    slot = i % 2
    for cp in gather(dest_ref, slot):
        cp.wait()
    gate = gate_ref[...]
    y = gate[:, 4:5] * buf_ref[slot, 0] + gate[:, 5:6] * buf_ref[slot, 1]
    o_ref[...] = _layer_norm(DN_ALPHA * h_ref[...] + y, ln_ref[0:1, :], ln_ref[1:2, :])

    @pl.when(i == last)
    def _():
        for cp in gather(dest_next_ref, (i + 1) % 2):
            cp.wait()


def _combine(dest3, route, h, ln, y_rows, tm):
    t, d = h.shape
    nt = t // tm
    return pl.pallas_call(
        _combine_body,
        grid=(nt,),
        in_specs=[pl.BlockSpec((1, 1, 2 * tm), lambda i: (i, 0, 0), memory_space=pltpu.SMEM),
                  pl.BlockSpec((1, 1, 2 * tm), lambda i: (jnp.minimum(i + 1, nt - 1), 0, 0),
                               memory_space=pltpu.SMEM),
                  pl.BlockSpec((tm, 8), lambda i: (i, 0)),
                  pl.BlockSpec((tm, d), lambda i: (i, 0)),
                  pl.BlockSpec((8, d), lambda i: (0, 0)),
                  pl.BlockSpec(memory_space=pl.ANY)],
        out_specs=pl.BlockSpec((tm, d), lambda i: (i, 0)),
        out_shape=jax.ShapeDtypeStruct((t, d), F32),
        scratch_shapes=[pltpu.VMEM((2, 2, tm, d), F32), pltpu.SemaphoreType.DMA((2,))],
        compiler_params=_cparams(1),
        name="moe_combine_ln",
    )(dest3, dest3, route, h, ln, y_rows)


def _moe(h, route, counts, w1, w3, w2, layer, ln):
    t, d = h.shape
    tm = min(COMBINE_TILE, t)
    a = 2 * t
    expert = route[:, 0:2].astype(jnp.int32)
    rank = route[:, 2:4].astype(jnp.int32)
    cnt = counts[0, :N_EXPERTS].astype(jnp.int32)
    padded = (cnt + MOE_BLOCK - 1) // MOE_BLOCK * MOE_BLOCK
    pend = jnp.cumsum(padded)
    pstart = pend - padded
    sel = expert[..., None] == jnp.arange(N_EXPERTS, dtype=jnp.int32)
    dest = jnp.sum(jnp.where(sel, pstart, 0), axis=-1) + rank
    n_blocks = -(-a // MOE_BLOCK) + N_EXPERTS
    nused = (pend[-1:] // MOE_BLOCK).astype(jnp.int32)
    dest3 = dest.reshape(t // tm, 1, 2 * tm)
    x_rows = _dispatch((pstart + cnt).astype(jnp.int32), (padded - cnt).astype(jnp.int32),
                       pend[-1:].astype(jnp.int32), dest3, h, n_blocks * MOE_BLOCK)
    y_rows = _experts((pstart // MOE_BLOCK).astype(jnp.int32), (padded // MOE_BLOCK).astype(jnp.int32),
                      nused, x_rows, w1, w3, w2, layer)
    return _combine(dest3, route, h, ln, y_rows, tm)


def _pad_rows(x, rows):
    return jnp.zeros((rows,) + x.shape[1:], x.dtype).at[:x.shape[0]].set(x)


def _route_weights(wg, bg, we, be):
    d = wg.shape[0]
    wr = jnp.zeros((d, LANES), F32)
    wr = wr.at[:, :MOE_GROUPS].set(wg)
    wr = wr.at[:, MOE_GROUPS:MOE_GROUPS + N_EXPERTS].set(jnp.transpose(we, (1, 0, 2)).reshape(d, N_EXPERTS))
    br = jnp.zeros((8, LANES), F32)
    br = br.at[0, :MOE_GROUPS].set(bg)
    br = br.at[0, MOE_GROUPS:MOE_GROUPS + N_EXPERTS].set(be.reshape(N_EXPERTS))
    return wr, br


def kernel(x, ab_w_in, rw_mu, rw_w0, rw_w2, rw_a0, rw_a2, rw_g2, rw_k_k, rw_k_a, rw_r_k, rw_gn_g, rw_gn_b, gla_gk_w2, gla_gk_b, gla_norm_g, ab_w_out, cd_w_in, s5_a_re, s5_a_im, s5_log_dt, s5_b_re, s5_b_im, s5_c_re, s5_c_im, s5_d, s5_glu_w, s5_glu_b, hg_lb, hg_norm_g, cd_w_out, ln1_g, ln1_b, moe_wg, moe_bg, moe_we, moe_be, moe_w1, moe_w3, moe_w2, ln2_g, ln2_b):
    batch, seq, d = x.shape
    t = batch * seq
    assert d == D_MODEL and seq % CHUNK == 0
    rows_per_seq = seq // S5_CHUNK
    assert rows_per_seq & (rows_per_seq - 1) == 0, "S5 chunk scan assumes a power-of-two chunk count"
    tm = min(ROW_TILE, t)
    assert t % tm == 0
    xt = x.reshape(t, d)
    ln1 = [_pad_rows(jnp.stack([ln1_g[l], ln1_b[l]]), 8) for l in range(DEPTH)]
    ln2 = [_pad_rows(jnp.stack([ln2_g[l], ln2_b[l]]), 8) for l in range(DEPTH)]

    j = 0
    w = ab_w_in[j]
    r_, wl_, k_, v_, al_, gl_ = 0, 512, 576, 1088, 1600, 1664
    gq, gk, gv, glow, ggate = 1792, 2048, 2304, 2816, 2832
    w0cols = jnp.concatenate([
        w[:, r_:r_ + 512], w[:, k_:k_ + 512], w[:, v_:v_ + 512], w[:, gv:gv + 512], w[:, ggate:ggate + 512],
        w[:, wl_:wl_ + 64], w[:, al_:al_ + 64], w[:, gl_:gl_ + 128], w[:, gq:gq + 256], w[:, gk:gk + 256],
        w[:, glow:glow + 16], jnp.zeros((d, 112), F32)], axis=1).astype(BF16)
    p0 = _matmul(xt, w0cols, tm)

    mu = rw_mu[j]
    vec = _pad_rows(jnp.stack([mu[r_:r_ + 512], mu[k_:k_ + 512], mu[v_:v_ + 512], rw_w0[j], rw_a0[j],
                               rw_k_k[j], rw_k_a[j], rw_r_k[j].reshape(-1), rw_gn_g[j], rw_gn_b[j]]), 16)
    mulo = _pad_rows(jnp.concatenate([mu[wl_:wl_ + 64], mu[al_:al_ + 64], mu[gl_:gl_ + 128]])[None], 8)
    w2p = _pad_rows(rw_w2[j], 128)
    a2p = jnp.zeros((128, 512), F32).at[64:].set(rw_a2[j])
    p0 = p0.reshape(batch, seq, -1)
    y_rw = _rwkv(p0, vec, mulo, w2p, a2p, rw_g2[j]).reshape(t, RW_WIDTH)
    y_gla = _gla(p0, _pad_rows(gla_gk_w2[j], 128), _pad_rows(gla_gk_b[j][None], 8),
                 _pad_rows(gla_norm_g[j][None], 8)).reshape(t, 512)

    wr, br = _route_weights(moe_wg[0], moe_bg[0], moe_we[0], moe_be[0])
    h, route, counts = _proj_route(y_rw, y_gla, xt, ab_w_out[j].astype(BF16), ln1[0], wr, br, tm)
    h = _moe(h, route, counts, moe_w1, moe_w3, moe_w2, 0, ln2[0])

    p1 = _matmul(h, cd_w_in[j].astype(BF16), tm)
    lb_sm = jax.nn.softmax(hg_lb.astype(F32), axis=0)
    lower = (jnp.cumsum(lb_sm, axis=0) - lb_sm[0])[1]
    y_hg = _hgrn(p1.reshape(batch, seq, -1), _pad_rows(lower[None], 8),
                 _pad_rows(hg_norm_g[j][None], 8)).reshape(t, 512)

    toep, bmat, emat, pw = _s5_tables(s5_a_re[j], s5_a_im[j], s5_log_dt[j], s5_b_re[j], s5_b_im[j],
                                      s5_c_re[j], s5_c_im[j], rows_per_seq)
    y_ssm = _s5_scan(p1.reshape(batch, seq, -1), toep, bmat, emat, pw).reshape(t, 512)
    y_s5 = _s5_post(y_ssm, p1, _pad_rows(jnp.stack([s5_d[j], s5_glu_b[j]]), 8),
                    s5_glu_w[j].astype(BF16), tm)

    wr, br = _route_weights(moe_wg[1], moe_bg[1], moe_we[1], moe_be[1])
    h2, route, counts = _proj_route(y_s5, y_hg, h, cd_w_out[j].astype(BF16), ln1[1], wr, br, tm)
    out = _moe(h2, route, counts, moe_w1, moe_w3, moe_w2, 1, ln2[1])
    return out.reshape(batch, seq, d)
```

```python
import functools
import math

import numpy as np
import jax
import jax.numpy as jnp
from jax import lax
from jax.experimental import pallas as pl
from jax.experimental.pallas import tpu as pltpu

F32 = jnp.float32
BF16 = jnp.bfloat16

D_MODEL = 1024
DEPTH = 2
RW_HEAD = 64
RW_WIDTH = 512
RW_GN_EPS = 64e-5
GLA_HEADS = 4
GLA_DK = 64
GLA_DV = 128
GLA_GATE_TAU = 16.0
S5_GROUP = 16
S5_GROUPS = 32
S5_STATE = 64
HG_HEADS = 4
HG_DK = 128
CHUNK = 64
S5_CHUNK = 16
S5_SCAN_ROWS = 32
S5_TILE_GROUPS = 8
NORM_EPS = 1e-5
MOE_GROUPS = 4
MOE_PER_GROUP = 8
N_EXPERTS = 32
EXPERT_HIDDEN = 512
MOE_BLOCK = 128
COMBINE_TILE = 128
EXPERT_CHUNK = 512
ROW_TILE = 512
DN_ALPHA = (2.0 * DEPTH) ** 0.25

LANES = 128
VMEM_LIMIT = 56 * 1024 * 1024


def _cparams(n_axes=1):
    return pltpu.CompilerParams(dimension_semantics=("arbitrary",) * n_axes,
                                vmem_limit_bytes=VMEM_LIMIT)


def _d(a, b):
    return jnp.dot(a, b, preferred_element_type=F32)


def _d_nt(a, b):
    return lax.dot_general(a, b, (((1,), (1,)), ((), ())), preferred_element_type=F32)


def _d_tn(a, b):
    return lax.dot_general(a, b, (((0,), (0,)), ((), ())), preferred_element_type=F32)


def _split(a):
    hi = a.astype(BF16)
    lo = (a - hi.astype(F32)).astype(BF16)
    return hi, lo


def _split3(a):
    hi = a.astype(BF16)
    r1 = a - hi.astype(F32)
    mid = r1.astype(BF16)
    lo = (r1 - mid.astype(F32)).astype(BF16)
    return hi, mid, lo


def _bdot(a, b):
    return _d(a.astype(BF16), b.astype(BF16))


def _bdot_nt(a, b):
    return _d_nt(a.astype(BF16), b.astype(BF16))


def _bdot_tn(a, b):
    return _d_tn(a.astype(BF16), b.astype(BF16))


def _hdot_with(d, a, b):
    ah, al = _split(a)
    bh, bl = _split(b)
    return d(ah, bh) + (d(ah, bl) + d(al, bh))


def _hdot(a, b):
    return _hdot_with(_d, a, b)


def _hdot_nt(a, b):
    return _hdot_with(_d_nt, a, b)


def _hdot_tn(a, b):
    return _hdot_with(_d_tn, a, b)


def _xdot_r(e, a):
    ah, am, al = _split3(a)
    return _d(e, ah) + (_d(e, am) + _d(e, al))


def _iota(shape, dim):
    return lax.broadcasted_iota(jnp.int32, shape, dim)


def _softplus(x):
    return jnp.maximum(x, 0.0) + jnp.log1p(jnp.exp(-jnp.abs(x)))


def _sigmoid(x):
    return 1.0 / (1.0 + jnp.exp(-x))


def _silu(x):
    return x * _sigmoid(x)


def _segsum(a, e):
    m = a.shape[0]
    s = _d(jnp.concatenate(_split3(a), axis=0), e)
    return s[:m] + (s[m:2 * m] + s[2 * m:])


def _cumsum_rows(g, chunk=None):
    n = g.shape[0]
    chunk = n if chunk is None else chunk
    shift = int(math.log2(chunk))
    r, c = _iota((n, n), 0), _iota((n, n), 1)
    tril = jnp.where((r >= c) & ((r >> shift) == (c >> shift)), 1.0, 0.0).astype(BF16)
    return _xdot_r(tril, g)


def _shift_mix(x, prev_ref, b, mu):
    c = x.shape[0]
    rolled = pltpu.roll(x, 1, 0)
    prev = jnp.where(_iota(x.shape, 0) == 0, jnp.broadcast_to(prev_ref[b, 0:1, :], x.shape), rolled)
    prev_ref[b, 0:1, :] = x[c - 1:c, :]
    return x + mu * (prev - x)


def _layer_norm(x, g, b):
    mu = jnp.mean(x, axis=-1, keepdims=True)
    xc = x - mu
    var = jnp.mean(xc * xc, axis=-1, keepdims=True)
    return xc * lax.rsqrt(var + NORM_EPS) * g + b


def _mm_body(x_ref, w_ref, o_ref):
    o_ref[...] = _d(x_ref[...].astype(BF16), w_ref[...])


def _matmul(x, w_bf16, tm):
    m, k = x.shape
    n = w_bf16.shape[1]
    return pl.pallas_call(
        _mm_body,
        grid=(m // tm,),
        in_specs=[pl.BlockSpec((tm, k), lambda i: (i, 0)),
                  pl.BlockSpec((k, n), lambda i: (0, 0))],
        out_specs=pl.BlockSpec((tm, n), lambda i: (i, 0)),
        out_shape=jax.ShapeDtypeStruct((m, n), F32),
        compiler_params=_cparams(1),
        name="in_proj",
    )(x, w_bf16)


_RV_MU_R, _RV_MU_K, _RV_MU_V, _RV_W0, _RV_A0, _RV_KK, _RV_KA, _RV_RK, _RV_GNG, _RV_GNB = range(10)


def _rwkv_body(r_ref, k_ref, v_ref, lo_ref, vec_ref, mulo_ref, w2_ref, a2_ref, g2_ref, ones_ref,
               o_ref, pr_ref, pk_ref, pv_ref, plo_ref, st_ref):
    nb, c = r_ref.shape[0], r_ref.shape[1]
    npair = RW_WIDTH // LANES

    @pl.when(pl.program_id(0) == 0)
    def _():
        pr_ref[...] = jnp.zeros_like(pr_ref)
        pk_ref[...] = jnp.zeros_like(pk_ref)
        pv_ref[...] = jnp.zeros_like(pv_ref)
        plo_ref[...] = jnp.zeros_like(plo_ref)
        st_ref[...] = jnp.zeros_like(st_ref)

    def vec(i):
        return vec_ref[i:i + 1, :]

    ones_bd = ones_ref[...]
    lane = _iota((c, LANES), 1)
    m1 = lane < RW_HEAD
    row2 = _iota((2 * c, 4 * c), 0)
    col2 = _iota((2 * c, 4 * c), 1) & (c - 1)
    tri = ((row2 < c) & (row2 > col2)) | ((row2 >= c) & ((row2 - c) >= col2))
    eye2 = jnp.where(_iota((2 * c, 2 * c), 0) == _iota((2 * c, 2 * c), 1), 1.0, 0.0)
    bd_p = (_iota((LANES, LANES), 0) >> 6) == (_iota((LANES, LANES), 1) >> 6)

    def halves(x):
        return jnp.concatenate([jnp.where(m1, x, 0.0), jnp.where(m1, 0.0, x)], axis=0)

    def stacked(ref, prev_ref, mu):
        return jnp.concatenate([_shift_mix(ref[b], prev_ref, b, mu) for b in range(nb)], axis=0)

    xr = stacked(r_ref, pr_ref, vec(_RV_MU_R))
    xk = stacked(k_ref, pk_ref, vec(_RV_MU_K))
    xv = stacked(v_ref, pv_ref, vec(_RV_MU_V))
    xlo = stacked(lo_ref, plo_ref, mulo_ref[0:1, :])
    lo_a = xlo[:, :LANES]
    lo_g = xlo[:, LANES:]
    w = -_softplus(-(vec(_RV_W0) + _bdot(jnp.tanh(lo_a), w2_ref[...]))) - 0.5
    g = -jnp.exp(w)
    a = _sigmoid(vec(_RV_A0) + _bdot(lo_a, a2_ref[...]))
    gate = _bdot(_sigmoid(lo_g), g2_ref[...])
    kk = xk * vec(_RV_KK)
    kk = kk / jnp.maximum(jnp.sqrt(_segsum(kk * kk, ones_bd)), 1e-12)
    k2 = xk * (1.0 + (a - 1.0) * vec(_RV_KA))
    gc = _cumsum_rows(g, c)
    g_last = jnp.concatenate([jnp.broadcast_to(gc[(b + 1) * c - 1:(b + 1) * c, :], (c, gc.shape[1]))
                              for b in range(nb)], axis=0)
    e_neg = jnp.exp(-gc)
    e_end = jnp.exp(g_last - gc)
    prep = dict(
        xv=xv,
        at=-kk * jnp.exp(gc - g),
        bt=(kk * a) * e_neg, kt=k2 * e_neg, rt=xr * jnp.exp(gc),
        bh=(kk * a) * e_end, kh=k2 * e_end,
        gam=jnp.exp(g_last))

    chains = [(b, p) for b in range(nb) for p in range(npair)]

    def part(ch, name):
        b, p = ch
        return prep[name][b * c:(b + 1) * c, p * LANES:(p + 1) * LANES]

    a_ak, a_row, vv, ak, pinv = {}, {}, {}, {}, {}
    for ch in chains:
        lhs = jnp.concatenate([part(ch, 'at'), part(ch, 'rt')], axis=0)
        rhs = jnp.concatenate([halves(part(ch, 'bt')), halves(part(ch, 'kt'))], axis=0)
        aa = jnp.where(tri, _bdot_nt(lhs, rhs), 0.0)
        a_ak[ch] = aa[:c, 2 * c:].astype(BF16)
        a_row[ch] = aa[c:, :].astype(BF16)
        abd = halves(aa[:c, :2 * c])
        pinv[ch] = eye2 + abd
        ak[ch] = abd
        vv[ch] = halves(part(ch, 'xv')).astype(BF16)
    nlev = int(math.log2(c))
    for lev in range(nlev):
        for ch in chains:
            akb = ak[ch].astype(BF16)
            if lev == 0:
                ak[ch] = _d(akb, akb)
            elif lev < nlev - 1:
                out = _d(akb, jnp.concatenate([akb, pinv[ch].astype(BF16)], axis=1))
                ak[ch] = out[:, :2 * c]
                pinv[ch] = pinv[ch] + out[:, 2 * c:]
            else:
                pinv[ch] = pinv[ch] + _d(akb, pinv[ch].astype(BF16))
    x2 = {ch: _d(a_ak[ch], vv[ch]) for ch in chains}

    sts = {ch: st_ref[ch[0] * npair + ch[1]] for ch in chains}
    xs = {ch: _d_nt(jnp.concatenate([part(ch, 'at'), part(ch, 'rt')], axis=0).astype(BF16),
                    sts[ch].astype(BF16)) for ch in chains}
    us = {}
    for ch in chains:
        u2 = _d(pinv[ch].astype(BF16), halves(xs[ch][:c] + x2[ch]).astype(BF16))
        us[ch] = u2[:c] + u2[c:]
    ys = {}
    for ch in chains:
        b, p = ch
        u = us[ch]
        ys[ch] = xs[ch][c:] + _d(a_row[ch], jnp.concatenate([halves(u).astype(BF16), vv[ch]], axis=0))
        upd = _d_tn(jnp.concatenate([u, part(ch, 'xv')], axis=0).astype(BF16),
                    jnp.concatenate([part(ch, 'bh'), part(ch, 'kh')], axis=0).astype(BF16))
        st_ref[b * npair + p] = sts[ch] * part(ch, 'gam')[0:1, :] + jnp.where(bd_p, upd, 0.0)

    inv_n = 1.0 / RW_HEAD
    y = jnp.concatenate([jnp.concatenate([ys[(b, p)] for p in range(npair)], axis=1)
                         for b in range(nb)], axis=0)
    mean = _segsum(y, ones_bd) * inv_n
    yc = y - mean
    var = _segsum(yc * yc, ones_bd) * inv_n
    yn = yc * lax.rsqrt(var + RW_GN_EPS) * vec(_RV_GNG) + vec(_RV_GNB)
    bonus = _segsum(xr * k2 * vec(_RV_RK), ones_bd) * xv
    out = (yn + bonus) * gate
    for b in range(nb):
        o_ref[b] = out[b * c:(b + 1) * c, :]


def _rwkv(p0, vec, mulo, w2p, a2p, g2):
    batch, seq, _ = p0.shape
    nc = seq // CHUNK
    c = CHUNK
    ones_bd = jnp.asarray(np.kron(np.eye(RW_WIDTH // RW_HEAD), np.ones((RW_HEAD, RW_HEAD))), BF16)

    def col(j, width):
        return pl.BlockSpec((batch, c, width), lambda i: (0, i, j))

    def full(shape):
        return pl.BlockSpec(shape, lambda i: (0,) * len(shape))

    return pl.pallas_call(
        _rwkv_body,
        grid=(nc,),
        in_specs=[col(0, 512), col(1, 512), col(2, 512), col(10, 256),
                  full((16, 512)), full((8, 256)), full((128, 512)), full((128, 512)),
                  full((128, 512)), full((512, 512))],
        out_specs=pl.BlockSpec((batch, c, 512), lambda i: (0, i, 0)),
        out_shape=jax.ShapeDtypeStruct((batch, seq, RW_WIDTH), F32),
        scratch_shapes=[pltpu.VMEM((batch, 8, 512), F32), pltpu.VMEM((batch, 8, 512), F32),
                        pltpu.VMEM((batch, 8, 512), F32), pltpu.VMEM((batch, 8, 256), F32),
                        pltpu.VMEM((batch * RW_WIDTH // LANES, LANES, LANES), F32)],
        compiler_params=_cparams(1),
        name="rwkv7",
    )(p0, p0, p0, p0, vec, mulo, w2p, a2p, g2, ones_bd)


def _gla_core(qs, ks, vs, gs, st_ref, heads_per_block):
    nb = len(qs)
    c = qs[0].shape[0]
    hpb = heads_per_block
    nblk = qs[0].shape[1] // LANES
    dk_shift = int(math.log2(LANES // hpb))
    row = _iota(qs[0].shape, 0)

    prep = []
    for q, k, g in zip(qs, ks, gs):
        b = _cumsum_rows(g)

        def brow(i, b=b):
            return jnp.broadcast_to(b[i:i + 1, :], b.shape)

        b15, b31, b47, blast = brow(15), brow(31), brow(47), brow(c - 1)
        ref_b = jnp.where(row < 32, b15, b47)
        ref_d = jnp.where(row < 16, 0.0, jnp.where(row < 32, b15, jnp.where(row < 48, b31, b47)))
        prep.append(dict(
            q_a=q * jnp.exp(jnp.minimum(b - b31, 0.0)), k_a=k * jnp.exp(jnp.minimum(b31 - b, 0.0)),
            q_b=q * jnp.exp(jnp.minimum(b - ref_b, 0.0)), k_b=k * jnp.exp(jnp.minimum(ref_b - b, 0.0)),
            q_d=q * jnp.exp(b - ref_d), k_d=k * jnp.exp(ref_d - b),
            q_i=q * jnp.exp(b), k_s=k * jnp.exp(blast - b), gam=jnp.exp(b[c - 1:c, :])))

    ri = _iota((hpb * c, c), 0) & (c - 1)
    ci = _iota((hpb * c, c), 1)
    mask_a = (ri >= 32) & (ci < 32)
    mask_b = ((ri >> 5) == (ci >> 5)) & (((ri >> 4) & 1) == 1) & (((ci >> 4) & 1) == 0)
    mask_d = ((ri >> 4) == (ci >> 4)) & (ri >= ci)
    lane = _iota((c, LANES), 1)
    bd = (_iota((hpb * LANES, LANES), 0) >> 7) == (_iota((hpb * LANES, LANES), 1) >> dk_shift)

    def heads_rows(x):
        if hpb == 1:
            return x
        return jnp.concatenate([jnp.where((lane >> dk_shift) == h, x, 0.0) for h in range(hpb)], axis=0)

    chains = [(bi, blk) for bi in range(nb) for blk in range(nblk)]

    def part(ch, name):
        bi, blk = ch
        return prep[bi][name][:, blk * LANES:(blk + 1) * LANES]

    v_p = {ch: vs[ch[0]][:, ch[1] * hpb * LANES:(ch[1] + 1) * hpb * LANES].astype(BF16) for ch in chains}
    probs = {}
    for ch in chains:
        s_a = _bdot_nt(heads_rows(part(ch, 'q_a')), part(ch, 'k_a'))
        s_b = _bdot_nt(heads_rows(part(ch, 'q_b')), part(ch, 'k_b'))
        s_d = _bdot_nt(heads_rows(part(ch, 'q_d')), part(ch, 'k_d'))
        probs[ch] = (jnp.where(mask_a, s_a, 0.0) + jnp.where(mask_b, s_b, 0.0)
                     + jnp.where(mask_d, s_d, 0.0)).astype(BF16)
    outs = {}
    for ch in chains:
        pv = _d(probs[ch], v_p[ch])
        o = pv[:c]
        for h in range(1, hpb):
            o = jnp.where((_iota(o.shape, 1) >> 7) == h, pv[h * c:(h + 1) * c], o)
        si = ch[0] * nblk + ch[1]
        sp = st_ref[si]
        outs[ch] = o + _bdot_nt(part(ch, 'q_i'), sp)
        upd = _d_tn(v_p[ch], part(ch, 'k_s').astype(BF16))
        st_ref[si] = sp * part(ch, 'gam') + jnp.where(bd, upd, 0.0)
    return [jnp.concatenate([outs[(bi, blk)] for blk in range(nblk)], axis=1) for bi in range(nb)]


def _gated_rmsnorm(o, gate, norm_g):
    nh = o.shape[1] // LANES
    outs = []
    for h in range(nh):
        sl = slice(h * LANES, (h + 1) * LANES)
        oh = o[:, sl]
        ms = jnp.mean(oh * oh, axis=-1, keepdims=True)
        outs.append(oh * lax.rsqrt(ms + NORM_EPS) * norm_g * _silu(gate[:, sl]))
    return jnp.concatenate(outs, axis=1)


def _gla_body(q_ref, k_ref, v_ref, gate_ref, gk_ref, w2_ref, vec_ref, ng_ref, o_ref, st_ref):
    nb = q_ref.shape[0]

    @pl.when(pl.program_id(0) == 0)
    def _():
        st_ref[...] = jnp.zeros_like(st_ref)

    gs = [-_softplus(-(_hdot(gk_ref[b], w2_ref[...]) + vec_ref[0:1, :])) * (1.0 / GLA_GATE_TAU)
          for b in range(nb)]
    qs = [q_ref[b] * (GLA_DK ** -0.5) for b in range(nb)]
    os_ = _gla_core(qs, [k_ref[b] for b in range(nb)], [v_ref[b] for b in range(nb)], gs, st_ref, 2)
    for b in range(nb):
        o_ref[b] = _gated_rmsnorm(os_[b], gate_ref[b], ng_ref[0:1, :])


def _seq_specs(batch, c):
    def col(j, width):
        return pl.BlockSpec((batch, c, width), lambda i: (0, i, j))

    def full(shape):
        return pl.BlockSpec(shape, lambda i: (0,) * len(shape))

    return col, full


def _gla(p0, gk_w2p, gk_b, norm_g):
    batch, seq, _ = p0.shape
    c = CHUNK
    col, full = _seq_specs(batch, c)
    return pl.pallas_call(
        _gla_body,
        grid=(seq // c,),
        in_specs=[col(11, 256), col(12, 256), col(3, 512), col(4, 512), col(26, 128),
                  full((128, 256)), full((8, 256)), full((8, 128))],
        out_specs=pl.BlockSpec((batch, c, 512), lambda i: (0, i, 0)),
        out_shape=jax.ShapeDtypeStruct((batch, seq, 512), F32),
        scratch_shapes=[pltpu.VMEM((batch * 2, 2 * LANES, LANES), F32)],
        compiler_params=_cparams(1),
        name="gla",
    )(p0, p0, p0, p0, p0, gk_w2p, gk_b, norm_g)


def _hgrn_body(q_ref, f_ref, i_ref, gate_ref, lb_ref, ng_ref, o_ref, st_ref):
    nb = q_ref.shape[0]

    @pl.when(pl.program_id(0) == 0)
    def _():
        st_ref[...] = jnp.zeros_like(st_ref)

    lb = lb_ref[0:1, :]
    qs, ks, gs = [], [], []
    for b in range(nb):
        f = f_ref[b]
        gs.append(jnp.log(lb + (1.0 - lb) * _sigmoid(f)))
        ks.append((1.0 - lb) * _sigmoid(-f))
        qs.append(_silu(q_ref[b]))
    os_ = _gla_core(qs, ks, [i_ref[b] for b in range(nb)], gs, st_ref, 1)
    for b in range(nb):
        o_ref[b] = _gated_rmsnorm(os_[b], gate_ref[b], ng_ref[0:1, :])


def _hgrn(p1, lb, norm_g):
    batch, seq, _ = p1.shape
    c = CHUNK
    col, full = _seq_specs(batch, c)
    return pl.pallas_call(
        _hgrn_body,
        grid=(seq // c,),
        in_specs=[col(1, 512), col(2, 512), col(3, 512), col(4, 512), full((8, 512)), full((8, 128))],
        out_specs=pl.BlockSpec((batch, c, 512), lambda i: (0, i, 0)),
        out_shape=jax.ShapeDtypeStruct((batch, seq, 512), F32),
        scratch_shapes=[pltpu.VMEM((batch * HG_HEADS, LANES, LANES), F32)],
        compiler_params=_cparams(1),
        name="hgrn2",
    )(p1, p1, p1, p1, lb, norm_g)


def _s5_body(u_ref, dcat_ref, bm_ref, e_ref, pw_ref, o_ref, toep_ref):
    seq = u_ref.shape[0]
    rows = seq // S5_CHUNK

    @pl.when(pl.program_id(1) == 0)
    def _():
        for tk in range(S5_CHUNK):
            lo = (S5_CHUNK - 1 - tk) * LANES
            toep_ref[tk * LANES:(tk + 1) * LANES, :] = dcat_ref[0, :, lo:lo + S5_CHUNK * LANES]

    u = jnp.concatenate([u_ref[pl.ds(tk, rows, stride=S5_CHUNK), :] for tk in range(S5_CHUNK)],
                        axis=1).astype(BF16)
    s = _d(u, bm_ref[0])
    rin = _iota(s.shape, 0)
    pw = pw_ref[0]

    def swap_halves(x):
        return jnp.concatenate([pltpu.roll(x[:, k * LANES:(k + 1) * LANES], S5_STATE, 1)
                                for k in range(x.shape[1] // LANES)], axis=1)

    h = s
    for lev in range(int(math.log2(rows))):
        sh = 1 << lev
        hs = jnp.where(rin >= sh, pltpu.roll(h, sh, 0), 0.0)
        h = h + pw[2 * lev:2 * lev + 1, :] * hs + pw[2 * lev + 1:2 * lev + 2, :] * swap_halves(hs)
    hprev = jnp.where(rin >= 1, pltpu.roll(h, 1, 0), 0.0)
    y = _d(u, toep_ref[...]) + _bdot(hprev, e_ref[0])
    for tk in range(S5_CHUNK):
        o_ref[pl.ds(tk, rows, stride=S5_CHUNK), :] = y[:, tk * LANES:(tk + 1) * LANES]


def _s5_scan(p1, dcat, bmat, emat, pw):
    batch, seq, _ = p1.shape
    nt, width, nstate = bmat.shape

    def table(shape):
        return pl.BlockSpec((1,) + shape, lambda j, b: (j, 0, 0), pipeline_mode=pl.Buffered(1))

    return pl.pallas_call(
        _s5_body,
        grid=(nt, batch),
        in_specs=[pl.BlockSpec((None, seq, LANES), lambda j, b: (b, 0, j)),
                  table(dcat.shape[1:]), table((width, nstate)), table((nstate, width)),
                  table((S5_SCAN_ROWS, nstate))],
        out_specs=pl.BlockSpec((None, seq, LANES), lambda j, b: (b, 0, j)),
        out_shape=jax.ShapeDtypeStruct((batch, seq, nt * LANES), F32),
        scratch_shapes=[pltpu.VMEM((width, width), BF16)],
        compiler_params=_cparams(2),
        name="s5_scan",
    )(p1, dcat, bmat, emat, pw)


def _s5_post_body(y_ref, u_ref, vec_ref, w_ref, o_ref):
    y = y_ref[...] + vec_ref[0:1, :] * u_ref[...]
    y = 0.5 * y * (1.0 + jnp.tanh(math.sqrt(2.0 / math.pi) * (y + 0.044715 * (y * y * y))))
    o_ref[...] = y * _sigmoid(_bdot(y, w_ref[...]) + vec_ref[1:2, :])


def _s5_post(y_ssm, p1, vec, glu_w, tm):
    t = y_ssm.shape[0]
    return pl.pallas_call(
        _s5_post_body,
        grid=(t // tm,),
        in_specs=[pl.BlockSpec((tm, 512), lambda i: (i, 0)),
                  pl.BlockSpec((tm, 512), lambda i: (i, 0)),
                  pl.BlockSpec((8, 512), lambda i: (0, 0)),
                  pl.BlockSpec((512, 512), lambda i: (0, 0))],
        out_specs=pl.BlockSpec((tm, 512), lambda i: (i, 0)),
        out_shape=jax.ShapeDtypeStruct((t, 512), F32),
        compiler_params=_cparams(1),
        name="s5_post",
    )(y_ssm, p1, vec, glu_w)


def _s5_tables(a_re, a_im, log_dt, b_re, b_im, c_re, c_im, rows_per_seq):
    c = S5_CHUNK
    lam_re = jnp.minimum(a_re, -1e-4)
    lam_im = a_im
    dt = jnp.exp(log_dt)[:, None]
    mag = jnp.exp(lam_re * dt)
    abar_re = mag * jnp.cos(lam_im * dt)
    abar_im = mag * jnp.sin(lam_im * dt)
    den = lam_re * lam_re + lam_im * lam_im
    num_re = abar_re - 1.0
    z_re = (num_re * lam_re + abar_im * lam_im) / den
    z_im = (abar_im * lam_re - num_re * lam_im) / den

    def power(n):
        n = jnp.asarray(n, F32)[..., None, None]
        m = jnp.exp(n * (lam_re * dt))
        return m * jnp.cos(n * (lam_im * dt)), m * jnp.sin(n * (lam_im * dt))

    def cmul(ar, ai, br, bi):
        return ar * br - ai * bi, ar * bi + ai * br

    tau = jnp.arange(c)
    p_re, p_im = power(tau)
    zb_re, zb_im = cmul(z_re[..., None], z_im[..., None], b_re, b_im)
    cp_re, cp_im = cmul(c_re[None], c_im[None], p_re[:, :, None, :], p_im[:, :, None, :])
    hi = lax.Precision.HIGHEST
    kern = (jnp.einsum('tgon,gni->gtoi', cp_re, zb_re, precision=hi)
            - jnp.einsum('tgon,gni->gtoi', cp_im, zb_im, precision=hi))
    q_re, q_im = power(c - 1 - tau)
    bm_re, bm_im = cmul(q_re[..., None], q_im[..., None], zb_re[None], zb_im[None])
    bmat = jnp.concatenate([jnp.transpose(bm_re, (1, 0, 3, 2)), jnp.transpose(bm_im, (1, 0, 3, 2))],
                           axis=-1).reshape(S5_GROUPS, c * S5_GROUP, 2 * S5_STATE)
    r_re, r_im = power(tau + 1)
    e_re, e_im = cmul(c_re[None], c_im[None], r_re[:, :, None, :], r_im[:, :, None, :])
    emat = jnp.concatenate([jnp.transpose(e_re, (1, 3, 0, 2)), -jnp.transpose(e_im, (1, 3, 0, 2))],
                           axis=1).reshape(S5_GROUPS, 2 * S5_STATE, c * S5_GROUP)
    nlev = int(math.log2(rows_per_seq))
    assert 2 * nlev <= S5_SCAN_ROWS
    s_re, s_im = power(c * (2 ** jnp.arange(nlev)))
    pw = jnp.zeros((S5_GROUPS, S5_SCAN_ROWS, 2 * S5_STATE), F32)
    pw = pw.at[:, 0:2 * nlev:2, :].set(jnp.transpose(jnp.concatenate([s_re, s_re], -1), (1, 0, 2)))
    pw = pw.at[:, 1:2 * nlev:2, :].set(jnp.transpose(jnp.concatenate([-s_im, s_im], -1), (1, 0, 2)))
    nt, gl = S5_GROUPS // S5_TILE_GROUPS, S5_TILE_GROUPS
    eye = jnp.eye(gl, dtype=F32)
    spread = jnp.tile(jnp.eye(S5_GROUP, dtype=F32), (1, gl))
    lane_group = jnp.arange(LANES) // S5_GROUP
    kern5 = jnp.transpose(kern.reshape(nt, gl, c, S5_GROUP, S5_GROUP), (0, 2, 1, 4, 3))
    dblk = jnp.einsum('jtgio,op->jtgip', kern5, spread, precision=hi)
    dblk = jnp.where((jnp.arange(gl)[:, None, None] == lane_group[None, None, :]), dblk, 0.0)
    dblk = dblk.reshape(nt, c, LANES, LANES)
    dcat = jnp.transpose(dblk, (0, 2, 1, 3)).reshape(nt, LANES, c * LANES)
    dcat = jnp.pad(dcat, ((0, 0), (0, 0), ((c - 1) * LANES, 0)))
    bmat5 = jnp.transpose(bmat.reshape(nt, gl, c, S5_GROUP, 2 * S5_STATE), (0, 2, 1, 3, 4))
    bmat_t = (bmat5[:, :, :, :, None, :] * eye[None, None, :, None, :, None])
    bmat_t = bmat_t.reshape(nt, c * LANES, gl * 2 * S5_STATE)
    emat5 = jnp.transpose(emat.reshape(nt, gl, 2 * S5_STATE, c, S5_GROUP), (0, 3, 1, 2, 4))
    eblk = jnp.einsum('jtgno,op->jtgnp', emat5, spread, precision=hi)
    eblk = jnp.where((jnp.arange(gl)[:, None, None] == lane_group[None, None, :]), eblk, 0.0)
    emat_t = jnp.transpose(eblk.reshape(nt, c, gl * 2 * S5_STATE, LANES), (0, 2, 1, 3))
    emat_t = emat_t.reshape(nt, gl * 2 * S5_STATE, c * LANES)
    pw_t = jnp.transpose(pw.reshape(nt, gl, S5_SCAN_ROWS, 2 * S5_STATE), (0, 2, 1, 3))
    pw_t = pw_t.reshape(nt, S5_SCAN_ROWS, gl * 2 * S5_STATE)
    return dcat.astype(BF16), bmat_t.astype(BF16), emat_t.astype(BF16), pw_t


def _proj_route_body(ya_ref, yb_ref, x_ref, w_ref, ln_ref, wr_ref, br_ref,
                     h_ref, route_ref, cnt_ref, carry_ref):
    tm = ya_ref.shape[0]
    half = w_ref.shape[0] // 2

    @pl.when(pl.program_id(0) == 0)
    def _():
        carry_ref[...] = jnp.zeros_like(carry_ref)

    mix = _bdot(ya_ref[...], w_ref[:half, :]) + _bdot(yb_ref[...], w_ref[half:, :])
    h = _layer_norm(DN_ALPHA * x_ref[...] + mix, ln_ref[0:1, :], ln_ref[1:2, :])
    h_ref[...] = h

    logits = _hdot(h, wr_ref[...]) + br_ref[0:1, :]
    lane = _iota((tm, LANES), 1).astype(F32)
    neg = -jnp.inf

    def softmax_masked(mask):
        xm = jnp.where(mask, logits, neg)
        m = jnp.max(xm, axis=-1, keepdims=True)
        e = jnp.exp(xm - m)
        return e / jnp.sum(e, axis=-1, keepdims=True)

    def top1(pm):
        m = jnp.max(pm, axis=-1, keepdims=True)
        idx = jnp.min(jnp.where(pm == m, lane, float(LANES)), axis=-1, keepdims=True)
        return m, idx

    coarse = jnp.where(lane < MOE_GROUPS, softmax_masked(lane < MOE_GROUPS), -1.0)
    p_grp, grp = top1(coarse)
    lo = MOE_GROUPS + MOE_PER_GROUP * grp
    fmask = (lane >= lo) & (lane < lo + MOE_PER_GROUP)
    fine = jnp.where(fmask, softmax_masked(fmask), -1.0)
    p1, j1 = top1(fine)
    p2, j2 = top1(jnp.where(lane == j1, -1.0, fine))
    denom = p1 + p2
    g1 = p_grp * (p1 / denom)
    g2 = p_grp * (p2 / denom)
    e1 = j1 - MOE_GROUPS
    e2 = j2 - MOE_GROUPS

    oh1 = jnp.where(lane == e1, 1.0, 0.0)
    oh2 = jnp.where(lane == e2, 1.0, 0.0)
    cnt = oh1 + oh2
    strict = jnp.where(_iota((tm, tm), 0) > _iota((tm, tm), 1), 1.0, 0.0).astype(BF16)
    before = _d(strict, cnt.astype(BF16)) + carry_ref[0:1, :]
    r1 = jnp.sum(oh1 * before, axis=-1, keepdims=True)
    r2 = jnp.sum(oh2 * before, axis=-1, keepdims=True)
    carry_ref[0:1, :] = carry_ref[0:1, :] + jnp.sum(cnt, axis=0, keepdims=True)
    cnt_ref[...] = carry_ref[...]

    out = jnp.where(lane == 0, e1, 0.0)
    out = jnp.where(lane == 1, e2, out)
    out = jnp.where(lane == 2, r1, out)
    out = jnp.where(lane == 3, r2, out)
    out = jnp.where(lane == 4, g1, out)
    out = jnp.where(lane == 5, g2, out)
    route_ref[...] = out[:, :8]


def _proj_route(ya, yb, resid, w_out, ln, wr, br, tm):
    t, d = resid.shape
    return pl.pallas_call(
        _proj_route_body,
        grid=(t // tm,),
        in_specs=[pl.BlockSpec((tm, 512), lambda i: (i, 0)),
                  pl.BlockSpec((tm, 512), lambda i: (i, 0)),
                  pl.BlockSpec((tm, d), lambda i: (i, 0)),
                  pl.BlockSpec((d, d), lambda i: (0, 0)),
                  pl.BlockSpec((8, d), lambda i: (0, 0)),
                  pl.BlockSpec((d, LANES), lambda i: (0, 0)),
                  pl.BlockSpec((8, LANES), lambda i: (0, 0))],
        out_specs=[pl.BlockSpec((tm, d), lambda i: (i, 0)),
                   pl.BlockSpec((tm, 8), lambda i: (i, 0)),
                   pl.BlockSpec((8, LANES), lambda i: (0, 0))],
        out_shape=[jax.ShapeDtypeStruct((t, d), F32),
                   jax.ShapeDtypeStruct((t, 8), F32),
                   jax.ShapeDtypeStruct((8, LANES), F32)],
        scratch_shapes=[pltpu.VMEM((8, LANES), F32)],
        compiler_params=_cparams(1),
        name="proj_ln_route",
    )(ya, yb, resid, w_out, ln, wr, br)


def _row_copy(src_ref, src_row, dst_ref, dst_row, sem):
    return pltpu.make_async_copy(src_ref.at[pl.ds(src_row, 1)], dst_ref.at[pl.ds(dst_row, 1)], sem)


def _start_all(copies):
    for n, cp in enumerate(copies):
        cp.start(priority=n % 2)


def _dispatch_body(pad_start_ref, pad_len_ref, used_rows_ref, dest_ref, h_ref, rows_ref, hbuf_ref, zbuf_ref,
                   lsem, ssem, zsem):
    i = pl.program_id(0)
    last = pl.num_programs(0) - 1
    tm = hbuf_ref.shape[1]
    slot = i % 2

    def load(tile, s):
        return pltpu.make_async_copy(h_ref.at[pl.ds(tile * tm, tm)], hbuf_ref.at[s], lsem.at[s])

    def scatters(s):
        return [_row_copy(hbuf_ref.at[s], r, rows_ref, dest_ref[0, 0, 2 * r + q], ssem.at[s])
                for r in range(tm) for q in range(2)]

    @pl.when(i == 0)
    def _():
        load(0, 0).start()

    @pl.when(i >= 1)
    def _():
        for cp in scatters(1 - slot):
            cp.wait()

    @pl.when(i < last)
    def _():
        load(i + 1, 1 - slot).start()

    load(i, slot).wait()
    _start_all(scatters(slot))

    @pl.when(i == last)
    def _():
        for cp in scatters(slot):
            cp.wait()
        zbuf_ref[...] = jnp.zeros_like(zbuf_ref)
        sub = 8
        half = zbuf_ref.shape[0]

        def pad_fills():
            out = []
            for e in range(N_EXPERTS):
                start = pad_start_ref[e]
                npad = pad_len_ref[e]
                end = start + npad
                run = half
                while run >= sub:
                    end = end - (npad & run)
                    dst = rows_ref.at[pl.ds(pl.multiple_of(end, run), run)]
                    out.append(((npad & run) != 0,
                                pltpu.make_async_copy(zbuf_ref.at[pl.ds(0, run)], dst, zsem)))
                    run //= 2
                for k in range(sub - 1):
                    out.append((k < (npad & (sub - 1)), _row_copy(zbuf_ref, 0, rows_ref, start + k, zsem)))
            return out

        for pred, cp in pad_fills():
            pl.when(pred)(cp.start)
        for pred, cp in pad_fills():
            pl.when(pred)(cp.wait)

        def tail(b):
            return pltpu.make_async_copy(
                zbuf_ref, rows_ref.at[pl.ds(pl.multiple_of(b * half, half), half)], zsem)

        first_free = used_rows_ref[0] // half
        n_half = rows_ref.shape[0] // half
        lax.fori_loop(first_free, n_half, lambda b, c: (tail(b).start(), c)[1], 0)
        lax.fori_loop(first_free, n_half, lambda b, c: (tail(b).wait(), c)[1], 0)


def _dispatch(pad_start, pad_len, used_rows, dest3, h, n_rows):
    t, d = h.shape
    nt, _, tm2 = dest3.shape
    tm = tm2 // 2
    grid_spec = pltpu.PrefetchScalarGridSpec(
        num_scalar_prefetch=3,
        grid=(nt,),
        in_specs=[pl.BlockSpec((1, 1, tm2), lambda i, *_: (i, 0, 0), memory_space=pltpu.SMEM),
                  pl.BlockSpec(memory_space=pl.ANY)],
        out_specs=pl.BlockSpec(memory_space=pl.ANY),
        scratch_shapes=[pltpu.VMEM((2, tm, d), F32), pltpu.VMEM((MOE_BLOCK // 2, d), F32),
                        pltpu.SemaphoreType.DMA((2,)), pltpu.SemaphoreType.DMA((2,)),
                        pltpu.SemaphoreType.DMA],
    )
    return pl.pallas_call(
        _dispatch_body,
        grid_spec=grid_spec,
        out_shape=jax.ShapeDtypeStruct((n_rows, d), F32),
        compiler_params=_cparams(1),
        name="moe_dispatch",
    )(pad_start, pad_len, used_rows, dest3, h)


def _expert_body(first_ref, nblk_ref, nused_ref, x_ref, w1_ref, w3_ref, w2_ref, y_ref,
                 xbuf_ref, ybuf_ref, w1b_ref, w3b_ref, w2b_ref, xsem, ysem):
    e = pl.program_id(0)
    row0 = first_ref[e] * MOE_BLOCK
    nblk = nblk_ref[e]
    big = xbuf_ref.shape[1]
    per_big = big // MOE_BLOCK

    def cast_weights():
        w1b_ref[...] = w1_ref[0].astype(BF16)
        w3b_ref[...] = w3_ref[0].astype(BF16)
        w2b_ref[...] = w2_ref[0].astype(BF16)

    def stream(start_row, count, size, casts_first):
        def x_copy(j, s):
            src = x_ref.at[pl.ds(pl.multiple_of(start_row + j * size, MOE_BLOCK), size)]
            return pltpu.make_async_copy(src, xbuf_ref.at[s, pl.ds(0, size)], xsem.at[s])

        def y_copy(j, s):
            dst = y_ref.at[pl.ds(pl.multiple_of(start_row + j * size, MOE_BLOCK), size)]
            return pltpu.make_async_copy(ybuf_ref.at[s, pl.ds(0, size)], dst, ysem.at[s])

        @pl.when(count > 0)
        def _():
            x_copy(0, 0).start(priority=1)
            pl.when(casts_first)(cast_weights)

            def chunk(j, carry):
                s = j % 2

                @pl.when(j + 1 < count)
                def _():
                    x_copy(j + 1, 1 - s).start(priority=1)

                x_copy(j, s).wait()
                xb = xbuf_ref[s, pl.ds(0, size), :].astype(BF16)
                hid = _silu(_d(xb, w1b_ref[...])) * _d(xb, w3b_ref[...])

                @pl.when(j >= 2)
                def _():
                    y_copy(j - 2, s).wait()

                ybuf_ref[s, pl.ds(0, size), :] = _d(hid.astype(BF16), w2b_ref[...])
                y_copy(j, s).start()
                return carry

            lax.fori_loop(0, count, chunk, 0)

            @pl.when(count >= 2)
            def _():
                y_copy(count - 2, count % 2).wait()

            y_copy(count - 1, (count - 1) % 2).wait()

    @pl.when(nblk > 0)
    def _():
        nbig = nblk // per_big
        stream(row0, nbig, big, nbig > 0)
        stream(row0 + nbig * big, nblk - nbig * per_big, MOE_BLOCK, nbig == 0)

    @pl.when(e == pl.num_programs(0) - 1)
    def _():
        ybuf_ref[0, pl.ds(0, MOE_BLOCK), :] = jnp.zeros((MOE_BLOCK, ybuf_ref.shape[2]), F32)
        nused = nused_ref[0]
        ntot = y_ref.shape[0] // MOE_BLOCK

        def fill(b, carry):
            cp = pltpu.make_async_copy(ybuf_ref.at[0, pl.ds(0, MOE_BLOCK)],
                                       y_ref.at[pl.ds(b * MOE_BLOCK, MOE_BLOCK)], ysem.at[0])
            cp.start()
            cp.wait()
            return carry

        lax.fori_loop(nused, ntot, fill, 0)


def _experts(first_blk, nblk, nused, x_rows, w1, w3, w2, layer):
    r, d = x_rows.shape
    hid = w1.shape[-1]
    grid_spec = pltpu.PrefetchScalarGridSpec(
        num_scalar_prefetch=3,
        grid=(N_EXPERTS,),
        in_specs=[pl.BlockSpec(memory_space=pl.ANY),
                  pl.BlockSpec((None, 1, d, hid), lambda e, *_: (layer, e, 0, 0)),
                  pl.BlockSpec((None, 1, d, hid), lambda e, *_: (layer, e, 0, 0)),
                  pl.BlockSpec((None, 1, hid, d), lambda e, *_: (layer, e, 0, 0))],
        out_specs=pl.BlockSpec(memory_space=pl.ANY),
        scratch_shapes=[pltpu.VMEM((2, EXPERT_CHUNK, d), F32), pltpu.VMEM((2, EXPERT_CHUNK, d), F32),
                        pltpu.VMEM((d, hid), BF16), pltpu.VMEM((d, hid), BF16), pltpu.VMEM((hid, d), BF16),
                        pltpu.SemaphoreType.DMA((2,)), pltpu.SemaphoreType.DMA((2,))],
    )
    return pl.pallas_call(
        _expert_body,
        grid_spec=grid_spec,
        out_shape=jax.ShapeDtypeStruct((r, d), F32),
        compiler_params=_cparams(1),
        name="moe_experts",
    )(first_blk, nblk, nused, x_rows, w1, w3, w2)


def _combine_body(dest_ref, dest_next_ref, gate_ref, h_ref, ln_ref, rows_ref, o_ref, buf_ref, sem):
    i = pl.program_id(0)
    last = pl.num_programs(0) - 1
    tm = h_ref.shape[0]

    def gather(dref, slot):
        return [_row_copy(rows_ref, dref[0, 0, 2 * r + s], buf_ref.at[slot, s], r, sem.at[slot])
                for r in range(tm) for s in range(2)]

    @pl.when(i == 0)
    def _():
        _start_all(gather(dest_ref, 0))

    _start_all(gather(dest_next_ref, (i + 1) % 2))
    slot = i % 2
    for cp in gather(dest_ref, slot):
        cp.wait()
    gate = gate_ref[...]
    y = gate[:, 4:5] * buf_ref[slot, 0] + gate[:, 5:6] * buf_ref[slot, 1]
    o_ref[...] = _layer_norm(DN_ALPHA * h_ref[...] + y, ln_ref[0:1, :], ln_ref[1:2, :])

    @pl.when(i == last)
    def _():
        for cp in gather(dest_next_ref, (i + 1) % 2):
            cp.wait()


def _combine(dest3, route, h, ln, y_rows, tm):
    t, d = h.shape
    nt = t // tm
    return pl.pallas_call(
        _combine_body,
        grid=(nt,),
        in_specs=[pl.BlockSpec((1, 1, 2 * tm), lambda i: (i, 0, 0), memory_space=pltpu.SMEM),
                  pl.BlockSpec((1, 1, 2 * tm), lambda i: (jnp.minimum(i + 1, nt - 1), 0, 0),
                               memory_space=pltpu.SMEM),
                  pl.BlockSpec((tm, 8), lambda i: (i, 0)),
                  pl.BlockSpec((tm, d), lambda i: (i, 0)),
                  pl.BlockSpec((8, d), lambda i: (0, 0)),
                  pl.BlockSpec(memory_space=pl.ANY)],
        out_specs=pl.BlockSpec((tm, d), lambda i: (i, 0)),
        out_shape=jax.ShapeDtypeStruct((t, d), F32),
        scratch_shapes=[pltpu.VMEM((2, 2, tm, d), F32), pltpu.SemaphoreType.DMA((2,))],
        compiler_params=_cparams(1),
        name="moe_combine_ln",
    )(dest3, dest3, route, h, ln, y_rows)


def _moe(h, route, counts, w1, w3, w2, layer, ln):
    t, d = h.shape
    tm = min(COMBINE_TILE, t)
    a = 2 * t
    expert = route[:, 0:2].astype(jnp.int32)
    rank = route[:, 2:4].astype(jnp.int32)
    cnt = counts[0, :N_EXPERTS].astype(jnp.int32)
    padded = (cnt + MOE_BLOCK - 1) // MOE_BLOCK * MOE_BLOCK
    pend = jnp.cumsum(padded)
    pstart = pend - padded
    sel = expert[..., None] == jnp.arange(N_EXPERTS, dtype=jnp.int32)
    dest = jnp.sum(jnp.where(sel, pstart, 0), axis=-1) + rank
    n_blocks = -(-a // MOE_BLOCK) + N_EXPERTS
    nused = (pend[-1:] // MOE_BLOCK).astype(jnp.int32)
    dest3 = dest.reshape(t // tm, 1, 2 * tm)
    x_rows = _dispatch((pstart + cnt).astype(jnp.int32), (padded - cnt).astype(jnp.int32),
                       pend[-1:].astype(jnp.int32), dest3, h, n_blocks * MOE_BLOCK)
    y_rows = _experts((pstart // MOE_BLOCK).astype(jnp.int32), (padded // MOE_BLOCK).astype(jnp.int32),
                      nused, x_rows, w1, w3, w2, layer)
    return _combine(dest3, route, h, ln, y_rows, tm)


def _pad_rows(x, rows):
    return jnp.zeros((rows,) + x.shape[1:], x.dtype).at[:x.shape[0]].set(x)


def _route_weights(wg, bg, we, be):
    d = wg.shape[0]
    wr = jnp.zeros((d, LANES), F32)
    wr = wr.at[:, :MOE_GROUPS].set(wg)
    wr = wr.at[:, MOE_GROUPS:MOE_GROUPS + N_EXPERTS].set(jnp.transpose(we, (1, 0, 2)).reshape(d, N_EXPERTS))
    br = jnp.zeros((8, LANES), F32)
    br = br.at[0, :MOE_GROUPS].set(bg)
    br = br.at[0, MOE_GROUPS:MOE_GROUPS + N_EXPERTS].set(be.reshape(N_EXPERTS))
    return wr, br


def kernel(x, ab_w_in, rw_mu, rw_w0, rw_w2, rw_a0, rw_a2, rw_g2, rw_k_k, rw_k_a, rw_r_k, rw_gn_g, rw_gn_b, gla_gk_w2, gla_gk_b, gla_norm_g, ab_w_out, cd_w_in, s5_a_re, s5_a_im, s5_log_dt, s5_b_re, s5_b_im, s5_c_re, s5_c_im, s5_d, s5_glu_w, s5_glu_b, hg_lb, hg_norm_g, cd_w_out, ln1_g, ln1_b, moe_wg, moe_bg, moe_we, moe_be, moe_w1, moe_w3, moe_w2, ln2_g, ln2_b):
    batch, seq, d = x.shape
    t = batch * seq
    assert d == D_MODEL and seq % CHUNK == 0
    rows_per_seq = seq // S5_CHUNK
    assert rows_per_seq & (rows_per_seq - 1) == 0, "S5 chunk scan assumes a power-of-two chunk count"
    tm = min(ROW_TILE, t)
    assert t % tm == 0
    xt = x.reshape(t, d)
    ln1 = [_pad_rows(jnp.stack([ln1_g[l], ln1_b[l]]), 8) for l in range(DEPTH)]
    ln2 = [_pad_rows(jnp.stack([ln2_g[l], ln2_b[l]]), 8) for l in range(DEPTH)]

    j = 0
    w = ab_w_in[j]
    r_, wl_, k_, v_, al_, gl_ = 0, 512, 576, 1088, 1600, 1664
    gq, gk, gv, glow, ggate = 1792, 2048, 2304, 2816, 2832
    w0cols = jnp.concatenate([
        w[:, r_:r_ + 512], w[:, k_:k_ + 512], w[:, v_:v_ + 512], w[:, gv:gv + 512], w[:, ggate:ggate + 512],
        w[:, wl_:wl_ + 64], w[:, al_:al_ + 64], w[:, gl_:gl_ + 128], w[:, gq:gq + 256], w[:, gk:gk + 256],
        w[:, glow:glow + 16], jnp.zeros((d, 112), F32)], axis=1).astype(BF16)
    p0 = _matmul(xt, w0cols, tm)

    mu = rw_mu[j]
    vec = _pad_rows(jnp.stack([mu[r_:r_ + 512], mu[k_:k_ + 512], mu[v_:v_ + 512], rw_w0[j], rw_a0[j],
                               rw_k_k[j], rw_k_a[j], rw_r_k[j].reshape(-1), rw_gn_g[j], rw_gn_b[j]]), 16)
    mulo = _pad_rows(jnp.concatenate([mu[wl_:wl_ + 64], mu[al_:al_ + 64], mu[gl_:gl_ + 128]])[None], 8)
    w2p = _pad_rows(rw_w2[j], 128)
    a2p = jnp.zeros((128, 512), F32).at[64:].set(rw_a2[j])
    p0 = p0.reshape(batch, seq, -1)
    y_rw = _rwkv(p0, vec, mulo, w2p, a2p, rw_g2[j]).reshape(t, RW_WIDTH)
    y_gla = _gla(p0, _pad_rows(gla_gk_w2[j], 128), _pad_rows(gla_gk_b[j][None], 8),
                 _pad_rows(gla_norm_g[j][None], 8)).reshape(t, 512)

    wr, br = _route_weights(moe_wg[0], moe_bg[0], moe_we[0], moe_be[0])
    h, route, counts = _proj_route(y_rw, y_gla, xt, ab_w_out[j].astype(BF16), ln1[0], wr, br, tm)
    h = _moe(h, route, counts, moe_w1, moe_w3, moe_w2, 0, ln2[0])

    p1 = _matmul(h, cd_w_in[j].astype(BF16), tm)
    lb_sm = jax.nn.softmax(hg_lb.astype(F32), axis=0)
    lower = (jnp.cumsum(lb_sm, axis=0) - lb_sm[0])[1]
    y_hg = _hgrn(p1.reshape(batch, seq, -1), _pad_rows(lower[None], 8),
                 _pad_rows(hg_norm_g[j][None], 8)).reshape(t, 512)

    toep, bmat, emat, pw = _s5_tables(s5_a_re[j], s5_a_im[j], s5_log_dt[j], s5_b_re[j], s5_b_im[j],
                                      s5_c_re[j], s5_c_im[j], rows_per_seq)
    y_ssm = _s5_scan(p1.reshape(batch, seq, -1), toep, bmat, emat, pw).reshape(t, 512)
    y_s5 = _s5_post(y_ssm, p1, _pad_rows(jnp.stack([s5_d[j], s5_glu_b[j]]), 8),
                    s5_glu_w[j].astype(BF16), tm)

    wr, br = _route_weights(moe_wg[1], moe_bg[1], moe_we[1], moe_be[1])
    h2, route, counts = _proj_route(y_s5, y_hg, h, cd_w_out[j].astype(BF16), ln1[1], wr, br, tm)
    out = _moe(h2, route, counts, moe_w1, moe_w3, moe_w2, 1, ln2[1])
    return out.reshape(batch, seq, d)
```

```python
import functools
import math

import numpy as np
import jax
import jax.numpy as jnp
from jax import lax
from jax.experimental import pallas as pl
from jax.experimental.pallas import tpu as pltpu

F32 = jnp.float32
BF16 = jnp.bfloat16

D_MODEL = 1024
DEPTH = 2
RW_HEAD = 64
RW_WIDTH = 512
RW_GN_EPS = 64e-5
GLA_HEADS = 4
GLA_DK = 64
GLA_DV = 128
GLA_GATE_TAU = 16.0
S5_GROUP = 16
S5_GROUPS = 32
S5_STATE = 64
HG_HEADS = 4
HG_DK = 128
CHUNK = 64
S5_CHUNK = 16
S5_SCAN_ROWS = 32
S5_TILE_GROUPS = 8
NORM_EPS = 1e-5
MOE_GROUPS = 4
MOE_PER_GROUP = 8
N_EXPERTS = 32
EXPERT_HIDDEN = 512
MOE_BLOCK = 128
COMBINE_TILE = 128
EXPERT_CHUNK = 512
ROW_TILE = 512
DN_ALPHA = (2.0 * DEPTH) ** 0.25

LANES = 128
VMEM_LIMIT = 56 * 1024 * 1024


def _cparams(n_axes=1):
    return pltpu.CompilerParams(dimension_semantics=("arbitrary",) * n_axes,
                                vmem_limit_bytes=VMEM_LIMIT)


def _d(a, b):
    return jnp.dot(a, b, preferred_element_type=F32)


def _d_nt(a, b):
    return lax.dot_general(a, b, (((1,), (1,)), ((), ())), preferred_element_type=F32)


def _d_tn(a, b):
    return lax.dot_general(a, b, (((0,), (0,)), ((), ())), preferred_element_type=F32)


def _split(a):
    hi = a.astype(BF16)
    lo = (a - hi.astype(F32)).astype(BF16)
    return hi, lo


def _split3(a):
    hi = a.astype(BF16)
    r1 = a - hi.astype(F32)
    mid = r1.astype(BF16)
    lo = (r1 - mid.astype(F32)).astype(BF16)
    return hi, mid, lo


def _bdot(a, b):
    return _d(a.astype(BF16), b.astype(BF16))


def _bdot_nt(a, b):
    return _d_nt(a.astype(BF16), b.astype(BF16))


def _bdot_tn(a, b):
    return _d_tn(a.astype(BF16), b.astype(BF16))


def _hdot_with(d, a, b):
    ah, al = _split(a)
    bh, bl = _split(b)
    return d(ah, bh) + (d(ah, bl) + d(al, bh))


def _hdot(a, b):
    return _hdot_with(_d, a, b)


def _hdot_nt(a, b):
    return _hdot_with(_d_nt, a, b)


def _hdot_tn(a, b):
    return _hdot_with(_d_tn, a, b)


def _xdot_r(e, a):
    ah, am, al = _split3(a)
    return _d(e, ah) + (_d(e, am) + _d(e, al))


def _iota(shape, dim):
    return lax.broadcasted_iota(jnp.int32, shape, dim)


def _softplus(x):
    return jnp.maximum(x, 0.0) + jnp.log1p(jnp.exp(-jnp.abs(x)))


def _sigmoid(x):
    return 1.0 / (1.0 + jnp.exp(-x))


def _silu(x):
    return x * _sigmoid(x)


def _segsum(a, e):
    m = a.shape[0]
    s = _d(jnp.concatenate(_split3(a), axis=0), e)
    return s[:m] + (s[m:2 * m] + s[2 * m:])


def _cumsum_rows(g, chunk=None):
    n = g.shape[0]
    chunk = n if chunk is None else chunk
    shift = int(math.log2(chunk))
    r, c = _iota((n, n), 0), _iota((n, n), 1)
    tril = jnp.where((r >= c) & ((r >> shift) == (c >> shift)), 1.0, 0.0).astype(BF16)
    return _xdot_r(tril, g)


def _shift_mix(x, prev_ref, b, mu):
    c = x.shape[0]
    rolled = pltpu.roll(x, 1, 0)
    prev = jnp.where(_iota(x.shape, 0) == 0, jnp.broadcast_to(prev_ref[b, 0:1, :], x.shape), rolled)
    prev_ref[b, 0:1, :] = x[c - 1:c, :]
    return x + mu * (prev - x)


def _layer_norm(x, g, b):
    mu = jnp.mean(x, axis=-1, keepdims=True)
    xc = x - mu
    var = jnp.mean(xc * xc, axis=-1, keepdims=True)
    return xc * lax.rsqrt(var + NORM_EPS) * g + b


def _mm_body(x_ref, w_ref, o_ref):
    o_ref[...] = _d(x_ref[...].astype(BF16), w_ref[...])


def _matmul(x, w_bf16, tm):
    m, k = x.shape
    n = w_bf16.shape[1]
    return pl.pallas_call(
        _mm_body,
        grid=(m // tm,),
        in_specs=[pl.BlockSpec((tm, k), lambda i: (i, 0)),
                  pl.BlockSpec((k, n), lambda i: (0, 0))],
        out_specs=pl.BlockSpec((tm, n), lambda i: (i, 0)),
        out_shape=jax.ShapeDtypeStruct((m, n), F32),
        compiler_params=_cparams(1),
        name="in_proj",
    )(x, w_bf16)


_RV_MU_R, _RV_MU_K, _RV_MU_V, _RV_W0, _RV_A0, _RV_KK, _RV_KA, _RV_RK, _RV_GNG, _RV_GNB = range(10)


def _rwkv_body(r_ref, k_ref, v_ref, lo_ref, vec_ref, mulo_ref, w2_ref, a2_ref, g2_ref, ones_ref,
               o_ref, pr_ref, pk_ref, pv_ref, plo_ref, st_ref):
    nb, c = r_ref.shape[0], r_ref.shape[1]
    npair = RW_WIDTH // LANES

    @pl.when(pl.program_id(0) == 0)
    def _():
        pr_ref[...] = jnp.zeros_like(pr_ref)
        pk_ref[...] = jnp.zeros_like(pk_ref)
        pv_ref[...] = jnp.zeros_like(pv_ref)
        plo_ref[...] = jnp.zeros_like(plo_ref)
        st_ref[...] = jnp.zeros_like(st_ref)

    def vec(i):
        return vec_ref[i:i + 1, :]

    ones_bd = ones_ref[...]
    lane = _iota((c, LANES), 1)
    m1 = lane < RW_HEAD
    row2 = _iota((2 * c, 4 * c), 0)
    col2 = _iota((2 * c, 4 * c), 1) & (c - 1)
    tri = ((row2 < c) & (row2 > col2)) | ((row2 >= c) & ((row2 - c) >= col2))
    eye2 = jnp.where(_iota((2 * c, 2 * c), 0) == _iota((2 * c, 2 * c), 1), 1.0, 0.0)
    bd_p = (_iota((LANES, LANES), 0) >> 6) == (_iota((LANES, LANES), 1) >> 6)

    def halves(x):
        return jnp.concatenate([jnp.where(m1, x, 0.0), jnp.where(m1, 0.0, x)], axis=0)

    def stacked(ref, prev_ref, mu):
        return jnp.concatenate([_shift_mix(ref[b], prev_ref, b, mu) for b in range(nb)], axis=0)

    xr = stacked(r_ref, pr_ref, vec(_RV_MU_R))
    xk = stacked(k_ref, pk_ref, vec(_RV_MU_K))
    xv = stacked(v_ref, pv_ref, vec(_RV_MU_V))
    xlo = stacked(lo_ref, plo_ref, mulo_ref[0:1, :])
    lo_a = xlo[:, :LANES]
    lo_g = xlo[:, LANES:]
    w = -_softplus(-(vec(_RV_W0) + _bdot(jnp.tanh(lo_a), w2_ref[...]))) - 0.5
    g = -jnp.exp(w)
    a = _sigmoid(vec(_RV_A0) + _bdot(lo_a, a2_ref[...]))
    gate = _bdot(_sigmoid(lo_g), g2_ref[...])
    kk = xk * vec(_RV_KK)
    kk = kk / jnp.maximum(jnp.sqrt(_segsum(kk * kk, ones_bd)), 1e-12)
    k2 = xk * (1.0 + (a - 1.0) * vec(_RV_KA))
    gc = _cumsum_rows(g, c)
    g_last = jnp.concatenate([jnp.broadcast_to(gc[(b + 1) * c - 1:(b + 1) * c, :], (c, gc.shape[1]))
                              for b in range(nb)], axis=0)
    e_neg = jnp.exp(-gc)
    e_end = jnp.exp(g_last - gc)
    prep = dict(
        xv=xv,
        at=-kk * jnp.exp(gc - g),
        bt=(kk * a) * e_neg, kt=k2 * e_neg, rt=xr * jnp.exp(gc),
        bh=(kk * a) * e_end, kh=k2 * e_end,
        gam=jnp.exp(g_last))

    chains = [(b, p) for b in range(nb) for p in range(npair)]

    def part(ch, name):
        b, p = ch
        return prep[name][b * c:(b + 1) * c, p * LANES:(p + 1) * LANES]

    a_ak, a_row, vv, ak, pinv = {}, {}, {}, {}, {}
    for ch in chains:
        lhs = jnp.concatenate([part(ch, 'at'), part(ch, 'rt')], axis=0)
        rhs = jnp.concatenate([halves(part(ch, 'bt')), halves(part(ch, 'kt'))], axis=0)
        aa = jnp.where(tri, _bdot_nt(lhs, rhs), 0.0)
        a_ak[ch] = aa[:c, 2 * c:].astype(BF16)
        a_row[ch] = aa[c:, :].astype(BF16)
        abd = halves(aa[:c, :2 * c])
        pinv[ch] = eye2 + abd
        ak[ch] = abd
        vv[ch] = halves(part(ch, 'xv')).astype(BF16)
    nlev = int(math.log2(c))
    for lev in range(nlev):
        for ch in chains:
            akb = ak[ch].astype(BF16)
            if lev == 0:
                ak[ch] = _d(akb, akb)
            elif lev < nlev - 1:
                out = _d(akb, jnp.concatenate([akb, pinv[ch].astype(BF16)], axis=1))
                ak[ch] = out[:, :2 * c]
                pinv[ch] = pinv[ch] + out[:, 2 * c:]
            else:
                pinv[ch] = pinv[ch] + _d(akb, pinv[ch].astype(BF16))
    x2 = {ch: _d(a_ak[ch], vv[ch]) for ch in chains}

    sts = {ch: st_ref[ch[0] * npair + ch[1]] for ch in chains}
    xs = {ch: _d_nt(jnp.concatenate([part(ch, 'at'), part(ch, 'rt')], axis=0).astype(BF16),
                    sts[ch].astype(BF16)) for ch in chains}
    us = {}
    for ch in chains:
        u2 = _d(pinv[ch].astype(BF16), halves(xs[ch][:c] + x2[ch]).astype(BF16))
        us[ch] = u2[:c] + u2[c:]
    ys = {}
    for ch in chains:
        b, p = ch
        u = us[ch]
        ys[ch] = xs[ch][c:] + _d(a_row[ch], jnp.concatenate([halves(u).astype(BF16), vv[ch]], axis=0))
        upd = _d_tn(jnp.concatenate([u, part(ch, 'xv')], axis=0).astype(BF16),
                    jnp.concatenate([part(ch, 'bh'), part(ch, 'kh')], axis=0).astype(BF16))
        st_ref[b * npair + p] = sts[ch] * part(ch, 'gam')[0:1, :] + jnp.where(bd_p, upd, 0.0)

    inv_n = 1.0 / RW_HEAD
    y = jnp.concatenate([jnp.concatenate([ys[(b, p)] for p in range(npair)], axis=1)
                         for b in range(nb)], axis=0)
    mean = _segsum(y, ones_bd) * inv_n
    yc = y - mean
    var = _segsum(yc * yc, ones_bd) * inv_n
    yn = yc * lax.rsqrt(var + RW_GN_EPS) * vec(_RV_GNG) + vec(_RV_GNB)
    bonus = _segsum(xr * k2 * vec(_RV_RK), ones_bd) * xv
    out = (yn + bonus) * gate
    for b in range(nb):
        o_ref[b] = out[b * c:(b + 1) * c, :]


def _rwkv(p0, vec, mulo, w2p, a2p, g2):
    batch, seq, _ = p0.shape
    nc = seq // CHUNK
    c = CHUNK
    ones_bd = jnp.asarray(np.kron(np.eye(RW_WIDTH // RW_HEAD), np.ones((RW_HEAD, RW_HEAD))), BF16)

    def col(j, width):
        return pl.BlockSpec((batch, c, width), lambda i: (0, i, j))

    def full(shape):
        return pl.BlockSpec(shape, lambda i: (0,) * len(shape))

    return pl.pallas_call(
        _rwkv_body,
        grid=(nc,),
        in_specs=[col(0, 512), col(1, 512), col(2, 512), col(10, 256),
                  full((16, 512)), full((8, 256)), full((128, 512)), full((128, 512)),
                  full((128, 512)), full((512, 512))],
        out_specs=pl.BlockSpec((batch, c, 512), lambda i: (0, i, 0)),
        out_shape=jax.ShapeDtypeStruct((batch, seq, RW_WIDTH), F32),
        scratch_shapes=[pltpu.VMEM((batch, 8, 512), F32), pltpu.VMEM((batch, 8, 512), F32),
                        pltpu.VMEM((batch, 8, 512), F32), pltpu.VMEM((batch, 8, 256), F32),
                        pltpu.VMEM((batch * RW_WIDTH // LANES, LANES, LANES), F32)],
        compiler_params=_cparams(1),
        name="rwkv7",
    )(p0, p0, p0, p0, vec, mulo, w2p, a2p, g2, ones_bd)


def _gla_core(qs, ks, vs, gs, st_ref, heads_per_block):
    nb = len(qs)
    c = qs[0].shape[0]
    hpb = heads_per_block
    nblk = qs[0].shape[1] // LANES
    dk_shift = int(math.log2(LANES // hpb))
    row = _iota(qs[0].shape, 0)

    prep = []
    for q, k, g in zip(qs, ks, gs):
        b = _cumsum_rows(g)

        def brow(i, b=b):
            return jnp.broadcast_to(b[i:i + 1, :], b.shape)

        b15, b31, b47, blast = brow(15), brow(31), brow(47), brow(c - 1)
        ref_b = jnp.where(row < 32, b15, b47)
        ref_d = jnp.where(row < 16, 0.0, jnp.where(row < 32, b15, jnp.where(row < 48, b31, b47)))
        prep.append(dict(
            q_a=q * jnp.exp(jnp.minimum(b - b31, 0.0)), k_a=k * jnp.exp(jnp.minimum(b31 - b, 0.0)),
            q_b=q * jnp.exp(jnp.minimum(b - ref_b, 0.0)), k_b=k * jnp.exp(jnp.minimum(ref_b - b, 0.0)),
            q_d=q * jnp.exp(b - ref_d), k_d=k * jnp.exp(ref_d - b),
            q_i=q * jnp.exp(b), k_s=k * jnp.exp(blast - b), gam=jnp.exp(b[c - 1:c, :])))

    ri = _iota((hpb * c, c), 0) & (c - 1)
    ci = _iota((hpb * c, c), 1)
    mask_a = (ri >= 32) & (ci < 32)
    mask_b = ((ri >> 5) == (ci >> 5)) & (((ri >> 4) & 1) == 1) & (((ci >> 4) & 1) == 0)
    mask_d = ((ri >> 4) == (ci >> 4)) & (ri >= ci)
    lane = _iota((c, LANES), 1)
    bd = (_iota((hpb * LANES, LANES), 0) >> 7) == (_iota((hpb * LANES, LANES), 1) >> dk_shift)

    def heads_rows(x):
        if hpb == 1:
            return x
        return jnp.concatenate([jnp.where((lane >> dk_shift) == h, x, 0.0) for h in range(hpb)], axis=0)

    chains = [(bi, blk) for bi in range(nb) for blk in range(nblk)]

    def part(ch, name):
        bi, blk = ch
        return prep[bi][name][:, blk * LANES:(blk + 1) * LANES]

    v_p = {ch: vs[ch[0]][:, ch[1] * hpb * LANES:(ch[1] + 1) * hpb * LANES].astype(BF16) for ch in chains}
    probs = {}
    for ch in chains:
        s_a = _bdot_nt(heads_rows(part(ch, 'q_a')), part(ch, 'k_a'))
        s_b = _bdot_nt(heads_rows(part(ch, 'q_b')), part(ch, 'k_b'))
        s_d = _bdot_nt(heads_rows(part(ch, 'q_d')), part(ch, 'k_d'))
        probs[ch] = (jnp.where(mask_a, s_a, 0.0) + jnp.where(mask_b, s_b, 0.0)
                     + jnp.where(mask_d, s_d, 0.0)).astype(BF16)
    outs = {}
    for ch in chains:
        pv = _d(probs[ch], v_p[ch])
        o = pv[:c]
        for h in range(1, hpb):
            o = jnp.where((_iota(o.shape, 1) >> 7) == h, pv[h * c:(h + 1) * c], o)
        si = ch[0] * nblk + ch[1]
        sp = st_ref[si]
        outs[ch] = o + _bdot_nt(part(ch, 'q_i'), sp)
        upd = _d_tn(v_p[ch], part(ch, 'k_s').astype(BF16))
        st_ref[si] = sp * part(ch, 'gam') + jnp.where(bd, upd, 0.0)
    return [jnp.concatenate([outs[(bi, blk)] for blk in range(nblk)], axis=1) for bi in range(nb)]


def _gated_rmsnorm(o, gate, norm_g):
    nh = o.shape[1] // LANES
    outs = []
    for h in range(nh):
        sl = slice(h * LANES, (h + 1) * LANES)
        oh = o[:, sl]
        ms = jnp.mean(oh * oh, axis=-1, keepdims=True)
        outs.append(oh * lax.rsqrt(ms + NORM_EPS) * norm_g * _silu(gate[:, sl]))
    return jnp.concatenate(outs, axis=1)


def _gla_body(q_ref, k_ref, v_ref, gate_ref, gk_ref, w2_ref, vec_ref, ng_ref, o_ref, st_ref):
    nb = q_ref.shape[0]

    @pl.when(pl.program_id(0) == 0)
    def _():
        st_ref[...] = jnp.zeros_like(st_ref)

    gs = [-_softplus(-(_hdot(gk_ref[b], w2_ref[...]) + vec_ref[0:1, :])) * (1.0 / GLA_GATE_TAU)
          for b in range(nb)]
    qs = [q_ref[b] * (GLA_DK ** -0.5) for b in range(nb)]
    os_ = _gla_core(qs, [k_ref[b] for b in range(nb)], [v_ref[b] for b in range(nb)], gs, st_ref, 2)
    for b in range(nb):
        o_ref[b] = _gated_rmsnorm(os_[b], gate_ref[b], ng_ref[0:1, :])


def _seq_specs(batch, c):
    def col(j, width):
        return pl.BlockSpec((batch, c, width), lambda i: (0, i, j))

    def full(shape):
        return pl.BlockSpec(shape, lambda i: (0,) * len(shape))

    return col, full


def _gla(p0, gk_w2p, gk_b, norm_g):
    batch, seq, _ = p0.shape
    c = CHUNK
    col, full = _seq_specs(batch, c)
    return pl.pallas_call(
        _gla_body,
        grid=(seq // c,),
        in_specs=[col(11, 256), col(12, 256), col(3, 512), col(4, 512), col(26, 128),
                  full((128, 256)), full((8, 256)), full((8, 128))],
        out_specs=pl.BlockSpec((batch, c, 512), lambda i: (0, i, 0)),
        out_shape=jax.ShapeDtypeStruct((batch, seq, 512), F32),
        scratch_shapes=[pltpu.VMEM((batch * 2, 2 * LANES, LANES), F32)],
        compiler_params=_cparams(1),
        name="gla",
    )(p0, p0, p0, p0, p0, gk_w2p, gk_b, norm_g)


def _hgrn_body(q_ref, f_ref, i_ref, gate_ref, lb_ref, ng_ref, o_ref, st_ref):
    nb = q_ref.shape[0]

    @pl.when(pl.program_id(0) == 0)
    def _():
        st_ref[...] = jnp.zeros_like(st_ref)

    lb = lb_ref[0:1, :]
    qs, ks, gs = [], [], []
    for b in range(nb):
        f = f_ref[b]
        gs.append(jnp.log(lb + (1.0 - lb) * _sigmoid(f)))
        ks.append((1.0 - lb) * _sigmoid(-f))
        qs.append(_silu(q_ref[b]))
    os_ = _gla_core(qs, ks, [i_ref[b] for b in range(nb)], gs, st_ref, 1)
    for b in range(nb):
        o_ref[b] = _gated_rmsnorm(os_[b], gate_ref[b], ng_ref[0:1, :])


def _hgrn(p1, lb, norm_g):
    batch, seq, _ = p1.shape
    c = CHUNK
    col, full = _seq_specs(batch, c)
    return pl.pallas_call(
        _hgrn_body,
        grid=(seq // c,),
        in_specs=[col(1, 512), col(2, 512), col(3, 512), col(4, 512), full((8, 512)), full((8, 128))],
        out_specs=pl.BlockSpec((batch, c, 512), lambda i: (0, i, 0)),
        out_shape=jax.ShapeDtypeStruct((batch, seq, 512), F32),
        scratch_shapes=[pltpu.VMEM((batch * HG_HEADS, LANES, LANES), F32)],
        compiler_params=_cparams(1),
        name="hgrn2",
    )(p1, p1, p1, p1, lb, norm_g)


def _s5_body(u_ref, dcat_ref, bm_ref, e_ref, pw_ref, o_ref, toep_ref):
    seq = u_ref.shape[0]
    rows = seq // S5_CHUNK

    @pl.when(pl.program_id(1) == 0)
    def _():
        for tk in range(S5_CHUNK):
            lo = (S5_CHUNK - 1 - tk) * LANES
            toep_ref[tk * LANES:(tk + 1) * LANES, :] = dcat_ref[0, :, lo:lo + S5_CHUNK * LANES]

    u = jnp.concatenate([u_ref[pl.ds(tk, rows, stride=S5_CHUNK), :] for tk in range(S5_CHUNK)],
                        axis=1).astype(BF16)
    s = _d(u, bm_ref[0])
    rin = _iota(s.shape, 0)
    pw = pw_ref[0]

    def swap_halves(x):
        return jnp.concatenate([pltpu.roll(x[:, k * LANES:(k + 1) * LANES], S5_STATE, 1)
                                for k in range(x.shape[1] // LANES)], axis=1)

    h = s
    for lev in range(int(math.log2(rows))):
        sh = 1 << lev
        hs = jnp.where(rin >= sh, pltpu.roll(h, sh, 0), 0.0)
        h = h + pw[2 * lev:2 * lev + 1, :] * hs + pw[2 * lev + 1:2 * lev + 2, :] * swap_halves(hs)
    hprev = jnp.where(rin >= 1, pltpu.roll(h, 1, 0), 0.0)
    y = _d(u, toep_ref[...]) + _bdot(hprev, e_ref[0])
    for tk in range(S5_CHUNK):
        o_ref[pl.ds(tk, rows, stride=S5_CHUNK), :] = y[:, tk * LANES:(tk + 1) * LANES]


def _s5_scan(p1, dcat, bmat, emat, pw):
    batch, seq, _ = p1.shape
    nt, width, nstate = bmat.shape

    def table(shape):
        return pl.BlockSpec((1,) + shape, lambda j, b: (j, 0, 0), pipeline_mode=pl.Buffered(1))

    return pl.pallas_call(
        _s5_body,
        grid=(nt, batch),
        in_specs=[pl.BlockSpec((None, seq, LANES), lambda j, b: (b, 0, j)),
                  table(dcat.shape[1:]), table((width, nstate)), table((nstate, width)),
                  table((S5_SCAN_ROWS, nstate))],
        out_specs=pl.BlockSpec((None, seq, LANES), lambda j, b: (b, 0, j)),
        out_shape=jax.ShapeDtypeStruct((batch, seq, nt * LANES), F32),
        scratch_shapes=[pltpu.VMEM((width, width), BF16)],
        compiler_params=_cparams(2),
        name="s5_scan",
    )(p1, dcat, bmat, emat, pw)


def _s5_post_body(y_ref, u_ref, vec_ref, w_ref, o_ref):
    y = y_ref[...] + vec_ref[0:1, :] * u_ref[...]
    y = 0.5 * y * (1.0 + jnp.tanh(math.sqrt(2.0 / math.pi) * (y + 0.044715 * (y * y * y))))
    o_ref[...] = y * _sigmoid(_bdot(y, w_ref[...]) + vec_ref[1:2, :])


def _s5_post(y_ssm, p1, vec, glu_w, tm):
    t = y_ssm.shape[0]
    return pl.pallas_call(
        _s5_post_body,
        grid=(t // tm,),
        in_specs=[pl.BlockSpec((tm, 512), lambda i: (i, 0)),
                  pl.BlockSpec((tm, 512), lambda i: (i, 0)),
                  pl.BlockSpec((8, 512), lambda i: (0, 0)),
                  pl.BlockSpec((512, 512), lambda i: (0, 0))],
        out_specs=pl.BlockSpec((tm, 512), lambda i: (i, 0)),
        out_shape=jax.ShapeDtypeStruct((t, 512), F32),
        compiler_params=_cparams(1),
        name="s5_post",
    )(y_ssm, p1, vec, glu_w)


def _s5_tables(a_re, a_im, log_dt, b_re, b_im, c_re, c_im, rows_per_seq):
    c = S5_CHUNK
    lam_re = jnp.minimum(a_re, -1e-4)
    lam_im = a_im
    dt = jnp.exp(log_dt)[:, None]
    mag = jnp.exp(lam_re * dt)
    abar_re = mag * jnp.cos(lam_im * dt)
    abar_im = mag * jnp.sin(lam_im * dt)
    den = lam_re * lam_re + lam_im * lam_im
    num_re = abar_re - 1.0
    z_re = (num_re * lam_re + abar_im * lam_im) / den
    z_im = (abar_im * lam_re - num_re * lam_im) / den

    def power(n):
        n = jnp.asarray(n, F32)[..., None, None]
        m = jnp.exp(n * (lam_re * dt))
        return m * jnp.cos(n * (lam_im * dt)), m * jnp.sin(n * (lam_im * dt))

    def cmul(ar, ai, br, bi):
        return ar * br - ai * bi, ar * bi + ai * br

    tau = jnp.arange(c)
    p_re, p_im = power(tau)
    zb_re, zb_im = cmul(z_re[..., None], z_im[..., None], b_re, b_im)
    cp_re, cp_im = cmul(c_re[None], c_im[None], p_re[:, :, None, :], p_im[:, :, None, :])
    hi = lax.Precision.HIGHEST
    kern = (jnp.einsum('tgon,gni->gtoi', cp_re, zb_re, precision=hi)
            - jnp.einsum('tgon,gni->gtoi', cp_im, zb_im, precision=hi))
    q_re, q_im = power(c - 1 - tau)
    bm_re, bm_im = cmul(q_re[..., None], q_im[..., None], zb_re[None], zb_im[None])
    bmat = jnp.concatenate([jnp.transpose(bm_re, (1, 0, 3, 2)), jnp.transpose(bm_im, (1, 0, 3, 2))],
                           axis=-1).reshape(S5_GROUPS, c * S5_GROUP, 2 * S5_STATE)
    r_re, r_im = power(tau + 1)
    e_re, e_im = cmul(c_re[None], c_im[None], r_re[:, :, None, :], r_im[:, :, None, :])
    emat = jnp.concatenate([jnp.transpose(e_re, (1, 3, 0, 2)), -jnp.transpose(e_im, (1, 3, 0, 2))],
                           axis=1).reshape(S5_GROUPS, 2 * S5_STATE, c * S5_GROUP)
    nlev = int(math.log2(rows_per_seq))
    assert 2 * nlev <= S5_SCAN_ROWS
    s_re, s_im = power(c * (2 ** jnp.arange(nlev)))
    pw = jnp.zeros((S5_GROUPS, S5_SCAN_ROWS, 2 * S5_STATE), F32)
    pw = pw.at[:, 0:2 * nlev:2, :].set(jnp.transpose(jnp.concatenate([s_re, s_re], -1), (1, 0, 2)))
    pw = pw.at[:, 1:2 * nlev:2, :].set(jnp.transpose(jnp.concatenate([-s_im, s_im], -1), (1, 0, 2)))
    nt, gl = S5_GROUPS // S5_TILE_GROUPS, S5_TILE_GROUPS
    eye = jnp.eye(gl, dtype=F32)
    spread = jnp.tile(jnp.eye(S5_GROUP, dtype=F32), (1, gl))
    lane_group = jnp.arange(LANES) // S5_GROUP
    kern5 = jnp.transpose(kern.reshape(nt, gl, c, S5_GROUP, S5_GROUP), (0, 2, 1, 4, 3))
    dblk = jnp.einsum('jtgio,op->jtgip', kern5, spread, precision=hi)
    dblk = jnp.where((jnp.arange(gl)[:, None, None] == lane_group[None, None, :]), dblk, 0.0)
    dblk = dblk.reshape(nt, c, LANES, LANES)
    dcat = jnp.transpose(dblk, (0, 2, 1, 3)).reshape(nt, LANES, c * LANES)
    dcat = jnp.pad(dcat, ((0, 0), (0, 0), ((c - 1) * LANES, 0)))
    bmat5 = jnp.transpose(bmat.reshape(nt, gl, c, S5_GROUP, 2 * S5_STATE), (0, 2, 1, 3, 4))
    bmat_t = (bmat5[:, :, :, :, None, :] * eye[None, None, :, None, :, None])
    bmat_t = bmat_t.reshape(nt, c * LANES, gl * 2 * S5_STATE)
    emat5 = jnp.transpose(emat.reshape(nt, gl, 2 * S5_STATE, c, S5_GROUP), (0, 3, 1, 2, 4))
    eblk = jnp.einsum('jtgno,op->jtgnp', emat5, spread, precision=hi)
    eblk = jnp.where((jnp.arange(gl)[:, None, None] == lane_group[None, None, :]), eblk, 0.0)
    emat_t = jnp.transpose(eblk.reshape(nt, c, gl * 2 * S5_STATE, LANES), (0, 2, 1, 3))
    emat_t = emat_t.reshape(nt, gl * 2 * S5_STATE, c * LANES)
    pw_t = jnp.transpose(pw.reshape(nt, gl, S5_SCAN_ROWS, 2 * S5_STATE), (0, 2, 1, 3))
    pw_t = pw_t.reshape(nt, S5_SCAN_ROWS, gl * 2 * S5_STATE)
    return dcat.astype(BF16), bmat_t.astype(BF16), emat_t.astype(BF16), pw_t


def _proj_route_body(ya_ref, yb_ref, x_ref, w_ref, ln_ref, wr_ref, br_ref,
                     h_ref, route_ref, cnt_ref, carry_ref):
    tm = ya_ref.shape[0]
    half = w_ref.shape[0] // 2

    @pl.when(pl.program_id(0) == 0)
    def _():
        carry_ref[...] = jnp.zeros_like(carry_ref)

    mix = _bdot(ya_ref[...], w_ref[:half, :]) + _bdot(yb_ref[...], w_ref[half:, :])
    h = _layer_norm(DN_ALPHA * x_ref[...] + mix, ln_ref[0:1, :], ln_ref[1:2, :])
    h_ref[...] = h

    logits = _hdot(h, wr_ref[...]) + br_ref[0:1, :]
    lane = _iota((tm, LANES), 1).astype(F32)
    neg = -jnp.inf

    def softmax_masked(mask):
        xm = jnp.where(mask, logits, neg)
        m = jnp.max(xm, axis=-1, keepdims=True)
        e = jnp.exp(xm - m)
        return e / jnp.sum(e, axis=-1, keepdims=True)

    def top1(pm):
        m = jnp.max(pm, axis=-1, keepdims=True)
        idx = jnp.min(jnp.where(pm == m, lane, float(LANES)), axis=-1, keepdims=True)
        return m, idx

    coarse = jnp.where(lane < MOE_GROUPS, softmax_masked(lane < MOE_GROUPS), -1.0)
    p_grp, grp = top1(coarse)
    lo = MOE_GROUPS + MOE_PER_GROUP * grp
    fmask = (lane >= lo) & (lane < lo + MOE_PER_GROUP)
    fine = jnp.where(fmask, softmax_masked(fmask), -1.0)
    p1, j1 = top1(fine)
    p2, j2 = top1(jnp.where(lane == j1, -1.0, fine))
    denom = p1 + p2
    g1 = p_grp * (p1 / denom)
    g2 = p_grp * (p2 / denom)
    e1 = j1 - MOE_GROUPS
    e2 = j2 - MOE_GROUPS

    oh1 = jnp.where(lane == e1, 1.0, 0.0)
    oh2 = jnp.where(lane == e2, 1.0, 0.0)
    cnt = oh1 + oh2
    strict = jnp.where(_iota((tm, tm), 0) > _iota((tm, tm), 1), 1.0, 0.0).astype(BF16)
    before = _d(strict, cnt.astype(BF16)) + carry_ref[0:1, :]
    r1 = jnp.sum(oh1 * before, axis=-1, keepdims=True)
    r2 = jnp.sum(oh2 * before, axis=-1, keepdims=True)
    carry_ref[0:1, :] = carry_ref[0:1, :] + jnp.sum(cnt, axis=0, keepdims=True)
    cnt_ref[...] = carry_ref[...]

    out = jnp.where(lane == 0, e1, 0.0)
    out = jnp.where(lane == 1, e2, out)
    out = jnp.where(lane == 2, r1, out)
    out = jnp.where(lane == 3, r2, out)
    out = jnp.where(lane == 4, g1, out)
    out = jnp.where(lane == 5, g2, out)
    route_ref[...] = out[:, :8]


def _proj_route(ya, yb, resid, w_out, ln, wr, br, tm):
    t, d = resid.shape
    return pl.pallas_call(
        _proj_route_body,
        grid=(t // tm,),
        in_specs=[pl.BlockSpec((tm, 512), lambda i: (i, 0)),
                  pl.BlockSpec((tm, 512), lambda i: (i, 0)),
                  pl.BlockSpec((tm, d), lambda i: (i, 0)),
                  pl.BlockSpec((d, d), lambda i: (0, 0)),
                  pl.BlockSpec((8, d), lambda i: (0, 0)),
                  pl.BlockSpec((d, LANES), lambda i: (0, 0)),
                  pl.BlockSpec((8, LANES), lambda i: (0, 0))],
        out_specs=[pl.BlockSpec((tm, d), lambda i: (i, 0)),
                   pl.BlockSpec((tm, 8), lambda i: (i, 0)),
                   pl.BlockSpec((8, LANES), lambda i: (0, 0))],
        out_shape=[jax.ShapeDtypeStruct((t, d), F32),
                   jax.ShapeDtypeStruct((t, 8), F32),
                   jax.ShapeDtypeStruct((8, LANES), F32)],
        scratch_shapes=[pltpu.VMEM((8, LANES), F32)],
        compiler_params=_cparams(1),
        name="proj_ln_route",
    )(ya, yb, resid, w_out, ln, wr, br)


def _row_copy(src_ref, src_row, dst_ref, dst_row, sem):
    return pltpu.make_async_copy(src_ref.at[pl.ds(src_row, 1)], dst_ref.at[pl.ds(dst_row, 1)], sem)


def _start_all(copies):
    for n, cp in enumerate(copies):
        cp.start(priority=n % 2)


def _dispatch_body(pad_start_ref, pad_len_ref, used_rows_ref, dest_ref, h_ref, rows_ref, hbuf_ref, zbuf_ref,
                   lsem, ssem, zsem):
    i = pl.program_id(0)
    last = pl.num_programs(0) - 1
    tm = hbuf_ref.shape[1]
    slot = i % 2

    def load(tile, s):
        return pltpu.make_async_copy(h_ref.at[pl.ds(tile * tm, tm)], hbuf_ref.at[s], lsem.at[s])

    def scatters(s):
        return [_row_copy(hbuf_ref.at[s], r, rows_ref, dest_ref[0, 0, 2 * r + q], ssem.at[s])
                for r in range(tm) for q in range(2)]

    @pl.when(i == 0)
    def _():
        load(0, 0).start()

    @pl.when(i >= 1)
    def _():
        for cp in scatters(1 - slot):
            cp.wait()

    @pl.when(i < last)
    def _():
        load(i + 1, 1 - slot).start()

    load(i, slot).wait()
    _start_all(scatters(slot))

    @pl.when(i == last)
    def _():
        for cp in scatters(slot):
            cp.wait()
        zbuf_ref[...] = jnp.zeros_like(zbuf_ref)
        sub = 8
        half = zbuf_ref.shape[0]

        def pad_fills():
            out = []
            for e in range(N_EXPERTS):
                start = pad_start_ref[e]
                npad = pad_len_ref[e]
                end = start + npad
                run = half
                while run >= sub:
                    end = end - (npad & run)
                    dst = rows_ref.at[pl.ds(pl.multiple_of(end, run), run)]
                    out.append(((npad & run) != 0,
                                pltpu.make_async_copy(zbuf_ref.at[pl.ds(0, run)], dst, zsem)))
                    run //= 2
                for k in range(sub - 1):
                    out.append((k < (npad & (sub - 1)), _row_copy(zbuf_ref, 0, rows_ref, start + k, zsem)))
            return out

        for pred, cp in pad_fills():
            pl.when(pred)(cp.start)
        for pred, cp in pad_fills():
            pl.when(pred)(cp.wait)

        def tail(b):
            return pltpu.make_async_copy(
                zbuf_ref, rows_ref.at[pl.ds(pl.multiple_of(b * half, half), half)], zsem)

        first_free = used_rows_ref[0] // half
        n_half = rows_ref.shape[0] // half
        lax.fori_loop(first_free, n_half, lambda b, c: (tail(b).start(), c)[1], 0)
        lax.fori_loop(first_free, n_half, lambda b, c: (tail(b).wait(), c)[1], 0)


def _dispatch(pad_start, pad_len, used_rows, dest3, h, n_rows):
    t, d = h.shape
    nt, _, tm2 = dest3.shape
    tm = tm2 // 2
    grid_spec = pltpu.PrefetchScalarGridSpec(
        num_scalar_prefetch=3,
        grid=(nt,),
        in_specs=[pl.BlockSpec((1, 1, tm2), lambda i, *_: (i, 0, 0), memory_space=pltpu.SMEM),
                  pl.BlockSpec(memory_space=pl.ANY)],
        out_specs=pl.BlockSpec(memory_space=pl.ANY),
        scratch_shapes=[pltpu.VMEM((2, tm, d), F32), pltpu.VMEM((MOE_BLOCK // 2, d), F32),
                        pltpu.SemaphoreType.DMA((2,)), pltpu.SemaphoreType.DMA((2,)),
                        pltpu.SemaphoreType.DMA],
    )
    return pl.pallas_call(
        _dispatch_body,
        grid_spec=grid_spec,
        out_shape=jax.ShapeDtypeStruct((n_rows, d), F32),
        compiler_params=_cparams(1),
        name="moe_dispatch",
    )(pad_start, pad_len, used_rows, dest3, h)


def _expert_body(first_ref, nblk_ref, nused_ref, x_ref, w1_ref, w3_ref, w2_ref, y_ref,
                 xbuf_ref, ybuf_ref, w1b_ref, w3b_ref, w2b_ref, ypend_ref, ycnt_ref, xsem, ysem):
    e = pl.program_id(0)
    row0 = first_ref[e] * MOE_BLOCK
    nblk = nblk_ref[e]
    big = xbuf_ref.shape[1]
    per_big = big // MOE_BLOCK

    @pl.when(e == 0)
    def _():
        ypend_ref[0] = 0
        ypend_ref[1] = 0
        ycnt_ref[0] = 0

    def drain(slot):
        for size in (big, MOE_BLOCK):
            @pl.when(ypend_ref[slot] == size)
            def _(size=size):
                pltpu.make_async_copy(ybuf_ref.at[slot, pl.ds(0, size)], y_ref.at[pl.ds(0, size)],
                                      ysem.at[slot]).wait()
        ypend_ref[slot] = 0

    def cast_weights():
        w1b_ref[...] = w1_ref[0].astype(BF16)
        w3b_ref[...] = w3_ref[0].astype(BF16)
        w2b_ref[...] = w2_ref[0].astype(BF16)

    def stream(start_row, count, size, casts_first):
        def x_copy(j, s):
            src = x_ref.at[pl.ds(pl.multiple_of(start_row + j * size, MOE_BLOCK), size)]
            return pltpu.make_async_copy(src, xbuf_ref.at[s, pl.ds(0, size)], xsem.at[s])

        def y_copy(j, s):
            dst = y_ref.at[pl.ds(pl.multiple_of(start_row + j * size, MOE_BLOCK), size)]
            return pltpu.make_async_copy(ybuf_ref.at[s, pl.ds(0, size)], dst, ysem.at[s])

        @pl.when(count > 0)
        def _():
            x_copy(0, 0).start(priority=1)
            pl.when(casts_first)(cast_weights)

            def chunk(j, carry):
                s = j % 2

                @pl.when(j + 1 < count)
                def _():
                    x_copy(j + 1, 1 - s).start(priority=1)

                x_copy(j, s).wait()
                xb = xbuf_ref[s, pl.ds(0, size), :].astype(BF16)
                hid = _silu(_d(xb, w1b_ref[...])) * _d(xb, w3b_ref[...])
                ys = ycnt_ref[0] % 2
                drain(ys)
                ybuf_ref[ys, pl.ds(0, size), :] = _d(hid.astype(BF16), w2b_ref[...])
                y_copy(j, ys).start()
                ypend_ref[ys] = size
                ycnt_ref[0] = ycnt_ref[0] + 1
                return carry

            lax.fori_loop(0, count, chunk, 0)

    @pl.when(nblk > 0)
    def _():
        nbig = nblk // per_big
        stream(row0, nbig, big, nbig > 0)
        stream(row0 + nbig * big, nblk - nbig * per_big, MOE_BLOCK, nbig == 0)

    @pl.when(e == pl.num_programs(0) - 1)
    def _():
        drain(0)
        drain(1)
        ybuf_ref[0, pl.ds(0, MOE_BLOCK), :] = jnp.zeros((MOE_BLOCK, ybuf_ref.shape[2]), F32)
        nused = nused_ref[0]
        ntot = y_ref.shape[0] // MOE_BLOCK

        def fill(b, carry):
            cp = pltpu.make_async_copy(ybuf_ref.at[0, pl.ds(0, MOE_BLOCK)],
                                       y_ref.at[pl.ds(b * MOE_BLOCK, MOE_BLOCK)], ysem.at[0])
            cp.start()
            cp.wait()
            return carry

        lax.fori_loop(nused, ntot, fill, 0)


def _experts(first_blk, nblk, nused, x_rows, w1, w3, w2, layer):
    r, d = x_rows.shape
    hid = w1.shape[-1]
    grid_spec = pltpu.PrefetchScalarGridSpec(
        num_scalar_prefetch=3,
        grid=(N_EXPERTS,),
        in_specs=[pl.BlockSpec(memory_space=pl.ANY),
                  pl.BlockSpec((None, 1, d, hid), lambda e, *_: (layer, e, 0, 0)),
                  pl.BlockSpec((None, 1, d, hid), lambda e, *_: (layer, e, 0, 0)),
                  pl.BlockSpec((None, 1, hid, d), lambda e, *_: (layer, e, 0, 0))],
        out_specs=pl.BlockSpec(memory_space=pl.ANY),
        scratch_shapes=[pltpu.VMEM((2, EXPERT_CHUNK, d), F32), pltpu.VMEM((2, EXPERT_CHUNK, d), F32),
                        pltpu.VMEM((d, hid), BF16), pltpu.VMEM((d, hid), BF16), pltpu.VMEM((hid, d), BF16),
                        pltpu.SMEM((2,), jnp.int32), pltpu.SMEM((1,), jnp.int32),
                        pltpu.SemaphoreType.DMA((2,)), pltpu.SemaphoreType.DMA((2,))],
    )
    return pl.pallas_call(
        _expert_body,
        grid_spec=grid_spec,
        out_shape=jax.ShapeDtypeStruct((r, d), F32),
        compiler_params=_cparams(1),
        name="moe_experts",
    )(first_blk, nblk, nused, x_rows, w1, w3, w2)


def _combine_body(dest_ref, dest_next_ref, gate_ref, h_ref, ln_ref, rows_ref, o_ref, buf_ref, sem):
    i = pl.program_id(0)
    last = pl.num_programs(0) - 1
    tm = h_ref.shape[0]

    def gather(dref, slot):
        return [_row_copy(rows_ref, dref[0, 0, 2 * r + s], buf_ref.at[slot, s], r, sem.at[slot])
                for r in range(tm) for s in range(2)]

    @pl.when(i == 0)
    def _():
        _start_all(gather(dest_ref, 0))

    _start_all(gather(dest_next_ref, (i + 1) % 2))
    slot = i % 2
    for cp in gather(dest_ref, slot):
        cp.wait()
    gate = gate_ref[...]
    y = gate[:, 4:5] * buf_ref[slot, 0] + gate[:, 5:6] * buf_ref[slot, 1]
    o_ref[...] = _layer_norm(DN_ALPHA * h_ref[...] + y, ln_ref[0:1, :], ln_ref[1:2, :])

    @pl.when(i == last)
    def _():
        for cp in gather(dest_next_ref, (i + 1) % 2):
            cp.wait()


def _combine(dest3, route, h, ln, y_rows, tm):
    t, d = h.shape
    nt = t // tm
    return pl.pallas_call(
        _combine_body,
        grid=(nt,),
        in_specs=[pl.BlockSpec((1, 1, 2 * tm), lambda i: (i, 0, 0), memory_space=pltpu.SMEM),
                  pl.BlockSpec((1, 1, 2 * tm), lambda i: (jnp.minimum(i + 1, nt - 1), 0, 0),
                               memory_space=pltpu.SMEM),
                  pl.BlockSpec((tm, 8), lambda i: (i, 0)),
                  pl.BlockSpec((tm, d), lambda i: (i, 0)),
                  pl.BlockSpec((8, d), lambda i: (0, 0)),
                  pl.BlockSpec(memory_space=pl.ANY)],
        out_specs=pl.BlockSpec((tm, d), lambda i: (i, 0)),
        out_shape=jax.ShapeDtypeStruct((t, d), F32),
        scratch_shapes=[pltpu.VMEM((2, 2, tm, d), F32), pltpu.SemaphoreType.DMA((2,))],
        compiler_params=_cparams(1),
        name="moe_combine_ln",
    )(dest3, dest3, route, h, ln, y_rows)


def _moe(h, route, counts, w1, w3, w2, layer, ln):
    t, d = h.shape
    tm = min(COMBINE_TILE, t)
    a = 2 * t
    expert = route[:, 0:2].astype(jnp.int32)
    rank = route[:, 2:4].astype(jnp.int32)
    cnt = counts[0, :N_EXPERTS].astype(jnp.int32)
    padded = (cnt + MOE_BLOCK - 1) // MOE_BLOCK * MOE_BLOCK
    pend = jnp.cumsum(padded)
    pstart = pend - padded
    sel = expert[..., None] == jnp.arange(N_EXPERTS, dtype=jnp.int32)
    dest = jnp.sum(jnp.where(sel, pstart, 0), axis=-1) + rank
    n_blocks = -(-a // MOE_BLOCK) + N_EXPERTS
    nused = (pend[-1:] // MOE_BLOCK).astype(jnp.int32)
    dest3 = dest.reshape(t // tm, 1, 2 * tm)
    x_rows = _dispatch((pstart + cnt).astype(jnp.int32), (padded - cnt).astype(jnp.int32),
                       pend[-1:].astype(jnp.int32), dest3, h, n_blocks * MOE_BLOCK)
    y_rows = _experts((pstart // MOE_BLOCK).astype(jnp.int32), (padded // MOE_BLOCK).astype(jnp.int32),
                      nused, x_rows, w1, w3, w2, layer)
    return _combine(dest3, route, h, ln, y_rows, tm)


def _pad_rows(x, rows):
    return jnp.zeros((rows,) + x.shape[1:], x.dtype).at[:x.shape[0]].set(x)


def _route_weights(wg, bg, we, be):
    d = wg.shape[0]
    wr = jnp.zeros((d, LANES), F32)
    wr = wr.at[:, :MOE_GROUPS].set(wg)
    wr = wr.at[:, MOE_GROUPS:MOE_GROUPS + N_EXPERTS].set(jnp.transpose(we, (1, 0, 2)).reshape(d, N_EXPERTS))
    br = jnp.zeros((8, LANES), F32)
    br = br.at[0, :MOE_GROUPS].set(bg)
    br = br.at[0, MOE_GROUPS:MOE_GROUPS + N_EXPERTS].set(be.reshape(N_EXPERTS))
    return wr, br


def kernel(x, ab_w_in, rw_mu, rw_w0, rw_w2, rw_a0, rw_a2, rw_g2, rw_k_k, rw_k_a, rw_r_k, rw_gn_g, rw_gn_b, gla_gk_w2, gla_gk_b, gla_norm_g, ab_w_out, cd_w_in, s5_a_re, s5_a_im, s5_log_dt, s5_b_re, s5_b_im, s5_c_re, s5_c_im, s5_d, s5_glu_w, s5_glu_b, hg_lb, hg_norm_g, cd_w_out, ln1_g, ln1_b, moe_wg, moe_bg, moe_we, moe_be, moe_w1, moe_w3, moe_w2, ln2_g, ln2_b):
    batch, seq, d = x.shape
    t = batch * seq
    assert d == D_MODEL and seq % CHUNK == 0
    rows_per_seq = seq // S5_CHUNK
    assert rows_per_seq & (rows_per_seq - 1) == 0, "S5 chunk scan assumes a power-of-two chunk count"
    tm = min(ROW_TILE, t)
    assert t % tm == 0
    xt = x.reshape(t, d)
    ln1 = [_pad_rows(jnp.stack([ln1_g[l], ln1_b[l]]), 8) for l in range(DEPTH)]
    ln2 = [_pad_rows(jnp.stack([ln2_g[l], ln2_b[l]]), 8) for l in range(DEPTH)]

    j = 0
    w = ab_w_in[j]
    r_, wl_, k_, v_, al_, gl_ = 0, 512, 576, 1088, 1600, 1664
    gq, gk, gv, glow, ggate = 1792, 2048, 2304, 2816, 2832
    w0cols = jnp.concatenate([
        w[:, r_:r_ + 512], w[:, k_:k_ + 512], w[:, v_:v_ + 512], w[:, gv:gv + 512], w[:, ggate:ggate + 512],
        w[:, wl_:wl_ + 64], w[:, al_:al_ + 64], w[:, gl_:gl_ + 128], w[:, gq:gq + 256], w[:, gk:gk + 256],
        w[:, glow:glow + 16], jnp.zeros((d, 112), F32)], axis=1).astype(BF16)
    p0 = _matmul(xt, w0cols, tm)

    mu = rw_mu[j]
    vec = _pad_rows(jnp.stack([mu[r_:r_ + 512], mu[k_:k_ + 512], mu[v_:v_ + 512], rw_w0[j], rw_a0[j],
                               rw_k_k[j], rw_k_a[j], rw_r_k[j].reshape(-1), rw_gn_g[j], rw_gn_b[j]]), 16)
    mulo = _pad_rows(jnp.concatenate([mu[wl_:wl_ + 64], mu[al_:al_ + 64], mu[gl_:gl_ + 128]])[None], 8)
    w2p = _pad_rows(rw_w2[j], 128)
    a2p = jnp.zeros((128, 512), F32).at[64:].set(rw_a2[j])
    p0 = p0.reshape(batch, seq, -1)
    y_rw = _rwkv(p0, vec, mulo, w2p, a2p, rw_g2[j]).reshape(t, RW_WIDTH)
    y_gla = _gla(p0, _pad_rows(gla_gk_w2[j], 128), _pad_rows(gla_gk_b[j][None], 8),
                 _pad_rows(gla_norm_g[j][None], 8)).reshape(t, 512)

    wr, br = _route_weights(moe_wg[0], moe_bg[0], moe_we[0], moe_be[0])
    h, route, counts = _proj_route(y_rw, y_gla, xt, ab_w_out[j].astype(BF16), ln1[0], wr, br, tm)
    h = _moe(h, route, counts, moe_w1, moe_w3, moe_w2, 0, ln2[0])

    p1 = _matmul(h, cd_w_in[j].astype(BF16), tm)
    lb_sm = jax.nn.softmax(hg_lb.astype(F32), axis=0)
    lower = (jnp.cumsum(lb_sm, axis=0) - lb_sm[0])[1]
    y_hg = _hgrn(p1.reshape(batch, seq, -1), _pad_rows(lower[None], 8),
                 _pad_rows(hg_norm_g[j][None], 8)).reshape(t, 512)

    toep, bmat, emat, pw = _s5_tables(s5_a_re[j], s5_a_im[j], s5_log_dt[j], s5_b_re[j], s5_b_im[j],
                                      s5_c_re[j], s5_c_im[j], rows_per_seq)
    y_ssm = _s5_scan(p1.reshape(batch, seq, -1), toep, bmat, emat, pw).reshape(t, 512)
    y_s5 = _s5_post(y_ssm, p1, _pad_rows(jnp.stack([s5_d[j], s5_glu_b[j]]), 8),
                    s5_glu_w[j].astype(BF16), tm)

    wr, br = _route_weights(moe_wg[1], moe_bg[1], moe_we[1], moe_be[1])
    h2, route, counts = _proj_route(y_s5, y_hg, h, cd_w_out[j].astype(BF16), ln1[1], wr, br, tm)
    out = _moe(h2, route, counts, moe_w1, moe_w3, moe_w2, 1, ln2[1])
    return out.reshape(batch, seq, d)
```

```python
import functools
import math

import numpy as np
import jax
import jax.numpy as jnp
from jax import lax
from jax.experimental import pallas as pl
from jax.experimental.pallas import tpu as pltpu

F32 = jnp.float32
BF16 = jnp.bfloat16

D_MODEL = 1024
DEPTH = 2
RW_HEAD = 64
RW_WIDTH = 512
RW_GN_EPS = 64e-5
GLA_HEADS = 4
GLA_DK = 64
GLA_DV = 128
GLA_GATE_TAU = 16.0
S5_GROUP = 16
S5_GROUPS = 32
S5_STATE = 64
HG_HEADS = 4
HG_DK = 128
CHUNK = 64
S5_CHUNK = 16
S5_SCAN_ROWS = 32
S5_TILE_GROUPS = 8
NORM_EPS = 1e-5
MOE_GROUPS = 4
MOE_PER_GROUP = 8
N_EXPERTS = 32
EXPERT_HIDDEN = 512
MOE_BLOCK = 128
COMBINE_TILE = 128
EXPERT_CHUNK = 512
ROW_TILE = 512
DN_ALPHA = (2.0 * DEPTH) ** 0.25

LANES = 128
VMEM_LIMIT = 56 * 1024 * 1024


def _cparams(n_axes=1):
    return pltpu.CompilerParams(dimension_semantics=("arbitrary",) * n_axes,
                                vmem_limit_bytes=VMEM_LIMIT)


def _d(a, b):
    return jnp.dot(a, b, preferred_element_type=F32)


def _d_nt(a, b):
    return lax.dot_general(a, b, (((1,), (1,)), ((), ())), preferred_element_type=F32)


def _d_tn(a, b):
    return lax.dot_general(a, b, (((0,), (0,)), ((), ())), preferred_element_type=F32)


def _split(a):
    hi = a.astype(BF16)
    lo = (a - hi.astype(F32)).astype(BF16)
    return hi, lo


def _split3(a):
    hi = a.astype(BF16)
    r1 = a - hi.astype(F32)
    mid = r1.astype(BF16)
    lo = (r1 - mid.astype(F32)).astype(BF16)
    return hi, mid, lo


def _bdot(a, b):
    return _d(a.astype(BF16), b.astype(BF16))


def _bdot_nt(a, b):
    return _d_nt(a.astype(BF16), b.astype(BF16))


def _bdot_tn(a, b):
    return _d_tn(a.astype(BF16), b.astype(BF16))


def _hdot_with(d, a, b):
    ah, al = _split(a)
    bh, bl = _split(b)
    return d(ah, bh) + (d(ah, bl) + d(al, bh))


def _hdot(a, b):
    return _hdot_with(_d, a, b)


def _hdot_nt(a, b):
    return _hdot_with(_d_nt, a, b)


def _hdot_tn(a, b):
    return _hdot_with(_d_tn, a, b)


def _xdot_r(e, a):
    ah, am, al = _split3(a)
    return _d(e, ah) + (_d(e, am) + _d(e, al))


def _iota(shape, dim):
    return lax.broadcasted_iota(jnp.int32, shape, dim)


def _softplus(x):
    return jnp.maximum(x, 0.0) + jnp.log1p(jnp.exp(-jnp.abs(x)))


def _sigmoid(x):
    return 1.0 / (1.0 + jnp.exp(-x))


def _silu(x):
    return x * _sigmoid(x)


def _segsum(a, e):
    m = a.shape[0]
    s = _d(jnp.concatenate(_split3(a), axis=0), e)
    return s[:m] + (s[m:2 * m] + s[2 * m:])


def _cumsum_rows(g, chunk=None):
    n = g.shape[0]
    chunk = n if chunk is None else chunk
    shift = int(math.log2(chunk))
    r, c = _iota((n, n), 0), _iota((n, n), 1)
    tril = jnp.where((r >= c) & ((r >> shift) == (c >> shift)), 1.0, 0.0).astype(BF16)
    return _xdot_r(tril, g)


def _shift_mix(x, prev_ref, b, mu):
    c = x.shape[0]
    rolled = pltpu.roll(x, 1, 0)
    prev = jnp.where(_iota(x.shape, 0) == 0, jnp.broadcast_to(prev_ref[b, 0:1, :], x.shape), rolled)
    prev_ref[b, 0:1, :] = x[c - 1:c, :]
    return x + mu * (prev - x)


def _layer_norm(x, g, b):
    mu = jnp.mean(x, axis=-1, keepdims=True)
    xc = x - mu
    var = jnp.mean(xc * xc, axis=-1, keepdims=True)
    return xc * lax.rsqrt(var + NORM_EPS) * g + b


def _mm_body(x_ref, w_ref, o_ref):
    o_ref[...] = _d(x_ref[...].astype(BF16), w_ref[...])


def _matmul(x, w_bf16, tm):
    m, k = x.shape
    n = w_bf16.shape[1]
    return pl.pallas_call(
        _mm_body,
        grid=(m // tm,),
        in_specs=[pl.BlockSpec((tm, k), lambda i: (i, 0)),
                  pl.BlockSpec((k, n), lambda i: (0, 0))],
        out_specs=pl.BlockSpec((tm, n), lambda i: (i, 0)),
        out_shape=jax.ShapeDtypeStruct((m, n), F32),
        compiler_params=_cparams(1),
        name="in_proj",
    )(x, w_bf16)


_RV_MU_R, _RV_MU_K, _RV_MU_V, _RV_W0, _RV_A0, _RV_KK, _RV_KA, _RV_RK, _RV_GNG, _RV_GNB = range(10)


def _rwkv_body(r_ref, k_ref, v_ref, lo_ref, vec_ref, mulo_ref, w2_ref, a2_ref, g2_ref, ones_ref,
               o_ref, pr_ref, pk_ref, pv_ref, plo_ref, st_ref):
    nb, c = r_ref.shape[0], r_ref.shape[1]
    npair = RW_WIDTH // LANES

    @pl.when(pl.program_id(0) == 0)
    def _():
        pr_ref[...] = jnp.zeros_like(pr_ref)
        pk_ref[...] = jnp.zeros_like(pk_ref)
        pv_ref[...] = jnp.zeros_like(pv_ref)
        plo_ref[...] = jnp.zeros_like(plo_ref)
        st_ref[...] = jnp.zeros_like(st_ref)

    def vec(i):
        return vec_ref[i:i + 1, :]

    ones_bd = ones_ref[...]
    lane = _iota((c, LANES), 1)
    m1 = lane < RW_HEAD
    row2 = _iota((2 * c, 4 * c), 0)
    col2 = _iota((2 * c, 4 * c), 1) & (c - 1)
    tri = ((row2 < c) & (row2 > col2)) | ((row2 >= c) & ((row2 - c) >= col2))
    eye2 = jnp.where(_iota((2 * c, 2 * c), 0) == _iota((2 * c, 2 * c), 1), 1.0, 0.0)
    bd_p = (_iota((LANES, LANES), 0) >> 6) == (_iota((LANES, LANES), 1) >> 6)

    def halves(x):
        return jnp.concatenate([jnp.where(m1, x, 0.0), jnp.where(m1, 0.0, x)], axis=0)

    def stacked(ref, prev_ref, mu):
        return jnp.concatenate([_shift_mix(ref[b], prev_ref, b, mu) for b in range(nb)], axis=0)

    xr = stacked(r_ref, pr_ref, vec(_RV_MU_R))
    xk = stacked(k_ref, pk_ref, vec(_RV_MU_K))
    xv = stacked(v_ref, pv_ref, vec(_RV_MU_V))
    xlo = stacked(lo_ref, plo_ref, mulo_ref[0:1, :])
    lo_a = xlo[:, :LANES]
    lo_g = xlo[:, LANES:]
    w = -_softplus(-(vec(_RV_W0) + _bdot(jnp.tanh(lo_a), w2_ref[...]))) - 0.5
    g = -jnp.exp(w)
    a = _sigmoid(vec(_RV_A0) + _bdot(lo_a, a2_ref[...]))
    gate = _bdot(_sigmoid(lo_g), g2_ref[...])
    kk = xk * vec(_RV_KK)
    kk = kk / jnp.maximum(jnp.sqrt(_segsum(kk * kk, ones_bd)), 1e-12)
    k2 = xk * (1.0 + (a - 1.0) * vec(_RV_KA))
    gc = _cumsum_rows(g, c)
    g_last = jnp.concatenate([jnp.broadcast_to(gc[(b + 1) * c - 1:(b + 1) * c, :], (c, gc.shape[1]))
                              for b in range(nb)], axis=0)
    e_neg = jnp.exp(-gc)
    e_end = jnp.exp(g_last - gc)
    prep = dict(
        xv=xv,
        at=-kk * jnp.exp(gc - g),
        bt=(kk * a) * e_neg, kt=k2 * e_neg, rt=xr * jnp.exp(gc),
        bh=(kk * a) * e_end, kh=k2 * e_end,
        gam=jnp.exp(g_last))

    chains = [(b, p) for b in range(nb) for p in range(npair)]

    def part(ch, name):
        b, p = ch
        return prep[name][b * c:(b + 1) * c, p * LANES:(p + 1) * LANES]

    a_ak, a_row, vv, ak, pinv = {}, {}, {}, {}, {}
    for ch in chains:
        lhs = jnp.concatenate([part(ch, 'at'), part(ch, 'rt')], axis=0)
        rhs = jnp.concatenate([halves(part(ch, 'bt')), halves(part(ch, 'kt'))], axis=0)
        aa = jnp.where(tri, _bdot_nt(lhs, rhs), 0.0)
        a_ak[ch] = aa[:c, 2 * c:].astype(BF16)
        a_row[ch] = aa[c:, :].astype(BF16)
        abd = halves(aa[:c, :2 * c])
        pinv[ch] = eye2 + abd
        ak[ch] = abd
        vv[ch] = halves(part(ch, 'xv')).astype(BF16)
    nlev = int(math.log2(c))
    for lev in range(nlev):
        for ch in chains:
            akb = ak[ch].astype(BF16)
            if lev == 0:
                ak[ch] = _d(akb, akb)
            elif lev < nlev - 1:
                out = _d(akb, jnp.concatenate([akb, pinv[ch].astype(BF16)], axis=1))
                ak[ch] = out[:, :2 * c]
                pinv[ch] = pinv[ch] + out[:, 2 * c:]
            else:
                pinv[ch] = pinv[ch] + _d(akb, pinv[ch].astype(BF16))
    x2 = {ch: _d(a_ak[ch], vv[ch]) for ch in chains}

    sts = {ch: st_ref[ch[0] * npair + ch[1]] for ch in chains}
    xs = {ch: _d_nt(jnp.concatenate([part(ch, 'at'), part(ch, 'rt')], axis=0).astype(BF16),
                    sts[ch].astype(BF16)) for ch in chains}
    us = {}
    for ch in chains:
        u2 = _d(pinv[ch].astype(BF16), halves(xs[ch][:c] + x2[ch]).astype(BF16))
        us[ch] = u2[:c] + u2[c:]
    ys = {}
    for ch in chains:
        b, p = ch
        u = us[ch]
        ys[ch] = xs[ch][c:] + _d(a_row[ch], jnp.concatenate([halves(u).astype(BF16), vv[ch]], axis=0))
        upd = _d_tn(jnp.concatenate([u, part(ch, 'xv')], axis=0).astype(BF16),
                    jnp.concatenate([part(ch, 'bh'), part(ch, 'kh')], axis=0).astype(BF16))
        st_ref[b * npair + p] = sts[ch] * part(ch, 'gam')[0:1, :] + jnp.where(bd_p, upd, 0.0)

    inv_n = 1.0 / RW_HEAD
    y = jnp.concatenate([jnp.concatenate([ys[(b, p)] for p in range(npair)], axis=1)
                         for b in range(nb)], axis=0)
    mean = _segsum(y, ones_bd) * inv_n
    yc = y - mean
    var = _segsum(yc * yc, ones_bd) * inv_n
    yn = yc * lax.rsqrt(var + RW_GN_EPS) * vec(_RV_GNG) + vec(_RV_GNB)
    bonus = _segsum(xr * k2 * vec(_RV_RK), ones_bd) * xv
    out = (yn + bonus) * gate
    for b in range(nb):
        o_ref[b] = out[b * c:(b + 1) * c, :]


def _rwkv(p0, vec, mulo, w2p, a2p, g2):
    batch, seq, _ = p0.shape
    nc = seq // CHUNK
    c = CHUNK
    ones_bd = jnp.asarray(np.kron(np.eye(RW_WIDTH // RW_HEAD), np.ones((RW_HEAD, RW_HEAD))), BF16)

    def col(j, width):
        return pl.BlockSpec((batch, c, width), lambda i: (0, i, j))

    def full(shape):
        return pl.BlockSpec(shape, lambda i: (0,) * len(shape))

    return pl.pallas_call(
        _rwkv_body,
        grid=(nc,),
        in_specs=[col(0, 512), col(1, 512), col(2, 512), col(10, 256),
                  full((16, 512)), full((8, 256)), full((128, 512)), full((128, 512)),
                  full((128, 512)), full((512, 512))],
        out_specs=pl.BlockSpec((batch, c, 512), lambda i: (0, i, 0)),
        out_shape=jax.ShapeDtypeStruct((batch, seq, RW_WIDTH), F32),
        scratch_shapes=[pltpu.VMEM((batch, 8, 512), F32), pltpu.VMEM((batch, 8, 512), F32),
                        pltpu.VMEM((batch, 8, 512), F32), pltpu.VMEM((batch, 8, 256), F32),
                        pltpu.VMEM((batch * RW_WIDTH // LANES, LANES, LANES), F32)],
        compiler_params=_cparams(1),
        name="rwkv7",
    )(p0, p0, p0, p0, vec, mulo, w2p, a2p, g2, ones_bd)


def _gla_core(qs, ks, vs, gs, st_ref, heads_per_block):
    nb = len(qs)
    c = qs[0].shape[0]
    hpb = heads_per_block
    nblk = qs[0].shape[1] // LANES
    dk_shift = int(math.log2(LANES // hpb))
    row = _iota(qs[0].shape, 0)

    prep = []
    for q, k, g in zip(qs, ks, gs):
        b = _cumsum_rows(g)

        def brow(i, b=b):
            return jnp.broadcast_to(b[i:i + 1, :], b.shape)

        b15, b31, b47, blast = brow(15), brow(31), brow(47), brow(c - 1)
        ref_b = jnp.where(row < 32, b15, b47)
        ref_d = jnp.where(row < 16, 0.0, jnp.where(row < 32, b15, jnp.where(row < 48, b31, b47)))
        prep.append(dict(
            q_a=q * jnp.exp(jnp.minimum(b - b31, 0.0)), k_a=k * jnp.exp(jnp.minimum(b31 - b, 0.0)),
            q_b=q * jnp.exp(jnp.minimum(b - ref_b, 0.0)), k_b=k * jnp.exp(jnp.minimum(ref_b - b, 0.0)),
            q_d=q * jnp.exp(b - ref_d), k_d=k * jnp.exp(ref_d - b),
            q_i=q * jnp.exp(b), k_s=k * jnp.exp(blast - b), gam=jnp.exp(b[c - 1:c, :])))

    ri = _iota((hpb * c, c), 0) & (c - 1)
    ci = _iota((hpb * c, c), 1)
    mask_a = (ri >= 32) & (ci < 32)
    mask_b = ((ri >> 5) == (ci >> 5)) & (((ri >> 4) & 1) == 1) & (((ci >> 4) & 1) == 0)
    mask_d = ((ri >> 4) == (ci >> 4)) & (ri >= ci)
    lane = _iota((c, LANES), 1)
    bd = (_iota((hpb * LANES, LANES), 0) >> 7) == (_iota((hpb * LANES, LANES), 1) >> dk_shift)

    def heads_rows(x):
        if hpb == 1:
            return x
        return jnp.concatenate([jnp.where((lane >> dk_shift) == h, x, 0.0) for h in range(hpb)], axis=0)

    chains = [(bi, blk) for bi in range(nb) for blk in range(nblk)]

    def part(ch, name):
        bi, blk = ch
        return prep[bi][name][:, blk * LANES:(blk + 1) * LANES]

    v_p = {ch: vs[ch[0]][:, ch[1] * hpb * LANES:(ch[1] + 1) * hpb * LANES].astype(BF16) for ch in chains}
    probs = {}
    for ch in chains:
        s_a = _bdot_nt(heads_rows(part(ch, 'q_a')), part(ch, 'k_a'))
        s_b = _bdot_nt(heads_rows(part(ch, 'q_b')), part(ch, 'k_b'))
        s_d = _bdot_nt(heads_rows(part(ch, 'q_d')), part(ch, 'k_d'))
        probs[ch] = (jnp.where(mask_a, s_a, 0.0) + jnp.where(mask_b, s_b, 0.0)
                     + jnp.where(mask_d, s_d, 0.0)).astype(BF16)
    outs = {}
    for ch in chains:
        pv = _d(probs[ch], v_p[ch])
        o = pv[:c]
        for h in range(1, hpb):
            o = jnp.where((_iota(o.shape, 1) >> 7) == h, pv[h * c:(h + 1) * c], o)
        si = ch[0] * nblk + ch[1]
        sp = st_ref[si]
        outs[ch] = o + _bdot_nt(part(ch, 'q_i'), sp)
        upd = _d_tn(v_p[ch], part(ch, 'k_s').astype(BF16))
        st_ref[si] = sp * part(ch, 'gam') + jnp.where(bd, upd, 0.0)
    return [jnp.concatenate([outs[(bi, blk)] for blk in range(nblk)], axis=1) for bi in range(nb)]


def _gated_rmsnorm(o, gate, norm_g):
    nh = o.shape[1] // LANES
    outs = []
    for h in range(nh):
        sl = slice(h * LANES, (h + 1) * LANES)
        oh = o[:, sl]
        ms = jnp.mean(oh * oh, axis=-1, keepdims=True)
        outs.append(oh * lax.rsqrt(ms + NORM_EPS) * norm_g * _silu(gate[:, sl]))
    return jnp.concatenate(outs, axis=1)


def _gla_body(q_ref, k_ref, v_ref, gate_ref, gk_ref, w2_ref, vec_ref, ng_ref, o_ref, st_ref):
    nb = q_ref.shape[0]

    @pl.when(pl.program_id(0) == 0)
    def _():
        st_ref[...] = jnp.zeros_like(st_ref)

    gs = [-_softplus(-(_hdot(gk_ref[b], w2_ref[...]) + vec_ref[0:1, :])) * (1.0 / GLA_GATE_TAU)
          for b in range(nb)]
    qs = [q_ref[b] * (GLA_DK ** -0.5) for b in range(nb)]
    os_ = _gla_core(qs, [k_ref[b] for b in range(nb)], [v_ref[b] for b in range(nb)], gs, st_ref, 2)
    for b in range(nb):
        o_ref[b] = _gated_rmsnorm(os_[b], gate_ref[b], ng_ref[0:1, :])


def _seq_specs(batch, c):
    def col(j, width):
        return pl.BlockSpec((batch, c, width), lambda i: (0, i, j))

    def full(shape):
        return pl.BlockSpec(shape, lambda i: (0,) * len(shape))

    return col, full


def _gla(p0, gk_w2p, gk_b, norm_g):
    batch, seq, _ = p0.shape
    c = CHUNK
    col, full = _seq_specs(batch, c)
    return pl.pallas_call(
        _gla_body,
        grid=(seq // c,),
        in_specs=[col(11, 256), col(12, 256), col(3, 512), col(4, 512), col(26, 128),
                  full((128, 256)), full((8, 256)), full((8, 128))],
        out_specs=pl.BlockSpec((batch, c, 512), lambda i: (0, i, 0)),
        out_shape=jax.ShapeDtypeStruct((batch, seq, 512), F32),
        scratch_shapes=[pltpu.VMEM((batch * 2, 2 * LANES, LANES), F32)],
        compiler_params=_cparams(1),
        name="gla",
    )(p0, p0, p0, p0, p0, gk_w2p, gk_b, norm_g)


def _hgrn_body(q_ref, f_ref, i_ref, gate_ref, lb_ref, ng_ref, o_ref, st_ref):
    nb = q_ref.shape[0]

    @pl.when(pl.program_id(0) == 0)
    def _():
        st_ref[...] = jnp.zeros_like(st_ref)

    lb = lb_ref[0:1, :]
    qs, ks, gs = [], [], []
    for b in range(nb):
        f = f_ref[b]
        gs.append(jnp.log(lb + (1.0 - lb) * _sigmoid(f)))
        ks.append((1.0 - lb) * _sigmoid(-f))
        qs.append(_silu(q_ref[b]))
    os_ = _gla_core(qs, ks, [i_ref[b] for b in range(nb)], gs, st_ref, 1)
    for b in range(nb):
        o_ref[b] = _gated_rmsnorm(os_[b], gate_ref[b], ng_ref[0:1, :])


def _hgrn(p1, lb, norm_g):
    batch, seq, _ = p1.shape
    c = CHUNK
    col, full = _seq_specs(batch, c)
    return pl.pallas_call(
        _hgrn_body,
        grid=(seq // c,),
        in_specs=[col(1, 512), col(2, 512), col(3, 512), col(4, 512), full((8, 512)), full((8, 128))],
        out_specs=pl.BlockSpec((batch, c, 512), lambda i: (0, i, 0)),
        out_shape=jax.ShapeDtypeStruct((batch, seq, 512), F32),
        scratch_shapes=[pltpu.VMEM((batch * HG_HEADS, LANES, LANES), F32)],
        compiler_params=_cparams(1),
        name="hgrn2",
    )(p1, p1, p1, p1, lb, norm_g)


def _s5_body(u_ref, dcat_ref, bm_ref, eblk_ref, pw_ref, o_ref, toep_ref, e_ref):
    seq = u_ref.shape[0]
    rows = seq // S5_CHUNK

    @pl.when(pl.program_id(1) == 0)
    def _():
        for tk in range(S5_CHUNK):
            lo = (S5_CHUNK - 1 - tk) * LANES
            toep_ref[tk * LANES:(tk + 1) * LANES, :] = dcat_ref[0, :, lo:lo + S5_CHUNK * LANES]
            e_ref[:, tk * LANES:(tk + 1) * LANES] = eblk_ref[0, tk]

    u = jnp.concatenate([u_ref[pl.ds(tk, rows, stride=S5_CHUNK), :] for tk in range(S5_CHUNK)],
                        axis=1).astype(BF16)
    s = _d(u, bm_ref[0])
    rin = _iota(s.shape, 0)
    pw = pw_ref[0]

    def swap_halves(x):
        return jnp.concatenate([pltpu.roll(x[:, k * LANES:(k + 1) * LANES], S5_STATE, 1)
                                for k in range(x.shape[1] // LANES)], axis=1)

    h = s
    for lev in range(int(math.log2(rows))):
        sh = 1 << lev
        hs = jnp.where(rin >= sh, pltpu.roll(h, sh, 0), 0.0)
        h = h + pw[2 * lev:2 * lev + 1, :] * hs + pw[2 * lev + 1:2 * lev + 2, :] * swap_halves(hs)
    hprev = jnp.where(rin >= 1, pltpu.roll(h, 1, 0), 0.0)
    y = _d(u, toep_ref[...]) + _bdot(hprev, e_ref[...])
    for tk in range(S5_CHUNK):
        o_ref[pl.ds(tk, rows, stride=S5_CHUNK), :] = y[:, tk * LANES:(tk + 1) * LANES]


def _s5_scan(p1, dcat, bmat, emat, pw):
    batch, seq, _ = p1.shape
    nt, width, nstate = bmat.shape

    def table(shape):
        return pl.BlockSpec((1,) + shape, lambda j, b: (j, 0, 0), pipeline_mode=pl.Buffered(1))

    return pl.pallas_call(
        _s5_body,
        grid=(nt, batch),
        in_specs=[pl.BlockSpec((None, seq, LANES), lambda j, b: (b, 0, j)),
                  table(dcat.shape[1:]), table((width, nstate)),
                  pl.BlockSpec((1,) + emat.shape[1:], lambda j, b: (j, 0, 0, 0), pipeline_mode=pl.Buffered(1)),
                  table((S5_SCAN_ROWS, nstate))],
        out_specs=pl.BlockSpec((None, seq, LANES), lambda j, b: (b, 0, j)),
        out_shape=jax.ShapeDtypeStruct((batch, seq, nt * LANES), F32),
        scratch_shapes=[pltpu.VMEM((width, width), BF16), pltpu.VMEM((nstate, width), BF16)],
        compiler_params=_cparams(2),
        name="s5_scan",
    )(p1, dcat, bmat, emat, pw)


def _s5_post_body(y_ref, u_ref, vec_ref, w_ref, o_ref):
    y = y_ref[...] + vec_ref[0:1, :] * u_ref[...]
    y = 0.5 * y * (1.0 + jnp.tanh(math.sqrt(2.0 / math.pi) * (y + 0.044715 * (y * y * y))))
    o_ref[...] = y * _sigmoid(_bdot(y, w_ref[...]) + vec_ref[1:2, :])


def _s5_post(y_ssm, p1, vec, glu_w, tm):
    t = y_ssm.shape[0]
    return pl.pallas_call(
        _s5_post_body,
        grid=(t // tm,),
        in_specs=[pl.BlockSpec((tm, 512), lambda i: (i, 0)),
                  pl.BlockSpec((tm, 512), lambda i: (i, 0)),
                  pl.BlockSpec((8, 512), lambda i: (0, 0)),
                  pl.BlockSpec((512, 512), lambda i: (0, 0))],
        out_specs=pl.BlockSpec((tm, 512), lambda i: (i, 0)),
        out_shape=jax.ShapeDtypeStruct((t, 512), F32),
        compiler_params=_cparams(1),
        name="s5_post",
    )(y_ssm, p1, vec, glu_w)


def _s5_tables(a_re, a_im, log_dt, b_re, b_im, c_re, c_im, rows_per_seq):
    c = S5_CHUNK
    lam_re = jnp.minimum(a_re, -1e-4)
    lam_im = a_im
    dt = jnp.exp(log_dt)[:, None]
    mag = jnp.exp(lam_re * dt)
    abar_re = mag * jnp.cos(lam_im * dt)
    abar_im = mag * jnp.sin(lam_im * dt)
    den = lam_re * lam_re + lam_im * lam_im
    num_re = abar_re - 1.0
    z_re = (num_re * lam_re + abar_im * lam_im) / den
    z_im = (abar_im * lam_re - num_re * lam_im) / den

    def power(n):
        n = jnp.asarray(n, F32)[..., None, None]
        m = jnp.exp(n * (lam_re * dt))
        return m * jnp.cos(n * (lam_im * dt)), m * jnp.sin(n * (lam_im * dt))

    def cmul(ar, ai, br, bi):
        return ar * br - ai * bi, ar * bi + ai * br

    tau = jnp.arange(c)
    p_re, p_im = power(tau)
    zb_re, zb_im = cmul(z_re[..., None], z_im[..., None], b_re, b_im)
    cp_re, cp_im = cmul(c_re[None], c_im[None], p_re[:, :, None, :], p_im[:, :, None, :])
    hi = lax.Precision.HIGHEST
    kern = (jnp.einsum('tgon,gni->gtoi', cp_re, zb_re, precision=hi)
            - jnp.einsum('tgon,gni->gtoi', cp_im, zb_im, precision=hi))
    q_re, q_im = power(c - 1 - tau)
    bm_re, bm_im = cmul(q_re[..., None], q_im[..., None], zb_re[None], zb_im[None])
    bmat = jnp.concatenate([jnp.transpose(bm_re, (1, 0, 3, 2)), jnp.transpose(bm_im, (1, 0, 3, 2))],
                           axis=-1).reshape(S5_GROUPS, c * S5_GROUP, 2 * S5_STATE)
    r_re, r_im = power(tau + 1)
    e_re, e_im = cmul(c_re[None], c_im[None], r_re[:, :, None, :], r_im[:, :, None, :])
    emat = jnp.concatenate([jnp.transpose(e_re, (1, 3, 0, 2)), -jnp.transpose(e_im, (1, 3, 0, 2))],
                           axis=1).reshape(S5_GROUPS, 2 * S5_STATE, c * S5_GROUP)
    nlev = int(math.log2(rows_per_seq))
    assert 2 * nlev <= S5_SCAN_ROWS
    s_re, s_im = power(c * (2 ** jnp.arange(nlev)))
    pw = jnp.zeros((S5_GROUPS, S5_SCAN_ROWS, 2 * S5_STATE), F32)
    pw = pw.at[:, 0:2 * nlev:2, :].set(jnp.transpose(jnp.concatenate([s_re, s_re], -1), (1, 0, 2)))
    pw = pw.at[:, 1:2 * nlev:2, :].set(jnp.transpose(jnp.concatenate([-s_im, s_im], -1), (1, 0, 2)))
    nt, gl = S5_GROUPS // S5_TILE_GROUPS, S5_TILE_GROUPS
    spread = jnp.tile(jnp.eye(S5_GROUP, dtype=F32), (1, gl))
    lane_group = jnp.arange(LANES) // S5_GROUP
    kern5 = jnp.transpose(kern.reshape(nt, gl, c, S5_GROUP, S5_GROUP), (0, 2, 1, 4, 3))
    dblk = jnp.einsum('jtgio,op->jtgip', kern5, spread, precision=hi)
    dblk = jnp.where((jnp.arange(gl)[:, None, None] == lane_group[None, None, :]), dblk, 0.0)
    dblk = dblk.reshape(nt, c, LANES, LANES)
    dcat = jnp.transpose(dblk, (0, 2, 1, 3)).reshape(nt, LANES, c * LANES)
    dcat = jnp.pad(dcat, ((0, 0), (0, 0), ((c - 1) * LANES, 0)))
    bmat5 = jnp.transpose(bmat.reshape(nt, gl, c, S5_GROUP, 2 * S5_STATE), (0, 2, 1, 3, 4))
    row_group = (jnp.arange(c * LANES) // S5_GROUP) % gl
    col_tile = jnp.arange(gl * 2 * S5_STATE) // (2 * S5_STATE)
    bmat_t = jnp.where(row_group[:, None] == col_tile[None, :],
                       jnp.tile(bmat5.reshape(nt, c * LANES, 2 * S5_STATE), (1, 1, gl)), 0.0)
    emat5 = jnp.transpose(emat.reshape(nt, gl, 2 * S5_STATE, c, S5_GROUP), (0, 3, 1, 2, 4))
    eblk = jnp.einsum('jtgno,op->jtgnp', emat5, spread, precision=hi)
    eblk = jnp.where((jnp.arange(gl)[:, None, None] == lane_group[None, None, :]), eblk, 0.0)
    eblk = eblk.reshape(nt, c, gl * 2 * S5_STATE, LANES)
    pw_t = jnp.transpose(pw.reshape(nt, gl, S5_SCAN_ROWS, 2 * S5_STATE), (0, 2, 1, 3))
    pw_t = pw_t.reshape(nt, S5_SCAN_ROWS, gl * 2 * S5_STATE)
    return dcat.astype(BF16), bmat_t.astype(BF16), eblk.astype(BF16), pw_t


def _proj_route_body(ya_ref, yb_ref, x_ref, w_ref, ln_ref, wr_ref, br_ref,
                     h_ref, route_ref, cnt_ref, carry_ref):
    tm = ya_ref.shape[0]
    half = w_ref.shape[0] // 2

    @pl.when(pl.program_id(0) == 0)
    def _():
        carry_ref[...] = jnp.zeros_like(carry_ref)

    mix = _bdot(ya_ref[...], w_ref[:half, :]) + _bdot(yb_ref[...], w_ref[half:, :])
    h = _layer_norm(DN_ALPHA * x_ref[...] + mix, ln_ref[0:1, :], ln_ref[1:2, :])
    h_ref[...] = h

    logits = _hdot(h, wr_ref[...]) + br_ref[0:1, :]
    lane = _iota((tm, LANES), 1).astype(F32)
    neg = -jnp.inf

    def softmax_masked(mask):
        xm = jnp.where(mask, logits, neg)
        m = jnp.max(xm, axis=-1, keepdims=True)
        e = jnp.exp(xm - m)
        return e / jnp.sum(e, axis=-1, keepdims=True)

    def top1(pm):
        m = jnp.max(pm, axis=-1, keepdims=True)
        idx = jnp.min(jnp.where(pm == m, lane, float(LANES)), axis=-1, keepdims=True)
        return m, idx

    coarse = jnp.where(lane < MOE_GROUPS, softmax_masked(lane < MOE_GROUPS), -1.0)
    p_grp, grp = top1(coarse)
    lo = MOE_GROUPS + MOE_PER_GROUP * grp
    fmask = (lane >= lo) & (lane < lo + MOE_PER_GROUP)
    fine = jnp.where(fmask, softmax_masked(fmask), -1.0)
    p1, j1 = top1(fine)
    p2, j2 = top1(jnp.where(lane == j1, -1.0, fine))
    denom = p1 + p2
    g1 = p_grp * (p1 / denom)
    g2 = p_grp * (p2 / denom)
    e1 = j1 - MOE_GROUPS
    e2 = j2 - MOE_GROUPS

    oh1 = jnp.where(lane == e1, 1.0, 0.0)
    oh2 = jnp.where(lane == e2, 1.0, 0.0)
    cnt = oh1 + oh2
    strict = jnp.where(_iota((tm, tm), 0) > _iota((tm, tm), 1), 1.0, 0.0).astype(BF16)
    before = _d(strict, cnt.astype(BF16)) + carry_ref[0:1, :]
    r1 = jnp.sum(oh1 * before, axis=-1, keepdims=True)
    r2 = jnp.sum(oh2 * before, axis=-1, keepdims=True)
    carry_ref[0:1, :] = carry_ref[0:1, :] + jnp.sum(cnt, axis=0, keepdims=True)
    cnt_ref[...] = carry_ref[...]

    out = jnp.where(lane == 0, e1, 0.0)
    out = jnp.where(lane == 1, e2, out)
    out = jnp.where(lane == 2, r1, out)
    out = jnp.where(lane == 3, r2, out)
    out = jnp.where(lane == 4, g1, out)
    out = jnp.where(lane == 5, g2, out)
    route_ref[...] = out[:, :8]


def _proj_route(ya, yb, resid, w_out, ln, wr, br, tm):
    t, d = resid.shape
    return pl.pallas_call(
        _proj_route_body,
        grid=(t // tm,),
        in_specs=[pl.BlockSpec((tm, 512), lambda i: (i, 0)),
                  pl.BlockSpec((tm, 512), lambda i: (i, 0)),
                  pl.BlockSpec((tm, d), lambda i: (i, 0)),
                  pl.BlockSpec((d, d), lambda i: (0, 0)),
                  pl.BlockSpec((8, d), lambda i: (0, 0)),
                  pl.BlockSpec((d, LANES), lambda i: (0, 0)),
                  pl.BlockSpec((8, LANES), lambda i: (0, 0))],
        out_specs=[pl.BlockSpec((tm, d), lambda i: (i, 0)),
                   pl.BlockSpec((tm, 8), lambda i: (i, 0)),
                   pl.BlockSpec((8, LANES), lambda i: (0, 0))],
        out_shape=[jax.ShapeDtypeStruct((t, d), F32),
                   jax.ShapeDtypeStruct((t, 8), F32),
                   jax.ShapeDtypeStruct((8, LANES), F32)],
        scratch_shapes=[pltpu.VMEM((8, LANES), F32)],
        compiler_params=_cparams(1),
        name="proj_ln_route",
    )(ya, yb, resid, w_out, ln, wr, br)


def _row_copy(src_ref, src_row, dst_ref, dst_row, sem):
    return pltpu.make_async_copy(src_ref.at[pl.ds(src_row, 1)], dst_ref.at[pl.ds(dst_row, 1)], sem)


def _start_all(copies):
    for n, cp in enumerate(copies):
        cp.start(priority=n % 2)


def _dispatch_body(pad_start_ref, pad_len_ref, used_rows_ref, dest_ref, h_ref, rows_ref, hbuf_ref, zbuf_ref,
                   lsem, ssem, zsem):
    i = pl.program_id(0)
    last = pl.num_programs(0) - 1
    tm = hbuf_ref.shape[1]
    slot = i % 2

    def load(tile, s):
        return pltpu.make_async_copy(h_ref.at[pl.ds(tile * tm, tm)], hbuf_ref.at[s], lsem.at[s])

    def scatters(s):
        return [_row_copy(hbuf_ref.at[s], r, rows_ref, dest_ref[0, 0, 2 * r + q], ssem.at[s])
                for r in range(tm) for q in range(2)]

    @pl.when(i == 0)
    def _():
        load(0, 0).start()

    @pl.when(i >= 1)
    def _():
        for cp in scatters(1 - slot):
            cp.wait()

    @pl.when(i < last)
    def _():
        load(i + 1, 1 - slot).start()

    load(i, slot).wait()
    _start_all(scatters(slot))

    @pl.when(i == last)
    def _():
        for cp in scatters(slot):
            cp.wait()
        zbuf_ref[...] = jnp.zeros_like(zbuf_ref)
        sub = 8
        half = zbuf_ref.shape[0]

        def pad_fills():
            out = []
            for e in range(N_EXPERTS):
                start = pad_start_ref[e]
                npad = pad_len_ref[e]
                end = start + npad
                run = half
                while run >= sub:
                    end = end - (npad & run)
                    dst = rows_ref.at[pl.ds(pl.multiple_of(end, run), run)]
                    out.append(((npad & run) != 0,
                                pltpu.make_async_copy(zbuf_ref.at[pl.ds(0, run)], dst, zsem)))
                    run //= 2
                for k in range(sub - 1):
                    out.append((k < (npad & (sub - 1)), _row_copy(zbuf_ref, 0, rows_ref, start + k, zsem)))
            return out

        for pred, cp in pad_fills():
            pl.when(pred)(cp.start)
        for pred, cp in pad_fills():
            pl.when(pred)(cp.wait)

        def tail(b):
            return pltpu.make_async_copy(
                zbuf_ref, rows_ref.at[pl.ds(pl.multiple_of(b * half, half), half)], zsem)

        first_free = used_rows_ref[0] // half
        n_half = rows_ref.shape[0] // half
        lax.fori_loop(first_free, n_half, lambda b, c: (tail(b).start(), c)[1], 0)
        lax.fori_loop(first_free, n_half, lambda b, c: (tail(b).wait(), c)[1], 0)


def _dispatch(pad_start, pad_len, used_rows, dest3, h, n_rows):
    t, d = h.shape
    nt, _, tm2 = dest3.shape
    tm = tm2 // 2
    grid_spec = pltpu.PrefetchScalarGridSpec(
        num_scalar_prefetch=3,
        grid=(nt,),
        in_specs=[pl.BlockSpec((1, 1, tm2), lambda i, *_: (i, 0, 0), memory_space=pltpu.SMEM),
                  pl.BlockSpec(memory_space=pl.ANY)],
        out_specs=pl.BlockSpec(memory_space=pl.ANY),
        scratch_shapes=[pltpu.VMEM((2, tm, d), F32), pltpu.VMEM((MOE_BLOCK // 2, d), F32),
                        pltpu.SemaphoreType.DMA((2,)), pltpu.SemaphoreType.DMA((2,)),
                        pltpu.SemaphoreType.DMA],
    )
    return pl.pallas_call(
        _dispatch_body,
        grid_spec=grid_spec,
        out_shape=jax.ShapeDtypeStruct((n_rows, d), F32),
        compiler_params=_cparams(1),
        name="moe_dispatch",
    )(pad_start, pad_len, used_rows, dest3, h)


def _expert_body(first_ref, nblk_ref, nused_ref, x_ref, w1_ref, w3_ref, w2_ref, y_ref,
                 xbuf_ref, ybuf_ref, w1b_ref, w3b_ref, w2b_ref, ypend_ref, ycnt_ref, xsem, ysem):
    e = pl.program_id(0)
    row0 = first_ref[e] * MOE_BLOCK
    nblk = nblk_ref[e]
    big = xbuf_ref.shape[1]
    per_big = big // MOE_BLOCK

    @pl.when(e == 0)
    def _():
        ypend_ref[0] = 0
        ypend_ref[1] = 0
        ycnt_ref[0] = 0

    def drain(slot):
        for size in (big, MOE_BLOCK):
            @pl.when(ypend_ref[slot] == size)
            def _(size=size):
                pltpu.make_async_copy(ybuf_ref.at[slot, pl.ds(0, size)], y_ref.at[pl.ds(0, size)],
                                      ysem.at[slot]).wait()
        ypend_ref[slot] = 0

    def cast_weights():
        w1b_ref[...] = w1_ref[0].astype(BF16)
        w3b_ref[...] = w3_ref[0].astype(BF16)
        w2b_ref[...] = w2_ref[0].astype(BF16)

    def stream(start_row, count, size, casts_first):
        def x_copy(j, s):
            src = x_ref.at[pl.ds(pl.multiple_of(start_row + j * size, MOE_BLOCK), size)]
            return pltpu.make_async_copy(src, xbuf_ref.at[s, pl.ds(0, size)], xsem.at[s])

        def y_copy(j, s):
            dst = y_ref.at[pl.ds(pl.multiple_of(start_row + j * size, MOE_BLOCK), size)]
            return pltpu.make_async_copy(ybuf_ref.at[s, pl.ds(0, size)], dst, ysem.at[s])

        @pl.when(count > 0)
        def _():
            x_copy(0, 0).start(priority=1)
            pl.when(casts_first)(cast_weights)

            def chunk(j, carry):
                s = j % 2

                @pl.when(j + 1 < count)
                def _():
                    x_copy(j + 1, 1 - s).start(priority=1)

                x_copy(j, s).wait()
                xb = xbuf_ref[s, pl.ds(0, size), :].astype(BF16)
                hid = _silu(_d(xb, w1b_ref[...])) * _d(xb, w3b_ref[...])
                ys = ycnt_ref[0] % 2
                drain(ys)
                ybuf_ref[ys, pl.ds(0, size), :] = _d(hid.astype(BF16), w2b_ref[...])
                y_copy(j, ys).start()
                ypend_ref[ys] = size
                ycnt_ref[0] = ycnt_ref[0] + 1
                return carry

            lax.fori_loop(0, count, chunk, 0)

    @pl.when(nblk > 0)
    def _():
        nbig = nblk // per_big
        stream(row0, nbig, big, nbig > 0)
        stream(row0 + nbig * big, nblk - nbig * per_big, MOE_BLOCK, nbig == 0)

    @pl.when(e == pl.num_programs(0) - 1)
    def _():
        drain(0)
        drain(1)
        ybuf_ref[0, pl.ds(0, MOE_BLOCK), :] = jnp.zeros((MOE_BLOCK, ybuf_ref.shape[2]), F32)
        nused = nused_ref[0]
        ntot = y_ref.shape[0] // MOE_BLOCK

        def fill(b, carry):
            cp = pltpu.make_async_copy(ybuf_ref.at[0, pl.ds(0, MOE_BLOCK)],
                                       y_ref.at[pl.ds(b * MOE_BLOCK, MOE_BLOCK)], ysem.at[0])
            cp.start()
            cp.wait()
            return carry

        lax.fori_loop(nused, ntot, fill, 0)


def _experts(first_blk, nblk, nused, x_rows, w1, w3, w2, layer):
    r, d = x_rows.shape
    hid = w1.shape[-1]
    grid_spec = pltpu.PrefetchScalarGridSpec(
        num_scalar_prefetch=3,
        grid=(N_EXPERTS,),
        in_specs=[pl.BlockSpec(memory_space=pl.ANY),
                  pl.BlockSpec((None, 1, d, hid), lambda e, *_: (layer, e, 0, 0)),
                  pl.BlockSpec((None, 1, d, hid), lambda e, *_: (layer, e, 0, 0)),
                  pl.BlockSpec((None, 1, hid, d), lambda e, *_: (layer, e, 0, 0))],
        out_specs=pl.BlockSpec(memory_space=pl.ANY),
        scratch_shapes=[pltpu.VMEM((2, EXPERT_CHUNK, d), F32), pltpu.VMEM((2, EXPERT_CHUNK, d), F32),
                        pltpu.VMEM((d, hid), BF16), pltpu.VMEM((d, hid), BF16), pltpu.VMEM((hid, d), BF16),
                        pltpu.SMEM((2,), jnp.int32), pltpu.SMEM((1,), jnp.int32),
                        pltpu.SemaphoreType.DMA((2,)), pltpu.SemaphoreType.DMA((2,))],
    )
    return pl.pallas_call(
        _expert_body,
        grid_spec=grid_spec,
        out_shape=jax.ShapeDtypeStruct((r, d), F32),
        compiler_params=_cparams(1),
        name="moe_experts",
    )(first_blk, nblk, nused, x_rows, w1, w3, w2)


def _combine_body(dest_ref, dest_next_ref, gate_ref, h_ref, ln_ref, rows_ref, o_ref, buf_ref, sem):
    i = pl.program_id(0)
    last = pl.num_programs(0) - 1
    tm = h_ref.shape[0]

    def gather(dref, slot):
        return [_row_copy(rows_ref, dref[0, 0, 2 * r + s], buf_ref.at[slot, s], r, sem.at[slot])
                for r in range(tm) for s in range(2)]

    @pl.when(i == 0)
    def _():
        _start_all(gather(dest_ref, 0))

    _start_all(gather(dest_next_ref, (i + 1) % 2))
    slot = i % 2
    for cp in gather(dest_ref, slot):
        cp.wait()
    gate = gate_ref[...]
    y = gate[:, 4:5] * buf_ref[slot, 0] + gate[:, 5:6] * buf_ref[slot, 1]
    o_ref[...] = _layer_norm(DN_ALPHA * h_ref[...] + y, ln_ref[0:1, :], ln_ref[1:2, :])

    @pl.when(i == last)
    def _():
        for cp in gather(dest_next_ref, (i + 1) % 2):
            cp.wait()


def _combine(dest3, route, h, ln, y_rows, tm):
    t, d = h.shape
    nt = t // tm
    return pl.pallas_call(
        _combine_body,
        grid=(nt,),
        in_specs=[pl.BlockSpec((1, 1, 2 * tm), lambda i: (i, 0, 0), memory_space=pltpu.SMEM),
                  pl.BlockSpec((1, 1, 2 * tm), lambda i: (jnp.minimum(i + 1, nt - 1), 0, 0),
                               memory_space=pltpu.SMEM),
                  pl.BlockSpec((tm, 8), lambda i: (i, 0)),
                  pl.BlockSpec((tm, d), lambda i: (i, 0)),
                  pl.BlockSpec((8, d), lambda i: (0, 0)),
                  pl.BlockSpec(memory_space=pl.ANY)],
        out_specs=pl.BlockSpec((tm, d), lambda i: (i, 0)),
        out_shape=jax.ShapeDtypeStruct((t, d), F32),
        scratch_shapes=[pltpu.VMEM((2, 2, tm, d), F32), pltpu.SemaphoreType.DMA((2,))],
        compiler_params=_cparams(1),
        name="moe_combine_ln",
    )(dest3, dest3, route, h, ln, y_rows)


def _moe(h, route, counts, w1, w3, w2, layer, ln):
    t, d = h.shape
    tm = min(COMBINE_TILE, t)
    a = 2 * t
    expert = route[:, 0:2].astype(jnp.int32)
    rank = route[:, 2:4].astype(jnp.int32)
    cnt = counts[0, :N_EXPERTS].astype(jnp.int32)
    padded = (cnt + MOE_BLOCK - 1) // MOE_BLOCK * MOE_BLOCK
    pend = jnp.cumsum(padded)
    pstart = pend - padded
    sel = expert[..., None] == jnp.arange(N_EXPERTS, dtype=jnp.int32)
    dest = jnp.sum(jnp.where(sel, pstart, 0), axis=-1) + rank
    n_blocks = -(-a // MOE_BLOCK) + N_EXPERTS
    nused = (pend[-1:] // MOE_BLOCK).astype(jnp.int32)
    dest3 = dest.reshape(t // tm, 1, 2 * tm)
    x_rows = _dispatch((pstart + cnt).astype(jnp.int32), (padded - cnt).astype(jnp.int32),
                       pend[-1:].astype(jnp.int32), dest3, h, n_blocks * MOE_BLOCK)
    y_rows = _experts((pstart // MOE_BLOCK).astype(jnp.int32), (padded // MOE_BLOCK).astype(jnp.int32),
                      nused, x_rows, w1, w3, w2, layer)
    return _combine(dest3, route, h, ln, y_rows, tm)


def _pad_rows(x, rows):
    return jnp.zeros((rows,) + x.shape[1:], x.dtype).at[:x.shape[0]].set(x)


def _route_weights(wg, bg, we, be):
    d = wg.shape[0]
    wr = jnp.zeros((d, LANES), F32)
    wr = wr.at[:, :MOE_GROUPS].set(wg)
    wr = wr.at[:, MOE_GROUPS:MOE_GROUPS + N_EXPERTS].set(jnp.transpose(we, (1, 0, 2)).reshape(d, N_EXPERTS))
    br = jnp.zeros((8, LANES), F32)
    br = br.at[0, :MOE_GROUPS].set(bg)
    br = br.at[0, MOE_GROUPS:MOE_GROUPS + N_EXPERTS].set(be.reshape(N_EXPERTS))
    return wr, br


def kernel(x, ab_w_in, rw_mu, rw_w0, rw_w2, rw_a0, rw_a2, rw_g2, rw_k_k, rw_k_a, rw_r_k, rw_gn_g, rw_gn_b, gla_gk_w2, gla_gk_b, gla_norm_g, ab_w_out, cd_w_in, s5_a_re, s5_a_im, s5_log_dt, s5_b_re, s5_b_im, s5_c_re, s5_c_im, s5_d, s5_glu_w, s5_glu_b, hg_lb, hg_norm_g, cd_w_out, ln1_g, ln1_b, moe_wg, moe_bg, moe_we, moe_be, moe_w1, moe_w3, moe_w2, ln2_g, ln2_b):
    batch, seq, d = x.shape
    t = batch * seq
    assert d == D_MODEL and seq % CHUNK == 0
    rows_per_seq = seq // S5_CHUNK
    assert rows_per_seq & (rows_per_seq - 1) == 0, "S5 chunk scan assumes a power-of-two chunk count"
    tm = min(ROW_TILE, t)
    assert t % tm == 0
    xt = x.reshape(t, d)
    ln1 = [_pad_rows(jnp.stack([ln1_g[l], ln1_b[l]]), 8) for l in range(DEPTH)]
    ln2 = [_pad_rows(jnp.stack([ln2_g[l], ln2_b[l]]), 8) for l in range(DEPTH)]

    j = 0
    w = ab_w_in[j]
    r_, wl_, k_, v_, al_, gl_ = 0, 512, 576, 1088, 1600, 1664
    gq, gk, gv, glow, ggate = 1792, 2048, 2304, 2816, 2832
    w0cols = jnp.concatenate([
        w[:, r_:r_ + 512], w[:, k_:k_ + 512], w[:, v_:v_ + 512], w[:, gv:gv + 512], w[:, ggate:ggate + 512],
        w[:, wl_:wl_ + 64], w[:, al_:al_ + 64], w[:, gl_:gl_ + 128], w[:, gq:gq + 256], w[:, gk:gk + 256],
        w[:, glow:glow + 16], jnp.zeros((d, 112), F32)], axis=1).astype(BF16)
    p0 = _matmul(xt, w0cols, tm)

    mu = rw_mu[j]
    vec = _pad_rows(jnp.stack([mu[r_:r_ + 512], mu[k_:k_ + 512], mu[v_:v_ + 512], rw_w0[j], rw_a0[j],
                               rw_k_k[j], rw_k_a[j], rw_r_k[j].reshape(-1), rw_gn_g[j], rw_gn_b[j]]), 16)
    mulo = _pad_rows(jnp.concatenate([mu[wl_:wl_ + 64], mu[al_:al_ + 64], mu[gl_:gl_ + 128]])[None], 8)
    w2p = _pad_rows(rw_w2[j], 128)
    a2p = jnp.zeros((128, 512), F32).at[64:].set(rw_a2[j])
    p0 = p0.reshape(batch, seq, -1)
    y_rw = _rwkv(p0, vec, mulo, w2p, a2p, rw_g2[j]).reshape(t, RW_WIDTH)
    y_gla = _gla(p0, _pad_rows(gla_gk_w2[j], 128), _pad_rows(gla_gk_b[j][None], 8),
                 _pad_rows(gla_norm_g[j][None], 8)).reshape(t, 512)

    wr, br = _route_weights(moe_wg[0], moe_bg[0], moe_we[0], moe_be[0])
    h, route, counts = _proj_route(y_rw, y_gla, xt, ab_w_out[j].astype(BF16), ln1[0], wr, br, tm)
    h = _moe(h, route, counts, moe_w1, moe_w3, moe_w2, 0, ln2[0])

    p1 = _matmul(h, cd_w_in[j].astype(BF16), tm)
    lb_sm = jax.nn.softmax(hg_lb.astype(F32), axis=0)
    lower = (jnp.cumsum(lb_sm, axis=0) - lb_sm[0])[1]
    y_hg = _hgrn(p1.reshape(batch, seq, -1), _pad_rows(lower[None], 8),
                 _pad_rows(hg_norm_g[j][None], 8)).reshape(t, 512)

    toep, bmat, emat, pw = _s5_tables(s5_a_re[j], s5_a_im[j], s5_log_dt[j], s5_b_re[j], s5_b_im[j],
                                      s5_c_re[j], s5_c_im[j], rows_per_seq)
    y_ssm = _s5_scan(p1.reshape(batch, seq, -1), toep, bmat, emat, pw).reshape(t, 512)
    y_s5 = _s5_post(y_ssm, p1, _pad_rows(jnp.stack([s5_d[j], s5_glu_b[j]]), 8),
                    s5_glu_w[j].astype(BF16), tm)

    wr, br = _route_weights(moe_wg[1], moe_bg[1], moe_we[1], moe_be[1])
    h2, route, counts = _proj_route(y_s5, y_hg, h, cd_w_out[j].astype(BF16), ln1[1], wr, br, tm)
    out = _moe(h2, route, counts, moe_w1, moe_w3, moe_w2, 1, ln2[1])
    return out.reshape(batch, seq, d)
```

```python
import functools
import math

import numpy as np
import jax
import jax.numpy as jnp
from jax import lax
from jax.experimental import pallas as pl
from jax.experimental.pallas import tpu as pltpu

F32 = jnp.float32
BF16 = jnp.bfloat16

D_MODEL = 1024
DEPTH = 2
RW_HEAD = 64
RW_WIDTH = 512
RW_GN_EPS = 64e-5
GLA_HEADS = 4
GLA_DK = 64
GLA_DV = 128
GLA_GATE_TAU = 16.0
S5_GROUP = 16
S5_GROUPS = 32
S5_STATE = 64
HG_HEADS = 4
HG_DK = 128
CHUNK = 64
S5_CHUNK = 16
S5_SCAN_ROWS = 32
S5_TILE_GROUPS = 8
NORM_EPS = 1e-5
MOE_GROUPS = 4
MOE_PER_GROUP = 8
N_EXPERTS = 32
EXPERT_HIDDEN = 512
MOE_BLOCK = 128
COMBINE_TILE = 128
EXPERT_CHUNK = 512
ROW_TILE = 512
DN_ALPHA = (2.0 * DEPTH) ** 0.25

LANES = 128
VMEM_LIMIT = 56 * 1024 * 1024


def _cparams(n_axes=1):
    return pltpu.CompilerParams(dimension_semantics=("arbitrary",) * n_axes,
                                vmem_limit_bytes=VMEM_LIMIT)


def _d(a, b):
    return jnp.dot(a, b, preferred_element_type=F32)


def _d_nt(a, b):
    return lax.dot_general(a, b, (((1,), (1,)), ((), ())), preferred_element_type=F32)


def _d_tn(a, b):
    return lax.dot_general(a, b, (((0,), (0,)), ((), ())), preferred_element_type=F32)


def _split(a):
    hi = a.astype(BF16)
    lo = (a - hi.astype(F32)).astype(BF16)
    return hi, lo


def _split3(a):
    hi = a.astype(BF16)
    r1 = a - hi.astype(F32)
    mid = r1.astype(BF16)
    lo = (r1 - mid.astype(F32)).astype(BF16)
    return hi, mid, lo


def _bdot(a, b):
    return _d(a.astype(BF16), b.astype(BF16))


def _bdot_nt(a, b):
    return _d_nt(a.astype(BF16), b.astype(BF16))


def _bdot_tn(a, b):
    return _d_tn(a.astype(BF16), b.astype(BF16))


def _hdot_with(d, a, b):
    ah, al = _split(a)
    bh, bl = _split(b)
    return d(ah, bh) + (d(ah, bl) + d(al, bh))


def _hdot(a, b):
    return _hdot_with(_d, a, b)


def _hdot_nt(a, b):
    return _hdot_with(_d_nt, a, b)


def _hdot_tn(a, b):
    return _hdot_with(_d_tn, a, b)


def _xdot_r(e, a):
    ah, am, al = _split3(a)
    return _d(e, ah) + (_d(e, am) + _d(e, al))


def _iota(shape, dim):
    return lax.broadcasted_iota(jnp.int32, shape, dim)


def _softplus(x):
    return jnp.maximum(x, 0.0) + jnp.log1p(jnp.exp(-jnp.abs(x)))


def _sigmoid(x):
    return 1.0 / (1.0 + jnp.exp(-x))


def _silu(x):
    return x * _sigmoid(x)


def _segsum(a, e):
    m = a.shape[0]
    s = _d(jnp.concatenate(_split3(a), axis=0), e)
    return s[:m] + (s[m:2 * m] + s[2 * m:])


def _cumsum_rows(g, chunk=None):
    n = g.shape[0]
    chunk = n if chunk is None else chunk
    shift = int(math.log2(chunk))
    r, c = _iota((n, n), 0), _iota((n, n), 1)
    tril = jnp.where((r >= c) & ((r >> shift) == (c >> shift)), 1.0, 0.0).astype(BF16)
    return _xdot_r(tril, g)


def _shift_mix(x, prev_ref, b, mu):
    c = x.shape[0]
    rolled = pltpu.roll(x, 1, 0)
    prev = jnp.where(_iota(x.shape, 0) == 0, jnp.broadcast_to(prev_ref[b, 0:1, :], x.shape), rolled)
    prev_ref[b, 0:1, :] = x[c - 1:c, :]
    return x + mu * (prev - x)


def _layer_norm(x, g, b):
    mu = jnp.mean(x, axis=-1, keepdims=True)
    xc = x - mu
    var = jnp.mean(xc * xc, axis=-1, keepdims=True)
    return xc * lax.rsqrt(var + NORM_EPS) * g + b


def _mm_body(x_ref, w_ref, o_ref):
    o_ref[...] = _d(x_ref[...].astype(BF16), w_ref[...])


def _matmul(x, w_bf16, tm):
    m, k = x.shape
    n = w_bf16.shape[1]
    return pl.pallas_call(
        _mm_body,
        grid=(m // tm,),
        in_specs=[pl.BlockSpec((tm, k), lambda i: (i, 0)),
                  pl.BlockSpec((k, n), lambda i: (0, 0))],
        out_specs=pl.BlockSpec((tm, n), lambda i: (i, 0)),
        out_shape=jax.ShapeDtypeStruct((m, n), F32),
        compiler_params=_cparams(1),
        name="in_proj",
    )(x, w_bf16)


_RV_MU_R, _RV_MU_K, _RV_MU_V, _RV_W0, _RV_A0, _RV_KK, _RV_KA, _RV_RK, _RV_GNG, _RV_GNB = range(10)


def _rwkv_body(r_ref, k_ref, v_ref, lo_ref, vec_ref, mulo_ref, w2_ref, a2_ref, g2_ref, ones_ref,
               o_ref, pr_ref, pk_ref, pv_ref, plo_ref, st_ref):
    nb, c = r_ref.shape[0], r_ref.shape[1]
    npair = RW_WIDTH // LANES

    @pl.when(pl.program_id(0) == 0)
    def _():
        pr_ref[...] = jnp.zeros_like(pr_ref)
        pk_ref[...] = jnp.zeros_like(pk_ref)
        pv_ref[...] = jnp.zeros_like(pv_ref)
        plo_ref[...] = jnp.zeros_like(plo_ref)
        st_ref[...] = jnp.zeros_like(st_ref)

    def vec(i):
        return vec_ref[i:i + 1, :]

    ones_bd = ones_ref[...]
    lane = _iota((c, LANES), 1)
    m1 = lane < RW_HEAD
    row2 = _iota((2 * c, 4 * c), 0)
    col2 = _iota((2 * c, 4 * c), 1) & (c - 1)
    tri = ((row2 < c) & (row2 > col2)) | ((row2 >= c) & ((row2 - c) >= col2))
    eye2 = jnp.where(_iota((2 * c, 2 * c), 0) == _iota((2 * c, 2 * c), 1), 1.0, 0.0)
    bd_p = (_iota((LANES, LANES), 0) >> 6) == (_iota((LANES, LANES), 1) >> 6)

    def halves(x):
        return jnp.concatenate([jnp.where(m1, x, 0.0), jnp.where(m1, 0.0, x)], axis=0)

    def stacked(ref, prev_ref, mu):
        return jnp.concatenate([_shift_mix(ref[b], prev_ref, b, mu) for b in range(nb)], axis=0)

    xr = stacked(r_ref, pr_ref, vec(_RV_MU_R))
    xk = stacked(k_ref, pk_ref, vec(_RV_MU_K))
    xv = stacked(v_ref, pv_ref, vec(_RV_MU_V))
    xlo = stacked(lo_ref, plo_ref, mulo_ref[0:1, :])
    lo_a = xlo[:, :LANES]
    lo_g = xlo[:, LANES:]
    w = -_softplus(-(vec(_RV_W0) + _bdot(jnp.tanh(lo_a), w2_ref[...]))) - 0.5
    g = -jnp.exp(w)
    a = _sigmoid(vec(_RV_A0) + _bdot(lo_a, a2_ref[...]))
    gate = _bdot(_sigmoid(lo_g), g2_ref[...])
    kk = xk * vec(_RV_KK)
    kk = kk / jnp.maximum(jnp.sqrt(_segsum(kk * kk, ones_bd)), 1e-12)
    k2 = xk * (1.0 + (a - 1.0) * vec(_RV_KA))
    gc = _cumsum_rows(g, c)
    g_last = jnp.concatenate([jnp.broadcast_to(gc[(b + 1) * c - 1:(b + 1) * c, :], (c, gc.shape[1]))
                              for b in range(nb)], axis=0)
    e_neg = jnp.exp(-gc)
    e_end = jnp.exp(g_last - gc)
    prep = dict(
        xv=xv,
        at=-kk * jnp.exp(gc - g),
        bt=(kk * a) * e_neg, kt=k2 * e_neg, rt=xr * jnp.exp(gc),
        bh=(kk * a) * e_end, kh=k2 * e_end,
        gam=jnp.exp(g_last))

    chains = [(b, p) for b in range(nb) for p in range(npair)]

    def part(ch, name):
        b, p = ch
        return prep[name][b * c:(b + 1) * c, p * LANES:(p + 1) * LANES]

    a_ak, a_row, vv, ak, pinv = {}, {}, {}, {}, {}
    for ch in chains:
        lhs = jnp.concatenate([part(ch, 'at'), part(ch, 'rt')], axis=0)
        rhs = jnp.concatenate([halves(part(ch, 'bt')), halves(part(ch, 'kt'))], axis=0)
        aa = jnp.where(tri, _bdot_nt(lhs, rhs), 0.0)
        a_ak[ch] = aa[:c, 2 * c:].astype(BF16)
        a_row[ch] = aa[c:, :].astype(BF16)
        abd = halves(aa[:c, :2 * c])
        pinv[ch] = eye2 + abd
        ak[ch] = abd
        vv[ch] = halves(part(ch, 'xv')).astype(BF16)
    nlev = int(math.log2(c))
    for lev in range(nlev):
        for ch in chains:
            akb = ak[ch].astype(BF16)
            if lev == 0:
                ak[ch] = _d(akb, akb)
            elif lev < nlev - 1:
                out = _d(akb, jnp.concatenate([akb, pinv[ch].astype(BF16)], axis=1))
                ak[ch] = out[:, :2 * c]
                pinv[ch] = pinv[ch] + out[:, 2 * c:]
            else:
                pinv[ch] = pinv[ch] + _d(akb, pinv[ch].astype(BF16))
    x2 = {ch: _d(a_ak[ch], vv[ch]) for ch in chains}

    sts = {ch: st_ref[ch[0] * npair + ch[1]] for ch in chains}
    xs = {ch: _d_nt(jnp.concatenate([part(ch, 'at'), part(ch, 'rt')], axis=0).astype(BF16),
                    sts[ch].astype(BF16)) for ch in chains}
    us = {}
    for ch in chains:
        u2 = _d(pinv[ch].astype(BF16), halves(xs[ch][:c] + x2[ch]).astype(BF16))
        us[ch] = u2[:c] + u2[c:]
    ys = {}
    for ch in chains:
        b, p = ch
        u = us[ch]
        ys[ch] = xs[ch][c:] + _d(a_row[ch], jnp.concatenate([halves(u).astype(BF16), vv[ch]], axis=0))
        upd = _d_tn(jnp.concatenate([u, part(ch, 'xv')], axis=0).astype(BF16),
                    jnp.concatenate([part(ch, 'bh'), part(ch, 'kh')], axis=0).astype(BF16))
        st_ref[b * npair + p] = sts[ch] * part(ch, 'gam')[0:1, :] + jnp.where(bd_p, upd, 0.0)

    inv_n = 1.0 / RW_HEAD
    y = jnp.concatenate([jnp.concatenate([ys[(b, p)] for p in range(npair)], axis=1)
                         for b in range(nb)], axis=0)
    mean = _segsum(y, ones_bd) * inv_n
    yc = y - mean
    var = _segsum(yc * yc, ones_bd) * inv_n
    yn = yc * lax.rsqrt(var + RW_GN_EPS) * vec(_RV_GNG) + vec(_RV_GNB)
    bonus = _segsum(xr * k2 * vec(_RV_RK), ones_bd) * xv
    out = (yn + bonus) * gate
    for b in range(nb):
        o_ref[b] = out[b * c:(b + 1) * c, :]


def _rwkv(p0, vec, mulo, w2p, a2p, g2):
    batch, seq, _ = p0.shape
    nc = seq // CHUNK
    c = CHUNK
    ones_bd = jnp.asarray(np.kron(np.eye(RW_WIDTH // RW_HEAD), np.ones((RW_HEAD, RW_HEAD))), BF16)

    def col(j, width):
        return pl.BlockSpec((batch, c, width), lambda i: (0, i, j))

    def full(shape):
        return pl.BlockSpec(shape, lambda i: (0,) * len(shape))

    return pl.pallas_call(
        _rwkv_body,
        grid=(nc,),
        in_specs=[col(0, 512), col(1, 512), col(2, 512), col(10, 256),
                  full((16, 512)), full((8, 256)), full((128, 512)), full((128, 512)),
                  full((128, 512)), full((512, 512))],
        out_specs=pl.BlockSpec((batch, c, 512), lambda i: (0, i, 0)),
        out_shape=jax.ShapeDtypeStruct((batch, seq, RW_WIDTH), F32),
        scratch_shapes=[pltpu.VMEM((batch, 8, 512), F32), pltpu.VMEM((batch, 8, 512), F32),
                        pltpu.VMEM((batch, 8, 512), F32), pltpu.VMEM((batch, 8, 256), F32),
                        pltpu.VMEM((batch * RW_WIDTH // LANES, LANES, LANES), F32)],
        compiler_params=_cparams(1),
        name="rwkv7",
    )(p0, p0, p0, p0, vec, mulo, w2p, a2p, g2, ones_bd)


def _gla_core(qs, ks, vs, gs, st_ref, heads_per_block):
    nb = len(qs)
    c = qs[0].shape[0]
    hpb = heads_per_block
    nblk = qs[0].shape[1] // LANES
    dk_shift = int(math.log2(LANES // hpb))
    row = _iota(qs[0].shape, 0)

    prep = []
    for q, k, g in zip(qs, ks, gs):
        b = _cumsum_rows(g)

        def brow(i, b=b):
            return jnp.broadcast_to(b[i:i + 1, :], b.shape)

        b15, b31, b47, blast = brow(15), brow(31), brow(47), brow(c - 1)
        ref_b = jnp.where(row < 32, b15, b47)
        ref_d = jnp.where(row < 16, 0.0, jnp.where(row < 32, b15, jnp.where(row < 48, b31, b47)))
        prep.append(dict(
            q_a=q * jnp.exp(jnp.minimum(b - b31, 0.0)), k_a=k * jnp.exp(jnp.minimum(b31 - b, 0.0)),
            q_b=q * jnp.exp(jnp.minimum(b - ref_b, 0.0)), k_b=k * jnp.exp(jnp.minimum(ref_b - b, 0.0)),
            q_d=q * jnp.exp(b - ref_d), k_d=k * jnp.exp(ref_d - b),
            q_i=q * jnp.exp(b), k_s=k * jnp.exp(blast - b), gam=jnp.exp(b[c - 1:c, :])))

    ri = _iota((hpb * c, c), 0) & (c - 1)
    ci = _iota((hpb * c, c), 1)
    mask_a = (ri >= 32) & (ci < 32)
    mask_b = ((ri >> 5) == (ci >> 5)) & (((ri >> 4) & 1) == 1) & (((ci >> 4) & 1) == 0)
    mask_d = ((ri >> 4) == (ci >> 4)) & (ri >= ci)
    lane = _iota((c, LANES), 1)
    bd = (_iota((hpb * LANES, LANES), 0) >> 7) == (_iota((hpb * LANES, LANES), 1) >> dk_shift)

    def heads_rows(x):
        if hpb == 1:
            return x
        return jnp.concatenate([jnp.where((lane >> dk_shift) == h, x, 0.0) for h in range(hpb)], axis=0)

    chains = [(bi, blk) for bi in range(nb) for blk in range(nblk)]

    def part(ch, name):
        bi, blk = ch
        return prep[bi][name][:, blk * LANES:(blk + 1) * LANES]

    v_p = {ch: vs[ch[0]][:, ch[1] * hpb * LANES:(ch[1] + 1) * hpb * LANES].astype(BF16) for ch in chains}
    probs = {}
    for ch in chains:
        s_a = _bdot_nt(heads_rows(part(ch, 'q_a')), part(ch, 'k_a'))
        s_b = _bdot_nt(heads_rows(part(ch, 'q_b')), part(ch, 'k_b'))
        s_d = _bdot_nt(heads_rows(part(ch, 'q_d')), part(ch, 'k_d'))
        probs[ch] = (jnp.where(mask_a, s_a, 0.0) + jnp.where(mask_b, s_b, 0.0)
                     + jnp.where(mask_d, s_d, 0.0)).astype(BF16)
    outs = {}
    for ch in chains:
        pv = _d(probs[ch], v_p[ch])
        o = pv[:c]
        for h in range(1, hpb):
            o = jnp.where((_iota(o.shape, 1) >> 7) == h, pv[h * c:(h + 1) * c], o)
        si = ch[0] * nblk + ch[1]
        sp = st_ref[si]
        outs[ch] = o + _bdot_nt(part(ch, 'q_i'), sp)
        upd = _d_tn(v_p[ch], part(ch, 'k_s').astype(BF16))
        st_ref[si] = sp * part(ch, 'gam') + jnp.where(bd, upd, 0.0)
    return [jnp.concatenate([outs[(bi, blk)] for blk in range(nblk)], axis=1) for bi in range(nb)]


def _gated_rmsnorm(o, gate, norm_g):
    nh = o.shape[1] // LANES
    outs = []
    for h in range(nh):
        sl = slice(h * LANES, (h + 1) * LANES)
        oh = o[:, sl]
        ms = jnp.mean(oh * oh, axis=-1, keepdims=True)
        outs.append(oh * lax.rsqrt(ms + NORM_EPS) * norm_g * _silu(gate[:, sl]))
    return jnp.concatenate(outs, axis=1)


def _gla_body(q_ref, k_ref, v_ref, gate_ref, gk_ref, w2_ref, vec_ref, ng_ref, o_ref, st_ref):
    nb = q_ref.shape[0]

    @pl.when(pl.program_id(0) == 0)
    def _():
        st_ref[...] = jnp.zeros_like(st_ref)

    gs = [-_softplus(-(_hdot(gk_ref[b], w2_ref[...]) + vec_ref[0:1, :])) * (1.0 / GLA_GATE_TAU)
          for b in range(nb)]
    qs = [q_ref[b] * (GLA_DK ** -0.5) for b in range(nb)]
    os_ = _gla_core(qs, [k_ref[b] for b in range(nb)], [v_ref[b] for b in range(nb)], gs, st_ref, 2)
    for b in range(nb):
        o_ref[b] = _gated_rmsnorm(os_[b], gate_ref[b], ng_ref[0:1, :])


def _seq_specs(batch, c):
    def col(j, width):
        return pl.BlockSpec((batch, c, width), lambda i: (0, i, j))

    def full(shape):
        return pl.BlockSpec(shape, lambda i: (0,) * len(shape))

    return col, full


def _gla(p0, gk_w2p, gk_b, norm_g):
    batch, seq, _ = p0.shape
    c = CHUNK
    col, full = _seq_specs(batch, c)
    return pl.pallas_call(
        _gla_body,
        grid=(seq // c,),
        in_specs=[col(11, 256), col(12, 256), col(3, 512), col(4, 512), col(26, 128),
                  full((128, 256)), full((8, 256)), full((8, 128))],
        out_specs=pl.BlockSpec((batch, c, 512), lambda i: (0, i, 0)),
        out_shape=jax.ShapeDtypeStruct((batch, seq, 512), F32),
        scratch_shapes=[pltpu.VMEM((batch * 2, 2 * LANES, LANES), F32)],
        compiler_params=_cparams(1),
        name="gla",
    )(p0, p0, p0, p0, p0, gk_w2p, gk_b, norm_g)


def _hgrn_body(q_ref, f_ref, i_ref, gate_ref, lb_ref, ng_ref, o_ref, st_ref):
    nb = q_ref.shape[0]

    @pl.when(pl.program_id(0) == 0)
    def _():
        st_ref[...] = jnp.zeros_like(st_ref)

    lb = lb_ref[0:1, :]
    qs, ks, gs = [], [], []
    for b in range(nb):
        f = f_ref[b]
        gs.append(jnp.log(lb + (1.0 - lb) * _sigmoid(f)))
        ks.append((1.0 - lb) * _sigmoid(-f))
        qs.append(_silu(q_ref[b]))
    os_ = _gla_core(qs, ks, [i_ref[b] for b in range(nb)], gs, st_ref, 1)
    for b in range(nb):
        o_ref[b] = _gated_rmsnorm(os_[b], gate_ref[b], ng_ref[0:1, :])


def _hgrn(p1, lb, norm_g):
    batch, seq, _ = p1.shape
    c = CHUNK
    col, full = _seq_specs(batch, c)
    return pl.pallas_call(
        _hgrn_body,
        grid=(seq // c,),
        in_specs=[col(1, 512), col(2, 512), col(3, 512), col(4, 512), full((8, 512)), full((8, 128))],
        out_specs=pl.BlockSpec((batch, c, 512), lambda i: (0, i, 0)),
        out_shape=jax.ShapeDtypeStruct((batch, seq, 512), F32),
        scratch_shapes=[pltpu.VMEM((batch * HG_HEADS, LANES, LANES), F32)],
        compiler_params=_cparams(1),
        name="hgrn2",
    )(p1, p1, p1, p1, lb, norm_g)


def _s5_body(u_ref, dcat_ref, bm_ref, eblk_ref, pw_ref, o_ref, toep_ref, e_ref):
    seq = u_ref.shape[0]
    rows = seq // S5_CHUNK

    @pl.when(pl.program_id(1) == 0)
    def _():
        for tk in range(S5_CHUNK):
            lo = (S5_CHUNK - 1 - tk) * LANES
            toep_ref[tk * LANES:(tk + 1) * LANES, :] = dcat_ref[0, :, lo:lo + S5_CHUNK * LANES]
            e_ref[:, tk * LANES:(tk + 1) * LANES] = eblk_ref[0, tk]

    u = jnp.concatenate([u_ref[pl.ds(tk, rows, stride=S5_CHUNK), :] for tk in range(S5_CHUNK)],
                        axis=1).astype(BF16)
    s = _d(u, bm_ref[0])
    rin = _iota(s.shape, 0)
    pw = pw_ref[0]

    def swap_halves(x):
        return jnp.concatenate([pltpu.roll(x[:, k * LANES:(k + 1) * LANES], S5_STATE, 1)
                                for k in range(x.shape[1] // LANES)], axis=1)

    h = s
    for lev in range(int(math.log2(rows))):
        sh = 1 << lev
        hs = jnp.where(rin >= sh, pltpu.roll(h, sh, 0), 0.0)
        h = h + pw[2 * lev:2 * lev + 1, :] * hs + pw[2 * lev + 1:2 * lev + 2, :] * swap_halves(hs)
    hprev = jnp.where(rin >= 1, pltpu.roll(h, 1, 0), 0.0)
    y = _d(u, toep_ref[...]) + _bdot(hprev, e_ref[...])
    for tk in range(S5_CHUNK):
        o_ref[pl.ds(tk, rows, stride=S5_CHUNK), :] = y[:, tk * LANES:(tk + 1) * LANES]


def _s5_scan(p1, dcat, bmat, emat, pw):
    batch, seq, _ = p1.shape
    nt, width, nstate = bmat.shape

    def table(shape):
        return pl.BlockSpec((1,) + shape, lambda j, b: (j, 0, 0), pipeline_mode=pl.Buffered(1))

    return pl.pallas_call(
        _s5_body,
        grid=(nt, batch),
        in_specs=[pl.BlockSpec((None, seq, LANES), lambda j, b: (b, 0, j)),
                  table(dcat.shape[1:]), table((width, nstate)),
                  pl.BlockSpec((1,) + emat.shape[1:], lambda j, b: (j, 0, 0, 0), pipeline_mode=pl.Buffered(1)),
                  table((S5_SCAN_ROWS, nstate))],
        out_specs=pl.BlockSpec((None, seq, LANES), lambda j, b: (b, 0, j)),
        out_shape=jax.ShapeDtypeStruct((batch, seq, nt * LANES), F32),
        scratch_shapes=[pltpu.VMEM((width, width), BF16), pltpu.VMEM((nstate, width), BF16)],
        compiler_params=_cparams(2),
        name="s5_scan",
    )(p1, dcat, bmat, emat, pw)


def _s5_post_body(y_ref, u_ref, vec_ref, w_ref, o_ref):
    y = y_ref[...] + vec_ref[0:1, :] * u_ref[...]
    y = 0.5 * y * (1.0 + jnp.tanh(math.sqrt(2.0 / math.pi) * (y + 0.044715 * (y * y * y))))
    o_ref[...] = y * _sigmoid(_bdot(y, w_ref[...]) + vec_ref[1:2, :])


def _s5_post(y_ssm, p1, vec, glu_w, tm):
    t = y_ssm.shape[0]
    return pl.pallas_call(
        _s5_post_body,
        grid=(t // tm,),
        in_specs=[pl.BlockSpec((tm, 512), lambda i: (i, 0)),
                  pl.BlockSpec((tm, 512), lambda i: (i, 0)),
                  pl.BlockSpec((8, 512), lambda i: (0, 0)),
                  pl.BlockSpec((512, 512), lambda i: (0, 0))],
        out_specs=pl.BlockSpec((tm, 512), lambda i: (i, 0)),
        out_shape=jax.ShapeDtypeStruct((t, 512), F32),
        compiler_params=_cparams(1),
        name="s5_post",
    )(y_ssm, p1, vec, glu_w)


def _s5_tables(a_re, a_im, log_dt, b_re, b_im, c_re, c_im, rows_per_seq):
    c = S5_CHUNK
    lam_re = jnp.minimum(a_re, -1e-4)
    lam_im = a_im
    dt = jnp.exp(log_dt)[:, None]
    mag = jnp.exp(lam_re * dt)
    abar_re = mag * jnp.cos(lam_im * dt)
    abar_im = mag * jnp.sin(lam_im * dt)
    den = lam_re * lam_re + lam_im * lam_im
    num_re = abar_re - 1.0
    z_re = (num_re * lam_re + abar_im * lam_im) / den
    z_im = (abar_im * lam_re - num_re * lam_im) / den

    def power(n):
        n = jnp.asarray(n, F32)[..., None, None]
        m = jnp.exp(n * (lam_re * dt))
        return m * jnp.cos(n * (lam_im * dt)), m * jnp.sin(n * (lam_im * dt))

    def cmul(ar, ai, br, bi):
        return ar * br - ai * bi, ar * bi + ai * br

    tau = jnp.arange(c)
    p_re, p_im = power(tau)
    zb_re, zb_im = cmul(z_re[..., None], z_im[..., None], b_re, b_im)
    cp_re, cp_im = cmul(c_re[None], c_im[None], p_re[:, :, None, :], p_im[:, :, None, :])
    hi = lax.Precision.HIGHEST
    kern = (jnp.einsum('tgon,gni->gtoi', cp_re, zb_re, precision=hi)
            - jnp.einsum('tgon,gni->gtoi', cp_im, zb_im, precision=hi))
    q_re, q_im = power(c - 1 - tau)
    bm_re, bm_im = cmul(q_re[..., None], q_im[..., None], zb_re[None], zb_im[None])
    bmat = jnp.concatenate([jnp.transpose(bm_re, (1, 0, 3, 2)), jnp.transpose(bm_im, (1, 0, 3, 2))],
                           axis=-1).reshape(S5_GROUPS, c * S5_GROUP, 2 * S5_STATE)
    r_re, r_im = power(tau + 1)
    nlev = int(math.log2(rows_per_seq))
    assert 2 * nlev <= S5_SCAN_ROWS
    s_re, s_im = power(c * (2 ** jnp.arange(nlev)))
    pw = jnp.zeros((S5_GROUPS, S5_SCAN_ROWS, 2 * S5_STATE), F32)
    pw = pw.at[:, 0:2 * nlev:2, :].set(jnp.transpose(jnp.concatenate([s_re, s_re], -1), (1, 0, 2)))
    pw = pw.at[:, 1:2 * nlev:2, :].set(jnp.transpose(jnp.concatenate([-s_im, s_im], -1), (1, 0, 2)))
    nt, gl = S5_GROUPS // S5_TILE_GROUPS, S5_TILE_GROUPS
    spread = jnp.tile(jnp.eye(S5_GROUP, dtype=F32), (1, gl))
    lane_group = jnp.arange(LANES) // S5_GROUP
    kern5 = jnp.transpose(kern.reshape(nt, gl, c, S5_GROUP, S5_GROUP), (0, 2, 1, 4, 3))
    dblk = jnp.einsum('jtgio,op->jtgip', kern5, spread, precision=hi)
    dblk = jnp.where((jnp.arange(gl)[:, None, None] == lane_group[None, None, :]), dblk, 0.0)
    dblk = dblk.reshape(nt, c, LANES, LANES)
    dcat = jnp.transpose(dblk, (0, 2, 1, 3)).reshape(nt, LANES, c * LANES)
    dcat = jnp.pad(dcat, ((0, 0), (0, 0), ((c - 1) * LANES, 0)))
    bmat5 = jnp.transpose(bmat.reshape(nt, gl, c, S5_GROUP, 2 * S5_STATE), (0, 2, 1, 3, 4))
    row_group = (jnp.arange(c * LANES) // S5_GROUP) % gl
    col_tile = jnp.arange(gl * 2 * S5_STATE) // (2 * S5_STATE)
    bmat_t = jnp.where(row_group[:, None] == col_tile[None, :],
                       jnp.tile(bmat5.reshape(nt, c * LANES, 2 * S5_STATE), (1, 1, gl)), 0.0)
    own_group = (jnp.arange(S5_GROUPS) % gl)[:, None, None] == lane_group[None, None, :]

    def spread_c(cmat):
        cs = jnp.where(own_group, jnp.einsum('gon,op->gnp', cmat, spread, precision=hi), 0.0)
        return cs.reshape(nt, 1, gl, S5_STATE, LANES)

    def per_state(pw_):
        return jnp.transpose(pw_.reshape(c, nt, gl, S5_STATE), (1, 0, 2, 3))[..., None]

    cs_re, cs_im, rr, ri = spread_c(c_re), spread_c(c_im), per_state(r_re), per_state(r_im)
    eblk = jnp.concatenate([cs_re * rr - cs_im * ri, -(cs_re * ri + cs_im * rr)], axis=3)
    eblk = eblk.reshape(nt, c, gl * 2 * S5_STATE, LANES)
    pw_t = jnp.transpose(pw.reshape(nt, gl, S5_SCAN_ROWS, 2 * S5_STATE), (0, 2, 1, 3))
    pw_t = pw_t.reshape(nt, S5_SCAN_ROWS, gl * 2 * S5_STATE)
    return dcat.astype(BF16), bmat_t.astype(BF16), eblk.astype(BF16), pw_t


def _proj_route_body(ya_ref, yb_ref, x_ref, w_ref, ln_ref, wr_ref, br_ref,
                     h_ref, route_ref, cnt_ref, carry_ref):
    tm = ya_ref.shape[0]
    half = w_ref.shape[0] // 2

    @pl.when(pl.program_id(0) == 0)
    def _():
        carry_ref[...] = jnp.zeros_like(carry_ref)

    mix = _bdot(ya_ref[...], w_ref[:half, :]) + _bdot(yb_ref[...], w_ref[half:, :])
    h = _layer_norm(DN_ALPHA * x_ref[...] + mix, ln_ref[0:1, :], ln_ref[1:2, :])
    h_ref[...] = h

    logits = _hdot(h, wr_ref[...]) + br_ref[0:1, :]
    lane = _iota((tm, LANES), 1).astype(F32)
    neg = -jnp.inf

    def softmax_masked(mask):
        xm = jnp.where(mask, logits, neg)
        m = jnp.max(xm, axis=-1, keepdims=True)
        e = jnp.exp(xm - m)
        return e / jnp.sum(e, axis=-1, keepdims=True)

    def top1(pm):
        m = jnp.max(pm, axis=-1, keepdims=True)
        idx = jnp.min(jnp.where(pm == m, lane, float(LANES)), axis=-1, keepdims=True)
        return m, idx

    coarse = jnp.where(lane < MOE_GROUPS, softmax_masked(lane < MOE_GROUPS), -1.0)
    p_grp, grp = top1(coarse)
    lo = MOE_GROUPS + MOE_PER_GROUP * grp
    fmask = (lane >= lo) & (lane < lo + MOE_PER_GROUP)
    fine = jnp.where(fmask, softmax_masked(fmask), -1.0)
    p1, j1 = top1(fine)
    p2, j2 = top1(jnp.where(lane == j1, -1.0, fine))
    denom = p1 + p2
    g1 = p_grp * (p1 / denom)
    g2 = p_grp * (p2 / denom)
    e1 = j1 - MOE_GROUPS
    e2 = j2 - MOE_GROUPS

    oh1 = jnp.where(lane == e1, 1.0, 0.0)
    oh2 = jnp.where(lane == e2, 1.0, 0.0)
    cnt = oh1 + oh2
    strict = jnp.where(_iota((tm, tm), 0) > _iota((tm, tm), 1), 1.0, 0.0).astype(BF16)
    before = _d(strict, cnt.astype(BF16)) + carry_ref[0:1, :]
    r1 = jnp.sum(oh1 * before, axis=-1, keepdims=True)
    r2 = jnp.sum(oh2 * before, axis=-1, keepdims=True)
    carry_ref[0:1, :] = carry_ref[0:1, :] + jnp.sum(cnt, axis=0, keepdims=True)
    cnt_ref[...] = carry_ref[...]

    out = jnp.where(lane == 0, e1, 0.0)
    out = jnp.where(lane == 1, e2, out)
    out = jnp.where(lane == 2, r1, out)
    out = jnp.where(lane == 3, r2, out)
    out = jnp.where(lane == 4, g1, out)
    out = jnp.where(lane == 5, g2, out)
    route_ref[...] = out[:, :8]


def _proj_route(ya, yb, resid, w_out, ln, wr, br, tm):
    t, d = resid.shape
    return pl.pallas_call(
        _proj_route_body,
        grid=(t // tm,),
        in_specs=[pl.BlockSpec((tm, 512), lambda i: (i, 0)),
                  pl.BlockSpec((tm, 512), lambda i: (i, 0)),
                  pl.BlockSpec((tm, d), lambda i: (i, 0)),
                  pl.BlockSpec((d, d), lambda i: (0, 0)),
                  pl.BlockSpec((8, d), lambda i: (0, 0)),
                  pl.BlockSpec((d, LANES), lambda i: (0, 0)),
                  pl.BlockSpec((8, LANES), lambda i: (0, 0))],
        out_specs=[pl.BlockSpec((tm, d), lambda i: (i, 0)),
                   pl.BlockSpec((tm, 8), lambda i: (i, 0)),
                   pl.BlockSpec((8, LANES), lambda i: (0, 0))],
        out_shape=[jax.ShapeDtypeStruct((t, d), F32),
                   jax.ShapeDtypeStruct((t, 8), F32),
                   jax.ShapeDtypeStruct((8, LANES), F32)],
        scratch_shapes=[pltpu.VMEM((8, LANES), F32)],
        compiler_params=_cparams(1),
        name="proj_ln_route",
    )(ya, yb, resid, w_out, ln, wr, br)


def _row_copy(src_ref, src_row, dst_ref, dst_row, sem):
    return pltpu.make_async_copy(src_ref.at[pl.ds(src_row, 1)], dst_ref.at[pl.ds(dst_row, 1)], sem)


def _start_all(copies):
    for n, cp in enumerate(copies):
        cp.start(priority=n % 2)


def _dispatch_body(pad_start_ref, pad_len_ref, used_rows_ref, dest_ref, h_ref, rows_ref, hbuf_ref, zbuf_ref,
                   lsem, ssem, zsem):
    i = pl.program_id(0)
    last = pl.num_programs(0) - 1
    tm = hbuf_ref.shape[1]
    slot = i % 2

    def load(tile, s):
        return pltpu.make_async_copy(h_ref.at[pl.ds(tile * tm, tm)], hbuf_ref.at[s], lsem.at[s])

    def scatters(s):
        return [_row_copy(hbuf_ref.at[s], r, rows_ref, dest_ref[0, 0, 2 * r + q], ssem.at[s])
                for r in range(tm) for q in range(2)]

    @pl.when(i == 0)
    def _():
        load(0, 0).start()

    @pl.when(i >= 1)
    def _():
        for cp in scatters(1 - slot):
            cp.wait()

    @pl.when(i < last)
    def _():
        load(i + 1, 1 - slot).start()

    load(i, slot).wait()
    _start_all(scatters(slot))

    @pl.when(i == last)
    def _():
        for cp in scatters(slot):
            cp.wait()
        zbuf_ref[...] = jnp.zeros_like(zbuf_ref)
        sub = 8
        half = zbuf_ref.shape[0]

        def pad_fills():
            out = []
            for e in range(N_EXPERTS):
                start = pad_start_ref[e]
                npad = pad_len_ref[e]
                end = start + npad
                run = half
                while run >= sub:
                    end = end - (npad & run)
                    dst = rows_ref.at[pl.ds(pl.multiple_of(end, run), run)]
                    out.append(((npad & run) != 0,
                                pltpu.make_async_copy(zbuf_ref.at[pl.ds(0, run)], dst, zsem)))
                    run //= 2
                for k in range(sub - 1):
                    out.append((k < (npad & (sub - 1)), _row_copy(zbuf_ref, 0, rows_ref, start + k, zsem)))
            return out

        for pred, cp in pad_fills():
            pl.when(pred)(cp.start)
        for pred, cp in pad_fills():
            pl.when(pred)(cp.wait)

        def tail(b):
            return pltpu.make_async_copy(
                zbuf_ref, rows_ref.at[pl.ds(pl.multiple_of(b * half, half), half)], zsem)

        first_free = used_rows_ref[0] // half
        n_half = rows_ref.shape[0] // half
        lax.fori_loop(first_free, n_half, lambda b, c: (tail(b).start(), c)[1], 0)
        lax.fori_loop(first_free, n_half, lambda b, c: (tail(b).wait(), c)[1], 0)


def _dispatch(pad_start, pad_len, used_rows, dest3, h, n_rows):
    t, d = h.shape
    nt, _, tm2 = dest3.shape
    tm = tm2 // 2
    grid_spec = pltpu.PrefetchScalarGridSpec(
        num_scalar_prefetch=3,
        grid=(nt,),
        in_specs=[pl.BlockSpec((1, 1, tm2), lambda i, *_: (i, 0, 0), memory_space=pltpu.SMEM),
                  pl.BlockSpec(memory_space=pl.ANY)],
        out_specs=pl.BlockSpec(memory_space=pl.ANY),
        scratch_shapes=[pltpu.VMEM((2, tm, d), F32), pltpu.VMEM((MOE_BLOCK // 2, d), F32),
                        pltpu.SemaphoreType.DMA((2,)), pltpu.SemaphoreType.DMA((2,)),
                        pltpu.SemaphoreType.DMA],
    )
    return pl.pallas_call(
        _dispatch_body,
        grid_spec=grid_spec,
        out_shape=jax.ShapeDtypeStruct((n_rows, d), F32),
        compiler_params=_cparams(1),
        name="moe_dispatch",
    )(pad_start, pad_len, used_rows, dest3, h)


def _expert_body(first_ref, nblk_ref, nused_ref, x_ref, w1_ref, w3_ref, w2_ref, y_ref,
                 xbuf_ref, ybuf_ref, w1b_ref, w3b_ref, w2b_ref, ypend_ref, ycnt_ref, xsem, ysem):
    e = pl.program_id(0)
    row0 = first_ref[e] * MOE_BLOCK
    nblk = nblk_ref[e]
    big = xbuf_ref.shape[1]
    per_big = big // MOE_BLOCK

    @pl.when(e == 0)
    def _():
        ypend_ref[0] = 0
        ypend_ref[1] = 0
        ycnt_ref[0] = 0

    def drain(slot):
        for size in (big, MOE_BLOCK):
            @pl.when(ypend_ref[slot] == size)
            def _(size=size):
                pltpu.make_async_copy(ybuf_ref.at[slot, pl.ds(0, size)], y_ref.at[pl.ds(0, size)],
                                      ysem.at[slot]).wait()
        ypend_ref[slot] = 0

    def cast_weights():
        w1b_ref[...] = w1_ref[0].astype(BF16)
        w3b_ref[...] = w3_ref[0].astype(BF16)
        w2b_ref[...] = w2_ref[0].astype(BF16)

    def stream(start_row, count, size, casts_first):
        def x_copy(j, s):
            src = x_ref.at[pl.ds(pl.multiple_of(start_row + j * size, MOE_BLOCK), size)]
            return pltpu.make_async_copy(src, xbuf_ref.at[s, pl.ds(0, size)], xsem.at[s])

        def y_copy(j, s):
            dst = y_ref.at[pl.ds(pl.multiple_of(start_row + j * size, MOE_BLOCK), size)]
            return pltpu.make_async_copy(ybuf_ref.at[s, pl.ds(0, size)], dst, ysem.at[s])

        @pl.when(count > 0)
        def _():
            x_copy(0, 0).start(priority=1)
            pl.when(casts_first)(cast_weights)

            def chunk(j, carry):
                s = j % 2

                @pl.when(j + 1 < count)
                def _():
                    x_copy(j + 1, 1 - s).start(priority=1)

                x_copy(j, s).wait()
                xb = xbuf_ref[s, pl.ds(0, size), :].astype(BF16)
                hid = _silu(_d(xb, w1b_ref[...])) * _d(xb, w3b_ref[...])
                ys = ycnt_ref[0] % 2
                drain(ys)
                ybuf_ref[ys, pl.ds(0, size), :] = _d(hid.astype(BF16), w2b_ref[...])
                y_copy(j, ys).start()
                ypend_ref[ys] = size
                ycnt_ref[0] = ycnt_ref[0] + 1
                return carry

            lax.fori_loop(0, count, chunk, 0)

    @pl.when(nblk > 0)
    def _():
        nbig = nblk // per_big
        stream(row0, nbig, big, nbig > 0)
        stream(row0 + nbig * big, nblk - nbig * per_big, MOE_BLOCK, nbig == 0)

    @pl.when(e == pl.num_programs(0) - 1)
    def _():
        drain(0)
        drain(1)
        ybuf_ref[0, pl.ds(0, MOE_BLOCK), :] = jnp.zeros((MOE_BLOCK, ybuf_ref.shape[2]), F32)
        nused = nused_ref[0]
        ntot = y_ref.shape[0] // MOE_BLOCK

        def fill(b, carry):
            cp = pltpu.make_async_copy(ybuf_ref.at[0, pl.ds(0, MOE_BLOCK)],
                                       y_ref.at[pl.ds(b * MOE_BLOCK, MOE_BLOCK)], ysem.at[0])
            cp.start()
            cp.wait()
            return carry

        lax.fori_loop(nused, ntot, fill, 0)


def _experts(first_blk, nblk, nused, x_rows, w1, w3, w2, layer):
    r, d = x_rows.shape
    hid = w1.shape[-1]
    grid_spec = pltpu.PrefetchScalarGridSpec(
        num_scalar_prefetch=3,
        grid=(N_EXPERTS,),
        in_specs=[pl.BlockSpec(memory_space=pl.ANY),
                  pl.BlockSpec((None, 1, d, hid), lambda e, *_: (layer, e, 0, 0)),
                  pl.BlockSpec((None, 1, d, hid), lambda e, *_: (layer, e, 0, 0)),
                  pl.BlockSpec((None, 1, hid, d), lambda e, *_: (layer, e, 0, 0))],
        out_specs=pl.BlockSpec(memory_space=pl.ANY),
        scratch_shapes=[pltpu.VMEM((2, EXPERT_CHUNK, d), F32), pltpu.VMEM((2, EXPERT_CHUNK, d), F32),
                        pltpu.VMEM((d, hid), BF16), pltpu.VMEM((d, hid), BF16), pltpu.VMEM((hid, d), BF16),
                        pltpu.SMEM((2,), jnp.int32), pltpu.SMEM((1,), jnp.int32),
                        pltpu.SemaphoreType.DMA((2,)), pltpu.SemaphoreType.DMA((2,))],
    )
    return pl.pallas_call(
        _expert_body,
        grid_spec=grid_spec,
        out_shape=jax.ShapeDtypeStruct((r, d), F32),
        compiler_params=_cparams(1),
        name="moe_experts",
    )(first_blk, nblk, nused, x_rows, w1, w3, w2)


def _combine_body(dest_ref, dest_next_ref, gate_ref, h_ref, ln_ref, rows_ref, o_ref, buf_ref, sem):
    i = pl.program_id(0)
    last = pl.num_programs(0) - 1
    tm = h_ref.shape[0]

    def gather(dref, slot):
        return [_row_copy(rows_ref, dref[0, 0, 2 * r + s], buf_ref.at[slot, s], r, sem.at[slot])
                for r in range(tm) for s in range(2)]

    @pl.when(i == 0)
    def _():
        _start_all(gather(dest_ref, 0))

    _start_all(gather(dest_next_ref, (i + 1) % 2))
    slot = i % 2
    for cp in gather(dest_ref, slot):
        cp.wait()
    gate = gate_ref[...]
    y = gate[:, 4:5] * buf_ref[slot, 0] + gate[:, 5:6] * buf_ref[slot, 1]
    o_ref[...] = _layer_norm(DN_ALPHA * h_ref[...] + y, ln_ref[0:1, :], ln_ref[1:2, :])

    @pl.when(i == last)
    def _():
        for cp in gather(dest_next_ref, (i + 1) % 2):
            cp.wait()


def _combine(dest3, route, h, ln, y_rows, tm):
    t, d = h.shape
    nt = t // tm
    return pl.pallas_call(
        _combine_body,
        grid=(nt,),
        in_specs=[pl.BlockSpec((1, 1, 2 * tm), lambda i: (i, 0, 0), memory_space=pltpu.SMEM),
                  pl.BlockSpec((1, 1, 2 * tm), lambda i: (jnp.minimum(i + 1, nt - 1), 0, 0),
                               memory_space=pltpu.SMEM),
                  pl.BlockSpec((tm, 8), lambda i: (i, 0)),
                  pl.BlockSpec((tm, d), lambda i: (i, 0)),
                  pl.BlockSpec((8, d), lambda i: (0, 0)),
                  pl.BlockSpec(memory_space=pl.ANY)],
        out_specs=pl.BlockSpec((tm, d), lambda i: (i, 0)),
        out_shape=jax.ShapeDtypeStruct((t, d), F32),
        scratch_shapes=[pltpu.VMEM((2, 2, tm, d), F32), pltpu.SemaphoreType.DMA((2,))],
        compiler_params=_cparams(1),
        name="moe_combine_ln",
    )(dest3, dest3, route, h, ln, y_rows)


def _moe(h, route, counts, w1, w3, w2, layer, ln):
    t, d = h.shape
    tm = min(COMBINE_TILE, t)
    a = 2 * t
    expert = route[:, 0:2].astype(jnp.int32)
    rank = route[:, 2:4].astype(jnp.int32)
    cnt = counts[0, :N_EXPERTS].astype(jnp.int32)
    padded = (cnt + MOE_BLOCK - 1) // MOE_BLOCK * MOE_BLOCK
    pend = jnp.cumsum(padded)
    pstart = pend - padded
    sel = expert[..., None] == jnp.arange(N_EXPERTS, dtype=jnp.int32)
    dest = jnp.sum(jnp.where(sel, pstart, 0), axis=-1) + rank
    n_blocks = -(-a // MOE_BLOCK) + N_EXPERTS
    nused = (pend[-1:] // MOE_BLOCK).astype(jnp.int32)
    dest3 = dest.reshape(t // tm, 1, 2 * tm)
    x_rows = _dispatch((pstart + cnt).astype(jnp.int32), (padded - cnt).astype(jnp.int32),
                       pend[-1:].astype(jnp.int32), dest3, h, n_blocks * MOE_BLOCK)
    y_rows = _experts((pstart // MOE_BLOCK).astype(jnp.int32), (padded // MOE_BLOCK).astype(jnp.int32),
                      nused, x_rows, w1, w3, w2, layer)
    return _combine(dest3, route, h, ln, y_rows, tm)


def _pad_rows(x, rows):
    return jnp.zeros((rows,) + x.shape[1:], x.dtype).at[:x.shape[0]].set(x)


def _route_weights(wg, bg, we, be):
    d = wg.shape[0]
    wr = jnp.zeros((d, LANES), F32)
    wr = wr.at[:, :MOE_GROUPS].set(wg)
    wr = wr.at[:, MOE_GROUPS:MOE_GROUPS + N_EXPERTS].set(jnp.transpose(we, (1, 0, 2)).reshape(d, N_EXPERTS))
    br = jnp.zeros((8, LANES), F32)
    br = br.at[0, :MOE_GROUPS].set(bg)
    br = br.at[0, MOE_GROUPS:MOE_GROUPS + N_EXPERTS].set(be.reshape(N_EXPERTS))
    return wr, br


def kernel(x, ab_w_in, rw_mu, rw_w0, rw_w2, rw_a0, rw_a2, rw_g2, rw_k_k, rw_k_a, rw_r_k, rw_gn_g, rw_gn_b, gla_gk_w2, gla_gk_b, gla_norm_g, ab_w_out, cd_w_in, s5_a_re, s5_a_im, s5_log_dt, s5_b_re, s5_b_im, s5_c_re, s5_c_im, s5_d, s5_glu_w, s5_glu_b, hg_lb, hg_norm_g, cd_w_out, ln1_g, ln1_b, moe_wg, moe_bg, moe_we, moe_be, moe_w1, moe_w3, moe_w2, ln2_g, ln2_b):
    batch, seq, d = x.shape
    t = batch * seq
    assert d == D_MODEL and seq % CHUNK == 0
    rows_per_seq = seq // S5_CHUNK
    assert rows_per_seq & (rows_per_seq - 1) == 0, "S5 chunk scan assumes a power-of-two chunk count"
    tm = min(ROW_TILE, t)
    assert t % tm == 0
    xt = x.reshape(t, d)
    ln1 = [_pad_rows(jnp.stack([ln1_g[l], ln1_b[l]]), 8) for l in range(DEPTH)]
    ln2 = [_pad_rows(jnp.stack([ln2_g[l], ln2_b[l]]), 8) for l in range(DEPTH)]

    j = 0
    w = ab_w_in[j]
    r_, wl_, k_, v_, al_, gl_ = 0, 512, 576, 1088, 1600, 1664
    gq, gk, gv, glow, ggate = 1792, 2048, 2304, 2816, 2832
    w0cols = jnp.concatenate([
        w[:, r_:r_ + 512], w[:, k_:k_ + 512], w[:, v_:v_ + 512], w[:, gv:gv + 512], w[:, ggate:ggate + 512],
        w[:, wl_:wl_ + 64], w[:, al_:al_ + 64], w[:, gl_:gl_ + 128], w[:, gq:gq + 256], w[:, gk:gk + 256],
        w[:, glow:glow + 16], jnp.zeros((d, 112), F32)], axis=1).astype(BF16)
    p0 = _matmul(xt, w0cols, tm)

    mu = rw_mu[j]
    vec = _pad_rows(jnp.stack([mu[r_:r_ + 512], mu[k_:k_ + 512], mu[v_:v_ + 512], rw_w0[j], rw_a0[j],
                               rw_k_k[j], rw_k_a[j], rw_r_k[j].reshape(-1), rw_gn_g[j], rw_gn_b[j]]), 16)
    mulo = _pad_rows(jnp.concatenate([mu[wl_:wl_ + 64], mu[al_:al_ + 64], mu[gl_:gl_ + 128]])[None], 8)
    w2p = _pad_rows(rw_w2[j], 128)
    a2p = jnp.zeros((128, 512), F32).at[64:].set(rw_a2[j])
    p0 = p0.reshape(batch, seq, -1)
    y_rw = _rwkv(p0, vec, mulo, w2p, a2p, rw_g2[j]).reshape(t, RW_WIDTH)
    y_gla = _gla(p0, _pad_rows(gla_gk_w2[j], 128), _pad_rows(gla_gk_b[j][None], 8),
                 _pad_rows(gla_norm_g[j][None], 8)).reshape(t, 512)

    wr, br = _route_weights(moe_wg[0], moe_bg[0], moe_we[0], moe_be[0])
    h, route, counts = _proj_route(y_rw, y_gla, xt, ab_w_out[j].astype(BF16), ln1[0], wr, br, tm)
    h = _moe(h, route, counts, moe_w1, moe_w3, moe_w2, 0, ln2[0])

    p1 = _matmul(h, cd_w_in[j].astype(BF16), tm)
    lb_sm = jax.nn.softmax(hg_lb.astype(F32), axis=0)
    lower = (jnp.cumsum(lb_sm, axis=0) - lb_sm[0])[1]
    y_hg = _hgrn(p1.reshape(batch, seq, -1), _pad_rows(lower[None], 8),
                 _pad_rows(hg_norm_g[j][None], 8)).reshape(t, 512)

    toep, bmat, emat, pw = _s5_tables(s5_a_re[j], s5_a_im[j], s5_log_dt[j], s5_b_re[j], s5_b_im[j],
                                      s5_c_re[j], s5_c_im[j], rows_per_seq)
    y_ssm = _s5_scan(p1.reshape(batch, seq, -1), toep, bmat, emat, pw).reshape(t, 512)
    y_s5 = _s5_post(y_ssm, p1, _pad_rows(jnp.stack([s5_d[j], s5_glu_b[j]]), 8),
                    s5_glu_w[j].astype(BF16), tm)

    wr, br = _route_weights(moe_wg[1], moe_bg[1], moe_we[1], moe_be[1])
    h2, route, counts = _proj_route(y_s5, y_hg, h, cd_w_out[j].astype(BF16), ln1[1], wr, br, tm)
    out = _moe(h2, route, counts, moe_w1, moe_w3, moe_w2, 1, ln2[1])
    return out.reshape(batch, seq, d)
```

```python
import functools
import math

import numpy as np
import jax
import jax.numpy as jnp
from jax import lax
from jax.experimental import pallas as pl
from jax.experimental.pallas import tpu as pltpu

F32 = jnp.float32
BF16 = jnp.bfloat16

D_MODEL = 1024
DEPTH = 2
RW_HEAD = 64
RW_WIDTH = 512
RW_GN_EPS = 64e-5
GLA_HEADS = 4
GLA_DK = 64
GLA_DV = 128
GLA_GATE_TAU = 16.0
S5_GROUP = 16
S5_GROUPS = 32
S5_STATE = 64
HG_HEADS = 4
HG_DK = 128
CHUNK = 64
S5_CHUNK = 16
S5_SCAN_ROWS = 32
S5_TILE_GROUPS = 8
NORM_EPS = 1e-5
MOE_GROUPS = 4
MOE_PER_GROUP = 8
N_EXPERTS = 32
EXPERT_HIDDEN = 512
MOE_BLOCK = 128
COMBINE_TILE = 128
EXPERT_CHUNK = 512
ROW_TILE = 512
DN_ALPHA = (2.0 * DEPTH) ** 0.25

LANES = 128
VMEM_LIMIT = 56 * 1024 * 1024


def _cparams(n_axes=1):
    return pltpu.CompilerParams(dimension_semantics=("arbitrary",) * n_axes,
                                vmem_limit_bytes=VMEM_LIMIT)


def _d(a, b):
    return jnp.dot(a, b, preferred_element_type=F32)


def _d_nt(a, b):
    return lax.dot_general(a, b, (((1,), (1,)), ((), ())), preferred_element_type=F32)


def _d_tn(a, b):
    return lax.dot_general(a, b, (((0,), (0,)), ((), ())), preferred_element_type=F32)


def _split(a):
    hi = a.astype(BF16)
    lo = (a - hi.astype(F32)).astype(BF16)
    return hi, lo


def _split3(a):
    hi = a.astype(BF16)
    r1 = a - hi.astype(F32)
    mid = r1.astype(BF16)
    lo = (r1 - mid.astype(F32)).astype(BF16)
    return hi, mid, lo


def _bdot(a, b):
    return _d(a.astype(BF16), b.astype(BF16))


def _bdot_nt(a, b):
    return _d_nt(a.astype(BF16), b.astype(BF16))


def _bdot_tn(a, b):
    return _d_tn(a.astype(BF16), b.astype(BF16))


def _hdot_with(d, a, b):
    ah, al = _split(a)
    bh, bl = _split(b)
    return d(ah, bh) + (d(ah, bl) + d(al, bh))


def _hdot(a, b):
    return _hdot_with(_d, a, b)


def _hdot_nt(a, b):
    return _hdot_with(_d_nt, a, b)


def _hdot_tn(a, b):
    return _hdot_with(_d_tn, a, b)


def _xdot_r(e, a):
    ah, am, al = _split3(a)
    return _d(e, ah) + (_d(e, am) + _d(e, al))


def _iota(shape, dim):
    return lax.broadcasted_iota(jnp.int32, shape, dim)


def _softplus(x):
    return jnp.maximum(x, 0.0) + jnp.log1p(jnp.exp(-jnp.abs(x)))


def _sigmoid(x):
    return 1.0 / (1.0 + jnp.exp(-x))


def _silu(x):
    return x * _sigmoid(x)


def _segsum(a, e):
    m = a.shape[0]
    s = _d(jnp.concatenate(_split3(a), axis=0), e)
    return s[:m] + (s[m:2 * m] + s[2 * m:])


def _cumsum_rows(g, chunk=None):
    n = g.shape[0]
    chunk = n if chunk is None else chunk
    shift = int(math.log2(chunk))
    r, c = _iota((n, n), 0), _iota((n, n), 1)
    tril = jnp.where((r >= c) & ((r >> shift) == (c >> shift)), 1.0, 0.0).astype(BF16)
    return _xdot_r(tril, g)


def _shift_mix(x, prev_ref, b, mu):
    c = x.shape[0]
    rolled = pltpu.roll(x, 1, 0)
    prev = jnp.where(_iota(x.shape, 0) == 0, jnp.broadcast_to(prev_ref[b, 0:1, :], x.shape), rolled)
    prev_ref[b, 0:1, :] = x[c - 1:c, :]
    return x + mu * (prev - x)


def _layer_norm(x, g, b):
    mu = jnp.mean(x, axis=-1, keepdims=True)
    xc = x - mu
    var = jnp.mean(xc * xc, axis=-1, keepdims=True)
    return xc * lax.rsqrt(var + NORM_EPS) * g + b


def _mm_body(x_ref, w_ref, o_ref):
    o_ref[...] = _d(x_ref[...].astype(BF16), w_ref[...])


def _matmul(x, w_bf16, tm):
    m, k = x.shape
    n = w_bf16.shape[1]
    return pl.pallas_call(
        _mm_body,
        grid=(m // tm,),
        in_specs=[pl.BlockSpec((tm, k), lambda i: (i, 0)),
                  pl.BlockSpec((k, n), lambda i: (0, 0))],
        out_specs=pl.BlockSpec((tm, n), lambda i: (i, 0)),
        out_shape=jax.ShapeDtypeStruct((m, n), F32),
        compiler_params=_cparams(1),
        name="in_proj",
    )(x, w_bf16)


_RV_MU_R, _RV_MU_K, _RV_MU_V, _RV_W0, _RV_A0, _RV_KK, _RV_KA, _RV_RK, _RV_GNG, _RV_GNB = range(10)


def _rwkv_body(r_ref, k_ref, v_ref, lo_ref, vec_ref, mulo_ref, w2_ref, a2_ref, g2_ref, ones_ref,
               o_ref, pr_ref, pk_ref, pv_ref, plo_ref, st_ref):
    nb, c = r_ref.shape[0], r_ref.shape[1]
    npair = RW_WIDTH // LANES

    @pl.when(pl.program_id(0) == 0)
    def _():
        pr_ref[...] = jnp.zeros_like(pr_ref)
        pk_ref[...] = jnp.zeros_like(pk_ref)
        pv_ref[...] = jnp.zeros_like(pv_ref)
        plo_ref[...] = jnp.zeros_like(plo_ref)
        st_ref[...] = jnp.zeros_like(st_ref)

    def vec(i):
        return vec_ref[i:i + 1, :]

    ones_bd = ones_ref[...]
    lane = _iota((c, LANES), 1)
    m1 = lane < RW_HEAD
    row2 = _iota((2 * c, 4 * c), 0)
    col2 = _iota((2 * c, 4 * c), 1) & (c - 1)
    tri = ((row2 < c) & (row2 > col2)) | ((row2 >= c) & ((row2 - c) >= col2))
    eye2 = jnp.where(_iota((2 * c, 2 * c), 0) == _iota((2 * c, 2 * c), 1), 1.0, 0.0)
    bd_p = (_iota((LANES, LANES), 0) >> 6) == (_iota((LANES, LANES), 1) >> 6)

    def halves(x):
        return jnp.concatenate([jnp.where(m1, x, 0.0), jnp.where(m1, 0.0, x)], axis=0)

    def stacked(ref, prev_ref, mu):
        return jnp.concatenate([_shift_mix(ref[b], prev_ref, b, mu) for b in range(nb)], axis=0)

    xr = stacked(r_ref, pr_ref, vec(_RV_MU_R))
    xk = stacked(k_ref, pk_ref, vec(_RV_MU_K))
    xv = stacked(v_ref, pv_ref, vec(_RV_MU_V))
    xlo = stacked(lo_ref, plo_ref, mulo_ref[0:1, :])
    lo_a = xlo[:, :LANES]
    lo_g = xlo[:, LANES:]
    w = -_softplus(-(vec(_RV_W0) + _bdot(jnp.tanh(lo_a), w2_ref[...]))) - 0.5
    g = -jnp.exp(w)
    a = _sigmoid(vec(_RV_A0) + _bdot(lo_a, a2_ref[...]))
    gate = _bdot(_sigmoid(lo_g), g2_ref[...])
    kk = xk * vec(_RV_KK)
    kk = kk / jnp.maximum(jnp.sqrt(_segsum(kk * kk, ones_bd)), 1e-12)
    k2 = xk * (1.0 + (a - 1.0) * vec(_RV_KA))
    gc = _cumsum_rows(g, c)
    g_last = jnp.concatenate([jnp.broadcast_to(gc[(b + 1) * c - 1:(b + 1) * c, :], (c, gc.shape[1]))
                              for b in range(nb)], axis=0)
    e_neg = jnp.exp(-gc)
    e_end = jnp.exp(g_last - gc)
    prep = dict(
        xv=xv,
        at=-kk * jnp.exp(gc - g),
        bt=(kk * a) * e_neg, kt=k2 * e_neg, rt=xr * jnp.exp(gc),
        bh=(kk * a) * e_end, kh=k2 * e_end,
        gam=jnp.exp(g_last))

    chains = [(b, p) for b in range(nb) for p in range(npair)]

    def part(ch, name):
        b, p = ch
        return prep[name][b * c:(b + 1) * c, p * LANES:(p + 1) * LANES]

    a_ak, a_row, vv, ak, pinv = {}, {}, {}, {}, {}
    for ch in chains:
        lhs = jnp.concatenate([part(ch, 'at'), part(ch, 'rt')], axis=0)
        rhs = jnp.concatenate([halves(part(ch, 'bt')), halves(part(ch, 'kt'))], axis=0)
        aa = jnp.where(tri, _bdot_nt(lhs, rhs), 0.0)
        a_ak[ch] = aa[:c, 2 * c:].astype(BF16)
        a_row[ch] = aa[c:, :].astype(BF16)
        abd = halves(aa[:c, :2 * c])
        pinv[ch] = eye2 + abd
        ak[ch] = abd
        vv[ch] = halves(part(ch, 'xv')).astype(BF16)
    nlev = int(math.log2(c))
    for lev in range(nlev):
        for ch in chains:
            akb = ak[ch].astype(BF16)
            if lev == 0:
                ak[ch] = _d(akb, akb)
            elif lev < nlev - 1:
                out = _d(akb, jnp.concatenate([akb, pinv[ch].astype(BF16)], axis=1))
                ak[ch] = out[:, :2 * c]
                pinv[ch] = pinv[ch] + out[:, 2 * c:]
            else:
                pinv[ch] = pinv[ch] + _d(akb, pinv[ch].astype(BF16))
    x2 = {ch: _d(a_ak[ch], vv[ch]) for ch in chains}

    sts = {ch: st_ref[ch[0] * npair + ch[1]] for ch in chains}
    xs = {ch: _d_nt(jnp.concatenate([part(ch, 'at'), part(ch, 'rt')], axis=0).astype(BF16),
                    sts[ch].astype(BF16)) for ch in chains}
    us = {}
    for ch in chains:
        u2 = _d(pinv[ch].astype(BF16), halves(xs[ch][:c] + x2[ch]).astype(BF16))
        us[ch] = u2[:c] + u2[c:]
    ys = {}
    for ch in chains:
        b, p = ch
        u = us[ch]
        ys[ch] = xs[ch][c:] + _d(a_row[ch], jnp.concatenate([halves(u).astype(BF16), vv[ch]], axis=0))
        upd = _d_tn(jnp.concatenate([u, part(ch, 'xv')], axis=0).astype(BF16),
                    jnp.concatenate([part(ch, 'bh'), part(ch, 'kh')], axis=0).astype(BF16))
        st_ref[b * npair + p] = sts[ch] * part(ch, 'gam')[0:1, :] + jnp.where(bd_p, upd, 0.0)

    inv_n = 1.0 / RW_HEAD
    y = jnp.concatenate([jnp.concatenate([ys[(b, p)] for p in range(npair)], axis=1)
                         for b in range(nb)], axis=0)
    mean = _segsum(y, ones_bd) * inv_n
    yc = y - mean
    var = _segsum(yc * yc, ones_bd) * inv_n
    yn = yc * lax.rsqrt(var + RW_GN_EPS) * vec(_RV_GNG) + vec(_RV_GNB)
    bonus = _segsum(xr * k2 * vec(_RV_RK), ones_bd) * xv
    out = (yn + bonus) * gate
    for b in range(nb):
        o_ref[b] = out[b * c:(b + 1) * c, :]


def _rwkv(p0, vec, mulo, w2p, a2p, g2):
    batch, seq, _ = p0.shape
    nc = seq // CHUNK
    c = CHUNK
    ones_bd = jnp.asarray(np.kron(np.eye(RW_WIDTH // RW_HEAD), np.ones((RW_HEAD, RW_HEAD))), BF16)

    def col(j, width):
        return pl.BlockSpec((batch, c, width), lambda i: (0, i, j))

    def full(shape):
        return pl.BlockSpec(shape, lambda i: (0,) * len(shape))

    return pl.pallas_call(
        _rwkv_body,
        grid=(nc,),
        in_specs=[col(0, 512), col(1, 512), col(2, 512), col(10, 256),
                  full((16, 512)), full((8, 256)), full((128, 512)), full((128, 512)),
                  full((128, 512)), full((512, 512))],
        out_specs=pl.BlockSpec((batch, c, 512), lambda i: (0, i, 0)),
        out_shape=jax.ShapeDtypeStruct((batch, seq, RW_WIDTH), F32),
        scratch_shapes=[pltpu.VMEM((batch, 8, 512), F32), pltpu.VMEM((batch, 8, 512), F32),
                        pltpu.VMEM((batch, 8, 512), F32), pltpu.VMEM((batch, 8, 256), F32),
                        pltpu.VMEM((batch * RW_WIDTH // LANES, LANES, LANES), F32)],
        compiler_params=_cparams(1),
        name="rwkv7",
    )(p0, p0, p0, p0, vec, mulo, w2p, a2p, g2, ones_bd)


def _gla_core(qs, ks, vs, gs, st_ref, heads_per_block):
    nb = len(qs)
    c = qs[0].shape[0]
    hpb = heads_per_block
    nblk = qs[0].shape[1] // LANES
    dk_shift = int(math.log2(LANES // hpb))
    row = _iota(qs[0].shape, 0)

    prep = []
    for q, k, g in zip(qs, ks, gs):
        b = _cumsum_rows(g)

        def brow(i, b=b):
            return jnp.broadcast_to(b[i:i + 1, :], b.shape)

        b15, b31, b47, blast = brow(15), brow(31), brow(47), brow(c - 1)
        ref_b = jnp.where(row < 32, b15, b47)
        ref_d = jnp.where(row < 16, 0.0, jnp.where(row < 32, b15, jnp.where(row < 48, b31, b47)))
        prep.append(dict(
            q_a=q * jnp.exp(jnp.minimum(b - b31, 0.0)), k_a=k * jnp.exp(jnp.minimum(b31 - b, 0.0)),
            q_b=q * jnp.exp(jnp.minimum(b - ref_b, 0.0)), k_b=k * jnp.exp(jnp.minimum(ref_b - b, 0.0)),
            q_d=q * jnp.exp(b - ref_d), k_d=k * jnp.exp(ref_d - b),
            q_i=q * jnp.exp(b), k_s=k * jnp.exp(blast - b), gam=jnp.exp(b[c - 1:c, :])))

    ri = _iota((hpb * c, c), 0) & (c - 1)
    ci = _iota((hpb * c, c), 1)
    mask_a = (ri >= 32) & (ci < 32)
    mask_b = ((ri >> 5) == (ci >> 5)) & (((ri >> 4) & 1) == 1) & (((ci >> 4) & 1) == 0)
    mask_d = ((ri >> 4) == (ci >> 4)) & (ri >= ci)
    lane = _iota((c, LANES), 1)
    bd = (_iota((hpb * LANES, LANES), 0) >> 7) == (_iota((hpb * LANES, LANES), 1) >> dk_shift)

    def heads_rows(x):
        if hpb == 1:
            return x
        return jnp.concatenate([jnp.where((lane >> dk_shift) == h, x, 0.0) for h in range(hpb)], axis=0)

    chains = [(bi, blk) for bi in range(nb) for blk in range(nblk)]

    def part(ch, name):
        bi, blk = ch
        return prep[bi][name][:, blk * LANES:(blk + 1) * LANES]

    v_p = {ch: vs[ch[0]][:, ch[1] * hpb * LANES:(ch[1] + 1) * hpb * LANES].astype(BF16) for ch in chains}
    probs = {}
    for ch in chains:
        s_a = _bdot_nt(heads_rows(part(ch, 'q_a')), part(ch, 'k_a'))
        s_b = _bdot_nt(heads_rows(part(ch, 'q_b')), part(ch, 'k_b'))
        s_d = _bdot_nt(heads_rows(part(ch, 'q_d')), part(ch, 'k_d'))
        probs[ch] = (jnp.where(mask_a, s_a, 0.0) + jnp.where(mask_b, s_b, 0.0)
                     + jnp.where(mask_d, s_d, 0.0)).astype(BF16)
    outs = {}
    for ch in chains:
        pv = _d(probs[ch], v_p[ch])
        o = pv[:c]
        for h in range(1, hpb):
            o = jnp.where((_iota(o.shape, 1) >> 7) == h, pv[h * c:(h + 1) * c], o)
        si = ch[0] * nblk + ch[1]
        sp = st_ref[si]
        outs[ch] = o + _bdot_nt(part(ch, 'q_i'), sp)
        upd = _d_tn(v_p[ch], part(ch, 'k_s').astype(BF16))
        st_ref[si] = sp * part(ch, 'gam') + jnp.where(bd, upd, 0.0)
    return [jnp.concatenate([outs[(bi, blk)] for blk in range(nblk)], axis=1) for bi in range(nb)]


def _gated_rmsnorm(o, gate, norm_g):
    nh = o.shape[1] // LANES
    outs = []
    for h in range(nh):
        sl = slice(h * LANES, (h + 1) * LANES)
        oh = o[:, sl]
        ms = jnp.mean(oh * oh, axis=-1, keepdims=True)
        outs.append(oh * lax.rsqrt(ms + NORM_EPS) * norm_g * _silu(gate[:, sl]))
    return jnp.concatenate(outs, axis=1)


def _gla_body(q_ref, k_ref, v_ref, gate_ref, gk_ref, w2_ref, vec_ref, ng_ref, o_ref, st_ref):
    nb = q_ref.shape[0]

    @pl.when(pl.program_id(0) == 0)
    def _():
        st_ref[...] = jnp.zeros_like(st_ref)

    gs = [-_softplus(-(_hdot(gk_ref[b], w2_ref[...]) + vec_ref[0:1, :])) * (1.0 / GLA_GATE_TAU)
          for b in range(nb)]
    qs = [q_ref[b] * (GLA_DK ** -0.5) for b in range(nb)]
    os_ = _gla_core(qs, [k_ref[b] for b in range(nb)], [v_ref[b] for b in range(nb)], gs, st_ref, 2)
    for b in range(nb):
        o_ref[b] = _gated_rmsnorm(os_[b], gate_ref[b], ng_ref[0:1, :])


def _seq_specs(batch, c):
    def col(j, width):
        return pl.BlockSpec((batch, c, width), lambda i: (0, i, j))

    def full(shape):
        return pl.BlockSpec(shape, lambda i: (0,) * len(shape))

    return col, full


def _gla(p0, gk_w2p, gk_b, norm_g):
    batch, seq, _ = p0.shape
    c = CHUNK
    col, full = _seq_specs(batch, c)
    return pl.pallas_call(
        _gla_body,
        grid=(seq // c,),
        in_specs=[col(11, 256), col(12, 256), col(3, 512), col(4, 512), col(26, 128),
                  full((128, 256)), full((8, 256)), full((8, 128))],
        out_specs=pl.BlockSpec((batch, c, 512), lambda i: (0, i, 0)),
        out_shape=jax.ShapeDtypeStruct((batch, seq, 512), F32),
        scratch_shapes=[pltpu.VMEM((batch * 2, 2 * LANES, LANES), F32)],
        compiler_params=_cparams(1),
        name="gla",
    )(p0, p0, p0, p0, p0, gk_w2p, gk_b, norm_g)


def _hgrn_body(q_ref, f_ref, i_ref, gate_ref, lb_ref, ng_ref, o_ref, st_ref):
    nb = q_ref.shape[0]

    @pl.when(pl.program_id(0) == 0)
    def _():
        st_ref[...] = jnp.zeros_like(st_ref)

    lb = lb_ref[0:1, :]
    qs, ks, gs = [], [], []
    for b in range(nb):
        f = f_ref[b]
        gs.append(jnp.log(lb + (1.0 - lb) * _sigmoid(f)))
        ks.append((1.0 - lb) * _sigmoid(-f))
        qs.append(_silu(q_ref[b]))
    os_ = _gla_core(qs, ks, [i_ref[b] for b in range(nb)], gs, st_ref, 1)
    for b in range(nb):
        o_ref[b] = _gated_rmsnorm(os_[b], gate_ref[b], ng_ref[0:1, :])


def _hgrn(p1, lb, norm_g):
    batch, seq, _ = p1.shape
    c = CHUNK
    col, full = _seq_specs(batch, c)
    return pl.pallas_call(
        _hgrn_body,
        grid=(seq // c,),
        in_specs=[col(1, 512), col(2, 512), col(3, 512), col(4, 512), full((8, 512)), full((8, 128))],
        out_specs=pl.BlockSpec((batch, c, 512), lambda i: (0, i, 0)),
        out_shape=jax.ShapeDtypeStruct((batch, seq, 512), F32),
        scratch_shapes=[pltpu.VMEM((batch * HG_HEADS, LANES, LANES), F32)],
        compiler_params=_cparams(1),
        name="hgrn2",
    )(p1, p1, p1, p1, lb, norm_g)


def _s5_body(u_ref, dcat_ref, bm_ref, eblk_ref, pw_ref, o_ref, toep_ref, e_ref):
    seq = u_ref.shape[0]
    rows = seq // S5_CHUNK

    @pl.when(pl.program_id(1) == 0)
    def _():
        for tk in range(S5_CHUNK):
            lo = (S5_CHUNK - 1 - tk) * LANES
            toep_ref[tk * LANES:(tk + 1) * LANES, :] = dcat_ref[0, :, lo:lo + S5_CHUNK * LANES]
            e_ref[:, tk * LANES:(tk + 1) * LANES] = eblk_ref[0, tk]

    u = jnp.concatenate([u_ref[pl.ds(tk, rows, stride=S5_CHUNK), :] for tk in range(S5_CHUNK)],
                        axis=1).astype(BF16)
    s = _d(u, bm_ref[0])
    rin = _iota(s.shape, 0)
    pw = pw_ref[0]

    def swap_halves(x):
        return jnp.concatenate([pltpu.roll(x[:, k * LANES:(k + 1) * LANES], S5_STATE, 1)
                                for k in range(x.shape[1] // LANES)], axis=1)

    h = s
    for lev in range(int(math.log2(rows))):
        sh = 1 << lev
        hs = jnp.where(rin >= sh, pltpu.roll(h, sh, 0), 0.0)
        h = h + pw[2 * lev:2 * lev + 1, :] * hs + pw[2 * lev + 1:2 * lev + 2, :] * swap_halves(hs)
    hprev = jnp.where(rin >= 1, pltpu.roll(h, 1, 0), 0.0)
    y = _d(u, toep_ref[...]) + _bdot(hprev, e_ref[...])
    for tk in range(S5_CHUNK):
        o_ref[pl.ds(tk, rows, stride=S5_CHUNK), :] = y[:, tk * LANES:(tk + 1) * LANES]


def _s5_scan(p1, dcat, bmat, emat, pw):
    batch, seq, _ = p1.shape
    nt, width, nstate = bmat.shape

    def table(shape):
        return pl.BlockSpec((1,) + shape, lambda j, b: (j, 0, 0), pipeline_mode=pl.Buffered(1))

    return pl.pallas_call(
        _s5_body,
        grid=(nt, batch),
        in_specs=[pl.BlockSpec((None, seq, LANES), lambda j, b: (b, 0, j)),
                  table(dcat.shape[1:]), table((width, nstate)),
                  pl.BlockSpec((1,) + emat.shape[1:], lambda j, b: (j, 0, 0, 0), pipeline_mode=pl.Buffered(1)),
                  table((S5_SCAN_ROWS, nstate))],
        out_specs=pl.BlockSpec((None, seq, LANES), lambda j, b: (b, 0, j)),
        out_shape=jax.ShapeDtypeStruct((batch, seq, nt * LANES), F32),
        scratch_shapes=[pltpu.VMEM((width, width), BF16), pltpu.VMEM((nstate, width), BF16)],
        compiler_params=_cparams(2),
        name="s5_scan",
    )(p1, dcat, bmat, emat, pw)


def _s5_post_body(y_ref, u_ref, vec_ref, w_ref, o_ref):
    y = y_ref[...] + vec_ref[0:1, :] * u_ref[...]
    y = 0.5 * y * (1.0 + jnp.tanh(math.sqrt(2.0 / math.pi) * (y + 0.044715 * (y * y * y))))
    o_ref[...] = y * _sigmoid(_bdot(y, w_ref[...]) + vec_ref[1:2, :])


def _s5_post(y_ssm, p1, vec, glu_w, tm):
    t = y_ssm.shape[0]
    return pl.pallas_call(
        _s5_post_body,
        grid=(t // tm,),
        in_specs=[pl.BlockSpec((tm, 512), lambda i: (i, 0)),
                  pl.BlockSpec((tm, 512), lambda i: (i, 0)),
                  pl.BlockSpec((8, 512), lambda i: (0, 0)),
                  pl.BlockSpec((512, 512), lambda i: (0, 0))],
        out_specs=pl.BlockSpec((tm, 512), lambda i: (i, 0)),
        out_shape=jax.ShapeDtypeStruct((t, 512), F32),
        compiler_params=_cparams(1),
        name="s5_post",
    )(y_ssm, p1, vec, glu_w)


def _s5_tables(a_re, a_im, log_dt, b_re, b_im, c_re, c_im, rows_per_seq):
    c = S5_CHUNK
    lam_re = jnp.minimum(a_re, -1e-4)
    lam_im = a_im
    dt = jnp.exp(log_dt)[:, None]
    mag = jnp.exp(lam_re * dt)
    abar_re = mag * jnp.cos(lam_im * dt)
    abar_im = mag * jnp.sin(lam_im * dt)
    den = lam_re * lam_re + lam_im * lam_im
    num_re = abar_re - 1.0
    z_re = (num_re * lam_re + abar_im * lam_im) / den
    z_im = (abar_im * lam_re - num_re * lam_im) / den

    def power(n):
        n = jnp.asarray(n, F32)[..., None, None]
        m = jnp.exp(n * (lam_re * dt))
        return m * jnp.cos(n * (lam_im * dt)), m * jnp.sin(n * (lam_im * dt))

    def cmul(ar, ai, br, bi):
        return ar * br - ai * bi, ar * bi + ai * br

    tau = jnp.arange(c)
    p_re, p_im = power(tau)
    zb_re, zb_im = cmul(z_re[..., None], z_im[..., None], b_re, b_im)
    cp_re, cp_im = cmul(c_re[None], c_im[None], p_re[:, :, None, :], p_im[:, :, None, :])
    hi = lax.Precision.HIGHEST
    kern = (jnp.einsum('tgon,gni->gtoi', cp_re, zb_re, precision=hi)
            - jnp.einsum('tgon,gni->gtoi', cp_im, zb_im, precision=hi))
    q_re, q_im = power(c - 1 - tau)
    bm_re, bm_im = cmul(q_re[..., None], q_im[..., None], zb_re[None], zb_im[None])
    bmat = jnp.concatenate([jnp.transpose(bm_re, (1, 0, 3, 2)), jnp.transpose(bm_im, (1, 0, 3, 2))],
                           axis=-1).reshape(S5_GROUPS, c * S5_GROUP, 2 * S5_STATE)
    r_re, r_im = power(tau + 1)
    nlev = int(math.log2(rows_per_seq))
    assert 2 * nlev <= S5_SCAN_ROWS
    s_re, s_im = power(c * (2 ** jnp.arange(nlev)))
    pw = jnp.zeros((S5_GROUPS, S5_SCAN_ROWS, 2 * S5_STATE), F32)
    pw = pw.at[:, 0:2 * nlev:2, :].set(jnp.transpose(jnp.concatenate([s_re, s_re], -1), (1, 0, 2)))
    pw = pw.at[:, 1:2 * nlev:2, :].set(jnp.transpose(jnp.concatenate([-s_im, s_im], -1), (1, 0, 2)))
    nt, gl = S5_GROUPS // S5_TILE_GROUPS, S5_TILE_GROUPS
    spread = jnp.tile(jnp.eye(S5_GROUP, dtype=F32), (1, gl))
    lane_group = jnp.arange(LANES) // S5_GROUP
    kern5 = jnp.transpose(kern.reshape(nt, gl, c, S5_GROUP, S5_GROUP), (0, 2, 1, 4, 3))
    dblk = jnp.einsum('jtgio,op->jtgip', kern5, spread, precision=hi)
    dblk = jnp.where((jnp.arange(gl)[:, None, None] == lane_group[None, None, :]), dblk, 0.0)
    dblk = dblk.reshape(nt, c, LANES, LANES)
    dcat = jnp.transpose(dblk, (0, 2, 1, 3)).reshape(nt, LANES, c * LANES)
    dcat = jnp.pad(dcat, ((0, 0), (0, 0), ((c - 1) * LANES, 0)))
    bmat5 = jnp.transpose(bmat.reshape(nt, gl, c, S5_GROUP, 2 * S5_STATE), (0, 2, 1, 3, 4))
    row_group = (jnp.arange(c * LANES) // S5_GROUP) % gl
    col_tile = jnp.arange(gl * 2 * S5_STATE) // (2 * S5_STATE)
    bmat_t = jnp.where(row_group[:, None] == col_tile[None, :],
                       jnp.tile(bmat5.reshape(nt, c * LANES, 2 * S5_STATE), (1, 1, gl)), 0.0)
    own_group = (jnp.arange(S5_GROUPS) % gl)[:, None, None] == lane_group[None, None, :]

    def spread_c(cmat):
        cs = jnp.where(own_group, jnp.einsum('gon,op->gnp', cmat, spread, precision=hi), 0.0)
        return cs.reshape(nt, 1, gl, S5_STATE, LANES)

    def per_state(pw_):
        return jnp.transpose(pw_.reshape(c, nt, gl, S5_STATE), (1, 0, 2, 3))[..., None]

    cs_re, cs_im, rr, ri = spread_c(c_re), spread_c(c_im), per_state(r_re), per_state(r_im)
    eblk = jnp.concatenate([cs_re * rr - cs_im * ri, -(cs_re * ri + cs_im * rr)], axis=3)
    eblk = eblk.reshape(nt, c, gl * 2 * S5_STATE, LANES)
    pw_t = jnp.transpose(pw.reshape(nt, gl, S5_SCAN_ROWS, 2 * S5_STATE), (0, 2, 1, 3))
    pw_t = pw_t.reshape(nt, S5_SCAN_ROWS, gl * 2 * S5_STATE)
    return dcat.astype(BF16), bmat_t.astype(BF16), eblk.astype(BF16), pw_t


def _proj_route_body(ya_ref, yb_ref, x_ref, w_ref, ln_ref, wr_ref, br_ref,
                     h_ref, route_ref, cnt_ref, carry_ref):
    tm = ya_ref.shape[0]
    half = w_ref.shape[0] // 2

    @pl.when(pl.program_id(0) == 0)
    def _():
        carry_ref[...] = jnp.zeros_like(carry_ref)

    mix = _bdot(ya_ref[...], w_ref[:half, :]) + _bdot(yb_ref[...], w_ref[half:, :])
    h = _layer_norm(DN_ALPHA * x_ref[...] + mix, ln_ref[0:1, :], ln_ref[1:2, :])
    h_ref[...] = h

    logits = _hdot(h, wr_ref[...]) + br_ref[0:1, :]
    lane = _iota((tm, LANES), 1).astype(F32)
    neg = -jnp.inf

    def softmax_masked(mask):
        xm = jnp.where(mask, logits, neg)
        m = jnp.max(xm, axis=-1, keepdims=True)
        e = jnp.exp(xm - m)
        return e / jnp.sum(e, axis=-1, keepdims=True)

    def top1(pm):
        m = jnp.max(pm, axis=-1, keepdims=True)
        idx = jnp.min(jnp.where(pm == m, lane, float(LANES)), axis=-1, keepdims=True)
        return m, idx

    coarse = jnp.where(lane < MOE_GROUPS, softmax_masked(lane < MOE_GROUPS), -1.0)
    p_grp, grp = top1(coarse)
    lo = MOE_GROUPS + MOE_PER_GROUP * grp
    fmask = (lane >= lo) & (lane < lo + MOE_PER_GROUP)
    fine = jnp.where(fmask, softmax_masked(fmask), -1.0)
    p1, j1 = top1(fine)
    p2, j2 = top1(jnp.where(lane == j1, -1.0, fine))
    denom = p1 + p2
    g1 = p_grp * (p1 / denom)
    g2 = p_grp * (p2 / denom)
    e1 = j1 - MOE_GROUPS
    e2 = j2 - MOE_GROUPS

    oh1 = jnp.where(lane == e1, 1.0, 0.0)
    oh2 = jnp.where(lane == e2, 1.0, 0.0)
    cnt = oh1 + oh2
    strict = jnp.where(_iota((tm, tm), 0) > _iota((tm, tm), 1), 1.0, 0.0).astype(BF16)
    before = _d(strict, cnt.astype(BF16)) + carry_ref[0:1, :]
    r1 = jnp.sum(oh1 * before, axis=-1, keepdims=True)
    r2 = jnp.sum(oh2 * before, axis=-1, keepdims=True)
    carry_ref[0:1, :] = carry_ref[0:1, :] + jnp.sum(cnt, axis=0, keepdims=True)
    cnt_ref[...] = carry_ref[...]

    out = jnp.where(lane == 0, e1, 0.0)
    out = jnp.where(lane == 1, e2, out)
    out = jnp.where(lane == 2, r1, out)
    out = jnp.where(lane == 3, r2, out)
    out = jnp.where(lane == 4, g1, out)
    out = jnp.where(lane == 5, g2, out)
    route_ref[...] = out[:, :8]


def _proj_route(ya, yb, resid, w_out, ln, wr, br, tm):
    t, d = resid.shape
    return pl.pallas_call(
        _proj_route_body,
        grid=(t // tm,),
        in_specs=[pl.BlockSpec((tm, 512), lambda i: (i, 0)),
                  pl.BlockSpec((tm, 512), lambda i: (i, 0)),
                  pl.BlockSpec((tm, d), lambda i: (i, 0)),
                  pl.BlockSpec((d, d), lambda i: (0, 0)),
                  pl.BlockSpec((8, d), lambda i: (0, 0)),
                  pl.BlockSpec((d, LANES), lambda i: (0, 0)),
                  pl.BlockSpec((8, LANES), lambda i: (0, 0))],
        out_specs=[pl.BlockSpec((tm, d), lambda i: (i, 0)),
                   pl.BlockSpec((tm, 8), lambda i: (i, 0)),
                   pl.BlockSpec((8, LANES), lambda i: (0, 0))],
        out_shape=[jax.ShapeDtypeStruct((t, d), F32),
                   jax.ShapeDtypeStruct((t, 8), F32),
                   jax.ShapeDtypeStruct((8, LANES), F32)],
        scratch_shapes=[pltpu.VMEM((8, LANES), F32)],
        compiler_params=_cparams(1),
        name="proj_ln_route",
    )(ya, yb, resid, w_out, ln, wr, br)


def _row_copy(src_ref, src_row, dst_ref, dst_row, sem):
    return pltpu.make_async_copy(src_ref.at[pl.ds(src_row, 1)], dst_ref.at[pl.ds(dst_row, 1)], sem)


def _start_all(copies):
    for n, cp in enumerate(copies):
        cp.start(priority=n % 2)


def _dispatch_body(pad_start_ref, pad_len_ref, used_rows_ref, dest_ref, h_ref, rows_ref, hbuf_ref, zbuf_ref,
                   lsem, ssem, zsem):
    i = pl.program_id(0)
    last = pl.num_programs(0) - 1
    tm = hbuf_ref.shape[1]
    slot = i % 2

    def load(tile, s):
        return pltpu.make_async_copy(h_ref.at[pl.ds(tile * tm, tm)], hbuf_ref.at[s], lsem.at[s])

    def scatters(s):
        return [_row_copy(hbuf_ref.at[s], r, rows_ref, dest_ref[0, 0, 2 * r + q], ssem.at[s])
                for r in range(tm) for q in range(2)]

    @pl.when(i == 0)
    def _():
        load(0, 0).start()

    @pl.when(i >= 1)
    def _():
        for cp in scatters(1 - slot):
            cp.wait()

    @pl.when(i < last)
    def _():
        load(i + 1, 1 - slot).start()

    load(i, slot).wait()
    _start_all(scatters(slot))

    @pl.when(i == last)
    def _():
        for cp in scatters(slot):
            cp.wait()
        zbuf_ref[...] = jnp.zeros_like(zbuf_ref)
        sub = 8
        half = zbuf_ref.shape[0]

        def pad_fills():
            out = []
            for e in range(N_EXPERTS):
                start = pad_start_ref[e]
                npad = pad_len_ref[e]
                end = start + npad
                run = half
                while run >= sub:
                    end = end - (npad & run)
                    dst = rows_ref.at[pl.ds(pl.multiple_of(end, run), run)]
                    out.append(((npad & run) != 0,
                                pltpu.make_async_copy(zbuf_ref.at[pl.ds(0, run)], dst, zsem)))
                    run //= 2
                for k in range(sub - 1):
                    out.append((k < (npad & (sub - 1)), _row_copy(zbuf_ref, 0, rows_ref, start + k, zsem)))
            return out

        for pred, cp in pad_fills():
            pl.when(pred)(cp.start)
        for pred, cp in pad_fills():
            pl.when(pred)(cp.wait)

        def tail(b):
            return pltpu.make_async_copy(
                zbuf_ref, rows_ref.at[pl.ds(pl.multiple_of(b * half, half), half)], zsem)

        first_free = used_rows_ref[0] // half
        n_half = rows_ref.shape[0] // half
        lax.fori_loop(first_free, n_half, lambda b, c: (tail(b).start(), c)[1], 0)
        lax.fori_loop(first_free, n_half, lambda b, c: (tail(b).wait(), c)[1], 0)


def _dispatch(pad_start, pad_len, used_rows, dest3, h, n_rows):
    t, d = h.shape
    nt, _, tm2 = dest3.shape
    tm = tm2 // 2
    grid_spec = pltpu.PrefetchScalarGridSpec(
        num_scalar_prefetch=3,
        grid=(nt,),
        in_specs=[pl.BlockSpec((1, 1, tm2), lambda i, *_: (i, 0, 0), memory_space=pltpu.SMEM),
                  pl.BlockSpec(memory_space=pl.ANY)],
        out_specs=pl.BlockSpec(memory_space=pl.ANY),
        scratch_shapes=[pltpu.VMEM((2, tm, d), F32), pltpu.VMEM((MOE_BLOCK // 2, d), F32),
                        pltpu.SemaphoreType.DMA((2,)), pltpu.SemaphoreType.DMA((2,)),
                        pltpu.SemaphoreType.DMA],
    )
    return pl.pallas_call(
        _dispatch_body,
        grid_spec=grid_spec,
        out_shape=jax.ShapeDtypeStruct((n_rows, d), F32),
        compiler_params=_cparams(1),
        name="moe_dispatch",
    )(pad_start, pad_len, used_rows, dest3, h)


def _expert_body(first_ref, nblk_ref, nused_ref, x_ref, w1_ref, w3_ref, w2_ref, y_ref,
                 xbuf_ref, ybuf_ref, w1b_ref, w3b_ref, w2b_ref, ypend_ref, ycnt_ref, xcnt_ref, xpre_ref,
                 xsem, ysem):
    e = pl.program_id(0)
    row0 = first_ref[e] * MOE_BLOCK
    nblk = nblk_ref[e]
    big = xbuf_ref.shape[1]
    per_big = big // MOE_BLOCK

    @pl.when(e == 0)
    def _():
        ypend_ref[0] = 0
        ypend_ref[1] = 0
        ycnt_ref[0] = 0
        xcnt_ref[0] = 0
        xpre_ref[0] = 0

    def drain(slot):
        for size in (big, MOE_BLOCK):
            @pl.when(ypend_ref[slot] == size)
            def _(size=size):
                pltpu.make_async_copy(ybuf_ref.at[slot, pl.ds(0, size)], y_ref.at[pl.ds(0, size)],
                                      ysem.at[slot]).wait()
        ypend_ref[slot] = 0

    def cast_weights():
        w1b_ref[...] = w1_ref[0].astype(BF16)
        w3b_ref[...] = w3_ref[0].astype(BF16)
        w2b_ref[...] = w2_ref[0].astype(BF16)

    def x_copy(row, size, slot):
        src = x_ref.at[pl.ds(pl.multiple_of(row, MOE_BLOCK), size)]
        return pltpu.make_async_copy(src, xbuf_ref.at[slot, pl.ds(0, size)], xsem.at[slot])

    def start_x(row, is_big, slot):
        pl.when(is_big)(lambda: x_copy(row, big, slot).start(priority=1))
        pl.when(jnp.logical_not(is_big))(lambda: x_copy(row, MOE_BLOCK, slot).start(priority=1))

    nbig = nblk // per_big
    ntail = nblk - nbig * per_big
    tail_row = row0 + nbig * big
    e_next = jnp.minimum(e + 1, pl.num_programs(0) - 1)
    next_ok = (e + 1 < pl.num_programs(0)) & (nblk_ref[e_next] > 0)

    def prefetch_next_expert(slot):
        @pl.when(next_ok)
        def _():
            start_x(first_ref[e_next] * MOE_BLOCK, nblk_ref[e_next] >= per_big, slot)
            xpre_ref[0] = 1

    def stream(start_row, count, size, casts_first, start_following):
        def y_copy(j, s):
            dst = y_ref.at[pl.ds(pl.multiple_of(start_row + j * size, MOE_BLOCK), size)]
            return pltpu.make_async_copy(ybuf_ref.at[s, pl.ds(0, size)], dst, ysem.at[s])

        @pl.when(count > 0)
        def _():
            pl.when(casts_first)(cast_weights)

            def chunk(j, carry):
                s = xcnt_ref[0] % 2
                xcnt_ref[0] = xcnt_ref[0] + 1

                @pl.when(j + 1 < count)
                def _():
                    x_copy(start_row + (j + 1) * size, size, 1 - s).start(priority=1)

                @pl.when(j + 1 == count)
                def _():
                    start_following(1 - s)

                x_copy(start_row + j * size, size, s).wait()
                xb = xbuf_ref[s, pl.ds(0, size), :].astype(BF16)
                hid = _silu(_d(xb, w1b_ref[...])) * _d(xb, w3b_ref[...])
                ys = ycnt_ref[0] % 2
                drain(ys)
                ybuf_ref[ys, pl.ds(0, size), :] = _d(hid.astype(BF16), w2b_ref[...])
                y_copy(j, ys).start()
                ypend_ref[ys] = size
                ycnt_ref[0] = ycnt_ref[0] + 1
                return carry

            lax.fori_loop(0, count, chunk, 0)

    @pl.when(nblk > 0)
    def _():
        @pl.when(xpre_ref[0] == 0)
        def _():
            start_x(row0, nbig > 0, xcnt_ref[0] % 2)

        xpre_ref[0] = 0

        def after_big(slot):
            pl.when(ntail > 0)(lambda: x_copy(tail_row, MOE_BLOCK, slot).start(priority=1))
            pl.when(ntail == 0)(lambda: prefetch_next_expert(slot))

        stream(row0, nbig, big, nbig > 0, after_big)
        stream(tail_row, ntail, MOE_BLOCK, nbig == 0, prefetch_next_expert)

    @pl.when(e == pl.num_programs(0) - 1)
    def _():
        drain(0)
        drain(1)
        ybuf_ref[0, pl.ds(0, MOE_BLOCK), :] = jnp.zeros((MOE_BLOCK, ybuf_ref.shape[2]), F32)
        nused = nused_ref[0]
        ntot = y_ref.shape[0] // MOE_BLOCK

        def fill(b, carry):
            cp = pltpu.make_async_copy(ybuf_ref.at[0, pl.ds(0, MOE_BLOCK)],
                                       y_ref.at[pl.ds(b * MOE_BLOCK, MOE_BLOCK)], ysem.at[0])
            cp.start()
            cp.wait()
            return carry

        lax.fori_loop(nused, ntot, fill, 0)


def _experts(first_blk, nblk, nused, x_rows, w1, w3, w2, layer):
    r, d = x_rows.shape
    hid = w1.shape[-1]
    grid_spec = pltpu.PrefetchScalarGridSpec(
        num_scalar_prefetch=3,
        grid=(N_EXPERTS,),
        in_specs=[pl.BlockSpec(memory_space=pl.ANY),
                  pl.BlockSpec((None, 1, d, hid), lambda e, *_: (layer, e, 0, 0)),
                  pl.BlockSpec((None, 1, d, hid), lambda e, *_: (layer, e, 0, 0)),
                  pl.BlockSpec((None, 1, hid, d), lambda e, *_: (layer, e, 0, 0))],
        out_specs=pl.BlockSpec(memory_space=pl.ANY),
        scratch_shapes=[pltpu.VMEM((2, EXPERT_CHUNK, d), F32), pltpu.VMEM((2, EXPERT_CHUNK, d), F32),
                        pltpu.VMEM((d, hid), BF16), pltpu.VMEM((d, hid), BF16), pltpu.VMEM((hid, d), BF16),
                        pltpu.SMEM((2,), jnp.int32), pltpu.SMEM((1,), jnp.int32),
                        pltpu.SMEM((1,), jnp.int32), pltpu.SMEM((1,), jnp.int32),
                        pltpu.SemaphoreType.DMA((2,)), pltpu.SemaphoreType.DMA((2,))],
    )
    return pl.pallas_call(
        _expert_body,
        grid_spec=grid_spec,
        out_shape=jax.ShapeDtypeStruct((r, d), F32),
        compiler_params=_cparams(1),
        name="moe_experts",
    )(first_blk, nblk, nused, x_rows, w1, w3, w2)


def _combine_body(dest_ref, dest_next_ref, gate_ref, h_ref, ln_ref, rows_ref, o_ref, buf_ref, sem):
    i = pl.program_id(0)
    last = pl.num_programs(0) - 1
    tm = h_ref.shape[0]

    def gather(dref, slot):
        return [_row_copy(rows_ref, dref[0, 0, 2 * r + s], buf_ref.at[slot, s], r, sem.at[slot])
                for r in range(tm) for s in range(2)]

    @pl.when(i == 0)
    def _():
        _start_all(gather(dest_ref, 0))

    _start_all(gather(dest_next_ref, (i + 1) % 2))
    slot = i % 2
    for cp in gather(dest_ref, slot):
        cp.wait()
    gate = gate_ref[...]
    y = gate[:, 4:5] * buf_ref[slot, 0] + gate[:, 5:6] * buf_ref[slot, 1]
    o_ref[...] = _layer_norm(DN_ALPHA * h_ref[...] + y, ln_ref[0:1, :], ln_ref[1:2, :])

    @pl.when(i == last)
    def _():
        for cp in gather(dest_next_ref, (i + 1) % 2):
            cp.wait()


def _combine(dest3, route, h, ln, y_rows, tm):
    t, d = h.shape
    nt = t // tm
    return pl.pallas_call(
        _combine_body,
        grid=(nt,),
        in_specs=[pl.BlockSpec((1, 1, 2 * tm), lambda i: (i, 0, 0), memory_space=pltpu.SMEM),
                  pl.BlockSpec((1, 1, 2 * tm), lambda i: (jnp.minimum(i + 1, nt - 1), 0, 0),
                               memory_space=pltpu.SMEM),
                  pl.BlockSpec((tm, 8), lambda i: (i, 0)),
                  pl.BlockSpec((tm, d), lambda i: (i, 0)),
                  pl.BlockSpec((8, d), lambda i: (0, 0)),
                  pl.BlockSpec(memory_space=pl.ANY)],
        out_specs=pl.BlockSpec((tm, d), lambda i: (i, 0)),
        out_shape=jax.ShapeDtypeStruct((t, d), F32),
        scratch_shapes=[pltpu.VMEM((2, 2, tm, d), F32), pltpu.SemaphoreType.DMA((2,))],
        compiler_params=_cparams(1),
        name="moe_combine_ln",
    )(dest3, dest3, route, h, ln, y_rows)


def _moe(h, route, counts, w1, w3, w2, layer, ln):
    t, d = h.shape
    tm = min(COMBINE_TILE, t)
    a = 2 * t
    expert = route[:, 0:2].astype(jnp.int32)
    rank = route[:, 2:4].astype(jnp.int32)
    cnt = counts[0, :N_EXPERTS].astype(jnp.int32)
    padded = (cnt + MOE_BLOCK - 1) // MOE_BLOCK * MOE_BLOCK
    pend = jnp.cumsum(padded)
    pstart = pend - padded
    sel = expert[..., None] == jnp.arange(N_EXPERTS, dtype=jnp.int32)
    dest = jnp.sum(jnp.where(sel, pstart, 0), axis=-1) + rank
    n_blocks = -(-a // MOE_BLOCK) + N_EXPERTS
    nused = (pend[-1:] // MOE_BLOCK).astype(jnp.int32)
    dest3 = dest.reshape(t // tm, 1, 2 * tm)
    x_rows = _dispatch((pstart + cnt).astype(jnp.int32), (padded - cnt).astype(jnp.int32),
                       pend[-1:].astype(jnp.int32), dest3, h, n_blocks * MOE_BLOCK)
    y_rows = _experts((pstart // MOE_BLOCK).astype(jnp.int32), (padded // MOE_BLOCK).astype(jnp.int32),
                      nused, x_rows, w1, w3, w2, layer)
    return _combine(dest3, route, h, ln, y_rows, tm)


def _pad_rows(x, rows):
    return jnp.zeros((rows,) + x.shape[1:], x.dtype).at[:x.shape[0]].set(x)


def _route_weights(wg, bg, we, be):
    d = wg.shape[0]
    wr = jnp.zeros((d, LANES), F32)
    wr = wr.at[:, :MOE_GROUPS].set(wg)
    wr = wr.at[:, MOE_GROUPS:MOE_GROUPS + N_EXPERTS].set(jnp.transpose(we, (1, 0, 2)).reshape(d, N_EXPERTS))
    br = jnp.zeros((8, LANES), F32)
    br = br.at[0, :MOE_GROUPS].set(bg)
    br = br.at[0, MOE_GROUPS:MOE_GROUPS + N_EXPERTS].set(be.reshape(N_EXPERTS))
    return wr, br


def kernel(x, ab_w_in, rw_mu, rw_w0, rw_w2, rw_a0, rw_a2, rw_g2, rw_k_k, rw_k_a, rw_r_k, rw_gn_g, rw_gn_b, gla_gk_w2, gla_gk_b, gla_norm_g, ab_w_out, cd_w_in, s5_a_re, s5_a_im, s5_log_dt, s5_b_re, s5_b_im, s5_c_re, s5_c_im, s5_d, s5_glu_w, s5_glu_b, hg_lb, hg_norm_g, cd_w_out, ln1_g, ln1_b, moe_wg, moe_bg, moe_we, moe_be, moe_w1, moe_w3, moe_w2, ln2_g, ln2_b):
    batch, seq, d = x.shape
    t = batch * seq
    assert d == D_MODEL and seq % CHUNK == 0
    rows_per_seq = seq // S5_CHUNK
    assert rows_per_seq & (rows_per_seq - 1) == 0, "S5 chunk scan assumes a power-of-two chunk count"
    tm = min(ROW_TILE, t)
    assert t % tm == 0
    xt = x.reshape(t, d)
    ln1 = [_pad_rows(jnp.stack([ln1_g[l], ln1_b[l]]), 8) for l in range(DEPTH)]
    ln2 = [_pad_rows(jnp.stack([ln2_g[l], ln2_b[l]]), 8) for l in range(DEPTH)]

    j = 0
    w = ab_w_in[j]
    r_, wl_, k_, v_, al_, gl_ = 0, 512, 576, 1088, 1600, 1664
    gq, gk, gv, glow, ggate = 1792, 2048, 2304, 2816, 2832
    w0cols = jnp.concatenate([
        w[:, r_:r_ + 512], w[:, k_:k_ + 512], w[:, v_:v_ + 512], w[:, gv:gv + 512], w[:, ggate:ggate + 512],
        w[:, wl_:wl_ + 64], w[:, al_:al_ + 64], w[:, gl_:gl_ + 128], w[:, gq:gq + 256], w[:, gk:gk + 256],
        w[:, glow:glow + 16], jnp.zeros((d, 112), F32)], axis=1).astype(BF16)
    p0 = _matmul(xt, w0cols, tm)

    mu = rw_mu[j]
    vec = _pad_rows(jnp.stack([mu[r_:r_ + 512], mu[k_:k_ + 512], mu[v_:v_ + 512], rw_w0[j], rw_a0[j],
                               rw_k_k[j], rw_k_a[j], rw_r_k[j].reshape(-1), rw_gn_g[j], rw_gn_b[j]]), 16)
    mulo = _pad_rows(jnp.concatenate([mu[wl_:wl_ + 64], mu[al_:al_ + 64], mu[gl_:gl_ + 128]])[None], 8)
    w2p = _pad_rows(rw_w2[j], 128)
    a2p = jnp.zeros((128, 512), F32).at[64:].set(rw_a2[j])
    p0 = p0.reshape(batch, seq, -1)
    y_rw = _rwkv(p0, vec, mulo, w2p, a2p, rw_g2[j]).reshape(t, RW_WIDTH)
    y_gla = _gla(p0, _pad_rows(gla_gk_w2[j], 128), _pad_rows(gla_gk_b[j][None], 8),
                 _pad_rows(gla_norm_g[j][None], 8)).reshape(t, 512)

    wr, br = _route_weights(moe_wg[0], moe_bg[0], moe_we[0], moe_be[0])
    h, route, counts = _proj_route(y_rw, y_gla, xt, ab_w_out[j].astype(BF16), ln1[0], wr, br, tm)
    h = _moe(h, route, counts, moe_w1, moe_w3, moe_w2, 0, ln2[0])

    p1 = _matmul(h, cd_w_in[j].astype(BF16), tm)
    lb_sm = jax.nn.softmax(hg_lb.astype(F32), axis=0)
    lower = (jnp.cumsum(lb_sm, axis=0) - lb_sm[0])[1]
    y_hg = _hgrn(p1.reshape(batch, seq, -1), _pad_rows(lower[None], 8),
                 _pad_rows(hg_norm_g[j][None], 8)).reshape(t, 512)

    toep, bmat, emat, pw = _s5_tables(s5_a_re[j], s5_a_im[j], s5_log_dt[j], s5_b_re[j], s5_b_im[j],
                                      s5_c_re[j], s5_c_im[j], rows_per_seq)
    y_ssm = _s5_scan(p1.reshape(batch, seq, -1), toep, bmat, emat, pw).reshape(t, 512)
    y_s5 = _s5_post(y_ssm, p1, _pad_rows(jnp.stack([s5_d[j], s5_glu_b[j]]), 8),
                    s5_glu_w[j].astype(BF16), tm)

    wr, br = _route_weights(moe_wg[1], moe_bg[1], moe_we[1], moe_be[1])
    h2, route, counts = _proj_route(y_s5, y_hg, h, cd_w_out[j].astype(BF16), ln1[1], wr, br, tm)
    out = _moe(h2, route, counts, moe_w1, moe_w3, moe_w2, 1, ln2[1])
    return out.reshape(batch, seq, d)
```

```python
import functools
import math

import numpy as np
import jax
import jax.numpy as jnp
from jax import lax
from jax.experimental import pallas as pl
from jax.experimental.pallas import tpu as pltpu

F32 = jnp.float32
BF16 = jnp.bfloat16

D_MODEL = 1024
DEPTH = 2
RW_HEAD = 64
RW_WIDTH = 512
RW_GN_EPS = 64e-5
GLA_HEADS = 4
GLA_DK = 64
GLA_DV = 128
GLA_GATE_TAU = 16.0
S5_GROUP = 16
S5_GROUPS = 32
S5_STATE = 64
HG_HEADS = 4
HG_DK = 128
CHUNK = 64
S5_CHUNK = 16
S5_SCAN_ROWS = 32
S5_TILE_GROUPS = 8
NORM_EPS = 1e-5
MOE_GROUPS = 4
MOE_PER_GROUP = 8
N_EXPERTS = 32
EXPERT_HIDDEN = 512
MOE_BLOCK = 128
COMBINE_TILE = 256
EXPERT_CHUNK = 512
ROW_TILE = 512
DN_ALPHA = (2.0 * DEPTH) ** 0.25

LANES = 128
VMEM_LIMIT = 56 * 1024 * 1024


def _cparams(n_axes=1):
    return pltpu.CompilerParams(dimension_semantics=("arbitrary",) * n_axes,
                                vmem_limit_bytes=VMEM_LIMIT)


def _d(a, b):
    return jnp.dot(a, b, preferred_element_type=F32)


def _d_nt(a, b):
    return lax.dot_general(a, b, (((1,), (1,)), ((), ())), preferred_element_type=F32)


def _d_tn(a, b):
    return lax.dot_general(a, b, (((0,), (0,)), ((), ())), preferred_element_type=F32)


def _split(a):
    hi = a.astype(BF16)
    lo = (a - hi.astype(F32)).astype(BF16)
    return hi, lo


def _split3(a):
    hi = a.astype(BF16)
    r1 = a - hi.astype(F32)
    mid = r1.astype(BF16)
    lo = (r1 - mid.astype(F32)).astype(BF16)
    return hi, mid, lo


def _bdot(a, b):
    return _d(a.astype(BF16), b.astype(BF16))


def _bdot_nt(a, b):
    return _d_nt(a.astype(BF16), b.astype(BF16))


def _bdot_tn(a, b):
    return _d_tn(a.astype(BF16), b.astype(BF16))


def _hdot_with(d, a, b):
    ah, al = _split(a)
    bh, bl = _split(b)
    return d(ah, bh) + (d(ah, bl) + d(al, bh))


def _hdot(a, b):
    return _hdot_with(_d, a, b)


def _hdot_nt(a, b):
    return _hdot_with(_d_nt, a, b)


def _hdot_tn(a, b):
    return _hdot_with(_d_tn, a, b)


def _xdot_r(e, a):
    ah, am, al = _split3(a)
    return _d(e, ah) + (_d(e, am) + _d(e, al))


def _iota(shape, dim):
    return lax.broadcasted_iota(jnp.int32, shape, dim)


def _softplus(x):
    return jnp.maximum(x, 0.0) + jnp.log1p(jnp.exp(-jnp.abs(x)))


def _sigmoid(x):
    return 1.0 / (1.0 + jnp.exp(-x))


def _silu(x):
    return x * _sigmoid(x)


def _segsum(a, e):
    m = a.shape[0]
    s = _d(jnp.concatenate(_split3(a), axis=0), e)
    return s[:m] + (s[m:2 * m] + s[2 * m:])


def _cumsum_rows(g, chunk=None):
    n = g.shape[0]
    chunk = n if chunk is None else chunk
    shift = int(math.log2(chunk))
    r, c = _iota((n, n), 0), _iota((n, n), 1)
    tril = jnp.where((r >= c) & ((r >> shift) == (c >> shift)), 1.0, 0.0).astype(BF16)
    return _xdot_r(tril, g)


def _shift_mix(x, prev_ref, b, mu):
    c = x.shape[0]
    rolled = pltpu.roll(x, 1, 0)
    prev = jnp.where(_iota(x.shape, 0) == 0, jnp.broadcast_to(prev_ref[b, 0:1, :], x.shape), rolled)
    prev_ref[b, 0:1, :] = x[c - 1:c, :]
    return x + mu * (prev - x)


def _layer_norm(x, g, b):
    mu = jnp.mean(x, axis=-1, keepdims=True)
    xc = x - mu
    var = jnp.mean(xc * xc, axis=-1, keepdims=True)
    return xc * lax.rsqrt(var + NORM_EPS) * g + b


def _mm_body(x_ref, w_ref, o_ref):
    o_ref[...] = _d(x_ref[...].astype(BF16), w_ref[...])


def _matmul(x, w_bf16, tm):
    m, k = x.shape
    n = w_bf16.shape[1]
    return pl.pallas_call(
        _mm_body,
        grid=(m // tm,),
        in_specs=[pl.BlockSpec((tm, k), lambda i: (i, 0)),
                  pl.BlockSpec((k, n), lambda i: (0, 0))],
        out_specs=pl.BlockSpec((tm, n), lambda i: (i, 0)),
        out_shape=jax.ShapeDtypeStruct((m, n), F32),
        compiler_params=_cparams(1),
        name="in_proj",
    )(x, w_bf16)


_RV_MU_R, _RV_MU_K, _RV_MU_V, _RV_W0, _RV_A0, _RV_KK, _RV_KA, _RV_RK, _RV_GNG, _RV_GNB = range(10)


def _rwkv_body(r_ref, k_ref, v_ref, lo_ref, vec_ref, mulo_ref, w2_ref, a2_ref, g2_ref, ones_ref,
               o_ref, pr_ref, pk_ref, pv_ref, plo_ref, st_ref):
    nb, c = r_ref.shape[0], r_ref.shape[1]
    npair = RW_WIDTH // LANES

    @pl.when(pl.program_id(0) == 0)
    def _():
        pr_ref[...] = jnp.zeros_like(pr_ref)
        pk_ref[...] = jnp.zeros_like(pk_ref)
        pv_ref[...] = jnp.zeros_like(pv_ref)
        plo_ref[...] = jnp.zeros_like(plo_ref)
        st_ref[...] = jnp.zeros_like(st_ref)

    def vec(i):
        return vec_ref[i:i + 1, :]

    ones_bd = ones_ref[...]
    lane = _iota((c, LANES), 1)
    m1 = lane < RW_HEAD
    row2 = _iota((2 * c, 4 * c), 0)
    col2 = _iota((2 * c, 4 * c), 1) & (c - 1)
    tri = ((row2 < c) & (row2 > col2)) | ((row2 >= c) & ((row2 - c) >= col2))
    eye2 = jnp.where(_iota((2 * c, 2 * c), 0) == _iota((2 * c, 2 * c), 1), 1.0, 0.0)
    bd_p = (_iota((LANES, LANES), 0) >> 6) == (_iota((LANES, LANES), 1) >> 6)

    def halves(x):
        return jnp.concatenate([jnp.where(m1, x, 0.0), jnp.where(m1, 0.0, x)], axis=0)

    def stacked(ref, prev_ref, mu):
        return jnp.concatenate([_shift_mix(ref[b], prev_ref, b, mu) for b in range(nb)], axis=0)

    xr = stacked(r_ref, pr_ref, vec(_RV_MU_R))
    xk = stacked(k_ref, pk_ref, vec(_RV_MU_K))
    xv = stacked(v_ref, pv_ref, vec(_RV_MU_V))
    xlo = stacked(lo_ref, plo_ref, mulo_ref[0:1, :])
    lo_a = xlo[:, :LANES]
    lo_g = xlo[:, LANES:]
    w = -_softplus(-(vec(_RV_W0) + _bdot(jnp.tanh(lo_a), w2_ref[...]))) - 0.5
    g = -jnp.exp(w)
    a = _sigmoid(vec(_RV_A0) + _bdot(lo_a, a2_ref[...]))
    gate = _bdot(_sigmoid(lo_g), g2_ref[...])
    kk = xk * vec(_RV_KK)
    kk = kk / jnp.maximum(jnp.sqrt(_segsum(kk * kk, ones_bd)), 1e-12)
    k2 = xk * (1.0 + (a - 1.0) * vec(_RV_KA))
    gc = _cumsum_rows(g, c)
    g_last = jnp.concatenate([jnp.broadcast_to(gc[(b + 1) * c - 1:(b + 1) * c, :], (c, gc.shape[1]))
                              for b in range(nb)], axis=0)
    e_neg = jnp.exp(-gc)
    e_end = jnp.exp(g_last - gc)
    prep = dict(
        xv=xv,
        at=-kk * jnp.exp(gc - g),
        bt=(kk * a) * e_neg, kt=k2 * e_neg, rt=xr * jnp.exp(gc),
        bh=(kk * a) * e_end, kh=k2 * e_end,
        gam=jnp.exp(g_last))

    chains = [(b, p) for b in range(nb) for p in range(npair)]

    def part(ch, name):
        b, p = ch
        return prep[name][b * c:(b + 1) * c, p * LANES:(p + 1) * LANES]

    a_ak, a_row, vv, ak, pinv = {}, {}, {}, {}, {}
    for ch in chains:
        lhs = jnp.concatenate([part(ch, 'at'), part(ch, 'rt')], axis=0)
        rhs = jnp.concatenate([halves(part(ch, 'bt')), halves(part(ch, 'kt'))], axis=0)
        aa = jnp.where(tri, _bdot_nt(lhs, rhs), 0.0)
        a_ak[ch] = aa[:c, 2 * c:].astype(BF16)
        a_row[ch] = aa[c:, :].astype(BF16)
        abd = halves(aa[:c, :2 * c])
        pinv[ch] = eye2 + abd
        ak[ch] = abd
        vv[ch] = halves(part(ch, 'xv')).astype(BF16)
    nlev = int(math.log2(c))
    for lev in range(nlev):
        for ch in chains:
            akb = ak[ch].astype(BF16)
            if lev == 0:
                ak[ch] = _d(akb, akb)
            elif lev < nlev - 1:
                out = _d(akb, jnp.concatenate([akb, pinv[ch].astype(BF16)], axis=1))
                ak[ch] = out[:, :2 * c]
                pinv[ch] = pinv[ch] + out[:, 2 * c:]
            else:
                pinv[ch] = pinv[ch] + _d(akb, pinv[ch].astype(BF16))
    x2 = {ch: _d(a_ak[ch], vv[ch]) for ch in chains}

    sts = {ch: st_ref[ch[0] * npair + ch[1]] for ch in chains}
    xs = {ch: _d_nt(jnp.concatenate([part(ch, 'at'), part(ch, 'rt')], axis=0).astype(BF16),
                    sts[ch].astype(BF16)) for ch in chains}
    us = {}
    for ch in chains:
        u2 = _d(pinv[ch].astype(BF16), halves(xs[ch][:c] + x2[ch]).astype(BF16))
        us[ch] = u2[:c] + u2[c:]
    ys = {}
    for ch in chains:
        b, p = ch
        u = us[ch]
        ys[ch] = xs[ch][c:] + _d(a_row[ch], jnp.concatenate([halves(u).astype(BF16), vv[ch]], axis=0))
        upd = _d_tn(jnp.concatenate([u, part(ch, 'xv')], axis=0).astype(BF16),
                    jnp.concatenate([part(ch, 'bh'), part(ch, 'kh')], axis=0).astype(BF16))
        st_ref[b * npair + p] = sts[ch] * part(ch, 'gam')[0:1, :] + jnp.where(bd_p, upd, 0.0)

    inv_n = 1.0 / RW_HEAD
    y = jnp.concatenate([jnp.concatenate([ys[(b, p)] for p in range(npair)], axis=1)
                         for b in range(nb)], axis=0)
    mean = _segsum(y, ones_bd) * inv_n
    yc = y - mean
    var = _segsum(yc * yc, ones_bd) * inv_n
    yn = yc * lax.rsqrt(var + RW_GN_EPS) * vec(_RV_GNG) + vec(_RV_GNB)
    bonus = _segsum(xr * k2 * vec(_RV_RK), ones_bd) * xv
    out = (yn + bonus) * gate
    for b in range(nb):
        o_ref[b] = out[b * c:(b + 1) * c, :]


def _rwkv(p0, vec, mulo, w2p, a2p, g2):
    batch, seq, _ = p0.shape
    nc = seq // CHUNK
    c = CHUNK
    ones_bd = jnp.asarray(np.kron(np.eye(RW_WIDTH // RW_HEAD), np.ones((RW_HEAD, RW_HEAD))), BF16)

    def col(j, width):
        return pl.BlockSpec((batch, c, width), lambda i: (0, i, j))

    def full(shape):
        return pl.BlockSpec(shape, lambda i: (0,) * len(shape))

    return pl.pallas_call(
        _rwkv_body,
        grid=(nc,),
        in_specs=[col(0, 512), col(1, 512), col(2, 512), col(10, 256),
                  full((16, 512)), full((8, 256)), full((128, 512)), full((128, 512)),
                  full((128, 512)), full((512, 512))],
        out_specs=pl.BlockSpec((batch, c, 512), lambda i: (0, i, 0)),
        out_shape=jax.ShapeDtypeStruct((batch, seq, RW_WIDTH), F32),
        scratch_shapes=[pltpu.VMEM((batch, 8, 512), F32), pltpu.VMEM((batch, 8, 512), F32),
                        pltpu.VMEM((batch, 8, 512), F32), pltpu.VMEM((batch, 8, 256), F32),
                        pltpu.VMEM((batch * RW_WIDTH // LANES, LANES, LANES), F32)],
        compiler_params=_cparams(1),
        name="rwkv7",
    )(p0, p0, p0, p0, vec, mulo, w2p, a2p, g2, ones_bd)


def _gla_core(qs, ks, vs, gs, st_ref, heads_per_block):
    nb = len(qs)
    c = qs[0].shape[0]
    hpb = heads_per_block
    nblk = qs[0].shape[1] // LANES
    dk_shift = int(math.log2(LANES // hpb))
    row = _iota(qs[0].shape, 0)

    prep = []
    for q, k, g in zip(qs, ks, gs):
        b = _cumsum_rows(g)

        def brow(i, b=b):
            return jnp.broadcast_to(b[i:i + 1, :], b.shape)

        b15, b31, b47, blast = brow(15), brow(31), brow(47), brow(c - 1)
        ref_b = jnp.where(row < 32, b15, b47)
        ref_d = jnp.where(row < 16, 0.0, jnp.where(row < 32, b15, jnp.where(row < 48, b31, b47)))
        prep.append(dict(
            q_a=q * jnp.exp(jnp.minimum(b - b31, 0.0)), k_a=k * jnp.exp(jnp.minimum(b31 - b, 0.0)),
            q_b=q * jnp.exp(jnp.minimum(b - ref_b, 0.0)), k_b=k * jnp.exp(jnp.minimum(ref_b - b, 0.0)),
            q_d=q * jnp.exp(b - ref_d), k_d=k * jnp.exp(ref_d - b),
            q_i=q * jnp.exp(b), k_s=k * jnp.exp(blast - b), gam=jnp.exp(b[c - 1:c, :])))

    ri = _iota((hpb * c, c), 0) & (c - 1)
    ci = _iota((hpb * c, c), 1)
    mask_a = (ri >= 32) & (ci < 32)
    mask_b = ((ri >> 5) == (ci >> 5)) & (((ri >> 4) & 1) == 1) & (((ci >> 4) & 1) == 0)
    mask_d = ((ri >> 4) == (ci >> 4)) & (ri >= ci)
    lane = _iota((c, LANES), 1)
    bd = (_iota((hpb * LANES, LANES), 0) >> 7) == (_iota((hpb * LANES, LANES), 1) >> dk_shift)

    def heads_rows(x):
        if hpb == 1:
            return x
        return jnp.concatenate([jnp.where((lane >> dk_shift) == h, x, 0.0) for h in range(hpb)], axis=0)

    chains = [(bi, blk) for bi in range(nb) for blk in range(nblk)]

    def part(ch, name):
        bi, blk = ch
        return prep[bi][name][:, blk * LANES:(blk + 1) * LANES]

    v_p = {ch: vs[ch[0]][:, ch[1] * hpb * LANES:(ch[1] + 1) * hpb * LANES].astype(BF16) for ch in chains}
    probs = {}
    for ch in chains:
        s_a = _bdot_nt(heads_rows(part(ch, 'q_a')), part(ch, 'k_a'))
        s_b = _bdot_nt(heads_rows(part(ch, 'q_b')), part(ch, 'k_b'))
        s_d = _bdot_nt(heads_rows(part(ch, 'q_d')), part(ch, 'k_d'))
        probs[ch] = (jnp.where(mask_a, s_a, 0.0) + jnp.where(mask_b, s_b, 0.0)
                     + jnp.where(mask_d, s_d, 0.0)).astype(BF16)
    outs = {}
    for ch in chains:
        pv = _d(probs[ch], v_p[ch])
        o = pv[:c]
        for h in range(1, hpb):
            o = jnp.where((_iota(o.shape, 1) >> 7) == h, pv[h * c:(h + 1) * c], o)
        si = ch[0] * nblk + ch[1]
        sp = st_ref[si]
        outs[ch] = o + _bdot_nt(part(ch, 'q_i'), sp)
        upd = _d_tn(v_p[ch], part(ch, 'k_s').astype(BF16))
        st_ref[si] = sp * part(ch, 'gam') + jnp.where(bd, upd, 0.0)
    return [jnp.concatenate([outs[(bi, blk)] for blk in range(nblk)], axis=1) for bi in range(nb)]


def _gated_rmsnorm(o, gate, norm_g):
    nh = o.shape[1] // LANES
    outs = []
    for h in range(nh):
        sl = slice(h * LANES, (h + 1) * LANES)
        oh = o[:, sl]
        ms = jnp.mean(oh * oh, axis=-1, keepdims=True)
        outs.append(oh * lax.rsqrt(ms + NORM_EPS) * norm_g * _silu(gate[:, sl]))
    return jnp.concatenate(outs, axis=1)


def _gla_body(q_ref, k_ref, v_ref, gate_ref, gk_ref, w2_ref, vec_ref, ng_ref, o_ref, st_ref):
    nb = q_ref.shape[0]

    @pl.when(pl.program_id(0) == 0)
    def _():
        st_ref[...] = jnp.zeros_like(st_ref)

    gs = [-_softplus(-(_hdot(gk_ref[b], w2_ref[...]) + vec_ref[0:1, :])) * (1.0 / GLA_GATE_TAU)
          for b in range(nb)]
    qs = [q_ref[b] * (GLA_DK ** -0.5) for b in range(nb)]
    os_ = _gla_core(qs, [k_ref[b] for b in range(nb)], [v_ref[b] for b in range(nb)], gs, st_ref, 2)
    for b in range(nb):
        o_ref[b] = _gated_rmsnorm(os_[b], gate_ref[b], ng_ref[0:1, :])


def _seq_specs(batch, c):
    def col(j, width):
        return pl.BlockSpec((batch, c, width), lambda i: (0, i, j))

    def full(shape):
        return pl.BlockSpec(shape, lambda i: (0,) * len(shape))

    return col, full


def _gla(p0, gk_w2p, gk_b, norm_g):
    batch, seq, _ = p0.shape
    c = CHUNK
    col, full = _seq_specs(batch, c)
    return pl.pallas_call(
        _gla_body,
        grid=(seq // c,),
        in_specs=[col(11, 256), col(12, 256), col(3, 512), col(4, 512), col(26, 128),
                  full((128, 256)), full((8, 256)), full((8, 128))],
        out_specs=pl.BlockSpec((batch, c, 512), lambda i: (0, i, 0)),
        out_shape=jax.ShapeDtypeStruct((batch, seq, 512), F32),
        scratch_shapes=[pltpu.VMEM((batch * 2, 2 * LANES, LANES), F32)],
        compiler_params=_cparams(1),
        name="gla",
    )(p0, p0, p0, p0, p0, gk_w2p, gk_b, norm_g)


def _hgrn_body(q_ref, f_ref, i_ref, gate_ref, lb_ref, ng_ref, o_ref, st_ref):
    nb = q_ref.shape[0]

    @pl.when(pl.program_id(0) == 0)
    def _():
        st_ref[...] = jnp.zeros_like(st_ref)

    lb = lb_ref[0:1, :]
    qs, ks, gs = [], [], []
    for b in range(nb):
        f = f_ref[b]
        gs.append(jnp.log(lb + (1.0 - lb) * _sigmoid(f)))
        ks.append((1.0 - lb) * _sigmoid(-f))
        qs.append(_silu(q_ref[b]))
    os_ = _gla_core(qs, ks, [i_ref[b] for b in range(nb)], gs, st_ref, 1)
    for b in range(nb):
        o_ref[b] = _gated_rmsnorm(os_[b], gate_ref[b], ng_ref[0:1, :])


def _hgrn(p1, lb, norm_g):
    batch, seq, _ = p1.shape
    c = CHUNK
    col, full = _seq_specs(batch, c)
    return pl.pallas_call(
        _hgrn_body,
        grid=(seq // c,),
        in_specs=[col(1, 512), col(2, 512), col(3, 512), col(4, 512), full((8, 512)), full((8, 128))],
        out_specs=pl.BlockSpec((batch, c, 512), lambda i: (0, i, 0)),
        out_shape=jax.ShapeDtypeStruct((batch, seq, 512), F32),
        scratch_shapes=[pltpu.VMEM((batch * HG_HEADS, LANES, LANES), F32)],
        compiler_params=_cparams(1),
        name="hgrn2",
    )(p1, p1, p1, p1, lb, norm_g)


def _s5_body(u_ref, dcat_ref, bm_ref, eblk_ref, pw_ref, o_ref, toep_ref, e_ref):
    seq = u_ref.shape[0]
    rows = seq // S5_CHUNK

    @pl.when(pl.program_id(1) == 0)
    def _():
        for tk in range(S5_CHUNK):
            lo = (S5_CHUNK - 1 - tk) * LANES
            toep_ref[tk * LANES:(tk + 1) * LANES, :] = dcat_ref[0, :, lo:lo + S5_CHUNK * LANES]
            e_ref[:, tk * LANES:(tk + 1) * LANES] = eblk_ref[0, tk]

    u = jnp.concatenate([u_ref[pl.ds(tk, rows, stride=S5_CHUNK), :] for tk in range(S5_CHUNK)],
                        axis=1).astype(BF16)
    s = _d(u, bm_ref[0])
    rin = _iota(s.shape, 0)
    pw = pw_ref[0]

    def swap_halves(x):
        return jnp.concatenate([pltpu.roll(x[:, k * LANES:(k + 1) * LANES], S5_STATE, 1)
                                for k in range(x.shape[1] // LANES)], axis=1)

    h = s
    for lev in range(int(math.log2(rows))):
        sh = 1 << lev
        hs = jnp.where(rin >= sh, pltpu.roll(h, sh, 0), 0.0)
        h = h + pw[2 * lev:2 * lev + 1, :] * hs + pw[2 * lev + 1:2 * lev + 2, :] * swap_halves(hs)
    hprev = jnp.where(rin >= 1, pltpu.roll(h, 1, 0), 0.0)
    y = _d(u, toep_ref[...]) + _bdot(hprev, e_ref[...])
    for tk in range(S5_CHUNK):
        o_ref[pl.ds(tk, rows, stride=S5_CHUNK), :] = y[:, tk * LANES:(tk + 1) * LANES]


def _s5_scan(p1, dcat, bmat, emat, pw):
    batch, seq, _ = p1.shape
    nt, width, nstate = bmat.shape

    def table(shape):
        return pl.BlockSpec((1,) + shape, lambda j, b: (j, 0, 0), pipeline_mode=pl.Buffered(1))

    return pl.pallas_call(
        _s5_body,
        grid=(nt, batch),
        in_specs=[pl.BlockSpec((None, seq, LANES), lambda j, b: (b, 0, j)),
                  table(dcat.shape[1:]), table((width, nstate)),
                  pl.BlockSpec((1,) + emat.shape[1:], lambda j, b: (j, 0, 0, 0), pipeline_mode=pl.Buffered(1)),
                  table((S5_SCAN_ROWS, nstate))],
        out_specs=pl.BlockSpec((None, seq, LANES), lambda j, b: (b, 0, j)),
        out_shape=jax.ShapeDtypeStruct((batch, seq, nt * LANES), F32),
        scratch_shapes=[pltpu.VMEM((width, width), BF16), pltpu.VMEM((nstate, width), BF16)],
        compiler_params=_cparams(2),
        name="s5_scan",
    )(p1, dcat, bmat, emat, pw)


def _s5_post_body(y_ref, u_ref, vec_ref, w_ref, o_ref):
    y = y_ref[...] + vec_ref[0:1, :] * u_ref[...]
    y = 0.5 * y * (1.0 + jnp.tanh(math.sqrt(2.0 / math.pi) * (y + 0.044715 * (y * y * y))))
    o_ref[...] = y * _sigmoid(_bdot(y, w_ref[...]) + vec_ref[1:2, :])


def _s5_post(y_ssm, p1, vec, glu_w, tm):
    t = y_ssm.shape[0]
    return pl.pallas_call(
        _s5_post_body,
        grid=(t // tm,),
        in_specs=[pl.BlockSpec((tm, 512), lambda i: (i, 0)),
                  pl.BlockSpec((tm, 512), lambda i: (i, 0)),
                  pl.BlockSpec((8, 512), lambda i: (0, 0)),
                  pl.BlockSpec((512, 512), lambda i: (0, 0))],
        out_specs=pl.BlockSpec((tm, 512), lambda i: (i, 0)),
        out_shape=jax.ShapeDtypeStruct((t, 512), F32),
        compiler_params=_cparams(1),
        name="s5_post",
    )(y_ssm, p1, vec, glu_w)


def _s5_tables(a_re, a_im, log_dt, b_re, b_im, c_re, c_im, rows_per_seq):
    c = S5_CHUNK
    lam_re = jnp.minimum(a_re, -1e-4)
    lam_im = a_im
    dt = jnp.exp(log_dt)[:, None]
    mag = jnp.exp(lam_re * dt)
    abar_re = mag * jnp.cos(lam_im * dt)
    abar_im = mag * jnp.sin(lam_im * dt)
    den = lam_re * lam_re + lam_im * lam_im
    num_re = abar_re - 1.0
    z_re = (num_re * lam_re + abar_im * lam_im) / den
    z_im = (abar_im * lam_re - num_re * lam_im) / den

    def power(n):
        n = jnp.asarray(n, F32)[..., None, None]
        m = jnp.exp(n * (lam_re * dt))
        return m * jnp.cos(n * (lam_im * dt)), m * jnp.sin(n * (lam_im * dt))

    def cmul(ar, ai, br, bi):
        return ar * br - ai * bi, ar * bi + ai * br

    tau = jnp.arange(c)
    p_re, p_im = power(tau)
    zb_re, zb_im = cmul(z_re[..., None], z_im[..., None], b_re, b_im)
    cp_re, cp_im = cmul(c_re[None], c_im[None], p_re[:, :, None, :], p_im[:, :, None, :])
    hi = lax.Precision.HIGHEST
    kern = (jnp.einsum('tgon,gni->gtoi', cp_re, zb_re, precision=hi)
            - jnp.einsum('tgon,gni->gtoi', cp_im, zb_im, precision=hi))
    q_re, q_im = power(c - 1 - tau)
    bm_re, bm_im = cmul(q_re[..., None], q_im[..., None], zb_re[None], zb_im[None])
    bmat = jnp.concatenate([jnp.transpose(bm_re, (1, 0, 3, 2)), jnp.transpose(bm_im, (1, 0, 3, 2))],
                           axis=-1).reshape(S5_GROUPS, c * S5_GROUP, 2 * S5_STATE)
    r_re, r_im = power(tau + 1)
    nlev = int(math.log2(rows_per_seq))
    assert 2 * nlev <= S5_SCAN_ROWS
    s_re, s_im = power(c * (2 ** jnp.arange(nlev)))
    pw = jnp.zeros((S5_GROUPS, S5_SCAN_ROWS, 2 * S5_STATE), F32)
    pw = pw.at[:, 0:2 * nlev:2, :].set(jnp.transpose(jnp.concatenate([s_re, s_re], -1), (1, 0, 2)))
    pw = pw.at[:, 1:2 * nlev:2, :].set(jnp.transpose(jnp.concatenate([-s_im, s_im], -1), (1, 0, 2)))
    nt, gl = S5_GROUPS // S5_TILE_GROUPS, S5_TILE_GROUPS
    spread = jnp.tile(jnp.eye(S5_GROUP, dtype=F32), (1, gl))
    lane_group = jnp.arange(LANES) // S5_GROUP
    kern5 = jnp.transpose(kern.reshape(nt, gl, c, S5_GROUP, S5_GROUP), (0, 2, 1, 4, 3))
    dblk = jnp.einsum('jtgio,op->jtgip', kern5, spread, precision=hi)
    dblk = jnp.where((jnp.arange(gl)[:, None, None] == lane_group[None, None, :]), dblk, 0.0)
    dblk = dblk.reshape(nt, c, LANES, LANES)
    dcat = jnp.transpose(dblk, (0, 2, 1, 3)).reshape(nt, LANES, c * LANES)
    dcat = jnp.pad(dcat, ((0, 0), (0, 0), ((c - 1) * LANES, 0)))
    bmat5 = jnp.transpose(bmat.reshape(nt, gl, c, S5_GROUP, 2 * S5_STATE), (0, 2, 1, 3, 4))
    row_group = (jnp.arange(c * LANES) // S5_GROUP) % gl
    col_tile = jnp.arange(gl * 2 * S5_STATE) // (2 * S5_STATE)
    bmat_t = jnp.where(row_group[:, None] == col_tile[None, :],
                       jnp.tile(bmat5.reshape(nt, c * LANES, 2 * S5_STATE), (1, 1, gl)), 0.0)
    own_group = (jnp.arange(S5_GROUPS) % gl)[:, None, None] == lane_group[None, None, :]

    def spread_c(cmat):
        cs = jnp.where(own_group, jnp.einsum('gon,op->gnp', cmat, spread, precision=hi), 0.0)
        return cs.reshape(nt, 1, gl, S5_STATE, LANES)

    def per_state(pw_):
        return jnp.transpose(pw_.reshape(c, nt, gl, S5_STATE), (1, 0, 2, 3))[..., None]

    cs_re, cs_im, rr, ri = spread_c(c_re), spread_c(c_im), per_state(r_re), per_state(r_im)
    eblk = jnp.concatenate([cs_re * rr - cs_im * ri, -(cs_re * ri + cs_im * rr)], axis=3)
    eblk = eblk.reshape(nt, c, gl * 2 * S5_STATE, LANES)
    pw_t = jnp.transpose(pw.reshape(nt, gl, S5_SCAN_ROWS, 2 * S5_STATE), (0, 2, 1, 3))
    pw_t = pw_t.reshape(nt, S5_SCAN_ROWS, gl * 2 * S5_STATE)
    return dcat.astype(BF16), bmat_t.astype(BF16), eblk.astype(BF16), pw_t


def _proj_route_body(ya_ref, yb_ref, x_ref, w_ref, ln_ref, wr_ref, br_ref,
                     h_ref, route_ref, cnt_ref, carry_ref):
    tm = ya_ref.shape[0]
    half = w_ref.shape[0] // 2

    @pl.when(pl.program_id(0) == 0)
    def _():
        carry_ref[...] = jnp.zeros_like(carry_ref)

    mix = _bdot(ya_ref[...], w_ref[:half, :]) + _bdot(yb_ref[...], w_ref[half:, :])
    h = _layer_norm(DN_ALPHA * x_ref[...] + mix, ln_ref[0:1, :], ln_ref[1:2, :])
    h_ref[...] = h

    logits = _hdot(h, wr_ref[...]) + br_ref[0:1, :]
    lane = _iota((tm, LANES), 1).astype(F32)
    neg = -jnp.inf

    def softmax_masked(mask):
        xm = jnp.where(mask, logits, neg)
        m = jnp.max(xm, axis=-1, keepdims=True)
        e = jnp.exp(xm - m)
        return e / jnp.sum(e, axis=-1, keepdims=True)

    def top1(pm):
        m = jnp.max(pm, axis=-1, keepdims=True)
        idx = jnp.min(jnp.where(pm == m, lane, float(LANES)), axis=-1, keepdims=True)
        return m, idx

    coarse = jnp.where(lane < MOE_GROUPS, softmax_masked(lane < MOE_GROUPS), -1.0)
    p_grp, grp = top1(coarse)
    lo = MOE_GROUPS + MOE_PER_GROUP * grp
    fmask = (lane >= lo) & (lane < lo + MOE_PER_GROUP)
    fine = jnp.where(fmask, softmax_masked(fmask), -1.0)
    p1, j1 = top1(fine)
    p2, j2 = top1(jnp.where(lane == j1, -1.0, fine))
    denom = p1 + p2
    g1 = p_grp * (p1 / denom)
    g2 = p_grp * (p2 / denom)
    e1 = j1 - MOE_GROUPS
    e2 = j2 - MOE_GROUPS

    oh1 = jnp.where(lane == e1, 1.0, 0.0)
    oh2 = jnp.where(lane == e2, 1.0, 0.0)
    cnt = oh1 + oh2
    strict = jnp.where(_iota((tm, tm), 0) > _iota((tm, tm), 1), 1.0, 0.0).astype(BF16)
    before = _d(strict, cnt.astype(BF16)) + carry_ref[0:1, :]
    r1 = jnp.sum(oh1 * before, axis=-1, keepdims=True)
    r2 = jnp.sum(oh2 * before, axis=-1, keepdims=True)
    carry_ref[0:1, :] = carry_ref[0:1, :] + jnp.sum(cnt, axis=0, keepdims=True)
    cnt_ref[...] = carry_ref[...]

    out = jnp.where(lane == 0, e1, 0.0)
    out = jnp.where(lane == 1, e2, out)
    out = jnp.where(lane == 2, r1, out)
    out = jnp.where(lane == 3, r2, out)
    out = jnp.where(lane == 4, g1, out)
    out = jnp.where(lane == 5, g2, out)
    route_ref[...] = out[:, :8]


def _proj_route(ya, yb, resid, w_out, ln, wr, br, tm):
    t, d = resid.shape
    return pl.pallas_call(
        _proj_route_body,
        grid=(t // tm,),
        in_specs=[pl.BlockSpec((tm, 512), lambda i: (i, 0)),
                  pl.BlockSpec((tm, 512), lambda i: (i, 0)),
                  pl.BlockSpec((tm, d), lambda i: (i, 0)),
                  pl.BlockSpec((d, d), lambda i: (0, 0)),
                  pl.BlockSpec((8, d), lambda i: (0, 0)),
                  pl.BlockSpec((d, LANES), lambda i: (0, 0)),
                  pl.BlockSpec((8, LANES), lambda i: (0, 0))],
        out_specs=[pl.BlockSpec((tm, d), lambda i: (i, 0)),
                   pl.BlockSpec((tm, 8), lambda i: (i, 0)),
                   pl.BlockSpec((8, LANES), lambda i: (0, 0))],
        out_shape=[jax.ShapeDtypeStruct((t, d), F32),
                   jax.ShapeDtypeStruct((t, 8), F32),
                   jax.ShapeDtypeStruct((8, LANES), F32)],
        scratch_shapes=[pltpu.VMEM((8, LANES), F32)],
        compiler_params=_cparams(1),
        name="proj_ln_route",
    )(ya, yb, resid, w_out, ln, wr, br)


def _row_copy(src_ref, src_row, dst_ref, dst_row, sem):
    return pltpu.make_async_copy(src_ref.at[pl.ds(src_row, 1)], dst_ref.at[pl.ds(dst_row, 1)], sem)


def _start_all(copies):
    for n, cp in enumerate(copies):
        cp.start(priority=n % 2)


def _dispatch_body(pad_start_ref, pad_len_ref, used_rows_ref, dest_ref, h_ref, rows_ref, hbuf_ref, zbuf_ref,
                   lsem, ssem, zsem):
    i = pl.program_id(0)
    last = pl.num_programs(0) - 1
    tm = hbuf_ref.shape[1]
    slot = i % 2

    def load(tile, s):
        return pltpu.make_async_copy(h_ref.at[pl.ds(tile * tm, tm)], hbuf_ref.at[s], lsem.at[s])

    def scatters(s):
        return [_row_copy(hbuf_ref.at[s], r, rows_ref, dest_ref[0, 0, 2 * r + q], ssem.at[s])
                for r in range(tm) for q in range(2)]

    @pl.when(i == 0)
    def _():
        load(0, 0).start()

    @pl.when(i >= 1)
    def _():
        for cp in scatters(1 - slot):
            cp.wait()

    @pl.when(i < last)
    def _():
        load(i + 1, 1 - slot).start()

    load(i, slot).wait()
    _start_all(scatters(slot))

    @pl.when(i == last)
    def _():
        for cp in scatters(slot):
            cp.wait()
        zbuf_ref[...] = jnp.zeros_like(zbuf_ref)
        sub = 8
        half = zbuf_ref.shape[0]

        def pad_fills():
            out = []
            for e in range(N_EXPERTS):
                start = pad_start_ref[e]
                npad = pad_len_ref[e]
                end = start + npad
                run = half
                while run >= sub:
                    end = end - (npad & run)
                    dst = rows_ref.at[pl.ds(pl.multiple_of(end, run), run)]
                    out.append(((npad & run) != 0,
                                pltpu.make_async_copy(zbuf_ref.at[pl.ds(0, run)], dst, zsem)))
                    run //= 2
                for k in range(sub - 1):
                    out.append((k < (npad & (sub - 1)), _row_copy(zbuf_ref, 0, rows_ref, start + k, zsem)))
            return out

        for pred, cp in pad_fills():
            pl.when(pred)(cp.start)
        for pred, cp in pad_fills():
            pl.when(pred)(cp.wait)

        def tail(b):
            return pltpu.make_async_copy(
                zbuf_ref, rows_ref.at[pl.ds(pl.multiple_of(b * half, half), half)], zsem)

        first_free = used_rows_ref[0] // half
        n_half = rows_ref.shape[0] // half
        lax.fori_loop(first_free, n_half, lambda b, c: (tail(b).start(), c)[1], 0)
        lax.fori_loop(first_free, n_half, lambda b, c: (tail(b).wait(), c)[1], 0)


def _dispatch(pad_start, pad_len, used_rows, dest3, h, n_rows):
    t, d = h.shape
    nt, _, tm2 = dest3.shape
    tm = tm2 // 2
    grid_spec = pltpu.PrefetchScalarGridSpec(
        num_scalar_prefetch=3,
        grid=(nt,),
        in_specs=[pl.BlockSpec((1, 1, tm2), lambda i, *_: (i, 0, 0), memory_space=pltpu.SMEM),
                  pl.BlockSpec(memory_space=pl.ANY)],
        out_specs=pl.BlockSpec(memory_space=pl.ANY),
        scratch_shapes=[pltpu.VMEM((2, tm, d), F32), pltpu.VMEM((MOE_BLOCK // 2, d), F32),
                        pltpu.SemaphoreType.DMA((2,)), pltpu.SemaphoreType.DMA((2,)),
                        pltpu.SemaphoreType.DMA],
    )
    return pl.pallas_call(
        _dispatch_body,
        grid_spec=grid_spec,
        out_shape=jax.ShapeDtypeStruct((n_rows, d), F32),
        compiler_params=_cparams(1),
        name="moe_dispatch",
    )(pad_start, pad_len, used_rows, dest3, h)


def _expert_body(first_ref, nblk_ref, nused_ref, x_ref, w1_ref, w3_ref, w2_ref, y_ref,
                 xbuf_ref, ybuf_ref, w1b_ref, w3b_ref, w2b_ref, ypend_ref, ycnt_ref, xcnt_ref, xpre_ref,
                 xsem, ysem):
    e = pl.program_id(0)
    row0 = first_ref[e] * MOE_BLOCK
    nblk = nblk_ref[e]
    big = xbuf_ref.shape[1]
    per_big = big // MOE_BLOCK

    @pl.when(e == 0)
    def _():
        ypend_ref[0] = 0
        ypend_ref[1] = 0
        ycnt_ref[0] = 0
        xcnt_ref[0] = 0
        xpre_ref[0] = 0

    def drain(slot):
        for size in (big, MOE_BLOCK):
            @pl.when(ypend_ref[slot] == size)
            def _(size=size):
                pltpu.make_async_copy(ybuf_ref.at[slot, pl.ds(0, size)], y_ref.at[pl.ds(0, size)],
                                      ysem.at[slot]).wait()
        ypend_ref[slot] = 0

    def cast_weights():
        w1b_ref[...] = w1_ref[0].astype(BF16)
        w3b_ref[...] = w3_ref[0].astype(BF16)
        w2b_ref[...] = w2_ref[0].astype(BF16)

    def x_copy(row, size, slot):
        src = x_ref.at[pl.ds(pl.multiple_of(row, MOE_BLOCK), size)]
        return pltpu.make_async_copy(src, xbuf_ref.at[slot, pl.ds(0, size)], xsem.at[slot])

    def start_x(row, is_big, slot):
        pl.when(is_big)(lambda: x_copy(row, big, slot).start(priority=1))
        pl.when(jnp.logical_not(is_big))(lambda: x_copy(row, MOE_BLOCK, slot).start(priority=1))

    nbig = nblk // per_big
    ntail = nblk - nbig * per_big
    tail_row = row0 + nbig * big
    e_next = jnp.minimum(e + 1, pl.num_programs(0) - 1)
    next_ok = (e + 1 < pl.num_programs(0)) & (nblk_ref[e_next] > 0)

    def prefetch_next_expert(slot):
        @pl.when(next_ok)
        def _():
            start_x(first_ref[e_next] * MOE_BLOCK, nblk_ref[e_next] >= per_big, slot)
            xpre_ref[0] = 1

    def stream(start_row, count, size, casts_first, start_following):
        def y_copy(j, s):
            dst = y_ref.at[pl.ds(pl.multiple_of(start_row + j * size, MOE_BLOCK), size)]
            return pltpu.make_async_copy(ybuf_ref.at[s, pl.ds(0, size)], dst, ysem.at[s])

        @pl.when(count > 0)
        def _():
            pl.when(casts_first)(cast_weights)

            def chunk(j, carry):
                s = xcnt_ref[0] % 2
                xcnt_ref[0] = xcnt_ref[0] + 1

                @pl.when(j + 1 < count)
                def _():
                    x_copy(start_row + (j + 1) * size, size, 1 - s).start(priority=1)

                @pl.when(j + 1 == count)
                def _():
                    start_following(1 - s)

                x_copy(start_row + j * size, size, s).wait()
                xb = xbuf_ref[s, pl.ds(0, size), :].astype(BF16)
                hid = _silu(_d(xb, w1b_ref[...])) * _d(xb, w3b_ref[...])
                ys = ycnt_ref[0] % 2
                drain(ys)
                ybuf_ref[ys, pl.ds(0, size), :] = _d(hid.astype(BF16), w2b_ref[...])
                y_copy(j, ys).start()
                ypend_ref[ys] = size
                ycnt_ref[0] = ycnt_ref[0] + 1
                return carry

            lax.fori_loop(0, count, chunk, 0)

    @pl.when(nblk > 0)
    def _():
        @pl.when(xpre_ref[0] == 0)
        def _():
            start_x(row0, nbig > 0, xcnt_ref[0] % 2)

        xpre_ref[0] = 0

        def after_big(slot):
            pl.when(ntail > 0)(lambda: x_copy(tail_row, MOE_BLOCK, slot).start(priority=1))
            pl.when(ntail == 0)(lambda: prefetch_next_expert(slot))

        stream(row0, nbig, big, nbig > 0, after_big)
        stream(tail_row, ntail, MOE_BLOCK, nbig == 0, prefetch_next_expert)

    @pl.when(e == pl.num_programs(0) - 1)
    def _():
        drain(0)
        drain(1)
        ybuf_ref[0, pl.ds(0, MOE_BLOCK), :] = jnp.zeros((MOE_BLOCK, ybuf_ref.shape[2]), F32)
        nused = nused_ref[0]
        ntot = y_ref.shape[0] // MOE_BLOCK

        def fill(b, carry):
            cp = pltpu.make_async_copy(ybuf_ref.at[0, pl.ds(0, MOE_BLOCK)],
                                       y_ref.at[pl.ds(b * MOE_BLOCK, MOE_BLOCK)], ysem.at[0])
            cp.start()
            cp.wait()
            return carry

        lax.fori_loop(nused, ntot, fill, 0)


def _experts(first_blk, nblk, nused, x_rows, w1, w3, w2, layer):
    r, d = x_rows.shape
    hid = w1.shape[-1]
    grid_spec = pltpu.PrefetchScalarGridSpec(
        num_scalar_prefetch=3,
        grid=(N_EXPERTS,),
        in_specs=[pl.BlockSpec(memory_space=pl.ANY),
                  pl.BlockSpec((None, 1, d, hid), lambda e, *_: (layer, e, 0, 0)),
                  pl.BlockSpec((None, 1, d, hid), lambda e, *_: (layer, e, 0, 0)),
                  pl.BlockSpec((None, 1, hid, d), lambda e, *_: (layer, e, 0, 0))],
        out_specs=pl.BlockSpec(memory_space=pl.ANY),
        scratch_shapes=[pltpu.VMEM((2, EXPERT_CHUNK, d), F32), pltpu.VMEM((2, EXPERT_CHUNK, d), F32),
                        pltpu.VMEM((d, hid), BF16), pltpu.VMEM((d, hid), BF16), pltpu.VMEM((hid, d), BF16),
                        pltpu.SMEM((2,), jnp.int32), pltpu.SMEM((1,), jnp.int32),
                        pltpu.SMEM((1,), jnp.int32), pltpu.SMEM((1,), jnp.int32),
                        pltpu.SemaphoreType.DMA((2,)), pltpu.SemaphoreType.DMA((2,))],
    )
    return pl.pallas_call(
        _expert_body,
        grid_spec=grid_spec,
        out_shape=jax.ShapeDtypeStruct((r, d), F32),
        compiler_params=_cparams(1),
        name="moe_experts",
    )(first_blk, nblk, nused, x_rows, w1, w3, w2)


def _combine_body(dest_ref, dest_next_ref, gate_ref, h_ref, ln_ref, rows_ref, o_ref, buf_ref, sem):
    i = pl.program_id(0)
    last = pl.num_programs(0) - 1
    tm = h_ref.shape[0]

    def gather(dref, slot):
        return [_row_copy(rows_ref, dref[0, 0, 2 * r + s], buf_ref.at[slot, s], r, sem.at[slot])
                for r in range(tm) for s in range(2)]

    @pl.when(i == 0)
    def _():
        _start_all(gather(dest_ref, 0))

    _start_all(gather(dest_next_ref, (i + 1) % 2))
    slot = i % 2
    for cp in gather(dest_ref, slot):
        cp.wait()
    gate = gate_ref[...]
    y = gate[:, 4:5] * buf_ref[slot, 0] + gate[:, 5:6] * buf_ref[slot, 1]
    o_ref[...] = _layer_norm(DN_ALPHA * h_ref[...] + y, ln_ref[0:1, :], ln_ref[1:2, :])

    @pl.when(i == last)
    def _():
        for cp in gather(dest_next_ref, (i + 1) % 2):
            cp.wait()


def _combine(dest3, route, h, ln, y_rows, tm):
    t, d = h.shape
    nt = t // tm
    return pl.pallas_call(
        _combine_body,
        grid=(nt,),
        in_specs=[pl.BlockSpec((1, 1, 2 * tm), lambda i: (i, 0, 0), memory_space=pltpu.SMEM),
                  pl.BlockSpec((1, 1, 2 * tm), lambda i: (jnp.minimum(i + 1, nt - 1), 0, 0),
                               memory_space=pltpu.SMEM),
                  pl.BlockSpec((tm, 8), lambda i: (i, 0)),
                  pl.BlockSpec((tm, d), lambda i: (i, 0)),
                  pl.BlockSpec((8, d), lambda i: (0, 0)),
                  pl.BlockSpec(memory_space=pl.ANY)],
        out_specs=pl.BlockSpec((tm, d), lambda i: (i, 0)),
        out_shape=jax.ShapeDtypeStruct((t, d), F32),
        scratch_shapes=[pltpu.VMEM((2, 2, tm, d), F32), pltpu.SemaphoreType.DMA((2,))],
        compiler_params=_cparams(1),
        name="moe_combine_ln",
    )(dest3, dest3, route, h, ln, y_rows)


def _moe(h, route, counts, w1, w3, w2, layer, ln):
    t, d = h.shape
    tm = min(COMBINE_TILE, t)
    a = 2 * t
    expert = route[:, 0:2].astype(jnp.int32)
    rank = route[:, 2:4].astype(jnp.int32)
    cnt = counts[0, :N_EXPERTS].astype(jnp.int32)
    padded = (cnt + MOE_BLOCK - 1) // MOE_BLOCK * MOE_BLOCK
    pend = jnp.cumsum(padded)
    pstart = pend - padded
    sel = expert[..., None] == jnp.arange(N_EXPERTS, dtype=jnp.int32)
    dest = jnp.sum(jnp.where(sel, pstart, 0), axis=-1) + rank
    n_blocks = -(-a // MOE_BLOCK) + N_EXPERTS
    nused = (pend[-1:] // MOE_BLOCK).astype(jnp.int32)
    dest3 = dest.reshape(t // tm, 1, 2 * tm)
    x_rows = _dispatch((pstart + cnt).astype(jnp.int32), (padded - cnt).astype(jnp.int32),
                       pend[-1:].astype(jnp.int32), dest3, h, n_blocks * MOE_BLOCK)
    y_rows = _experts((pstart // MOE_BLOCK).astype(jnp.int32), (padded // MOE_BLOCK).astype(jnp.int32),
                      nused, x_rows, w1, w3, w2, layer)
    return _combine(dest3, route, h, ln, y_rows, tm)


def _pad_rows(x, rows):
    return jnp.zeros((rows,) + x.shape[1:], x.dtype).at[:x.shape[0]].set(x)


def _route_weights(wg, bg, we, be):
    d = wg.shape[0]
    wr = jnp.zeros((d, LANES), F32)
    wr = wr.at[:, :MOE_GROUPS].set(wg)
    wr = wr.at[:, MOE_GROUPS:MOE_GROUPS + N_EXPERTS].set(jnp.transpose(we, (1, 0, 2)).reshape(d, N_EXPERTS))
    br = jnp.zeros((8, LANES), F32)
    br = br.at[0, :MOE_GROUPS].set(bg)
    br = br.at[0, MOE_GROUPS:MOE_GROUPS + N_EXPERTS].set(be.reshape(N_EXPERTS))
    return wr, br


def kernel(x, ab_w_in, rw_mu, rw_w0, rw_w2, rw_a0, rw_a2, rw_g2, rw_k_k, rw_k_a, rw_r_k, rw_gn_g, rw_gn_b, gla_gk_w2, gla_gk_b, gla_norm_g, ab_w_out, cd_w_in, s5_a_re, s5_a_im, s5_log_dt, s5_b_re, s5_b_im, s5_c_re, s5_c_im, s5_d, s5_glu_w, s5_glu_b, hg_lb, hg_norm_g, cd_w_out, ln1_g, ln1_b, moe_wg, moe_bg, moe_we, moe_be, moe_w1, moe_w3, moe_w2, ln2_g, ln2_b):
    batch, seq, d = x.shape
    t = batch * seq
    assert d == D_MODEL and seq % CHUNK == 0
    rows_per_seq = seq // S5_CHUNK
    assert rows_per_seq & (rows_per_seq - 1) == 0, "S5 chunk scan assumes a power-of-two chunk count"
    tm = min(ROW_TILE, t)
    assert t % tm == 0
    xt = x.reshape(t, d)
    ln1 = [_pad_rows(jnp.stack([ln1_g[l], ln1_b[l]]), 8) for l in range(DEPTH)]
    ln2 = [_pad_rows(jnp.stack([ln2_g[l], ln2_b[l]]), 8) for l in range(DEPTH)]

    j = 0
    w = ab_w_in[j]
    r_, wl_, k_, v_, al_, gl_ = 0, 512, 576, 1088, 1600, 1664
    gq, gk, gv, glow, ggate = 1792, 2048, 2304, 2816, 2832
    w0cols = jnp.concatenate([
        w[:, r_:r_ + 512], w[:, k_:k_ + 512], w[:, v_:v_ + 512], w[:, gv:gv + 512], w[:, ggate:ggate + 512],
        w[:, wl_:wl_ + 64], w[:, al_:al_ + 64], w[:, gl_:gl_ + 128], w[:, gq:gq + 256], w[:, gk:gk + 256],
        w[:, glow:glow + 16], jnp.zeros((d, 112), F32)], axis=1).astype(BF16)
    p0 = _matmul(xt, w0cols, tm)

    mu = rw_mu[j]
    vec = _pad_rows(jnp.stack([mu[r_:r_ + 512], mu[k_:k_ + 512], mu[v_:v_ + 512], rw_w0[j], rw_a0[j],
                               rw_k_k[j], rw_k_a[j], rw_r_k[j].reshape(-1), rw_gn_g[j], rw_gn_b[j]]), 16)
    mulo = _pad_rows(jnp.concatenate([mu[wl_:wl_ + 64], mu[al_:al_ + 64], mu[gl_:gl_ + 128]])[None], 8)
    w2p = _pad_rows(rw_w2[j], 128)
    a2p = jnp.zeros((128, 512), F32).at[64:].set(rw_a2[j])
    p0 = p0.reshape(batch, seq, -1)
    y_rw = _rwkv(p0, vec, mulo, w2p, a2p, rw_g2[j]).reshape(t, RW_WIDTH)
    y_gla = _gla(p0, _pad_rows(gla_gk_w2[j], 128), _pad_rows(gla_gk_b[j][None], 8),
                 _pad_rows(gla_norm_g[j][None], 8)).reshape(t, 512)

    wr, br = _route_weights(moe_wg[0], moe_bg[0], moe_we[0], moe_be[0])
    h, route, counts = _proj_route(y_rw, y_gla, xt, ab_w_out[j].astype(BF16), ln1[0], wr, br, tm)
    h = _moe(h, route, counts, moe_w1, moe_w3, moe_w2, 0, ln2[0])

    p1 = _matmul(h, cd_w_in[j].astype(BF16), tm)
    lb_sm = jax.nn.softmax(hg_lb.astype(F32), axis=0)
    lower = (jnp.cumsum(lb_sm, axis=0) - lb_sm[0])[1]
    y_hg = _hgrn(p1.reshape(batch, seq, -1), _pad_rows(lower[None], 8),
                 _pad_rows(hg_norm_g[j][None], 8)).reshape(t, 512)

    toep, bmat, emat, pw = _s5_tables(s5_a_re[j], s5_a_im[j], s5_log_dt[j], s5_b_re[j], s5_b_im[j],
                                      s5_c_re[j], s5_c_im[j], rows_per_seq)
    y_ssm = _s5_scan(p1.reshape(batch, seq, -1), toep, bmat, emat, pw).reshape(t, 512)
    y_s5 = _s5_post(y_ssm, p1, _pad_rows(jnp.stack([s5_d[j], s5_glu_b[j]]), 8),
                    s5_glu_w[j].astype(BF16), tm)

    wr, br = _route_weights(moe_wg[1], moe_bg[1], moe_we[1], moe_be[1])
    h2, route, counts = _proj_route(y_s5, y_hg, h, cd_w_out[j].astype(BF16), ln1[1], wr, br, tm)
    out = _moe(h2, route, counts, moe_w1, moe_w3, moe_w2, 1, ln2[1])
    return out.reshape(batch, seq, d)
```

```python
import functools
import math

import numpy as np
import jax
import jax.numpy as jnp
from jax import lax
from jax.experimental import pallas as pl
from jax.experimental.pallas import tpu as pltpu

F32 = jnp.float32
BF16 = jnp.bfloat16

D_MODEL = 1024
DEPTH = 2
RW_HEAD = 64
RW_WIDTH = 512
RW_GN_EPS = 64e-5
GLA_HEADS = 4
GLA_DK = 64
GLA_DV = 128
GLA_GATE_TAU = 16.0
S5_GROUP = 16
S5_GROUPS = 32
S5_STATE = 64
HG_HEADS = 4
HG_DK = 128
CHUNK = 64
S5_CHUNK = 16
S5_SCAN_ROWS = 32
S5_TILE_GROUPS = 8
NORM_EPS = 1e-5
MOE_GROUPS = 4
MOE_PER_GROUP = 8
N_EXPERTS = 32
EXPERT_HIDDEN = 512
MOE_BLOCK = 128
COMBINE_TILE = 256
EXPERT_CHUNK = 512
ROW_TILE = 512
DN_ALPHA = (2.0 * DEPTH) ** 0.25

LANES = 128
VMEM_LIMIT = 56 * 1024 * 1024


def _cparams(n_axes=1):
    return pltpu.CompilerParams(dimension_semantics=("arbitrary",) * n_axes,
                                vmem_limit_bytes=VMEM_LIMIT)


def _d(a, b):
    return jnp.dot(a, b, preferred_element_type=F32)


def _d_nt(a, b):
    return lax.dot_general(a, b, (((1,), (1,)), ((), ())), preferred_element_type=F32)


def _d_tn(a, b):
    return lax.dot_general(a, b, (((0,), (0,)), ((), ())), preferred_element_type=F32)


def _split(a):
    hi = a.astype(BF16)
    lo = (a - hi.astype(F32)).astype(BF16)
    return hi, lo


def _split3(a):
    hi = a.astype(BF16)
    r1 = a - hi.astype(F32)
    mid = r1.astype(BF16)
    lo = (r1 - mid.astype(F32)).astype(BF16)
    return hi, mid, lo


def _bdot(a, b):
    return _d(a.astype(BF16), b.astype(BF16))


def _bdot_nt(a, b):
    return _d_nt(a.astype(BF16), b.astype(BF16))


def _bdot_tn(a, b):
    return _d_tn(a.astype(BF16), b.astype(BF16))


def _hdot_with(d, a, b):
    ah, al = _split(a)
    bh, bl = _split(b)
    return d(ah, bh) + (d(ah, bl) + d(al, bh))


def _hdot(a, b):
    return _hdot_with(_d, a, b)


def _hdot_nt(a, b):
    return _hdot_with(_d_nt, a, b)


def _hdot_tn(a, b):
    return _hdot_with(_d_tn, a, b)


def _xdot_r(e, a):
    ah, am, al = _split3(a)
    return _d(e, ah) + (_d(e, am) + _d(e, al))


def _iota(shape, dim):
    return lax.broadcasted_iota(jnp.int32, shape, dim)


def _softplus(x):
    return jnp.maximum(x, 0.0) + jnp.log1p(jnp.exp(-jnp.abs(x)))


def _sigmoid(x):
    return 1.0 / (1.0 + jnp.exp(-x))


def _silu(x):
    return x * _sigmoid(x)


def _segsum(a, e):
    m = a.shape[0]
    s = _d(jnp.concatenate(_split3(a), axis=0), e)
    return s[:m] + (s[m:2 * m] + s[2 * m:])


def _cumsum_rows(g, chunk=None):
    n = g.shape[0]
    chunk = n if chunk is None else chunk
    shift = int(math.log2(chunk))
    r, c = _iota((n, n), 0), _iota((n, n), 1)
    tril = jnp.where((r >= c) & ((r >> shift) == (c >> shift)), 1.0, 0.0).astype(BF16)
    return _xdot_r(tril, g)


def _shift_mix(x, prev_ref, b, mu):
    c = x.shape[0]
    rolled = pltpu.roll(x, 1, 0)
    prev = jnp.where(_iota(x.shape, 0) == 0, jnp.broadcast_to(prev_ref[b, 0:1, :], x.shape), rolled)
    prev_ref[b, 0:1, :] = x[c - 1:c, :]
    return x + mu * (prev - x)


def _layer_norm(x, g, b):
    mu = jnp.mean(x, axis=-1, keepdims=True)
    xc = x - mu
    var = jnp.mean(xc * xc, axis=-1, keepdims=True)
    return xc * lax.rsqrt(var + NORM_EPS) * g + b


def _mm_body(x_ref, w_ref, o_ref):
    o_ref[...] = _d(x_ref[...].astype(BF16), w_ref[...])


def _matmul(x, w_bf16, tm):
    m, k = x.shape
    n = w_bf16.shape[1]
    return pl.pallas_call(
        _mm_body,
        grid=(m // tm,),
        in_specs=[pl.BlockSpec((tm, k), lambda i: (i, 0)),
                  pl.BlockSpec((k, n), lambda i: (0, 0))],
        out_specs=pl.BlockSpec((tm, n), lambda i: (i, 0)),
        out_shape=jax.ShapeDtypeStruct((m, n), F32),
        compiler_params=_cparams(1),
        name="in_proj",
    )(x, w_bf16)


_RV_MU_R, _RV_MU_K, _RV_MU_V, _RV_W0, _RV_A0, _RV_KK, _RV_KA, _RV_RK, _RV_GNG, _RV_GNB = range(10)


def _rwkv_body(r_ref, k_ref, v_ref, lo_ref, vec_ref, mulo_ref, w2_ref, a2_ref, g2_ref, ones_ref,
               o_ref, pr_ref, pk_ref, pv_ref, plo_ref, st_ref):
    nb, c = r_ref.shape[0], r_ref.shape[1]
    npair = RW_WIDTH // LANES

    @pl.when(pl.program_id(0) == 0)
    def _():
        pr_ref[...] = jnp.zeros_like(pr_ref)
        pk_ref[...] = jnp.zeros_like(pk_ref)
        pv_ref[...] = jnp.zeros_like(pv_ref)
        plo_ref[...] = jnp.zeros_like(plo_ref)
        st_ref[...] = jnp.zeros_like(st_ref)

    def vec(i):
        return vec_ref[i:i + 1, :]

    ones_bd = ones_ref[...]
    lane = _iota((c, LANES), 1)
    m1 = lane < RW_HEAD
    row2 = _iota((2 * c, 4 * c), 0)
    col2 = _iota((2 * c, 4 * c), 1) & (c - 1)
    tri = ((row2 < c) & (row2 > col2)) | ((row2 >= c) & ((row2 - c) >= col2))
    eye2 = jnp.where(_iota((2 * c, 2 * c), 0) == _iota((2 * c, 2 * c), 1), 1.0, 0.0)
    bd_p = (_iota((LANES, LANES), 0) >> 6) == (_iota((LANES, LANES), 1) >> 6)

    def halves(x):
        return jnp.concatenate([jnp.where(m1, x, 0.0), jnp.where(m1, 0.0, x)], axis=0)

    def stacked(ref, prev_ref, mu):
        return jnp.concatenate([_shift_mix(ref[b], prev_ref, b, mu) for b in range(nb)], axis=0)

    xr = stacked(r_ref, pr_ref, vec(_RV_MU_R))
    xk = stacked(k_ref, pk_ref, vec(_RV_MU_K))
    xv = stacked(v_ref, pv_ref, vec(_RV_MU_V))
    xlo = stacked(lo_ref, plo_ref, mulo_ref[0:1, :])
    lo_a = xlo[:, :LANES]
    lo_g = xlo[:, LANES:]
    w = -_softplus(-(vec(_RV_W0) + _bdot(jnp.tanh(lo_a), w2_ref[...]))) - 0.5
    g = -jnp.exp(w)
    a = _sigmoid(vec(_RV_A0) + _bdot(lo_a, a2_ref[...]))
    gate = _bdot(_sigmoid(lo_g), g2_ref[...])
    kk = xk * vec(_RV_KK)
    kk = kk / jnp.maximum(jnp.sqrt(_segsum(kk * kk, ones_bd)), 1e-12)
    k2 = xk * (1.0 + (a - 1.0) * vec(_RV_KA))
    gc = _cumsum_rows(g, c)
    g_last = jnp.concatenate([jnp.broadcast_to(gc[(b + 1) * c - 1:(b + 1) * c, :], (c, gc.shape[1]))
                              for b in range(nb)], axis=0)
    e_neg = jnp.exp(-gc)
    e_end = jnp.exp(g_last - gc)
    prep = dict(
        xv=xv,
        at=-kk * jnp.exp(gc - g),
        bt=(kk * a) * e_neg, kt=k2 * e_neg, rt=xr * jnp.exp(gc),
        bh=(kk * a) * e_end, kh=k2 * e_end,
        gam=jnp.exp(g_last))

    chains = [(b, p) for b in range(nb) for p in range(npair)]

    def part(ch, name):
        b, p = ch
        return prep[name][b * c:(b + 1) * c, p * LANES:(p + 1) * LANES]

    a_ak, a_row, vv, ak, pinv = {}, {}, {}, {}, {}
    for ch in chains:
        lhs = jnp.concatenate([part(ch, 'at'), part(ch, 'rt')], axis=0)
        rhs = jnp.concatenate([halves(part(ch, 'bt')), halves(part(ch, 'kt'))], axis=0)
        aa = jnp.where(tri, _bdot_nt(lhs, rhs), 0.0)
        a_ak[ch] = aa[:c, 2 * c:].astype(BF16)
        a_row[ch] = aa[c:, :].astype(BF16)
        abd = halves(aa[:c, :2 * c])
        pinv[ch] = eye2 + abd
        ak[ch] = abd
        vv[ch] = halves(part(ch, 'xv')).astype(BF16)
    nlev = int(math.log2(c))
    for lev in range(nlev):
        for ch in chains:
            akb = ak[ch].astype(BF16)
            if lev == 0:
                ak[ch] = _d(akb, akb)
            elif lev < nlev - 1:
                out = _d(akb, jnp.concatenate([akb, pinv[ch].astype(BF16)], axis=1))
                ak[ch] = out[:, :2 * c]
                pinv[ch] = pinv[ch] + out[:, 2 * c:]
            else:
                pinv[ch] = pinv[ch] + _d(akb, pinv[ch].astype(BF16))
    x2 = {ch: _d(a_ak[ch], vv[ch]) for ch in chains}

    sts = {ch: st_ref[ch[0] * npair + ch[1]] for ch in chains}
    xs = {ch: _d_nt(jnp.concatenate([part(ch, 'at'), part(ch, 'rt')], axis=0).astype(BF16),
                    sts[ch].astype(BF16)) for ch in chains}
    us = {}
    for ch in chains:
        u2 = _d(pinv[ch].astype(BF16), halves(xs[ch][:c] + x2[ch]).astype(BF16))
        us[ch] = u2[:c] + u2[c:]
    ys = {}
    for ch in chains:
        b, p = ch
        u = us[ch]
        ys[ch] = xs[ch][c:] + _d(a_row[ch], jnp.concatenate([halves(u).astype(BF16), vv[ch]], axis=0))
        upd = _d_tn(jnp.concatenate([u, part(ch, 'xv')], axis=0).astype(BF16),
                    jnp.concatenate([part(ch, 'bh'), part(ch, 'kh')], axis=0).astype(BF16))
        st_ref[b * npair + p] = sts[ch] * part(ch, 'gam')[0:1, :] + jnp.where(bd_p, upd, 0.0)

    inv_n = 1.0 / RW_HEAD
    y = jnp.concatenate([jnp.concatenate([ys[(b, p)] for p in range(npair)], axis=1)
                         for b in range(nb)], axis=0)
    mean = _segsum(y, ones_bd) * inv_n
    yc = y - mean
    var = _segsum(yc * yc, ones_bd) * inv_n
    yn = yc * lax.rsqrt(var + RW_GN_EPS) * vec(_RV_GNG) + vec(_RV_GNB)
    bonus = _segsum(xr * k2 * vec(_RV_RK), ones_bd) * xv
    out = (yn + bonus) * gate
    for b in range(nb):
        o_ref[b] = out[b * c:(b + 1) * c, :]


def _rwkv(p0, vec, mulo, w2p, a2p, g2):
    batch, seq, _ = p0.shape
    nc = seq // CHUNK
    c = CHUNK
    ones_bd = jnp.asarray(np.kron(np.eye(RW_WIDTH // RW_HEAD), np.ones((RW_HEAD, RW_HEAD))), BF16)

    def col(j, width):
        return pl.BlockSpec((batch, c, width), lambda i: (0, i, j))

    def full(shape):
        return pl.BlockSpec(shape, lambda i: (0,) * len(shape))

    return pl.pallas_call(
        _rwkv_body,
        grid=(nc,),
        in_specs=[col(0, 512), col(1, 512), col(2, 512), col(10, 256),
                  full((16, 512)), full((8, 256)), full((128, 512)), full((128, 512)),
                  full((128, 512)), full((512, 512))],
        out_specs=pl.BlockSpec((batch, c, 512), lambda i: (0, i, 0)),
        out_shape=jax.ShapeDtypeStruct((batch, seq, RW_WIDTH), F32),
        scratch_shapes=[pltpu.VMEM((batch, 8, 512), F32), pltpu.VMEM((batch, 8, 512), F32),
                        pltpu.VMEM((batch, 8, 512), F32), pltpu.VMEM((batch, 8, 256), F32),
                        pltpu.VMEM((batch * RW_WIDTH // LANES, LANES, LANES), F32)],
        compiler_params=_cparams(1),
        name="rwkv7",
    )(p0, p0, p0, p0, vec, mulo, w2p, a2p, g2, ones_bd)


def _gla_core(qs, ks, vs, gs, st_ref, heads_per_block):
    nb = len(qs)
    c = qs[0].shape[0]
    hpb = heads_per_block
    nblk = qs[0].shape[1] // LANES
    dk_shift = int(math.log2(LANES // hpb))
    row = _iota(qs[0].shape, 0)

    prep = []
    for q, k, g in zip(qs, ks, gs):
        b = _cumsum_rows(g)

        def brow(i, b=b):
            return jnp.broadcast_to(b[i:i + 1, :], b.shape)

        b15, b31, b47, blast = brow(15), brow(31), brow(47), brow(c - 1)
        ref_b = jnp.where(row < 32, b15, b47)
        ref_d = jnp.where(row < 16, 0.0, jnp.where(row < 32, b15, jnp.where(row < 48, b31, b47)))
        prep.append(dict(
            q_a=q * jnp.exp(jnp.minimum(b - b31, 0.0)), k_a=k * jnp.exp(jnp.minimum(b31 - b, 0.0)),
            q_b=q * jnp.exp(jnp.minimum(b - ref_b, 0.0)), k_b=k * jnp.exp(jnp.minimum(ref_b - b, 0.0)),
            q_d=q * jnp.exp(b - ref_d), k_d=k * jnp.exp(ref_d - b),
            q_i=q * jnp.exp(b), k_s=k * jnp.exp(blast - b), gam=jnp.exp(b[c - 1:c, :])))

    ri = _iota((hpb * c, c), 0) & (c - 1)
    ci = _iota((hpb * c, c), 1)
    mask_a = (ri >= 32) & (ci < 32)
    mask_b = ((ri >> 5) == (ci >> 5)) & (((ri >> 4) & 1) == 1) & (((ci >> 4) & 1) == 0)
    mask_d = ((ri >> 4) == (ci >> 4)) & (ri >= ci)
    lane = _iota((c, LANES), 1)
    bd = (_iota((hpb * LANES, LANES), 0) >> 7) == (_iota((hpb * LANES, LANES), 1) >> dk_shift)

    def heads_rows(x):
        if hpb == 1:
            return x
        return jnp.concatenate([jnp.where((lane >> dk_shift) == h, x, 0.0) for h in range(hpb)], axis=0)

    chains = [(bi, blk) for bi in range(nb) for blk in range(nblk)]

    def part(ch, name):
        bi, blk = ch
        return prep[bi][name][:, blk * LANES:(blk + 1) * LANES]

    v_p = {ch: vs[ch[0]][:, ch[1] * hpb * LANES:(ch[1] + 1) * hpb * LANES].astype(BF16) for ch in chains}
    probs = {}
    for ch in chains:
        s_a = _bdot_nt(heads_rows(part(ch, 'q_a')), part(ch, 'k_a'))
        s_b = _bdot_nt(heads_rows(part(ch, 'q_b')), part(ch, 'k_b'))
        s_d = _bdot_nt(heads_rows(part(ch, 'q_d')), part(ch, 'k_d'))
        probs[ch] = (jnp.where(mask_a, s_a, 0.0) + jnp.where(mask_b, s_b, 0.0)
                     + jnp.where(mask_d, s_d, 0.0)).astype(BF16)
    outs = {}
    for ch in chains:
        pv = _d(probs[ch], v_p[ch])
        o = pv[:c]
        for h in range(1, hpb):
            o = jnp.where((_iota(o.shape, 1) >> 7) == h, pv[h * c:(h + 1) * c], o)
        si = ch[0] * nblk + ch[1]
        sp = st_ref[si]
        outs[ch] = o + _bdot_nt(part(ch, 'q_i'), sp)
        upd = _d_tn(v_p[ch], part(ch, 'k_s').astype(BF16))
        st_ref[si] = sp * part(ch, 'gam') + jnp.where(bd, upd, 0.0)
    return [jnp.concatenate([outs[(bi, blk)] for blk in range(nblk)], axis=1) for bi in range(nb)]


def _gated_rmsnorm(o, gate, norm_g):
    nh = o.shape[1] // LANES
    outs = []
    for h in range(nh):
        sl = slice(h * LANES, (h + 1) * LANES)
        oh = o[:, sl]
        ms = jnp.mean(oh * oh, axis=-1, keepdims=True)
        outs.append(oh * lax.rsqrt(ms + NORM_EPS) * norm_g * _silu(gate[:, sl]))
    return jnp.concatenate(outs, axis=1)


def _gla_body(q_ref, k_ref, v_ref, gate_ref, gk_ref, w2_ref, vec_ref, ng_ref, o_ref, st_ref):
    nb = q_ref.shape[0]

    @pl.when(pl.program_id(0) == 0)
    def _():
        st_ref[...] = jnp.zeros_like(st_ref)

    gs = [-_softplus(-(_hdot(gk_ref[b], w2_ref[...]) + vec_ref[0:1, :])) * (1.0 / GLA_GATE_TAU)
          for b in range(nb)]
    qs = [q_ref[b] * (GLA_DK ** -0.5) for b in range(nb)]
    os_ = _gla_core(qs, [k_ref[b] for b in range(nb)], [v_ref[b] for b in range(nb)], gs, st_ref, 2)
    for b in range(nb):
        o_ref[b] = _gated_rmsnorm(os_[b], gate_ref[b], ng_ref[0:1, :])


def _seq_specs(batch, c):
    def col(j, width):
        return pl.BlockSpec((batch, c, width), lambda i: (0, i, j))

    def full(shape):
        return pl.BlockSpec(shape, lambda i: (0,) * len(shape))

    return col, full


def _gla(p0, gk_w2p, gk_b, norm_g):
    batch, seq, _ = p0.shape
    c = CHUNK
    col, full = _seq_specs(batch, c)
    return pl.pallas_call(
        _gla_body,
        grid=(seq // c,),
        in_specs=[col(11, 256), col(12, 256), col(3, 512), col(4, 512), col(26, 128),
                  full((128, 256)), full((8, 256)), full((8, 128))],
        out_specs=pl.BlockSpec((batch, c, 512), lambda i: (0, i, 0)),
        out_shape=jax.ShapeDtypeStruct((batch, seq, 512), F32),
        scratch_shapes=[pltpu.VMEM((batch * 2, 2 * LANES, LANES), F32)],
        compiler_params=_cparams(1),
        name="gla",
    )(p0, p0, p0, p0, p0, gk_w2p, gk_b, norm_g)


def _hgrn_body(q_ref, f_ref, i_ref, gate_ref, lb_ref, ng_ref, o_ref, st_ref):
    nb = q_ref.shape[0]

    @pl.when(pl.program_id(0) == 0)
    def _():
        st_ref[...] = jnp.zeros_like(st_ref)

    lb = lb_ref[0:1, :]
    qs, ks, gs = [], [], []
    for b in range(nb):
        f = f_ref[b]
        gs.append(jnp.log(lb + (1.0 - lb) * _sigmoid(f)))
        ks.append((1.0 - lb) * _sigmoid(-f))
        qs.append(_silu(q_ref[b]))
    os_ = _gla_core(qs, ks, [i_ref[b] for b in range(nb)], gs, st_ref, 1)
    for b in range(nb):
        o_ref[b] = _gated_rmsnorm(os_[b], gate_ref[b], ng_ref[0:1, :])


def _hgrn(p1, lb, norm_g):
    batch, seq, _ = p1.shape
    c = CHUNK
    col, full = _seq_specs(batch, c)
    return pl.pallas_call(
        _hgrn_body,
        grid=(seq // c,),
        in_specs=[col(1, 512), col(2, 512), col(3, 512), col(4, 512), full((8, 512)), full((8, 128))],
        out_specs=pl.BlockSpec((batch, c, 512), lambda i: (0, i, 0)),
        out_shape=jax.ShapeDtypeStruct((batch, seq, 512), F32),
        scratch_shapes=[pltpu.VMEM((batch * HG_HEADS, LANES, LANES), F32)],
        compiler_params=_cparams(1),
        name="hgrn2",
    )(p1, p1, p1, p1, lb, norm_g)


def _s5_body(u_ref, dcat_ref, bm_ref, eblk_ref, pw_ref, o_ref, toep_ref, e_ref):
    seq = u_ref.shape[0]
    rows = seq // S5_CHUNK

    @pl.when(pl.program_id(1) == 0)
    def _():
        for tk in range(S5_CHUNK):
            lo = (S5_CHUNK - 1 - tk) * LANES
            toep_ref[tk * LANES:(tk + 1) * LANES, :] = dcat_ref[0, :, lo:lo + S5_CHUNK * LANES]
            e_ref[:, tk * LANES:(tk + 1) * LANES] = eblk_ref[0, tk]

    u = jnp.concatenate([u_ref[pl.ds(tk, rows, stride=S5_CHUNK), :] for tk in range(S5_CHUNK)],
                        axis=1).astype(BF16)
    s = _d(u, bm_ref[0])
    rin = _iota(s.shape, 0)
    pw = pw_ref[0]

    def swap_halves(x):
        return jnp.concatenate([pltpu.roll(x[:, k * LANES:(k + 1) * LANES], S5_STATE, 1)
                                for k in range(x.shape[1] // LANES)], axis=1)

    h = s
    for lev in range(int(math.log2(rows))):
        sh = 1 << lev
        hs = jnp.where(rin >= sh, pltpu.roll(h, sh, 0), 0.0)
        h = h + pw[2 * lev:2 * lev + 1, :] * hs + pw[2 * lev + 1:2 * lev + 2, :] * swap_halves(hs)
    hprev = jnp.where(rin >= 1, pltpu.roll(h, 1, 0), 0.0)
    y = _d(u, toep_ref[...]) + _bdot(hprev, e_ref[...])
    for tk in range(S5_CHUNK):
        o_ref[pl.ds(tk, rows, stride=S5_CHUNK), :] = y[:, tk * LANES:(tk + 1) * LANES]


def _s5_scan(p1, dcat, bmat, emat, pw):
    batch, seq, _ = p1.shape
    nt, width, nstate = bmat.shape

    def table(shape):
        return pl.BlockSpec((1,) + shape, lambda j, b: (j, 0, 0), pipeline_mode=pl.Buffered(1))

    return pl.pallas_call(
        _s5_body,
        grid=(nt, batch),
        in_specs=[pl.BlockSpec((None, seq, LANES), lambda j, b: (b, 0, j)),
                  table(dcat.shape[1:]), table((width, nstate)),
                  pl.BlockSpec((1,) + emat.shape[1:], lambda j, b: (j, 0, 0, 0), pipeline_mode=pl.Buffered(1)),
                  table((S5_SCAN_ROWS, nstate))],
        out_specs=pl.BlockSpec((None, seq, LANES), lambda j, b: (b, 0, j)),
        out_shape=jax.ShapeDtypeStruct((batch, seq, nt * LANES), F32),
        scratch_shapes=[pltpu.VMEM((width, width), BF16), pltpu.VMEM((nstate, width), BF16)],
        compiler_params=_cparams(2),
        name="s5_scan",
    )(p1, dcat, bmat, emat, pw)


def _s5_post_body(y_ref, u_ref, vec_ref, w_ref, o_ref):
    y = y_ref[...] + vec_ref[0:1, :] * u_ref[...]
    y = 0.5 * y * (1.0 + jnp.tanh(math.sqrt(2.0 / math.pi) * (y + 0.044715 * (y * y * y))))
    o_ref[...] = y * _sigmoid(_bdot(y, w_ref[...]) + vec_ref[1:2, :])


def _s5_post(y_ssm, p1, vec, glu_w, tm):
    t = y_ssm.shape[0]
    return pl.pallas_call(
        _s5_post_body,
        grid=(t // tm,),
        in_specs=[pl.BlockSpec((tm, 512), lambda i: (i, 0)),
                  pl.BlockSpec((tm, 512), lambda i: (i, 0)),
                  pl.BlockSpec((8, 512), lambda i: (0, 0)),
                  pl.BlockSpec((512, 512), lambda i: (0, 0))],
        out_specs=pl.BlockSpec((tm, 512), lambda i: (i, 0)),
        out_shape=jax.ShapeDtypeStruct((t, 512), F32),
        compiler_params=_cparams(1),
        name="s5_post",
    )(y_ssm, p1, vec, glu_w)


def _s5_tables(a_re, a_im, log_dt, b_re, b_im, c_re, c_im, rows_per_seq):
    c = S5_CHUNK
    lam_re = jnp.minimum(a_re, -1e-4)
    lam_im = a_im
    dt = jnp.exp(log_dt)[:, None]
    mag = jnp.exp(lam_re * dt)
    abar_re = mag * jnp.cos(lam_im * dt)
    abar_im = mag * jnp.sin(lam_im * dt)
    den = lam_re * lam_re + lam_im * lam_im
    num_re = abar_re - 1.0
    z_re = (num_re * lam_re + abar_im * lam_im) / den
    z_im = (abar_im * lam_re - num_re * lam_im) / den

    def power(n):
        n = jnp.asarray(n, F32)[..., None, None]
        m = jnp.exp(n * (lam_re * dt))
        return m * jnp.cos(n * (lam_im * dt)), m * jnp.sin(n * (lam_im * dt))

    def cmul(ar, ai, br, bi):
        return ar * br - ai * bi, ar * bi + ai * br

    tau = jnp.arange(c)
    p_re, p_im = power(tau)
    zb_re, zb_im = cmul(z_re[..., None], z_im[..., None], b_re, b_im)
    cp_re, cp_im = cmul(c_re[None], c_im[None], p_re[:, :, None, :], p_im[:, :, None, :])
    hi = lax.Precision.HIGHEST
    kern = (jnp.einsum('tgon,gni->gtoi', cp_re, zb_re, precision=hi)
            - jnp.einsum('tgon,gni->gtoi', cp_im, zb_im, precision=hi))
    q_re, q_im = power(c - 1 - tau)
    bm_re, bm_im = cmul(q_re[..., None], q_im[..., None], zb_re[None], zb_im[None])
    bmat = jnp.concatenate([jnp.transpose(bm_re, (1, 0, 3, 2)), jnp.transpose(bm_im, (1, 0, 3, 2))],
                           axis=-1).reshape(S5_GROUPS, c * S5_GROUP, 2 * S5_STATE)
    r_re, r_im = power(tau + 1)
    nlev = int(math.log2(rows_per_seq))
    assert 2 * nlev <= S5_SCAN_ROWS
    s_re, s_im = power(c * (2 ** jnp.arange(nlev)))
    pw = jnp.zeros((S5_GROUPS, S5_SCAN_ROWS, 2 * S5_STATE), F32)
    pw = pw.at[:, 0:2 * nlev:2, :].set(jnp.transpose(jnp.concatenate([s_re, s_re], -1), (1, 0, 2)))
    pw = pw.at[:, 1:2 * nlev:2, :].set(jnp.transpose(jnp.concatenate([-s_im, s_im], -1), (1, 0, 2)))
    nt, gl = S5_GROUPS // S5_TILE_GROUPS, S5_TILE_GROUPS
    spread = jnp.tile(jnp.eye(S5_GROUP, dtype=F32), (1, gl))
    lane_group = jnp.arange(LANES) // S5_GROUP
    kern5 = jnp.transpose(kern.reshape(nt, gl, c, S5_GROUP, S5_GROUP), (0, 2, 1, 4, 3))
    dblk = jnp.einsum('jtgio,op->jtgip', kern5, spread, precision=hi)
    dblk = jnp.where((jnp.arange(gl)[:, None, None] == lane_group[None, None, :]), dblk, 0.0)
    dblk = dblk.reshape(nt, c, LANES, LANES)
    dcat = jnp.transpose(dblk, (0, 2, 1, 3)).reshape(nt, LANES, c * LANES)
    dcat = jnp.pad(dcat, ((0, 0), (0, 0), ((c - 1) * LANES, 0)))
    bmat5 = jnp.transpose(bmat.reshape(nt, gl, c, S5_GROUP, 2 * S5_STATE), (0, 2, 1, 3, 4))
    row_group = (jnp.arange(c * LANES) // S5_GROUP) % gl
    col_tile = jnp.arange(gl * 2 * S5_STATE) // (2 * S5_STATE)
    bmat_t = jnp.where(row_group[:, None] == col_tile[None, :],
                       jnp.tile(bmat5.reshape(nt, c * LANES, 2 * S5_STATE), (1, 1, gl)), 0.0)
    own_group = (jnp.arange(S5_GROUPS) % gl)[:, None, None] == lane_group[None, None, :]

    def spread_c(cmat):
        cs = jnp.where(own_group, jnp.einsum('gon,op->gnp', cmat, spread, precision=hi), 0.0)
        return cs.reshape(nt, 1, gl, S5_STATE, LANES)

    def per_state(pw_):
        return jnp.transpose(pw_.reshape(c, nt, gl, S5_STATE), (1, 0, 2, 3))[..., None]

    cs_re, cs_im, rr, ri = spread_c(c_re), spread_c(c_im), per_state(r_re), per_state(r_im)
    eblk = jnp.concatenate([cs_re * rr - cs_im * ri, -(cs_re * ri + cs_im * rr)], axis=3)
    eblk = eblk.reshape(nt, c, gl * 2 * S5_STATE, LANES)
    pw_t = jnp.transpose(pw.reshape(nt, gl, S5_SCAN_ROWS, 2 * S5_STATE), (0, 2, 1, 3))
    pw_t = pw_t.reshape(nt, S5_SCAN_ROWS, gl * 2 * S5_STATE)
    return dcat.astype(BF16), bmat_t.astype(BF16), eblk.astype(BF16), pw_t


def _proj_route_body(ya_ref, yb_ref, x_ref, w_ref, ln_ref, wr_ref, br_ref,
                     h_ref, route_ref, cnt_ref, carry_ref):
    tm = ya_ref.shape[0]
    half = w_ref.shape[0] // 2

    @pl.when(pl.program_id(0) == 0)
    def _():
        carry_ref[...] = jnp.zeros_like(carry_ref)

    mix = _bdot(ya_ref[...], w_ref[:half, :]) + _bdot(yb_ref[...], w_ref[half:, :])
    h = _layer_norm(DN_ALPHA * x_ref[...] + mix, ln_ref[0:1, :], ln_ref[1:2, :])
    h_ref[...] = h

    logits = _hdot(h, wr_ref[...]) + br_ref[0:1, :]
    lane = _iota((tm, LANES), 1).astype(F32)
    neg = -jnp.inf

    def softmax_masked(mask):
        xm = jnp.where(mask, logits, neg)
        m = jnp.max(xm, axis=-1, keepdims=True)
        e = jnp.exp(xm - m)
        return e / jnp.sum(e, axis=-1, keepdims=True)

    def top1(pm):
        m = jnp.max(pm, axis=-1, keepdims=True)
        idx = jnp.min(jnp.where(pm == m, lane, float(LANES)), axis=-1, keepdims=True)
        return m, idx

    coarse = jnp.where(lane < MOE_GROUPS, softmax_masked(lane < MOE_GROUPS), -1.0)
    p_grp, grp = top1(coarse)
    lo = MOE_GROUPS + MOE_PER_GROUP * grp
    fmask = (lane >= lo) & (lane < lo + MOE_PER_GROUP)
    fine = jnp.where(fmask, softmax_masked(fmask), -1.0)
    p1, j1 = top1(fine)
    p2, j2 = top1(jnp.where(lane == j1, -1.0, fine))
    denom = p1 + p2
    g1 = p_grp * (p1 / denom)
    g2 = p_grp * (p2 / denom)
    e1 = j1 - MOE_GROUPS
    e2 = j2 - MOE_GROUPS

    oh1 = jnp.where(lane == e1, 1.0, 0.0)
    oh2 = jnp.where(lane == e2, 1.0, 0.0)
    cnt = oh1 + oh2
    strict = jnp.where(_iota((tm, tm), 0) > _iota((tm, tm), 1), 1.0, 0.0).astype(BF16)
    before = _d(strict, cnt.astype(BF16)) + carry_ref[0:1, :]
    r1 = jnp.sum(oh1 * before, axis=-1, keepdims=True)
    r2 = jnp.sum(oh2 * before, axis=-1, keepdims=True)
    carry_ref[0:1, :] = carry_ref[0:1, :] + jnp.sum(cnt, axis=0, keepdims=True)
    cnt_ref[...] = carry_ref[...]

    out = jnp.where(lane == 0, e1, 0.0)
    out = jnp.where(lane == 1, e2, out)
    out = jnp.where(lane == 2, r1, out)
    out = jnp.where(lane == 3, r2, out)
    out = jnp.where(lane == 4, g1, out)
    out = jnp.where(lane == 5, g2, out)
    route_ref[...] = out[:, :8]


def _proj_route(ya, yb, resid, w_out, ln, wr, br, tm):
    t, d = resid.shape
    return pl.pallas_call(
        _proj_route_body,
        grid=(t // tm,),
        in_specs=[pl.BlockSpec((tm, 512), lambda i: (i, 0)),
                  pl.BlockSpec((tm, 512), lambda i: (i, 0)),
                  pl.BlockSpec((tm, d), lambda i: (i, 0)),
                  pl.BlockSpec((d, d), lambda i: (0, 0)),
                  pl.BlockSpec((8, d), lambda i: (0, 0)),
                  pl.BlockSpec((d, LANES), lambda i: (0, 0)),
                  pl.BlockSpec((8, LANES), lambda i: (0, 0))],
        out_specs=[pl.BlockSpec((tm, d), lambda i: (i, 0)),
                   pl.BlockSpec((tm, 8), lambda i: (i, 0)),
                   pl.BlockSpec((8, LANES), lambda i: (0, 0))],
        out_shape=[jax.ShapeDtypeStruct((t, d), F32),
                   jax.ShapeDtypeStruct((t, 8), F32),
                   jax.ShapeDtypeStruct((8, LANES), F32)],
        scratch_shapes=[pltpu.VMEM((8, LANES), F32)],
        compiler_params=_cparams(1),
        name="proj_ln_route",
    )(ya, yb, resid, w_out, ln, wr, br)


def _row_copy(src_ref, src_row, dst_ref, dst_row, sem):
    return pltpu.make_async_copy(src_ref.at[pl.ds(src_row, 1)], dst_ref.at[pl.ds(dst_row, 1)], sem)


def _start_all(copies):
    for n, cp in enumerate(copies):
        cp.start(priority=n % 2)


def _dispatch_body(pad_start_ref, pad_len_ref, used_rows_ref, dest_ref, h_ref, rows_ref, hbuf_ref, zbuf_ref,
                   lsem, ssem, zsem):
    i = pl.program_id(0)
    last = pl.num_programs(0) - 1
    tm = hbuf_ref.shape[1]
    slot = i % 2

    del lsem

    def scatters(s):
        return [_row_copy(h_ref, i * tm + r, rows_ref, dest_ref[0, 0, 2 * r + q], ssem.at[s])
                for r in range(tm) for q in range(2)]

    @pl.when(i >= 1)
    def _():
        for cp in scatters(1 - slot):
            cp.wait()

    _start_all(scatters(slot))

    @pl.when(i == last)
    def _():
        for cp in scatters(slot):
            cp.wait()
        zbuf_ref[...] = jnp.zeros_like(zbuf_ref)
        sub = 8
        half = zbuf_ref.shape[0]

        def pad_fills():
            out = []
            for e in range(N_EXPERTS):
                start = pad_start_ref[e]
                npad = pad_len_ref[e]
                end = start + npad
                run = half
                while run >= sub:
                    end = end - (npad & run)
                    dst = rows_ref.at[pl.ds(pl.multiple_of(end, run), run)]
                    out.append(((npad & run) != 0,
                                pltpu.make_async_copy(zbuf_ref.at[pl.ds(0, run)], dst, zsem)))
                    run //= 2
                for k in range(sub - 1):
                    out.append((k < (npad & (sub - 1)), _row_copy(zbuf_ref, 0, rows_ref, start + k, zsem)))
            return out

        for pred, cp in pad_fills():
            pl.when(pred)(cp.start)
        for pred, cp in pad_fills():
            pl.when(pred)(cp.wait)

        def tail(b):
            return pltpu.make_async_copy(
                zbuf_ref, rows_ref.at[pl.ds(pl.multiple_of(b * half, half), half)], zsem)

        first_free = used_rows_ref[0] // half
        n_half = rows_ref.shape[0] // half
        lax.fori_loop(first_free, n_half, lambda b, c: (tail(b).start(), c)[1], 0)
        lax.fori_loop(first_free, n_half, lambda b, c: (tail(b).wait(), c)[1], 0)


def _dispatch(pad_start, pad_len, used_rows, dest3, h, n_rows):
    t, d = h.shape
    nt, _, tm2 = dest3.shape
    tm = tm2 // 2
    grid_spec = pltpu.PrefetchScalarGridSpec(
        num_scalar_prefetch=3,
        grid=(nt,),
        in_specs=[pl.BlockSpec((1, 1, tm2), lambda i, *_: (i, 0, 0), memory_space=pltpu.SMEM),
                  pl.BlockSpec(memory_space=pl.ANY)],
        out_specs=pl.BlockSpec(memory_space=pl.ANY),
        scratch_shapes=[pltpu.VMEM((2, tm, d), F32), pltpu.VMEM((MOE_BLOCK // 2, d), F32),
                        pltpu.SemaphoreType.DMA((2,)), pltpu.SemaphoreType.DMA((2,)),
                        pltpu.SemaphoreType.DMA],
    )
    return pl.pallas_call(
        _dispatch_body,
        grid_spec=grid_spec,
        out_shape=jax.ShapeDtypeStruct((n_rows, d), F32),
        compiler_params=_cparams(1),
        name="moe_dispatch",
    )(pad_start, pad_len, used_rows, dest3, h)


def _expert_body(first_ref, nblk_ref, nused_ref, x_ref, w1_ref, w3_ref, w2_ref, y_ref,
                 xbuf_ref, ybuf_ref, w1b_ref, w3b_ref, w2b_ref, ypend_ref, ycnt_ref, xcnt_ref, xpre_ref,
                 xsem, ysem):
    e = pl.program_id(0)
    row0 = first_ref[e] * MOE_BLOCK
    nblk = nblk_ref[e]
    big = xbuf_ref.shape[1]
    per_big = big // MOE_BLOCK

    @pl.when(e == 0)
    def _():
        ypend_ref[0] = 0
        ypend_ref[1] = 0
        ycnt_ref[0] = 0
        xcnt_ref[0] = 0
        xpre_ref[0] = 0

    def drain(slot):
        for size in (big, MOE_BLOCK):
            @pl.when(ypend_ref[slot] == size)
            def _(size=size):
                pltpu.make_async_copy(ybuf_ref.at[slot, pl.ds(0, size)], y_ref.at[pl.ds(0, size)],
                                      ysem.at[slot]).wait()
        ypend_ref[slot] = 0

    def cast_weights():
        w1b_ref[...] = w1_ref[0].astype(BF16)
        w3b_ref[...] = w3_ref[0].astype(BF16)
        w2b_ref[...] = w2_ref[0].astype(BF16)

    def x_copy(row, size, slot):
        src = x_ref.at[pl.ds(pl.multiple_of(row, MOE_BLOCK), size)]
        return pltpu.make_async_copy(src, xbuf_ref.at[slot, pl.ds(0, size)], xsem.at[slot])

    def start_x(row, is_big, slot):
        pl.when(is_big)(lambda: x_copy(row, big, slot).start(priority=1))
        pl.when(jnp.logical_not(is_big))(lambda: x_copy(row, MOE_BLOCK, slot).start(priority=1))

    nbig = nblk // per_big
    ntail = nblk - nbig * per_big
    tail_row = row0 + nbig * big
    e_next = jnp.minimum(e + 1, pl.num_programs(0) - 1)
    next_ok = (e + 1 < pl.num_programs(0)) & (nblk_ref[e_next] > 0)

    def prefetch_next_expert(slot):
        @pl.when(next_ok)
        def _():
            start_x(first_ref[e_next] * MOE_BLOCK, nblk_ref[e_next] >= per_big, slot)
            xpre_ref[0] = 1

    def stream(start_row, count, size, casts_first, start_following):
        def y_copy(j, s):
            dst = y_ref.at[pl.ds(pl.multiple_of(start_row + j * size, MOE_BLOCK), size)]
            return pltpu.make_async_copy(ybuf_ref.at[s, pl.ds(0, size)], dst, ysem.at[s])

        @pl.when(count > 0)
        def _():
            pl.when(casts_first)(cast_weights)

            def chunk(j, carry):
                s = xcnt_ref[0] % 2
                xcnt_ref[0] = xcnt_ref[0] + 1

                @pl.when(j + 1 < count)
                def _():
                    x_copy(start_row + (j + 1) * size, size, 1 - s).start(priority=1)

                @pl.when(j + 1 == count)
                def _():
                    start_following(1 - s)

                x_copy(start_row + j * size, size, s).wait()
                xb = xbuf_ref[s, pl.ds(0, size), :].astype(BF16)
                hid = _silu(_d(xb, w1b_ref[...])) * _d(xb, w3b_ref[...])
                ys = ycnt_ref[0] % 2
                drain(ys)
                ybuf_ref[ys, pl.ds(0, size), :] = _d(hid.astype(BF16), w2b_ref[...])
                y_copy(j, ys).start()
                ypend_ref[ys] = size
                ycnt_ref[0] = ycnt_ref[0] + 1
                return carry

            lax.fori_loop(0, count, chunk, 0)

    @pl.when(nblk > 0)
    def _():
        @pl.when(xpre_ref[0] == 0)
        def _():
            start_x(row0, nbig > 0, xcnt_ref[0] % 2)

        xpre_ref[0] = 0

        def after_big(slot):
            pl.when(ntail > 0)(lambda: x_copy(tail_row, MOE_BLOCK, slot).start(priority=1))
            pl.when(ntail == 0)(lambda: prefetch_next_expert(slot))

        stream(row0, nbig, big, nbig > 0, after_big)
        stream(tail_row, ntail, MOE_BLOCK, nbig == 0, prefetch_next_expert)

    @pl.when(e == pl.num_programs(0) - 1)
    def _():
        drain(0)
        drain(1)
        ybuf_ref[0, pl.ds(0, MOE_BLOCK), :] = jnp.zeros((MOE_BLOCK, ybuf_ref.shape[2]), F32)
        nused = nused_ref[0]
        ntot = y_ref.shape[0] // MOE_BLOCK

        def fill(b, carry):
            cp = pltpu.make_async_copy(ybuf_ref.at[0, pl.ds(0, MOE_BLOCK)],
                                       y_ref.at[pl.ds(b * MOE_BLOCK, MOE_BLOCK)], ysem.at[0])
            cp.start()
            cp.wait()
            return carry

        lax.fori_loop(nused, ntot, fill, 0)


def _experts(first_blk, nblk, nused, x_rows, w1, w3, w2, layer):
    r, d = x_rows.shape
    hid = w1.shape[-1]
    grid_spec = pltpu.PrefetchScalarGridSpec(
        num_scalar_prefetch=3,
        grid=(N_EXPERTS,),
        in_specs=[pl.BlockSpec(memory_space=pl.ANY),
                  pl.BlockSpec((None, 1, d, hid), lambda e, *_: (layer, e, 0, 0)),
                  pl.BlockSpec((None, 1, d, hid), lambda e, *_: (layer, e, 0, 0)),
                  pl.BlockSpec((None, 1, hid, d), lambda e, *_: (layer, e, 0, 0))],
        out_specs=pl.BlockSpec(memory_space=pl.ANY),
        scratch_shapes=[pltpu.VMEM((2, EXPERT_CHUNK, d), F32), pltpu.VMEM((2, EXPERT_CHUNK, d), F32),
                        pltpu.VMEM((d, hid), BF16), pltpu.VMEM((d, hid), BF16), pltpu.VMEM((hid, d), BF16),
                        pltpu.SMEM((2,), jnp.int32), pltpu.SMEM((1,), jnp.int32),
                        pltpu.SMEM((1,), jnp.int32), pltpu.SMEM((1,), jnp.int32),
                        pltpu.SemaphoreType.DMA((2,)), pltpu.SemaphoreType.DMA((2,))],
    )
    return pl.pallas_call(
        _expert_body,
        grid_spec=grid_spec,
        out_shape=jax.ShapeDtypeStruct((r, d), F32),
        compiler_params=_cparams(1),
        name="moe_experts",
    )(first_blk, nblk, nused, x_rows, w1, w3, w2)


def _combine_body(dest_ref, dest_next_ref, gate_ref, h_ref, ln_ref, rows_ref, o_ref, buf_ref, sem):
    i = pl.program_id(0)
    last = pl.num_programs(0) - 1
    tm = h_ref.shape[0]

    def gather(dref, slot):
        return [_row_copy(rows_ref, dref[0, 0, 2 * r + s], buf_ref.at[slot, s], r, sem.at[slot])
                for r in range(tm) for s in range(2)]

    @pl.when(i == 0)
    def _():
        _start_all(gather(dest_ref, 0))

    _start_all(gather(dest_next_ref, (i + 1) % 2))
    slot = i % 2
    for cp in gather(dest_ref, slot):
        cp.wait()
    gate = gate_ref[...]
    y = gate[:, 4:5] * buf_ref[slot, 0] + gate[:, 5:6] * buf_ref[slot, 1]
    o_ref[...] = _layer_norm(DN_ALPHA * h_ref[...] + y, ln_ref[0:1, :], ln_ref[1:2, :])

    @pl.when(i == last)
    def _():
        for cp in gather(dest_next_ref, (i + 1) % 2):
            cp.wait()


def _combine(dest3, route, h, ln, y_rows, tm):
    t, d = h.shape
    nt = t // tm
    return pl.pallas_call(
        _combine_body,
        grid=(nt,),
        in_specs=[pl.BlockSpec((1, 1, 2 * tm), lambda i: (i, 0, 0), memory_space=pltpu.SMEM),
                  pl.BlockSpec((1, 1, 2 * tm), lambda i: (jnp.minimum(i + 1, nt - 1), 0, 0),
                               memory_space=pltpu.SMEM),
                  pl.BlockSpec((tm, 8), lambda i: (i, 0)),
                  pl.BlockSpec((tm, d), lambda i: (i, 0)),
                  pl.BlockSpec((8, d), lambda i: (0, 0)),
                  pl.BlockSpec(memory_space=pl.ANY)],
        out_specs=pl.BlockSpec((tm, d), lambda i: (i, 0)),
        out_shape=jax.ShapeDtypeStruct((t, d), F32),
        scratch_shapes=[pltpu.VMEM((2, 2, tm, d), F32), pltpu.SemaphoreType.DMA((2,))],
        compiler_params=_cparams(1),
        name="moe_combine_ln",
    )(dest3, dest3, route, h, ln, y_rows)


def _moe(h, route, counts, w1, w3, w2, layer, ln):
    t, d = h.shape
    tm = min(COMBINE_TILE, t)
    a = 2 * t
    expert = route[:, 0:2].astype(jnp.int32)
    rank = route[:, 2:4].astype(jnp.int32)
    cnt = counts[0, :N_EXPERTS].astype(jnp.int32)
    padded = (cnt + MOE_BLOCK - 1) // MOE_BLOCK * MOE_BLOCK
    pend = jnp.cumsum(padded)
    pstart = pend - padded
    sel = expert[..., None] == jnp.arange(N_EXPERTS, dtype=jnp.int32)
    dest = jnp.sum(jnp.where(sel, pstart, 0), axis=-1) + rank
    n_blocks = -(-a // MOE_BLOCK) + N_EXPERTS
    nused = (pend[-1:] // MOE_BLOCK).astype(jnp.int32)
    dest3 = dest.reshape(t // tm, 1, 2 * tm)
    x_rows = _dispatch((pstart + cnt).astype(jnp.int32), (padded - cnt).astype(jnp.int32),
                       pend[-1:].astype(jnp.int32), dest3, h, n_blocks * MOE_BLOCK)
    y_rows = _experts((pstart // MOE_BLOCK).astype(jnp.int32), (padded // MOE_BLOCK).astype(jnp.int32),
                      nused, x_rows, w1, w3, w2, layer)
    return _combine(dest3, route, h, ln, y_rows, tm)


def _pad_rows(x, rows):
    return jnp.zeros((rows,) + x.shape[1:], x.dtype).at[:x.shape[0]].set(x)


def _route_weights(wg, bg, we, be):
    d = wg.shape[0]
    wr = jnp.zeros((d, LANES), F32)
    wr = wr.at[:, :MOE_GROUPS].set(wg)
    wr = wr.at[:, MOE_GROUPS:MOE_GROUPS + N_EXPERTS].set(jnp.transpose(we, (1, 0, 2)).reshape(d, N_EXPERTS))
    br = jnp.zeros((8, LANES), F32)
    br = br.at[0, :MOE_GROUPS].set(bg)
    br = br.at[0, MOE_GROUPS:MOE_GROUPS + N_EXPERTS].set(be.reshape(N_EXPERTS))
    return wr, br


def kernel(x, ab_w_in, rw_mu, rw_w0, rw_w2, rw_a0, rw_a2, rw_g2, rw_k_k, rw_k_a, rw_r_k, rw_gn_g, rw_gn_b, gla_gk_w2, gla_gk_b, gla_norm_g, ab_w_out, cd_w_in, s5_a_re, s5_a_im, s5_log_dt, s5_b_re, s5_b_im, s5_c_re, s5_c_im, s5_d, s5_glu_w, s5_glu_b, hg_lb, hg_norm_g, cd_w_out, ln1_g, ln1_b, moe_wg, moe_bg, moe_we, moe_be, moe_w1, moe_w3, moe_w2, ln2_g, ln2_b):
    batch, seq, d = x.shape
    t = batch * seq
    assert d == D_MODEL and seq % CHUNK == 0
    rows_per_seq = seq // S5_CHUNK
    assert rows_per_seq & (rows_per_seq - 1) == 0, "S5 chunk scan assumes a power-of-two chunk count"
    tm = min(ROW_TILE, t)
    assert t % tm == 0
    xt = x.reshape(t, d)
    ln1 = [_pad_rows(jnp.stack([ln1_g[l], ln1_b[l]]), 8) for l in range(DEPTH)]
    ln2 = [_pad_rows(jnp.stack([ln2_g[l], ln2_b[l]]), 8) for l in range(DEPTH)]

    j = 0
    w = ab_w_in[j]
    r_, wl_, k_, v_, al_, gl_ = 0, 512, 576, 1088, 1600, 1664
    gq, gk, gv, glow, ggate = 1792, 2048, 2304, 2816, 2832
    w0cols = jnp.concatenate([
        w[:, r_:r_ + 512], w[:, k_:k_ + 512], w[:, v_:v_ + 512], w[:, gv:gv + 512], w[:, ggate:ggate + 512],
        w[:, wl_:wl_ + 64], w[:, al_:al_ + 64], w[:, gl_:gl_ + 128], w[:, gq:gq + 256], w[:, gk:gk + 256],
        w[:, glow:glow + 16], jnp.zeros((d, 112), F32)], axis=1).astype(BF16)
    p0 = _matmul(xt, w0cols, tm)

    mu = rw_mu[j]
    vec = _pad_rows(jnp.stack([mu[r_:r_ + 512], mu[k_:k_ + 512], mu[v_:v_ + 512], rw_w0[j], rw_a0[j],
                               rw_k_k[j], rw_k_a[j], rw_r_k[j].reshape(-1), rw_gn_g[j], rw_gn_b[j]]), 16)
    mulo = _pad_rows(jnp.concatenate([mu[wl_:wl_ + 64], mu[al_:al_ + 64], mu[gl_:gl_ + 128]])[None], 8)
    w2p = _pad_rows(rw_w2[j], 128)
    a2p = jnp.zeros((128, 512), F32).at[64:].set(rw_a2[j])
    p0 = p0.reshape(batch, seq, -1)
    y_rw = _rwkv(p0, vec, mulo, w2p, a2p, rw_g2[j]).reshape(t, RW_WIDTH)
    y_gla = _gla(p0, _pad_rows(gla_gk_w2[j], 128), _pad_rows(gla_gk_b[j][None], 8),
                 _pad_rows(gla_norm_g[j][None], 8)).reshape(t, 512)

    wr, br = _route_weights(moe_wg[0], moe_bg[0], moe_we[0], moe_be[0])
    h, route, counts = _proj_route(y_rw, y_gla, xt, ab_w_out[j].astype(BF16), ln1[0], wr, br, tm)
    h = _moe(h, route, counts, moe_w1, moe_w3, moe_w2, 0, ln2[0])

    p1 = _matmul(h, cd_w_in[j].astype(BF16), tm)
    lb_sm = jax.nn.softmax(hg_lb.astype(F32), axis=0)
    lower = (jnp.cumsum(lb_sm, axis=0) - lb_sm[0])[1]
    y_hg = _hgrn(p1.reshape(batch, seq, -1), _pad_rows(lower[None], 8),
                 _pad_rows(hg_norm_g[j][None], 8)).reshape(t, 512)

    toep, bmat, emat, pw = _s5_tables(s5_a_re[j], s5_a_im[j], s5_log_dt[j], s5_b_re[j], s5_b_im[j],
                                      s5_c_re[j], s5_c_im[j], rows_per_seq)
    y_ssm = _s5_scan(p1.reshape(batch, seq, -1), toep, bmat, emat, pw).reshape(t, 512)
    y_s5 = _s5_post(y_ssm, p1, _pad_rows(jnp.stack([s5_d[j], s5_glu_b[j]]), 8),
                    s5_glu_w[j].astype(BF16), tm)

    wr, br = _route_weights(moe_wg[1], moe_bg[1], moe_we[1], moe_be[1])
    h2, route, counts = _proj_route(y_s5, y_hg, h, cd_w_out[j].astype(BF16), ln1[1], wr, br, tm)
    out = _moe(h2, route, counts, moe_w1, moe_w3, moe_w2, 1, ln2[1])
    return out.reshape(batch, seq, d)
```
